```python
import jax, jax.numpy as jnp
from jax import lax
import numpy as np

D_MODEL = 1024
BATCH = 8
SEQ = 4096
DEPTH = 4

N_MIXERS = 3
SB_HEADS = 16
SB_HEAD_DIM = D_MODEL // SB_HEADS
SB_Q_BLOCK = 128
CONV_WIDTH = 3
GLA_HEADS = 4
GLA_DK = D_MODEL // 2
GLA_DV = D_MODEL
GLA_DK_HEAD = GLA_DK // GLA_HEADS
GLA_DV_HEAD = GLA_DV // GLA_HEADS
GLA_GATE_RANK = 16
GLA_GATE_NORMALIZER = 16.0
GLA_CHUNK = 64
D_FF = 4 * D_MODEL
RMS_EPS = 1e-6

kernel_name = "hybrid_sb_conv_gla_sqrelu_sandwich"


def rms_norm(x, gain):
    xf = x.astype(jnp.float32)
    y = xf * lax.rsqrt(jnp.mean(xf * xf, axis=-1, keepdims=True) + RMS_EPS) * gain.astype(jnp.float32)
    return y.astype(x.dtype)


def stick_breaking_mixer(xn, w_qkv, w_o):
    b, s, _ = xn.shape
    qkv = (xn @ w_qkv).reshape(b, s, 3, SB_HEADS, SB_HEAD_DIM)
    q, k, v = [qkv[:, :, i].transpose(0, 2, 1, 3).astype(jnp.float32) for i in range(3)]
    scale = SB_HEAD_DIM ** -0.5
    n_blocks = s // SB_Q_BLOCK
    q_blocks = q.reshape(b, SB_HEADS, n_blocks, SB_Q_BLOCK, SB_HEAD_DIM).transpose(2, 0, 1, 3, 4)
    key_pos = jnp.arange(s)

    def one_block(args):
        qb, start = args
        z = jnp.einsum('bhqd,bhkd->bhqk', qb, k) * scale
        q_pos = start + jnp.arange(SB_Q_BLOCK)
        mask = key_pos[None, :] < q_pos[:, None]
        log_beta = jax.nn.log_sigmoid(z)
        log_1m_beta = jnp.where(mask, jax.nn.log_sigmoid(-z), 0.0)
        suffix = lax.cumsum(log_1m_beta, axis=3, reverse=True) - log_1m_beta
        weights = jnp.where(mask, jnp.exp(log_beta + suffix), 0.0)
        return jnp.einsum('bhqk,bhkd->bhqd', weights, v)

    o = lax.map(one_block, (q_blocks, jnp.arange(n_blocks) * SB_Q_BLOCK))
    o = o.transpose(1, 0, 3, 2, 4).reshape(b, s, D_MODEL).astype(xn.dtype)
    return o @ w_o


def short_conv_mixer(xn, w_in, conv_w, w_out):
    s = xn.shape[1]
    bcu = xn @ w_in
    b_gate, c_gate, u = jnp.split(bcu, 3, axis=-1)
    h = c_gate * u
    hp = jnp.pad(h, ((0, 0), (CONV_WIDTH - 1, 0), (0, 0)))
    conv = sum(conv_w[i] * hp[:, i:i + s] for i in range(CONV_WIDTH))
    return (b_gate * conv) @ w_out


def gla_mixer(xn, w_in, w_gate_up, b_gate, head_norm, w_o):
    b, s, _ = xn.shape
    n_chunks = s // GLA_CHUNK
    proj = xn @ w_in
    q, k, v, g, a_low = jnp.split(
        proj, np.cumsum([GLA_DK, GLA_DK, GLA_DV, GLA_DV]).tolist(), axis=-1)
    log_gate = jax.nn.log_sigmoid(
        (a_low @ w_gate_up + b_gate).astype(jnp.float32)) / GLA_GATE_NORMALIZER

    def to_chunks(t, dh):
        t = t.astype(jnp.float32).reshape(b, n_chunks, GLA_CHUNK, GLA_HEADS, dh)
        return t.transpose(1, 0, 3, 2, 4)

    qc = to_chunks(q, GLA_DK_HEAD) * (GLA_DK_HEAD ** -0.5)
    kc = to_chunks(k, GLA_DK_HEAD)
    vc = to_chunks(v, GLA_DV_HEAD)
    gc = to_chunks(log_gate, GLA_DK_HEAD)
    causal = jnp.tril(jnp.ones((GLA_CHUNK, GLA_CHUNK), dtype=bool))

    def step(state, inp):
        qi, ki, vi, gi = inp
        cum = jnp.cumsum(gi, axis=2)
        inter = jnp.einsum('bhcd,bhde->bhce', qi * jnp.exp(cum), state)
        diff = cum[:, :, :, None, :] - cum[:, :, None, :, :]
        decay = jnp.exp(jnp.where(causal[None, None, :, :, None], diff, -jnp.inf))
        scores = jnp.einsum('bhid,bhjd,bhijd->bhij', qi, ki, decay)
        out = inter + jnp.einsum('bhij,bhje->bhie', scores, vi)
        last = cum[:, :, -1:, :]
        new_state = jnp.exp(last)[:, :, 0, :, None] * state + jnp.einsum(
            'bhcd,bhce->bhde', ki * jnp.exp(last - cum), vi)
        return new_state, out

    state0 = jnp.zeros((b, GLA_HEADS, GLA_DK_HEAD, GLA_DV_HEAD), jnp.float32)
    _, o = lax.scan(step, state0, (qc, kc, vc, gc))
    o = o.transpose(1, 0, 3, 2, 4).reshape(b, s, GLA_HEADS, GLA_DV_HEAD)
    o = rms_norm(o, head_norm).reshape(b, s, GLA_DV).astype(xn.dtype)
    return (o * jax.nn.silu(g)) @ w_o


def squared_relu_mlp(xn, w_up, w_down):
    return jnp.square(jax.nn.relu(xn @ w_up)) @ w_down


def _fwd_setup_inputs(seed: int = 0) -> dict:
    key = jax.random.key(seed)
    ks = jax.random.split(key, 16)
    n_sb = (DEPTH + 2) // 3
    n_conv = (DEPTH + 1) // 3
    n_gla = DEPTH // 3
    f32 = jnp.float32

    def nrm(k, shape, scale):
        return jax.random.normal(k, shape, f32) * scale

    d_in_gla = 2 * GLA_DK + 2 * GLA_DV + GLA_GATE_RANK
    return {
        "x": nrm(ks[0], (BATCH, SEQ, D_MODEL), 1.0),
        "norm_gains": 1.0 + nrm(ks[1], (DEPTH, 4, D_MODEL), 0.02),
        "sb_w_qkv": nrm(ks[2], (n_sb, D_MODEL, 3 * D_MODEL), D_MODEL ** -0.5),
        "sb_w_o": nrm(ks[3], (n_sb, D_MODEL, D_MODEL), D_MODEL ** -0.5),
        "conv_w_in": nrm(ks[4], (n_conv, D_MODEL, 3 * D_MODEL), D_MODEL ** -0.5),
        "conv_w": nrm(ks[5], (n_conv, CONV_WIDTH, D_MODEL), CONV_WIDTH ** -0.5),
        "conv_w_out": nrm(ks[6], (n_conv, D_MODEL, D_MODEL), D_MODEL ** -0.5),
        "gla_w_in": nrm(ks[7], (n_gla, D_MODEL, d_in_gla), D_MODEL ** -0.5),
        "gla_w_gate_up": nrm(ks[8], (n_gla, GLA_GATE_RANK, GLA_DK), GLA_GATE_RANK ** -0.5),
        "gla_b_gate": nrm(ks[9], (n_gla, GLA_DK), 0.1),
        "gla_head_norm": 1.0 + nrm(ks[10], (n_gla, GLA_HEADS, GLA_DV_HEAD), 0.02),
        "gla_w_o": nrm(ks[11], (n_gla, GLA_DV, D_MODEL), GLA_DV ** -0.5),
        "ffn_w_up": nrm(ks[12], (DEPTH, D_MODEL, D_FF), D_MODEL ** -0.5),
        "ffn_w_down": nrm(ks[13], (DEPTH, D_FF, D_MODEL), D_FF ** -0.5),
    }


def _fwd_reference(x, norm_gains, sb_w_qkv, sb_w_o, conv_w_in, conv_w, conv_w_out,
              gla_w_in, gla_w_gate_up, gla_b_gate, gla_head_norm, gla_w_o,
              ffn_w_up, ffn_w_down):
    h = x
    for i in range(DEPTH):
        kind, j = i % N_MIXERS, i // N_MIXERS
        xn = rms_norm(h, norm_gains[i, 0])
        if kind == 0:
            m = stick_breaking_mixer(xn, sb_w_qkv[j], sb_w_o[j])
        elif kind == 1:
            m = short_conv_mixer(xn, conv_w_in[j], conv_w[j], conv_w_out[j])
        else:
            m = gla_mixer(xn, gla_w_in[j], gla_w_gate_up[j], gla_b_gate[j],
                          gla_head_norm[j], gla_w_o[j])
        h = h + rms_norm(m, norm_gains[i, 1])
        f = squared_relu_mlp(rms_norm(h, norm_gains[i, 2]), ffn_w_up[i], ffn_w_down[i])
        h = h + rms_norm(f, norm_gains[i, 3])
    return h


import jax as _jax
import jax.numpy as _jnp

TWIN_FORMAT = 'train_step'
FWD_PARAMS = ['x', 'norm_gains', 'sb_w_qkv', 'sb_w_o', 'conv_w_in', 'conv_w', 'conv_w_out', 'gla_w_in', 'gla_w_gate_up', 'gla_b_gate', 'gla_head_norm', 'gla_w_o', 'ffn_w_up', 'ffn_w_down']
TWIN_WEIGHTS = ['norm_gains', 'sb_w_qkv', 'sb_w_o', 'conv_w_in', 'conv_w', 'conv_w_out', 'gla_w_in', 'gla_w_gate_up', 'gla_b_gate', 'gla_head_norm', 'gla_w_o', 'ffn_w_up', 'ffn_w_down']
TWIN_DIFF_INPUT = 'x'
TWIN_INPUTS = ['x', 'norm_gains', 'sb_w_qkv', 'sb_w_o', 'conv_w_in', 'conv_w', 'conv_w_out', 'gla_w_in', 'gla_w_gate_up', 'gla_b_gate', 'gla_head_norm', 'gla_w_o', 'ffn_w_up', 'ffn_w_down', 'loss_target', 'm_norm_gains', 'm_sb_w_qkv', 'm_sb_w_o', 'm_conv_w_in', 'm_conv_w', 'm_conv_w_out', 'm_gla_w_in', 'm_gla_w_gate_up', 'm_gla_b_gate', 'm_gla_head_norm', 'm_gla_w_o', 'm_ffn_w_up', 'm_ffn_w_down', 'v_norm_gains', 'v_sb_w_qkv', 'v_sb_w_o', 'v_conv_w_in', 'v_conv_w', 'v_conv_w_out', 'v_gla_w_in', 'v_gla_w_gate_up', 'v_gla_b_gate', 'v_gla_head_norm', 'v_gla_w_o', 'v_ffn_w_up', 'v_ffn_w_down']
TWIN_OUTPUTS = ['loss', 'grad_x', 'grad_norm_gains', 'grad_sb_w_qkv', 'grad_sb_w_o', 'grad_conv_w_in', 'grad_conv_w', 'grad_conv_w_out', 'grad_gla_w_in', 'grad_gla_w_gate_up', 'grad_gla_b_gate', 'grad_gla_head_norm', 'grad_gla_w_o', 'grad_ffn_w_up', 'grad_ffn_w_down', 'delta_norm_gains', 'delta_sb_w_qkv', 'delta_sb_w_o', 'delta_conv_w_in', 'delta_conv_w', 'delta_conv_w_out', 'delta_gla_w_in', 'delta_gla_w_gate_up', 'delta_gla_b_gate', 'delta_gla_head_norm', 'delta_gla_w_o', 'delta_ffn_w_up', 'delta_ffn_w_down', 'new_m_norm_gains', 'new_m_sb_w_qkv', 'new_m_sb_w_o', 'new_m_conv_w_in', 'new_m_conv_w', 'new_m_conv_w_out', 'new_m_gla_w_in', 'new_m_gla_w_gate_up', 'new_m_gla_b_gate', 'new_m_gla_head_norm', 'new_m_gla_w_o', 'new_m_ffn_w_up', 'new_m_ffn_w_down', 'new_v_norm_gains', 'new_v_sb_w_qkv', 'new_v_sb_w_o', 'new_v_conv_w_in', 'new_v_conv_w', 'new_v_conv_w_out', 'new_v_gla_w_in', 'new_v_gla_w_gate_up', 'new_v_gla_b_gate', 'new_v_gla_head_norm', 'new_v_gla_w_o', 'new_v_ffn_w_up', 'new_v_ffn_w_down']
TWIN_LEAF_KINDS = {'loss': 'loss', 'grad_x': 'grad_x', 'grad_norm_gains': 'grad_w', 'grad_sb_w_qkv': 'grad_w', 'grad_sb_w_o': 'grad_w', 'grad_conv_w_in': 'grad_w', 'grad_conv_w': 'grad_w', 'grad_conv_w_out': 'grad_w', 'grad_gla_w_in': 'grad_w', 'grad_gla_w_gate_up': 'grad_w', 'grad_gla_b_gate': 'grad_w', 'grad_gla_head_norm': 'grad_w', 'grad_gla_w_o': 'grad_w', 'grad_ffn_w_up': 'grad_w', 'grad_ffn_w_down': 'grad_w', 'delta_norm_gains': 'delta_w', 'delta_sb_w_qkv': 'delta_w', 'delta_sb_w_o': 'delta_w', 'delta_conv_w_in': 'delta_w', 'delta_conv_w': 'delta_w', 'delta_conv_w_out': 'delta_w', 'delta_gla_w_in': 'delta_w', 'delta_gla_w_gate_up': 'delta_w', 'delta_gla_b_gate': 'delta_w', 'delta_gla_head_norm': 'delta_w', 'delta_gla_w_o': 'delta_w', 'delta_ffn_w_up': 'delta_w', 'delta_ffn_w_down': 'delta_w', 'new_m_norm_gains': 'new_m', 'new_m_sb_w_qkv': 'new_m', 'new_m_sb_w_o': 'new_m', 'new_m_conv_w_in': 'new_m', 'new_m_conv_w': 'new_m', 'new_m_conv_w_out': 'new_m', 'new_m_gla_w_in': 'new_m', 'new_m_gla_w_gate_up': 'new_m', 'new_m_gla_b_gate': 'new_m', 'new_m_gla_head_norm': 'new_m', 'new_m_gla_w_o': 'new_m', 'new_m_ffn_w_up': 'new_m', 'new_m_ffn_w_down': 'new_m', 'new_v_norm_gains': 'new_v', 'new_v_sb_w_qkv': 'new_v', 'new_v_sb_w_o': 'new_v', 'new_v_conv_w_in': 'new_v', 'new_v_conv_w': 'new_v', 'new_v_conv_w_out': 'new_v', 'new_v_gla_w_in': 'new_v', 'new_v_gla_w_gate_up': 'new_v', 'new_v_gla_b_gate': 'new_v', 'new_v_gla_head_norm': 'new_v', 'new_v_gla_w_o': 'new_v', 'new_v_ffn_w_up': 'new_v', 'new_v_ffn_w_down': 'new_v'}


def _forward(args):
    return _fwd_reference(*[args[k] for k in FWD_PARAMS])


def _output_shape():
    out = _jax.eval_shape(lambda: _forward(_fwd_setup_inputs(0)))
    return out.shape, out.dtype

N_MICROBATCH = 1
ADAM_LR = 0.001
ADAM_B1 = 0.9
ADAM_B2 = 0.999
ADAM_EPS = 1e-08
ADAM_WD = 0.01
ADAM_STEP = 10
PER_EXAMPLE_BATCH_AXIS = {'x': 0, 'loss_target': 0}
SHARED_INPUTS = []
_WEIGHT_DTYPES = {'norm_gains': _jnp.float32, 'sb_w_qkv': _jnp.float32, 'sb_w_o': _jnp.float32, 'conv_w_in': _jnp.float32, 'conv_w': _jnp.float32, 'conv_w_out': _jnp.float32, 'gla_w_in': _jnp.float32, 'gla_w_gate_up': _jnp.float32, 'gla_b_gate': _jnp.float32, 'gla_head_norm': _jnp.float32, 'gla_w_o': _jnp.float32, 'ffn_w_up': _jnp.float32, 'ffn_w_down': _jnp.float32}
MOMENT_SCALE = {'norm_gains': 2.383331e+01, 'sb_w_qkv': 3.417920e+00, 'sb_w_o': 6.174629e+00, 'conv_w_in': 1.139702e+00, 'conv_w': 1.230147e+00, 'conv_w_out': 1.290567e+00, 'gla_w_in': 1.167243e+00, 'gla_w_gate_up': 3.249406e-01, 'gla_b_gate': 1.047745e+00, 'gla_head_norm': 1.232030e+00, 'gla_w_o': 1.188039e+00, 'ffn_w_up': 1.803087e+00, 'ffn_w_down': 1.290634e+01}


def _to_microbatches(a, axis):
    t = _jnp.moveaxis(a, axis, 0)
    t = t.reshape((N_MICROBATCH, t.shape[0] // N_MICROBATCH) + t.shape[1:])
    return _jnp.moveaxis(t, 1, axis + 1)


def setup_inputs(seed: int = 0) -> dict:
    inp = _fwd_setup_inputs(seed)
    key = _jax.random.fold_in(_jax.random.key(seed), 7919)
    shape, _ = _output_shape()
    out = dict(inp)
    out["loss_target"] = _jax.random.normal(_jax.random.fold_in(key, 0), shape, _jnp.float32)
    for i, name in enumerate(TWIN_WEIGHTS):
        w = inp[name].astype(_jnp.float32)
        if MOMENT_SCALE is None:
            s = _jnp.sqrt(_jnp.mean(_jnp.square(w)) + 1e-30)
        else:
            s = MOMENT_SCALE[name]
        km, kv = _jax.random.split(_jax.random.fold_in(key, i + 1))
        out[name] = w
        out["m_" + name] = s * _jax.random.normal(km, w.shape, _jnp.float32)
        out["v_" + name] = (s * s) * _jax.random.uniform(kv, w.shape, _jnp.float32, 0.5, 1.5)
    if N_MICROBATCH > 1:
        for name, axis in PER_EXAMPLE_BATCH_AXIS.items():
            out[name] = _to_microbatches(out[name], axis)
    return {'x': out['x'], 'norm_gains': out['norm_gains'], 'sb_w_qkv': out['sb_w_qkv'], 'sb_w_o': out['sb_w_o'], 'conv_w_in': out['conv_w_in'], 'conv_w': out['conv_w'], 'conv_w_out': out['conv_w_out'], 'gla_w_in': out['gla_w_in'], 'gla_w_gate_up': out['gla_w_gate_up'], 'gla_b_gate': out['gla_b_gate'], 'gla_head_norm': out['gla_head_norm'], 'gla_w_o': out['gla_w_o'], 'ffn_w_up': out['ffn_w_up'], 'ffn_w_down': out['ffn_w_down'], 'loss_target': out['loss_target'], 'm_norm_gains': out['m_norm_gains'], 'm_sb_w_qkv': out['m_sb_w_qkv'], 'm_sb_w_o': out['m_sb_w_o'], 'm_conv_w_in': out['m_conv_w_in'], 'm_conv_w': out['m_conv_w'], 'm_conv_w_out': out['m_conv_w_out'], 'm_gla_w_in': out['m_gla_w_in'], 'm_gla_w_gate_up': out['m_gla_w_gate_up'], 'm_gla_b_gate': out['m_gla_b_gate'], 'm_gla_head_norm': out['m_gla_head_norm'], 'm_gla_w_o': out['m_gla_w_o'], 'm_ffn_w_up': out['m_ffn_w_up'], 'm_ffn_w_down': out['m_ffn_w_down'], 'v_norm_gains': out['v_norm_gains'], 'v_sb_w_qkv': out['v_sb_w_qkv'], 'v_sb_w_o': out['v_sb_w_o'], 'v_conv_w_in': out['v_conv_w_in'], 'v_conv_w': out['v_conv_w'], 'v_conv_w_out': out['v_conv_w_out'], 'v_gla_w_in': out['v_gla_w_in'], 'v_gla_w_gate_up': out['v_gla_w_gate_up'], 'v_gla_b_gate': out['v_gla_b_gate'], 'v_gla_head_norm': out['v_gla_head_norm'], 'v_gla_w_o': out['v_gla_w_o'], 'v_ffn_w_up': out['v_ffn_w_up'], 'v_ffn_w_down': out['v_ffn_w_down']}


def _loss(weights, diff, rest, loss_target):
    with _jax.named_scope("forward"):
        args = {**rest, TWIN_DIFF_INPUT: diff, **{k: w.astype(_WEIGHT_DTYPES[k]) for k, w in weights.items()}}
        y = _forward(args)
    with _jax.named_scope("loss_head"):
        err = _jnp.square(y.astype(_jnp.float32) - loss_target)
        return 0.5 * _jnp.sum(_jnp.mean(err, axis=-1)) if err.ndim else 0.5 * err


def _adamw(w, g, m, v):
    m = ADAM_B1 * m + (1.0 - ADAM_B1) * g
    v = ADAM_B2 * v + (1.0 - ADAM_B2) * _jnp.square(g)
    m_hat = m / (1.0 - ADAM_B1 ** ADAM_STEP)
    v_hat = v / (1.0 - ADAM_B2 ** ADAM_STEP)
    delta = -ADAM_LR * (m_hat / (_jnp.sqrt(v_hat) + ADAM_EPS) + ADAM_WD * w)
    return delta, m, v


def reference(x, norm_gains, sb_w_qkv, sb_w_o, conv_w_in, conv_w, conv_w_out, gla_w_in, gla_w_gate_up, gla_b_gate, gla_head_norm, gla_w_o, ffn_w_up, ffn_w_down, loss_target, m_norm_gains, m_sb_w_qkv, m_sb_w_o, m_conv_w_in, m_conv_w, m_conv_w_out, m_gla_w_in, m_gla_w_gate_up, m_gla_b_gate, m_gla_head_norm, m_gla_w_o, m_ffn_w_up, m_ffn_w_down, v_norm_gains, v_sb_w_qkv, v_sb_w_o, v_conv_w_in, v_conv_w, v_conv_w_out, v_gla_w_in, v_gla_w_gate_up, v_gla_b_gate, v_gla_head_norm, v_gla_w_o, v_ffn_w_up, v_ffn_w_down):
    given = dict(x=x, norm_gains=norm_gains, sb_w_qkv=sb_w_qkv, sb_w_o=sb_w_o, conv_w_in=conv_w_in, conv_w=conv_w, conv_w_out=conv_w_out, gla_w_in=gla_w_in, gla_w_gate_up=gla_w_gate_up, gla_b_gate=gla_b_gate, gla_head_norm=gla_head_norm, gla_w_o=gla_w_o, ffn_w_up=ffn_w_up, ffn_w_down=ffn_w_down, loss_target=loss_target, m_norm_gains=m_norm_gains, m_sb_w_qkv=m_sb_w_qkv, m_sb_w_o=m_sb_w_o, m_conv_w_in=m_conv_w_in, m_conv_w=m_conv_w, m_conv_w_out=m_conv_w_out, m_gla_w_in=m_gla_w_in, m_gla_w_gate_up=m_gla_w_gate_up, m_gla_b_gate=m_gla_b_gate, m_gla_head_norm=m_gla_head_norm, m_gla_w_o=m_gla_w_o, m_ffn_w_up=m_ffn_w_up, m_ffn_w_down=m_ffn_w_down, v_norm_gains=v_norm_gains, v_sb_w_qkv=v_sb_w_qkv, v_sb_w_o=v_sb_w_o, v_conv_w_in=v_conv_w_in, v_conv_w=v_conv_w, v_conv_w_out=v_conv_w_out, v_gla_w_in=v_gla_w_in, v_gla_w_gate_up=v_gla_w_gate_up, v_gla_b_gate=v_gla_b_gate, v_gla_head_norm=v_gla_head_norm, v_gla_w_o=v_gla_w_o, v_ffn_w_up=v_ffn_w_up, v_ffn_w_down=v_ffn_w_down)
    weights = {n: given[n] for n in TWIN_WEIGHTS}
    shared = {n: given[n] for n in SHARED_INPUTS}
    per_example = {n: given[n] for n in ['x']}
    grad_fn = _jax.value_and_grad(_loss, argnums=(0, 1))

    def one_microbatch(ex, loss_target):
        ex = dict(ex)
        diff = ex.pop(TWIN_DIFF_INPUT)
        return grad_fn(weights, diff, {**shared, **ex}, loss_target)

    if N_MICROBATCH == 1:
        loss, (grad_w, grad_x) = one_microbatch(per_example, given["loss_target"])
    else:
        def body(carry, xs):
            loss_sum, grad_sum = carry
            l_k, (gw_k, gx_k) = one_microbatch(xs[0], xs[1])
            with _jax.named_scope("update"):
                return (loss_sum + l_k, _jax.tree.map(_jnp.add, grad_sum, gw_k)), gx_k

        init = (_jnp.zeros((), _jnp.float32), _jax.tree.map(_jnp.zeros_like, weights))
        (loss, grad_w), grad_x = _jax.lax.scan(body, init, (per_example, given["loss_target"]))
    with _jax.named_scope("update"):
        delta_w, new_m, new_v = {}, {}, {}
        for n in TWIN_WEIGHTS:
            delta_w[n], new_m[n], new_v[n] = _adamw(weights[n], grad_w[n], given["m_" + n], given["v_" + n])
    return (loss, grad_x, *[grad_w[n] for n in TWIN_WEIGHTS], *[delta_w[n] for n in TWIN_WEIGHTS],
            *[new_m[n] for n in TWIN_WEIGHTS], *[new_v[n] for n in TWIN_WEIGHTS])
```

```python
import functools
import math

import jax
import jax.numpy as jnp
from jax import lax
from jax.experimental import pallas as pl
from jax.experimental.pallas import tpu as pltpu

F32 = jnp.float32
BF16 = jnp.bfloat16

N_DEV = 8
SB_HEADS = 16
GLA_HEADS = 4
GLA_CHUNK = 64
GLA_GATE_RANK = 16
GLA_GATE_NORMALIZER = 16.0
CONV_WIDTH = 3
DEPTH = 4
RMS_EPS = 1e-6
ADAM_LR = 0.001
ADAM_B1 = 0.9
ADAM_B2 = 0.999
ADAM_EPS = 1e-08
ADAM_WD = 0.01
ADAM_STEP = 10

LANES = 128
SB_BLOCK = 128
VMEM_LIMIT_BYTES = 56 * 1024 * 1024
PACK_COLS = 1024
PACK_ROW_BLOCK = 128

WEIGHTS = ['norm_gains', 'sb_w_qkv', 'sb_w_o', 'conv_w_in', 'conv_w', 'conv_w_out', 'gla_w_in',
           'gla_w_gate_up', 'gla_b_gate', 'gla_head_norm', 'gla_w_o', 'ffn_w_up', 'ffn_w_down']
SHARD_AXIS = {'norm_gains': 2, 'sb_w_qkv': 2, 'sb_w_o': 1, 'conv_w_in': 2, 'conv_w': 2, 'conv_w_out': 1,
              'gla_w_in': 2, 'gla_w_gate_up': 2, 'gla_b_gate': 1, 'gla_head_norm': 2, 'gla_w_o': 1,
              'ffn_w_up': 2, 'ffn_w_down': 1}
F32_PAYLOAD = ('norm_gains', 'conv_w', 'gla_b_gate', 'gla_head_norm')

_NN = (((1,), (0,)), ((), ()))
_NT = (((1,), (1,)), ((), ()))
_TN = (((0,), (0,)), ((), ()))
_DIMS = {'nn': _NN, 'nt': _NT, 'tn': _TN}


def _params(*semantics):
    return pltpu.CompilerParams(dimension_semantics=semantics, vmem_limit_bytes=VMEM_LIMIT_BYTES)


def _dot(a, b, dims=_NN):
    return lax.dot_general(a.astype(BF16), b.astype(BF16), dims, preferred_element_type=F32)


def _split_hi_lo(x):
    hi = x.astype(BF16)
    lo = (x - hi.astype(F32)).astype(BF16)
    return hi, lo


def _dot_exact_rhs(x, ones_mat, dims=_NN):
    hi, lo = _split_hi_lo(x)
    return (lax.dot_general(hi, ones_mat, dims, preferred_element_type=F32)
            + lax.dot_general(lo, ones_mat, dims, preferred_element_type=F32))


def _dot_exact_lhs(ones_mat, x, dims=_NN):
    hi, lo = _split_hi_lo(x)
    return (lax.dot_general(ones_mat, hi, dims, preferred_element_type=F32)
            + lax.dot_general(ones_mat, lo, dims, preferred_element_type=F32))


def _log_sigmoid(z):
    return jnp.minimum(z, 0.0) - jnp.log(1.0 + jnp.exp(-jnp.abs(z)))


def _sigmoid(z):
    return 1.0 / (1.0 + jnp.exp(-z))


def matmul(a, b, *, mode, name, out_dtype=F32, a_pre=None, epi=None, extra=None, tm=512, tn=512, tk=512):
    if mode == 'nn':
        (m, k), (k2, n) = a.shape, b.shape
    elif mode == 'nt':
        (m, k), (n, k2) = a.shape, b.shape
    else:
        (k, m), (k2, n) = a.shape, b.shape
    assert k == k2, (a.shape, b.shape, mode)
    tm, tn, tk = min(tm, m), min(tn, n), min(tk, k)
    assert m % tm == 0 and n % tn == 0 and k % tk == 0, (a.shape, b.shape, mode)
    nk = k // tk
    if mode == 'tn':
        a_spec = pl.BlockSpec((tk, tm), lambda i, j, kk: (kk, i))
    else:
        a_spec = pl.BlockSpec((tm, tk), lambda i, j, kk: (i, kk))
    if mode == 'nt':
        b_spec = pl.BlockSpec((tn, tk), lambda i, j, kk: (j, kk))
    else:
        b_spec = pl.BlockSpec((tk, tn), lambda i, j, kk: (kk, j))
    in_specs, operands = [a_spec, b_spec], [a, b]
    if epi == 'logsig16':
        in_specs.append(pl.BlockSpec((1, tn), lambda i, j, kk: (0, j)))
        operands.append(extra)
    elif epi is not None:
        in_specs.append(pl.BlockSpec((tm, tn), lambda i, j, kk: (i, j)))
        operands.append(extra)

    def body(a_ref, b_ref, *rest):
        if epi is None:
            o_ref, acc_ref = rest
        else:
            e_ref, o_ref, acc_ref = rest
        kk = pl.program_id(2)

        @pl.when(kk == 0)
        def _():
            acc_ref[...] = jnp.zeros_like(acc_ref)

        av = a_ref[...]
        if a_pre == 'relu2':
            av = jnp.square(jnp.maximum(av.astype(F32), 0.0))
        acc_ref[...] += _dot(av, b_ref[...], _DIMS[mode])

        @pl.when(kk == nk - 1)
        def _():
            r = acc_ref[...]
            if epi == 'mul2relu':
                r = r * (2.0 * jnp.maximum(e_ref[...], 0.0))
            elif epi == 'add':
                r = r + e_ref[...]
            elif epi == 'logsig16':
                r = _log_sigmoid(r + e_ref[...]) / GLA_GATE_NORMALIZER
            o_ref[...] = r.astype(out_dtype)

    return pl.pallas_call(
        body, name=name, grid=(m // tm, n // tn, nk),
        in_specs=in_specs, out_specs=pl.BlockSpec((tm, tn), lambda i, j, kk: (i, j)),
        out_shape=jax.ShapeDtypeStruct((m, n), out_dtype),
        scratch_shapes=[pltpu.VMEM((tm, tn), F32)],
        compiler_params=_params("parallel", "parallel", "arbitrary"),
    )(*operands)


def rms_fwd(x, gain, *, name, res=None, out_dtype=F32, tm=512):
    t, d = x.shape
    tm = min(tm, t)
    row = pl.BlockSpec((tm, d), lambda i: (i, 0))
    in_specs, operands = [row, pl.BlockSpec((1, d), lambda i: (0, 0))], [x, gain]
    if res is not None:
        in_specs.append(row)
        operands.append(res)

    def body(x_ref, g_ref, *rest):
        xv = x_ref[...]
        r = lax.rsqrt(jnp.mean(xv * xv, axis=-1, keepdims=True) + RMS_EPS)
        y = xv * r * g_ref[...]
        if res is not None:
            y = rest[0][...] + y
        rest[-1][...] = y.astype(out_dtype)

    return pl.pallas_call(
        body, name=name, grid=(t // tm,), in_specs=in_specs, out_specs=row,
        out_shape=jax.ShapeDtypeStruct((t, d), out_dtype), compiler_params=_params("parallel"),
    )(*operands)


def rms_bwd(x, gain, dy, *, name, dres=None, tm=512):
    t, d = x.shape
    tm = min(tm, t)
    row = pl.BlockSpec((tm, d), lambda i: (i, 0))
    vec = pl.BlockSpec((1, d), lambda i: (0, 0))
    in_specs, operands = [row, vec, row], [x, gain, dy]
    if dres is not None:
        in_specs.append(row)
        operands.append(dres)

    def body(x_ref, g_ref, dy_ref, *rest):
        dx_ref, dg_ref = rest[-2], rest[-1]

        @pl.when(pl.program_id(0) == 0)
        def _():
            dg_ref[...] = jnp.zeros_like(dg_ref)

        xv, dyv = x_ref[...], dy_ref[...]
        r = lax.rsqrt(jnp.mean(xv * xv, axis=-1, keepdims=True) + RMS_EPS)
        u = dyv * g_ref[...]
        dx = r * u - xv * (r * r * r * jnp.mean(u * xv, axis=-1, keepdims=True))
        if dres is not None:
            dx = rest[0][...] + dx
        dx_ref[...] = dx
        dg_ref[...] += jnp.sum(dyv * xv * r, axis=0, keepdims=True)

    return pl.pallas_call(
        body, name=name, grid=(t // tm,), in_specs=in_specs, out_specs=[row, vec],
        out_shape=[jax.ShapeDtypeStruct((t, d), F32), jax.ShapeDtypeStruct((1, d), F32)],
        compiler_params=_params("arbitrary"),
    )(*operands)


def loss_head(y, target, *, name, tm=512):
    t, d = y.shape
    tm = min(tm, t)
    nt = t // tm
    row = pl.BlockSpec((tm, d), lambda i: (i, 0))

    def body(y_ref, t_ref, loss_ref, dy_ref, acc_ref):
        i = pl.program_id(0)

        @pl.when(i == 0)
        def _():
            acc_ref[...] = jnp.zeros_like(acc_ref)

        err = y_ref[...] - t_ref[...]
        dy_ref[...] = err * (1.0 / d)
        acc_ref[...] += jnp.sum(err * err, axis=0, keepdims=True)

        @pl.when(i == nt - 1)
        def _():
            loss_ref[...] = jnp.sum(acc_ref[...], axis=1, keepdims=True) * (0.5 / d)

    return pl.pallas_call(
        body, name=name, grid=(nt,), in_specs=[row, row],
        out_specs=[pl.BlockSpec((1, 1), lambda i: (0, 0)), row],
        out_shape=[jax.ShapeDtypeStruct((1, 1), F32), jax.ShapeDtypeStruct((t, d), F32)],
        scratch_shapes=[pltpu.VMEM((1, d), F32)], compiler_params=_params("arbitrary"),
    )(y, target)


def _sb_masks(qi, kb):
    rows = lax.broadcasted_iota(jnp.int32, (SB_BLOCK, SB_BLOCK), 0)
    cols = lax.broadcasted_iota(jnp.int32, (SB_BLOCK, SB_BLOCK), 1)
    return (kb * SB_BLOCK + cols) < (qi * SB_BLOCK + rows)


def _sb_tri(strict):
    r = lax.broadcasted_iota(jnp.int32, (SB_BLOCK, SB_BLOCK), 0)
    c = lax.broadcasted_iota(jnp.int32, (SB_BLOCK, SB_BLOCK), 1)
    return jnp.where((r > c) if strict else (r >= c), 1.0, 0.0).astype(BF16)


def _sb_logits(q_h, k_blk, mask, scale):
    z = _dot(q_h, k_blk, _NT) * scale
    ls = _log_sigmoid(z)
    lm = ls - z
    lmm = jnp.where(mask, lm, 0.0)
    return ls, lm, lmm, jnp.sum(lmm, axis=1, keepdims=True)


def _sb_weights(ls, lmm, mask, tri_strict, later):
    suffix = _dot_exact_rhs(lmm, tri_strict)
    return jnp.where(mask, jnp.exp(ls + suffix + later), 0.0)


def sb_fwd(qkv, *, name):
    t, d3 = qkv.shape
    d = d3 // 3
    head_dim = d // SB_HEADS
    assert 2 * head_dim == LANES and t % SB_BLOCK == 0
    pairs = d // LANES
    nq = t // SB_BLOCK
    scale = head_dim ** -0.5

    def body(q_ref, k_ref, v_ref, o_ref, tot_ref):
        qi = pl.program_id(1)
        lane = lax.broadcasted_iota(jnp.int32, (SB_BLOCK, LANES), 1)
        first = lane < head_dim
        q = q_ref[...]
        qs = [jnp.where(first, q, jnp.zeros_like(q)), jnp.where(first, jnp.zeros_like(q), q)]
        tri = _sb_tri(True)

        def step(i, carry):
            kb = qi - i
            ks = pl.multiple_of(kb * SB_BLOCK, SB_BLOCK)
            k_blk = k_ref[pl.ds(ks, SB_BLOCK), :]
            v_blk = v_ref[pl.ds(ks, SB_BLOCK), :]
            mask = _sb_masks(qi, kb)
            new = []
            for h in range(2):
                acc, later = carry[2 * h], carry[2 * h + 1]
                ls, _, lmm, row = _sb_logits(qs[h], k_blk, mask, scale)
                w = _sb_weights(ls, lmm, mask, tri, later)
                new += [acc + _dot(w, v_blk), later + row]
            return tuple(new)

        zero = jnp.zeros((SB_BLOCK, LANES), F32)
        zcol = jnp.zeros((SB_BLOCK, 1), F32)
        out = lax.fori_loop(0, qi + 1, step, (zero, zcol, zero, zcol))
        o_ref[...] = jnp.where(first, out[0], out[2])
        tot_ref[...] = jnp.where(first, out[1], out[3])

    blk = lambda off: pl.BlockSpec((t, LANES), lambda p, i: (0, off + p))
    qblk = pl.BlockSpec((SB_BLOCK, LANES), lambda p, i: (i, p))
    shape = jax.ShapeDtypeStruct((t, d), F32)
    return pl.pallas_call(
        body, name=name, grid=(pairs, nq), in_specs=[qblk, blk(pairs), blk(2 * pairs)],
        out_specs=[qblk, qblk], out_shape=[shape, shape], compiler_params=_params("parallel", "arbitrary"),
    )(qkv, qkv, qkv)


def sb_bwd(qkv, tot, do, *, name):
    t, d3 = qkv.shape
    d = d3 // 3
    head_dim = d // SB_HEADS
    pairs = d // LANES
    nq = t // SB_BLOCK
    scale = head_dim ** -0.5

    def body(q_ref, k_ref, v_ref, tot_ref, do_ref, dq_ref, dk_ref, dv_ref):
        qi = pl.program_id(1)

        @pl.when(qi == 0)
        def _():
            dk_ref[...] = jnp.zeros_like(dk_ref)
            dv_ref[...] = jnp.zeros_like(dv_ref)

        lane = lax.broadcasted_iota(jnp.int32, (SB_BLOCK, LANES), 1)
        first = lane < head_dim
        q, dov, totv = q_ref[...], do_ref[...], tot_ref[...]
        sel = [first, jnp.logical_not(first)]
        qs = [jnp.where(s, q, jnp.zeros_like(q)) for s in sel]
        dos = [jnp.where(s, dov, 0.0) for s in sel]
        tots = [totv[:, 0:1], totv[:, head_dim:head_dim + 1]]
        tri_strict = _sb_tri(True)
        rows = lax.broadcasted_iota(jnp.int32, (SB_BLOCK, SB_BLOCK), 0)
        cols = lax.broadcasted_iota(jnp.int32, (SB_BLOCK, SB_BLOCK), 1)
        tri_before = jnp.where(rows < cols, 1.0, 0.0).astype(BF16)

        def step(kb, carry):
            ks = pl.multiple_of(kb * SB_BLOCK, SB_BLOCK)
            k_blk = k_ref[pl.ds(ks, SB_BLOCK), :]
            v_blk = v_ref[pl.ds(ks, SB_BLOCK), :]
            mask = _sb_masks(qi, kb)
            new = []
            dk_acc = jnp.zeros((SB_BLOCK, LANES), F32)
            dv_acc = jnp.zeros((SB_BLOCK, LANES), F32)
            for h in range(2):
                dq, seen, before = carry[3 * h], carry[3 * h + 1], carry[3 * h + 2]
                ls, lm, lmm, row = _sb_logits(qs[h], k_blk, mask, scale)
                seen = seen + row
                w = _sb_weights(ls, lmm, mask, tri_strict, tots[h] - seen)
                da = _dot(dos[h], v_blk, _NT) * w
                g = _dot_exact_rhs(da, tri_before) + before
                dz = jnp.where(mask, da * jnp.exp(lm) - g * jnp.exp(ls), 0.0) * scale
                dq = dq + _dot(dz, k_blk)
                dk_acc = dk_acc + _dot(dz, qs[h], _TN)
                dv_acc = dv_acc + _dot(w, dos[h], _TN)
                new += [dq, seen, before + jnp.sum(da, axis=1, keepdims=True)]
            dk_ref[pl.ds(ks, SB_BLOCK), :] += dk_acc
            dv_ref[pl.ds(ks, SB_BLOCK), :] += dv_acc
            return tuple(new)

        zero = jnp.zeros((SB_BLOCK, LANES), F32)
        zcol = jnp.zeros((SB_BLOCK, 1), F32)
        out = lax.fori_loop(0, qi + 1, step, (zero, zcol, zcol, zero, zcol, zcol))
        dq_ref[...] = jnp.where(first, out[0], out[3])

    qblk = pl.BlockSpec((SB_BLOCK, LANES), lambda p, i: (i, p))
    col = lambda off: pl.BlockSpec((t, LANES), lambda p, i: (0, off + p))
    shape = jax.ShapeDtypeStruct((t, d), F32)
    return pl.pallas_call(
        body, name=name, grid=(pairs, nq),
        in_specs=[qblk, col(pairs), col(2 * pairs), qblk, qblk],
        out_specs=[qblk, col(0), col(0)], out_shape=[shape, shape, shape],
        compiler_params=_params("parallel", "arbitrary"),
    )(qkv, qkv, qkv, tot, do)


def _shift_down(x, s):
    rows = lax.broadcasted_iota(jnp.int32, x.shape, 0)
    return jnp.where(rows >= s, pltpu.roll(x, s, 0), 0.0)


def _shift_up(x, s):
    t = x.shape[0]
    rows = lax.broadcasted_iota(jnp.int32, x.shape, 0)
    return jnp.where(rows < t - s, pltpu.roll(x, t - s, 0), 0.0)


def conv_fwd(bcu, w, *, name):
    t, d3 = bcu.shape
    d = d3 // 3
    nb = d // LANES
    col = lambda off: pl.BlockSpec((t, LANES), lambda j: (0, off + j))

    def body(b_ref, c_ref, u_ref, w_ref, y_ref):
        hh = c_ref[...] * u_ref[...]
        conv = w_ref[0:1, :] * _shift_down(hh, 2) + w_ref[1:2, :] * _shift_down(hh, 1) + w_ref[2:3, :] * hh
        y_ref[...] = b_ref[...] * conv

    return pl.pallas_call(
        body, name=name, grid=(nb,),
        in_specs=[col(0), col(nb), col(2 * nb), pl.BlockSpec((CONV_WIDTH, LANES), lambda j: (0, j))],
        out_specs=col(0), out_shape=jax.ShapeDtypeStruct((t, d), F32), compiler_params=_params("parallel"),
    )(bcu, bcu, bcu, w)


def conv_bwd(bcu, w, dy, *, name):
    t, d3 = bcu.shape
    d = d3 // 3
    nb = d // LANES
    col = lambda off: pl.BlockSpec((t, LANES), lambda j: (0, off + j))
    wspec = pl.BlockSpec((CONV_WIDTH, LANES), lambda j: (0, j))

    def body(b_ref, c_ref, u_ref, w_ref, dy_ref, db_ref, dc_ref, du_ref, dw_ref):
        c, u, dyv = c_ref[...], u_ref[...], dy_ref[...]
        hh = c * u
        h2, h1 = _shift_down(hh, 2), _shift_down(hh, 1)
        w0, w1, w2 = w_ref[0:1, :], w_ref[1:2, :], w_ref[2:3, :]
        db_ref[...] = dyv * (w0 * h2 + w1 * h1 + w2 * hh)
        dconv = dyv * b_ref[...]
        dhh = w2 * dconv + w1 * _shift_up(dconv, 1) + w0 * _shift_up(dconv, 2)
        dc_ref[...] = dhh * u
        du_ref[...] = dhh * c
        dw_ref[0:1, :] = jnp.sum(dconv * h2, axis=0, keepdims=True)
        dw_ref[1:2, :] = jnp.sum(dconv * h1, axis=0, keepdims=True)
        dw_ref[2:3, :] = jnp.sum(dconv * hh, axis=0, keepdims=True)

    shape = jax.ShapeDtypeStruct((t, d), F32)
    return pl.pallas_call(
        body, name=name, grid=(nb,),
        in_specs=[col(0), col(nb), col(2 * nb), wspec, col(0)],
        out_specs=[col(0), col(0), col(0), wspec],
        out_shape=[shape, shape, shape, jax.ShapeDtypeStruct((CONV_WIDTH, d), F32)],
        compiler_params=_params("parallel"),
    )(bcu, bcu, bcu, w, dy)


def _gla_chunk(q_ref, k_ref, lg_ref, scale):
    c = GLA_CHUNK
    rows = lax.broadcasted_iota(jnp.int32, (c, c), 0)
    cols = lax.broadcasted_iota(jnp.int32, (c, c), 1)
    causal = rows >= cols
    tril = jnp.where(causal, 1.0, 0.0).astype(BF16)
    q = q_ref[...] * scale
    k = k_ref[...]
    cum = _dot_exact_lhs(tril, lg_ref[...])
    last = cum[c - 1:c, :]
    eq = jnp.exp(cum)
    ek = jnp.exp(-cum)
    el = jnp.exp(last - cum)
    return causal, tril, q, k, cum, last, eq, ek, el


def gla_fwd(proj, lg, *, name):
    t, d3 = proj.shape
    d = d3 // 3
    dk, dv = d // 2 // GLA_HEADS, d // GLA_HEADS
    assert dk == LANES and dv == 2 * LANES
    c = GLA_CHUNK
    nc = t // c
    scale = dk ** -0.5
    nh = GLA_HEADS

    def body(q_ref, k_ref, v_ref, lg_ref, o_ref, st_out_ref, st_ref):
        @pl.when(pl.program_id(1) == 0)
        def _():
            st_ref[...] = jnp.zeros_like(st_ref)

        causal, _, q, k, _, last, eq, ek, el = _gla_chunk(q_ref, k_ref, lg_ref, scale)
        v = v_ref[...]
        st = st_ref[...]
        st_out_ref[...] = st
        qt = q * eq
        scores = jnp.where(causal, _dot(qt, k * ek, _NT), 0.0)
        o_ref[...] = _dot(qt, st, _NT) + _dot(scores, v)
        st_ref[...] = st * jnp.exp(last) + _dot(v, k * el, _TN)

    return pl.pallas_call(
        body, name=name, grid=(nh, nc),
        in_specs=[pl.BlockSpec((c, dk), lambda h, i: (i, h)), pl.BlockSpec((c, dk), lambda h, i: (i, nh + h)),
                  pl.BlockSpec((c, dv), lambda h, i: (i, nh + h)), pl.BlockSpec((c, dk), lambda h, i: (i, h))],
        out_specs=[pl.BlockSpec((c, dv), lambda h, i: (i, h)),
                   pl.BlockSpec((None, None, dv, dk), lambda h, i: (h, i, 0, 0))],
        out_shape=[jax.ShapeDtypeStruct((t, d), F32), jax.ShapeDtypeStruct((nh, nc, dv, dk), F32)],
        scratch_shapes=[pltpu.VMEM((dv, dk), F32)], compiler_params=_params("parallel", "arbitrary"),
    )(proj, proj, proj, lg)


def gla_bwd(proj, lg, states, do, *, name):
    t, d3 = proj.shape
    d = d3 // 3
    dk, dv = d // 2 // GLA_HEADS, d // GLA_HEADS
    c = GLA_CHUNK
    nc = t // c
    scale = dk ** -0.5
    nh = GLA_HEADS

    def body(q_ref, k_ref, v_ref, lg_ref, st_ref, do_ref, dq_ref, dk_ref, dv_ref, dzg_ref, dbg_ref, dst_ref):
        @pl.when(pl.program_id(1) == 0)
        def _():
            dst_ref[...] = jnp.zeros_like(dst_ref)
            dbg_ref[...] = jnp.zeros_like(dbg_ref)

        causal, tril, q, k, _, last, eq, ek, el = _gla_chunk(q_ref, k_ref, lg_ref, scale)
        v, st, dov, dst = v_ref[...], st_ref[...], do_ref[...], dst_ref[...]
        qt, kt, kh = q * eq, k * ek, k * el
        scores = jnp.where(causal, _dot(qt, kt, _NT), 0.0)
        dscores = jnp.where(causal, _dot(dov, v, _NT), 0.0)
        dqt = _dot(dov, st) + _dot(dscores, kt)
        dkt = _dot(dscores, qt, _TN)
        dkh = _dot(v, dst)
        dv_ref[...] = _dot(scores, dov, _TN) + _dot(kh, dst, _NT)
        dq_ref[...] = dqt * eq * scale
        dk_ref[...] = dkt * ek + dkh * el
        kh_dkh = kh * dkh
        e_last = jnp.exp(last)
        dlast = jnp.sum(kh_dkh, axis=0, keepdims=True) + e_last * jnp.sum(dst * st, axis=0, keepdims=True)
        dcum = qt * dqt - kt * dkt - kh_dkh
        dlg = _dot_exact_lhs(tril, dcum, _TN) + dlast
        lgv = lg_ref[...]
        dzg = dlg * (1.0 - jnp.exp(lgv * GLA_GATE_NORMALIZER)) / GLA_GATE_NORMALIZER
        dzg_ref[...] = dzg
        dbg_ref[...] += jnp.sum(dzg, axis=0, keepdims=True)
        dst_ref[...] = dst * e_last + _dot(dov, qt, _TN)

    rev = lambda i: nc - 1 - i
    half = jax.ShapeDtypeStruct((t, d // 2), F32)
    return pl.pallas_call(
        body, name=name, grid=(nh, nc),
        in_specs=[pl.BlockSpec((c, dk), lambda h, i: (rev(i), h)), pl.BlockSpec((c, dk), lambda h, i: (rev(i), nh + h)),
                  pl.BlockSpec((c, dv), lambda h, i: (rev(i), nh + h)), pl.BlockSpec((c, dk), lambda h, i: (rev(i), h)),
                  pl.BlockSpec((None, None, dv, dk), lambda h, i: (h, rev(i), 0, 0)),
                  pl.BlockSpec((c, dv), lambda h, i: (rev(i), h))],
        out_specs=[pl.BlockSpec((c, dk), lambda h, i: (rev(i), h)), pl.BlockSpec((c, dk), lambda h, i: (rev(i), h)),
                   pl.BlockSpec((c, dv), lambda h, i: (rev(i), h)), pl.BlockSpec((c, dk), lambda h, i: (rev(i), h)),
                   pl.BlockSpec((1, dk), lambda h, i: (0, h))],
        out_shape=[half, half, jax.ShapeDtypeStruct((t, d), F32), half, jax.ShapeDtypeStruct((1, d // 2), F32)],
        scratch_shapes=[pltpu.VMEM((dv, dk), F32)], compiler_params=_params("parallel", "arbitrary"),
    )(proj, proj, proj, lg, states, do)


def gla_post_fwd(o, proj, head_norm, *, name, tm=512):
    t, d = o.shape
    dv = d // GLA_HEADS
    tm = min(tm, t)

    def body(o_ref, g_ref, hn_ref, y_ref):
        for h in range(GLA_HEADS):
            sl = slice(h * dv, (h + 1) * dv)
            ov, gv = o_ref[:, sl], g_ref[:, sl]
            r = lax.rsqrt(jnp.mean(ov * ov, axis=-1, keepdims=True) + RMS_EPS)
            y_ref[:, sl] = (ov * r * hn_ref[:, sl]) * (gv * _sigmoid(gv))

    row = pl.BlockSpec((tm, d), lambda i: (i, 0))
    return pl.pallas_call(
        body, name=name, grid=(t // tm,),
        in_specs=[row, pl.BlockSpec((tm, d), lambda i: (i, 2)), pl.BlockSpec((1, d), lambda i: (0, 0))],
        out_specs=row, out_shape=jax.ShapeDtypeStruct((t, d), F32), compiler_params=_params("parallel"),
    )(o, proj, head_norm)


def gla_post_bwd(o, proj, head_norm, dy, *, name, tm=512):
    t, d = o.shape
    dv = d // GLA_HEADS
    tm = min(tm, t)

    def body(o_ref, g_ref, hn_ref, dy_ref, do_ref, dg_ref, dhn_ref):
        @pl.when(pl.program_id(0) == 0)
        def _():
            dhn_ref[...] = jnp.zeros_like(dhn_ref)

        for h in range(GLA_HEADS):
            sl = slice(h * dv, (h + 1) * dv)
            ov, gv, dyv, hn = o_ref[:, sl], g_ref[:, sl], dy_ref[:, sl], hn_ref[:, sl]
            r = lax.rsqrt(jnp.mean(ov * ov, axis=-1, keepdims=True) + RMS_EPS)
            sg = _sigmoid(gv)
            silu = gv * sg
            on = ov * r * hn
            dg_ref[:, sl] = dyv * on * (sg * (1.0 + gv * (1.0 - sg)))
            don = dyv * silu
            u = don * hn
            do_ref[:, sl] = r * u - ov * (r * r * r * jnp.mean(u * ov, axis=-1, keepdims=True))
            dhn_ref[:, sl] += jnp.sum(don * ov * r, axis=0, keepdims=True)

    row = pl.BlockSpec((tm, d), lambda i: (i, 0))
    vec = pl.BlockSpec((1, d), lambda i: (0, 0))
    shape = jax.ShapeDtypeStruct((t, d), F32)
    return pl.pallas_call(
        body, name=name, grid=(t // tm,),
        in_specs=[row, pl.BlockSpec((tm, d), lambda i: (i, 2)), vec, row],
        out_specs=[row, row, vec], out_shape=[shape, shape, jax.ShapeDtypeStruct((1, d), F32)],
        compiler_params=_params("arbitrary"),
    )(o, proj, head_norm, dy)


def _ffn_fwd(h, gains, w_up, w_down, tag):
    xn = rms_fwd(h, gains[2], name=f"{tag}_ffn_norm", out_dtype=BF16)
    u = matmul(xn, w_up, mode='nn', name=f"{tag}_ffn_up")
    f = matmul(u, w_down, mode='nn', a_pre='relu2', name=f"{tag}_ffn_down")
    h_out = rms_fwd(f, gains[3], res=h, name=f"{tag}_ffn_out")
    return h_out, (h, xn, u, f)


def _ffn_bwd(dh, saved, gains, w_up, w_down, tag):
    h, xn, u, f = saved
    df, dg3 = rms_bwd(f, gains[3], dh, name=f"{tag}_ffn_out_bwd")
    du = matmul(df, w_down, mode='nt', epi='mul2relu', extra=u, name=f"{tag}_ffn_da")
    dw_down = matmul(u, df, mode='tn', a_pre='relu2', name=f"{tag}_ffn_dwdown")
    dw_up = matmul(xn, du, mode='tn', name=f"{tag}_ffn_dwup")
    dxn = matmul(du, w_up, mode='nt', name=f"{tag}_ffn_dxn")
    dh_in, dg2 = rms_bwd(h, gains[2], dxn, dres=dh, name=f"{tag}_ffn_norm_bwd")
    return dh_in, dg2, dg3, dw_up, dw_down


def _sb_layer_fwd(xn, w, j, tag):
    qkv = matmul(xn, w['sb_w_qkv'][j], mode='nn', out_dtype=BF16, name=f"{tag}_qkv")
    o, tot = sb_fwd(qkv, name=f"{tag}_sb")
    m = matmul(o, w['sb_w_o'][j], mode='nn', name=f"{tag}_wo")
    return m, (qkv, o, tot)


def _sb_layer_bwd(dm, xn, saved, w, j, tag):
    qkv, o, tot = saved
    do = matmul(dm, w['sb_w_o'][j], mode='nt', name=f"{tag}_do")
    dw_o = matmul(o, dm, mode='tn', name=f"{tag}_dwo")
    dq, dk, dv = sb_bwd(qkv, tot, do, name=f"{tag}_sb_bwd")
    dqkv = jnp.concatenate([dq, dk, dv], axis=1)
    dw_qkv = matmul(xn, dqkv, mode='tn', name=f"{tag}_dwqkv")
    dxn = matmul(dqkv, w['sb_w_qkv'][j], mode='nt', name=f"{tag}_dxn")
    return dxn, {('sb_w_qkv', j): dw_qkv, ('sb_w_o', j): dw_o}


def _conv_layer_fwd(xn, w, j, tag):
    bcu = matmul(xn, w['conv_w_in'][j], mode='nn', name=f"{tag}_in")
    y = conv_fwd(bcu, w['conv_w'][j], name=f"{tag}_conv")
    m = matmul(y, w['conv_w_out'][j], mode='nn', name=f"{tag}_out")
    return m, (bcu, y)


def _conv_layer_bwd(dm, xn, saved, w, j, tag):
    bcu, y = saved
    dy = matmul(dm, w['conv_w_out'][j], mode='nt', name=f"{tag}_dy")
    dw_out = matmul(y, dm, mode='tn', name=f"{tag}_dwout")
    db, dc, du, dw_conv = conv_bwd(bcu, w['conv_w'][j], dy, name=f"{tag}_conv_bwd")
    dbcu = jnp.concatenate([db, dc, du], axis=1)
    dw_in = matmul(xn, dbcu, mode='tn', name=f"{tag}_dwin")
    dxn = matmul(dbcu, w['conv_w_in'][j], mode='nt', name=f"{tag}_dxn")
    return dxn, {('conv_w_in', j): dw_in, ('conv_w', j): dw_conv, ('conv_w_out', j): dw_out}


def _gla_split(w_in, w_gate_up):
    d = w_in.shape[0]
    w_main = w_in[:, :3 * d]
    w_a = jnp.pad(w_in[:, 3 * d:], ((0, 0), (0, LANES - GLA_GATE_RANK)))
    w_gu = jnp.pad(w_gate_up, ((0, LANES - GLA_GATE_RANK), (0, 0)))
    return w_main, w_a, w_gu


def _gla_layer_fwd(xn, w, j, tag):
    w_main, w_a, w_gu = _gla_split(w['gla_w_in'][j], w['gla_w_gate_up'][j])
    proj = matmul(xn, w_main, mode='nn', name=f"{tag}_in")
    a_low = matmul(xn, w_a, mode='nn', name=f"{tag}_alow")
    lg = matmul(a_low, w_gu, mode='nn', epi='logsig16', extra=w['gla_b_gate'][j][None, :], name=f"{tag}_gate")
    o, states = gla_fwd(proj, lg, name=f"{tag}_gla")
    hn = w['gla_head_norm'][j].reshape(1, -1)
    y = gla_post_fwd(o, proj, hn, name=f"{tag}_post")
    m = matmul(y, w['gla_w_o'][j], mode='nn', name=f"{tag}_wo")
    return m, (proj, a_low, lg, o, states, y)


def _gla_layer_bwd(dm, xn, saved, w, j, tag):
    proj, a_low, lg, o, states, y = saved
    w_main, w_a, w_gu = _gla_split(w['gla_w_in'][j], w['gla_w_gate_up'][j])
    hn = w['gla_head_norm'][j].reshape(1, -1)
    dy = matmul(dm, w['gla_w_o'][j], mode='nt', name=f"{tag}_dy")
    dw_o = matmul(y, dm, mode='tn', name=f"{tag}_dwo")
    do, dg, dhn = gla_post_bwd(o, proj, hn, dy, name=f"{tag}_post_bwd")
    dq, dk, dv, dzg, dbg = gla_bwd(proj, lg, states, do, name=f"{tag}_gla_bwd")
    da_low = matmul(dzg, w_gu, mode='nt', name=f"{tag}_dalow")
    dw_gu = matmul(a_low, dzg, mode='tn', name=f"{tag}_dwgu")[:GLA_GATE_RANK]
    dproj = jnp.concatenate([dq, dk, dv, dg], axis=1)
    dw_main = matmul(xn, dproj, mode='tn', name=f"{tag}_dwin")
    dw_a = matmul(xn, da_low, mode='tn', name=f"{tag}_dwa")[:, :GLA_GATE_RANK]
    dxn_a = matmul(da_low, w_a, mode='nt', name=f"{tag}_dxn_a")
    dxn = matmul(dproj, w_main, mode='nt', epi='add', extra=dxn_a, name=f"{tag}_dxn")
    grads = {('gla_w_in', j): jnp.concatenate([dw_main, dw_a], axis=1), ('gla_w_gate_up', j): dw_gu,
             ('gla_b_gate', j): dbg[0], ('gla_head_norm', j): dhn.reshape(w['gla_head_norm'][j].shape),
             ('gla_w_o', j): dw_o}
    return dxn, grads


_MIXERS = ((_sb_layer_fwd, _sb_layer_bwd), (_conv_layer_fwd, _conv_layer_bwd), (_gla_layer_fwd, _gla_layer_bwd))


def local_step(x, w, target):
    depth = w['norm_gains'].shape[0]
    gains = [[w['norm_gains'][i, s][None, :] for s in range(4)] for i in range(depth)]
    h = x
    tape = []
    for i in range(depth):
        kind, j = i % 3, i // 3
        tag = f"l{i}"
        xn = rms_fwd(h, gains[i][0], name=f"{tag}_mix_norm", out_dtype=BF16)
        m, saved = _MIXERS[kind][0](xn, w, j, tag)
        h_mid = rms_fwd(m, gains[i][1], res=h, name=f"{tag}_mix_out")
        h_out, ffn_saved = _ffn_fwd(h_mid, gains[i], w['ffn_w_up'][i], w['ffn_w_down'][i], tag)
        tape.append((h, xn, m, saved, ffn_saved))
        h = h_out
    loss, dh = loss_head(h, target, name="loss_head")

    grads = {}
    dgains = [[None] * 4 for _ in range(depth)]
    for i in reversed(range(depth)):
        kind, j = i % 3, i // 3
        tag = f"l{i}"
        h_in, xn, m, saved, ffn_saved = tape[i]
        dh, dgains[i][2], dgains[i][3], grads[('ffn_w_up', i)], grads[('ffn_w_down', i)] = _ffn_bwd(
            dh, ffn_saved, gains[i], w['ffn_w_up'][i], w['ffn_w_down'][i], tag)
        dm, dgains[i][1] = rms_bwd(m, gains[i][1], dh, name=f"{tag}_mix_out_bwd")
        dxn, g = _MIXERS[kind][1](dm, xn, saved, w, j, tag)
        grads.update(g)
        dh, dgains[i][0] = rms_bwd(h_in, gains[i][0], dxn, dres=dh, name=f"{tag}_mix_norm_bwd")

    full = {'norm_gains': jnp.stack([jnp.concatenate(row, axis=0) for row in dgains])}
    for name in WEIGHTS[1:]:
        full[name] = jnp.stack([grads[(name, j)] for j in range(w[name].shape[0])])
    return loss, dh, full


def _segments(shard_shapes):
    segs, off = {}, 0
    for name in WEIGHTS:
        n = math.prod(shard_shapes[name]) * (2 if name in F32_PAYLOAD else 1)
        segs[name] = (off, n)
        off += -(-n // PACK_COLS) * PACK_COLS
    block = PACK_COLS * PACK_ROW_BLOCK
    return segs, -(-off // block) * block


def _pack(parts, segs, total, dtype):
    lead = next(iter(parts.values())).shape[:-1]
    pieces, pos = [], 0
    for name in WEIGHTS:
        off, n = segs[name]
        if off > pos:
            pieces.append(jnp.zeros(lead + (off - pos,), dtype))
        pieces.append(parts[name].astype(dtype))
        pos = off + n
    if total > pos:
        pieces.append(jnp.zeros(lead + (total - pos,), dtype))
    return jnp.concatenate(pieces, axis=-1)


def _unshard(gathered, axis):
    moved = jnp.moveaxis(gathered, 0, axis)
    shape = moved.shape
    return moved.reshape(shape[:axis] + (shape[axis] * shape[axis + 1],) + shape[axis + 2:])


def _shard_split(full, axis):
    shape = full.shape
    cut = full.reshape(shape[:axis] + (N_DEV, shape[axis] // N_DEV) + shape[axis + 1:])
    return jnp.moveaxis(cut, axis, 0)


def _mesh_position():
    return lax.axis_index("x"), lax.axis_index("y"), lax.axis_index("c")


def all_gather(shard, *, name):
    rows, cols = shard.shape

    def body(x_ref, out_ref, send_sems, recv_sems, local_sem):
        x, y, c = _mesh_position()
        me, sibling = (x, y, c), (x, y, 1 - c)
        chips = [(1 - x, y), (x, 1 - y), (1 - x, 1 - y)]

        def block(px, py, pc):
            return out_ref.at[pl.ds((4 * px + 2 * py + pc) * rows, rows), :]

        def copy(k, blk, to, src=None):
            return pltpu.make_async_remote_copy(
                src_ref=block(*blk) if src is None else src, dst_ref=block(*blk),
                send_sem=send_sems.at[k], recv_sem=recv_sems.at[k],
                device_id=to, device_id_type=pl.DeviceIdType.MESH)

        mine = pltpu.make_async_copy(x_ref, block(*me), local_sem)
        mine.start()
        first = [copy(0, me, sibling, src=x_ref)]
        first += [copy(1 + j, me, (*chip, c), src=x_ref) for j, chip in enumerate(chips)]
        for cp in first:
            cp.start()
        passed = [copy(4 + j, (*chip, c), sibling) for j, chip in enumerate(chips)]
        for j, chip in enumerate(chips):
            copy(1 + j, (*chip, c), me).wait_recv()
            passed[j].start()
        copy(0, sibling, me).wait_recv()
        for j, chip in enumerate(chips):
            copy(4 + j, (*chip, 1 - c), me).wait_recv()
        for cp in first + passed:
            cp.wait_send()
        mine.wait()

    return pl.pallas_call(
        body, name=name,
        out_shape=jax.ShapeDtypeStruct((N_DEV * rows, cols), shard.dtype),
        in_specs=[pl.BlockSpec(memory_space=pl.ANY)], out_specs=pl.BlockSpec(memory_space=pl.ANY),
        scratch_shapes=[pltpu.SemaphoreType.DMA((7,)), pltpu.SemaphoreType.DMA((7,)), pltpu.SemaphoreType.DMA],
    )(shard)


def exchange_shards(parts, *, name):
    _, rows, cols = parts.shape

    def body(in_ref, out_ref, send_sems, recv_sems, local_sem):
        x, y, c = _mesh_position()
        my_id = 4 * x + 2 * y + c
        mine = pltpu.make_async_copy(in_ref.at[my_id], out_ref.at[my_id], local_sem)
        mine.start()
        copies = []
        for k in range(1, N_DEV):
            px = 1 - x if k & 4 else x
            py = 1 - y if k & 2 else y
            pc = 1 - c if k & 1 else c
            peer_id = 4 * px + 2 * py + pc
            copies.append(pltpu.make_async_remote_copy(
                src_ref=in_ref.at[peer_id], dst_ref=out_ref.at[my_id],
                send_sem=send_sems.at[k - 1], recv_sem=recv_sems.at[k - 1],
                device_id=(px, py, pc), device_id_type=pl.DeviceIdType.MESH))
        for cp in copies:
            cp.start()
        for k in range(1, N_DEV):
            px = 1 - x if k & 4 else x
            py = 1 - y if k & 2 else y
            pc = 1 - c if k & 1 else c
            peer_id = 4 * px + 2 * py + pc
            pltpu.make_async_remote_copy(
                src_ref=in_ref.at[peer_id], dst_ref=out_ref.at[peer_id],
                send_sem=send_sems.at[k - 1], recv_sem=recv_sems.at[k - 1],
                device_id=(px, py, pc), device_id_type=pl.DeviceIdType.MESH).wait_recv()
        for cp in copies:
            cp.wait_send()
        mine.wait()

    return pl.pallas_call(
        body, name=name, out_shape=jax.ShapeDtypeStruct(parts.shape, parts.dtype),
        in_specs=[pl.BlockSpec(memory_space=pl.ANY)], out_specs=pl.BlockSpec(memory_space=pl.ANY),
        scratch_shapes=[pltpu.SemaphoreType.DMA((7,)), pltpu.SemaphoreType.DMA((7,)), pltpu.SemaphoreType.DMA],
    )(parts)


def adamw(parts, w, m, v, *, name):
    _, rows, cols = parts.shape
    tr = PACK_ROW_BLOCK
    c1 = 1.0 - ADAM_B1 ** ADAM_STEP
    c2 = 1.0 - ADAM_B2 ** ADAM_STEP

    def body(p_ref, w_ref, m_ref, v_ref, g_ref, d_ref, nm_ref, nv_ref):
        g = p_ref[0].astype(F32)
        for s in range(1, N_DEV):
            g = g + p_ref[s].astype(F32)
        nm = ADAM_B1 * m_ref[...] + (1.0 - ADAM_B1) * g
        nv = ADAM_B2 * v_ref[...] + (1.0 - ADAM_B2) * jnp.square(g)
        m_hat = nm / c1
        v_hat = nv / c2
        g_ref[...] = g
        d_ref[...] = -ADAM_LR * (m_hat / (jnp.sqrt(v_hat) + ADAM_EPS) + ADAM_WD * w_ref[...])
        nm_ref[...] = nm
        nv_ref[...] = nv

    row = pl.BlockSpec((tr, cols), lambda i: (i, 0))
    shape = jax.ShapeDtypeStruct((rows, cols), F32)
    return pl.pallas_call(
        body, name=name, grid=(rows // tr,),
        in_specs=[pl.BlockSpec((N_DEV, tr, cols), lambda i: (0, i, 0)), row, row, row],
        out_specs=[row, row, row, row], out_shape=[shape, shape, shape, shape],
        compiler_params=_params("parallel"),
    )(parts, w, m, v)


def kernel(x, norm_gains, sb_w_qkv, sb_w_o, conv_w_in, conv_w, conv_w_out, gla_w_in, gla_w_gate_up, gla_b_gate, gla_head_norm, gla_w_o, ffn_w_up, ffn_w_down, loss_target, m_norm_gains, m_sb_w_qkv, m_sb_w_o, m_conv_w_in, m_conv_w, m_conv_w_out, m_gla_w_in, m_gla_w_gate_up, m_gla_b_gate, m_gla_head_norm, m_gla_w_o, m_ffn_w_up, m_ffn_w_down, v_norm_gains, v_sb_w_qkv, v_sb_w_o, v_conv_w_in, v_conv_w, v_conv_w_out, v_gla_w_in, v_gla_w_gate_up, v_gla_b_gate, v_gla_head_norm, v_gla_w_o, v_ffn_w_up, v_ffn_w_down):
    shards = dict(zip(WEIGHTS, (norm_gains, sb_w_qkv, sb_w_o, conv_w_in, conv_w, conv_w_out, gla_w_in,
                                gla_w_gate_up, gla_b_gate, gla_head_norm, gla_w_o, ffn_w_up, ffn_w_down)))
    moments_m = dict(zip(WEIGHTS, (m_norm_gains, m_sb_w_qkv, m_sb_w_o, m_conv_w_in, m_conv_w, m_conv_w_out,
                                   m_gla_w_in, m_gla_w_gate_up, m_gla_b_gate, m_gla_head_norm, m_gla_w_o,
                                   m_ffn_w_up, m_ffn_w_down)))
    moments_v = dict(zip(WEIGHTS, (v_norm_gains, v_sb_w_qkv, v_sb_w_o, v_conv_w_in, v_conv_w, v_conv_w_out,
                                   v_gla_w_in, v_gla_w_gate_up, v_gla_b_gate, v_gla_head_norm, v_gla_w_o,
                                   v_ffn_w_up, v_ffn_w_down)))
    shard_shapes = {n: a.shape for n, a in shards.items()}

    segs, total = _segments(shard_shapes)
    payload = _pack({n: (lax.bitcast_convert_type(a, BF16) if n in F32_PAYLOAD else a.astype(BF16)).reshape(-1)
                     for n, a in shards.items()}, segs, total, BF16)
    rows = total // PACK_COLS
    gathered = all_gather(payload.reshape(rows, PACK_COLS), name="weights_all_gather").reshape(N_DEV, total)
    whole = {}
    for n in WEIGHTS:
        off, length = segs[n]
        piece = gathered[:, off:off + length]
        if n in F32_PAYLOAD:
            piece = lax.bitcast_convert_type(piece.reshape(N_DEV, length // 2, 2), F32)
        whole[n] = _unshard(piece.reshape((N_DEV,) + shard_shapes[n]), SHARD_AXIS[n])

    loss, grad_x, grads = local_step(x[0], whole, loss_target[0])
    loss = lax.psum(loss[0, 0], ("x", "y", "c"))

    gsegs, gtotal = _segments_plain(shard_shapes)
    parts = _pack({n: _shard_split(grads[n], SHARD_AXIS[n]).reshape(N_DEV, -1) for n in WEIGHTS}, gsegs, gtotal, BF16)
    grows = gtotal // PACK_COLS
    received = exchange_shards(parts.reshape(N_DEV, grows, PACK_COLS), name="grads_exchange")

    def packed(group):
        return _pack({n: group[n].reshape(-1) for n in WEIGHTS}, gsegs, gtotal, F32).reshape(grows, PACK_COLS)

    outs = adamw(received, packed(shards), packed(moments_m), packed(moments_v), name="adamw")
    unpacked = []
    for buf in outs:
        flat = buf.reshape(-1)
        unpacked.append([flat[gsegs[n][0]:gsegs[n][0] + gsegs[n][1]].reshape(shard_shapes[n]) for n in WEIGHTS])
    return (loss, grad_x[None], *unpacked[0], *unpacked[1], *unpacked[2], *unpacked[3])


def _segments_plain(shard_shapes):
    segs, off = {}, 0
    for name in WEIGHTS:
        n = math.prod(shard_shapes[name])
        segs[name] = (off, n)
        off += -(-n // PACK_COLS) * PACK_COLS
    block = PACK_COLS * PACK_ROW_BLOCK
    return segs, -(-off // block) * block
```

```python
import functools
import math

import jax
import jax.numpy as jnp
from jax import lax
from jax.experimental import pallas as pl
from jax.experimental.pallas import tpu as pltpu

F32 = jnp.float32
BF16 = jnp.bfloat16

N_DEV = 8
SB_HEADS = 16
GLA_HEADS = 4
GLA_CHUNK = 64
GLA_GATE_RANK = 16
GLA_GATE_NORMALIZER = 16.0
CONV_WIDTH = 3
DEPTH = 4
RMS_EPS = 1e-6
ADAM_LR = 0.001
ADAM_B1 = 0.9
ADAM_B2 = 0.999
ADAM_EPS = 1e-08
ADAM_WD = 0.01
ADAM_STEP = 10

LANES = 128
SB_BLOCK = 256
VMEM_LIMIT_BYTES = 56 * 1024 * 1024
PACK_COLS = 1024
PACK_ROW_ALIGN = 16
PACK_ROW_BLOCK = 128

WEIGHTS = ['norm_gains', 'sb_w_qkv', 'sb_w_o', 'conv_w_in', 'conv_w', 'conv_w_out', 'gla_w_in',
           'gla_w_gate_up', 'gla_b_gate', 'gla_head_norm', 'gla_w_o', 'ffn_w_up', 'ffn_w_down']
SHARD_AXIS = {'norm_gains': 2, 'sb_w_qkv': 2, 'sb_w_o': 1, 'conv_w_in': 2, 'conv_w': 2, 'conv_w_out': 1,
              'gla_w_in': 2, 'gla_w_gate_up': 2, 'gla_b_gate': 1, 'gla_head_norm': 2, 'gla_w_o': 1,
              'ffn_w_up': 2, 'ffn_w_down': 1}
F32_PAYLOAD = ('norm_gains', 'conv_w', 'gla_b_gate', 'gla_head_norm')

_NN = (((1,), (0,)), ((), ()))
_NT = (((1,), (1,)), ((), ()))
_TN = (((0,), (0,)), ((), ()))
_DIMS = {'nn': _NN, 'nt': _NT, 'tn': _TN}


def _params(*semantics):
    return pltpu.CompilerParams(dimension_semantics=semantics, vmem_limit_bytes=VMEM_LIMIT_BYTES)


def _dot(a, b, dims=_NN):
    return lax.dot_general(a.astype(BF16), b.astype(BF16), dims, preferred_element_type=F32)


def _split_hi_lo(x):
    hi = x.astype(BF16)
    lo = (x - hi.astype(F32)).astype(BF16)
    return hi, lo


def _dot_exact_rhs(x, ones_mat, dims=_NN):
    hi, lo = _split_hi_lo(x)
    return (lax.dot_general(hi, ones_mat, dims, preferred_element_type=F32)
            + lax.dot_general(lo, ones_mat, dims, preferred_element_type=F32))


def _dot_exact_lhs(ones_mat, x, dims=_NN):
    hi, lo = _split_hi_lo(x)
    return (lax.dot_general(ones_mat, hi, dims, preferred_element_type=F32)
            + lax.dot_general(ones_mat, lo, dims, preferred_element_type=F32))


def _log_sigmoid(z):
    return jnp.minimum(z, 0.0) - jnp.log(1.0 + jnp.exp(-jnp.abs(z)))


def _sigmoid(z):
    return 1.0 / (1.0 + jnp.exp(-z))


def matmul(a, b, *, mode, name, out_dtype=F32, a_pre=None, epi=None, extra=None, tm=512, tn=512, tk=512):
    if mode == 'nn':
        (m, k), (k2, n) = a.shape, b.shape
    elif mode == 'nt':
        (m, k), (n, k2) = a.shape, b.shape
    else:
        (k, m), (k2, n) = a.shape, b.shape
    assert k == k2, (a.shape, b.shape, mode)
    tm, tn, tk = min(tm, m), min(tn, n), min(tk, k)
    assert m % tm == 0 and n % tn == 0 and k % tk == 0, (a.shape, b.shape, mode)
    nk = k // tk
    if mode == 'tn':
        a_spec = pl.BlockSpec((tk, tm), lambda i, j, kk: (kk, i))
    else:
        a_spec = pl.BlockSpec((tm, tk), lambda i, j, kk: (i, kk))
    if mode == 'nt':
        b_spec = pl.BlockSpec((tn, tk), lambda i, j, kk: (j, kk))
    else:
        b_spec = pl.BlockSpec((tk, tn), lambda i, j, kk: (kk, j))
    in_specs, operands = [a_spec, b_spec], [a, b]
    if epi == 'logsig16':
        in_specs.append(pl.BlockSpec((1, tn), lambda i, j, kk: (0, j)))
        operands.append(extra)
    elif epi is not None:
        in_specs.append(pl.BlockSpec((tm, tn), lambda i, j, kk: (i, j)))
        operands.append(extra)

    def body(a_ref, b_ref, *rest):
        if epi is None:
            o_ref, acc_ref = rest
        else:
            e_ref, o_ref, acc_ref = rest
        kk = pl.program_id(2)

        @pl.when(kk == 0)
        def _():
            acc_ref[...] = jnp.zeros_like(acc_ref)

        av = a_ref[...]
        if a_pre == 'relu2':
            av = jnp.square(jnp.maximum(av.astype(F32), 0.0))
        acc_ref[...] += _dot(av, b_ref[...], _DIMS[mode])

        @pl.when(kk == nk - 1)
        def _():
            r = acc_ref[...]
            if epi == 'mul2relu':
                r = r * (2.0 * jnp.maximum(e_ref[...], 0.0))
            elif epi == 'add':
                r = r + e_ref[...]
            elif epi == 'logsig16':
                r = _log_sigmoid(r + e_ref[...]) / GLA_GATE_NORMALIZER
            o_ref[...] = r.astype(out_dtype)

    return pl.pallas_call(
        body, name=name, grid=(m // tm, n // tn, nk),
        in_specs=in_specs, out_specs=pl.BlockSpec((tm, tn), lambda i, j, kk: (i, j)),
        out_shape=jax.ShapeDtypeStruct((m, n), out_dtype),
        scratch_shapes=[pltpu.VMEM((tm, tn), F32)],
        compiler_params=_params("parallel", "parallel", "arbitrary"),
    )(*operands)


def rms_fwd(x, gain, *, name, res=None, out_dtype=F32, tm=512):
    t, d = x.shape
    tm = min(tm, t)
    row = pl.BlockSpec((tm, d), lambda i: (i, 0))
    in_specs, operands = [row, pl.BlockSpec((1, d), lambda i: (0, 0))], [x, gain]
    if res is not None:
        in_specs.append(row)
        operands.append(res)

    def body(x_ref, g_ref, *rest):
        xv = x_ref[...]
        r = lax.rsqrt(jnp.mean(xv * xv, axis=-1, keepdims=True) + RMS_EPS)
        y = xv * r * g_ref[...]
        if res is not None:
            y = rest[0][...] + y
        rest[-1][...] = y.astype(out_dtype)

    return pl.pallas_call(
        body, name=name, grid=(t // tm,), in_specs=in_specs, out_specs=row,
        out_shape=jax.ShapeDtypeStruct((t, d), out_dtype), compiler_params=_params("parallel"),
    )(*operands)


def rms_bwd(x, gain, dy, *, name, dres=None, tm=512):
    t, d = x.shape
    tm = min(tm, t)
    row = pl.BlockSpec((tm, d), lambda i: (i, 0))
    vec = pl.BlockSpec((1, d), lambda i: (0, 0))
    in_specs, operands = [row, vec, row], [x, gain, dy]
    if dres is not None:
        in_specs.append(row)
        operands.append(dres)

    def body(x_ref, g_ref, dy_ref, *rest):
        dx_ref, dg_ref = rest[-2], rest[-1]

        @pl.when(pl.program_id(0) == 0)
        def _():
            dg_ref[...] = jnp.zeros_like(dg_ref)

        xv, dyv = x_ref[...], dy_ref[...]
        r = lax.rsqrt(jnp.mean(xv * xv, axis=-1, keepdims=True) + RMS_EPS)
        u = dyv * g_ref[...]
        dx = r * u - xv * (r * r * r * jnp.mean(u * xv, axis=-1, keepdims=True))
        if dres is not None:
            dx = rest[0][...] + dx
        dx_ref[...] = dx
        dg_ref[...] += jnp.sum(dyv * xv * r, axis=0, keepdims=True)

    return pl.pallas_call(
        body, name=name, grid=(t // tm,), in_specs=in_specs, out_specs=[row, vec],
        out_shape=[jax.ShapeDtypeStruct((t, d), F32), jax.ShapeDtypeStruct((1, d), F32)],
        compiler_params=_params("arbitrary"),
    )(*operands)


def loss_head(y, target, *, name, tm=512):
    t, d = y.shape
    tm = min(tm, t)
    nt = t // tm
    row = pl.BlockSpec((tm, d), lambda i: (i, 0))

    def body(y_ref, t_ref, loss_ref, dy_ref, acc_ref):
        i = pl.program_id(0)

        @pl.when(i == 0)
        def _():
            acc_ref[...] = jnp.zeros_like(acc_ref)

        err = y_ref[...] - t_ref[...]
        dy_ref[...] = err * (1.0 / d)
        acc_ref[...] += jnp.sum(err * err, axis=0, keepdims=True)

        @pl.when(i == nt - 1)
        def _():
            loss_ref[...] = jnp.sum(acc_ref[...], axis=1, keepdims=True) * (0.5 / d)

    return pl.pallas_call(
        body, name=name, grid=(nt,), in_specs=[row, row],
        out_specs=[pl.BlockSpec((1, 1), lambda i: (0, 0)), row],
        out_shape=[jax.ShapeDtypeStruct((1, 1), F32), jax.ShapeDtypeStruct((t, d), F32)],
        scratch_shapes=[pltpu.VMEM((1, d), F32)], compiler_params=_params("arbitrary"),
    )(y, target)


def _sb_masks(qi, kb):
    rows = lax.broadcasted_iota(jnp.int32, (SB_BLOCK, SB_BLOCK), 0)
    cols = lax.broadcasted_iota(jnp.int32, (SB_BLOCK, SB_BLOCK), 1)
    return (kb * SB_BLOCK + cols) < (qi * SB_BLOCK + rows)


def _sb_tri(strict):
    r = lax.broadcasted_iota(jnp.int32, (SB_BLOCK, SB_BLOCK), 0)
    c = lax.broadcasted_iota(jnp.int32, (SB_BLOCK, SB_BLOCK), 1)
    return jnp.where((r > c) if strict else (r >= c), 1.0, 0.0).astype(BF16)


def _sb_logits(q_h, k_blk, mask, scale):
    z = _dot(q_h, k_blk, _NT) * scale
    ls = _log_sigmoid(z)
    lm = ls - z
    lmm = jnp.where(mask, lm, 0.0)
    return ls, lm, lmm, jnp.sum(lmm, axis=1, keepdims=True)


def _sb_weights(ls, lmm, mask, tri_strict, later):
    suffix = _dot_exact_rhs(lmm, tri_strict)
    return jnp.where(mask, jnp.exp(ls + suffix + later), 0.0)


def sb_fwd(qkv, *, name):
    t, d3 = qkv.shape
    d = d3 // 3
    head_dim = d // SB_HEADS
    assert 2 * head_dim == LANES and t % SB_BLOCK == 0
    pairs = d // LANES
    nq = t // SB_BLOCK
    scale = head_dim ** -0.5

    def body(q_ref, k_ref, v_ref, o_ref, tot_ref):
        qi = pl.program_id(1)
        lane = lax.broadcasted_iota(jnp.int32, (SB_BLOCK, LANES), 1)
        first = lane < head_dim
        q = q_ref[...]
        qs = [jnp.where(first, q, jnp.zeros_like(q)), jnp.where(first, jnp.zeros_like(q), q)]
        tri = _sb_tri(True)

        def step(i, carry):
            kb = qi - i
            ks = pl.multiple_of(kb * SB_BLOCK, SB_BLOCK)
            k_blk = k_ref[pl.ds(ks, SB_BLOCK), :]
            v_blk = v_ref[pl.ds(ks, SB_BLOCK), :]
            mask = _sb_masks(qi, kb)
            new = []
            for h in range(2):
                acc, later = carry[2 * h], carry[2 * h + 1]
                ls, _, lmm, row = _sb_logits(qs[h], k_blk, mask, scale)
                w = _sb_weights(ls, lmm, mask, tri, later)
                new += [acc + _dot(w, v_blk), later + row]
            return tuple(new)

        zero = jnp.zeros((SB_BLOCK, LANES), F32)
        zcol = jnp.zeros((SB_BLOCK, 1), F32)
        out = lax.fori_loop(0, qi + 1, step, (zero, zcol, zero, zcol))
        o_ref[...] = jnp.where(first, out[0], out[2])
        tot_ref[...] = jnp.where(first, out[1], out[3])

    blk = lambda off: pl.BlockSpec((t, LANES), lambda p, i: (0, off + p))
    qblk = pl.BlockSpec((SB_BLOCK, LANES), lambda p, i: (i, p))
    shape = jax.ShapeDtypeStruct((t, d), F32)
    return pl.pallas_call(
        body, name=name, grid=(pairs, nq), in_specs=[qblk, blk(pairs), blk(2 * pairs)],
        out_specs=[qblk, qblk], out_shape=[shape, shape], compiler_params=_params("parallel", "arbitrary"),
    )(qkv, qkv, qkv)


def sb_bwd(qkv, tot, do, *, name):
    t, d3 = qkv.shape
    d = d3 // 3
    head_dim = d // SB_HEADS
    pairs = d // LANES
    nq = t // SB_BLOCK
    scale = head_dim ** -0.5

    def body(q_ref, k_ref, v_ref, tot_ref, do_ref, dq_ref, dk_ref, dv_ref):
        qi = pl.program_id(1)

        @pl.when(qi == 0)
        def _():
            dk_ref[...] = jnp.zeros_like(dk_ref)
            dv_ref[...] = jnp.zeros_like(dv_ref)

        lane = lax.broadcasted_iota(jnp.int32, (SB_BLOCK, LANES), 1)
        first = lane < head_dim
        q, dov, totv = q_ref[...], do_ref[...], tot_ref[...]
        sel = [first, jnp.logical_not(first)]
        qs = [jnp.where(s, q, jnp.zeros_like(q)) for s in sel]
        dos = [jnp.where(s, dov, 0.0) for s in sel]
        tots = [totv[:, 0:1], totv[:, head_dim:head_dim + 1]]
        tri_strict = _sb_tri(True)
        rows = lax.broadcasted_iota(jnp.int32, (SB_BLOCK, SB_BLOCK), 0)
        cols = lax.broadcasted_iota(jnp.int32, (SB_BLOCK, SB_BLOCK), 1)
        tri_before = jnp.where(rows < cols, 1.0, 0.0).astype(BF16)

        def step(kb, carry):
            ks = pl.multiple_of(kb * SB_BLOCK, SB_BLOCK)
            k_blk = k_ref[pl.ds(ks, SB_BLOCK), :]
            v_blk = v_ref[pl.ds(ks, SB_BLOCK), :]
            mask = _sb_masks(qi, kb)
            new = []
            dk_acc = jnp.zeros((SB_BLOCK, LANES), F32)
            dv_acc = jnp.zeros((SB_BLOCK, LANES), F32)
            for h in range(2):
                dq, seen, before = carry[3 * h], carry[3 * h + 1], carry[3 * h + 2]
                ls, lm, lmm, row = _sb_logits(qs[h], k_blk, mask, scale)
                seen = seen + row
                w = _sb_weights(ls, lmm, mask, tri_strict, tots[h] - seen)
                da = _dot(dos[h], v_blk, _NT) * w
                g = _dot_exact_rhs(da, tri_before) + before
                dz = jnp.where(mask, da * jnp.exp(lm) - g * jnp.exp(ls), 0.0) * scale
                dq = dq + _dot(dz, k_blk)
                dk_acc = dk_acc + _dot(dz, qs[h], _TN)
                dv_acc = dv_acc + _dot(w, dos[h], _TN)
                new += [dq, seen, before + jnp.sum(da, axis=1, keepdims=True)]
            dk_ref[pl.ds(ks, SB_BLOCK), :] += dk_acc
            dv_ref[pl.ds(ks, SB_BLOCK), :] += dv_acc
            return tuple(new)

        zero = jnp.zeros((SB_BLOCK, LANES), F32)
        zcol = jnp.zeros((SB_BLOCK, 1), F32)
        out = lax.fori_loop(0, qi + 1, step, (zero, zcol, zcol, zero, zcol, zcol))
        dq_ref[...] = jnp.where(first, out[0], out[3])

    qblk = pl.BlockSpec((SB_BLOCK, LANES), lambda p, i: (i, p))
    col = lambda off: pl.BlockSpec((t, LANES), lambda p, i: (0, off + p))
    shape = jax.ShapeDtypeStruct((t, d), F32)
    return pl.pallas_call(
        body, name=name, grid=(pairs, nq),
        in_specs=[qblk, col(pairs), col(2 * pairs), qblk, qblk],
        out_specs=[qblk, col(0), col(0)], out_shape=[shape, shape, shape],
        compiler_params=_params("parallel", "arbitrary"),
    )(qkv, qkv, qkv, tot, do)


def _shift_down(x, s):
    rows = lax.broadcasted_iota(jnp.int32, x.shape, 0)
    return jnp.where(rows >= s, pltpu.roll(x, s, 0), 0.0)


def _shift_up(x, s):
    t = x.shape[0]
    rows = lax.broadcasted_iota(jnp.int32, x.shape, 0)
    return jnp.where(rows < t - s, pltpu.roll(x, t - s, 0), 0.0)


def conv_fwd(bcu, w, *, name):
    t, d3 = bcu.shape
    d = d3 // 3
    nb = d // LANES
    col = lambda off: pl.BlockSpec((t, LANES), lambda j: (0, off + j))

    def body(b_ref, c_ref, u_ref, w_ref, y_ref):
        hh = c_ref[...] * u_ref[...]
        conv = w_ref[0:1, :] * _shift_down(hh, 2) + w_ref[1:2, :] * _shift_down(hh, 1) + w_ref[2:3, :] * hh
        y_ref[...] = b_ref[...] * conv

    return pl.pallas_call(
        body, name=name, grid=(nb,),
        in_specs=[col(0), col(nb), col(2 * nb), pl.BlockSpec((CONV_WIDTH, LANES), lambda j: (0, j))],
        out_specs=col(0), out_shape=jax.ShapeDtypeStruct((t, d), F32), compiler_params=_params("parallel"),
    )(bcu, bcu, bcu, w)


def conv_bwd(bcu, w, dy, *, name):
    t, d3 = bcu.shape
    d = d3 // 3
    nb = d // LANES
    col = lambda off: pl.BlockSpec((t, LANES), lambda j: (0, off + j))
    wspec = pl.BlockSpec((CONV_WIDTH, LANES), lambda j: (0, j))

    def body(b_ref, c_ref, u_ref, w_ref, dy_ref, db_ref, dc_ref, du_ref, dw_ref):
        c, u, dyv = c_ref[...], u_ref[...], dy_ref[...]
        hh = c * u
        h2, h1 = _shift_down(hh, 2), _shift_down(hh, 1)
        w0, w1, w2 = w_ref[0:1, :], w_ref[1:2, :], w_ref[2:3, :]
        db_ref[...] = dyv * (w0 * h2 + w1 * h1 + w2 * hh)
        dconv = dyv * b_ref[...]
        dhh = w2 * dconv + w1 * _shift_up(dconv, 1) + w0 * _shift_up(dconv, 2)
        dc_ref[...] = dhh * u
        du_ref[...] = dhh * c
        dw_ref[0:1, :] = jnp.sum(dconv * h2, axis=0, keepdims=True)
        dw_ref[1:2, :] = jnp.sum(dconv * h1, axis=0, keepdims=True)
        dw_ref[2:3, :] = jnp.sum(dconv * hh, axis=0, keepdims=True)

    shape = jax.ShapeDtypeStruct((t, d), F32)
    return pl.pallas_call(
        body, name=name, grid=(nb,),
        in_specs=[col(0), col(nb), col(2 * nb), wspec, col(0)],
        out_specs=[col(0), col(0), col(0), wspec],
        out_shape=[shape, shape, shape, jax.ShapeDtypeStruct((CONV_WIDTH, d), F32)],
        compiler_params=_params("parallel"),
    )(bcu, bcu, bcu, w, dy)


def _gla_chunk(q_ref, k_ref, lg_ref, scale):
    c = GLA_CHUNK
    rows = lax.broadcasted_iota(jnp.int32, (c, c), 0)
    cols = lax.broadcasted_iota(jnp.int32, (c, c), 1)
    causal = rows >= cols
    tril = jnp.where(causal, 1.0, 0.0).astype(BF16)
    q = q_ref[...] * scale
    k = k_ref[...]
    cum = _dot_exact_lhs(tril, lg_ref[...])
    last = cum[c - 1:c, :]
    eq = jnp.exp(cum)
    ek = jnp.exp(-cum)
    el = jnp.exp(last - cum)
    return causal, tril, q, k, cum, last, eq, ek, el


def gla_fwd(proj, lg, *, name):
    t, d3 = proj.shape
    d = d3 // 3
    dk, dv = d // 2 // GLA_HEADS, d // GLA_HEADS
    assert dk == LANES and dv == 2 * LANES
    c = GLA_CHUNK
    nc = t // c
    scale = dk ** -0.5
    nh = GLA_HEADS

    def body(q_ref, k_ref, v_ref, lg_ref, o_ref, st_out_ref, st_ref):
        @pl.when(pl.program_id(1) == 0)
        def _():
            st_ref[...] = jnp.zeros_like(st_ref)

        causal, _, q, k, _, last, eq, ek, el = _gla_chunk(q_ref, k_ref, lg_ref, scale)
        v = v_ref[...]
        st = st_ref[...]
        st_out_ref[...] = st
        qt = q * eq
        scores = jnp.where(causal, _dot(qt, k * ek, _NT), 0.0)
        o_ref[...] = _dot(qt, st, _NT) + _dot(scores, v)
        st_ref[...] = st * jnp.exp(last) + _dot(v, k * el, _TN)

    return pl.pallas_call(
        body, name=name, grid=(nh, nc),
        in_specs=[pl.BlockSpec((c, dk), lambda h, i: (i, h)), pl.BlockSpec((c, dk), lambda h, i: (i, nh + h)),
                  pl.BlockSpec((c, dv), lambda h, i: (i, nh + h)), pl.BlockSpec((c, dk), lambda h, i: (i, h))],
        out_specs=[pl.BlockSpec((c, dv), lambda h, i: (i, h)),
                   pl.BlockSpec((None, None, dv, dk), lambda h, i: (h, i, 0, 0))],
        out_shape=[jax.ShapeDtypeStruct((t, d), F32), jax.ShapeDtypeStruct((nh, nc, dv, dk), F32)],
        scratch_shapes=[pltpu.VMEM((dv, dk), F32)], compiler_params=_params("parallel", "arbitrary"),
    )(proj, proj, proj, lg)


def gla_bwd(proj, lg, states, do, *, name):
    t, d3 = proj.shape
    d = d3 // 3
    dk, dv = d // 2 // GLA_HEADS, d // GLA_HEADS
    c = GLA_CHUNK
    nc = t // c
    scale = dk ** -0.5
    nh = GLA_HEADS

    def body(q_ref, k_ref, v_ref, lg_ref, st_ref, do_ref, dq_ref, dk_ref, dv_ref, dzg_ref, dbg_ref, dst_ref):
        @pl.when(pl.program_id(1) == 0)
        def _():
            dst_ref[...] = jnp.zeros_like(dst_ref)
            dbg_ref[...] = jnp.zeros_like(dbg_ref)

        causal, tril, q, k, _, last, eq, ek, el = _gla_chunk(q_ref, k_ref, lg_ref, scale)
        v, st, dov, dst = v_ref[...], st_ref[...], do_ref[...], dst_ref[...]
        qt, kt, kh = q * eq, k * ek, k * el
        scores = jnp.where(causal, _dot(qt, kt, _NT), 0.0)
        dscores = jnp.where(causal, _dot(dov, v, _NT), 0.0)
        dqt = _dot(dov, st) + _dot(dscores, kt)
        dkt = _dot(dscores, qt, _TN)
        dkh = _dot(v, dst)
        dv_ref[...] = _dot(scores, dov, _TN) + _dot(kh, dst, _NT)
        dq_ref[...] = dqt * eq * scale
        dk_ref[...] = dkt * ek + dkh * el
        kh_dkh = kh * dkh
        e_last = jnp.exp(last)
        dlast = jnp.sum(kh_dkh, axis=0, keepdims=True) + e_last * jnp.sum(dst * st, axis=0, keepdims=True)
        dcum = qt * dqt - kt * dkt - kh_dkh
        dlg = _dot_exact_lhs(tril, dcum, _TN) + dlast
        lgv = lg_ref[...]
        dzg = dlg * (1.0 - jnp.exp(lgv * GLA_GATE_NORMALIZER)) / GLA_GATE_NORMALIZER
        dzg_ref[...] = dzg
        dbg_ref[...] += jnp.sum(dzg, axis=0, keepdims=True)
        dst_ref[...] = dst * e_last + _dot(dov, qt, _TN)

    rev = lambda i: nc - 1 - i
    half = jax.ShapeDtypeStruct((t, d // 2), F32)
    return pl.pallas_call(
        body, name=name, grid=(nh, nc),
        in_specs=[pl.BlockSpec((c, dk), lambda h, i: (rev(i), h)), pl.BlockSpec((c, dk), lambda h, i: (rev(i), nh + h)),
                  pl.BlockSpec((c, dv), lambda h, i: (rev(i), nh + h)), pl.BlockSpec((c, dk), lambda h, i: (rev(i), h)),
                  pl.BlockSpec((None, None, dv, dk), lambda h, i: (h, rev(i), 0, 0)),
                  pl.BlockSpec((c, dv), lambda h, i: (rev(i), h))],
        out_specs=[pl.BlockSpec((c, dk), lambda h, i: (rev(i), h)), pl.BlockSpec((c, dk), lambda h, i: (rev(i), h)),
                   pl.BlockSpec((c, dv), lambda h, i: (rev(i), h)), pl.BlockSpec((c, dk), lambda h, i: (rev(i), h)),
                   pl.BlockSpec((1, dk), lambda h, i: (0, h))],
        out_shape=[half, half, jax.ShapeDtypeStruct((t, d), F32), half, jax.ShapeDtypeStruct((1, d // 2), F32)],
        scratch_shapes=[pltpu.VMEM((dv, dk), F32)], compiler_params=_params("parallel", "arbitrary"),
    )(proj, proj, proj, lg, states, do)


def gla_post_fwd(o, proj, head_norm, *, name, tm=512):
    t, d = o.shape
    dv = d // GLA_HEADS
    tm = min(tm, t)

    def body(o_ref, g_ref, hn_ref, y_ref):
        for h in range(GLA_HEADS):
            sl = slice(h * dv, (h + 1) * dv)
            ov, gv = o_ref[:, sl], g_ref[:, sl]
            r = lax.rsqrt(jnp.mean(ov * ov, axis=-1, keepdims=True) + RMS_EPS)
            y_ref[:, sl] = (ov * r * hn_ref[:, sl]) * (gv * _sigmoid(gv))

    row = pl.BlockSpec((tm, d), lambda i: (i, 0))
    return pl.pallas_call(
        body, name=name, grid=(t // tm,),
        in_specs=[row, pl.BlockSpec((tm, d), lambda i: (i, 2)), pl.BlockSpec((1, d), lambda i: (0, 0))],
        out_specs=row, out_shape=jax.ShapeDtypeStruct((t, d), F32), compiler_params=_params("parallel"),
    )(o, proj, head_norm)


def gla_post_bwd(o, proj, head_norm, dy, *, name, tm=512):
    t, d = o.shape
    dv = d // GLA_HEADS
    tm = min(tm, t)

    def body(o_ref, g_ref, hn_ref, dy_ref, do_ref, dg_ref, dhn_ref):
        @pl.when(pl.program_id(0) == 0)
        def _():
            dhn_ref[...] = jnp.zeros_like(dhn_ref)

        for h in range(GLA_HEADS):
            sl = slice(h * dv, (h + 1) * dv)
            ov, gv, dyv, hn = o_ref[:, sl], g_ref[:, sl], dy_ref[:, sl], hn_ref[:, sl]
            r = lax.rsqrt(jnp.mean(ov * ov, axis=-1, keepdims=True) + RMS_EPS)
            sg = _sigmoid(gv)
            silu = gv * sg
            on = ov * r * hn
            dg_ref[:, sl] = dyv * on * (sg * (1.0 + gv * (1.0 - sg)))
            don = dyv * silu
            u = don * hn
            do_ref[:, sl] = r * u - ov * (r * r * r * jnp.mean(u * ov, axis=-1, keepdims=True))
            dhn_ref[:, sl] += jnp.sum(don * ov * r, axis=0, keepdims=True)

    row = pl.BlockSpec((tm, d), lambda i: (i, 0))
    vec = pl.BlockSpec((1, d), lambda i: (0, 0))
    shape = jax.ShapeDtypeStruct((t, d), F32)
    return pl.pallas_call(
        body, name=name, grid=(t // tm,),
        in_specs=[row, pl.BlockSpec((tm, d), lambda i: (i, 2)), vec, row],
        out_specs=[row, row, vec], out_shape=[shape, shape, jax.ShapeDtypeStruct((1, d), F32)],
        compiler_params=_params("arbitrary"),
    )(o, proj, head_norm, dy)


def _ffn_fwd(h, gains, w_up, w_down, tag):
    xn = rms_fwd(h, gains[2], name=f"{tag}_ffn_norm", out_dtype=BF16)
    u = matmul(xn, w_up, mode='nn', name=f"{tag}_ffn_up")
    f = matmul(u, w_down, mode='nn', a_pre='relu2', name=f"{tag}_ffn_down")
    h_out = rms_fwd(f, gains[3], res=h, name=f"{tag}_ffn_out")
    return h_out, (h, xn, u, f)


def _ffn_bwd(dh, saved, gains, w_up, w_down, tag):
    h, xn, u, f = saved
    df, dg3 = rms_bwd(f, gains[3], dh, name=f"{tag}_ffn_out_bwd")
    du = matmul(df, w_down, mode='nt', epi='mul2relu', extra=u, name=f"{tag}_ffn_da")
    dw_down = matmul(u, df, mode='tn', a_pre='relu2', name=f"{tag}_ffn_dwdown")
    dw_up = matmul(xn, du, mode='tn', name=f"{tag}_ffn_dwup")
    dxn = matmul(du, w_up, mode='nt', name=f"{tag}_ffn_dxn")
    dh_in, dg2 = rms_bwd(h, gains[2], dxn, dres=dh, name=f"{tag}_ffn_norm_bwd")
    return dh_in, dg2, dg3, dw_up, dw_down


def _sb_layer_fwd(xn, w, j, tag):
    qkv = matmul(xn, w['sb_w_qkv'][j], mode='nn', out_dtype=BF16, name=f"{tag}_qkv")
    o, tot = sb_fwd(qkv, name=f"{tag}_sb")
    m = matmul(o, w['sb_w_o'][j], mode='nn', name=f"{tag}_wo")
    return m, (qkv, o, tot)


def _sb_layer_bwd(dm, xn, saved, w, j, tag):
    qkv, o, tot = saved
    do = matmul(dm, w['sb_w_o'][j], mode='nt', name=f"{tag}_do")
    dw_o = matmul(o, dm, mode='tn', name=f"{tag}_dwo")
    dq, dk, dv = sb_bwd(qkv, tot, do, name=f"{tag}_sb_bwd")
    dqkv = jnp.concatenate([dq, dk, dv], axis=1)
    dw_qkv = matmul(xn, dqkv, mode='tn', name=f"{tag}_dwqkv")
    dxn = matmul(dqkv, w['sb_w_qkv'][j], mode='nt', name=f"{tag}_dxn")
    return dxn, {('sb_w_qkv', j): dw_qkv, ('sb_w_o', j): dw_o}


def _conv_layer_fwd(xn, w, j, tag):
    bcu = matmul(xn, w['conv_w_in'][j], mode='nn', name=f"{tag}_in")
    y = conv_fwd(bcu, w['conv_w'][j], name=f"{tag}_conv")
    m = matmul(y, w['conv_w_out'][j], mode='nn', name=f"{tag}_out")
    return m, (bcu, y)


def _conv_layer_bwd(dm, xn, saved, w, j, tag):
    bcu, y = saved
    dy = matmul(dm, w['conv_w_out'][j], mode='nt', name=f"{tag}_dy")
    dw_out = matmul(y, dm, mode='tn', name=f"{tag}_dwout")
    db, dc, du, dw_conv = conv_bwd(bcu, w['conv_w'][j], dy, name=f"{tag}_conv_bwd")
    dbcu = jnp.concatenate([db, dc, du], axis=1)
    dw_in = matmul(xn, dbcu, mode='tn', name=f"{tag}_dwin")
    dxn = matmul(dbcu, w['conv_w_in'][j], mode='nt', name=f"{tag}_dxn")
    return dxn, {('conv_w_in', j): dw_in, ('conv_w', j): dw_conv, ('conv_w_out', j): dw_out}


def _gla_split(w_in, w_gate_up):
    d = w_in.shape[0]
    w_main = w_in[:, :3 * d]
    w_a = jnp.pad(w_in[:, 3 * d:], ((0, 0), (0, LANES - GLA_GATE_RANK)))
    w_gu = jnp.pad(w_gate_up, ((0, LANES - GLA_GATE_RANK), (0, 0)))
    return w_main, w_a, w_gu


def _gla_layer_fwd(xn, w, j, tag):
    w_main, w_a, w_gu = _gla_split(w['gla_w_in'][j], w['gla_w_gate_up'][j])
    proj = matmul(xn, w_main, mode='nn', name=f"{tag}_in")
    a_low = matmul(xn, w_a, mode='nn', name=f"{tag}_alow")
    lg = matmul(a_low, w_gu, mode='nn', epi='logsig16', extra=w['gla_b_gate'][j][None, :], name=f"{tag}_gate")
    o, states = gla_fwd(proj, lg, name=f"{tag}_gla")
    hn = w['gla_head_norm'][j].reshape(1, -1)
    y = gla_post_fwd(o, proj, hn, name=f"{tag}_post")
    m = matmul(y, w['gla_w_o'][j], mode='nn', name=f"{tag}_wo")
    return m, (proj, a_low, lg, o, states, y)


def _gla_layer_bwd(dm, xn, saved, w, j, tag):
    proj, a_low, lg, o, states, y = saved
    w_main, w_a, w_gu = _gla_split(w['gla_w_in'][j], w['gla_w_gate_up'][j])
    hn = w['gla_head_norm'][j].reshape(1, -1)
    dy = matmul(dm, w['gla_w_o'][j], mode='nt', name=f"{tag}_dy")
    dw_o = matmul(y, dm, mode='tn', name=f"{tag}_dwo")
    do, dg, dhn = gla_post_bwd(o, proj, hn, dy, name=f"{tag}_post_bwd")
    dq, dk, dv, dzg, dbg = gla_bwd(proj, lg, states, do, name=f"{tag}_gla_bwd")
    da_low = matmul(dzg, w_gu, mode='nt', name=f"{tag}_dalow")
    dw_gu = matmul(a_low, dzg, mode='tn', name=f"{tag}_dwgu")[:GLA_GATE_RANK]
    dproj = jnp.concatenate([dq, dk, dv, dg], axis=1)
    dw_main = matmul(xn, dproj, mode='tn', name=f"{tag}_dwin")
    dw_a = matmul(xn, da_low, mode='tn', name=f"{tag}_dwa")[:, :GLA_GATE_RANK]
    dxn_a = matmul(da_low, w_a, mode='nt', name=f"{tag}_dxn_a")
    dxn = matmul(dproj, w_main, mode='nt', epi='add', extra=dxn_a, name=f"{tag}_dxn")
    grads = {('gla_w_in', j): jnp.concatenate([dw_main, dw_a], axis=1), ('gla_w_gate_up', j): dw_gu,
             ('gla_b_gate', j): dbg[0], ('gla_head_norm', j): dhn.reshape(w['gla_head_norm'][j].shape),
             ('gla_w_o', j): dw_o}
    return dxn, grads


_MIXERS = ((_sb_layer_fwd, _sb_layer_bwd), (_conv_layer_fwd, _conv_layer_bwd), (_gla_layer_fwd, _gla_layer_bwd))


def local_step(x, w, target):
    depth = w['norm_gains'].shape[0]
    gains = [[w['norm_gains'][i, s][None, :] for s in range(4)] for i in range(depth)]
    h = x
    tape = []
    for i in range(depth):
        kind, j = i % 3, i // 3
        tag = f"l{i}"
        xn = rms_fwd(h, gains[i][0], name=f"{tag}_mix_norm", out_dtype=BF16)
        m, saved = _MIXERS[kind][0](xn, w, j, tag)
        h_mid = rms_fwd(m, gains[i][1], res=h, name=f"{tag}_mix_out")
        h_out, ffn_saved = _ffn_fwd(h_mid, gains[i], w['ffn_w_up'][i], w['ffn_w_down'][i], tag)
        tape.append((h, xn, m, saved, ffn_saved))
        h = h_out
    loss, dh = loss_head(h, target, name="loss_head")

    grads = {}
    dgains = [[None] * 4 for _ in range(depth)]
    for i in reversed(range(depth)):
        kind, j = i % 3, i // 3
        tag = f"l{i}"
        h_in, xn, m, saved, ffn_saved = tape[i]
        dh, dgains[i][2], dgains[i][3], grads[('ffn_w_up', i)], grads[('ffn_w_down', i)] = _ffn_bwd(
            dh, ffn_saved, gains[i], w['ffn_w_up'][i], w['ffn_w_down'][i], tag)
        dm, dgains[i][1] = rms_bwd(m, gains[i][1], dh, name=f"{tag}_mix_out_bwd")
        dxn, g = _MIXERS[kind][1](dm, xn, saved, w, j, tag)
        grads.update(g)
        dh, dgains[i][0] = rms_bwd(h_in, gains[i][0], dxn, dres=dh, name=f"{tag}_mix_norm_bwd")

    full = {'norm_gains': jnp.stack([jnp.concatenate(row, axis=0) for row in dgains])}
    for name in WEIGHTS[1:]:
        full[name] = jnp.stack([grads[(name, j)] for j in range(w[name].shape[0])])
    return loss, dh, full


def _segments(shard_shapes, f32_as_pairs):
    segs, row = {}, 0
    for name in WEIGHTS:
        n = math.prod(shard_shapes[name]) * (2 if f32_as_pairs and name in F32_PAYLOAD else 1)
        nrows = -(-n // (PACK_COLS * PACK_ROW_ALIGN)) * PACK_ROW_ALIGN
        segs[name] = (row, nrows, n)
        row += nrows
    return segs, -(-row // PACK_ROW_BLOCK) * PACK_ROW_BLOCK


def _pack(parts, segs, total_rows, dtype):
    pieces, row = [], 0
    for name in WEIGHTS:
        _, nrows, n = segs[name]
        p = parts[name].astype(dtype)
        lead = p.shape[:-1]
        p = jnp.pad(p, [(0, 0)] * len(lead) + [(0, nrows * PACK_COLS - n)])
        pieces.append(p.reshape(lead + (nrows, PACK_COLS)))
        row += nrows
    if total_rows > row:
        pieces.append(jnp.zeros(lead + (total_rows - row, PACK_COLS), dtype))
    return jnp.concatenate(pieces, axis=-2)


def _unpack(buf, seg):
    first, nrows, n = seg
    piece = buf[..., first:first + nrows, :]
    return piece.reshape(piece.shape[:-2] + (nrows * PACK_COLS,))[..., :n]


def _unshard(gathered, axis):
    moved = jnp.moveaxis(gathered, 0, axis)
    shape = moved.shape
    return moved.reshape(shape[:axis] + (shape[axis] * shape[axis + 1],) + shape[axis + 2:])


def _shard_split(full, axis):
    shape = full.shape
    cut = full.reshape(shape[:axis] + (N_DEV, shape[axis] // N_DEV) + shape[axis + 1:])
    return jnp.moveaxis(cut, axis, 0)


def _mesh_position():
    return lax.axis_index("x"), lax.axis_index("y"), lax.axis_index("c")


def all_gather(shard, *, name):
    rows, cols = shard.shape

    def body(x_ref, out_ref, send_sems, recv_sems, local_sem):
        x, y, c = _mesh_position()
        me, sibling = (x, y, c), (x, y, 1 - c)
        chips = [(1 - x, y), (x, 1 - y), (1 - x, 1 - y)]

        def block(px, py, pc):
            return out_ref.at[pl.ds((4 * px + 2 * py + pc) * rows, rows), :]

        def copy(k, blk, to, src=None):
            return pltpu.make_async_remote_copy(
                src_ref=block(*blk) if src is None else src, dst_ref=block(*blk),
                send_sem=send_sems.at[k], recv_sem=recv_sems.at[k],
                device_id=to, device_id_type=pl.DeviceIdType.MESH)

        mine = pltpu.make_async_copy(x_ref, block(*me), local_sem)
        mine.start()
        first = [copy(0, me, sibling, src=x_ref)]
        first += [copy(1 + j, me, (*chip, c), src=x_ref) for j, chip in enumerate(chips)]
        for cp in first:
            cp.start()
        passed = [copy(4 + j, (*chip, c), sibling) for j, chip in enumerate(chips)]
        for j, chip in enumerate(chips):
            copy(1 + j, (*chip, c), me).wait_recv()
            passed[j].start()
        copy(0, sibling, me).wait_recv()
        for j, chip in enumerate(chips):
            copy(4 + j, (*chip, 1 - c), me).wait_recv()
        for cp in first + passed:
            cp.wait_send()
        mine.wait()

    return pl.pallas_call(
        body, name=name,
        out_shape=jax.ShapeDtypeStruct((N_DEV * rows, cols), shard.dtype),
        in_specs=[pl.BlockSpec(memory_space=pl.ANY)], out_specs=pl.BlockSpec(memory_space=pl.ANY),
        scratch_shapes=[pltpu.SemaphoreType.DMA((7,)), pltpu.SemaphoreType.DMA((7,)), pltpu.SemaphoreType.DMA],
    )(shard)


def exchange_shards(parts, *, name):
    _, rows, cols = parts.shape

    def body(in_ref, out_ref, send_sems, recv_sems, local_sem):
        x, y, c = _mesh_position()
        my_id = 4 * x + 2 * y + c
        mine = pltpu.make_async_copy(in_ref.at[my_id], out_ref.at[my_id], local_sem)
        mine.start()
        copies = []
        for k in range(1, N_DEV):
            px = 1 - x if k & 4 else x
            py = 1 - y if k & 2 else y
            pc = 1 - c if k & 1 else c
            peer_id = 4 * px + 2 * py + pc
            copies.append(pltpu.make_async_remote_copy(
                src_ref=in_ref.at[peer_id], dst_ref=out_ref.at[my_id],
                send_sem=send_sems.at[k - 1], recv_sem=recv_sems.at[k - 1],
                device_id=(px, py, pc), device_id_type=pl.DeviceIdType.MESH))
        for cp in copies:
            cp.start()
        for k in range(1, N_DEV):
            px = 1 - x if k & 4 else x
            py = 1 - y if k & 2 else y
            pc = 1 - c if k & 1 else c
            peer_id = 4 * px + 2 * py + pc
            pltpu.make_async_remote_copy(
                src_ref=in_ref.at[peer_id], dst_ref=out_ref.at[peer_id],
                send_sem=send_sems.at[k - 1], recv_sem=recv_sems.at[k - 1],
                device_id=(px, py, pc), device_id_type=pl.DeviceIdType.MESH).wait_recv()
        for cp in copies:
            cp.wait_send()
        mine.wait()

    return pl.pallas_call(
        body, name=name, out_shape=jax.ShapeDtypeStruct(parts.shape, parts.dtype),
        in_specs=[pl.BlockSpec(memory_space=pl.ANY)], out_specs=pl.BlockSpec(memory_space=pl.ANY),
        scratch_shapes=[pltpu.SemaphoreType.DMA((7,)), pltpu.SemaphoreType.DMA((7,)), pltpu.SemaphoreType.DMA],
    )(parts)


def adamw(parts, w, m, v, *, name):
    _, rows, cols = parts.shape
    tr = PACK_ROW_BLOCK
    c1 = 1.0 - ADAM_B1 ** ADAM_STEP
    c2 = 1.0 - ADAM_B2 ** ADAM_STEP

    def body(p_ref, w_ref, m_ref, v_ref, g_ref, d_ref, nm_ref, nv_ref):
        g = p_ref[0].astype(F32)
        for s in range(1, N_DEV):
            g = g + p_ref[s].astype(F32)
        nm = ADAM_B1 * m_ref[...] + (1.0 - ADAM_B1) * g
        nv = ADAM_B2 * v_ref[...] + (1.0 - ADAM_B2) * jnp.square(g)
        m_hat = nm / c1
        v_hat = nv / c2
        g_ref[...] = g
        d_ref[...] = -ADAM_LR * (m_hat / (jnp.sqrt(v_hat) + ADAM_EPS) + ADAM_WD * w_ref[...])
        nm_ref[...] = nm
        nv_ref[...] = nv

    row = pl.BlockSpec((tr, cols), lambda i: (i, 0))
    shape = jax.ShapeDtypeStruct((rows, cols), F32)
    return pl.pallas_call(
        body, name=name, grid=(rows // tr,),
        in_specs=[pl.BlockSpec((N_DEV, tr, cols), lambda i: (0, i, 0)), row, row, row],
        out_specs=[row, row, row, row], out_shape=[shape, shape, shape, shape],
        compiler_params=_params("parallel"),
    )(parts, w, m, v)


def kernel(x, norm_gains, sb_w_qkv, sb_w_o, conv_w_in, conv_w, conv_w_out, gla_w_in, gla_w_gate_up, gla_b_gate, gla_head_norm, gla_w_o, ffn_w_up, ffn_w_down, loss_target, m_norm_gains, m_sb_w_qkv, m_sb_w_o, m_conv_w_in, m_conv_w, m_conv_w_out, m_gla_w_in, m_gla_w_gate_up, m_gla_b_gate, m_gla_head_norm, m_gla_w_o, m_ffn_w_up, m_ffn_w_down, v_norm_gains, v_sb_w_qkv, v_sb_w_o, v_conv_w_in, v_conv_w, v_conv_w_out, v_gla_w_in, v_gla_w_gate_up, v_gla_b_gate, v_gla_head_norm, v_gla_w_o, v_ffn_w_up, v_ffn_w_down):
    shards = dict(zip(WEIGHTS, (norm_gains, sb_w_qkv, sb_w_o, conv_w_in, conv_w, conv_w_out, gla_w_in,
                                gla_w_gate_up, gla_b_gate, gla_head_norm, gla_w_o, ffn_w_up, ffn_w_down)))
    moments_m = dict(zip(WEIGHTS, (m_norm_gains, m_sb_w_qkv, m_sb_w_o, m_conv_w_in, m_conv_w, m_conv_w_out,
                                   m_gla_w_in, m_gla_w_gate_up, m_gla_b_gate, m_gla_head_norm, m_gla_w_o,
                                   m_ffn_w_up, m_ffn_w_down)))
    moments_v = dict(zip(WEIGHTS, (v_norm_gains, v_sb_w_qkv, v_sb_w_o, v_conv_w_in, v_conv_w, v_conv_w_out,
                                   v_gla_w_in, v_gla_w_gate_up, v_gla_b_gate, v_gla_head_norm, v_gla_w_o,
                                   v_ffn_w_up, v_ffn_w_down)))
    shard_shapes = {n: a.shape for n, a in shards.items()}

    segs, rows = _segments(shard_shapes, True)
    payload = _pack({n: (lax.bitcast_convert_type(a, BF16) if n in F32_PAYLOAD else a.astype(BF16)).reshape(-1)
                     for n, a in shards.items()}, segs, rows, BF16)
    gathered = all_gather(payload, name="weights_all_gather").reshape(N_DEV, rows, PACK_COLS)
    whole = {}
    for n in WEIGHTS:
        piece = _unpack(gathered, segs[n])
        if n in F32_PAYLOAD:
            piece = lax.bitcast_convert_type(piece.reshape(N_DEV, -1, 2), F32)
        whole[n] = _unshard(piece.reshape((N_DEV,) + shard_shapes[n]), SHARD_AXIS[n])

    loss, grad_x, grads = local_step(x[0], whole, loss_target[0])
    loss = lax.psum(loss[0, 0], ("x", "y", "c"))

    gsegs, grows = _segments(shard_shapes, False)
    parts = _pack({n: _shard_split(grads[n], SHARD_AXIS[n]).reshape(N_DEV, -1) for n in WEIGHTS}, gsegs, grows, BF16)
    received = exchange_shards(parts, name="grads_exchange")

    def packed(group):
        return _pack({n: group[n].reshape(-1) for n in WEIGHTS}, gsegs, grows, F32)

    outs = adamw(received, packed(shards), packed(moments_m), packed(moments_v), name="adamw")
    unpacked = [[_unpack(buf, gsegs[n]).reshape(shard_shapes[n]) for n in WEIGHTS] for buf in outs]
    return (loss, grad_x[None], *unpacked[0], *unpacked[1], *unpacked[2], *unpacked[3])
```

```python
import functools
import math

import jax
import jax.numpy as jnp
from jax import lax
from jax.experimental import pallas as pl
from jax.experimental.pallas import tpu as pltpu

F32 = jnp.float32
BF16 = jnp.bfloat16

N_DEV = 8
SB_HEADS = 16
GLA_HEADS = 4
GLA_CHUNK = 64
GLA_GATE_RANK = 16
GLA_GATE_NORMALIZER = 16.0
CONV_WIDTH = 3
DEPTH = 4
RMS_EPS = 1e-6
ADAM_LR = 0.001
ADAM_B1 = 0.9
ADAM_B2 = 0.999
ADAM_EPS = 1e-08
ADAM_WD = 0.01
ADAM_STEP = 10

LANES = 128
SB_BLOCK = 256
VMEM_LIMIT_BYTES = 56 * 1024 * 1024
MM_TM, MM_TN, MM_TK = 512, 1024, 1024
PACK_COLS = 1024
PACK_ROW_ALIGN = 16
PACK_ROW_BLOCK = 128

WEIGHTS = ['norm_gains', 'sb_w_qkv', 'sb_w_o', 'conv_w_in', 'conv_w', 'conv_w_out', 'gla_w_in',
           'gla_w_gate_up', 'gla_b_gate', 'gla_head_norm', 'gla_w_o', 'ffn_w_up', 'ffn_w_down']
SHARD_AXIS = {'norm_gains': 2, 'sb_w_qkv': 2, 'sb_w_o': 1, 'conv_w_in': 2, 'conv_w': 2, 'conv_w_out': 1,
              'gla_w_in': 2, 'gla_w_gate_up': 2, 'gla_b_gate': 1, 'gla_head_norm': 2, 'gla_w_o': 1,
              'ffn_w_up': 2, 'ffn_w_down': 1}
F32_PAYLOAD = ('norm_gains', 'conv_w', 'gla_b_gate', 'gla_head_norm')

_NN = (((1,), (0,)), ((), ()))
_NT = (((1,), (1,)), ((), ()))
_TN = (((0,), (0,)), ((), ()))
_DIMS = {'nn': _NN, 'nt': _NT, 'tn': _TN}


def _params(*semantics):
    return pltpu.CompilerParams(dimension_semantics=semantics, vmem_limit_bytes=VMEM_LIMIT_BYTES)


def _dot(a, b, dims=_NN):
    return lax.dot_general(a.astype(BF16), b.astype(BF16), dims, preferred_element_type=F32)


def _split_hi_lo(x):
    hi = x.astype(BF16)
    lo = (x - hi.astype(F32)).astype(BF16)
    return hi, lo


def _dot_exact_rhs(x, ones_mat, dims=_NN):
    hi, lo = _split_hi_lo(x)
    return (lax.dot_general(hi, ones_mat, dims, preferred_element_type=F32)
            + lax.dot_general(lo, ones_mat, dims, preferred_element_type=F32))


def _dot_exact_lhs(ones_mat, x, dims=_NN):
    hi, lo = _split_hi_lo(x)
    return (lax.dot_general(ones_mat, hi, dims, preferred_element_type=F32)
            + lax.dot_general(ones_mat, lo, dims, preferred_element_type=F32))


def _log_sigmoid(z):
    return jnp.minimum(z, 0.0) - jnp.log(1.0 + jnp.exp(-jnp.abs(z)))


def _sigmoid(z):
    return 1.0 / (1.0 + jnp.exp(-z))


def matmul(a, b, *, mode, name, out_dtype=F32, epi=None, extra=None, tm=MM_TM, tn=MM_TN, tk=MM_TK):
    if mode == 'nn':
        (m, k), (k2, n) = a.shape, b.shape
    elif mode == 'nt':
        (m, k), (n, k2) = a.shape, b.shape
    else:
        (k, m), (k2, n) = a.shape, b.shape
    assert k == k2, (a.shape, b.shape, mode)
    tm, tn, tk = min(tm, m), min(tn, n), min(tk, k)
    assert m % tm == 0 and n % tn == 0 and k % tk == 0, (a.shape, b.shape, mode)
    nk = k // tk
    if mode == 'tn':
        a_spec = pl.BlockSpec((tk, tm), lambda i, j, kk: (kk, i))
    else:
        a_spec = pl.BlockSpec((tm, tk), lambda i, j, kk: (i, kk))
    if mode == 'nt':
        b_spec = pl.BlockSpec((tn, tk), lambda i, j, kk: (j, kk))
    else:
        b_spec = pl.BlockSpec((tk, tn), lambda i, j, kk: (kk, j))
    in_specs, operands = [a_spec, b_spec], [a, b]
    if epi == 'logsig16':
        in_specs.append(pl.BlockSpec((1, tn), lambda i, j, kk: (0, j)))
        operands.append(extra)
    elif epi in ('mul2relu', 'add'):
        in_specs.append(pl.BlockSpec((tm, tn), lambda i, j, kk: (i, j)))
        operands.append(extra)

    n_extra = len(operands) - 2
    pair = epi == 'relu2_pair'

    def body(a_ref, b_ref, *rest):
        e_ref = rest[0] if n_extra else None
        outs = rest[n_extra:n_extra + (2 if pair else 1)]

        def finish(r):
            if epi == 'mul2relu':
                r = r * (2.0 * jnp.maximum(e_ref[...], 0.0))
            elif epi == 'add':
                r = r + e_ref[...]
            elif epi == 'logsig16':
                r = _log_sigmoid(r + e_ref[...]) / GLA_GATE_NORMALIZER
            outs[0][...] = r.astype(outs[0].dtype)
            if pair:
                outs[1][...] = jnp.square(jnp.maximum(r, 0.0)).astype(outs[1].dtype)

        part = _dot(a_ref[...], b_ref[...], _DIMS[mode])
        if nk == 1:
            finish(part)
        else:
            acc_ref = rest[-1]
            kk = pl.program_id(2)

            @pl.when(kk == 0)
            def _():
                acc_ref[...] = part

            @pl.when(kk > 0)
            def _():
                acc_ref[...] += part

            @pl.when(kk == nk - 1)
            def _():
                finish(acc_ref[...])

    tile = pl.BlockSpec((tm, tn), lambda i, j, kk: (i, j))
    shape = jax.ShapeDtypeStruct((m, n), out_dtype)
    return pl.pallas_call(
        body, name=name, grid=(m // tm, n // tn, nk), in_specs=in_specs,
        out_specs=[tile, tile] if pair else tile,
        out_shape=[shape, jax.ShapeDtypeStruct((m, n), BF16)] if pair else shape,
        scratch_shapes=[pltpu.VMEM((tm, tn), F32)] if nk > 1 else [],
        compiler_params=_params("parallel", "parallel", "arbitrary"),
    )(*operands)


def rms_fwd(x, gain, *, name, res=None, out_dtype=F32, tm=512):
    t, d = x.shape
    tm = min(tm, t)
    row = pl.BlockSpec((tm, d), lambda i: (i, 0))
    in_specs, operands = [row, pl.BlockSpec((1, d), lambda i: (0, 0))], [x, gain]
    if res is not None:
        in_specs.append(row)
        operands.append(res)

    def body(x_ref, g_ref, *rest):
        xv = x_ref[...]
        r = lax.rsqrt(jnp.mean(xv * xv, axis=-1, keepdims=True) + RMS_EPS)
        y = xv * r * g_ref[...]
        if res is not None:
            y = rest[0][...] + y
        rest[-1][...] = y.astype(out_dtype)

    return pl.pallas_call(
        body, name=name, grid=(t // tm,), in_specs=in_specs, out_specs=row,
        out_shape=jax.ShapeDtypeStruct((t, d), out_dtype), compiler_params=_params("parallel"),
    )(*operands)


def rms_bwd(x, gain, dy, *, name, dres=None, out_dtype=F32, tm=512):
    t, d = x.shape
    tm = min(tm, t)
    row = pl.BlockSpec((tm, d), lambda i: (i, 0))
    vec = pl.BlockSpec((1, d), lambda i: (0, 0))
    in_specs, operands = [row, vec, row], [x, gain, dy]
    if dres is not None:
        in_specs.append(row)
        operands.append(dres)

    def body(x_ref, g_ref, dy_ref, *rest):
        dx_ref, dg_ref = rest[-2], rest[-1]

        @pl.when(pl.program_id(0) == 0)
        def _():
            dg_ref[...] = jnp.zeros_like(dg_ref)

        xv, dyv = x_ref[...], dy_ref[...]
        r = lax.rsqrt(jnp.mean(xv * xv, axis=-1, keepdims=True) + RMS_EPS)
        u = dyv * g_ref[...]
        dx = r * u - xv * (r * r * r * jnp.mean(u * xv, axis=-1, keepdims=True))
        if dres is not None:
            dx = rest[0][...] + dx
        dx_ref[...] = dx.astype(out_dtype)
        dg_ref[...] += jnp.sum(dyv * xv * r, axis=0, keepdims=True)

    return pl.pallas_call(
        body, name=name, grid=(t // tm,), in_specs=in_specs, out_specs=[row, vec],
        out_shape=[jax.ShapeDtypeStruct((t, d), out_dtype), jax.ShapeDtypeStruct((1, d), F32)],
        compiler_params=_params("arbitrary"),
    )(*operands)


def loss_head(y, target, *, name, tm=512):
    t, d = y.shape
    tm = min(tm, t)
    nt = t // tm
    row = pl.BlockSpec((tm, d), lambda i: (i, 0))

    def body(y_ref, t_ref, loss_ref, dy_ref, acc_ref):
        i = pl.program_id(0)

        @pl.when(i == 0)
        def _():
            acc_ref[...] = jnp.zeros_like(acc_ref)

        err = y_ref[...] - t_ref[...]
        dy_ref[...] = err * (1.0 / d)
        acc_ref[...] += jnp.sum(err * err, axis=0, keepdims=True)

        @pl.when(i == nt - 1)
        def _():
            loss_ref[...] = jnp.sum(acc_ref[...], axis=1, keepdims=True) * (0.5 / d)

    return pl.pallas_call(
        body, name=name, grid=(nt,), in_specs=[row, row],
        out_specs=[pl.BlockSpec((1, 1), lambda i: (0, 0)), row],
        out_shape=[jax.ShapeDtypeStruct((1, 1), F32), jax.ShapeDtypeStruct((t, d), F32)],
        scratch_shapes=[pltpu.VMEM((1, d), F32)], compiler_params=_params("arbitrary"),
    )(y, target)


def _sb_masks(qi, kb):
    rows = lax.broadcasted_iota(jnp.int32, (SB_BLOCK, SB_BLOCK), 0)
    cols = lax.broadcasted_iota(jnp.int32, (SB_BLOCK, SB_BLOCK), 1)
    return (kb * SB_BLOCK + cols) < (qi * SB_BLOCK + rows)


def _sb_tri(strict):
    r = lax.broadcasted_iota(jnp.int32, (SB_BLOCK, SB_BLOCK), 0)
    c = lax.broadcasted_iota(jnp.int32, (SB_BLOCK, SB_BLOCK), 1)
    return jnp.where((r > c) if strict else (r >= c), 1.0, 0.0).astype(BF16)


def _sb_logits(q_h, k_blk, mask, scale):
    z = _dot(q_h, k_blk, _NT) * scale
    ls = _log_sigmoid(z)
    lm = ls - z
    lmm = jnp.where(mask, lm, 0.0)
    return ls, lm, lmm, jnp.sum(lmm, axis=1, keepdims=True)


def _sb_weights(ls, lmm, mask, tri_strict, later):
    suffix = _dot_exact_rhs(lmm, tri_strict)
    return jnp.where(mask, jnp.exp(ls + suffix + later), 0.0)


def sb_fwd(qkv, *, name):
    t, d3 = qkv.shape
    d = d3 // 3
    head_dim = d // SB_HEADS
    assert 2 * head_dim == LANES and t % SB_BLOCK == 0
    pairs = d // LANES
    nq = t // SB_BLOCK
    scale = head_dim ** -0.5

    def body(q_ref, k_ref, v_ref, o_ref, tot_ref):
        qi = pl.program_id(1)
        lane = lax.broadcasted_iota(jnp.int32, (SB_BLOCK, LANES), 1)
        first = lane < head_dim
        q = q_ref[...]
        qs = [jnp.where(first, q, jnp.zeros_like(q)), jnp.where(first, jnp.zeros_like(q), q)]
        tri = _sb_tri(True)

        def step(i, carry):
            kb = qi - i
            ks = pl.multiple_of(kb * SB_BLOCK, SB_BLOCK)
            k_blk = k_ref[pl.ds(ks, SB_BLOCK), :]
            v_blk = v_ref[pl.ds(ks, SB_BLOCK), :]
            mask = _sb_masks(qi, kb)
            new = []
            for h in range(2):
                acc, later = carry[2 * h], carry[2 * h + 1]
                ls, _, lmm, row = _sb_logits(qs[h], k_blk, mask, scale)
                w = _sb_weights(ls, lmm, mask, tri, later)
                new += [acc + _dot(w, v_blk), later + row]
            return tuple(new)

        zero = jnp.zeros((SB_BLOCK, LANES), F32)
        zcol = jnp.zeros((SB_BLOCK, 1), F32)
        out = lax.fori_loop(0, qi + 1, step, (zero, zcol, zero, zcol))
        o_ref[...] = jnp.where(first, out[0], out[2]).astype(o_ref.dtype)
        tot_ref[...] = jnp.where(first, out[1], out[3])

    blk = lambda off: pl.BlockSpec((t, LANES), lambda p, i: (0, off + p))
    qblk = pl.BlockSpec((SB_BLOCK, LANES), lambda p, i: (i, p))
    return pl.pallas_call(
        body, name=name, grid=(pairs, nq), in_specs=[qblk, blk(pairs), blk(2 * pairs)],
        out_specs=[qblk, qblk],
        out_shape=[jax.ShapeDtypeStruct((t, d), BF16), jax.ShapeDtypeStruct((t, d), F32)],
        compiler_params=_params("parallel", "arbitrary"),
    )(qkv, qkv, qkv)


def sb_bwd(qkv, tot, do, *, name):
    t, d3 = qkv.shape
    d = d3 // 3
    head_dim = d // SB_HEADS
    pairs = d // LANES
    nq = t // SB_BLOCK
    scale = head_dim ** -0.5

    def body(q_ref, k_ref, v_ref, tot_ref, do_ref, dq_ref, dk_ref, dv_ref, dk_sum, dv_sum):
        qi = pl.program_id(1)

        @pl.when(qi == 0)
        def _():
            dk_sum[...] = jnp.zeros_like(dk_sum)
            dv_sum[...] = jnp.zeros_like(dv_sum)

        lane = lax.broadcasted_iota(jnp.int32, (SB_BLOCK, LANES), 1)
        first = lane < head_dim
        q, dov, totv = q_ref[...], do_ref[...], tot_ref[...]
        sel = [first, jnp.logical_not(first)]
        qs = [jnp.where(s, q, jnp.zeros_like(q)) for s in sel]
        dos = [jnp.where(s, dov, 0.0) for s in sel]
        tots = [totv[:, 0:1], totv[:, head_dim:head_dim + 1]]
        tri_strict = _sb_tri(True)
        rows = lax.broadcasted_iota(jnp.int32, (SB_BLOCK, SB_BLOCK), 0)
        cols = lax.broadcasted_iota(jnp.int32, (SB_BLOCK, SB_BLOCK), 1)
        tri_before = jnp.where(rows < cols, 1.0, 0.0).astype(BF16)

        def step(kb, carry):
            ks = pl.multiple_of(kb * SB_BLOCK, SB_BLOCK)
            k_blk = k_ref[pl.ds(ks, SB_BLOCK), :]
            v_blk = v_ref[pl.ds(ks, SB_BLOCK), :]
            mask = _sb_masks(qi, kb)
            new = []
            dk_acc = jnp.zeros((SB_BLOCK, LANES), F32)
            dv_acc = jnp.zeros((SB_BLOCK, LANES), F32)
            for h in range(2):
                dq, seen, before = carry[3 * h], carry[3 * h + 1], carry[3 * h + 2]
                ls, lm, lmm, row = _sb_logits(qs[h], k_blk, mask, scale)
                seen = seen + row
                w = _sb_weights(ls, lmm, mask, tri_strict, tots[h] - seen)
                da = _dot(dos[h], v_blk, _NT) * w
                g = _dot_exact_rhs(da, tri_before) + before
                dz = jnp.where(mask, da * jnp.exp(lm) - g * jnp.exp(ls), 0.0) * scale
                dq = dq + _dot(dz, k_blk)
                dk_acc = dk_acc + _dot(dz, qs[h], _TN)
                dv_acc = dv_acc + _dot(w, dos[h], _TN)
                new += [dq, seen, before + jnp.sum(da, axis=1, keepdims=True)]
            dk_sum[pl.ds(ks, SB_BLOCK), :] += dk_acc
            dv_sum[pl.ds(ks, SB_BLOCK), :] += dv_acc
            return tuple(new)

        zero = jnp.zeros((SB_BLOCK, LANES), F32)
        zcol = jnp.zeros((SB_BLOCK, 1), F32)
        out = lax.fori_loop(0, qi + 1, step, (zero, zcol, zcol, zero, zcol, zcol))
        dq_ref[...] = jnp.where(first, out[0], out[3]).astype(dq_ref.dtype)

        @pl.when(qi == nq - 1)
        def _():
            dk_ref[...] = dk_sum[...].astype(dk_ref.dtype)
            dv_ref[...] = dv_sum[...].astype(dv_ref.dtype)

    qblk = pl.BlockSpec((SB_BLOCK, LANES), lambda p, i: (i, p))
    col = lambda off: pl.BlockSpec((t, LANES), lambda p, i: (0, off + p))
    shape = jax.ShapeDtypeStruct((t, d), BF16)
    return pl.pallas_call(
        body, name=name, grid=(pairs, nq),
        in_specs=[qblk, col(pairs), col(2 * pairs), qblk, qblk],
        out_specs=[qblk, col(0), col(0)], out_shape=[shape, shape, shape],
        scratch_shapes=[pltpu.VMEM((t, LANES), F32), pltpu.VMEM((t, LANES), F32)],
        compiler_params=_params("parallel", "arbitrary"),
    )(qkv, qkv, qkv, tot, do)


def _shift_down(x, s):
    rows = lax.broadcasted_iota(jnp.int32, x.shape, 0)
    return jnp.where(rows >= s, pltpu.roll(x, s, 0), 0.0)


def _shift_up(x, s):
    t = x.shape[0]
    rows = lax.broadcasted_iota(jnp.int32, x.shape, 0)
    return jnp.where(rows < t - s, pltpu.roll(x, t - s, 0), 0.0)


def conv_fwd(bcu, w, *, name):
    t, d3 = bcu.shape
    d = d3 // 3
    nb = d // LANES
    col = lambda off: pl.BlockSpec((t, LANES), lambda j: (0, off + j))

    def body(b_ref, c_ref, u_ref, w_ref, y_ref):
        hh = c_ref[...] * u_ref[...]
        conv = w_ref[0:1, :] * _shift_down(hh, 2) + w_ref[1:2, :] * _shift_down(hh, 1) + w_ref[2:3, :] * hh
        y_ref[...] = (b_ref[...] * conv).astype(y_ref.dtype)

    return pl.pallas_call(
        body, name=name, grid=(nb,),
        in_specs=[col(0), col(nb), col(2 * nb), pl.BlockSpec((CONV_WIDTH, LANES), lambda j: (0, j))],
        out_specs=col(0), out_shape=jax.ShapeDtypeStruct((t, d), BF16), compiler_params=_params("parallel"),
    )(bcu, bcu, bcu, w)


def conv_bwd(bcu, w, dy, *, name):
    t, d3 = bcu.shape
    d = d3 // 3
    nb = d // LANES
    col = lambda off: pl.BlockSpec((t, LANES), lambda j: (0, off + j))
    wspec = pl.BlockSpec((CONV_WIDTH, LANES), lambda j: (0, j))

    def body(b_ref, c_ref, u_ref, w_ref, dy_ref, db_ref, dc_ref, du_ref, dw_ref):
        c, u, dyv = c_ref[...], u_ref[...], dy_ref[...]
        hh = c * u
        h2, h1 = _shift_down(hh, 2), _shift_down(hh, 1)
        w0, w1, w2 = w_ref[0:1, :], w_ref[1:2, :], w_ref[2:3, :]
        db_ref[...] = (dyv * (w0 * h2 + w1 * h1 + w2 * hh)).astype(db_ref.dtype)
        dconv = dyv * b_ref[...]
        dhh = w2 * dconv + w1 * _shift_up(dconv, 1) + w0 * _shift_up(dconv, 2)
        dc_ref[...] = (dhh * u).astype(dc_ref.dtype)
        du_ref[...] = (dhh * c).astype(du_ref.dtype)
        dw_ref[0:1, :] = jnp.sum(dconv * h2, axis=0, keepdims=True)
        dw_ref[1:2, :] = jnp.sum(dconv * h1, axis=0, keepdims=True)
        dw_ref[2:3, :] = jnp.sum(dconv * hh, axis=0, keepdims=True)

    shape = jax.ShapeDtypeStruct((t, d), BF16)
    return pl.pallas_call(
        body, name=name, grid=(nb,),
        in_specs=[col(0), col(nb), col(2 * nb), wspec, col(0)],
        out_specs=[col(0), col(0), col(0), wspec],
        out_shape=[shape, shape, shape, jax.ShapeDtypeStruct((CONV_WIDTH, d), F32)],
        compiler_params=_params("parallel"),
    )(bcu, bcu, bcu, w, dy)


def _gla_chunk(q_ref, k_ref, lg_ref, scale):
    c = GLA_CHUNK
    rows = lax.broadcasted_iota(jnp.int32, (c, c), 0)
    cols = lax.broadcasted_iota(jnp.int32, (c, c), 1)
    causal = rows >= cols
    tril = jnp.where(causal, 1.0, 0.0).astype(BF16)
    q = q_ref[...] * scale
    k = k_ref[...]
    cum = _dot_exact_lhs(tril, lg_ref[...])
    last = cum[c - 1:c, :]
    eq = jnp.exp(cum)
    ek = jnp.exp(-cum)
    el = jnp.exp(last - cum)
    return causal, tril, q, k, cum, last, eq, ek, el


def gla_fwd(proj, lg, *, name):
    t, d3 = proj.shape
    d = d3 // 3
    dk, dv = d // 2 // GLA_HEADS, d // GLA_HEADS
    assert dk == LANES and dv == 2 * LANES
    c = GLA_CHUNK
    nc = t // c
    scale = dk ** -0.5
    nh = GLA_HEADS

    def body(q_ref, k_ref, v_ref, lg_ref, o_ref, st_out_ref, st_ref):
        @pl.when(pl.program_id(1) == 0)
        def _():
            st_ref[...] = jnp.zeros_like(st_ref)

        causal, _, q, k, _, last, eq, ek, el = _gla_chunk(q_ref, k_ref, lg_ref, scale)
        v = v_ref[...]
        st = st_ref[...]
        st_out_ref[...] = st
        qt = q * eq
        scores = jnp.where(causal, _dot(qt, k * ek, _NT), 0.0)
        o_ref[...] = _dot(qt, st, _NT) + _dot(scores, v)
        st_ref[...] = st * jnp.exp(last) + _dot(v, k * el, _TN)

    return pl.pallas_call(
        body, name=name, grid=(nh, nc),
        in_specs=[pl.BlockSpec((c, dk), lambda h, i: (i, h)), pl.BlockSpec((c, dk), lambda h, i: (i, nh + h)),
                  pl.BlockSpec((c, dv), lambda h, i: (i, nh + h)), pl.BlockSpec((c, dk), lambda h, i: (i, h))],
        out_specs=[pl.BlockSpec((c, dv), lambda h, i: (i, h)),
                   pl.BlockSpec((None, None, dv, dk), lambda h, i: (h, i, 0, 0))],
        out_shape=[jax.ShapeDtypeStruct((t, d), F32), jax.ShapeDtypeStruct((nh, nc, dv, dk), F32)],
        scratch_shapes=[pltpu.VMEM((dv, dk), F32)], compiler_params=_params("parallel", "arbitrary"),
    )(proj, proj, proj, lg)


def gla_bwd(proj, lg, states, do, *, name):
    t, d3 = proj.shape
    d = d3 // 3
    dk, dv = d // 2 // GLA_HEADS, d // GLA_HEADS
    c = GLA_CHUNK
    nc = t // c
    scale = dk ** -0.5
    nh = GLA_HEADS

    def body(q_ref, k_ref, v_ref, lg_ref, st_ref, do_ref, dq_ref, dk_ref, dv_ref, dzg_ref, dbg_ref, dst_ref):
        @pl.when(pl.program_id(1) == 0)
        def _():
            dst_ref[...] = jnp.zeros_like(dst_ref)
            dbg_ref[...] = jnp.zeros_like(dbg_ref)

        causal, tril, q, k, _, last, eq, ek, el = _gla_chunk(q_ref, k_ref, lg_ref, scale)
        v, st, dov, dst = v_ref[...], st_ref[...], do_ref[...], dst_ref[...]
        qt, kt, kh = q * eq, k * ek, k * el
        scores = jnp.where(causal, _dot(qt, kt, _NT), 0.0)
        dscores = jnp.where(causal, _dot(dov, v, _NT), 0.0)
        dqt = _dot(dov, st) + _dot(dscores, kt)
        dkt = _dot(dscores, qt, _TN)
        dkh = _dot(v, dst)
        dv_ref[...] = (_dot(scores, dov, _TN) + _dot(kh, dst, _NT)).astype(dv_ref.dtype)
        dq_ref[...] = (dqt * eq * scale).astype(dq_ref.dtype)
        dk_ref[...] = (dkt * ek + dkh * el).astype(dk_ref.dtype)
        kh_dkh = kh * dkh
        e_last = jnp.exp(last)
        dlast = jnp.sum(kh_dkh, axis=0, keepdims=True) + e_last * jnp.sum(dst * st, axis=0, keepdims=True)
        dcum = qt * dqt - kt * dkt - kh_dkh
        dlg = _dot_exact_lhs(tril, dcum, _TN) + dlast
        lgv = lg_ref[...]
        dzg = dlg * (1.0 - jnp.exp(lgv * GLA_GATE_NORMALIZER)) / GLA_GATE_NORMALIZER
        dzg_ref[...] = dzg.astype(dzg_ref.dtype)
        dbg_ref[...] += jnp.sum(dzg, axis=0, keepdims=True)
        dst_ref[...] = dst * e_last + _dot(dov, qt, _TN)

    rev = lambda i: nc - 1 - i
    half = jax.ShapeDtypeStruct((t, d // 2), BF16)
    return pl.pallas_call(
        body, name=name, grid=(nh, nc),
        in_specs=[pl.BlockSpec((c, dk), lambda h, i: (rev(i), h)), pl.BlockSpec((c, dk), lambda h, i: (rev(i), nh + h)),
                  pl.BlockSpec((c, dv), lambda h, i: (rev(i), nh + h)), pl.BlockSpec((c, dk), lambda h, i: (rev(i), h)),
                  pl.BlockSpec((None, None, dv, dk), lambda h, i: (h, rev(i), 0, 0)),
                  pl.BlockSpec((c, dv), lambda h, i: (rev(i), h))],
        out_specs=[pl.BlockSpec((c, dk), lambda h, i: (rev(i), h)), pl.BlockSpec((c, dk), lambda h, i: (rev(i), h)),
                   pl.BlockSpec((c, dv), lambda h, i: (rev(i), h)), pl.BlockSpec((c, dk), lambda h, i: (rev(i), h)),
                   pl.BlockSpec((1, dk), lambda h, i: (0, h))],
        out_shape=[half, half, jax.ShapeDtypeStruct((t, d), BF16), half, jax.ShapeDtypeStruct((1, d // 2), F32)],
        scratch_shapes=[pltpu.VMEM((dv, dk), F32)], compiler_params=_params("parallel", "arbitrary"),
    )(proj, proj, proj, lg, states, do)


def gla_post_fwd(o, proj, head_norm, *, name, tm=512):
    t, d = o.shape
    dv = d // GLA_HEADS
    tm = min(tm, t)

    def body(o_ref, g_ref, hn_ref, y_ref):
        for h in range(GLA_HEADS):
            sl = slice(h * dv, (h + 1) * dv)
            ov, gv = o_ref[:, sl], g_ref[:, sl]
            r = lax.rsqrt(jnp.mean(ov * ov, axis=-1, keepdims=True) + RMS_EPS)
            y_ref[:, sl] = ((ov * r * hn_ref[:, sl]) * (gv * _sigmoid(gv))).astype(y_ref.dtype)

    row = pl.BlockSpec((tm, d), lambda i: (i, 0))
    return pl.pallas_call(
        body, name=name, grid=(t // tm,),
        in_specs=[row, pl.BlockSpec((tm, d), lambda i: (i, 2)), pl.BlockSpec((1, d), lambda i: (0, 0))],
        out_specs=row, out_shape=jax.ShapeDtypeStruct((t, d), BF16), compiler_params=_params("parallel"),
    )(o, proj, head_norm)


def gla_post_bwd(o, proj, head_norm, dy, *, name, tm=512):
    t, d = o.shape
    dv = d // GLA_HEADS
    tm = min(tm, t)

    def body(o_ref, g_ref, hn_ref, dy_ref, do_ref, dg_ref, dhn_ref):
        @pl.when(pl.program_id(0) == 0)
        def _():
            dhn_ref[...] = jnp.zeros_like(dhn_ref)

        for h in range(GLA_HEADS):
            sl = slice(h * dv, (h + 1) * dv)
            ov, gv, dyv, hn = o_ref[:, sl], g_ref[:, sl], dy_ref[:, sl], hn_ref[:, sl]
            r = lax.rsqrt(jnp.mean(ov * ov, axis=-1, keepdims=True) + RMS_EPS)
            sg = _sigmoid(gv)
            silu = gv * sg
            on = ov * r * hn
            dg_ref[:, sl] = (dyv * on * (sg * (1.0 + gv * (1.0 - sg)))).astype(dg_ref.dtype)
            don = dyv * silu
            u = don * hn
            do_ref[:, sl] = (r * u - ov * (r * r * r * jnp.mean(u * ov, axis=-1, keepdims=True))).astype(do_ref.dtype)
            dhn_ref[:, sl] += jnp.sum(don * ov * r, axis=0, keepdims=True)

    row = pl.BlockSpec((tm, d), lambda i: (i, 0))
    vec = pl.BlockSpec((1, d), lambda i: (0, 0))
    shape = jax.ShapeDtypeStruct((t, d), BF16)
    return pl.pallas_call(
        body, name=name, grid=(t // tm,),
        in_specs=[row, pl.BlockSpec((tm, d), lambda i: (i, 2)), vec, row],
        out_specs=[row, row, vec], out_shape=[shape, shape, jax.ShapeDtypeStruct((1, d), F32)],
        compiler_params=_params("arbitrary"),
    )(o, proj, head_norm, dy)


def _ffn_fwd(h, gains, w_up, w_down, tag):
    xn = rms_fwd(h, gains[2], name=f"{tag}_ffn_norm", out_dtype=BF16)
    u, act = matmul(xn, w_up, mode='nn', epi='relu2_pair', name=f"{tag}_ffn_up")
    f = matmul(act, w_down, mode='nn', name=f"{tag}_ffn_down")
    h_out = rms_fwd(f, gains[3], res=h, name=f"{tag}_ffn_out")
    return h_out, (h, xn, u, act, f)


def _ffn_bwd(dh, saved, gains, w_up, w_down, tag):
    h, xn, u, act, f = saved
    df, dg3 = rms_bwd(f, gains[3], dh, out_dtype=BF16, name=f"{tag}_ffn_out_bwd")
    du = matmul(df, w_down, mode='nt', epi='mul2relu', extra=u, out_dtype=BF16, name=f"{tag}_ffn_da")
    dw_down = matmul(act, df, mode='tn', name=f"{tag}_ffn_dwdown")
    dw_up = matmul(xn, du, mode='tn', name=f"{tag}_ffn_dwup")
    dxn = matmul(du, w_up, mode='nt', name=f"{tag}_ffn_dxn")
    dh_in, dg2 = rms_bwd(h, gains[2], dxn, dres=dh, name=f"{tag}_ffn_norm_bwd")
    return dh_in, dg2, dg3, dw_up, dw_down


def _sb_layer_fwd(xn, w, j, tag):
    qkv = matmul(xn, w['sb_w_qkv'][j], mode='nn', out_dtype=BF16, name=f"{tag}_qkv")
    o, tot = sb_fwd(qkv, name=f"{tag}_sb")
    m = matmul(o, w['sb_w_o'][j], mode='nn', name=f"{tag}_wo")
    return m, (qkv, o, tot)


def _sb_layer_bwd(dm, xn, saved, w, j, tag):
    qkv, o, tot = saved
    do = matmul(dm, w['sb_w_o'][j], mode='nt', out_dtype=BF16, name=f"{tag}_do")
    dw_o = matmul(o, dm, mode='tn', name=f"{tag}_dwo")
    dq, dk, dv = sb_bwd(qkv, tot, do, name=f"{tag}_sb_bwd")
    dqkv = jnp.concatenate([dq, dk, dv], axis=1)
    dw_qkv = matmul(xn, dqkv, mode='tn', name=f"{tag}_dwqkv")
    dxn = matmul(dqkv, w['sb_w_qkv'][j], mode='nt', name=f"{tag}_dxn")
    return dxn, {('sb_w_qkv', j): dw_qkv, ('sb_w_o', j): dw_o}


def _conv_layer_fwd(xn, w, j, tag):
    bcu = matmul(xn, w['conv_w_in'][j], mode='nn', name=f"{tag}_in")
    y = conv_fwd(bcu, w['conv_w'][j], name=f"{tag}_conv")
    m = matmul(y, w['conv_w_out'][j], mode='nn', name=f"{tag}_out")
    return m, (bcu, y)


def _conv_layer_bwd(dm, xn, saved, w, j, tag):
    bcu, y = saved
    dy = matmul(dm, w['conv_w_out'][j], mode='nt', name=f"{tag}_dy")
    dw_out = matmul(y, dm, mode='tn', name=f"{tag}_dwout")
    db, dc, du, dw_conv = conv_bwd(bcu, w['conv_w'][j], dy, name=f"{tag}_conv_bwd")
    dbcu = jnp.concatenate([db, dc, du], axis=1)
    dw_in = matmul(xn, dbcu, mode='tn', name=f"{tag}_dwin")
    dxn = matmul(dbcu, w['conv_w_in'][j], mode='nt', name=f"{tag}_dxn")
    return dxn, {('conv_w_in', j): dw_in, ('conv_w', j): dw_conv, ('conv_w_out', j): dw_out}


def _gla_split(w_in, w_gate_up):
    d = w_in.shape[0]
    w_main = w_in[:, :3 * d]
    w_a = jnp.pad(w_in[:, 3 * d:], ((0, 0), (0, LANES - GLA_GATE_RANK)))
    w_gu = jnp.pad(w_gate_up, ((0, LANES - GLA_GATE_RANK), (0, 0)))
    return w_main, w_a, w_gu


def _gla_layer_fwd(xn, w, j, tag):
    w_main, w_a, w_gu = _gla_split(w['gla_w_in'][j], w['gla_w_gate_up'][j])
    proj = matmul(xn, w_main, mode='nn', name=f"{tag}_in")
    a_low = matmul(xn, w_a, mode='nn', out_dtype=BF16, name=f"{tag}_alow")
    lg = matmul(a_low, w_gu, mode='nn', epi='logsig16', extra=w['gla_b_gate'][j][None, :], name=f"{tag}_gate")
    o, states = gla_fwd(proj, lg, name=f"{tag}_gla")
    hn = w['gla_head_norm'][j].reshape(1, -1)
    y = gla_post_fwd(o, proj, hn, name=f"{tag}_post")
    m = matmul(y, w['gla_w_o'][j], mode='nn', name=f"{tag}_wo")
    return m, (proj, a_low, lg, o, states, y)


def _gla_layer_bwd(dm, xn, saved, w, j, tag):
    proj, a_low, lg, o, states, y = saved
    w_main, w_a, w_gu = _gla_split(w['gla_w_in'][j], w['gla_w_gate_up'][j])
    hn = w['gla_head_norm'][j].reshape(1, -1)
    dy = matmul(dm, w['gla_w_o'][j], mode='nt', name=f"{tag}_dy")
    dw_o = matmul(y, dm, mode='tn', name=f"{tag}_dwo")
    do, dg, dhn = gla_post_bwd(o, proj, hn, dy, name=f"{tag}_post_bwd")
    dq, dk, dv, dzg, dbg = gla_bwd(proj, lg, states, do, name=f"{tag}_gla_bwd")
    da_low = matmul(dzg, w_gu, mode='nt', out_dtype=BF16, name=f"{tag}_dalow")
    dw_gu = matmul(a_low, dzg, mode='tn', name=f"{tag}_dwgu")[:GLA_GATE_RANK]
    dproj = jnp.concatenate([dq, dk, dv, dg], axis=1)
    dw_main = matmul(xn, dproj, mode='tn', name=f"{tag}_dwin")
    dw_a = matmul(xn, da_low, mode='tn', name=f"{tag}_dwa")[:, :GLA_GATE_RANK]
    dxn_a = matmul(da_low, w_a, mode='nt', name=f"{tag}_dxn_a")
    dxn = matmul(dproj, w_main, mode='nt', epi='add', extra=dxn_a, name=f"{tag}_dxn")
    grads = {('gla_w_in', j): jnp.concatenate([dw_main, dw_a], axis=1), ('gla_w_gate_up', j): dw_gu,
             ('gla_b_gate', j): dbg[0], ('gla_head_norm', j): dhn.reshape(w['gla_head_norm'][j].shape),
             ('gla_w_o', j): dw_o}
    return dxn, grads


_MIXERS = ((_sb_layer_fwd, _sb_layer_bwd), (_conv_layer_fwd, _conv_layer_bwd), (_gla_layer_fwd, _gla_layer_bwd))


def local_step(x, w, target):
    depth = w['norm_gains'].shape[0]
    gains = [[w['norm_gains'][i, s][None, :] for s in range(4)] for i in range(depth)]
    h = x
    tape = []
    for i in range(depth):
        kind, j = i % 3, i // 3
        tag = f"l{i}"
        xn = rms_fwd(h, gains[i][0], name=f"{tag}_mix_norm", out_dtype=BF16)
        m, saved = _MIXERS[kind][0](xn, w, j, tag)
        h_mid = rms_fwd(m, gains[i][1], res=h, name=f"{tag}_mix_out")
        h_out, ffn_saved = _ffn_fwd(h_mid, gains[i], w['ffn_w_up'][i], w['ffn_w_down'][i], tag)
        tape.append((h, xn, m, saved, ffn_saved))
        h = h_out
    loss, dh = loss_head(h, target, name="loss_head")

    grads = {}
    dgains = [[None] * 4 for _ in range(depth)]
    for i in reversed(range(depth)):
        kind, j = i % 3, i // 3
        tag = f"l{i}"
        h_in, xn, m, saved, ffn_saved = tape[i]
        dh, dgains[i][2], dgains[i][3], grads[('ffn_w_up', i)], grads[('ffn_w_down', i)] = _ffn_bwd(
            dh, ffn_saved, gains[i], w['ffn_w_up'][i], w['ffn_w_down'][i], tag)
        dm, dgains[i][1] = rms_bwd(m, gains[i][1], dh, out_dtype=BF16, name=f"{tag}_mix_out_bwd")
        dxn, g = _MIXERS[kind][1](dm, xn, saved, w, j, tag)
        grads.update(g)
        dh, dgains[i][0] = rms_bwd(h_in, gains[i][0], dxn, dres=dh, name=f"{tag}_mix_norm_bwd")

    full = {'norm_gains': jnp.stack([jnp.concatenate(row, axis=0) for row in dgains])}
    for name in WEIGHTS[1:]:
        full[name] = jnp.stack([grads[(name, j)] for j in range(w[name].shape[0])])
    return loss, dh, full


def _segments(shard_shapes, f32_as_pairs):
    segs, row = {}, 0
    for name in WEIGHTS:
        n = math.prod(shard_shapes[name]) * (2 if f32_as_pairs and name in F32_PAYLOAD else 1)
        nrows = -(-n // (PACK_COLS * PACK_ROW_ALIGN)) * PACK_ROW_ALIGN
        segs[name] = (row, nrows, n)
        row += nrows
    return segs, -(-row // PACK_ROW_BLOCK) * PACK_ROW_BLOCK


def _pack(parts, segs, total_rows, dtype):
    pieces, row = [], 0
    for name in WEIGHTS:
        _, nrows, n = segs[name]
        p = parts[name].astype(dtype)
        lead = p.shape[:-1]
        p = jnp.pad(p, [(0, 0)] * len(lead) + [(0, nrows * PACK_COLS - n)])
        pieces.append(p.reshape(lead + (nrows, PACK_COLS)))
        row += nrows
    if total_rows > row:
        pieces.append(jnp.zeros(lead + (total_rows - row, PACK_COLS), dtype))
    return jnp.concatenate(pieces, axis=-2)


def _unpack(buf, seg):
    first, nrows, n = seg
    piece = buf[..., first:first + nrows, :]
    return piece.reshape(piece.shape[:-2] + (nrows * PACK_COLS,))[..., :n]


def _unshard(gathered, axis):
    moved = jnp.moveaxis(gathered, 0, axis)
    shape = moved.shape
    return moved.reshape(shape[:axis] + (shape[axis] * shape[axis + 1],) + shape[axis + 2:])


def _shard_split(full, axis):
    shape = full.shape
    cut = full.reshape(shape[:axis] + (N_DEV, shape[axis] // N_DEV) + shape[axis + 1:])
    return jnp.moveaxis(cut, axis, 0)


def _mesh_position():
    return lax.axis_index("x"), lax.axis_index("y"), lax.axis_index("c")


def all_gather(shard, *, name):
    rows, cols = shard.shape

    def body(x_ref, out_ref, send_sems, recv_sems, local_sem):
        x, y, c = _mesh_position()
        me, sibling = (x, y, c), (x, y, 1 - c)
        chips = [(1 - x, y), (x, 1 - y), (1 - x, 1 - y)]

        def block(px, py, pc):
            return out_ref.at[pl.ds((4 * px + 2 * py + pc) * rows, rows), :]

        def copy(k, blk, to, src=None):
            return pltpu.make_async_remote_copy(
                src_ref=block(*blk) if src is None else src, dst_ref=block(*blk),
                send_sem=send_sems.at[k], recv_sem=recv_sems.at[k],
                device_id=to, device_id_type=pl.DeviceIdType.MESH)

        mine = pltpu.make_async_copy(x_ref, block(*me), local_sem)
        mine.start()
        first = [copy(0, me, sibling, src=x_ref)]
        first += [copy(1 + j, me, (*chip, c), src=x_ref) for j, chip in enumerate(chips)]
        for cp in first:
            cp.start()
        passed = [copy(4 + j, (*chip, c), sibling) for j, chip in enumerate(chips)]
        for j, chip in enumerate(chips):
            copy(1 + j, (*chip, c), me).wait_recv()
            passed[j].start()
        copy(0, sibling, me).wait_recv()
        for j, chip in enumerate(chips):
            copy(4 + j, (*chip, 1 - c), me).wait_recv()
        for cp in first + passed:
            cp.wait_send()
        mine.wait()

    return pl.pallas_call(
        body, name=name,
        out_shape=jax.ShapeDtypeStruct((N_DEV * rows, cols), shard.dtype),
        in_specs=[pl.BlockSpec(memory_space=pl.ANY)], out_specs=pl.BlockSpec(memory_space=pl.ANY),
        scratch_shapes=[pltpu.SemaphoreType.DMA((7,)), pltpu.SemaphoreType.DMA((7,)), pltpu.SemaphoreType.DMA],
    )(shard)


def exchange_shards(parts, *, name):
    _, rows, cols = parts.shape

    def body(in_ref, out_ref, send_sems, recv_sems, local_sem):
        x, y, c = _mesh_position()
        my_id = 4 * x + 2 * y + c
        mine = pltpu.make_async_copy(in_ref.at[my_id], out_ref.at[my_id], local_sem)
        mine.start()
        copies = []
        for k in range(1, N_DEV):
            px = 1 - x if k & 4 else x
            py = 1 - y if k & 2 else y
            pc = 1 - c if k & 1 else c
            peer_id = 4 * px + 2 * py + pc
            copies.append(pltpu.make_async_remote_copy(
                src_ref=in_ref.at[peer_id], dst_ref=out_ref.at[my_id],
                send_sem=send_sems.at[k - 1], recv_sem=recv_sems.at[k - 1],
                device_id=(px, py, pc), device_id_type=pl.DeviceIdType.MESH))
        for cp in copies:
            cp.start()
        for k in range(1, N_DEV):
            px = 1 - x if k & 4 else x
            py = 1 - y if k & 2 else y
            pc = 1 - c if k & 1 else c
            peer_id = 4 * px + 2 * py + pc
            pltpu.make_async_remote_copy(
                src_ref=in_ref.at[peer_id], dst_ref=out_ref.at[peer_id],
                send_sem=send_sems.at[k - 1], recv_sem=recv_sems.at[k - 1],
                device_id=(px, py, pc), device_id_type=pl.DeviceIdType.MESH).wait_recv()
        for cp in copies:
            cp.wait_send()
        mine.wait()

    return pl.pallas_call(
        body, name=name, out_shape=jax.ShapeDtypeStruct(parts.shape, parts.dtype),
        in_specs=[pl.BlockSpec(memory_space=pl.ANY)], out_specs=pl.BlockSpec(memory_space=pl.ANY),
        scratch_shapes=[pltpu.SemaphoreType.DMA((7,)), pltpu.SemaphoreType.DMA((7,)), pltpu.SemaphoreType.DMA],
    )(parts)


def adamw(parts, w, m, v, *, name):
    _, rows, cols = parts.shape
    tr = PACK_ROW_BLOCK
    c1 = 1.0 - ADAM_B1 ** ADAM_STEP
    c2 = 1.0 - ADAM_B2 ** ADAM_STEP

    def body(p_ref, w_ref, m_ref, v_ref, g_ref, d_ref, nm_ref, nv_ref):
        g = p_ref[0].astype(F32)
        for s in range(1, N_DEV):
            g = g + p_ref[s].astype(F32)
        nm = ADAM_B1 * m_ref[...] + (1.0 - ADAM_B1) * g
        nv = ADAM_B2 * v_ref[...] + (1.0 - ADAM_B2) * jnp.square(g)
        m_hat = nm / c1
        v_hat = nv / c2
        g_ref[...] = g
        d_ref[...] = -ADAM_LR * (m_hat / (jnp.sqrt(v_hat) + ADAM_EPS) + ADAM_WD * w_ref[...])
        nm_ref[...] = nm
        nv_ref[...] = nv

    row = pl.BlockSpec((tr, cols), lambda i: (i, 0))
    shape = jax.ShapeDtypeStruct((rows, cols), F32)
    return pl.pallas_call(
        body, name=name, grid=(rows // tr,),
        in_specs=[pl.BlockSpec((N_DEV, tr, cols), lambda i: (0, i, 0)), row, row, row],
        out_specs=[row, row, row, row], out_shape=[shape, shape, shape, shape],
        compiler_params=_params("parallel"),
    )(parts, w, m, v)


def kernel(x, norm_gains, sb_w_qkv, sb_w_o, conv_w_in, conv_w, conv_w_out, gla_w_in, gla_w_gate_up, gla_b_gate, gla_head_norm, gla_w_o, ffn_w_up, ffn_w_down, loss_target, m_norm_gains, m_sb_w_qkv, m_sb_w_o, m_conv_w_in, m_conv_w, m_conv_w_out, m_gla_w_in, m_gla_w_gate_up, m_gla_b_gate, m_gla_head_norm, m_gla_w_o, m_ffn_w_up, m_ffn_w_down, v_norm_gains, v_sb_w_qkv, v_sb_w_o, v_conv_w_in, v_conv_w, v_conv_w_out, v_gla_w_in, v_gla_w_gate_up, v_gla_b_gate, v_gla_head_norm, v_gla_w_o, v_ffn_w_up, v_ffn_w_down):
    shards = dict(zip(WEIGHTS, (norm_gains, sb_w_qkv, sb_w_o, conv_w_in, conv_w, conv_w_out, gla_w_in,
                                gla_w_gate_up, gla_b_gate, gla_head_norm, gla_w_o, ffn_w_up, ffn_w_down)))
    moments_m = dict(zip(WEIGHTS, (m_norm_gains, m_sb_w_qkv, m_sb_w_o, m_conv_w_in, m_conv_w, m_conv_w_out,
                                   m_gla_w_in, m_gla_w_gate_up, m_gla_b_gate, m_gla_head_norm, m_gla_w_o,
                                   m_ffn_w_up, m_ffn_w_down)))
    moments_v = dict(zip(WEIGHTS, (v_norm_gains, v_sb_w_qkv, v_sb_w_o, v_conv_w_in, v_conv_w, v_conv_w_out,
                                   v_gla_w_in, v_gla_w_gate_up, v_gla_b_gate, v_gla_head_norm, v_gla_w_o,
                                   v_ffn_w_up, v_ffn_w_down)))
    shard_shapes = {n: a.shape for n, a in shards.items()}

    segs, rows = _segments(shard_shapes, True)
    payload = _pack({n: (lax.bitcast_convert_type(a, BF16) if n in F32_PAYLOAD else a.astype(BF16)).reshape(-1)
                     for n, a in shards.items()}, segs, rows, BF16)
    gathered = all_gather(payload, name="weights_all_gather").reshape(N_DEV, rows, PACK_COLS)
    whole = {}
    for n in WEIGHTS:
        piece = _unpack(gathered, segs[n])
        if n in F32_PAYLOAD:
            piece = lax.bitcast_convert_type(piece.reshape(N_DEV, -1, 2), F32)
        whole[n] = _unshard(piece.reshape((N_DEV,) + shard_shapes[n]), SHARD_AXIS[n])

    loss, grad_x, grads = local_step(x[0], whole, loss_target[0])
    loss = lax.psum(loss[0, 0], ("x", "y", "c"))

    gsegs, grows = _segments(shard_shapes, False)
    parts = _pack({n: _shard_split(grads[n], SHARD_AXIS[n]).reshape(N_DEV, -1) for n in WEIGHTS}, gsegs, grows, BF16)
    received = exchange_shards(parts, name="grads_exchange")

    def packed(group):
        return _pack({n: group[n].reshape(-1) for n in WEIGHTS}, gsegs, grows, F32)

    outs = adamw(received, packed(shards), packed(moments_m), packed(moments_v), name="adamw")
    unpacked = [[_unpack(buf, gsegs[n]).reshape(shard_shapes[n]) for n in WEIGHTS] for buf in outs]
    return (loss, grad_x[None], *unpacked[0], *unpacked[1], *unpacked[2], *unpacked[3])
```

```python
import functools
import math

import jax
import jax.numpy as jnp
from jax import lax
from jax.experimental import pallas as pl
from jax.experimental.pallas import tpu as pltpu

F32 = jnp.float32
BF16 = jnp.bfloat16

N_DEV = 8
SB_HEADS = 16
GLA_HEADS = 4
GLA_CHUNK = 64
GLA_GATE_RANK = 16
GLA_GATE_NORMALIZER = 16.0
CONV_WIDTH = 3
DEPTH = 4
RMS_EPS = 1e-6
ADAM_LR = 0.001
ADAM_B1 = 0.9
ADAM_B2 = 0.999
ADAM_EPS = 1e-08
ADAM_WD = 0.01
ADAM_STEP = 10

LANES = 128
SB_BLOCK = 256
VMEM_LIMIT_BYTES = 56 * 1024 * 1024
MM_TM, MM_TN, MM_TK = 512, 1024, 1024
PACK_COLS = 1024
PACK_ROW_ALIGN = 16
PACK_ROW_BLOCK = 128

WEIGHTS = ['norm_gains', 'sb_w_qkv', 'sb_w_o', 'conv_w_in', 'conv_w', 'conv_w_out', 'gla_w_in',
           'gla_w_gate_up', 'gla_b_gate', 'gla_head_norm', 'gla_w_o', 'ffn_w_up', 'ffn_w_down']
SHARD_AXIS = {'norm_gains': 2, 'sb_w_qkv': 2, 'sb_w_o': 1, 'conv_w_in': 2, 'conv_w': 2, 'conv_w_out': 1,
              'gla_w_in': 2, 'gla_w_gate_up': 2, 'gla_b_gate': 1, 'gla_head_norm': 2, 'gla_w_o': 1,
              'ffn_w_up': 2, 'ffn_w_down': 1}
F32_PAYLOAD = ('norm_gains', 'conv_w', 'gla_b_gate', 'gla_head_norm')

_NN = (((1,), (0,)), ((), ()))
_NT = (((1,), (1,)), ((), ()))
_TN = (((0,), (0,)), ((), ()))
_DIMS = {'nn': _NN, 'nt': _NT, 'tn': _TN}


def _params(*semantics):
    return pltpu.CompilerParams(dimension_semantics=semantics, vmem_limit_bytes=VMEM_LIMIT_BYTES)


def _dot(a, b, dims=_NN):
    return lax.dot_general(a.astype(BF16), b.astype(BF16), dims, preferred_element_type=F32)


def _split_hi_lo(x):
    hi = x.astype(BF16)
    lo = (x - hi.astype(F32)).astype(BF16)
    return hi, lo


def _dot_exact_rhs(x, ones_mat, dims=_NN):
    hi, lo = _split_hi_lo(x)
    return (lax.dot_general(hi, ones_mat, dims, preferred_element_type=F32)
            + lax.dot_general(lo, ones_mat, dims, preferred_element_type=F32))


def _dot_exact_lhs(ones_mat, x, dims=_NN):
    hi, lo = _split_hi_lo(x)
    return (lax.dot_general(ones_mat, hi, dims, preferred_element_type=F32)
            + lax.dot_general(ones_mat, lo, dims, preferred_element_type=F32))


def _log_sigmoid(z):
    return jnp.minimum(z, 0.0) - jnp.log(1.0 + jnp.exp(-jnp.abs(z)))


def _sigmoid(z):
    return 1.0 / (1.0 + jnp.exp(-z))


def matmul(a, b, *, mode, name, out_dtype=F32, epi=None, extra=None, tm=MM_TM, tn=MM_TN, tk=MM_TK):
    if mode == 'nn':
        (m, k), (k2, n) = a.shape, b.shape
    elif mode == 'nt':
        (m, k), (n, k2) = a.shape, b.shape
    else:
        (k, m), (k2, n) = a.shape, b.shape
    assert k == k2, (a.shape, b.shape, mode)
    tm, tn, tk = min(tm, m), min(tn, n), min(tk, k)
    assert m % tm == 0 and n % tn == 0 and k % tk == 0, (a.shape, b.shape, mode)
    nk = k // tk
    if mode == 'tn':
        a_spec = pl.BlockSpec((tk, tm), lambda i, j, kk: (kk, i))
    else:
        a_spec = pl.BlockSpec((tm, tk), lambda i, j, kk: (i, kk))
    if mode == 'nt':
        b_spec = pl.BlockSpec((tn, tk), lambda i, j, kk: (j, kk))
    else:
        b_spec = pl.BlockSpec((tk, tn), lambda i, j, kk: (kk, j))
    in_specs, operands = [a_spec, b_spec], [a, b]
    if epi == 'logsig16':
        in_specs.append(pl.BlockSpec((1, tn), lambda i, j, kk: (0, j)))
        operands.append(extra)
    elif epi in ('mul2relu', 'add'):
        in_specs.append(pl.BlockSpec((tm, tn), lambda i, j, kk: (i, j)))
        operands.append(extra)

    n_extra = len(operands) - 2
    pair = epi == 'relu2_pair'

    def body(a_ref, b_ref, *rest):
        e_ref = rest[0] if n_extra else None
        outs = rest[n_extra:n_extra + (2 if pair else 1)]

        def finish(r):
            if epi == 'mul2relu':
                r = r * (2.0 * jnp.maximum(e_ref[...], 0.0))
            elif epi == 'add':
                r = r + e_ref[...]
            elif epi == 'logsig16':
                r = _log_sigmoid(r + e_ref[...]) / GLA_GATE_NORMALIZER
            outs[0][...] = r.astype(outs[0].dtype)
            if pair:
                outs[1][...] = jnp.square(jnp.maximum(r, 0.0)).astype(outs[1].dtype)

        part = _dot(a_ref[...], b_ref[...], _DIMS[mode])
        if nk == 1:
            finish(part)
        else:
            acc_ref = rest[-1]
            kk = pl.program_id(2)

            @pl.when(kk == 0)
            def _():
                acc_ref[...] = part

            @pl.when(kk > 0)
            def _():
                acc_ref[...] += part

            @pl.when(kk == nk - 1)
            def _():
                finish(acc_ref[...])

    tile = pl.BlockSpec((tm, tn), lambda i, j, kk: (i, j))
    shape = jax.ShapeDtypeStruct((m, n), out_dtype)
    return pl.pallas_call(
        body, name=name, grid=(m // tm, n // tn, nk), in_specs=in_specs,
        out_specs=[tile, tile] if pair else tile,
        out_shape=[shape, jax.ShapeDtypeStruct((m, n), BF16)] if pair else shape,
        scratch_shapes=[pltpu.VMEM((tm, tn), F32)] if nk > 1 else [],
        compiler_params=_params("parallel", "parallel", "arbitrary"),
    )(*operands)


def rms_fwd(x, gain, *, name, res=None, out_dtype=F32, tm=512):
    t, d = x.shape
    tm = min(tm, t)
    row = pl.BlockSpec((tm, d), lambda i: (i, 0))
    in_specs, operands = [row, pl.BlockSpec((1, d), lambda i: (0, 0))], [x, gain]
    if res is not None:
        in_specs.append(row)
        operands.append(res)

    def body(x_ref, g_ref, *rest):
        xv = x_ref[...]
        r = lax.rsqrt(jnp.mean(xv * xv, axis=-1, keepdims=True) + RMS_EPS)
        y = xv * r * g_ref[...]
        if res is not None:
            y = rest[0][...] + y
        rest[-1][...] = y.astype(out_dtype)

    return pl.pallas_call(
        body, name=name, grid=(t // tm,), in_specs=in_specs, out_specs=row,
        out_shape=jax.ShapeDtypeStruct((t, d), out_dtype), compiler_params=_params("parallel"),
    )(*operands)


def rms_bwd(x, gain, dy, *, name, dres=None, out_dtype=F32, tm=512):
    t, d = x.shape
    tm = min(tm, t)
    row = pl.BlockSpec((tm, d), lambda i: (i, 0))
    vec = pl.BlockSpec((1, d), lambda i: (0, 0))
    in_specs, operands = [row, vec, row], [x, gain, dy]
    if dres is not None:
        in_specs.append(row)
        operands.append(dres)

    def body(x_ref, g_ref, dy_ref, *rest):
        dx_ref, dg_ref = rest[-2], rest[-1]

        @pl.when(pl.program_id(0) == 0)
        def _():
            dg_ref[...] = jnp.zeros_like(dg_ref)

        xv, dyv = x_ref[...], dy_ref[...]
        r = lax.rsqrt(jnp.mean(xv * xv, axis=-1, keepdims=True) + RMS_EPS)
        u = dyv * g_ref[...]
        dx = r * u - xv * (r * r * r * jnp.mean(u * xv, axis=-1, keepdims=True))
        if dres is not None:
            dx = rest[0][...] + dx
        dx_ref[...] = dx.astype(out_dtype)
        dg_ref[...] += jnp.sum(dyv * xv * r, axis=0, keepdims=True)

    return pl.pallas_call(
        body, name=name, grid=(t // tm,), in_specs=in_specs, out_specs=[row, vec],
        out_shape=[jax.ShapeDtypeStruct((t, d), out_dtype), jax.ShapeDtypeStruct((1, d), F32)],
        compiler_params=_params("arbitrary"),
    )(*operands)


def loss_head(y, target, *, name, tm=512):
    t, d = y.shape
    tm = min(tm, t)
    nt = t // tm
    row = pl.BlockSpec((tm, d), lambda i: (i, 0))

    def body(y_ref, t_ref, loss_ref, dy_ref, acc_ref):
        i = pl.program_id(0)

        @pl.when(i == 0)
        def _():
            acc_ref[...] = jnp.zeros_like(acc_ref)

        err = y_ref[...] - t_ref[...]
        dy_ref[...] = err * (1.0 / d)
        acc_ref[...] += jnp.sum(err * err, axis=0, keepdims=True)

        @pl.when(i == nt - 1)
        def _():
            loss_ref[...] = jnp.sum(acc_ref[...], axis=1, keepdims=True) * (0.5 / d)

    return pl.pallas_call(
        body, name=name, grid=(nt,), in_specs=[row, row],
        out_specs=[pl.BlockSpec((1, 1), lambda i: (0, 0)), row],
        out_shape=[jax.ShapeDtypeStruct((1, 1), F32), jax.ShapeDtypeStruct((t, d), F32)],
        scratch_shapes=[pltpu.VMEM((1, d), F32)], compiler_params=_params("arbitrary"),
    )(y, target)


def _sb_block_iota():
    rows = lax.broadcasted_iota(jnp.int32, (SB_BLOCK, SB_BLOCK), 0)
    cols = lax.broadcasted_iota(jnp.int32, (SB_BLOCK, SB_BLOCK), 1)
    return rows, cols


def _sb_logits(q_h, k_blk, mask):
    z = _dot(q_h, k_blk, _NT)
    ls = _log_sigmoid(z)
    lm = ls - z
    if mask is not None:
        lm = jnp.where(mask, lm, 0.0)
    return ls, lm, jnp.sum(lm, axis=1, keepdims=True)


def _sb_weights(ls, lm, mask, tri_strict, later):
    suffix = _dot_exact_rhs(lm, tri_strict)
    w = jnp.exp(ls + suffix + later)
    return w if mask is None else jnp.where(mask, w, 0.0)


def sb_fwd(qkv, *, name):
    t, d3 = qkv.shape
    d = d3 // 3
    head_dim = d // SB_HEADS
    assert 2 * head_dim == LANES and t % SB_BLOCK == 0
    pairs = d // LANES
    nq = t // SB_BLOCK
    scale = head_dim ** -0.5

    def body(q_ref, k_ref, v_ref, o_ref, tot_ref):
        qi = pl.program_id(1)
        lane = lax.broadcasted_iota(jnp.int32, (SB_BLOCK, LANES), 1)
        first = lane < head_dim
        q = q_ref[...] * scale
        q2 = jnp.concatenate([jnp.where(first, q, jnp.zeros_like(q)), jnp.where(first, jnp.zeros_like(q), q)], axis=0)
        rows, cols = _sb_block_iota()
        tri = jnp.where(rows > cols, 1.0, 0.0).astype(BF16)
        earlier = cols < rows
        diagonal = jnp.concatenate([earlier, earlier], axis=0)

        def step(kb, carry, mask):
            ks = pl.multiple_of(kb * SB_BLOCK, SB_BLOCK)
            k_blk = k_ref[pl.ds(ks, SB_BLOCK), :]
            v_blk = v_ref[pl.ds(ks, SB_BLOCK), :]
            acc, later = carry
            ls, lm, row = _sb_logits(q2, k_blk, mask)
            w = _sb_weights(ls, lm, mask, tri, later)
            return acc + _dot(w, v_blk), later + row

        out = step(qi, (jnp.zeros((2 * SB_BLOCK, LANES), F32), jnp.zeros((2 * SB_BLOCK, 1), F32)), diagonal)
        acc, total = lax.fori_loop(0, qi, lambda i, carry: step(qi - 1 - i, carry, None), out)
        o_ref[...] = jnp.where(first, acc[:SB_BLOCK], acc[SB_BLOCK:]).astype(o_ref.dtype)
        tot_ref[...] = jnp.where(first, total[:SB_BLOCK], total[SB_BLOCK:])

    blk = lambda off: pl.BlockSpec((t, LANES), lambda p, i: (0, off + p))
    qblk = pl.BlockSpec((SB_BLOCK, LANES), lambda p, i: (i, p))
    return pl.pallas_call(
        body, name=name, grid=(pairs, nq), in_specs=[qblk, blk(pairs), blk(2 * pairs)],
        out_specs=[qblk, qblk],
        out_shape=[jax.ShapeDtypeStruct((t, d), BF16), jax.ShapeDtypeStruct((t, d), F32)],
        compiler_params=_params("parallel", "arbitrary"),
    )(qkv, qkv, qkv)


def sb_bwd(qkv, tot, do, *, name):
    t, d3 = qkv.shape
    d = d3 // 3
    head_dim = d // SB_HEADS
    pairs = d // LANES
    nq = t // SB_BLOCK
    scale = head_dim ** -0.5

    def body(q_ref, k_ref, v_ref, tot_ref, do_ref, dq_ref, dk_ref, dv_ref, dk_sum, dv_sum):
        qi = pl.program_id(1)

        @pl.when(qi == 0)
        def _():
            dk_sum[...] = jnp.zeros_like(dk_sum)
            dv_sum[...] = jnp.zeros_like(dv_sum)

        lane = lax.broadcasted_iota(jnp.int32, (SB_BLOCK, LANES), 1)
        first = lane < head_dim
        q, dov, totv = q_ref[...] * scale, do_ref[...], tot_ref[...]
        second = jnp.logical_not(first)
        q2 = jnp.concatenate([jnp.where(s, q, jnp.zeros_like(q)) for s in (first, second)], axis=0)
        do2 = jnp.concatenate([jnp.where(s, dov, jnp.zeros_like(dov)) for s in (first, second)], axis=0)
        tot2 = jnp.concatenate([totv[:, 0:1], totv[:, head_dim:head_dim + 1]], axis=0)
        rows, cols = _sb_block_iota()
        tri_strict = jnp.where(rows > cols, 1.0, 0.0).astype(BF16)
        tri_before = jnp.where(rows < cols, 1.0, 0.0).astype(BF16)
        earlier = cols < rows
        diagonal = jnp.concatenate([earlier, earlier], axis=0)

        def step(kb, carry, mask):
            ks = pl.multiple_of(kb * SB_BLOCK, SB_BLOCK)
            k_blk = k_ref[pl.ds(ks, SB_BLOCK), :]
            v_blk = v_ref[pl.ds(ks, SB_BLOCK), :]
            dq, seen, before = carry
            ls, lm, row = _sb_logits(q2, k_blk, mask)
            seen = seen + row
            w = _sb_weights(ls, lm, mask, tri_strict, tot2 - seen)
            da = _dot(do2, v_blk, _NT) * w
            g = _dot_exact_rhs(da, tri_before) + before
            dz = da - jnp.exp(ls) * (da + g)
            if mask is not None:
                dz = jnp.where(mask, dz, 0.0)
            dk_sum[pl.ds(ks, SB_BLOCK), :] += _dot(dz, q2, _TN)
            dv_sum[pl.ds(ks, SB_BLOCK), :] += _dot(w, do2, _TN)
            return dq + _dot(dz, k_blk * scale), seen, before + jnp.sum(da, axis=1, keepdims=True)

        zero = jnp.zeros((2 * SB_BLOCK, LANES), F32)
        zcol = jnp.zeros((2 * SB_BLOCK, 1), F32)
        out = lax.fori_loop(0, qi, lambda kb, carry: step(kb, carry, None), (zero, zcol, zcol))
        dq = step(qi, out, diagonal)[0]
        dq_ref[...] = jnp.where(first, dq[:SB_BLOCK], dq[SB_BLOCK:]).astype(dq_ref.dtype)

        @pl.when(qi == nq - 1)
        def _():
            dk_ref[...] = dk_sum[...].astype(dk_ref.dtype)
            dv_ref[...] = dv_sum[...].astype(dv_ref.dtype)

    qblk = pl.BlockSpec((SB_BLOCK, LANES), lambda p, i: (i, p))
    col = lambda off: pl.BlockSpec((t, LANES), lambda p, i: (0, off + p))
    shape = jax.ShapeDtypeStruct((t, d), BF16)
    return pl.pallas_call(
        body, name=name, grid=(pairs, nq),
        in_specs=[qblk, col(pairs), col(2 * pairs), qblk, qblk],
        out_specs=[qblk, col(0), col(0)], out_shape=[shape, shape, shape],
        scratch_shapes=[pltpu.VMEM((t, LANES), F32), pltpu.VMEM((t, LANES), F32)],
        compiler_params=_params("parallel", "arbitrary"),
    )(qkv, qkv, qkv, tot, do)


def _shift_down(x, s):
    rows = lax.broadcasted_iota(jnp.int32, x.shape, 0)
    return jnp.where(rows >= s, pltpu.roll(x, s, 0), 0.0)


def _shift_up(x, s):
    t = x.shape[0]
    rows = lax.broadcasted_iota(jnp.int32, x.shape, 0)
    return jnp.where(rows < t - s, pltpu.roll(x, t - s, 0), 0.0)


def conv_fwd(bcu, w, *, name):
    t, d3 = bcu.shape
    d = d3 // 3
    nb = d // LANES
    col = lambda off: pl.BlockSpec((t, LANES), lambda j: (0, off + j))

    def body(b_ref, c_ref, u_ref, w_ref, y_ref):
        hh = c_ref[...] * u_ref[...]
        conv = w_ref[0:1, :] * _shift_down(hh, 2) + w_ref[1:2, :] * _shift_down(hh, 1) + w_ref[2:3, :] * hh
        y_ref[...] = (b_ref[...] * conv).astype(y_ref.dtype)

    return pl.pallas_call(
        body, name=name, grid=(nb,),
        in_specs=[col(0), col(nb), col(2 * nb), pl.BlockSpec((CONV_WIDTH, LANES), lambda j: (0, j))],
        out_specs=col(0), out_shape=jax.ShapeDtypeStruct((t, d), BF16), compiler_params=_params("parallel"),
    )(bcu, bcu, bcu, w)


def conv_bwd(bcu, w, dy, *, name):
    t, d3 = bcu.shape
    d = d3 // 3
    nb = d // LANES
    col = lambda off: pl.BlockSpec((t, LANES), lambda j: (0, off + j))
    wspec = pl.BlockSpec((CONV_WIDTH, LANES), lambda j: (0, j))

    def body(b_ref, c_ref, u_ref, w_ref, dy_ref, db_ref, dc_ref, du_ref, dw_ref):
        c, u, dyv = c_ref[...], u_ref[...], dy_ref[...]
        hh = c * u
        h2, h1 = _shift_down(hh, 2), _shift_down(hh, 1)
        w0, w1, w2 = w_ref[0:1, :], w_ref[1:2, :], w_ref[2:3, :]
        db_ref[...] = (dyv * (w0 * h2 + w1 * h1 + w2 * hh)).astype(db_ref.dtype)
        dconv = dyv * b_ref[...]
        dhh = w2 * dconv + w1 * _shift_up(dconv, 1) + w0 * _shift_up(dconv, 2)
        dc_ref[...] = (dhh * u).astype(dc_ref.dtype)
        du_ref[...] = (dhh * c).astype(du_ref.dtype)
        dw_ref[0:1, :] = jnp.sum(dconv * h2, axis=0, keepdims=True)
        dw_ref[1:2, :] = jnp.sum(dconv * h1, axis=0, keepdims=True)
        dw_ref[2:3, :] = jnp.sum(dconv * hh, axis=0, keepdims=True)

    shape = jax.ShapeDtypeStruct((t, d), BF16)
    return pl.pallas_call(
        body, name=name, grid=(nb,),
        in_specs=[col(0), col(nb), col(2 * nb), wspec, col(0)],
        out_specs=[col(0), col(0), col(0), wspec],
        out_shape=[shape, shape, shape, jax.ShapeDtypeStruct((CONV_WIDTH, d), F32)],
        compiler_params=_params("parallel"),
    )(bcu, bcu, bcu, w, dy)


def _gla_chunk(q_ref, k_ref, lg_ref, scale):
    c = GLA_CHUNK
    rows = lax.broadcasted_iota(jnp.int32, (c, c), 0)
    cols = lax.broadcasted_iota(jnp.int32, (c, c), 1)
    causal = rows >= cols
    tril = jnp.where(causal, 1.0, 0.0).astype(BF16)
    q = q_ref[...] * scale
    k = k_ref[...]
    cum = _dot_exact_lhs(tril, lg_ref[...])
    last = cum[c - 1:c, :]
    eq = jnp.exp(cum)
    ek = jnp.exp(-cum)
    el = jnp.exp(last - cum)
    return causal, tril, q, k, cum, last, eq, ek, el


def gla_fwd(proj, lg, *, name):
    t, d3 = proj.shape
    d = d3 // 3
    dk, dv = d // 2 // GLA_HEADS, d // GLA_HEADS
    assert dk == LANES and dv == 2 * LANES
    c = GLA_CHUNK
    nc = t // c
    scale = dk ** -0.5
    nh = GLA_HEADS

    def body(q_ref, k_ref, v_ref, lg_ref, o_ref, st_out_ref, st_ref):
        @pl.when(pl.program_id(1) == 0)
        def _():
            st_ref[...] = jnp.zeros_like(st_ref)

        causal, _, q, k, _, last, eq, ek, el = _gla_chunk(q_ref, k_ref, lg_ref, scale)
        v = v_ref[...]
        st = st_ref[...]
        st_out_ref[...] = st
        qt = q * eq
        scores = jnp.where(causal, _dot(qt, k * ek, _NT), 0.0)
        o_ref[...] = _dot(qt, st, _NT) + _dot(scores, v)
        st_ref[...] = st * jnp.exp(last) + _dot(v, k * el, _TN)

    return pl.pallas_call(
        body, name=name, grid=(nh, nc),
        in_specs=[pl.BlockSpec((c, dk), lambda h, i: (i, h)), pl.BlockSpec((c, dk), lambda h, i: (i, nh + h)),
                  pl.BlockSpec((c, dv), lambda h, i: (i, nh + h)), pl.BlockSpec((c, dk), lambda h, i: (i, h))],
        out_specs=[pl.BlockSpec((c, dv), lambda h, i: (i, h)),
                   pl.BlockSpec((None, None, dv, dk), lambda h, i: (h, i, 0, 0))],
        out_shape=[jax.ShapeDtypeStruct((t, d), F32), jax.ShapeDtypeStruct((nh, nc, dv, dk), F32)],
        scratch_shapes=[pltpu.VMEM((dv, dk), F32)], compiler_params=_params("parallel", "arbitrary"),
    )(proj, proj, proj, lg)


def gla_bwd(proj, lg, states, do, *, name):
    t, d3 = proj.shape
    d = d3 // 3
    dk, dv = d // 2 // GLA_HEADS, d // GLA_HEADS
    c = GLA_CHUNK
    nc = t // c
    scale = dk ** -0.5
    nh = GLA_HEADS

    def body(q_ref, k_ref, v_ref, lg_ref, st_ref, do_ref, dq_ref, dk_ref, dv_ref, dzg_ref, dbg_ref, dst_ref):
        @pl.when(pl.program_id(1) == 0)
        def _():
            dst_ref[...] = jnp.zeros_like(dst_ref)
            dbg_ref[...] = jnp.zeros_like(dbg_ref)

        causal, tril, q, k, _, last, eq, ek, el = _gla_chunk(q_ref, k_ref, lg_ref, scale)
        v, st, dov, dst = v_ref[...], st_ref[...], do_ref[...], dst_ref[...]
        qt, kt, kh = q * eq, k * ek, k * el
        scores = jnp.where(causal, _dot(qt, kt, _NT), 0.0)
        dscores = jnp.where(causal, _dot(dov, v, _NT), 0.0)
        dqt = _dot(dov, st) + _dot(dscores, kt)
        dkt = _dot(dscores, qt, _TN)
        dkh = _dot(v, dst)
        dv_ref[...] = (_dot(scores, dov, _TN) + _dot(kh, dst, _NT)).astype(dv_ref.dtype)
        dq_ref[...] = (dqt * eq * scale).astype(dq_ref.dtype)
        dk_ref[...] = (dkt * ek + dkh * el).astype(dk_ref.dtype)
        kh_dkh = kh * dkh
        e_last = jnp.exp(last)
        dlast = jnp.sum(kh_dkh, axis=0, keepdims=True) + e_last * jnp.sum(dst * st, axis=0, keepdims=True)
        dcum = qt * dqt - kt * dkt - kh_dkh
        dlg = _dot_exact_lhs(tril, dcum, _TN) + dlast
        lgv = lg_ref[...]
        dzg = dlg * (1.0 - jnp.exp(lgv * GLA_GATE_NORMALIZER)) / GLA_GATE_NORMALIZER
        dzg_ref[...] = dzg.astype(dzg_ref.dtype)
        dbg_ref[...] += jnp.sum(dzg, axis=0, keepdims=True)
        dst_ref[...] = dst * e_last + _dot(dov, qt, _TN)

    rev = lambda i: nc - 1 - i
    half = jax.ShapeDtypeStruct((t, d // 2), BF16)
    return pl.pallas_call(
        body, name=name, grid=(nh, nc),
        in_specs=[pl.BlockSpec((c, dk), lambda h, i: (rev(i), h)), pl.BlockSpec((c, dk), lambda h, i: (rev(i), nh + h)),
                  pl.BlockSpec((c, dv), lambda h, i: (rev(i), nh + h)), pl.BlockSpec((c, dk), lambda h, i: (rev(i), h)),
                  pl.BlockSpec((None, None, dv, dk), lambda h, i: (h, rev(i), 0, 0)),
                  pl.BlockSpec((c, dv), lambda h, i: (rev(i), h))],
        out_specs=[pl.BlockSpec((c, dk), lambda h, i: (rev(i), h)), pl.BlockSpec((c, dk), lambda h, i: (rev(i), h)),
                   pl.BlockSpec((c, dv), lambda h, i: (rev(i), h)), pl.BlockSpec((c, dk), lambda h, i: (rev(i), h)),
                   pl.BlockSpec((1, dk), lambda h, i: (0, h))],
        out_shape=[half, half, jax.ShapeDtypeStruct((t, d), BF16), half, jax.ShapeDtypeStruct((1, d // 2), F32)],
        scratch_shapes=[pltpu.VMEM((dv, dk), F32)], compiler_params=_params("parallel", "arbitrary"),
    )(proj, proj, proj, lg, states, do)


def gla_post_fwd(o, proj, head_norm, *, name, tm=512):
    t, d = o.shape
    dv = d // GLA_HEADS
    tm = min(tm, t)

    def body(o_ref, g_ref, hn_ref, y_ref):
        for h in range(GLA_HEADS):
            sl = slice(h * dv, (h + 1) * dv)
            ov, gv = o_ref[:, sl], g_ref[:, sl]
            r = lax.rsqrt(jnp.mean(ov * ov, axis=-1, keepdims=True) + RMS_EPS)
            y_ref[:, sl] = ((ov * r * hn_ref[:, sl]) * (gv * _sigmoid(gv))).astype(y_ref.dtype)

    row = pl.BlockSpec((tm, d), lambda i: (i, 0))
    return pl.pallas_call(
        body, name=name, grid=(t // tm,),
        in_specs=[row, pl.BlockSpec((tm, d), lambda i: (i, 2)), pl.BlockSpec((1, d), lambda i: (0, 0))],
        out_specs=row, out_shape=jax.ShapeDtypeStruct((t, d), BF16), compiler_params=_params("parallel"),
    )(o, proj, head_norm)


def gla_post_bwd(o, proj, head_norm, dy, *, name, tm=512):
    t, d = o.shape
    dv = d // GLA_HEADS
    tm = min(tm, t)

    def body(o_ref, g_ref, hn_ref, dy_ref, do_ref, dg_ref, dhn_ref):
        @pl.when(pl.program_id(0) == 0)
        def _():
            dhn_ref[...] = jnp.zeros_like(dhn_ref)

        for h in range(GLA_HEADS):
            sl = slice(h * dv, (h + 1) * dv)
            ov, gv, dyv, hn = o_ref[:, sl], g_ref[:, sl], dy_ref[:, sl], hn_ref[:, sl]
            r = lax.rsqrt(jnp.mean(ov * ov, axis=-1, keepdims=True) + RMS_EPS)
            sg = _sigmoid(gv)
            silu = gv * sg
            on = ov * r * hn
            dg_ref[:, sl] = (dyv * on * (sg * (1.0 + gv * (1.0 - sg)))).astype(dg_ref.dtype)
            don = dyv * silu
            u = don * hn
            do_ref[:, sl] = (r * u - ov * (r * r * r * jnp.mean(u * ov, axis=-1, keepdims=True))).astype(do_ref.dtype)
            dhn_ref[:, sl] += jnp.sum(don * ov * r, axis=0, keepdims=True)

    row = pl.BlockSpec((tm, d), lambda i: (i, 0))
    vec = pl.BlockSpec((1, d), lambda i: (0, 0))
    shape = jax.ShapeDtypeStruct((t, d), BF16)
    return pl.pallas_call(
        body, name=name, grid=(t // tm,),
        in_specs=[row, pl.BlockSpec((tm, d), lambda i: (i, 2)), vec, row],
        out_specs=[row, row, vec], out_shape=[shape, shape, jax.ShapeDtypeStruct((1, d), F32)],
        compiler_params=_params("arbitrary"),
    )(o, proj, head_norm, dy)


def _ffn_fwd(h, gains, w_up, w_down, tag):
    xn = rms_fwd(h, gains[2], name=f"{tag}_ffn_norm", out_dtype=BF16)
    u, act = matmul(xn, w_up, mode='nn', epi='relu2_pair', name=f"{tag}_ffn_up")
    f = matmul(act, w_down, mode='nn', name=f"{tag}_ffn_down")
    h_out = rms_fwd(f, gains[3], res=h, name=f"{tag}_ffn_out")
    return h_out, (h, xn, u, act, f)


def _ffn_bwd(dh, saved, gains, w_up, w_down, tag):
    h, xn, u, act, f = saved
    df, dg3 = rms_bwd(f, gains[3], dh, out_dtype=BF16, name=f"{tag}_ffn_out_bwd")
    du = matmul(df, w_down, mode='nt', epi='mul2relu', extra=u, out_dtype=BF16, name=f"{tag}_ffn_da")
    dw_down = matmul(act, df, mode='tn', name=f"{tag}_ffn_dwdown")
    dw_up = matmul(xn, du, mode='tn', name=f"{tag}_ffn_dwup")
    dxn = matmul(du, w_up, mode='nt', name=f"{tag}_ffn_dxn")
    dh_in, dg2 = rms_bwd(h, gains[2], dxn, dres=dh, name=f"{tag}_ffn_norm_bwd")
    return dh_in, dg2, dg3, dw_up, dw_down


def _sb_layer_fwd(xn, w, j, tag):
    qkv = matmul(xn, w['sb_w_qkv'][j], mode='nn', out_dtype=BF16, name=f"{tag}_qkv")
    o, tot = sb_fwd(qkv, name=f"{tag}_sb")
    m = matmul(o, w['sb_w_o'][j], mode='nn', name=f"{tag}_wo")
    return m, (qkv, o, tot)


def _sb_layer_bwd(dm, xn, saved, w, j, tag):
    qkv, o, tot = saved
    do = matmul(dm, w['sb_w_o'][j], mode='nt', out_dtype=BF16, name=f"{tag}_do")
    dw_o = matmul(o, dm, mode='tn', name=f"{tag}_dwo")
    dq, dk, dv = sb_bwd(qkv, tot, do, name=f"{tag}_sb_bwd")
    dqkv = jnp.concatenate([dq, dk, dv], axis=1)
    dw_qkv = matmul(xn, dqkv, mode='tn', name=f"{tag}_dwqkv")
    dxn = matmul(dqkv, w['sb_w_qkv'][j], mode='nt', name=f"{tag}_dxn")
    return dxn, {('sb_w_qkv', j): dw_qkv, ('sb_w_o', j): dw_o}


def _conv_layer_fwd(xn, w, j, tag):
    bcu = matmul(xn, w['conv_w_in'][j], mode='nn', name=f"{tag}_in")
    y = conv_fwd(bcu, w['conv_w'][j], name=f"{tag}_conv")
    m = matmul(y, w['conv_w_out'][j], mode='nn', name=f"{tag}_out")
    return m, (bcu, y)


def _conv_layer_bwd(dm, xn, saved, w, j, tag):
    bcu, y = saved
    dy = matmul(dm, w['conv_w_out'][j], mode='nt', name=f"{tag}_dy")
    dw_out = matmul(y, dm, mode='tn', name=f"{tag}_dwout")
    db, dc, du, dw_conv = conv_bwd(bcu, w['conv_w'][j], dy, name=f"{tag}_conv_bwd")
    dbcu = jnp.concatenate([db, dc, du], axis=1)
    dw_in = matmul(xn, dbcu, mode='tn', name=f"{tag}_dwin")
    dxn = matmul(dbcu, w['conv_w_in'][j], mode='nt', name=f"{tag}_dxn")
    return dxn, {('conv_w_in', j): dw_in, ('conv_w', j): dw_conv, ('conv_w_out', j): dw_out}


def _gla_split(w_in, w_gate_up):
    d = w_in.shape[0]
    w_main = w_in[:, :3 * d]
    w_a = jnp.pad(w_in[:, 3 * d:], ((0, 0), (0, LANES - GLA_GATE_RANK)))
    w_gu = jnp.pad(w_gate_up, ((0, LANES - GLA_GATE_RANK), (0, 0)))
    return w_main, w_a, w_gu


def _gla_layer_fwd(xn, w, j, tag):
    w_main, w_a, w_gu = _gla_split(w['gla_w_in'][j], w['gla_w_gate_up'][j])
    proj = matmul(xn, w_main, mode='nn', name=f"{tag}_in")
    a_low = matmul(xn, w_a, mode='nn', out_dtype=BF16, name=f"{tag}_alow")
    lg = matmul(a_low, w_gu, mode='nn', epi='logsig16', extra=w['gla_b_gate'][j][None, :], name=f"{tag}_gate")
    o, states = gla_fwd(proj, lg, name=f"{tag}_gla")
    hn = w['gla_head_norm'][j].reshape(1, -1)
    y = gla_post_fwd(o, proj, hn, name=f"{tag}_post")
    m = matmul(y, w['gla_w_o'][j], mode='nn', name=f"{tag}_wo")
    return m, (proj, a_low, lg, o, states, y)


def _gla_layer_bwd(dm, xn, saved, w, j, tag):
    proj, a_low, lg, o, states, y = saved
    w_main, w_a, w_gu = _gla_split(w['gla_w_in'][j], w['gla_w_gate_up'][j])
    hn = w['gla_head_norm'][j].reshape(1, -1)
    dy = matmul(dm, w['gla_w_o'][j], mode='nt', name=f"{tag}_dy")
    dw_o = matmul(y, dm, mode='tn', name=f"{tag}_dwo")
    do, dg, dhn = gla_post_bwd(o, proj, hn, dy, name=f"{tag}_post_bwd")
    dq, dk, dv, dzg, dbg = gla_bwd(proj, lg, states, do, name=f"{tag}_gla_bwd")
    da_low = matmul(dzg, w_gu, mode='nt', out_dtype=BF16, name=f"{tag}_dalow")
    dw_gu = matmul(a_low, dzg, mode='tn', name=f"{tag}_dwgu")[:GLA_GATE_RANK]
    dproj = jnp.concatenate([dq, dk, dv, dg], axis=1)
    dw_main = matmul(xn, dproj, mode='tn', name=f"{tag}_dwin")
    dw_a = matmul(xn, da_low, mode='tn', name=f"{tag}_dwa")[:, :GLA_GATE_RANK]
    dxn_a = matmul(da_low, w_a, mode='nt', name=f"{tag}_dxn_a")
    dxn = matmul(dproj, w_main, mode='nt', epi='add', extra=dxn_a, name=f"{tag}_dxn")
    grads = {('gla_w_in', j): jnp.concatenate([dw_main, dw_a], axis=1), ('gla_w_gate_up', j): dw_gu,
             ('gla_b_gate', j): dbg[0], ('gla_head_norm', j): dhn.reshape(w['gla_head_norm'][j].shape),
             ('gla_w_o', j): dw_o}
    return dxn, grads


_MIXERS = ((_sb_layer_fwd, _sb_layer_bwd), (_conv_layer_fwd, _conv_layer_bwd), (_gla_layer_fwd, _gla_layer_bwd))


def local_step(x, w, target):
    depth = w['norm_gains'].shape[0]
    gains = [[w['norm_gains'][i, s][None, :] for s in range(4)] for i in range(depth)]
    h = x
    tape = []
    for i in range(depth):
        kind, j = i % 3, i // 3
        tag = f"l{i}"
        xn = rms_fwd(h, gains[i][0], name=f"{tag}_mix_norm", out_dtype=BF16)
        m, saved = _MIXERS[kind][0](xn, w, j, tag)
        h_mid = rms_fwd(m, gains[i][1], res=h, name=f"{tag}_mix_out")
        h_out, ffn_saved = _ffn_fwd(h_mid, gains[i], w['ffn_w_up'][i], w['ffn_w_down'][i], tag)
        tape.append((h, xn, m, saved, ffn_saved))
        h = h_out
    loss, dh = loss_head(h, target, name="loss_head")

    grads = {}
    dgains = [[None] * 4 for _ in range(depth)]
    for i in reversed(range(depth)):
        kind, j = i % 3, i // 3
        tag = f"l{i}"
        h_in, xn, m, saved, ffn_saved = tape[i]
        dh, dgains[i][2], dgains[i][3], grads[('ffn_w_up', i)], grads[('ffn_w_down', i)] = _ffn_bwd(
            dh, ffn_saved, gains[i], w['ffn_w_up'][i], w['ffn_w_down'][i], tag)
        dm, dgains[i][1] = rms_bwd(m, gains[i][1], dh, out_dtype=BF16, name=f"{tag}_mix_out_bwd")
        dxn, g = _MIXERS[kind][1](dm, xn, saved, w, j, tag)
        grads.update(g)
        dh, dgains[i][0] = rms_bwd(h_in, gains[i][0], dxn, dres=dh, name=f"{tag}_mix_norm_bwd")

    full = {'norm_gains': jnp.stack([jnp.concatenate(row, axis=0) for row in dgains])}
    for name in WEIGHTS[1:]:
        full[name] = jnp.stack([grads[(name, j)] for j in range(w[name].shape[0])])
    return loss, dh, full


def _segments(shard_shapes, f32_as_pairs):
    segs, row = {}, 0
    for name in WEIGHTS:
        n = math.prod(shard_shapes[name]) * (2 if f32_as_pairs and name in F32_PAYLOAD else 1)
        nrows = -(-n // (PACK_COLS * PACK_ROW_ALIGN)) * PACK_ROW_ALIGN
        segs[name] = (row, nrows, n)
        row += nrows
    return segs, -(-row // PACK_ROW_BLOCK) * PACK_ROW_BLOCK


def _pack(parts, segs, total_rows, dtype):
    pieces, row = [], 0
    for name in WEIGHTS:
        _, nrows, n = segs[name]
        p = parts[name].astype(dtype)
        lead = p.shape[:-1]
        p = jnp.pad(p, [(0, 0)] * len(lead) + [(0, nrows * PACK_COLS - n)])
        pieces.append(p.reshape(lead + (nrows, PACK_COLS)))
        row += nrows
    if total_rows > row:
        pieces.append(jnp.zeros(lead + (total_rows - row, PACK_COLS), dtype))
    return jnp.concatenate(pieces, axis=-2)


def _unpack(buf, seg):
    first, nrows, n = seg
    piece = buf[..., first:first + nrows, :]
    return piece.reshape(piece.shape[:-2] + (nrows * PACK_COLS,))[..., :n]


def _unshard(gathered, axis):
    moved = jnp.moveaxis(gathered, 0, axis)
    shape = moved.shape
    return moved.reshape(shape[:axis] + (shape[axis] * shape[axis + 1],) + shape[axis + 2:])


def _shard_split(full, axis):
    shape = full.shape
    cut = full.reshape(shape[:axis] + (N_DEV, shape[axis] // N_DEV) + shape[axis + 1:])
    return jnp.moveaxis(cut, axis, 0)


def _mesh_position():
    return lax.axis_index("x"), lax.axis_index("y"), lax.axis_index("c")


def all_gather(shard, *, name):
    rows, cols = shard.shape

    def body(x_ref, out_ref, send_sems, recv_sems, local_sem):
        x, y, c = _mesh_position()
        me, sibling = (x, y, c), (x, y, 1 - c)
        chips = [(1 - x, y), (x, 1 - y), (1 - x, 1 - y)]

        def block(px, py, pc):
            return out_ref.at[pl.ds((4 * px + 2 * py + pc) * rows, rows), :]

        def copy(k, blk, to, src=None):
            return pltpu.make_async_remote_copy(
                src_ref=block(*blk) if src is None else src, dst_ref=block(*blk),
                send_sem=send_sems.at[k], recv_sem=recv_sems.at[k],
                device_id=to, device_id_type=pl.DeviceIdType.MESH)

        mine = pltpu.make_async_copy(x_ref, block(*me), local_sem)
        mine.start()
        first = [copy(0, me, sibling, src=x_ref)]
        first += [copy(1 + j, me, (*chip, c), src=x_ref) for j, chip in enumerate(chips)]
        for cp in first:
            cp.start()
        passed = [copy(4 + j, (*chip, c), sibling) for j, chip in enumerate(chips)]
        for j, chip in enumerate(chips):
            copy(1 + j, (*chip, c), me).wait_recv()
            passed[j].start()
        copy(0, sibling, me).wait_recv()
        for j, chip in enumerate(chips):
            copy(4 + j, (*chip, 1 - c), me).wait_recv()
        for cp in first + passed:
            cp.wait_send()
        mine.wait()

    return pl.pallas_call(
        body, name=name,
        out_shape=jax.ShapeDtypeStruct((N_DEV * rows, cols), shard.dtype),
        in_specs=[pl.BlockSpec(memory_space=pl.ANY)], out_specs=pl.BlockSpec(memory_space=pl.ANY),
        scratch_shapes=[pltpu.SemaphoreType.DMA((7,)), pltpu.SemaphoreType.DMA((7,)), pltpu.SemaphoreType.DMA],
    )(shard)


def exchange_shards(parts, *, name):
    _, rows, cols = parts.shape

    def body(in_ref, out_ref, send_sems, recv_sems, local_sem):
        x, y, c = _mesh_position()
        my_id = 4 * x + 2 * y + c
        mine = pltpu.make_async_copy(in_ref.at[my_id], out_ref.at[my_id], local_sem)
        mine.start()
        copies = []
        for k in range(1, N_DEV):
            px = 1 - x if k & 4 else x
            py = 1 - y if k & 2 else y
            pc = 1 - c if k & 1 else c
            peer_id = 4 * px + 2 * py + pc
            copies.append(pltpu.make_async_remote_copy(
                src_ref=in_ref.at[peer_id], dst_ref=out_ref.at[my_id],
                send_sem=send_sems.at[k - 1], recv_sem=recv_sems.at[k - 1],
                device_id=(px, py, pc), device_id_type=pl.DeviceIdType.MESH))
        for cp in copies:
            cp.start()
        for k in range(1, N_DEV):
            px = 1 - x if k & 4 else x
            py = 1 - y if k & 2 else y
            pc = 1 - c if k & 1 else c
            peer_id = 4 * px + 2 * py + pc
            pltpu.make_async_remote_copy(
                src_ref=in_ref.at[peer_id], dst_ref=out_ref.at[peer_id],
                send_sem=send_sems.at[k - 1], recv_sem=recv_sems.at[k - 1],
                device_id=(px, py, pc), device_id_type=pl.DeviceIdType.MESH).wait_recv()
        for cp in copies:
            cp.wait_send()
        mine.wait()

    return pl.pallas_call(
        body, name=name, out_shape=jax.ShapeDtypeStruct(parts.shape, parts.dtype),
        in_specs=[pl.BlockSpec(memory_space=pl.ANY)], out_specs=pl.BlockSpec(memory_space=pl.ANY),
        scratch_shapes=[pltpu.SemaphoreType.DMA((7,)), pltpu.SemaphoreType.DMA((7,)), pltpu.SemaphoreType.DMA],
    )(parts)


def adamw(parts, w, m, v, *, name):
    _, rows, cols = parts.shape
    tr = PACK_ROW_BLOCK
    c1 = 1.0 - ADAM_B1 ** ADAM_STEP
    c2 = 1.0 - ADAM_B2 ** ADAM_STEP

    def body(p_ref, w_ref, m_ref, v_ref, g_ref, d_ref, nm_ref, nv_ref):
        g = p_ref[0].astype(F32)
        for s in range(1, N_DEV):
            g = g + p_ref[s].astype(F32)
        nm = ADAM_B1 * m_ref[...] + (1.0 - ADAM_B1) * g
        nv = ADAM_B2 * v_ref[...] + (1.0 - ADAM_B2) * jnp.square(g)
        m_hat = nm / c1
        v_hat = nv / c2
        g_ref[...] = g
        d_ref[...] = -ADAM_LR * (m_hat / (jnp.sqrt(v_hat) + ADAM_EPS) + ADAM_WD * w_ref[...])
        nm_ref[...] = nm
        nv_ref[...] = nv

    row = pl.BlockSpec((tr, cols), lambda i: (i, 0))
    shape = jax.ShapeDtypeStruct((rows, cols), F32)
    return pl.pallas_call(
        body, name=name, grid=(rows // tr,),
        in_specs=[pl.BlockSpec((N_DEV, tr, cols), lambda i: (0, i, 0)), row, row, row],
        out_specs=[row, row, row, row], out_shape=[shape, shape, shape, shape],
        compiler_params=_params("parallel"),
    )(parts, w, m, v)


def kernel(x, norm_gains, sb_w_qkv, sb_w_o, conv_w_in, conv_w, conv_w_out, gla_w_in, gla_w_gate_up, gla_b_gate, gla_head_norm, gla_w_o, ffn_w_up, ffn_w_down, loss_target, m_norm_gains, m_sb_w_qkv, m_sb_w_o, m_conv_w_in, m_conv_w, m_conv_w_out, m_gla_w_in, m_gla_w_gate_up, m_gla_b_gate, m_gla_head_norm, m_gla_w_o, m_ffn_w_up, m_ffn_w_down, v_norm_gains, v_sb_w_qkv, v_sb_w_o, v_conv_w_in, v_conv_w, v_conv_w_out, v_gla_w_in, v_gla_w_gate_up, v_gla_b_gate, v_gla_head_norm, v_gla_w_o, v_ffn_w_up, v_ffn_w_down):
    shards = dict(zip(WEIGHTS, (norm_gains, sb_w_qkv, sb_w_o, conv_w_in, conv_w, conv_w_out, gla_w_in,
                                gla_w_gate_up, gla_b_gate, gla_head_norm, gla_w_o, ffn_w_up, ffn_w_down)))
    moments_m = dict(zip(WEIGHTS, (m_norm_gains, m_sb_w_qkv, m_sb_w_o, m_conv_w_in, m_conv_w, m_conv_w_out,
                                   m_gla_w_in, m_gla_w_gate_up, m_gla_b_gate, m_gla_head_norm, m_gla_w_o,
                                   m_ffn_w_up, m_ffn_w_down)))
    moments_v = dict(zip(WEIGHTS, (v_norm_gains, v_sb_w_qkv, v_sb_w_o, v_conv_w_in, v_conv_w, v_conv_w_out,
                                   v_gla_w_in, v_gla_w_gate_up, v_gla_b_gate, v_gla_head_norm, v_gla_w_o,
                                   v_ffn_w_up, v_ffn_w_down)))
    shard_shapes = {n: a.shape for n, a in shards.items()}

    segs, rows = _segments(shard_shapes, True)
    payload = _pack({n: (lax.bitcast_convert_type(a, BF16) if n in F32_PAYLOAD else a.astype(BF16)).reshape(-1)
                     for n, a in shards.items()}, segs, rows, BF16)
    gathered = all_gather(payload, name="weights_all_gather").reshape(N_DEV, rows, PACK_COLS)
    whole = {}
    for n in WEIGHTS:
        piece = _unpack(gathered, segs[n])
        if n in F32_PAYLOAD:
            piece = lax.bitcast_convert_type(piece.reshape(N_DEV, -1, 2), F32)
        whole[n] = _unshard(piece.reshape((N_DEV,) + shard_shapes[n]), SHARD_AXIS[n])

    loss, grad_x, grads = local_step(x[0], whole, loss_target[0])
    loss = lax.psum(loss[0, 0], ("x", "y", "c"))

    gsegs, grows = _segments(shard_shapes, False)
    parts = _pack({n: _shard_split(grads[n], SHARD_AXIS[n]).reshape(N_DEV, -1) for n in WEIGHTS}, gsegs, grows, BF16)
    received = exchange_shards(parts, name="grads_exchange")

    def packed(group):
        return _pack({n: group[n].reshape(-1) for n in WEIGHTS}, gsegs, grows, F32)

    outs = adamw(received, packed(shards), packed(moments_m), packed(moments_v), name="adamw")
    unpacked = [[_unpack(buf, gsegs[n]).reshape(shard_shapes[n]) for n in WEIGHTS] for buf in outs]
    return (loss, grad_x[None], *unpacked[0], *unpacked[1], *unpacked[2], *unpacked[3])
```

```python
import functools
import math
import types

import jax
import jax.numpy as jnp
from jax import lax
from jax.experimental import pallas as pl
from jax.experimental.pallas import tpu as pltpu

F32 = jnp.float32
BF16 = jnp.bfloat16

N_DEV = 8
SB_HEADS = 16
GLA_HEADS = 4
GLA_CHUNK = 64
GLA_GATE_RANK = 16
GLA_GATE_NORMALIZER = 16.0
CONV_WIDTH = 3
DEPTH = 4
RMS_EPS = 1e-6
ADAM_LR = 0.001
ADAM_B1 = 0.9
ADAM_B2 = 0.999
ADAM_EPS = 1e-08
ADAM_WD = 0.01
ADAM_STEP = 10

LANES = 128
SB_BLOCK = 256
VMEM_LIMIT_BYTES = 56 * 1024 * 1024
MM_TM, MM_TN, MM_TK = 512, 1024, 1024
PACK_COLS = 1024
PACK_ROW_ALIGN = 16
PACK_ROW_BLOCK = 128

WEIGHTS = ['norm_gains', 'sb_w_qkv', 'sb_w_o', 'conv_w_in', 'conv_w', 'conv_w_out', 'gla_w_in',
           'gla_w_gate_up', 'gla_b_gate', 'gla_head_norm', 'gla_w_o', 'ffn_w_up', 'ffn_w_down']
SHARD_AXIS = {'norm_gains': 2, 'sb_w_qkv': 2, 'sb_w_o': 1, 'conv_w_in': 2, 'conv_w': 2, 'conv_w_out': 1,
              'gla_w_in': 2, 'gla_w_gate_up': 2, 'gla_b_gate': 1, 'gla_head_norm': 2, 'gla_w_o': 1,
              'ffn_w_up': 2, 'ffn_w_down': 1}
F32_PAYLOAD = ('norm_gains', 'conv_w', 'gla_b_gate', 'gla_head_norm')

_NN = (((1,), (0,)), ((), ()))
_NT = (((1,), (1,)), ((), ()))
_TN = (((0,), (0,)), ((), ()))
_DIMS = {'nn': _NN, 'nt': _NT, 'tn': _TN}


def _params(*semantics):
    return pltpu.CompilerParams(dimension_semantics=semantics, vmem_limit_bytes=VMEM_LIMIT_BYTES)


def _dot(a, b, dims=_NN):
    return lax.dot_general(a.astype(BF16), b.astype(BF16), dims, preferred_element_type=F32)


def _split_hi_lo(x):
    hi = x.astype(BF16)
    lo = (x - hi.astype(F32)).astype(BF16)
    return hi, lo


def _dot_exact_rhs(x, ones_mat, dims=_NN):
    hi, lo = _split_hi_lo(x)
    return (lax.dot_general(hi, ones_mat, dims, preferred_element_type=F32)
            + lax.dot_general(lo, ones_mat, dims, preferred_element_type=F32))


def _dot_exact_lhs(ones_mat, x, dims=_NN):
    hi, lo = _split_hi_lo(x)
    return (lax.dot_general(ones_mat, hi, dims, preferred_element_type=F32)
            + lax.dot_general(ones_mat, lo, dims, preferred_element_type=F32))


def _log_sigmoid(z):
    return jnp.minimum(z, 0.0) - jnp.log(1.0 + jnp.exp(-jnp.abs(z)))


def _sigmoid(z):
    return 1.0 / (1.0 + jnp.exp(-z))


def matmul(a, b, *, mode, name, out_dtype=F32, epi=None, extra=None, tm=MM_TM, tn=MM_TN, tk=MM_TK):
    if mode == 'nn':
        (m, k), (k2, n) = a.shape, b.shape
    elif mode == 'nt':
        (m, k), (n, k2) = a.shape, b.shape
    else:
        (k, m), (k2, n) = a.shape, b.shape
    assert k == k2, (a.shape, b.shape, mode)
    tm, tn, tk = min(tm, m), min(tn, n), min(tk, k)
    assert m % tm == 0 and n % tn == 0 and k % tk == 0, (a.shape, b.shape, mode)
    nk = k // tk
    if mode == 'tn':
        a_spec = pl.BlockSpec((tk, tm), lambda i, j, kk: (kk, i))
    else:
        a_spec = pl.BlockSpec((tm, tk), lambda i, j, kk: (i, kk))
    if mode == 'nt':
        b_spec = pl.BlockSpec((tn, tk), lambda i, j, kk: (j, kk))
    else:
        b_spec = pl.BlockSpec((tk, tn), lambda i, j, kk: (kk, j))
    in_specs, operands = [a_spec, b_spec], [a, b]
    if epi == 'logsig16':
        in_specs.append(pl.BlockSpec((1, tn), lambda i, j, kk: (0, j)))
        operands.append(extra)
    elif epi in ('mul2relu', 'add'):
        in_specs.append(pl.BlockSpec((tm, tn), lambda i, j, kk: (i, j)))
        operands.append(extra)

    n_extra = len(operands) - 2
    pair = epi == 'relu2_pair'

    def body(a_ref, b_ref, *rest):
        e_ref = rest[0] if n_extra else None
        outs = rest[n_extra:n_extra + (2 if pair else 1)]

        def finish(r):
            if epi == 'mul2relu':
                r = r * (2.0 * jnp.maximum(e_ref[...], 0.0))
            elif epi == 'add':
                r = r + e_ref[...]
            elif epi == 'logsig16':
                r = _log_sigmoid(r + e_ref[...]) / GLA_GATE_NORMALIZER
            outs[0][...] = r.astype(outs[0].dtype)
            if pair:
                outs[1][...] = jnp.square(jnp.maximum(r, 0.0)).astype(outs[1].dtype)

        part = _dot(a_ref[...], b_ref[...], _DIMS[mode])
        if nk == 1:
            finish(part)
        else:
            acc_ref = rest[-1]
            kk = pl.program_id(2)

            @pl.when(kk == 0)
            def _():
                acc_ref[...] = part

            @pl.when(kk > 0)
            def _():
                acc_ref[...] += part

            @pl.when(kk == nk - 1)
            def _():
                finish(acc_ref[...])

    tile = pl.BlockSpec((tm, tn), lambda i, j, kk: (i, j))
    shape = jax.ShapeDtypeStruct((m, n), out_dtype)
    return pl.pallas_call(
        body, name=name, grid=(m // tm, n // tn, nk), in_specs=in_specs,
        out_specs=[tile, tile] if pair else tile,
        out_shape=[shape, jax.ShapeDtypeStruct((m, n), BF16)] if pair else shape,
        scratch_shapes=[pltpu.VMEM((tm, tn), F32)] if nk > 1 else [],
        compiler_params=_params("parallel", "parallel", "arbitrary"),
    )(*operands)


def rms_fwd(x, gain, *, name, res=None, out_dtype=F32, tm=512):
    t, d = x.shape
    tm = min(tm, t)
    row = pl.BlockSpec((tm, d), lambda i: (i, 0))
    in_specs, operands = [row, pl.BlockSpec((1, d), lambda i: (0, 0))], [x, gain]
    if res is not None:
        in_specs.append(row)
        operands.append(res)

    def body(x_ref, g_ref, *rest):
        xv = x_ref[...]
        r = lax.rsqrt(jnp.mean(xv * xv, axis=-1, keepdims=True) + RMS_EPS)
        y = xv * r * g_ref[...]
        if res is not None:
            y = rest[0][...] + y
        rest[-1][...] = y.astype(out_dtype)

    return pl.pallas_call(
        body, name=name, grid=(t // tm,), in_specs=in_specs, out_specs=row,
        out_shape=jax.ShapeDtypeStruct((t, d), out_dtype), compiler_params=_params("parallel"),
    )(*operands)


def rms_bwd(x, gain, dy, *, name, dres=None, out_dtype=F32, tm=512):
    t, d = x.shape
    tm = min(tm, t)
    row = pl.BlockSpec((tm, d), lambda i: (i, 0))
    vec = pl.BlockSpec((1, d), lambda i: (0, 0))
    in_specs, operands = [row, vec, row], [x, gain, dy]
    if dres is not None:
        in_specs.append(row)
        operands.append(dres)

    def body(x_ref, g_ref, dy_ref, *rest):
        dx_ref, dg_ref = rest[-2], rest[-1]

        @pl.when(pl.program_id(0) == 0)
        def _():
            dg_ref[...] = jnp.zeros_like(dg_ref)

        xv, dyv = x_ref[...], dy_ref[...]
        r = lax.rsqrt(jnp.mean(xv * xv, axis=-1, keepdims=True) + RMS_EPS)
        u = dyv * g_ref[...]
        dx = r * u - xv * (r * r * r * jnp.mean(u * xv, axis=-1, keepdims=True))
        if dres is not None:
            dx = rest[0][...] + dx
        dx_ref[...] = dx.astype(out_dtype)
        dg_ref[...] += jnp.sum(dyv * xv * r, axis=0, keepdims=True)

    return pl.pallas_call(
        body, name=name, grid=(t // tm,), in_specs=in_specs, out_specs=[row, vec],
        out_shape=[jax.ShapeDtypeStruct((t, d), out_dtype), jax.ShapeDtypeStruct((1, d), F32)],
        compiler_params=_params("arbitrary"),
    )(*operands)


def loss_head(y, target, *, name, tm=512):
    t, d = y.shape
    tm = min(tm, t)
    nt = t // tm
    row = pl.BlockSpec((tm, d), lambda i: (i, 0))

    def body(y_ref, t_ref, loss_ref, dy_ref, acc_ref):
        i = pl.program_id(0)

        @pl.when(i == 0)
        def _():
            acc_ref[...] = jnp.zeros_like(acc_ref)

        err = y_ref[...] - t_ref[...]
        dy_ref[...] = err * (1.0 / d)
        acc_ref[...] += jnp.sum(err * err, axis=0, keepdims=True)

        @pl.when(i == nt - 1)
        def _():
            loss_ref[...] = jnp.sum(acc_ref[...], axis=1, keepdims=True) * (0.5 / d)

    return pl.pallas_call(
        body, name=name, grid=(nt,), in_specs=[row, row],
        out_specs=[pl.BlockSpec((1, 1), lambda i: (0, 0)), row],
        out_shape=[jax.ShapeDtypeStruct((1, 1), F32), jax.ShapeDtypeStruct((t, d), F32)],
        scratch_shapes=[pltpu.VMEM((1, d), F32)], compiler_params=_params("arbitrary"),
    )(y, target)


def _sb_block_iota():
    rows = lax.broadcasted_iota(jnp.int32, (SB_BLOCK, SB_BLOCK), 0)
    cols = lax.broadcasted_iota(jnp.int32, (SB_BLOCK, SB_BLOCK), 1)
    return rows, cols


def _sb_logits(q_h, k_blk, mask):
    z = _dot(q_h, k_blk, _NT)
    ls = _log_sigmoid(z)
    lm = ls - z
    if mask is not None:
        lm = jnp.where(mask, lm, 0.0)
    return ls, lm, jnp.sum(lm, axis=1, keepdims=True)


def _sb_weights(ls, lm, mask, tri_strict, later):
    suffix = _dot_exact_rhs(lm, tri_strict)
    w = jnp.exp(ls + suffix + later)
    return w if mask is None else jnp.where(mask, w, 0.0)


def _on_grid_step(p, i):
    return jnp.logical_and(pl.program_id(0) == p, pl.program_id(1) == i)


def sb_fwd(qkv, *, name, gather=None):
    t, d3 = qkv.shape
    d = d3 // 3
    head_dim = d // SB_HEADS
    assert 2 * head_dim == LANES and t % SB_BLOCK == 0
    pairs = d // LANES
    nq = t // SB_BLOCK
    scale = head_dim ** -0.5

    def body(q_ref, k_ref, v_ref, *rest):
        if gather is None:
            compute(q_ref, k_ref, v_ref, *rest)
            return
        x_ref, o_ref, tot_ref, out_ref, send_sems, recv_sems, local_sem = rest
        start, forward, finish = _gather_plan(x_ref, out_ref, send_sems, recv_sems, local_sem)
        pl.when(_on_grid_step(0, 0))(start)
        pl.when(_on_grid_step(pairs // 2, 0))(forward)
        compute(q_ref, k_ref, v_ref, o_ref, tot_ref)
        pl.when(_on_grid_step(pairs - 1, nq - 1))(finish)

    def compute(q_ref, k_ref, v_ref, o_ref, tot_ref):
        qi = pl.program_id(1)
        lane = lax.broadcasted_iota(jnp.int32, (SB_BLOCK, LANES), 1)
        first = lane < head_dim
        q = q_ref[...] * scale
        q2 = jnp.concatenate([jnp.where(first, q, jnp.zeros_like(q)), jnp.where(first, jnp.zeros_like(q), q)], axis=0)
        rows, cols = _sb_block_iota()
        tri = jnp.where(rows > cols, 1.0, 0.0).astype(BF16)
        earlier = cols < rows
        diagonal = jnp.concatenate([earlier, earlier], axis=0)

        def step(kb, carry, mask):
            ks = pl.multiple_of(kb * SB_BLOCK, SB_BLOCK)
            k_blk = k_ref[pl.ds(ks, SB_BLOCK), :]
            v_blk = v_ref[pl.ds(ks, SB_BLOCK), :]
            acc, later = carry
            ls, lm, row = _sb_logits(q2, k_blk, mask)
            w = _sb_weights(ls, lm, mask, tri, later)
            return acc + _dot(w, v_blk), later + row

        out = step(qi, (jnp.zeros((2 * SB_BLOCK, LANES), F32), jnp.zeros((2 * SB_BLOCK, 1), F32)), diagonal)
        acc, total = lax.fori_loop(0, qi, lambda i, carry: step(qi - 1 - i, carry, None), out)
        o_ref[...] = jnp.where(first, acc[:SB_BLOCK], acc[SB_BLOCK:]).astype(o_ref.dtype)
        tot_ref[...] = jnp.where(first, total[:SB_BLOCK], total[SB_BLOCK:])

    blk = lambda off: pl.BlockSpec((t, LANES), lambda p, i: (0, off + p))
    qblk = pl.BlockSpec((SB_BLOCK, LANES), lambda p, i: (i, p))
    in_specs, operands = [qblk, blk(pairs), blk(2 * pairs)], [qkv, qkv, qkv]
    out_specs = [qblk, qblk]
    out_shape = [jax.ShapeDtypeStruct((t, d), BF16), jax.ShapeDtypeStruct((t, d), F32)]
    if gather is None:
        return pl.pallas_call(
            body, name=name, grid=(pairs, nq), in_specs=in_specs, out_specs=out_specs, out_shape=out_shape,
            compiler_params=_params("parallel", "arbitrary"),
        )(*operands)
    whole = pl.BlockSpec(memory_space=pl.ANY)
    return pl.pallas_call(
        body, name=name, grid=(pairs, nq), in_specs=in_specs + [whole], out_specs=out_specs + [whole],
        out_shape=out_shape + [jax.ShapeDtypeStruct((N_DEV * gather.shape[0], gather.shape[1]), gather.dtype)],
        scratch_shapes=_comm_scratch(), compiler_params=_params("arbitrary", "arbitrary"),
    )(*operands, gather)


def sb_bwd(qkv, tot, do, *, name, exchange=None):
    t, d3 = qkv.shape
    d = d3 // 3
    head_dim = d // SB_HEADS
    pairs = d // LANES
    nq = t // SB_BLOCK
    scale = head_dim ** -0.5

    def body(*refs):
        if exchange is None:
            compute(*refs)
            return
        q_ref, k_ref, v_ref, tot_ref, do_ref, in_ref, dq_ref, dk_ref, dv_ref, out_ref = refs[:10]
        dk_sum, dv_sum, send_sems, recv_sems, local_sem = refs[10:]
        start, finish = _exchange_plan(in_ref, out_ref, send_sems, recv_sems, local_sem)
        pl.when(_on_grid_step(0, 0))(start)
        compute(q_ref, k_ref, v_ref, tot_ref, do_ref, dq_ref, dk_ref, dv_ref, dk_sum, dv_sum)
        pl.when(_on_grid_step(pairs - 1, nq - 1))(finish)

    def compute(q_ref, k_ref, v_ref, tot_ref, do_ref, dq_ref, dk_ref, dv_ref, dk_sum, dv_sum):
        qi = pl.program_id(1)

        @pl.when(qi == 0)
        def _():
            dk_sum[...] = jnp.zeros_like(dk_sum)
            dv_sum[...] = jnp.zeros_like(dv_sum)

        lane = lax.broadcasted_iota(jnp.int32, (SB_BLOCK, LANES), 1)
        first = lane < head_dim
        q, dov, totv = q_ref[...] * scale, do_ref[...], tot_ref[...]
        second = jnp.logical_not(first)
        q2 = jnp.concatenate([jnp.where(s, q, jnp.zeros_like(q)) for s in (first, second)], axis=0)
        do2 = jnp.concatenate([jnp.where(s, dov, jnp.zeros_like(dov)) for s in (first, second)], axis=0)
        tot2 = jnp.concatenate([totv[:, 0:1], totv[:, head_dim:head_dim + 1]], axis=0)
        rows, cols = _sb_block_iota()
        tri_strict = jnp.where(rows > cols, 1.0, 0.0).astype(BF16)
        tri_before = jnp.where(rows < cols, 1.0, 0.0).astype(BF16)
        earlier = cols < rows
        diagonal = jnp.concatenate([earlier, earlier], axis=0)

        def step(kb, carry, mask):
            ks = pl.multiple_of(kb * SB_BLOCK, SB_BLOCK)
            k_blk = k_ref[pl.ds(ks, SB_BLOCK), :]
            v_blk = v_ref[pl.ds(ks, SB_BLOCK), :]
            dq, seen, before = carry
            ls, lm, row = _sb_logits(q2, k_blk, mask)
            seen = seen + row
            w = _sb_weights(ls, lm, mask, tri_strict, tot2 - seen)
            da = _dot(do2, v_blk, _NT) * w
            g = _dot_exact_rhs(da, tri_before) + before
            dz = da - jnp.exp(ls) * (da + g)
            if mask is not None:
                dz = jnp.where(mask, dz, 0.0)
            dk_sum[pl.ds(ks, SB_BLOCK), :] += _dot(dz, q2, _TN)
            dv_sum[pl.ds(ks, SB_BLOCK), :] += _dot(w, do2, _TN)
            return dq + _dot(dz, k_blk * scale), seen, before + jnp.sum(da, axis=1, keepdims=True)

        zero = jnp.zeros((2 * SB_BLOCK, LANES), F32)
        zcol = jnp.zeros((2 * SB_BLOCK, 1), F32)
        out = lax.fori_loop(0, qi, lambda kb, carry: step(kb, carry, None), (zero, zcol, zcol))
        dq = step(qi, out, diagonal)[0]
        dq_ref[...] = jnp.where(first, dq[:SB_BLOCK], dq[SB_BLOCK:]).astype(dq_ref.dtype)

        @pl.when(qi == nq - 1)
        def _():
            dk_ref[...] = dk_sum[...].astype(dk_ref.dtype)
            dv_ref[...] = dv_sum[...].astype(dv_ref.dtype)

    qblk = pl.BlockSpec((SB_BLOCK, LANES), lambda p, i: (i, p))
    col = lambda off: pl.BlockSpec((t, LANES), lambda p, i: (0, off + p))
    shape = jax.ShapeDtypeStruct((t, d), BF16)
    in_specs, operands = [qblk, col(pairs), col(2 * pairs), qblk, qblk], [qkv, qkv, qkv, tot, do]
    out_specs, out_shape = [qblk, col(0), col(0)], [shape, shape, shape]
    sums = [pltpu.VMEM((t, LANES), F32), pltpu.VMEM((t, LANES), F32)]
    if exchange is None:
        return pl.pallas_call(
            body, name=name, grid=(pairs, nq), in_specs=in_specs, out_specs=out_specs, out_shape=out_shape,
            scratch_shapes=sums, compiler_params=_params("parallel", "arbitrary"),
        )(*operands)
    whole = pl.BlockSpec(memory_space=pl.ANY)
    return pl.pallas_call(
        body, name=name, grid=(pairs, nq), in_specs=in_specs + [whole], out_specs=out_specs + [whole],
        out_shape=out_shape + [jax.ShapeDtypeStruct(exchange.shape, exchange.dtype)],
        scratch_shapes=sums + _comm_scratch(), compiler_params=_params("arbitrary", "arbitrary"),
    )(*operands, exchange)


def _shift_down(x, s):
    rows = lax.broadcasted_iota(jnp.int32, x.shape, 0)
    return jnp.where(rows >= s, pltpu.roll(x, s, 0), 0.0)


def _shift_up(x, s):
    t = x.shape[0]
    rows = lax.broadcasted_iota(jnp.int32, x.shape, 0)
    return jnp.where(rows < t - s, pltpu.roll(x, t - s, 0), 0.0)


def conv_fwd(bcu, w, *, name):
    t, d3 = bcu.shape
    d = d3 // 3
    nb = d // LANES
    col = lambda off: pl.BlockSpec((t, LANES), lambda j: (0, off + j))

    def body(b_ref, c_ref, u_ref, w_ref, y_ref):
        hh = c_ref[...] * u_ref[...]
        conv = w_ref[0:1, :] * _shift_down(hh, 2) + w_ref[1:2, :] * _shift_down(hh, 1) + w_ref[2:3, :] * hh
        y_ref[...] = (b_ref[...] * conv).astype(y_ref.dtype)

    return pl.pallas_call(
        body, name=name, grid=(nb,),
        in_specs=[col(0), col(nb), col(2 * nb), pl.BlockSpec((CONV_WIDTH, LANES), lambda j: (0, j))],
        out_specs=col(0), out_shape=jax.ShapeDtypeStruct((t, d), BF16), compiler_params=_params("parallel"),
    )(bcu, bcu, bcu, w)


def conv_bwd(bcu, w, dy, *, name):
    t, d3 = bcu.shape
    d = d3 // 3
    nb = d // LANES
    col = lambda off: pl.BlockSpec((t, LANES), lambda j: (0, off + j))
    wspec = pl.BlockSpec((CONV_WIDTH, LANES), lambda j: (0, j))

    def body(b_ref, c_ref, u_ref, w_ref, dy_ref, db_ref, dc_ref, du_ref, dw_ref):
        c, u, dyv = c_ref[...], u_ref[...], dy_ref[...]
        hh = c * u
        h2, h1 = _shift_down(hh, 2), _shift_down(hh, 1)
        w0, w1, w2 = w_ref[0:1, :], w_ref[1:2, :], w_ref[2:3, :]
        db_ref[...] = (dyv * (w0 * h2 + w1 * h1 + w2 * hh)).astype(db_ref.dtype)
        dconv = dyv * b_ref[...]
        dhh = w2 * dconv + w1 * _shift_up(dconv, 1) + w0 * _shift_up(dconv, 2)
        dc_ref[...] = (dhh * u).astype(dc_ref.dtype)
        du_ref[...] = (dhh * c).astype(du_ref.dtype)
        dw_ref[0:1, :] = jnp.sum(dconv * h2, axis=0, keepdims=True)
        dw_ref[1:2, :] = jnp.sum(dconv * h1, axis=0, keepdims=True)
        dw_ref[2:3, :] = jnp.sum(dconv * hh, axis=0, keepdims=True)

    shape = jax.ShapeDtypeStruct((t, d), BF16)
    return pl.pallas_call(
        body, name=name, grid=(nb,),
        in_specs=[col(0), col(nb), col(2 * nb), wspec, col(0)],
        out_specs=[col(0), col(0), col(0), wspec],
        out_shape=[shape, shape, shape, jax.ShapeDtypeStruct((CONV_WIDTH, d), F32)],
        compiler_params=_params("parallel"),
    )(bcu, bcu, bcu, w, dy)


def _gla_chunk(q_ref, k_ref, lg_ref, scale):
    c = GLA_CHUNK
    rows = lax.broadcasted_iota(jnp.int32, (c, c), 0)
    cols = lax.broadcasted_iota(jnp.int32, (c, c), 1)
    causal = rows >= cols
    tril = jnp.where(causal, 1.0, 0.0).astype(BF16)
    q = q_ref[...] * scale
    k = k_ref[...]
    cum = _dot_exact_lhs(tril, lg_ref[...])
    last = cum[c - 1:c, :]
    eq = jnp.exp(cum)
    ek = jnp.exp(-cum)
    el = jnp.exp(last - cum)
    return causal, tril, q, k, cum, last, eq, ek, el


def gla_fwd(proj, lg, *, name):
    t, d3 = proj.shape
    d = d3 // 3
    dk, dv = d // 2 // GLA_HEADS, d // GLA_HEADS
    assert dk == LANES and dv == 2 * LANES
    c = GLA_CHUNK
    nc = t // c
    scale = dk ** -0.5
    nh = GLA_HEADS

    def body(q_ref, k_ref, v_ref, lg_ref, o_ref, st_out_ref, st_ref):
        @pl.when(pl.program_id(1) == 0)
        def _():
            st_ref[...] = jnp.zeros_like(st_ref)

        causal, _, q, k, _, last, eq, ek, el = _gla_chunk(q_ref, k_ref, lg_ref, scale)
        v = v_ref[...]
        st = st_ref[...]
        st_out_ref[...] = st
        qt = q * eq
        scores = jnp.where(causal, _dot(qt, k * ek, _NT), 0.0)
        o_ref[...] = _dot(qt, st, _NT) + _dot(scores, v)
        st_ref[...] = st * jnp.exp(last) + _dot(v, k * el, _TN)

    return pl.pallas_call(
        body, name=name, grid=(nh, nc),
        in_specs=[pl.BlockSpec((c, dk), lambda h, i: (i, h)), pl.BlockSpec((c, dk), lambda h, i: (i, nh + h)),
                  pl.BlockSpec((c, dv), lambda h, i: (i, nh + h)), pl.BlockSpec((c, dk), lambda h, i: (i, h))],
        out_specs=[pl.BlockSpec((c, dv), lambda h, i: (i, h)),
                   pl.BlockSpec((None, None, dv, dk), lambda h, i: (h, i, 0, 0))],
        out_shape=[jax.ShapeDtypeStruct((t, d), F32), jax.ShapeDtypeStruct((nh, nc, dv, dk), F32)],
        scratch_shapes=[pltpu.VMEM((dv, dk), F32)], compiler_params=_params("parallel", "arbitrary"),
    )(proj, proj, proj, lg)


def gla_bwd(proj, lg, states, do, *, name):
    t, d3 = proj.shape
    d = d3 // 3
    dk, dv = d // 2 // GLA_HEADS, d // GLA_HEADS
    c = GLA_CHUNK
    nc = t // c
    scale = dk ** -0.5
    nh = GLA_HEADS

    def body(q_ref, k_ref, v_ref, lg_ref, st_ref, do_ref, dq_ref, dk_ref, dv_ref, dzg_ref, dbg_ref, dst_ref):
        @pl.when(pl.program_id(1) == 0)
        def _():
            dst_ref[...] = jnp.zeros_like(dst_ref)
            dbg_ref[...] = jnp.zeros_like(dbg_ref)

        causal, tril, q, k, _, last, eq, ek, el = _gla_chunk(q_ref, k_ref, lg_ref, scale)
        v, st, dov, dst = v_ref[...], st_ref[...], do_ref[...], dst_ref[...]
        qt, kt, kh = q * eq, k * ek, k * el
        scores = jnp.where(causal, _dot(qt, kt, _NT), 0.0)
        dscores = jnp.where(causal, _dot(dov, v, _NT), 0.0)
        dqt = _dot(dov, st) + _dot(dscores, kt)
        dkt = _dot(dscores, qt, _TN)
        dkh = _dot(v, dst)
        dv_ref[...] = (_dot(scores, dov, _TN) + _dot(kh, dst, _NT)).astype(dv_ref.dtype)
        dq_ref[...] = (dqt * eq * scale).astype(dq_ref.dtype)
        dk_ref[...] = (dkt * ek + dkh * el).astype(dk_ref.dtype)
        kh_dkh = kh * dkh
        e_last = jnp.exp(last)
        dlast = jnp.sum(kh_dkh, axis=0, keepdims=True) + e_last * jnp.sum(dst * st, axis=0, keepdims=True)
        dcum = qt * dqt - kt * dkt - kh_dkh
        dlg = _dot_exact_lhs(tril, dcum, _TN) + dlast
        lgv = lg_ref[...]
        dzg = dlg * (1.0 - jnp.exp(lgv * GLA_GATE_NORMALIZER)) / GLA_GATE_NORMALIZER
        dzg_ref[...] = dzg.astype(dzg_ref.dtype)
        dbg_ref[...] += jnp.sum(dzg, axis=0, keepdims=True)
        dst_ref[...] = dst * e_last + _dot(dov, qt, _TN)

    rev = lambda i: nc - 1 - i
    half = jax.ShapeDtypeStruct((t, d // 2), BF16)
    return pl.pallas_call(
        body, name=name, grid=(nh, nc),
        in_specs=[pl.BlockSpec((c, dk), lambda h, i: (rev(i), h)), pl.BlockSpec((c, dk), lambda h, i: (rev(i), nh + h)),
                  pl.BlockSpec((c, dv), lambda h, i: (rev(i), nh + h)), pl.BlockSpec((c, dk), lambda h, i: (rev(i), h)),
                  pl.BlockSpec((None, None, dv, dk), lambda h, i: (h, rev(i), 0, 0)),
                  pl.BlockSpec((c, dv), lambda h, i: (rev(i), h))],
        out_specs=[pl.BlockSpec((c, dk), lambda h, i: (rev(i), h)), pl.BlockSpec((c, dk), lambda h, i: (rev(i), h)),
                   pl.BlockSpec((c, dv), lambda h, i: (rev(i), h)), pl.BlockSpec((c, dk), lambda h, i: (rev(i), h)),
                   pl.BlockSpec((1, dk), lambda h, i: (0, h))],
        out_shape=[half, half, jax.ShapeDtypeStruct((t, d), BF16), half, jax.ShapeDtypeStruct((1, d // 2), F32)],
        scratch_shapes=[pltpu.VMEM((dv, dk), F32)], compiler_params=_params("parallel", "arbitrary"),
    )(proj, proj, proj, lg, states, do)


def gla_post_fwd(o, proj, head_norm, *, name, tm=512):
    t, d = o.shape
    dv = d // GLA_HEADS
    tm = min(tm, t)

    def body(o_ref, g_ref, hn_ref, y_ref):
        for h in range(GLA_HEADS):
            sl = slice(h * dv, (h + 1) * dv)
            ov, gv = o_ref[:, sl], g_ref[:, sl]
            r = lax.rsqrt(jnp.mean(ov * ov, axis=-1, keepdims=True) + RMS_EPS)
            y_ref[:, sl] = ((ov * r * hn_ref[:, sl]) * (gv * _sigmoid(gv))).astype(y_ref.dtype)

    row = pl.BlockSpec((tm, d), lambda i: (i, 0))
    return pl.pallas_call(
        body, name=name, grid=(t // tm,),
        in_specs=[row, pl.BlockSpec((tm, d), lambda i: (i, 2)), pl.BlockSpec((1, d), lambda i: (0, 0))],
        out_specs=row, out_shape=jax.ShapeDtypeStruct((t, d), BF16), compiler_params=_params("parallel"),
    )(o, proj, head_norm)


def gla_post_bwd(o, proj, head_norm, dy, *, name, tm=512):
    t, d = o.shape
    dv = d // GLA_HEADS
    tm = min(tm, t)

    def body(o_ref, g_ref, hn_ref, dy_ref, do_ref, dg_ref, dhn_ref):
        @pl.when(pl.program_id(0) == 0)
        def _():
            dhn_ref[...] = jnp.zeros_like(dhn_ref)

        for h in range(GLA_HEADS):
            sl = slice(h * dv, (h + 1) * dv)
            ov, gv, dyv, hn = o_ref[:, sl], g_ref[:, sl], dy_ref[:, sl], hn_ref[:, sl]
            r = lax.rsqrt(jnp.mean(ov * ov, axis=-1, keepdims=True) + RMS_EPS)
            sg = _sigmoid(gv)
            silu = gv * sg
            on = ov * r * hn
            dg_ref[:, sl] = (dyv * on * (sg * (1.0 + gv * (1.0 - sg)))).astype(dg_ref.dtype)
            don = dyv * silu
            u = don * hn
            do_ref[:, sl] = (r * u - ov * (r * r * r * jnp.mean(u * ov, axis=-1, keepdims=True))).astype(do_ref.dtype)
            dhn_ref[:, sl] += jnp.sum(don * ov * r, axis=0, keepdims=True)

    row = pl.BlockSpec((tm, d), lambda i: (i, 0))
    vec = pl.BlockSpec((1, d), lambda i: (0, 0))
    shape = jax.ShapeDtypeStruct((t, d), BF16)
    return pl.pallas_call(
        body, name=name, grid=(t // tm,),
        in_specs=[row, pl.BlockSpec((tm, d), lambda i: (i, 2)), vec, row],
        out_specs=[row, row, vec], out_shape=[shape, shape, jax.ShapeDtypeStruct((1, d), F32)],
        compiler_params=_params("arbitrary"),
    )(o, proj, head_norm, dy)


def _ffn_fwd(h, gains, w_up, w_down, tag):
    xn = rms_fwd(h, gains[2], name=f"{tag}_ffn_norm", out_dtype=BF16)
    u, act = matmul(xn, w_up, mode='nn', epi='relu2_pair', name=f"{tag}_ffn_up")
    f = matmul(act, w_down, mode='nn', name=f"{tag}_ffn_down")
    h_out = rms_fwd(f, gains[3], res=h, name=f"{tag}_ffn_out")
    return h_out, (h, xn, u, act, f)


def _ffn_bwd(dh, saved, gains, w_up, w_down, tag):
    h, xn, u, act, f = saved
    df, dg3 = rms_bwd(f, gains[3], dh, out_dtype=BF16, name=f"{tag}_ffn_out_bwd")
    du = matmul(df, w_down, mode='nt', epi='mul2relu', extra=u, out_dtype=BF16, name=f"{tag}_ffn_da")
    dw_down = matmul(act, df, mode='tn', name=f"{tag}_ffn_dwdown")
    dw_up = matmul(xn, du, mode='tn', name=f"{tag}_ffn_dwup")
    dxn = matmul(du, w_up, mode='nt', name=f"{tag}_ffn_dxn")
    dh_in, dg2 = rms_bwd(h, gains[2], dxn, dres=dh, name=f"{tag}_ffn_norm_bwd")
    return dh_in, dg2, dg3, dw_up, dw_down


def _sb_layer_fwd(xn, w, j, tag, comm=None):
    qkv = matmul(xn, w['sb_w_qkv'][j], mode='nn', out_dtype=BF16, name=f"{tag}_qkv")
    if comm is None:
        o, tot = sb_fwd(qkv, name=f"{tag}_sb")
    else:
        o, tot, gathered = sb_fwd(qkv, name=f"{tag}_sb", gather=comm.rest_payload)
        comm.on_gathered(gathered)
    m = matmul(o, w['sb_w_o'][j], mode='nn', name=f"{tag}_wo")
    return m, (qkv, o, tot)


def _sb_layer_bwd(dm, xn, saved, w, j, tag, comm=None, grads=None):
    qkv, o, tot = saved
    do = matmul(dm, w['sb_w_o'][j], mode='nt', out_dtype=BF16, name=f"{tag}_do")
    dw_o = matmul(o, dm, mode='tn', name=f"{tag}_dwo")
    if comm is None:
        dq, dk, dv = sb_bwd(qkv, tot, do, name=f"{tag}_sb_bwd")
    else:
        dq, dk, dv, received = sb_bwd(qkv, tot, do, name=f"{tag}_sb_bwd", exchange=comm.rest_parts(grads))
        comm.on_received(received)
    dqkv = jnp.concatenate([dq, dk, dv], axis=1)
    dw_qkv = matmul(xn, dqkv, mode='tn', name=f"{tag}_dwqkv")
    dxn = matmul(dqkv, w['sb_w_qkv'][j], mode='nt', name=f"{tag}_dxn")
    return dxn, {('sb_w_qkv', j): dw_qkv, ('sb_w_o', j): dw_o}


def _conv_layer_fwd(xn, w, j, tag):
    bcu = matmul(xn, w['conv_w_in'][j], mode='nn', name=f"{tag}_in")
    y = conv_fwd(bcu, w['conv_w'][j], name=f"{tag}_conv")
    m = matmul(y, w['conv_w_out'][j], mode='nn', name=f"{tag}_out")
    return m, (bcu, y)


def _conv_layer_bwd(dm, xn, saved, w, j, tag):
    bcu, y = saved
    dy = matmul(dm, w['conv_w_out'][j], mode='nt', name=f"{tag}_dy")
    dw_out = matmul(y, dm, mode='tn', name=f"{tag}_dwout")
    db, dc, du, dw_conv = conv_bwd(bcu, w['conv_w'][j], dy, name=f"{tag}_conv_bwd")
    dbcu = jnp.concatenate([db, dc, du], axis=1)
    dw_in = matmul(xn, dbcu, mode='tn', name=f"{tag}_dwin")
    dxn = matmul(dbcu, w['conv_w_in'][j], mode='nt', name=f"{tag}_dxn")
    return dxn, {('conv_w_in', j): dw_in, ('conv_w', j): dw_conv, ('conv_w_out', j): dw_out}


def _gla_split(w_in, w_gate_up):
    d = w_in.shape[0]
    w_main = w_in[:, :3 * d]
    w_a = jnp.pad(w_in[:, 3 * d:], ((0, 0), (0, LANES - GLA_GATE_RANK)))
    w_gu = jnp.pad(w_gate_up, ((0, LANES - GLA_GATE_RANK), (0, 0)))
    return w_main, w_a, w_gu


def _gla_layer_fwd(xn, w, j, tag):
    w_main, w_a, w_gu = _gla_split(w['gla_w_in'][j], w['gla_w_gate_up'][j])
    proj = matmul(xn, w_main, mode='nn', name=f"{tag}_in")
    a_low = matmul(xn, w_a, mode='nn', out_dtype=BF16, name=f"{tag}_alow")
    lg = matmul(a_low, w_gu, mode='nn', epi='logsig16', extra=w['gla_b_gate'][j][None, :], name=f"{tag}_gate")
    o, states = gla_fwd(proj, lg, name=f"{tag}_gla")
    hn = w['gla_head_norm'][j].reshape(1, -1)
    y = gla_post_fwd(o, proj, hn, name=f"{tag}_post")
    m = matmul(y, w['gla_w_o'][j], mode='nn', name=f"{tag}_wo")
    return m, (proj, a_low, lg, o, states, y)


def _gla_layer_bwd(dm, xn, saved, w, j, tag):
    proj, a_low, lg, o, states, y = saved
    w_main, w_a, w_gu = _gla_split(w['gla_w_in'][j], w['gla_w_gate_up'][j])
    hn = w['gla_head_norm'][j].reshape(1, -1)
    dy = matmul(dm, w['gla_w_o'][j], mode='nt', name=f"{tag}_dy")
    dw_o = matmul(y, dm, mode='tn', name=f"{tag}_dwo")
    do, dg, dhn = gla_post_bwd(o, proj, hn, dy, name=f"{tag}_post_bwd")
    dq, dk, dv, dzg, dbg = gla_bwd(proj, lg, states, do, name=f"{tag}_gla_bwd")
    da_low = matmul(dzg, w_gu, mode='nt', out_dtype=BF16, name=f"{tag}_dalow")
    dw_gu = matmul(a_low, dzg, mode='tn', name=f"{tag}_dwgu")[:GLA_GATE_RANK]
    dproj = jnp.concatenate([dq, dk, dv, dg], axis=1)
    dw_main = matmul(xn, dproj, mode='tn', name=f"{tag}_dwin")
    dw_a = matmul(xn, da_low, mode='tn', name=f"{tag}_dwa")[:, :GLA_GATE_RANK]
    dxn_a = matmul(da_low, w_a, mode='nt', name=f"{tag}_dxn_a")
    dxn = matmul(dproj, w_main, mode='nt', epi='add', extra=dxn_a, name=f"{tag}_dxn")
    grads = {('gla_w_in', j): jnp.concatenate([dw_main, dw_a], axis=1), ('gla_w_gate_up', j): dw_gu,
             ('gla_b_gate', j): dbg[0], ('gla_head_norm', j): dhn.reshape(w['gla_head_norm'][j].shape),
             ('gla_w_o', j): dw_o}
    return dxn, grads


_MIXERS = ((_sb_layer_fwd, _sb_layer_bwd), (_conv_layer_fwd, _conv_layer_bwd), (_gla_layer_fwd, _gla_layer_bwd))


def local_step(x, w, target, comm=None):
    depth = len(w['norm_gains'])
    h = x
    tape = []
    for i in range(depth):
        kind, j = i % 3, i // 3
        tag = f"l{i}"
        extra = {'comm': comm} if (comm is not None and i == 0) else {}
        gains = [w['norm_gains'][i][s][None, :] for s in range(4)]
        xn = rms_fwd(h, gains[0], name=f"{tag}_mix_norm", out_dtype=BF16)
        m, saved = _MIXERS[kind][0](xn, w, j, tag, **extra)
        h_mid = rms_fwd(m, gains[1], res=h, name=f"{tag}_mix_out")
        h_out, ffn_saved = _ffn_fwd(h_mid, gains, w['ffn_w_up'][i], w['ffn_w_down'][i], tag)
        tape.append((h, xn, m, saved, ffn_saved, gains))
        h = h_out
    loss, dh = loss_head(h, target, name="loss_head")

    grads = {}
    for i in reversed(range(depth)):
        kind, j = i % 3, i // 3
        tag = f"l{i}"
        h_in, xn, m, saved, ffn_saved, gains = tape[i]
        dg = [None] * 4
        dh, dg[2], dg[3], grads[('ffn_w_up', i)], grads[('ffn_w_down', i)] = _ffn_bwd(
            dh, ffn_saved, gains, w['ffn_w_up'][i], w['ffn_w_down'][i], tag)
        dm, dg[1] = rms_bwd(m, gains[1], dh, out_dtype=BF16, name=f"{tag}_mix_out_bwd")
        extra = {'comm': comm, 'grads': grads} if (comm is not None and i == 0) else {}
        dxn, g = _MIXERS[kind][1](dm, xn, saved, w, j, tag, **extra)
        grads.update(g)
        dh, dg[0] = rms_bwd(h_in, gains[0], dxn, dres=dh, name=f"{tag}_mix_norm_bwd")
        grads[('norm_gains', i)] = jnp.concatenate(dg, axis=0)
    return loss, dh, grads


def _layer_of(name, j):
    kinds = {'sb_': 0, 'conv': 1, 'gla_': 2}
    return 3 * j + kinds[name[:4]] if name[:4] in kinds else j


def _segments(keys, shard_shapes, f32_as_pairs):
    segs, row = {}, 0
    for name, j in keys:
        n = math.prod(shard_shapes[name][1:]) * (2 if f32_as_pairs and name in F32_PAYLOAD else 1)
        nrows = -(-n // (PACK_COLS * PACK_ROW_ALIGN)) * PACK_ROW_ALIGN
        segs[(name, j)] = (row, nrows, n)
        row += nrows
    return segs, -(-row // PACK_ROW_BLOCK) * PACK_ROW_BLOCK


def _pack(parts, segs, total_rows, dtype):
    pieces, row = [], 0
    for key in segs:
        _, nrows, n = segs[key]
        p = parts[key].astype(dtype)
        lead = p.shape[:-1]
        p = jnp.pad(p, [(0, 0)] * len(lead) + [(0, nrows * PACK_COLS - n)])
        pieces.append(p.reshape(lead + (nrows, PACK_COLS)))
        row += nrows
    if total_rows > row:
        pieces.append(jnp.zeros(lead + (total_rows - row, PACK_COLS), dtype))
    return jnp.concatenate(pieces, axis=-2)


def _unpack(buf, seg):
    first, nrows, n = seg
    piece = buf[..., first:first + nrows, :]
    return piece.reshape(piece.shape[:-2] + (nrows * PACK_COLS,))[..., :n]


def _unshard(gathered, axis):
    moved = jnp.moveaxis(gathered, 0, axis)
    shape = moved.shape
    return moved.reshape(shape[:axis] + (shape[axis] * shape[axis + 1],) + shape[axis + 2:])


def _shard_split(full, axis):
    shape = full.shape
    cut = full.reshape(shape[:axis] + (N_DEV, shape[axis] // N_DEV) + shape[axis + 1:])
    return jnp.moveaxis(cut, axis, 0)


def _mesh_position():
    return lax.axis_index("x"), lax.axis_index("y"), lax.axis_index("c")


def _comm_scratch():
    return [pltpu.SemaphoreType.DMA((N_DEV - 1,)), pltpu.SemaphoreType.DMA((N_DEV - 1,)), pltpu.SemaphoreType.DMA]


def _gather_plan(x_ref, out_ref, send_sems, recv_sems, local_sem):
    rows = x_ref.shape[0]
    x, y, c = _mesh_position()
    me, sibling = (x, y, c), (x, y, 1 - c)
    chips = [(1 - x, y), (x, 1 - y), (1 - x, 1 - y)]

    def block(px, py, pc):
        return out_ref.at[pl.ds((4 * px + 2 * py + pc) * rows, rows), :]

    def copy(k, blk, to, src=None):
        return pltpu.make_async_remote_copy(
            src_ref=block(*blk) if src is None else src, dst_ref=block(*blk),
            send_sem=send_sems.at[k], recv_sem=recv_sems.at[k],
            device_id=to, device_id_type=pl.DeviceIdType.MESH)

    mine = pltpu.make_async_copy(x_ref, block(*me), local_sem)
    first = [copy(0, me, sibling, src=x_ref)]
    first += [copy(1 + j, me, (*chip, c), src=x_ref) for j, chip in enumerate(chips)]
    passed = [copy(4 + j, (*chip, c), sibling) for j, chip in enumerate(chips)]

    def start():
        mine.start()
        for cp in first:
            cp.start()

    def forward():
        for j, chip in enumerate(chips):
            copy(1 + j, (*chip, c), me).wait_recv()
            passed[j].start()

    def finish():
        copy(0, sibling, me).wait_recv()
        for j, chip in enumerate(chips):
            copy(4 + j, (*chip, 1 - c), me).wait_recv()
        for cp in first + passed:
            cp.wait_send()
        mine.wait()

    return start, forward, finish


def _exchange_plan(in_ref, out_ref, send_sems, recv_sems, local_sem):
    x, y, c = _mesh_position()
    my_id = 4 * x + 2 * y + c
    mine = pltpu.make_async_copy(in_ref.at[my_id], out_ref.at[my_id], local_sem)

    def copy(k, receive):
        px = 1 - x if k & 4 else x
        py = 1 - y if k & 2 else y
        pc = 1 - c if k & 1 else c
        peer_id = 4 * px + 2 * py + pc
        return pltpu.make_async_remote_copy(
            src_ref=in_ref.at[peer_id], dst_ref=out_ref.at[peer_id if receive else my_id],
            send_sem=send_sems.at[k - 1], recv_sem=recv_sems.at[k - 1],
            device_id=(px, py, pc), device_id_type=pl.DeviceIdType.MESH)

    def start():
        mine.start()
        for k in range(1, N_DEV):
            copy(k, False).start()

    def finish():
        for k in range(1, N_DEV):
            copy(k, True).wait_recv()
        for k in range(1, N_DEV):
            copy(k, False).wait_send()
        mine.wait()

    return start, finish


def all_gather(shard, *, name):
    rows, cols = shard.shape

    def body(x_ref, out_ref, send_sems, recv_sems, local_sem):
        start, forward, finish = _gather_plan(x_ref, out_ref, send_sems, recv_sems, local_sem)
        start()
        forward()
        finish()

    return pl.pallas_call(
        body, name=name, out_shape=jax.ShapeDtypeStruct((N_DEV * rows, cols), shard.dtype),
        in_specs=[pl.BlockSpec(memory_space=pl.ANY)], out_specs=pl.BlockSpec(memory_space=pl.ANY),
        scratch_shapes=_comm_scratch(),
    )(shard)


def exchange_shards(parts, *, name):
    def body(in_ref, out_ref, send_sems, recv_sems, local_sem):
        start, finish = _exchange_plan(in_ref, out_ref, send_sems, recv_sems, local_sem)
        start()
        finish()

    return pl.pallas_call(
        body, name=name, out_shape=jax.ShapeDtypeStruct(parts.shape, parts.dtype),
        in_specs=[pl.BlockSpec(memory_space=pl.ANY)], out_specs=pl.BlockSpec(memory_space=pl.ANY),
        scratch_shapes=_comm_scratch(),
    )(parts)


def adamw(parts, w, m, v, *, name):
    _, rows, cols = parts.shape
    tr = PACK_ROW_BLOCK
    c1 = 1.0 - ADAM_B1 ** ADAM_STEP
    c2 = 1.0 - ADAM_B2 ** ADAM_STEP

    def body(p_ref, w_ref, m_ref, v_ref, g_ref, d_ref, nm_ref, nv_ref):
        g = p_ref[0].astype(F32)
        for s in range(1, N_DEV):
            g = g + p_ref[s].astype(F32)
        nm = ADAM_B1 * m_ref[...] + (1.0 - ADAM_B1) * g
        nv = ADAM_B2 * v_ref[...] + (1.0 - ADAM_B2) * jnp.square(g)
        m_hat = nm / c1
        v_hat = nv / c2
        g_ref[...] = g
        d_ref[...] = -ADAM_LR * (m_hat / (jnp.sqrt(v_hat) + ADAM_EPS) + ADAM_WD * w_ref[...])
        nm_ref[...] = nm
        nv_ref[...] = nv

    row = pl.BlockSpec((tr, cols), lambda i: (i, 0))
    shape = jax.ShapeDtypeStruct((rows, cols), F32)
    return pl.pallas_call(
        body, name=name, grid=(rows // tr,),
        in_specs=[pl.BlockSpec((N_DEV, tr, cols), lambda i: (0, i, 0)), row, row, row],
        out_specs=[row, row, row, row], out_shape=[shape, shape, shape, shape],
        compiler_params=_params("parallel"),
    )(parts, w, m, v)


def kernel(x, norm_gains, sb_w_qkv, sb_w_o, conv_w_in, conv_w, conv_w_out, gla_w_in, gla_w_gate_up, gla_b_gate, gla_head_norm, gla_w_o, ffn_w_up, ffn_w_down, loss_target, m_norm_gains, m_sb_w_qkv, m_sb_w_o, m_conv_w_in, m_conv_w, m_conv_w_out, m_gla_w_in, m_gla_w_gate_up, m_gla_b_gate, m_gla_head_norm, m_gla_w_o, m_ffn_w_up, m_ffn_w_down, v_norm_gains, v_sb_w_qkv, v_sb_w_o, v_conv_w_in, v_conv_w, v_conv_w_out, v_gla_w_in, v_gla_w_gate_up, v_gla_b_gate, v_gla_head_norm, v_gla_w_o, v_ffn_w_up, v_ffn_w_down):
    shards = dict(zip(WEIGHTS, (norm_gains, sb_w_qkv, sb_w_o, conv_w_in, conv_w, conv_w_out, gla_w_in,
                                gla_w_gate_up, gla_b_gate, gla_head_norm, gla_w_o, ffn_w_up, ffn_w_down)))
    moments_m = dict(zip(WEIGHTS, (m_norm_gains, m_sb_w_qkv, m_sb_w_o, m_conv_w_in, m_conv_w, m_conv_w_out,
                                   m_gla_w_in, m_gla_w_gate_up, m_gla_b_gate, m_gla_head_norm, m_gla_w_o,
                                   m_ffn_w_up, m_ffn_w_down)))
    moments_v = dict(zip(WEIGHTS, (v_norm_gains, v_sb_w_qkv, v_sb_w_o, v_conv_w_in, v_conv_w, v_conv_w_out,
                                   v_gla_w_in, v_gla_w_gate_up, v_gla_b_gate, v_gla_head_norm, v_gla_w_o,
                                   v_ffn_w_up, v_ffn_w_down)))
    shard_shapes = {n: a.shape for n, a in shards.items()}

    keys = [(n, j) for n in WEIGHTS for j in range(shard_shapes[n][0])]
    groups = [[k for k in keys if _layer_of(*k) == 0], [k for k in keys if _layer_of(*k) != 0]]

    def payload(group):
        segs, rows = _segments(group, shard_shapes, True)
        flat = {(n, j): (lax.bitcast_convert_type(shards[n][j], BF16) if n in F32_PAYLOAD
                         else shards[n][j].astype(BF16)).reshape(-1) for n, j in group}
        return segs, _pack(flat, segs, rows, BF16)

    whole = {n: [None] * shard_shapes[n][0] for n in WEIGHTS}

    def take_gathered(segs, gathered):
        gathered = gathered.reshape(N_DEV, -1, PACK_COLS)
        for n, j in segs:
            piece = _unpack(gathered, segs[(n, j)])
            if n in F32_PAYLOAD:
                piece = lax.bitcast_convert_type(piece.reshape(N_DEV, -1, 2), F32)
            whole[n][j] = _unshard(piece.reshape((N_DEV,) + shard_shapes[n][1:]), SHARD_AXIS[n] - 1)

    segs0, payload0 = payload(groups[0])
    segs1, payload1 = payload(groups[1])
    take_gathered(segs0, all_gather(payload0, name="weights_all_gather"))

    gsegs = [_segments(group, shard_shapes, False) for group in groups]
    received = [None, None]

    def parts(g, grads):
        segs, rows = gsegs[g]
        return _pack({(n, j): _shard_split(grads[(n, j)], SHARD_AXIS[n] - 1).reshape(N_DEV, -1) for n, j in segs},
                     segs, rows, BF16)

    def on_received(buf):
        received[1] = buf

    comm = types.SimpleNamespace(rest_payload=payload1, on_gathered=functools.partial(take_gathered, segs1),
                                 rest_parts=functools.partial(parts, 1), on_received=on_received)

    loss, grad_x, grads = local_step(x[0], whole, loss_target[0], comm)
    loss = lax.psum(loss[0, 0], ("x", "y", "c"))
    received[0] = exchange_shards(parts(0, grads), name="grads_exchange")

    results = {}
    for g in range(2):
        segs, rows = gsegs[g]

        def packed(source):
            return _pack({(n, j): source[n][j].reshape(-1) for n, j in segs}, segs, rows, F32)

        outs = adamw(received[g], packed(shards), packed(moments_m), packed(moments_v), name=f"adamw{g}")
        for n, j in segs:
            results[(n, j)] = [_unpack(buf, segs[(n, j)]).reshape(shard_shapes[n][1:]) for buf in outs]
    stacked = [[jnp.stack([results[(n, j)][o] for j in range(shard_shapes[n][0])]) for n in WEIGHTS] for o in range(4)]
    return (loss, grad_x[None], *stacked[0], *stacked[1], *stacked[2], *stacked[3])
```

```python
import functools
import math
import types

import jax
import jax.numpy as jnp
from jax import lax
from jax.experimental import pallas as pl
from jax.experimental.pallas import tpu as pltpu

F32 = jnp.float32
BF16 = jnp.bfloat16

N_DEV = 8
SB_HEADS = 16
GLA_HEADS = 4
GLA_CHUNK = 64
GLA_GATE_RANK = 16
GLA_GATE_NORMALIZER = 16.0
CONV_WIDTH = 3
DEPTH = 4
RMS_EPS = 1e-6
ADAM_LR = 0.001
ADAM_B1 = 0.9
ADAM_B2 = 0.999
ADAM_EPS = 1e-08
ADAM_WD = 0.01
ADAM_STEP = 10

LANES = 128
SB_BLOCK = 256
VMEM_LIMIT_BYTES = 56 * 1024 * 1024
MM_TM, MM_TN, MM_TK = 512, 1024, 1024
PACK_COLS = 1024
PACK_ROW_ALIGN = 16
PACK_ROW_BLOCK = 128

WEIGHTS = ['norm_gains', 'sb_w_qkv', 'sb_w_o', 'conv_w_in', 'conv_w', 'conv_w_out', 'gla_w_in',
           'gla_w_gate_up', 'gla_b_gate', 'gla_head_norm', 'gla_w_o', 'ffn_w_up', 'ffn_w_down']
SHARD_AXIS = {'norm_gains': 2, 'sb_w_qkv': 2, 'sb_w_o': 1, 'conv_w_in': 2, 'conv_w': 2, 'conv_w_out': 1,
              'gla_w_in': 2, 'gla_w_gate_up': 2, 'gla_b_gate': 1, 'gla_head_norm': 2, 'gla_w_o': 1,
              'ffn_w_up': 2, 'ffn_w_down': 1}
F32_PAYLOAD = ('norm_gains', 'conv_w', 'gla_b_gate', 'gla_head_norm')

_NN = (((1,), (0,)), ((), ()))
_NT = (((1,), (1,)), ((), ()))
_TN = (((0,), (0,)), ((), ()))
_DIMS = {'nn': _NN, 'nt': _NT, 'tn': _TN}


def _params(*semantics):
    return pltpu.CompilerParams(dimension_semantics=semantics, vmem_limit_bytes=VMEM_LIMIT_BYTES)


def _dot(a, b, dims=_NN):
    return lax.dot_general(a.astype(BF16), b.astype(BF16), dims, preferred_element_type=F32)


def _split_hi_lo(x):
    hi = x.astype(BF16)
    lo = (x - hi.astype(F32)).astype(BF16)
    return hi, lo


def _dot_exact_rhs(x, ones_mat, dims=_NN):
    hi, lo = _split_hi_lo(x)
    return (lax.dot_general(hi, ones_mat, dims, preferred_element_type=F32)
            + lax.dot_general(lo, ones_mat, dims, preferred_element_type=F32))


def _dot_exact_lhs(ones_mat, x, dims=_NN):
    hi, lo = _split_hi_lo(x)
    return (lax.dot_general(ones_mat, hi, dims, preferred_element_type=F32)
            + lax.dot_general(ones_mat, lo, dims, preferred_element_type=F32))


def _log_sigmoid(z):
    return jnp.minimum(z, 0.0) - jnp.log(1.0 + jnp.exp(-jnp.abs(z)))


def _sigmoid(z):
    return 1.0 / (1.0 + jnp.exp(-z))


def matmul(a, b, *, mode, name, out_dtype=F32, epi=None, extra=None, tm=MM_TM, tn=MM_TN, tk=MM_TK):
    if mode == 'nn':
        (m, k), (k2, n) = a.shape, b.shape
    elif mode == 'nt':
        (m, k), (n, k2) = a.shape, b.shape
    else:
        (k, m), (k2, n) = a.shape, b.shape
    assert k == k2, (a.shape, b.shape, mode)
    tm, tn, tk = min(tm, m), min(tn, n), min(tk, k)
    assert m % tm == 0 and n % tn == 0 and k % tk == 0, (a.shape, b.shape, mode)
    nk = k // tk
    if mode == 'tn':
        a_spec = pl.BlockSpec((tk, tm), lambda i, j, kk: (kk, i))
    else:
        a_spec = pl.BlockSpec((tm, tk), lambda i, j, kk: (i, kk))
    if mode == 'nt':
        b_spec = pl.BlockSpec((tn, tk), lambda i, j, kk: (j, kk))
    else:
        b_spec = pl.BlockSpec((tk, tn), lambda i, j, kk: (kk, j))
    in_specs, operands = [a_spec, b_spec], [a, b]
    if epi == 'logsig16':
        in_specs.append(pl.BlockSpec((1, tn), lambda i, j, kk: (0, j)))
        operands.append(extra)
    elif epi in ('mul2relu', 'add'):
        in_specs.append(pl.BlockSpec((tm, tn), lambda i, j, kk: (i, j)))
        operands.append(extra)

    n_extra = len(operands) - 2
    pair = epi == 'relu2_pair'

    def body(a_ref, b_ref, *rest):
        e_ref = rest[0] if n_extra else None
        outs = rest[n_extra:n_extra + (2 if pair else 1)]

        def finish(r):
            if epi == 'mul2relu':
                r = r * (2.0 * jnp.maximum(e_ref[...], 0.0))
            elif epi == 'add':
                r = r + e_ref[...]
            elif epi == 'logsig16':
                r = _log_sigmoid(r + e_ref[...]) / GLA_GATE_NORMALIZER
            outs[0][...] = r.astype(outs[0].dtype)
            if pair:
                outs[1][...] = jnp.square(jnp.maximum(r, 0.0)).astype(outs[1].dtype)

        part = _dot(a_ref[...], b_ref[...], _DIMS[mode])
        if nk == 1:
            finish(part)
        else:
            acc_ref = rest[-1]
            kk = pl.program_id(2)

            @pl.when(kk == 0)
            def _():
                acc_ref[...] = part

            @pl.when(kk > 0)
            def _():
                acc_ref[...] += part

            @pl.when(kk == nk - 1)
            def _():
                finish(acc_ref[...])

    tile = pl.BlockSpec((tm, tn), lambda i, j, kk: (i, j))
    shape = jax.ShapeDtypeStruct((m, n), out_dtype)
    return pl.pallas_call(
        body, name=name, grid=(m // tm, n // tn, nk), in_specs=in_specs,
        out_specs=[tile, tile] if pair else tile,
        out_shape=[shape, jax.ShapeDtypeStruct((m, n), BF16)] if pair else shape,
        scratch_shapes=[pltpu.VMEM((tm, tn), F32)] if nk > 1 else [],
        compiler_params=_params("parallel", "parallel", "arbitrary"),
    )(*operands)


def rms_fwd(x, gain, *, name, res=None, out_dtype=F32, tm=512):
    t, d = x.shape
    tm = min(tm, t)
    row = pl.BlockSpec((tm, d), lambda i: (i, 0))
    in_specs, operands = [row, pl.BlockSpec((1, d), lambda i: (0, 0))], [x, gain]
    if res is not None:
        in_specs.append(row)
        operands.append(res)

    def body(x_ref, g_ref, *rest):
        xv = x_ref[...]
        r = lax.rsqrt(jnp.mean(xv * xv, axis=-1, keepdims=True) + RMS_EPS)
        y = xv * r * g_ref[...]
        if res is not None:
            y = rest[0][...] + y
        rest[-1][...] = y.astype(out_dtype)

    return pl.pallas_call(
        body, name=name, grid=(t // tm,), in_specs=in_specs, out_specs=row,
        out_shape=jax.ShapeDtypeStruct((t, d), out_dtype), compiler_params=_params("parallel"),
    )(*operands)


def rms_bwd(x, gain, dy, *, name, dres=None, out_dtype=F32, tm=512):
    t, d = x.shape
    tm = min(tm, t)
    row = pl.BlockSpec((tm, d), lambda i: (i, 0))
    vec = pl.BlockSpec((1, d), lambda i: (0, 0))
    in_specs, operands = [row, vec, row], [x, gain, dy]
    if dres is not None:
        in_specs.append(row)
        operands.append(dres)

    def body(x_ref, g_ref, dy_ref, *rest):
        dx_ref, dg_ref = rest[-2], rest[-1]

        @pl.when(pl.program_id(0) == 0)
        def _():
            dg_ref[...] = jnp.zeros_like(dg_ref)

        xv, dyv = x_ref[...], dy_ref[...]
        r = lax.rsqrt(jnp.mean(xv * xv, axis=-1, keepdims=True) + RMS_EPS)
        u = dyv * g_ref[...]
        dx = r * u - xv * (r * r * r * jnp.mean(u * xv, axis=-1, keepdims=True))
        if dres is not None:
            dx = rest[0][...] + dx
        dx_ref[...] = dx.astype(out_dtype)
        dg_ref[...] += jnp.sum(dyv * xv * r, axis=0, keepdims=True)

    return pl.pallas_call(
        body, name=name, grid=(t // tm,), in_specs=in_specs, out_specs=[row, vec],
        out_shape=[jax.ShapeDtypeStruct((t, d), out_dtype), jax.ShapeDtypeStruct((1, d), F32)],
        compiler_params=_params("arbitrary"),
    )(*operands)


def loss_head(y, target, *, name, tm=512):
    t, d = y.shape
    tm = min(tm, t)
    nt = t // tm
    row = pl.BlockSpec((tm, d), lambda i: (i, 0))

    def body(y_ref, t_ref, loss_ref, dy_ref, acc_ref):
        i = pl.program_id(0)

        @pl.when(i == 0)
        def _():
            acc_ref[...] = jnp.zeros_like(acc_ref)

        err = y_ref[...] - t_ref[...]
        dy_ref[...] = err * (1.0 / d)
        acc_ref[...] += jnp.sum(err * err, axis=0, keepdims=True)

        @pl.when(i == nt - 1)
        def _():
            loss_ref[...] = jnp.sum(acc_ref[...], axis=1, keepdims=True) * (0.5 / d)

    return pl.pallas_call(
        body, name=name, grid=(nt,), in_specs=[row, row],
        out_specs=[pl.BlockSpec((1, 1), lambda i: (0, 0)), row],
        out_shape=[jax.ShapeDtypeStruct((1, 1), F32), jax.ShapeDtypeStruct((t, d), F32)],
        scratch_shapes=[pltpu.VMEM((1, d), F32)], compiler_params=_params("arbitrary"),
    )(y, target)


def _sb_block_iota():
    rows = lax.broadcasted_iota(jnp.int32, (SB_BLOCK, SB_BLOCK), 0)
    cols = lax.broadcasted_iota(jnp.int32, (SB_BLOCK, SB_BLOCK), 1)
    return rows, cols


def _sb_logits(q_h, k_blk, mask):
    z = _dot(q_h, k_blk, _NT)
    ls = _log_sigmoid(z)
    lm = ls - z
    if mask is not None:
        lm = jnp.where(mask, lm, 0.0)
    return ls, lm, jnp.sum(lm, axis=1, keepdims=True)


def _sb_weights(ls, lm, mask, tri_strict, later):
    suffix = _dot_exact_rhs(lm, tri_strict)
    w = jnp.exp(ls + suffix + later)
    return w if mask is None else jnp.where(mask, w, 0.0)


def _on_grid_step(p, i):
    return jnp.logical_and(pl.program_id(0) == p, pl.program_id(1) == i)


def sb_fwd(qkv, *, name, gather=None):
    t, d3 = qkv.shape
    d = d3 // 3
    head_dim = d // SB_HEADS
    assert 2 * head_dim == LANES and t % SB_BLOCK == 0
    pairs = d // LANES
    nq = t // SB_BLOCK
    scale = head_dim ** -0.5

    def body(q_ref, k_ref, v_ref, *rest):
        if gather is None:
            compute(q_ref, k_ref, v_ref, *rest)
            return
        x_ref, o_ref, tot_ref, out_ref, send_sems, recv_sems, local_sem = rest
        start, forward, finish = _gather_plan(x_ref, out_ref, send_sems, recv_sems, local_sem)
        pl.when(_on_grid_step(0, 0))(start)
        pl.when(_on_grid_step(pairs // 2, 0))(forward)
        compute(q_ref, k_ref, v_ref, o_ref, tot_ref)
        pl.when(_on_grid_step(pairs - 1, nq - 1))(finish)

    def compute(q_ref, k_ref, v_ref, o_ref, tot_ref):
        qi = pl.program_id(1)
        lane = lax.broadcasted_iota(jnp.int32, (SB_BLOCK, LANES), 1)
        first = lane < head_dim
        q = q_ref[...] * scale
        q2 = jnp.concatenate([jnp.where(first, q, jnp.zeros_like(q)), jnp.where(first, jnp.zeros_like(q), q)], axis=0)
        rows, cols = _sb_block_iota()
        tri = jnp.where(rows > cols, 1.0, 0.0).astype(BF16)
        earlier = cols < rows
        diagonal = jnp.concatenate([earlier, earlier], axis=0)

        def step(kb, carry, mask):
            ks = pl.multiple_of(kb * SB_BLOCK, SB_BLOCK)
            k_blk = k_ref[pl.ds(ks, SB_BLOCK), :]
            v_blk = v_ref[pl.ds(ks, SB_BLOCK), :]
            acc, later = carry
            ls, lm, row = _sb_logits(q2, k_blk, mask)
            w = _sb_weights(ls, lm, mask, tri, later)
            return acc + _dot(w, v_blk), later + row

        out = step(qi, (jnp.zeros((2 * SB_BLOCK, LANES), F32), jnp.zeros((2 * SB_BLOCK, 1), F32)), diagonal)
        out = lax.fori_loop(0, qi // 2, lambda i, carry: step(qi - 2 - 2 * i, step(qi - 1 - 2 * i, carry, None), None), out)
        acc, total = lax.cond(qi % 2 == 1, lambda carry: step(0, carry, None), lambda carry: carry, out)
        o_ref[...] = jnp.where(first, acc[:SB_BLOCK], acc[SB_BLOCK:]).astype(o_ref.dtype)
        tot_ref[...] = jnp.where(first, total[:SB_BLOCK], total[SB_BLOCK:])

    blk = lambda off: pl.BlockSpec((t, LANES), lambda p, i: (0, off + p))
    qblk = pl.BlockSpec((SB_BLOCK, LANES), lambda p, i: (i, p))
    in_specs, operands = [qblk, blk(pairs), blk(2 * pairs)], [qkv, qkv, qkv]
    out_specs = [qblk, qblk]
    out_shape = [jax.ShapeDtypeStruct((t, d), BF16), jax.ShapeDtypeStruct((t, d), F32)]
    if gather is None:
        return pl.pallas_call(
            body, name=name, grid=(pairs, nq), in_specs=in_specs, out_specs=out_specs, out_shape=out_shape,
            compiler_params=_params("parallel", "arbitrary"),
        )(*operands)
    whole = pl.BlockSpec(memory_space=pl.ANY)
    return pl.pallas_call(
        body, name=name, grid=(pairs, nq), in_specs=in_specs + [whole], out_specs=out_specs + [whole],
        out_shape=out_shape + [jax.ShapeDtypeStruct((N_DEV * gather.shape[0], gather.shape[1]), gather.dtype)],
        scratch_shapes=_comm_scratch(), compiler_params=_params("arbitrary", "arbitrary"),
    )(*operands, gather)


def sb_bwd(qkv, tot, do, *, name, exchange=None):
    t, d3 = qkv.shape
    d = d3 // 3
    head_dim = d // SB_HEADS
    pairs = d // LANES
    nq = t // SB_BLOCK
    scale = head_dim ** -0.5

    def body(*refs):
        if exchange is None:
            compute(*refs)
            return
        q_ref, k_ref, v_ref, tot_ref, do_ref, in_ref, dq_ref, dk_ref, dv_ref, out_ref = refs[:10]
        dk_sum, dv_sum, send_sems, recv_sems, local_sem = refs[10:]
        start, finish = _exchange_plan(in_ref, out_ref, send_sems, recv_sems, local_sem)
        pl.when(_on_grid_step(0, 0))(start)
        compute(q_ref, k_ref, v_ref, tot_ref, do_ref, dq_ref, dk_ref, dv_ref, dk_sum, dv_sum)
        pl.when(_on_grid_step(pairs - 1, nq - 1))(finish)

    def compute(q_ref, k_ref, v_ref, tot_ref, do_ref, dq_ref, dk_ref, dv_ref, dk_sum, dv_sum):
        qi = pl.program_id(1)

        @pl.when(qi == 0)
        def _():
            dk_sum[...] = jnp.zeros_like(dk_sum)
            dv_sum[...] = jnp.zeros_like(dv_sum)

        lane = lax.broadcasted_iota(jnp.int32, (SB_BLOCK, LANES), 1)
        first = lane < head_dim
        q, dov, totv = q_ref[...] * scale, do_ref[...], tot_ref[...]
        second = jnp.logical_not(first)
        q2 = jnp.concatenate([jnp.where(s, q, jnp.zeros_like(q)) for s in (first, second)], axis=0)
        do2 = jnp.concatenate([jnp.where(s, dov, jnp.zeros_like(dov)) for s in (first, second)], axis=0)
        tot2 = jnp.concatenate([totv[:, 0:1], totv[:, head_dim:head_dim + 1]], axis=0)
        rows, cols = _sb_block_iota()
        tri_strict = jnp.where(rows > cols, 1.0, 0.0).astype(BF16)
        tri_before = jnp.where(rows < cols, 1.0, 0.0).astype(BF16)
        earlier = cols < rows
        diagonal = jnp.concatenate([earlier, earlier], axis=0)

        def step(kb, carry, mask):
            ks = pl.multiple_of(kb * SB_BLOCK, SB_BLOCK)
            k_blk = k_ref[pl.ds(ks, SB_BLOCK), :]
            v_blk = v_ref[pl.ds(ks, SB_BLOCK), :]
            dq, seen, before = carry
            ls, lm, row = _sb_logits(q2, k_blk, mask)
            seen = seen + row
            w = _sb_weights(ls, lm, mask, tri_strict, tot2 - seen)
            da = _dot(do2, v_blk, _NT) * w
            g = _dot_exact_rhs(da, tri_before) + before
            dz = da - jnp.exp(ls) * (da + g)
            if mask is not None:
                dz = jnp.where(mask, dz, 0.0)
            dk_sum[pl.ds(ks, SB_BLOCK), :] += _dot(dz, q2, _TN)
            dv_sum[pl.ds(ks, SB_BLOCK), :] += _dot(w, do2, _TN)
            return dq + _dot(dz, k_blk * scale), seen, before + jnp.sum(da, axis=1, keepdims=True)

        zero = jnp.zeros((2 * SB_BLOCK, LANES), F32)
        zcol = jnp.zeros((2 * SB_BLOCK, 1), F32)
        out = lax.fori_loop(0, qi, lambda kb, carry: step(kb, carry, None), (zero, zcol, zcol))
        dq = step(qi, out, diagonal)[0]
        dq_ref[...] = jnp.where(first, dq[:SB_BLOCK], dq[SB_BLOCK:]).astype(dq_ref.dtype)

        @pl.when(qi == nq - 1)
        def _():
            dk_ref[...] = dk_sum[...].astype(dk_ref.dtype)
            dv_ref[...] = dv_sum[...].astype(dv_ref.dtype)

    qblk = pl.BlockSpec((SB_BLOCK, LANES), lambda p, i: (i, p))
    col = lambda off: pl.BlockSpec((t, LANES), lambda p, i: (0, off + p))
    shape = jax.ShapeDtypeStruct((t, d), BF16)
    in_specs, operands = [qblk, col(pairs), col(2 * pairs), qblk, qblk], [qkv, qkv, qkv, tot, do]
    out_specs, out_shape = [qblk, col(0), col(0)], [shape, shape, shape]
    sums = [pltpu.VMEM((t, LANES), F32), pltpu.VMEM((t, LANES), F32)]
    if exchange is None:
        return pl.pallas_call(
            body, name=name, grid=(pairs, nq), in_specs=in_specs, out_specs=out_specs, out_shape=out_shape,
            scratch_shapes=sums, compiler_params=_params("parallel", "arbitrary"),
        )(*operands)
    whole = pl.BlockSpec(memory_space=pl.ANY)
    return pl.pallas_call(
        body, name=name, grid=(pairs, nq), in_specs=in_specs + [whole], out_specs=out_specs + [whole],
        out_shape=out_shape + [jax.ShapeDtypeStruct(exchange.shape, exchange.dtype)],
        scratch_shapes=sums + _comm_scratch(), compiler_params=_params("arbitrary", "arbitrary"),
    )(*operands, exchange)


def _shift_down(x, s):
    rows = lax.broadcasted_iota(jnp.int32, x.shape, 0)
    return jnp.where(rows >= s, pltpu.roll(x, s, 0), 0.0)


def _shift_up(x, s):
    t = x.shape[0]
    rows = lax.broadcasted_iota(jnp.int32, x.shape, 0)
    return jnp.where(rows < t - s, pltpu.roll(x, t - s, 0), 0.0)


def conv_fwd(bcu, w, *, name):
    t, d3 = bcu.shape
    d = d3 // 3
    nb = d // LANES
    col = lambda off: pl.BlockSpec((t, LANES), lambda j: (0, off + j))

    def body(b_ref, c_ref, u_ref, w_ref, y_ref):
        hh = c_ref[...] * u_ref[...]
        conv = w_ref[0:1, :] * _shift_down(hh, 2) + w_ref[1:2, :] * _shift_down(hh, 1) + w_ref[2:3, :] * hh
        y_ref[...] = (b_ref[...] * conv).astype(y_ref.dtype)

    return pl.pallas_call(
        body, name=name, grid=(nb,),
        in_specs=[col(0), col(nb), col(2 * nb), pl.BlockSpec((CONV_WIDTH, LANES), lambda j: (0, j))],
        out_specs=col(0), out_shape=jax.ShapeDtypeStruct((t, d), BF16), compiler_params=_params("parallel"),
    )(bcu, bcu, bcu, w)


def conv_bwd(bcu, w, dy, *, name):
    t, d3 = bcu.shape
    d = d3 // 3
    nb = d // LANES
    col = lambda off: pl.BlockSpec((t, LANES), lambda j: (0, off + j))
    wspec = pl.BlockSpec((CONV_WIDTH, LANES), lambda j: (0, j))

    def body(b_ref, c_ref, u_ref, w_ref, dy_ref, db_ref, dc_ref, du_ref, dw_ref):
        c, u, dyv = c_ref[...], u_ref[...], dy_ref[...]
        hh = c * u
        h2, h1 = _shift_down(hh, 2), _shift_down(hh, 1)
        w0, w1, w2 = w_ref[0:1, :], w_ref[1:2, :], w_ref[2:3, :]
        db_ref[...] = (dyv * (w0 * h2 + w1 * h1 + w2 * hh)).astype(db_ref.dtype)
        dconv = dyv * b_ref[...]
        dhh = w2 * dconv + w1 * _shift_up(dconv, 1) + w0 * _shift_up(dconv, 2)
        dc_ref[...] = (dhh * u).astype(dc_ref.dtype)
        du_ref[...] = (dhh * c).astype(du_ref.dtype)
        dw_ref[0:1, :] = jnp.sum(dconv * h2, axis=0, keepdims=True)
        dw_ref[1:2, :] = jnp.sum(dconv * h1, axis=0, keepdims=True)
        dw_ref[2:3, :] = jnp.sum(dconv * hh, axis=0, keepdims=True)

    shape = jax.ShapeDtypeStruct((t, d), BF16)
    return pl.pallas_call(
        body, name=name, grid=(nb,),
        in_specs=[col(0), col(nb), col(2 * nb), wspec, col(0)],
        out_specs=[col(0), col(0), col(0), wspec],
        out_shape=[shape, shape, shape, jax.ShapeDtypeStruct((CONV_WIDTH, d), F32)],
        compiler_params=_params("parallel"),
    )(bcu, bcu, bcu, w, dy)


def _gla_chunk(q_ref, k_ref, lg_ref, scale):
    c = GLA_CHUNK
    rows = lax.broadcasted_iota(jnp.int32, (c, c), 0)
    cols = lax.broadcasted_iota(jnp.int32, (c, c), 1)
    causal = rows >= cols
    tril = jnp.where(causal, 1.0, 0.0).astype(BF16)
    q = q_ref[...] * scale
    k = k_ref[...]
    cum = _dot_exact_lhs(tril, lg_ref[...])
    last = cum[c - 1:c, :]
    eq = jnp.exp(cum)
    ek = jnp.exp(-cum)
    el = jnp.exp(last - cum)
    return causal, tril, q, k, cum, last, eq, ek, el


def gla_fwd(proj, lg, *, name):
    t, d3 = proj.shape
    d = d3 // 3
    dk, dv = d // 2 // GLA_HEADS, d // GLA_HEADS
    assert dk == LANES and dv == 2 * LANES
    c = GLA_CHUNK
    nc = t // c
    scale = dk ** -0.5
    nh = GLA_HEADS

    def body(q_ref, k_ref, v_ref, lg_ref, o_ref, st_out_ref, st_ref):
        @pl.when(pl.program_id(1) == 0)
        def _():
            st_ref[...] = jnp.zeros_like(st_ref)

        causal, _, q, k, _, last, eq, ek, el = _gla_chunk(q_ref, k_ref, lg_ref, scale)
        v = v_ref[...]
        st = st_ref[...]
        st_out_ref[...] = st
        qt = q * eq
        scores = jnp.where(causal, _dot(qt, k * ek, _NT), 0.0)
        o_ref[...] = _dot(qt, st, _NT) + _dot(scores, v)
        st_ref[...] = st * jnp.exp(last) + _dot(v, k * el, _TN)

    return pl.pallas_call(
        body, name=name, grid=(nh, nc),
        in_specs=[pl.BlockSpec((c, dk), lambda h, i: (i, h)), pl.BlockSpec((c, dk), lambda h, i: (i, nh + h)),
                  pl.BlockSpec((c, dv), lambda h, i: (i, nh + h)), pl.BlockSpec((c, dk), lambda h, i: (i, h))],
        out_specs=[pl.BlockSpec((c, dv), lambda h, i: (i, h)),
                   pl.BlockSpec((None, None, dv, dk), lambda h, i: (h, i, 0, 0))],
        out_shape=[jax.ShapeDtypeStruct((t, d), F32), jax.ShapeDtypeStruct((nh, nc, dv, dk), F32)],
        scratch_shapes=[pltpu.VMEM((dv, dk), F32)], compiler_params=_params("parallel", "arbitrary"),
    )(proj, proj, proj, lg)


def gla_bwd(proj, lg, states, do, *, name):
    t, d3 = proj.shape
    d = d3 // 3
    dk, dv = d // 2 // GLA_HEADS, d // GLA_HEADS
    c = GLA_CHUNK
    nc = t // c
    scale = dk ** -0.5
    nh = GLA_HEADS

    def body(q_ref, k_ref, v_ref, lg_ref, st_ref, do_ref, dq_ref, dk_ref, dv_ref, dzg_ref, dbg_ref, dst_ref):
        @pl.when(pl.program_id(1) == 0)
        def _():
            dst_ref[...] = jnp.zeros_like(dst_ref)
            dbg_ref[...] = jnp.zeros_like(dbg_ref)

        causal, tril, q, k, _, last, eq, ek, el = _gla_chunk(q_ref, k_ref, lg_ref, scale)
        v, st, dov, dst = v_ref[...], st_ref[...], do_ref[...], dst_ref[...]
        qt, kt, kh = q * eq, k * ek, k * el
        scores = jnp.where(causal, _dot(qt, kt, _NT), 0.0)
        dscores = jnp.where(causal, _dot(dov, v, _NT), 0.0)
        dqt = _dot(dov, st) + _dot(dscores, kt)
        dkt = _dot(dscores, qt, _TN)
        dkh = _dot(v, dst)
        dv_ref[...] = (_dot(scores, dov, _TN) + _dot(kh, dst, _NT)).astype(dv_ref.dtype)
        dq_ref[...] = (dqt * eq * scale).astype(dq_ref.dtype)
        dk_ref[...] = (dkt * ek + dkh * el).astype(dk_ref.dtype)
        kh_dkh = kh * dkh
        e_last = jnp.exp(last)
        dlast = jnp.sum(kh_dkh, axis=0, keepdims=True) + e_last * jnp.sum(dst * st, axis=0, keepdims=True)
        dcum = qt * dqt - kt * dkt - kh_dkh
        dlg = _dot_exact_lhs(tril, dcum, _TN) + dlast
        lgv = lg_ref[...]
        dzg = dlg * (1.0 - jnp.exp(lgv * GLA_GATE_NORMALIZER)) / GLA_GATE_NORMALIZER
        dzg_ref[...] = dzg.astype(dzg_ref.dtype)
        dbg_ref[...] += jnp.sum(dzg, axis=0, keepdims=True)
        dst_ref[...] = dst * e_last + _dot(dov, qt, _TN)

    rev = lambda i: nc - 1 - i
    half = jax.ShapeDtypeStruct((t, d // 2), BF16)
    return pl.pallas_call(
        body, name=name, grid=(nh, nc),
        in_specs=[pl.BlockSpec((c, dk), lambda h, i: (rev(i), h)), pl.BlockSpec((c, dk), lambda h, i: (rev(i), nh + h)),
                  pl.BlockSpec((c, dv), lambda h, i: (rev(i), nh + h)), pl.BlockSpec((c, dk), lambda h, i: (rev(i), h)),
                  pl.BlockSpec((None, None, dv, dk), lambda h, i: (h, rev(i), 0, 0)),
                  pl.BlockSpec((c, dv), lambda h, i: (rev(i), h))],
        out_specs=[pl.BlockSpec((c, dk), lambda h, i: (rev(i), h)), pl.BlockSpec((c, dk), lambda h, i: (rev(i), h)),
                   pl.BlockSpec((c, dv), lambda h, i: (rev(i), h)), pl.BlockSpec((c, dk), lambda h, i: (rev(i), h)),
                   pl.BlockSpec((1, dk), lambda h, i: (0, h))],
        out_shape=[half, half, jax.ShapeDtypeStruct((t, d), BF16), half, jax.ShapeDtypeStruct((1, d // 2), F32)],
        scratch_shapes=[pltpu.VMEM((dv, dk), F32)], compiler_params=_params("parallel", "arbitrary"),
    )(proj, proj, proj, lg, states, do)


def gla_post_fwd(o, proj, head_norm, *, name, tm=512):
    t, d = o.shape
    dv = d // GLA_HEADS
    tm = min(tm, t)

    def body(o_ref, g_ref, hn_ref, y_ref):
        for h in range(GLA_HEADS):
            sl = slice(h * dv, (h + 1) * dv)
            ov, gv = o_ref[:, sl], g_ref[:, sl]
            r = lax.rsqrt(jnp.mean(ov * ov, axis=-1, keepdims=True) + RMS_EPS)
            y_ref[:, sl] = ((ov * r * hn_ref[:, sl]) * (gv * _sigmoid(gv))).astype(y_ref.dtype)

    row = pl.BlockSpec((tm, d), lambda i: (i, 0))
    return pl.pallas_call(
        body, name=name, grid=(t // tm,),
        in_specs=[row, pl.BlockSpec((tm, d), lambda i: (i, 2)), pl.BlockSpec((1, d), lambda i: (0, 0))],
        out_specs=row, out_shape=jax.ShapeDtypeStruct((t, d), BF16), compiler_params=_params("parallel"),
    )(o, proj, head_norm)


def gla_post_bwd(o, proj, head_norm, dy, *, name, tm=512):
    t, d = o.shape
    dv = d // GLA_HEADS
    tm = min(tm, t)

    def body(o_ref, g_ref, hn_ref, dy_ref, do_ref, dg_ref, dhn_ref):
        @pl.when(pl.program_id(0) == 0)
        def _():
            dhn_ref[...] = jnp.zeros_like(dhn_ref)

        for h in range(GLA_HEADS):
            sl = slice(h * dv, (h + 1) * dv)
            ov, gv, dyv, hn = o_ref[:, sl], g_ref[:, sl], dy_ref[:, sl], hn_ref[:, sl]
            r = lax.rsqrt(jnp.mean(ov * ov, axis=-1, keepdims=True) + RMS_EPS)
            sg = _sigmoid(gv)
            silu = gv * sg
            on = ov * r * hn
            dg_ref[:, sl] = (dyv * on * (sg * (1.0 + gv * (1.0 - sg)))).astype(dg_ref.dtype)
            don = dyv * silu
            u = don * hn
            do_ref[:, sl] = (r * u - ov * (r * r * r * jnp.mean(u * ov, axis=-1, keepdims=True))).astype(do_ref.dtype)
            dhn_ref[:, sl] += jnp.sum(don * ov * r, axis=0, keepdims=True)

    row = pl.BlockSpec((tm, d), lambda i: (i, 0))
    vec = pl.BlockSpec((1, d), lambda i: (0, 0))
    shape = jax.ShapeDtypeStruct((t, d), BF16)
    return pl.pallas_call(
        body, name=name, grid=(t // tm,),
        in_specs=[row, pl.BlockSpec((tm, d), lambda i: (i, 2)), vec, row],
        out_specs=[row, row, vec], out_shape=[shape, shape, jax.ShapeDtypeStruct((1, d), F32)],
        compiler_params=_params("arbitrary"),
    )(o, proj, head_norm, dy)


def _ffn_fwd(h, gains, w_up, w_down, tag):
    xn = rms_fwd(h, gains[2], name=f"{tag}_ffn_norm", out_dtype=BF16)
    u, act = matmul(xn, w_up, mode='nn', epi='relu2_pair', name=f"{tag}_ffn_up")
    f = matmul(act, w_down, mode='nn', name=f"{tag}_ffn_down")
    h_out = rms_fwd(f, gains[3], res=h, name=f"{tag}_ffn_out")
    return h_out, (h, xn, u, act, f)


def _ffn_bwd(dh, saved, gains, w_up, w_down, tag):
    h, xn, u, act, f = saved
    df, dg3 = rms_bwd(f, gains[3], dh, out_dtype=BF16, name=f"{tag}_ffn_out_bwd")
    du = matmul(df, w_down, mode='nt', epi='mul2relu', extra=u, out_dtype=BF16, name=f"{tag}_ffn_da")
    dw_down = matmul(act, df, mode='tn', name=f"{tag}_ffn_dwdown")
    dw_up = matmul(xn, du, mode='tn', name=f"{tag}_ffn_dwup")
    dxn = matmul(du, w_up, mode='nt', name=f"{tag}_ffn_dxn")
    dh_in, dg2 = rms_bwd(h, gains[2], dxn, dres=dh, name=f"{tag}_ffn_norm_bwd")
    return dh_in, dg2, dg3, dw_up, dw_down


def _sb_layer_fwd(xn, w, j, tag, comm=None):
    qkv = matmul(xn, w['sb_w_qkv'][j], mode='nn', out_dtype=BF16, name=f"{tag}_qkv")
    if comm is None:
        o, tot = sb_fwd(qkv, name=f"{tag}_sb")
    else:
        o, tot, gathered = sb_fwd(qkv, name=f"{tag}_sb", gather=comm.rest_payload)
        comm.on_gathered(gathered)
    m = matmul(o, w['sb_w_o'][j], mode='nn', name=f"{tag}_wo")
    return m, (qkv, o, tot)


def _sb_layer_bwd(dm, xn, saved, w, j, tag, comm=None, grads=None):
    qkv, o, tot = saved
    do = matmul(dm, w['sb_w_o'][j], mode='nt', out_dtype=BF16, name=f"{tag}_do")
    dw_o = matmul(o, dm, mode='tn', name=f"{tag}_dwo")
    if comm is None:
        dq, dk, dv = sb_bwd(qkv, tot, do, name=f"{tag}_sb_bwd")
    else:
        parts = comm.rest_parts({**grads, ('sb_w_o', j): dw_o})
        dq, dk, dv, received = sb_bwd(qkv, tot, do, name=f"{tag}_sb_bwd", exchange=parts)
        comm.on_received(received)
    dqkv = jnp.concatenate([dq, dk, dv], axis=1)
    dw_qkv = matmul(xn, dqkv, mode='tn', name=f"{tag}_dwqkv")
    dxn = matmul(dqkv, w['sb_w_qkv'][j], mode='nt', name=f"{tag}_dxn")
    return dxn, {('sb_w_qkv', j): dw_qkv, ('sb_w_o', j): dw_o}


def _conv_layer_fwd(xn, w, j, tag):
    bcu = matmul(xn, w['conv_w_in'][j], mode='nn', name=f"{tag}_in")
    y = conv_fwd(bcu, w['conv_w'][j], name=f"{tag}_conv")
    m = matmul(y, w['conv_w_out'][j], mode='nn', name=f"{tag}_out")
    return m, (bcu, y)


def _conv_layer_bwd(dm, xn, saved, w, j, tag):
    bcu, y = saved
    dy = matmul(dm, w['conv_w_out'][j], mode='nt', name=f"{tag}_dy")
    dw_out = matmul(y, dm, mode='tn', name=f"{tag}_dwout")
    db, dc, du, dw_conv = conv_bwd(bcu, w['conv_w'][j], dy, name=f"{tag}_conv_bwd")
    dbcu = jnp.concatenate([db, dc, du], axis=1)
    dw_in = matmul(xn, dbcu, mode='tn', name=f"{tag}_dwin")
    dxn = matmul(dbcu, w['conv_w_in'][j], mode='nt', name=f"{tag}_dxn")
    return dxn, {('conv_w_in', j): dw_in, ('conv_w', j): dw_conv, ('conv_w_out', j): dw_out}


def _gla_split(w_in, w_gate_up):
    d = w_in.shape[0]
    w_main = w_in[:, :3 * d]
    w_a = jnp.pad(w_in[:, 3 * d:], ((0, 0), (0, LANES - GLA_GATE_RANK)))
    w_gu = jnp.pad(w_gate_up, ((0, LANES - GLA_GATE_RANK), (0, 0)))
    return w_main, w_a, w_gu


def _gla_layer_fwd(xn, w, j, tag):
    w_main, w_a, w_gu = _gla_split(w['gla_w_in'][j], w['gla_w_gate_up'][j])
    proj = matmul(xn, w_main, mode='nn', name=f"{tag}_in")
    a_low = matmul(xn, w_a, mode='nn', out_dtype=BF16, name=f"{tag}_alow")
    lg = matmul(a_low, w_gu, mode='nn', epi='logsig16', extra=w['gla_b_gate'][j][None, :], name=f"{tag}_gate")
    o, states = gla_fwd(proj, lg, name=f"{tag}_gla")
    hn = w['gla_head_norm'][j].reshape(1, -1)
    y = gla_post_fwd(o, proj, hn, name=f"{tag}_post")
    m = matmul(y, w['gla_w_o'][j], mode='nn', name=f"{tag}_wo")
    return m, (proj, a_low, lg, o, states, y)


def _gla_layer_bwd(dm, xn, saved, w, j, tag):
    proj, a_low, lg, o, states, y = saved
    w_main, w_a, w_gu = _gla_split(w['gla_w_in'][j], w['gla_w_gate_up'][j])
    hn = w['gla_head_norm'][j].reshape(1, -1)
    dy = matmul(dm, w['gla_w_o'][j], mode='nt', name=f"{tag}_dy")
    dw_o = matmul(y, dm, mode='tn', name=f"{tag}_dwo")
    do, dg, dhn = gla_post_bwd(o, proj, hn, dy, name=f"{tag}_post_bwd")
    dq, dk, dv, dzg, dbg = gla_bwd(proj, lg, states, do, name=f"{tag}_gla_bwd")
    da_low = matmul(dzg, w_gu, mode='nt', out_dtype=BF16, name=f"{tag}_dalow")
    dw_gu = matmul(a_low, dzg, mode='tn', name=f"{tag}_dwgu")[:GLA_GATE_RANK]
    dproj = jnp.concatenate([dq, dk, dv, dg], axis=1)
    dw_main = matmul(xn, dproj, mode='tn', name=f"{tag}_dwin")
    dw_a = matmul(xn, da_low, mode='tn', name=f"{tag}_dwa")[:, :GLA_GATE_RANK]
    dxn_a = matmul(da_low, w_a, mode='nt', name=f"{tag}_dxn_a")
    dxn = matmul(dproj, w_main, mode='nt', epi='add', extra=dxn_a, name=f"{tag}_dxn")
    grads = {('gla_w_in', j): jnp.concatenate([dw_main, dw_a], axis=1), ('gla_w_gate_up', j): dw_gu,
             ('gla_b_gate', j): dbg[0], ('gla_head_norm', j): dhn.reshape(w['gla_head_norm'][j].shape),
             ('gla_w_o', j): dw_o}
    return dxn, grads


_MIXERS = ((_sb_layer_fwd, _sb_layer_bwd), (_conv_layer_fwd, _conv_layer_bwd), (_gla_layer_fwd, _gla_layer_bwd))


def local_step(x, w, target, comm=None):
    depth = len(w['norm_gains'])
    h = x
    tape = []
    for i in range(depth):
        kind, j = i % 3, i // 3
        tag = f"l{i}"
        extra = {'comm': comm} if (comm is not None and i == 0) else {}
        gains = [w['norm_gains'][i][s][None, :] for s in range(4)]
        xn = rms_fwd(h, gains[0], name=f"{tag}_mix_norm", out_dtype=BF16)
        m, saved = _MIXERS[kind][0](xn, w, j, tag, **extra)
        h_mid = rms_fwd(m, gains[1], res=h, name=f"{tag}_mix_out")
        h_out, ffn_saved = _ffn_fwd(h_mid, gains, w['ffn_w_up'][i], w['ffn_w_down'][i], tag)
        tape.append((h, xn, m, saved, ffn_saved, gains))
        h = h_out
    loss, dh = loss_head(h, target, name="loss_head")

    grads = {}
    for i in reversed(range(depth)):
        kind, j = i % 3, i // 3
        tag = f"l{i}"
        h_in, xn, m, saved, ffn_saved, gains = tape[i]
        dg = [None] * 4
        dh, dg[2], dg[3], grads[('ffn_w_up', i)], grads[('ffn_w_down', i)] = _ffn_bwd(
            dh, ffn_saved, gains, w['ffn_w_up'][i], w['ffn_w_down'][i], tag)
        dm, dg[1] = rms_bwd(m, gains[1], dh, out_dtype=BF16, name=f"{tag}_mix_out_bwd")
        extra = {'comm': comm, 'grads': grads} if (comm is not None and i == 0) else {}
        dxn, g = _MIXERS[kind][1](dm, xn, saved, w, j, tag, **extra)
        grads.update(g)
        dh, dg[0] = rms_bwd(h_in, gains[0], dxn, dres=dh, name=f"{tag}_mix_norm_bwd")
        grads[('norm_gains', i)] = jnp.concatenate(dg, axis=0)
    return loss, dh, grads


def _segments(keys, shard_shapes, f32_as_pairs):
    segs, row = {}, 0
    for name, j in keys:
        n = math.prod(shard_shapes[name][1:]) * (2 if f32_as_pairs and name in F32_PAYLOAD else 1)
        nrows = -(-n // (PACK_COLS * PACK_ROW_ALIGN)) * PACK_ROW_ALIGN
        segs[(name, j)] = (row, nrows, n)
        row += nrows
    return segs, -(-row // PACK_ROW_BLOCK) * PACK_ROW_BLOCK


def _pack(parts, segs, total_rows, dtype):
    pieces, row = [], 0
    for key in segs:
        _, nrows, n = segs[key]
        p = parts[key].astype(dtype)
        lead = p.shape[:-1]
        p = jnp.pad(p, [(0, 0)] * len(lead) + [(0, nrows * PACK_COLS - n)])
        pieces.append(p.reshape(lead + (nrows, PACK_COLS)))
        row += nrows
    if total_rows > row:
        pieces.append(jnp.zeros(lead + (total_rows - row, PACK_COLS), dtype))
    return jnp.concatenate(pieces, axis=-2)


def _unpack(buf, seg):
    first, nrows, n = seg
    piece = buf[..., first:first + nrows, :]
    return piece.reshape(piece.shape[:-2] + (nrows * PACK_COLS,))[..., :n]


def _unshard(gathered, axis):
    moved = jnp.moveaxis(gathered, 0, axis)
    shape = moved.shape
    return moved.reshape(shape[:axis] + (shape[axis] * shape[axis + 1],) + shape[axis + 2:])


def _shard_split(full, axis):
    shape = full.shape
    cut = full.reshape(shape[:axis] + (N_DEV, shape[axis] // N_DEV) + shape[axis + 1:])
    return jnp.moveaxis(cut, axis, 0)


def _mesh_position():
    return lax.axis_index("x"), lax.axis_index("y"), lax.axis_index("c")


def _comm_scratch():
    return [pltpu.SemaphoreType.DMA((N_DEV - 1,)), pltpu.SemaphoreType.DMA((N_DEV - 1,)), pltpu.SemaphoreType.DMA]


def _gather_plan(x_ref, out_ref, send_sems, recv_sems, local_sem):
    rows = x_ref.shape[0]
    x, y, c = _mesh_position()
    me, sibling = (x, y, c), (x, y, 1 - c)
    chips = [(1 - x, y), (x, 1 - y), (1 - x, 1 - y)]

    def block(px, py, pc):
        return out_ref.at[pl.ds((4 * px + 2 * py + pc) * rows, rows), :]

    def copy(k, blk, to, src=None):
        return pltpu.make_async_remote_copy(
            src_ref=block(*blk) if src is None else src, dst_ref=block(*blk),
            send_sem=send_sems.at[k], recv_sem=recv_sems.at[k],
            device_id=to, device_id_type=pl.DeviceIdType.MESH)

    mine = pltpu.make_async_copy(x_ref, block(*me), local_sem)
    first = [copy(0, me, sibling, src=x_ref)]
    first += [copy(1 + j, me, (*chip, c), src=x_ref) for j, chip in enumerate(chips)]
    passed = [copy(4 + j, (*chip, c), sibling) for j, chip in enumerate(chips)]

    def start():
        mine.start()
        for cp in first:
            cp.start()

    def forward():
        for j, chip in enumerate(chips):
            copy(1 + j, (*chip, c), me).wait_recv()
            passed[j].start()

    def finish():
        copy(0, sibling, me).wait_recv()
        for j, chip in enumerate(chips):
            copy(4 + j, (*chip, 1 - c), me).wait_recv()
        for cp in first + passed:
            cp.wait_send()
        mine.wait()

    return start, forward, finish


def _exchange_plan(in_ref, out_ref, send_sems, recv_sems, local_sem):
    x, y, c = _mesh_position()
    my_id = 4 * x + 2 * y + c
    mine = pltpu.make_async_copy(in_ref.at[my_id], out_ref.at[my_id], local_sem)

    def copy(k, receive):
        px = 1 - x if k & 4 else x
        py = 1 - y if k & 2 else y
        pc = 1 - c if k & 1 else c
        peer_id = 4 * px + 2 * py + pc
        return pltpu.make_async_remote_copy(
            src_ref=in_ref.at[peer_id], dst_ref=out_ref.at[peer_id if receive else my_id],
            send_sem=send_sems.at[k - 1], recv_sem=recv_sems.at[k - 1],
            device_id=(px, py, pc), device_id_type=pl.DeviceIdType.MESH)

    def start():
        mine.start()
        for k in range(1, N_DEV):
            copy(k, False).start()

    def finish():
        for k in range(1, N_DEV):
            copy(k, True).wait_recv()
        for k in range(1, N_DEV):
            copy(k, False).wait_send()
        mine.wait()

    return start, finish


def all_gather(shard, *, name):
    rows, cols = shard.shape

    def body(x_ref, out_ref, send_sems, recv_sems, local_sem):
        start, forward, finish = _gather_plan(x_ref, out_ref, send_sems, recv_sems, local_sem)
        start()
        forward()
        finish()

    return pl.pallas_call(
        body, name=name, out_shape=jax.ShapeDtypeStruct((N_DEV * rows, cols), shard.dtype),
        in_specs=[pl.BlockSpec(memory_space=pl.ANY)], out_specs=pl.BlockSpec(memory_space=pl.ANY),
        scratch_shapes=_comm_scratch(),
    )(shard)


def exchange_shards(parts, *, name):
    def body(in_ref, out_ref, send_sems, recv_sems, local_sem):
        start, finish = _exchange_plan(in_ref, out_ref, send_sems, recv_sems, local_sem)
        start()
        finish()

    return pl.pallas_call(
        body, name=name, out_shape=jax.ShapeDtypeStruct(parts.shape, parts.dtype),
        in_specs=[pl.BlockSpec(memory_space=pl.ANY)], out_specs=pl.BlockSpec(memory_space=pl.ANY),
        scratch_shapes=_comm_scratch(),
    )(parts)


def adamw(parts, w, m, v, *, name):
    _, rows, cols = parts.shape
    tr = PACK_ROW_BLOCK
    c1 = 1.0 - ADAM_B1 ** ADAM_STEP
    c2 = 1.0 - ADAM_B2 ** ADAM_STEP

    def body(p_ref, w_ref, m_ref, v_ref, g_ref, d_ref, nm_ref, nv_ref):
        g = p_ref[0].astype(F32)
        for s in range(1, N_DEV):
            g = g + p_ref[s].astype(F32)
        nm = ADAM_B1 * m_ref[...] + (1.0 - ADAM_B1) * g
        nv = ADAM_B2 * v_ref[...] + (1.0 - ADAM_B2) * jnp.square(g)
        m_hat = nm / c1
        v_hat = nv / c2
        g_ref[...] = g
        d_ref[...] = -ADAM_LR * (m_hat / (jnp.sqrt(v_hat) + ADAM_EPS) + ADAM_WD * w_ref[...])
        nm_ref[...] = nm
        nv_ref[...] = nv

    row = pl.BlockSpec((tr, cols), lambda i: (i, 0))
    shape = jax.ShapeDtypeStruct((rows, cols), F32)
    return pl.pallas_call(
        body, name=name, grid=(rows // tr,),
        in_specs=[pl.BlockSpec((N_DEV, tr, cols), lambda i: (0, i, 0)), row, row, row],
        out_specs=[row, row, row, row], out_shape=[shape, shape, shape, shape],
        compiler_params=_params("parallel"),
    )(parts, w, m, v)


def kernel(x, norm_gains, sb_w_qkv, sb_w_o, conv_w_in, conv_w, conv_w_out, gla_w_in, gla_w_gate_up, gla_b_gate, gla_head_norm, gla_w_o, ffn_w_up, ffn_w_down, loss_target, m_norm_gains, m_sb_w_qkv, m_sb_w_o, m_conv_w_in, m_conv_w, m_conv_w_out, m_gla_w_in, m_gla_w_gate_up, m_gla_b_gate, m_gla_head_norm, m_gla_w_o, m_ffn_w_up, m_ffn_w_down, v_norm_gains, v_sb_w_qkv, v_sb_w_o, v_conv_w_in, v_conv_w, v_conv_w_out, v_gla_w_in, v_gla_w_gate_up, v_gla_b_gate, v_gla_head_norm, v_gla_w_o, v_ffn_w_up, v_ffn_w_down):
    shards = dict(zip(WEIGHTS, (norm_gains, sb_w_qkv, sb_w_o, conv_w_in, conv_w, conv_w_out, gla_w_in,
                                gla_w_gate_up, gla_b_gate, gla_head_norm, gla_w_o, ffn_w_up, ffn_w_down)))
    moments_m = dict(zip(WEIGHTS, (m_norm_gains, m_sb_w_qkv, m_sb_w_o, m_conv_w_in, m_conv_w, m_conv_w_out,
                                   m_gla_w_in, m_gla_w_gate_up, m_gla_b_gate, m_gla_head_norm, m_gla_w_o,
                                   m_ffn_w_up, m_ffn_w_down)))
    moments_v = dict(zip(WEIGHTS, (v_norm_gains, v_sb_w_qkv, v_sb_w_o, v_conv_w_in, v_conv_w, v_conv_w_out,
                                   v_gla_w_in, v_gla_w_gate_up, v_gla_b_gate, v_gla_head_norm, v_gla_w_o,
                                   v_ffn_w_up, v_ffn_w_down)))
    shard_shapes = {n: a.shape for n, a in shards.items()}

    keys = [(n, j) for n in WEIGHTS for j in range(shard_shapes[n][0])]
    alone = [('norm_gains', 0), ('sb_w_qkv', 0)]
    groups = [alone, [k for k in keys if k not in alone]]

    def payload(group):
        segs, rows = _segments(group, shard_shapes, True)
        flat = {(n, j): (lax.bitcast_convert_type(shards[n][j], BF16) if n in F32_PAYLOAD
                         else shards[n][j].astype(BF16)).reshape(-1) for n, j in group}
        return segs, _pack(flat, segs, rows, BF16)

    whole = {n: [None] * shard_shapes[n][0] for n in WEIGHTS}

    def take_gathered(segs, gathered):
        gathered = gathered.reshape(N_DEV, -1, PACK_COLS)
        for n, j in segs:
            piece = _unpack(gathered, segs[(n, j)])
            if n in F32_PAYLOAD:
                piece = lax.bitcast_convert_type(piece.reshape(N_DEV, -1, 2), F32)
            whole[n][j] = _unshard(piece.reshape((N_DEV,) + shard_shapes[n][1:]), SHARD_AXIS[n] - 1)

    segs0, payload0 = payload(groups[0])
    segs1, payload1 = payload(groups[1])
    take_gathered(segs0, all_gather(payload0, name="weights_all_gather"))

    gsegs = [_segments(group, shard_shapes, False) for group in groups]
    received = [None, None]

    def parts(g, grads):
        segs, rows = gsegs[g]
        return _pack({(n, j): _shard_split(grads[(n, j)], SHARD_AXIS[n] - 1).reshape(N_DEV, -1) for n, j in segs},
                     segs, rows, BF16)

    def on_received(buf):
        received[1] = buf

    comm = types.SimpleNamespace(rest_payload=payload1, on_gathered=functools.partial(take_gathered, segs1),
                                 rest_parts=functools.partial(parts, 1), on_received=on_received)

    loss, grad_x, grads = local_step(x[0], whole, loss_target[0], comm)
    loss = lax.psum(loss[0, 0], ("x", "y", "c"))
    received[0] = exchange_shards(parts(0, grads), name="grads_exchange")

    results = {}
    for g in range(2):
        segs, rows = gsegs[g]

        def packed(source):
            return _pack({(n, j): source[n][j].reshape(-1) for n, j in segs}, segs, rows, F32)

        outs = adamw(received[g], packed(shards), packed(moments_m), packed(moments_v), name=f"adamw{g}")
        for n, j in segs:
            results[(n, j)] = [_unpack(buf, segs[(n, j)]).reshape(shard_shapes[n][1:]) for buf in outs]
    stacked = [[jnp.stack([results[(n, j)][o] for j in range(shard_shapes[n][0])]) for n in WEIGHTS] for o in range(4)]
    return (loss, grad_x[None], *stacked[0], *stacked[1], *stacked[2], *stacked[3])
```

```python
import functools
import math
import types

import jax
import jax.numpy as jnp
from jax import lax
from jax.experimental import pallas as pl
from jax.experimental.pallas import tpu as pltpu

F32 = jnp.float32
BF16 = jnp.bfloat16

N_DEV = 8
SB_HEADS = 16
GLA_HEADS = 4
GLA_CHUNK = 64
GLA_SUB = 16
GLA_GATE_RANK = 16
GLA_GATE_NORMALIZER = 16.0
CONV_WIDTH = 3
DEPTH = 4
RMS_EPS = 1e-6
ADAM_LR = 0.001
ADAM_B1 = 0.9
ADAM_B2 = 0.999
ADAM_EPS = 1e-08
ADAM_WD = 0.01
ADAM_STEP = 10

LANES = 128
SB_BLOCK = 256
VMEM_LIMIT_BYTES = 56 * 1024 * 1024
MM_TM, MM_TN, MM_TK = 1024, 1024, 1024
PACK_COLS = 1024
PACK_ROW_ALIGN = 16
PACK_ROW_BLOCK = 128

WEIGHTS = ['norm_gains', 'sb_w_qkv', 'sb_w_o', 'conv_w_in', 'conv_w', 'conv_w_out', 'gla_w_in',
           'gla_w_gate_up', 'gla_b_gate', 'gla_head_norm', 'gla_w_o', 'ffn_w_up', 'ffn_w_down']
SHARD_AXIS = {'norm_gains': 2, 'sb_w_qkv': 2, 'sb_w_o': 1, 'conv_w_in': 2, 'conv_w': 2, 'conv_w_out': 1,
              'gla_w_in': 2, 'gla_w_gate_up': 2, 'gla_b_gate': 1, 'gla_head_norm': 2, 'gla_w_o': 1,
              'ffn_w_up': 2, 'ffn_w_down': 1}
F32_PAYLOAD = ('norm_gains', 'conv_w', 'gla_b_gate', 'gla_head_norm')

_NN = (((1,), (0,)), ((), ()))
_NT = (((1,), (1,)), ((), ()))
_TN = (((0,), (0,)), ((), ()))
_DIMS = {'nn': _NN, 'nt': _NT, 'tn': _TN}


def _params(*semantics):
    return pltpu.CompilerParams(dimension_semantics=semantics, vmem_limit_bytes=VMEM_LIMIT_BYTES)


def _dot(a, b, dims=_NN):
    return lax.dot_general(a.astype(BF16), b.astype(BF16), dims, preferred_element_type=F32)


def _split_hi_lo(x):
    hi = x.astype(BF16)
    lo = (x - hi.astype(F32)).astype(BF16)
    return hi, lo


def _dot_exact_rhs(x, ones_mat, dims=_NN):
    hi, lo = _split_hi_lo(x)
    return (lax.dot_general(hi, ones_mat, dims, preferred_element_type=F32)
            + lax.dot_general(lo, ones_mat, dims, preferred_element_type=F32))


def _dot_exact_lhs(ones_mat, x, dims=_NN):
    hi, lo = _split_hi_lo(x)
    return (lax.dot_general(ones_mat, hi, dims, preferred_element_type=F32)
            + lax.dot_general(ones_mat, lo, dims, preferred_element_type=F32))


def _log_sigmoid(z):
    return jnp.minimum(z, 0.0) - jnp.log(1.0 + jnp.exp(-jnp.abs(z)))


def _sigmoid(z):
    return 1.0 / (1.0 + jnp.exp(-z))


def matmul(a, b, *, mode, name, out_dtype=F32, epi=None, extra=None, tm=MM_TM, tn=MM_TN, tk=MM_TK):
    if mode == 'nn':
        (m, k), (k2, n) = a.shape, b.shape
    elif mode == 'nt':
        (m, k), (n, k2) = a.shape, b.shape
    else:
        (k, m), (k2, n) = a.shape, b.shape
    assert k == k2, (a.shape, b.shape, mode)
    tm, tn, tk = min(tm, m), min(tn, n), min(tk, k)
    assert m % tm == 0 and n % tn == 0 and k % tk == 0, (a.shape, b.shape, mode)
    nk = k // tk
    if mode == 'tn':
        a_spec = pl.BlockSpec((tk, tm), lambda i, j, kk: (kk, i))
    else:
        a_spec = pl.BlockSpec((tm, tk), lambda i, j, kk: (i, kk))
    if mode == 'nt':
        b_spec = pl.BlockSpec((tn, tk), lambda i, j, kk: (j, kk))
    else:
        b_spec = pl.BlockSpec((tk, tn), lambda i, j, kk: (kk, j))
    in_specs, operands = [a_spec, b_spec], [a, b]
    if epi == 'logsig16':
        in_specs.append(pl.BlockSpec((1, tn), lambda i, j, kk: (0, j)))
        operands.append(extra)
    elif epi in ('mul2sqrt', 'add'):
        in_specs.append(pl.BlockSpec((tm, tn), lambda i, j, kk: (i, j)))
        operands.append(extra)

    n_extra = len(operands) - 2

    def body(a_ref, b_ref, *rest):
        e_ref = rest[0] if n_extra else None
        o_ref = rest[n_extra]

        def finish(r):
            if epi == 'relu2':
                r = jnp.square(jnp.maximum(r, 0.0))
            elif epi == 'mul2sqrt':
                r = r * (2.0 * jnp.sqrt(e_ref[...].astype(F32)))
            elif epi == 'add':
                r = r + e_ref[...]
            elif epi == 'logsig16':
                r = _log_sigmoid(r + e_ref[...]) / GLA_GATE_NORMALIZER
            o_ref[...] = r.astype(o_ref.dtype)

        part = _dot(a_ref[...], b_ref[...], _DIMS[mode])
        if nk == 1:
            finish(part)
        else:
            acc_ref = rest[-1]
            kk = pl.program_id(2)

            @pl.when(kk == 0)
            def _():
                acc_ref[...] = part

            @pl.when(kk > 0)
            def _():
                acc_ref[...] += part

            @pl.when(kk == nk - 1)
            def _():
                finish(acc_ref[...])

    return pl.pallas_call(
        body, name=name, grid=(m // tm, n // tn, nk), in_specs=in_specs,
        out_specs=pl.BlockSpec((tm, tn), lambda i, j, kk: (i, j)),
        out_shape=jax.ShapeDtypeStruct((m, n), out_dtype),
        scratch_shapes=[pltpu.VMEM((tm, tn), F32)] if nk > 1 else [],
        compiler_params=_params("parallel", "parallel", "arbitrary"),
    )(*operands)


def rms_fwd(x, gain, *, name, res=None, out_dtype=F32, tm=512):
    t, d = x.shape
    tm = min(tm, t)
    row = pl.BlockSpec((tm, d), lambda i: (i, 0))
    in_specs, operands = [row, pl.BlockSpec((1, d), lambda i: (0, 0))], [x, gain]
    if res is not None:
        in_specs.append(row)
        operands.append(res)

    def body(x_ref, g_ref, *rest):
        xv = x_ref[...]
        r = lax.rsqrt(jnp.mean(xv * xv, axis=-1, keepdims=True) + RMS_EPS)
        y = xv * r * g_ref[...]
        if res is not None:
            y = rest[0][...] + y
        rest[-1][...] = y.astype(out_dtype)

    return pl.pallas_call(
        body, name=name, grid=(t // tm,), in_specs=in_specs, out_specs=row,
        out_shape=jax.ShapeDtypeStruct((t, d), out_dtype), compiler_params=_params("parallel"),
    )(*operands)


def rms_bwd(x, gain, dy, *, name, dres=None, out_dtype=F32, tm=512):
    t, d = x.shape
    tm = min(tm, t)
    row = pl.BlockSpec((tm, d), lambda i: (i, 0))
    vec = pl.BlockSpec((1, d), lambda i: (0, 0))
    in_specs, operands = [row, vec, row], [x, gain, dy]
    if dres is not None:
        in_specs.append(row)
        operands.append(dres)

    def body(x_ref, g_ref, dy_ref, *rest):
        dx_ref, dg_ref = rest[-2], rest[-1]

        @pl.when(pl.program_id(0) == 0)
        def _():
            dg_ref[...] = jnp.zeros_like(dg_ref)

        xv, dyv = x_ref[...], dy_ref[...]
        r = lax.rsqrt(jnp.mean(xv * xv, axis=-1, keepdims=True) + RMS_EPS)
        u = dyv * g_ref[...]
        dx = r * u - xv * (r * r * r * jnp.mean(u * xv, axis=-1, keepdims=True))
        if dres is not None:
            dx = rest[0][...] + dx
        dx_ref[...] = dx.astype(out_dtype)
        dg_ref[...] += jnp.sum(dyv * xv * r, axis=0, keepdims=True)

    return pl.pallas_call(
        body, name=name, grid=(t // tm,), in_specs=in_specs, out_specs=[row, vec],
        out_shape=[jax.ShapeDtypeStruct((t, d), out_dtype), jax.ShapeDtypeStruct((1, d), F32)],
        compiler_params=_params("arbitrary"),
    )(*operands)


def loss_head(y, target, *, name, tm=512):
    t, d = y.shape
    tm = min(tm, t)
    nt = t // tm
    row = pl.BlockSpec((tm, d), lambda i: (i, 0))

    def body(y_ref, t_ref, loss_ref, dy_ref, acc_ref):
        i = pl.program_id(0)

        @pl.when(i == 0)
        def _():
            acc_ref[...] = jnp.zeros_like(acc_ref)

        err = y_ref[...] - t_ref[...]
        dy_ref[...] = err * (1.0 / d)
        acc_ref[...] += jnp.sum(err * err, axis=0, keepdims=True)

        @pl.when(i == nt - 1)
        def _():
            loss_ref[...] = jnp.sum(acc_ref[...], axis=1, keepdims=True) * (0.5 / d)

    return pl.pallas_call(
        body, name=name, grid=(nt,), in_specs=[row, row],
        out_specs=[pl.BlockSpec((1, 1), lambda i: (0, 0)), row],
        out_shape=[jax.ShapeDtypeStruct((1, 1), F32), jax.ShapeDtypeStruct((t, d), F32)],
        scratch_shapes=[pltpu.VMEM((1, d), F32)], compiler_params=_params("arbitrary"),
    )(y, target)


def _sb_block_iota():
    rows = lax.broadcasted_iota(jnp.int32, (SB_BLOCK, SB_BLOCK), 0)
    cols = lax.broadcasted_iota(jnp.int32, (SB_BLOCK, SB_BLOCK), 1)
    return rows, cols


def _sb_logits(q_h, k_blk, mask):
    z = _dot(q_h, k_blk, _NT)
    ls = _log_sigmoid(z)
    lm = ls - z
    if mask is not None:
        lm = jnp.where(mask, lm, 0.0)
    return ls, lm, jnp.sum(lm, axis=1, keepdims=True)


def _sb_weights(ls, lm, mask, tri_strict, later):
    suffix = _dot_exact_rhs(lm, tri_strict)
    w = jnp.exp(ls + suffix + later)
    return w if mask is None else jnp.where(mask, w, 0.0)


def _on_grid_step(p, i):
    return jnp.logical_and(pl.program_id(0) == p, pl.program_id(1) == i)


def sb_fwd(qkv, *, name, gather=None):
    t, d3 = qkv.shape
    d = d3 // 3
    head_dim = d // SB_HEADS
    assert 2 * head_dim == LANES and t % SB_BLOCK == 0
    pairs = d // LANES
    nq = t // SB_BLOCK
    scale = head_dim ** -0.5

    def body(q_ref, k_ref, v_ref, *rest):
        if gather is None:
            compute(q_ref, k_ref, v_ref, *rest)
            return
        x_ref, o_ref, tot_ref, out_ref, send_sems, recv_sems, local_sem = rest
        start, forward, finish = _gather_plan(x_ref, out_ref, send_sems, recv_sems, local_sem)
        pl.when(_on_grid_step(0, 0))(start)
        pl.when(_on_grid_step(3 * pairs // 4, 0))(forward)
        compute(q_ref, k_ref, v_ref, o_ref, tot_ref)
        pl.when(_on_grid_step(pairs - 1, nq - 1))(finish)

    def compute(q_ref, k_ref, v_ref, o_ref, tot_ref):
        qi = pl.program_id(1)
        lane = lax.broadcasted_iota(jnp.int32, (SB_BLOCK, LANES), 1)
        first = lane < head_dim
        q = q_ref[...] * scale
        q2 = jnp.concatenate([jnp.where(first, q, jnp.zeros_like(q)), jnp.where(first, jnp.zeros_like(q), q)], axis=0)
        rows, cols = _sb_block_iota()
        tri = jnp.where(rows > cols, 1.0, 0.0).astype(BF16)
        earlier = cols < rows
        diagonal = jnp.concatenate([earlier, earlier], axis=0)

        def step(kb, carry, mask):
            ks = pl.multiple_of(kb * SB_BLOCK, SB_BLOCK)
            k_blk = k_ref[pl.ds(ks, SB_BLOCK), :]
            v_blk = v_ref[pl.ds(ks, SB_BLOCK), :]
            acc, later = carry
            ls, lm, row = _sb_logits(q2, k_blk, mask)
            w = _sb_weights(ls, lm, mask, tri, later)
            return acc + _dot(w, v_blk), later + row

        out = step(qi, (jnp.zeros((2 * SB_BLOCK, LANES), F32), jnp.zeros((2 * SB_BLOCK, 1), F32)), diagonal)
        out = lax.fori_loop(0, qi // 2, lambda i, carry: step(qi - 2 - 2 * i, step(qi - 1 - 2 * i, carry, None), None), out)
        acc, total = lax.cond(qi % 2 == 1, lambda carry: step(0, carry, None), lambda carry: carry, out)
        o_ref[...] = jnp.where(first, acc[:SB_BLOCK], acc[SB_BLOCK:]).astype(o_ref.dtype)
        tot_ref[...] = jnp.where(first, total[:SB_BLOCK], total[SB_BLOCK:])

    blk = lambda off: pl.BlockSpec((t, LANES), lambda p, i: (0, off + p))
    qblk = pl.BlockSpec((SB_BLOCK, LANES), lambda p, i: (i, p))
    in_specs, operands = [qblk, blk(pairs), blk(2 * pairs)], [qkv, qkv, qkv]
    out_specs = [qblk, qblk]
    out_shape = [jax.ShapeDtypeStruct((t, d), BF16), jax.ShapeDtypeStruct((t, d), F32)]
    if gather is None:
        return pl.pallas_call(
            body, name=name, grid=(pairs, nq), in_specs=in_specs, out_specs=out_specs, out_shape=out_shape,
            compiler_params=_params("parallel", "arbitrary"),
        )(*operands)
    whole = pl.BlockSpec(memory_space=pl.ANY)
    return pl.pallas_call(
        body, name=name, grid=(pairs, nq), in_specs=in_specs + [whole], out_specs=out_specs + [whole],
        out_shape=out_shape + [jax.ShapeDtypeStruct((N_DEV * gather.shape[0], gather.shape[1]), gather.dtype)],
        scratch_shapes=_comm_scratch(), compiler_params=_params("arbitrary", "arbitrary"),
    )(*operands, gather)


def sb_bwd(qkv, tot, do, *, name, exchange=None):
    t, d3 = qkv.shape
    d = d3 // 3
    head_dim = d // SB_HEADS
    pairs = d // LANES
    nq = t // SB_BLOCK
    scale = head_dim ** -0.5

    def body(*refs):
        if exchange is None:
            compute(*refs)
            return
        q_ref, k_ref, v_ref, tot_ref, do_ref, in_ref, dq_ref, dk_ref, dv_ref, out_ref = refs[:10]
        dk_sum, dv_sum, send_sems, recv_sems, local_sem = refs[10:]
        start, finish = _exchange_plan(in_ref, out_ref, send_sems, recv_sems, local_sem)
        pl.when(_on_grid_step(0, 0))(start)
        compute(q_ref, k_ref, v_ref, tot_ref, do_ref, dq_ref, dk_ref, dv_ref, dk_sum, dv_sum)
        pl.when(_on_grid_step(pairs - 1, nq - 1))(finish)

    def compute(q_ref, k_ref, v_ref, tot_ref, do_ref, dq_ref, dk_ref, dv_ref, dk_sum, dv_sum):
        qi = pl.program_id(1)

        @pl.when(qi == 0)
        def _():
            dk_sum[...] = jnp.zeros_like(dk_sum)
            dv_sum[...] = jnp.zeros_like(dv_sum)

        lane = lax.broadcasted_iota(jnp.int32, (SB_BLOCK, LANES), 1)
        first = lane < head_dim
        q, dov, totv = q_ref[...] * scale, do_ref[...], tot_ref[...]
        second = jnp.logical_not(first)
        q2 = jnp.concatenate([jnp.where(s, q, jnp.zeros_like(q)) for s in (first, second)], axis=0)
        do2 = jnp.concatenate([jnp.where(s, dov, jnp.zeros_like(dov)) for s in (first, second)], axis=0)
        tot2 = jnp.concatenate([totv[:, 0:1], totv[:, head_dim:head_dim + 1]], axis=0)
        rows, cols = _sb_block_iota()
        tri_strict = jnp.where(rows > cols, 1.0, 0.0).astype(BF16)
        tri_before = jnp.where(rows < cols, 1.0, 0.0).astype(BF16)
        earlier = cols < rows
        diagonal = jnp.concatenate([earlier, earlier], axis=0)

        def step(kb, carry, mask):
            ks = pl.multiple_of(kb * SB_BLOCK, SB_BLOCK)
            k_blk = k_ref[pl.ds(ks, SB_BLOCK), :]
            v_blk = v_ref[pl.ds(ks, SB_BLOCK), :]
            dq, seen, before = carry
            ls, lm, row = _sb_logits(q2, k_blk, mask)
            seen = seen + row
            w = _sb_weights(ls, lm, mask, tri_strict, tot2 - seen)
            da = _dot(do2, v_blk, _NT) * w
            g = _dot_exact_rhs(da, tri_before) + before
            dz = da - jnp.exp(ls) * (da + g)
            if mask is not None:
                dz = jnp.where(mask, dz, 0.0)
            dk_sum[pl.ds(ks, SB_BLOCK), :] += _dot(dz, q2, _TN)
            dv_sum[pl.ds(ks, SB_BLOCK), :] += _dot(w, do2, _TN)
            return dq + _dot(dz, k_blk * scale), seen, before + jnp.sum(da, axis=1, keepdims=True)

        zero = jnp.zeros((2 * SB_BLOCK, LANES), F32)
        zcol = jnp.zeros((2 * SB_BLOCK, 1), F32)
        out = lax.fori_loop(0, qi, lambda kb, carry: step(kb, carry, None), (zero, zcol, zcol))
        dq = step(qi, out, diagonal)[0]
        dq_ref[...] = jnp.where(first, dq[:SB_BLOCK], dq[SB_BLOCK:]).astype(dq_ref.dtype)

        @pl.when(qi == nq - 1)
        def _():
            dk_ref[...] = dk_sum[...].astype(dk_ref.dtype)
            dv_ref[...] = dv_sum[...].astype(dv_ref.dtype)

    qblk = pl.BlockSpec((SB_BLOCK, LANES), lambda p, i: (i, p))
    col = lambda off: pl.BlockSpec((t, LANES), lambda p, i: (0, off + p))
    shape = jax.ShapeDtypeStruct((t, d), BF16)
    in_specs, operands = [qblk, col(pairs), col(2 * pairs), qblk, qblk], [qkv, qkv, qkv, tot, do]
    out_specs, out_shape = [qblk, col(0), col(0)], [shape, shape, shape]
    sums = [pltpu.VMEM((t, LANES), F32), pltpu.VMEM((t, LANES), F32)]
    if exchange is None:
        return pl.pallas_call(
            body, name=name, grid=(pairs, nq), in_specs=in_specs, out_specs=out_specs, out_shape=out_shape,
            scratch_shapes=sums, compiler_params=_params("parallel", "arbitrary"),
        )(*operands)
    whole = pl.BlockSpec(memory_space=pl.ANY)
    return pl.pallas_call(
        body, name=name, grid=(pairs, nq), in_specs=in_specs + [whole], out_specs=out_specs + [whole],
        out_shape=out_shape + [jax.ShapeDtypeStruct(exchange.shape, exchange.dtype)],
        scratch_shapes=sums + _comm_scratch(), compiler_params=_params("arbitrary", "arbitrary"),
    )(*operands, exchange)


def _shift_down(x, s):
    rows = lax.broadcasted_iota(jnp.int32, x.shape, 0)
    return jnp.where(rows >= s, pltpu.roll(x, s, 0), 0.0)


def _shift_up(x, s):
    t = x.shape[0]
    rows = lax.broadcasted_iota(jnp.int32, x.shape, 0)
    return jnp.where(rows < t - s, pltpu.roll(x, t - s, 0), 0.0)


def conv_fwd(bcu, w, *, name):
    t, d3 = bcu.shape
    d = d3 // 3
    nb = d // LANES
    col = lambda off: pl.BlockSpec((t, LANES), lambda j: (0, off + j))

    def body(b_ref, c_ref, u_ref, w_ref, y_ref):
        hh = c_ref[...] * u_ref[...]
        conv = w_ref[0:1, :] * _shift_down(hh, 2) + w_ref[1:2, :] * _shift_down(hh, 1) + w_ref[2:3, :] * hh
        y_ref[...] = (b_ref[...] * conv).astype(y_ref.dtype)

    return pl.pallas_call(
        body, name=name, grid=(nb,),
        in_specs=[col(0), col(nb), col(2 * nb), pl.BlockSpec((CONV_WIDTH, LANES), lambda j: (0, j))],
        out_specs=col(0), out_shape=jax.ShapeDtypeStruct((t, d), BF16), compiler_params=_params("parallel"),
    )(bcu, bcu, bcu, w)


def conv_bwd(bcu, w, dy, *, name):
    t, d3 = bcu.shape
    d = d3 // 3
    nb = d // LANES
    col = lambda off: pl.BlockSpec((t, LANES), lambda j: (0, off + j))
    wspec = pl.BlockSpec((CONV_WIDTH, LANES), lambda j: (0, j))

    def body(b_ref, c_ref, u_ref, w_ref, dy_ref, db_ref, dc_ref, du_ref, dw_ref):
        c, u, dyv = c_ref[...], u_ref[...], dy_ref[...]
        hh = c * u
        h2, h1 = _shift_down(hh, 2), _shift_down(hh, 1)
        w0, w1, w2 = w_ref[0:1, :], w_ref[1:2, :], w_ref[2:3, :]
        db_ref[...] = (dyv * (w0 * h2 + w1 * h1 + w2 * hh)).astype(db_ref.dtype)
        dconv = dyv * b_ref[...]
        dhh = w2 * dconv + w1 * _shift_up(dconv, 1) + w0 * _shift_up(dconv, 2)
        dc_ref[...] = (dhh * u).astype(dc_ref.dtype)
        du_ref[...] = (dhh * c).astype(du_ref.dtype)
        dw_ref[0:1, :] = jnp.sum(dconv * h2, axis=0, keepdims=True)
        dw_ref[1:2, :] = jnp.sum(dconv * h1, axis=0, keepdims=True)
        dw_ref[2:3, :] = jnp.sum(dconv * hh, axis=0, keepdims=True)

    shape = jax.ShapeDtypeStruct((t, d), BF16)
    return pl.pallas_call(
        body, name=name, grid=(nb,),
        in_specs=[col(0), col(nb), col(2 * nb), wspec, col(0)],
        out_specs=[col(0), col(0), col(0), wspec],
        out_shape=[shape, shape, shape, jax.ShapeDtypeStruct((CONV_WIDTH, d), F32)],
        compiler_params=_params("parallel"),
    )(bcu, bcu, bcu, w, dy)


def _gla_chunk(q_ref, k_ref, lg_ref, scale):
    c = GLA_CHUNK
    rows = lax.broadcasted_iota(jnp.int32, (c, c), 0)
    cols = lax.broadcasted_iota(jnp.int32, (c, c), 1)
    causal = rows >= cols
    tril = jnp.where(causal, 1.0, 0.0).astype(BF16)
    q = q_ref[...] * scale
    k = k_ref[...]
    lg = lg_ref[...]
    cum = _dot_exact_lhs(tril, lg)
    last = cum[c - 1:c, :]
    eq = jnp.exp(cum)
    el = jnp.exp(last - cum)
    return causal, tril, q, k, cum, lg, last, eq, el


def _gla_sub_blocks(q, k, cum, lg):
    key_row = lax.broadcasted_iota(jnp.int32, (GLA_CHUNK, 1), 0)
    out = []
    for lo in range(0, GLA_CHUNK, GLA_SUB):
        hi = lo + GLA_SUB
        ref = cum[lo:lo + 1, :] - lg[lo:lo + 1, :]
        eq = jnp.exp(cum[lo:hi] - ref)
        ek = jnp.where(key_row < hi, jnp.exp(ref - cum), 0.0)
        out.append((slice(lo, hi), eq, ek, q[lo:hi] * eq, k * ek))
    return out


def _gla_scores(subs, causal):
    return jnp.where(causal, jnp.concatenate([_dot(qt, kt, _NT) for _, _, _, qt, kt in subs], axis=0), 0.0)


def gla_fwd(proj, lg, *, name):
    t, d3 = proj.shape
    d = d3 // 3
    dk, dv = d // 2 // GLA_HEADS, d // GLA_HEADS
    assert dk == LANES and dv == 2 * LANES
    c = GLA_CHUNK
    nc = t // c
    scale = dk ** -0.5
    nh = GLA_HEADS

    def body(q_ref, k_ref, v_ref, lg_ref, o_ref, st_out_ref, st_ref):
        @pl.when(pl.program_id(1) == 0)
        def _():
            st_ref[...] = jnp.zeros_like(st_ref)

        causal, _, q, k, cum, lg, last, eq, el = _gla_chunk(q_ref, k_ref, lg_ref, scale)
        v = v_ref[...]
        st = st_ref[...]
        st_out_ref[...] = st
        scores = _gla_scores(_gla_sub_blocks(q, k, cum, lg), causal)
        o_ref[...] = _dot(q * eq, st, _NT) + _dot(scores, v)
        st_ref[...] = st * jnp.exp(last) + _dot(v, k * el, _TN)

    return pl.pallas_call(
        body, name=name, grid=(nh, nc),
        in_specs=[pl.BlockSpec((c, dk), lambda h, i: (i, h)), pl.BlockSpec((c, dk), lambda h, i: (i, nh + h)),
                  pl.BlockSpec((c, dv), lambda h, i: (i, nh + h)), pl.BlockSpec((c, dk), lambda h, i: (i, h))],
        out_specs=[pl.BlockSpec((c, dv), lambda h, i: (i, h)),
                   pl.BlockSpec((None, None, dv, dk), lambda h, i: (h, i, 0, 0))],
        out_shape=[jax.ShapeDtypeStruct((t, d), F32), jax.ShapeDtypeStruct((nh, nc, dv, dk), F32)],
        scratch_shapes=[pltpu.VMEM((dv, dk), F32)], compiler_params=_params("parallel", "arbitrary"),
    )(proj, proj, proj, lg)


def gla_bwd(proj, lg, states, do, *, name):
    t, d3 = proj.shape
    d = d3 // 3
    dk, dv = d // 2 // GLA_HEADS, d // GLA_HEADS
    c = GLA_CHUNK
    nc = t // c
    scale = dk ** -0.5
    nh = GLA_HEADS

    def body(q_ref, k_ref, v_ref, lg_ref, st_ref, do_ref, dq_ref, dk_ref, dv_ref, dzg_ref, dbg_ref, dst_ref):
        @pl.when(pl.program_id(1) == 0)
        def _():
            dst_ref[...] = jnp.zeros_like(dst_ref)
            dbg_ref[...] = jnp.zeros_like(dbg_ref)

        causal, tril, q, k, cum, lg, last, eq, el = _gla_chunk(q_ref, k_ref, lg_ref, scale)
        v, st, dov, dst = v_ref[...], st_ref[...], do_ref[...], dst_ref[...]
        subs = _gla_sub_blocks(q, k, cum, lg)
        qt, kh = q * eq, k * el
        scores = _gla_scores(subs, causal)
        dscores = jnp.where(causal, _dot(dov, v, _NT), 0.0)
        dq_parts = []
        dkh = _dot(v, dst)
        dk = dkh * el
        for rows, eq_sub, ek_sub, qt_sub, kt_sub in subs:
            dq_parts.append(_dot(dscores[rows], kt_sub) * eq_sub)
            dk = dk + _dot(dscores[rows], qt_sub, _TN) * ek_sub
        dq = _dot(dov, st) * eq + jnp.concatenate(dq_parts, axis=0)
        dv_ref[...] = (_dot(scores, dov, _TN) + _dot(kh, dst, _NT)).astype(dv_ref.dtype)
        dq_ref[...] = (dq * scale).astype(dq_ref.dtype)
        dk_ref[...] = dk.astype(dk_ref.dtype)
        e_last = jnp.exp(last)
        dlast = jnp.sum(kh * dkh, axis=0, keepdims=True) + e_last * jnp.sum(dst * st, axis=0, keepdims=True)
        dcum = q * dq - k * dk
        dlg = _dot_exact_lhs(tril, dcum, _TN) + dlast
        dzg = dlg * (1.0 - jnp.exp(lg * GLA_GATE_NORMALIZER)) / GLA_GATE_NORMALIZER
        dzg_ref[...] = dzg.astype(dzg_ref.dtype)
        dbg_ref[...] += jnp.sum(dzg, axis=0, keepdims=True)
        dst_ref[...] = dst * e_last + _dot(dov, qt, _TN)

    rev = lambda i: nc - 1 - i
    half = jax.ShapeDtypeStruct((t, d // 2), BF16)
    return pl.pallas_call(
        body, name=name, grid=(nh, nc),
        in_specs=[pl.BlockSpec((c, dk), lambda h, i: (rev(i), h)), pl.BlockSpec((c, dk), lambda h, i: (rev(i), nh + h)),
                  pl.BlockSpec((c, dv), lambda h, i: (rev(i), nh + h)), pl.BlockSpec((c, dk), lambda h, i: (rev(i), h)),
                  pl.BlockSpec((None, None, dv, dk), lambda h, i: (h, rev(i), 0, 0)),
                  pl.BlockSpec((c, dv), lambda h, i: (rev(i), h))],
        out_specs=[pl.BlockSpec((c, dk), lambda h, i: (rev(i), h)), pl.BlockSpec((c, dk), lambda h, i: (rev(i), h)),
                   pl.BlockSpec((c, dv), lambda h, i: (rev(i), h)), pl.BlockSpec((c, dk), lambda h, i: (rev(i), h)),
                   pl.BlockSpec((1, dk), lambda h, i: (0, h))],
        out_shape=[half, half, jax.ShapeDtypeStruct((t, d), BF16), half, jax.ShapeDtypeStruct((1, d // 2), F32)],
        scratch_shapes=[pltpu.VMEM((dv, dk), F32)], compiler_params=_params("parallel", "arbitrary"),
    )(proj, proj, proj, lg, states, do)


def gla_post_fwd(o, proj, head_norm, *, name, tm=512):
    t, d = o.shape
    dv = d // GLA_HEADS
    tm = min(tm, t)

    def body(o_ref, g_ref, hn_ref, y_ref):
        for h in range(GLA_HEADS):
            sl = slice(h * dv, (h + 1) * dv)
            ov, gv = o_ref[:, sl], g_ref[:, sl]
            r = lax.rsqrt(jnp.mean(ov * ov, axis=-1, keepdims=True) + RMS_EPS)
            y_ref[:, sl] = ((ov * r * hn_ref[:, sl]) * (gv * _sigmoid(gv))).astype(y_ref.dtype)

    row = pl.BlockSpec((tm, d), lambda i: (i, 0))
    return pl.pallas_call(
        body, name=name, grid=(t // tm,),
        in_specs=[row, pl.BlockSpec((tm, d), lambda i: (i, 2)), pl.BlockSpec((1, d), lambda i: (0, 0))],
        out_specs=row, out_shape=jax.ShapeDtypeStruct((t, d), BF16), compiler_params=_params("parallel"),
    )(o, proj, head_norm)


def gla_post_bwd(o, proj, head_norm, dy, *, name, tm=512):
    t, d = o.shape
    dv = d // GLA_HEADS
    tm = min(tm, t)

    def body(o_ref, g_ref, hn_ref, dy_ref, do_ref, dg_ref, dhn_ref):
        @pl.when(pl.program_id(0) == 0)
        def _():
            dhn_ref[...] = jnp.zeros_like(dhn_ref)

        for h in range(GLA_HEADS):
            sl = slice(h * dv, (h + 1) * dv)
            ov, gv, dyv, hn = o_ref[:, sl], g_ref[:, sl], dy_ref[:, sl], hn_ref[:, sl]
            r = lax.rsqrt(jnp.mean(ov * ov, axis=-1, keepdims=True) + RMS_EPS)
            sg = _sigmoid(gv)
            silu = gv * sg
            on = ov * r * hn
            dg_ref[:, sl] = (dyv * on * (sg * (1.0 + gv * (1.0 - sg)))).astype(dg_ref.dtype)
            don = dyv * silu
            u = don * hn
            do_ref[:, sl] = (r * u - ov * (r * r * r * jnp.mean(u * ov, axis=-1, keepdims=True))).astype(do_ref.dtype)
            dhn_ref[:, sl] += jnp.sum(don * ov * r, axis=0, keepdims=True)

    row = pl.BlockSpec((tm, d), lambda i: (i, 0))
    vec = pl.BlockSpec((1, d), lambda i: (0, 0))
    shape = jax.ShapeDtypeStruct((t, d), BF16)
    return pl.pallas_call(
        body, name=name, grid=(t // tm,),
        in_specs=[row, pl.BlockSpec((tm, d), lambda i: (i, 2)), vec, row],
        out_specs=[row, row, vec], out_shape=[shape, shape, jax.ShapeDtypeStruct((1, d), F32)],
        compiler_params=_params("arbitrary"),
    )(o, proj, head_norm, dy)


def _ffn_fwd(h, gains, w_up, w_down, tag):
    xn = rms_fwd(h, gains[2], name=f"{tag}_ffn_norm", out_dtype=BF16)
    act = matmul(xn, w_up, mode='nn', epi='relu2', out_dtype=BF16, name=f"{tag}_ffn_up")
    f = matmul(act, w_down, mode='nn', name=f"{tag}_ffn_down")
    h_out = rms_fwd(f, gains[3], res=h, name=f"{tag}_ffn_out")
    return h_out, (h, xn, act, f)


def _ffn_bwd(dh, saved, gains, w_up, w_down, tag):
    h, xn, act, f = saved
    df, dg3 = rms_bwd(f, gains[3], dh, out_dtype=BF16, name=f"{tag}_ffn_out_bwd")
    du = matmul(df, w_down, mode='nt', epi='mul2sqrt', extra=act, out_dtype=BF16, name=f"{tag}_ffn_da")
    dw_down = matmul(act, df, mode='tn', name=f"{tag}_ffn_dwdown")
    dw_up = matmul(xn, du, mode='tn', name=f"{tag}_ffn_dwup")
    dxn = matmul(du, w_up, mode='nt', name=f"{tag}_ffn_dxn")
    dh_in, dg2 = rms_bwd(h, gains[2], dxn, dres=dh, name=f"{tag}_ffn_norm_bwd")
    return dh_in, dg2, dg3, dw_up, dw_down


def _sb_layer_fwd(xn, w, j, tag, comm=None):
    qkv = matmul(xn, w['sb_w_qkv'][j], mode='nn', out_dtype=BF16, name=f"{tag}_qkv")
    if comm is None:
        o, tot = sb_fwd(qkv, name=f"{tag}_sb")
    else:
        o, tot, gathered = sb_fwd(qkv, name=f"{tag}_sb", gather=comm.rest_payload)
        comm.on_gathered(gathered)
    m = matmul(o, w['sb_w_o'][j], mode='nn', name=f"{tag}_wo")
    return m, (qkv, o, tot)


def _sb_layer_bwd(dm, xn, saved, w, j, tag, comm=None, grads=None):
    qkv, o, tot = saved
    do = matmul(dm, w['sb_w_o'][j], mode='nt', out_dtype=BF16, name=f"{tag}_do")
    dw_o = matmul(o, dm, mode='tn', name=f"{tag}_dwo")
    if comm is None:
        dq, dk, dv = sb_bwd(qkv, tot, do, name=f"{tag}_sb_bwd")
    else:
        parts = comm.rest_parts({**grads, ('sb_w_o', j): dw_o})
        dq, dk, dv, received = sb_bwd(qkv, tot, do, name=f"{tag}_sb_bwd", exchange=parts)
        comm.on_received(received)
    dqkv = jnp.concatenate([dq, dk, dv], axis=1)
    dw_qkv = matmul(xn, dqkv, mode='tn', name=f"{tag}_dwqkv")
    dxn = matmul(dqkv, w['sb_w_qkv'][j], mode='nt', name=f"{tag}_dxn")
    return dxn, {('sb_w_qkv', j): dw_qkv, ('sb_w_o', j): dw_o}


def _conv_layer_fwd(xn, w, j, tag):
    bcu = matmul(xn, w['conv_w_in'][j], mode='nn', name=f"{tag}_in")
    y = conv_fwd(bcu, w['conv_w'][j], name=f"{tag}_conv")
    m = matmul(y, w['conv_w_out'][j], mode='nn', name=f"{tag}_out")
    return m, (bcu, y)


def _conv_layer_bwd(dm, xn, saved, w, j, tag):
    bcu, y = saved
    dy = matmul(dm, w['conv_w_out'][j], mode='nt', name=f"{tag}_dy")
    dw_out = matmul(y, dm, mode='tn', name=f"{tag}_dwout")
    db, dc, du, dw_conv = conv_bwd(bcu, w['conv_w'][j], dy, name=f"{tag}_conv_bwd")
    dbcu = jnp.concatenate([db, dc, du], axis=1)
    dw_in = matmul(xn, dbcu, mode='tn', name=f"{tag}_dwin")
    dxn = matmul(dbcu, w['conv_w_in'][j], mode='nt', name=f"{tag}_dxn")
    return dxn, {('conv_w_in', j): dw_in, ('conv_w', j): dw_conv, ('conv_w_out', j): dw_out}


def _gla_split(w_in, w_gate_up):
    d = w_in.shape[0]
    w_main = w_in[:, :3 * d]
    w_a = jnp.pad(w_in[:, 3 * d:], ((0, 0), (0, LANES - GLA_GATE_RANK)))
    w_gu = jnp.pad(w_gate_up, ((0, LANES - GLA_GATE_RANK), (0, 0)))
    return w_main, w_a, w_gu


def _gla_layer_fwd(xn, w, j, tag):
    w_main, w_a, w_gu = _gla_split(w['gla_w_in'][j], w['gla_w_gate_up'][j])
    proj = matmul(xn, w_main, mode='nn', name=f"{tag}_in")
    a_low = matmul(xn, w_a, mode='nn', out_dtype=BF16, name=f"{tag}_alow")
    lg = matmul(a_low, w_gu, mode='nn', epi='logsig16', extra=w['gla_b_gate'][j][None, :], name=f"{tag}_gate")
    o, states = gla_fwd(proj, lg, name=f"{tag}_gla")
    hn = w['gla_head_norm'][j].reshape(1, -1)
    y = gla_post_fwd(o, proj, hn, name=f"{tag}_post")
    m = matmul(y, w['gla_w_o'][j], mode='nn', name=f"{tag}_wo")
    return m, (proj, a_low, lg, o, states, y)


def _gla_layer_bwd(dm, xn, saved, w, j, tag):
    proj, a_low, lg, o, states, y = saved
    w_main, w_a, w_gu = _gla_split(w['gla_w_in'][j], w['gla_w_gate_up'][j])
    hn = w['gla_head_norm'][j].reshape(1, -1)
    dy = matmul(dm, w['gla_w_o'][j], mode='nt', name=f"{tag}_dy")
    dw_o = matmul(y, dm, mode='tn', name=f"{tag}_dwo")
    do, dg, dhn = gla_post_bwd(o, proj, hn, dy, name=f"{tag}_post_bwd")
    dq, dk, dv, dzg, dbg = gla_bwd(proj, lg, states, do, name=f"{tag}_gla_bwd")
    da_low = matmul(dzg, w_gu, mode='nt', out_dtype=BF16, name=f"{tag}_dalow")
    dw_gu = matmul(a_low, dzg, mode='tn', name=f"{tag}_dwgu")[:GLA_GATE_RANK]
    dproj = jnp.concatenate([dq, dk, dv, dg], axis=1)
    dw_main = matmul(xn, dproj, mode='tn', name=f"{tag}_dwin")
    dw_a = matmul(xn, da_low, mode='tn', name=f"{tag}_dwa")[:, :GLA_GATE_RANK]
    dxn_a = matmul(da_low, w_a, mode='nt', name=f"{tag}_dxn_a")
    dxn = matmul(dproj, w_main, mode='nt', epi='add', extra=dxn_a, name=f"{tag}_dxn")
    grads = {('gla_w_in', j): jnp.concatenate([dw_main, dw_a], axis=1), ('gla_w_gate_up', j): dw_gu,
             ('gla_b_gate', j): dbg[0], ('gla_head_norm', j): dhn.reshape(w['gla_head_norm'][j].shape),
             ('gla_w_o', j): dw_o}
    return dxn, grads


_MIXERS = ((_sb_layer_fwd, _sb_layer_bwd), (_conv_layer_fwd, _conv_layer_bwd), (_gla_layer_fwd, _gla_layer_bwd))


def local_step(x, w, target, comm=None):
    depth = len(w['norm_gains'])
    h = x
    tape = []
    for i in range(depth):
        kind, j = i % 3, i // 3
        tag = f"l{i}"
        extra = {'comm': comm} if (comm is not None and i == 0) else {}
        gains = [w['norm_gains'][i][s][None, :] for s in range(4)]
        xn = rms_fwd(h, gains[0], name=f"{tag}_mix_norm", out_dtype=BF16)
        m, saved = _MIXERS[kind][0](xn, w, j, tag, **extra)
        h_mid = rms_fwd(m, gains[1], res=h, name=f"{tag}_mix_out")
        h_out, ffn_saved = _ffn_fwd(h_mid, gains, w['ffn_w_up'][i], w['ffn_w_down'][i], tag)
        tape.append((h, xn, m, saved, ffn_saved, gains))
        h = h_out
    loss, dh = loss_head(h, target, name="loss_head")

    grads = {}
    for i in reversed(range(depth)):
        kind, j = i % 3, i // 3
        tag = f"l{i}"
        h_in, xn, m, saved, ffn_saved, gains = tape[i]
        dg = [None] * 4
        dh, dg[2], dg[3], grads[('ffn_w_up', i)], grads[('ffn_w_down', i)] = _ffn_bwd(
            dh, ffn_saved, gains, w['ffn_w_up'][i], w['ffn_w_down'][i], tag)
        dm, dg[1] = rms_bwd(m, gains[1], dh, out_dtype=BF16, name=f"{tag}_mix_out_bwd")
        extra = {'comm': comm, 'grads': grads} if (comm is not None and i == 0) else {}
        dxn, g = _MIXERS[kind][1](dm, xn, saved, w, j, tag, **extra)
        grads.update(g)
        dh, dg[0] = rms_bwd(h_in, gains[0], dxn, dres=dh, name=f"{tag}_mix_norm_bwd")
        grads[('norm_gains', i)] = jnp.concatenate(dg, axis=0)
    return loss, dh, grads


def _segments(keys, shard_shapes, f32_as_pairs):
    segs, row = {}, 0
    for name, j in keys:
        n = math.prod(shard_shapes[name][1:]) * (2 if f32_as_pairs and name in F32_PAYLOAD else 1)
        nrows = -(-n // (PACK_COLS * PACK_ROW_ALIGN)) * PACK_ROW_ALIGN
        segs[(name, j)] = (row, nrows, n)
        row += nrows
    return segs, -(-row // PACK_ROW_BLOCK) * PACK_ROW_BLOCK


def _pack(parts, segs, total_rows, dtype):
    pieces, row = [], 0
    for key in segs:
        _, nrows, n = segs[key]
        p = parts[key].astype(dtype)
        lead = p.shape[:-1]
        p = jnp.pad(p, [(0, 0)] * len(lead) + [(0, nrows * PACK_COLS - n)])
        pieces.append(p.reshape(lead + (nrows, PACK_COLS)))
        row += nrows
    if total_rows > row:
        pieces.append(jnp.zeros(lead + (total_rows - row, PACK_COLS), dtype))
    return jnp.concatenate(pieces, axis=-2)


def _unpack(buf, seg):
    first, nrows, n = seg
    piece = buf[..., first:first + nrows, :]
    return piece.reshape(piece.shape[:-2] + (nrows * PACK_COLS,))[..., :n]


def _unshard(gathered, axis):
    moved = jnp.moveaxis(gathered, 0, axis)
    shape = moved.shape
    return moved.reshape(shape[:axis] + (shape[axis] * shape[axis + 1],) + shape[axis + 2:])


def _shard_split(full, axis):
    shape = full.shape
    cut = full.reshape(shape[:axis] + (N_DEV, shape[axis] // N_DEV) + shape[axis + 1:])
    return jnp.moveaxis(cut, axis, 0)


def _mesh_position():
    return lax.axis_index("x"), lax.axis_index("y"), lax.axis_index("c")


def _comm_scratch():
    return [pltpu.SemaphoreType.DMA((N_DEV - 1,)), pltpu.SemaphoreType.DMA((N_DEV - 1,)), pltpu.SemaphoreType.DMA]


def _gather_plan(x_ref, out_ref, send_sems, recv_sems, local_sem):
    rows = x_ref.shape[0]
    x, y, c = _mesh_position()
    me, sibling = (x, y, c), (x, y, 1 - c)
    chips = [(1 - x, y), (x, 1 - y), (1 - x, 1 - y)]

    def block(px, py, pc):
        return out_ref.at[pl.ds((4 * px + 2 * py + pc) * rows, rows), :]

    def copy(k, blk, to, src=None):
        return pltpu.make_async_remote_copy(
            src_ref=block(*blk) if src is None else src, dst_ref=block(*blk),
            send_sem=send_sems.at[k], recv_sem=recv_sems.at[k],
            device_id=to, device_id_type=pl.DeviceIdType.MESH)

    mine = pltpu.make_async_copy(x_ref, block(*me), local_sem)
    first = [copy(0, me, sibling, src=x_ref)]
    first += [copy(1 + j, me, (*chip, c), src=x_ref) for j, chip in enumerate(chips)]
    passed = [copy(4 + j, (*chip, c), sibling) for j, chip in enumerate(chips)]

    def start():
        mine.start()
        for cp in first:
            cp.start()

    def forward():
        for j, chip in enumerate(chips):
            copy(1 + j, (*chip, c), me).wait_recv()
            passed[j].start()

    def finish():
        copy(0, sibling, me).wait_recv()
        for j, chip in enumerate(chips):
            copy(4 + j, (*chip, 1 - c), me).wait_recv()
        for cp in first + passed:
            cp.wait_send()
        mine.wait()

    return start, forward, finish


def _exchange_plan(in_ref, out_ref, send_sems, recv_sems, local_sem):
    x, y, c = _mesh_position()
    my_id = 4 * x + 2 * y + c
    mine = pltpu.make_async_copy(in_ref.at[my_id], out_ref.at[my_id], local_sem)

    def copy(k, receive):
        px = 1 - x if k & 4 else x
        py = 1 - y if k & 2 else y
        pc = 1 - c if k & 1 else c
        peer_id = 4 * px + 2 * py + pc
        return pltpu.make_async_remote_copy(
            src_ref=in_ref.at[peer_id], dst_ref=out_ref.at[peer_id if receive else my_id],
            send_sem=send_sems.at[k - 1], recv_sem=recv_sems.at[k - 1],
            device_id=(px, py, pc), device_id_type=pl.DeviceIdType.MESH)

    def start():
        mine.start()
        for k in range(1, N_DEV):
            copy(k, False).start()

    def finish():
        for k in range(1, N_DEV):
            copy(k, True).wait_recv()
        for k in range(1, N_DEV):
            copy(k, False).wait_send()
        mine.wait()

    return start, finish


def all_gather(shard, *, name):
    rows, cols = shard.shape

    def body(x_ref, out_ref, send_sems, recv_sems, local_sem):
        start, forward, finish = _gather_plan(x_ref, out_ref, send_sems, recv_sems, local_sem)
        start()
        forward()
        finish()

    return pl.pallas_call(
        body, name=name, out_shape=jax.ShapeDtypeStruct((N_DEV * rows, cols), shard.dtype),
        in_specs=[pl.BlockSpec(memory_space=pl.ANY)], out_specs=pl.BlockSpec(memory_space=pl.ANY),
        scratch_shapes=_comm_scratch(),
    )(shard)


def exchange_shards(parts, *, name):
    def body(in_ref, out_ref, send_sems, recv_sems, local_sem):
        start, finish = _exchange_plan(in_ref, out_ref, send_sems, recv_sems, local_sem)
        start()
        finish()

    return pl.pallas_call(
        body, name=name, out_shape=jax.ShapeDtypeStruct(parts.shape, parts.dtype),
        in_specs=[pl.BlockSpec(memory_space=pl.ANY)], out_specs=pl.BlockSpec(memory_space=pl.ANY),
        scratch_shapes=_comm_scratch(),
    )(parts)


def adamw(parts, w, m, v, *, name):
    _, rows, cols = parts.shape
    tr = PACK_ROW_BLOCK
    c1 = 1.0 - ADAM_B1 ** ADAM_STEP
    c2 = 1.0 - ADAM_B2 ** ADAM_STEP

    def body(p_ref, w_ref, m_ref, v_ref, g_ref, d_ref, nm_ref, nv_ref):
        g = p_ref[0].astype(F32)
        for s in range(1, N_DEV):
            g = g + p_ref[s].astype(F32)
        nm = ADAM_B1 * m_ref[...] + (1.0 - ADAM_B1) * g
        nv = ADAM_B2 * v_ref[...] + (1.0 - ADAM_B2) * jnp.square(g)
        m_hat = nm / c1
        v_hat = nv / c2
        g_ref[...] = g
        d_ref[...] = -ADAM_LR * (m_hat / (jnp.sqrt(v_hat) + ADAM_EPS) + ADAM_WD * w_ref[...])
        nm_ref[...] = nm
        nv_ref[...] = nv

    row = pl.BlockSpec((tr, cols), lambda i: (i, 0))
    shape = jax.ShapeDtypeStruct((rows, cols), F32)
    return pl.pallas_call(
        body, name=name, grid=(rows // tr,),
        in_specs=[pl.BlockSpec((N_DEV, tr, cols), lambda i: (0, i, 0)), row, row, row],
        out_specs=[row, row, row, row], out_shape=[shape, shape, shape, shape],
        compiler_params=_params("parallel"),
    )(parts, w, m, v)


def kernel(x, norm_gains, sb_w_qkv, sb_w_o, conv_w_in, conv_w, conv_w_out, gla_w_in, gla_w_gate_up, gla_b_gate, gla_head_norm, gla_w_o, ffn_w_up, ffn_w_down, loss_target, m_norm_gains, m_sb_w_qkv, m_sb_w_o, m_conv_w_in, m_conv_w, m_conv_w_out, m_gla_w_in, m_gla_w_gate_up, m_gla_b_gate, m_gla_head_norm, m_gla_w_o, m_ffn_w_up, m_ffn_w_down, v_norm_gains, v_sb_w_qkv, v_sb_w_o, v_conv_w_in, v_conv_w, v_conv_w_out, v_gla_w_in, v_gla_w_gate_up, v_gla_b_gate, v_gla_head_norm, v_gla_w_o, v_ffn_w_up, v_ffn_w_down):
    shards = dict(zip(WEIGHTS, (norm_gains, sb_w_qkv, sb_w_o, conv_w_in, conv_w, conv_w_out, gla_w_in,
                                gla_w_gate_up, gla_b_gate, gla_head_norm, gla_w_o, ffn_w_up, ffn_w_down)))
    moments_m = dict(zip(WEIGHTS, (m_norm_gains, m_sb_w_qkv, m_sb_w_o, m_conv_w_in, m_conv_w, m_conv_w_out,
                                   m_gla_w_in, m_gla_w_gate_up, m_gla_b_gate, m_gla_head_norm, m_gla_w_o,
                                   m_ffn_w_up, m_ffn_w_down)))
    moments_v = dict(zip(WEIGHTS, (v_norm_gains, v_sb_w_qkv, v_sb_w_o, v_conv_w_in, v_conv_w, v_conv_w_out,
                                   v_gla_w_in, v_gla_w_gate_up, v_gla_b_gate, v_gla_head_norm, v_gla_w_o,
                                   v_ffn_w_up, v_ffn_w_down)))
    shard_shapes = {n: a.shape for n, a in shards.items()}

    keys = [(n, j) for n in WEIGHTS for j in range(shard_shapes[n][0])]
    alone = [('norm_gains', 0), ('sb_w_qkv', 0)]
    groups = [alone, [k for k in keys if k not in alone]]

    def payload(group):
        segs, rows = _segments(group, shard_shapes, True)
        flat = {(n, j): (lax.bitcast_convert_type(shards[n][j], BF16) if n in F32_PAYLOAD
                         else shards[n][j].astype(BF16)).reshape(-1) for n, j in group}
        return segs, _pack(flat, segs, rows, BF16)

    whole = {n: [None] * shard_shapes[n][0] for n in WEIGHTS}

    def take_gathered(segs, gathered):
        gathered = gathered.reshape(N_DEV, -1, PACK_COLS)
        for n, j in segs:
            piece = _unpack(gathered, segs[(n, j)])
            if n in F32_PAYLOAD:
                piece = lax.bitcast_convert_type(piece.reshape(N_DEV, -1, 2), F32)
            whole[n][j] = _unshard(piece.reshape((N_DEV,) + shard_shapes[n][1:]), SHARD_AXIS[n] - 1)

    segs0, payload0 = payload(groups[0])
    segs1, payload1 = payload(groups[1])
    take_gathered(segs0, all_gather(payload0, name="weights_all_gather"))

    gsegs = [_segments(group, shard_shapes, False) for group in groups]
    received = [None, None]

    def parts(g, grads):
        segs, rows = gsegs[g]
        return _pack({(n, j): _shard_split(grads[(n, j)], SHARD_AXIS[n] - 1).reshape(N_DEV, -1) for n, j in segs},
                     segs, rows, BF16)

    def on_received(buf):
        received[1] = buf

    comm = types.SimpleNamespace(rest_payload=payload1, on_gathered=functools.partial(take_gathered, segs1),
                                 rest_parts=functools.partial(parts, 1), on_received=on_received)

    loss, grad_x, grads = local_step(x[0], whole, loss_target[0], comm)
    loss = lax.psum(loss[0, 0], ("x", "y", "c"))
    received[0] = exchange_shards(parts(0, grads), name="grads_exchange")

    results = {}
    for g in range(2):
        segs, rows = gsegs[g]

        def packed(source):
            return _pack({(n, j): source[n][j].reshape(-1) for n, j in segs}, segs, rows, F32)

        outs = adamw(received[g], packed(shards), packed(moments_m), packed(moments_v), name=f"adamw{g}")
        for n, j in segs:
            results[(n, j)] = [_unpack(buf, segs[(n, j)]).reshape(shard_shapes[n][1:]) for buf in outs]
    stacked = [[jnp.stack([results[(n, j)][o] for j in range(shard_shapes[n][0])]) for n in WEIGHTS] for o in range(4)]
    return (loss, grad_x[None], *stacked[0], *stacked[1], *stacked[2], *stacked[3])
```

```python
import functools
import math
import types

import jax
import jax.numpy as jnp
from jax import lax
from jax.experimental import pallas as pl
from jax.experimental.pallas import tpu as pltpu

F32 = jnp.float32
BF16 = jnp.bfloat16

N_DEV = 8
SB_HEADS = 16
GLA_HEADS = 4
GLA_CHUNK = 64
GLA_SUB = 16
GLA_GATE_RANK = 16
GLA_GATE_NORMALIZER = 16.0
CONV_WIDTH = 3
DEPTH = 4
RMS_EPS = 1e-6
ADAM_LR = 0.001
ADAM_B1 = 0.9
ADAM_B2 = 0.999
ADAM_EPS = 1e-08
ADAM_WD = 0.01
ADAM_STEP = 10

LANES = 128
SB_BLOCK = 256
SB_FWD_Q_BLOCK = 512
SB_BWD_Q_BLOCK = 512
VMEM_LIMIT_BYTES = 56 * 1024 * 1024
MM_TM, MM_TN, MM_TK = 1024, 1024, 1024
PACK_COLS = 1024
PACK_ROW_ALIGN = 16
PACK_ROW_BLOCK = 128

WEIGHTS = ['norm_gains', 'sb_w_qkv', 'sb_w_o', 'conv_w_in', 'conv_w', 'conv_w_out', 'gla_w_in',
           'gla_w_gate_up', 'gla_b_gate', 'gla_head_norm', 'gla_w_o', 'ffn_w_up', 'ffn_w_down']
SHARD_AXIS = {'norm_gains': 2, 'sb_w_qkv': 2, 'sb_w_o': 1, 'conv_w_in': 2, 'conv_w': 2, 'conv_w_out': 1,
              'gla_w_in': 2, 'gla_w_gate_up': 2, 'gla_b_gate': 1, 'gla_head_norm': 2, 'gla_w_o': 1,
              'ffn_w_up': 2, 'ffn_w_down': 1}
F32_PAYLOAD = ('norm_gains', 'conv_w', 'gla_b_gate', 'gla_head_norm')

_NN = (((1,), (0,)), ((), ()))
_NT = (((1,), (1,)), ((), ()))
_TN = (((0,), (0,)), ((), ()))
_DIMS = {'nn': _NN, 'nt': _NT, 'tn': _TN}


def _params(*semantics):
    return pltpu.CompilerParams(dimension_semantics=semantics, vmem_limit_bytes=VMEM_LIMIT_BYTES)


def _dot(a, b, dims=_NN):
    return lax.dot_general(a.astype(BF16), b.astype(BF16), dims, preferred_element_type=F32)


def _split_hi_lo(x):
    hi = x.astype(BF16)
    lo = (x - hi.astype(F32)).astype(BF16)
    return hi, lo


def _dot_exact_rhs(x, ones_mat, dims=_NN):
    hi, lo = _split_hi_lo(x)
    return (lax.dot_general(hi, ones_mat, dims, preferred_element_type=F32)
            + lax.dot_general(lo, ones_mat, dims, preferred_element_type=F32))


def _dot_exact_lhs(ones_mat, x, dims=_NN):
    hi, lo = _split_hi_lo(x)
    return (lax.dot_general(ones_mat, hi, dims, preferred_element_type=F32)
            + lax.dot_general(ones_mat, lo, dims, preferred_element_type=F32))


def _log_sigmoid(z):
    return jnp.minimum(z, 0.0) - jnp.log(1.0 + jnp.exp(-jnp.abs(z)))


def _sigmoid(z):
    return 1.0 / (1.0 + jnp.exp(-z))


def matmul(a, b, *, mode, name, out_dtype=F32, epi=None, extra=None, tm=MM_TM, tn=MM_TN, tk=MM_TK):
    if mode == 'nn':
        (m, k), (k2, n) = a.shape, b.shape
    elif mode == 'nt':
        (m, k), (n, k2) = a.shape, b.shape
    else:
        (k, m), (k2, n) = a.shape, b.shape
    assert k == k2, (a.shape, b.shape, mode)
    tm, tn, tk = min(tm, m), min(tn, n), min(tk, k)
    assert m % tm == 0 and n % tn == 0 and k % tk == 0, (a.shape, b.shape, mode)
    nk = k // tk
    if mode == 'tn':
        a_spec = pl.BlockSpec((tk, tm), lambda i, j, kk: (kk, i))
    else:
        a_spec = pl.BlockSpec((tm, tk), lambda i, j, kk: (i, kk))
    if mode == 'nt':
        b_spec = pl.BlockSpec((tn, tk), lambda i, j, kk: (j, kk))
    else:
        b_spec = pl.BlockSpec((tk, tn), lambda i, j, kk: (kk, j))
    in_specs, operands = [a_spec, b_spec], [a, b]
    if epi == 'logsig16':
        in_specs.append(pl.BlockSpec((1, tn), lambda i, j, kk: (0, j)))
        operands.append(extra)
    elif epi in ('mul2sqrt', 'add'):
        in_specs.append(pl.BlockSpec((tm, tn), lambda i, j, kk: (i, j)))
        operands.append(extra)

    n_extra = len(operands) - 2

    def body(a_ref, b_ref, *rest):
        e_ref = rest[0] if n_extra else None
        o_ref = rest[n_extra]

        def finish(r):
            if epi == 'relu2':
                r = jnp.square(jnp.maximum(r, 0.0))
            elif epi == 'mul2sqrt':
                r = r * (2.0 * jnp.sqrt(e_ref[...].astype(F32)))
            elif epi == 'add':
                r = r + e_ref[...]
            elif epi == 'logsig16':
                r = _log_sigmoid(r + e_ref[...]) / GLA_GATE_NORMALIZER
            o_ref[...] = r.astype(o_ref.dtype)

        part = _dot(a_ref[...], b_ref[...], _DIMS[mode])
        if nk == 1:
            finish(part)
        else:
            acc_ref = rest[-1]
            kk = pl.program_id(2)

            @pl.when(kk == 0)
            def _():
                acc_ref[...] = part

            @pl.when(kk > 0)
            def _():
                acc_ref[...] += part

            @pl.when(kk == nk - 1)
            def _():
                finish(acc_ref[...])

    return pl.pallas_call(
        body, name=name, grid=(m // tm, n // tn, nk), in_specs=in_specs,
        out_specs=pl.BlockSpec((tm, tn), lambda i, j, kk: (i, j)),
        out_shape=jax.ShapeDtypeStruct((m, n), out_dtype),
        scratch_shapes=[pltpu.VMEM((tm, tn), F32)] if nk > 1 else [],
        compiler_params=_params("parallel", "parallel", "arbitrary"),
    )(*operands)


def rms_fwd(x, gain, *, name, res=None, out_dtype=F32, tm=512):
    t, d = x.shape
    tm = min(tm, t)
    row = pl.BlockSpec((tm, d), lambda i: (i, 0))
    in_specs, operands = [row, pl.BlockSpec((1, d), lambda i: (0, 0))], [x, gain]
    if res is not None:
        in_specs.append(row)
        operands.append(res)

    def body(x_ref, g_ref, *rest):
        xv = x_ref[...]
        r = lax.rsqrt(jnp.mean(xv * xv, axis=-1, keepdims=True) + RMS_EPS)
        y = xv * r * g_ref[...]
        if res is not None:
            y = rest[0][...] + y
        rest[-1][...] = y.astype(out_dtype)

    return pl.pallas_call(
        body, name=name, grid=(t // tm,), in_specs=in_specs, out_specs=row,
        out_shape=jax.ShapeDtypeStruct((t, d), out_dtype), compiler_params=_params("parallel"),
    )(*operands)


def rms_bwd(x, gain, dy, *, name, dres=None, out_dtype=F32, tm=512):
    t, d = x.shape
    tm = min(tm, t)
    row = pl.BlockSpec((tm, d), lambda i: (i, 0))
    vec = pl.BlockSpec((1, d), lambda i: (0, 0))
    in_specs, operands = [row, vec, row], [x, gain, dy]
    if dres is not None:
        in_specs.append(row)
        operands.append(dres)

    def body(x_ref, g_ref, dy_ref, *rest):
        dx_ref, dg_ref = rest[-2], rest[-1]

        @pl.when(pl.program_id(0) == 0)
        def _():
            dg_ref[...] = jnp.zeros_like(dg_ref)

        xv, dyv = x_ref[...], dy_ref[...]
        r = lax.rsqrt(jnp.mean(xv * xv, axis=-1, keepdims=True) + RMS_EPS)
        u = dyv * g_ref[...]
        dx = r * u - xv * (r * r * r * jnp.mean(u * xv, axis=-1, keepdims=True))
        if dres is not None:
            dx = rest[0][...] + dx
        dx_ref[...] = dx.astype(out_dtype)
        dg_ref[...] += jnp.sum(dyv * xv * r, axis=0, keepdims=True)

    return pl.pallas_call(
        body, name=name, grid=(t // tm,), in_specs=in_specs, out_specs=[row, vec],
        out_shape=[jax.ShapeDtypeStruct((t, d), out_dtype), jax.ShapeDtypeStruct((1, d), F32)],
        compiler_params=_params("arbitrary"),
    )(*operands)


def loss_head(y, target, *, name, tm=512):
    t, d = y.shape
    tm = min(tm, t)
    nt = t // tm
    row = pl.BlockSpec((tm, d), lambda i: (i, 0))

    def body(y_ref, t_ref, loss_ref, dy_ref, acc_ref):
        i = pl.program_id(0)

        @pl.when(i == 0)
        def _():
            acc_ref[...] = jnp.zeros_like(acc_ref)

        err = y_ref[...] - t_ref[...]
        dy_ref[...] = err * (1.0 / d)
        acc_ref[...] += jnp.sum(err * err, axis=0, keepdims=True)

        @pl.when(i == nt - 1)
        def _():
            loss_ref[...] = jnp.sum(acc_ref[...], axis=1, keepdims=True) * (0.5 / d)

    return pl.pallas_call(
        body, name=name, grid=(nt,), in_specs=[row, row],
        out_specs=[pl.BlockSpec((1, 1), lambda i: (0, 0)), row],
        out_shape=[jax.ShapeDtypeStruct((1, 1), F32), jax.ShapeDtypeStruct((t, d), F32)],
        scratch_shapes=[pltpu.VMEM((1, d), F32)], compiler_params=_params("arbitrary"),
    )(y, target)


def _sb_block_iota():
    rows = lax.broadcasted_iota(jnp.int32, (SB_BLOCK, SB_BLOCK), 0)
    cols = lax.broadcasted_iota(jnp.int32, (SB_BLOCK, SB_BLOCK), 1)
    return rows, cols


def _sb_logits(q_h, k_blk, mask):
    z = _dot(q_h, k_blk, _NT)
    ls = _log_sigmoid(z)
    lm = ls - z
    if mask is not None:
        lm = jnp.where(mask, lm, 0.0)
    return ls, lm, jnp.sum(lm, axis=1, keepdims=True)


def _sb_weights(ls, lm, mask, tri_strict, later):
    suffix = _dot_exact_rhs(lm, tri_strict)
    w = jnp.exp(ls + suffix + later)
    return w if mask is None else jnp.where(mask, w, 0.0)


def _on_grid_step(p, i):
    return jnp.logical_and(pl.program_id(0) == p, pl.program_id(1) == i)


def sb_fwd(qkv, *, name, gather=None):
    t, d3 = qkv.shape
    d = d3 // 3
    head_dim = d // SB_HEADS
    qb, kb_rows = SB_FWD_Q_BLOCK, SB_BLOCK
    assert 2 * head_dim == LANES and t % qb == 0
    pairs = d // LANES
    per_q = qb // kb_rows
    nq = t // qb
    scale = head_dim ** -0.5

    def body(q_ref, k_ref, v_ref, *rest):
        if gather is None:
            compute(q_ref, k_ref, v_ref, *rest)
            return
        x_ref, o_ref, tot_ref, out_ref, send_sems, recv_sems, local_sem = rest
        start, forward, finish = _gather_plan(x_ref, out_ref, send_sems, recv_sems, local_sem)
        pl.when(_on_grid_step(0, 0))(start)
        pl.when(_on_grid_step(3 * pairs // 4, 0))(forward)
        compute(q_ref, k_ref, v_ref, o_ref, tot_ref)
        pl.when(_on_grid_step(pairs - 1, nq - 1))(finish)

    def compute(q_ref, k_ref, v_ref, o_ref, tot_ref):
        qi = pl.program_id(1)
        lane = lax.broadcasted_iota(jnp.int32, (qb, LANES), 1)
        first = lane < head_dim
        q = q_ref[...] * scale
        q2 = jnp.concatenate([jnp.where(first, q, jnp.zeros_like(q)), jnp.where(first, jnp.zeros_like(q), q)], axis=0)
        rows, cols = _sb_block_iota()
        tri = jnp.where(rows > cols, 1.0, 0.0).astype(BF16)
        q_row = lax.broadcasted_iota(jnp.int32, (qb, kb_rows), 0)
        k_col = lax.broadcasted_iota(jnp.int32, (qb, kb_rows), 1)
        diagonal = [jnp.concatenate([m * kb_rows + k_col < q_row] * 2, axis=0) for m in range(per_q)]

        def step(kb, carry, mask):
            ks = pl.multiple_of(kb * SB_BLOCK, SB_BLOCK)
            k_blk = k_ref[pl.ds(ks, SB_BLOCK), :]
            v_blk = v_ref[pl.ds(ks, SB_BLOCK), :]
            acc, later = carry
            ls, lm, row = _sb_logits(q2, k_blk, mask)
            w = _sb_weights(ls, lm, mask, tri, later)
            return acc + _dot(w, v_blk), later + row

        out = (jnp.zeros((2 * qb, LANES), F32), jnp.zeros((2 * qb, 1), F32))
        for m in reversed(range(per_q)):
            out = step(per_q * qi + m, out, diagonal[m])
        acc, total = lax.fori_loop(0, per_q * qi, lambda i, carry: step(per_q * qi - 1 - i, carry, None), out)
        o_ref[...] = jnp.where(first, acc[:qb], acc[qb:]).astype(o_ref.dtype)
        tot_ref[...] = jnp.where(first, total[:qb], total[qb:])

    blk = lambda off: pl.BlockSpec((t, LANES), lambda p, i: (0, off + p))
    qblk = pl.BlockSpec((qb, LANES), lambda p, i: (i, p))
    in_specs, operands = [qblk, blk(pairs), blk(2 * pairs)], [qkv, qkv, qkv]
    out_specs = [qblk, qblk]
    out_shape = [jax.ShapeDtypeStruct((t, d), BF16), jax.ShapeDtypeStruct((t, d), F32)]
    if gather is None:
        return pl.pallas_call(
            body, name=name, grid=(pairs, nq), in_specs=in_specs, out_specs=out_specs, out_shape=out_shape,
            compiler_params=_params("parallel", "arbitrary"),
        )(*operands)
    whole = pl.BlockSpec(memory_space=pl.ANY)
    return pl.pallas_call(
        body, name=name, grid=(pairs, nq), in_specs=in_specs + [whole], out_specs=out_specs + [whole],
        out_shape=out_shape + [jax.ShapeDtypeStruct((N_DEV * gather.shape[0], gather.shape[1]), gather.dtype)],
        scratch_shapes=_comm_scratch(), compiler_params=_params("arbitrary", "arbitrary"),
    )(*operands, gather)


def sb_bwd(qkv, tot, do, *, name, exchange=None):
    t, d3 = qkv.shape
    d = d3 // 3
    head_dim = d // SB_HEADS
    pairs = d // LANES
    qb, kb_rows = SB_BWD_Q_BLOCK, SB_BLOCK
    per_q = qb // kb_rows
    nq = t // qb
    scale = head_dim ** -0.5

    def body(*refs):
        if exchange is None:
            compute(*refs)
            return
        q_ref, k_ref, v_ref, tot_ref, do_ref, in_ref, dq_ref, dk_ref, dv_ref, out_ref = refs[:10]
        dk_sum, dv_sum, send_sems, recv_sems, local_sem = refs[10:]
        start, finish = _exchange_plan(in_ref, out_ref, send_sems, recv_sems, local_sem)
        pl.when(_on_grid_step(0, 0))(start)
        compute(q_ref, k_ref, v_ref, tot_ref, do_ref, dq_ref, dk_ref, dv_ref, dk_sum, dv_sum)
        pl.when(_on_grid_step(pairs - 1, nq - 1))(finish)

    def compute(q_ref, k_ref, v_ref, tot_ref, do_ref, dq_ref, dk_ref, dv_ref, dk_sum, dv_sum):
        qi = pl.program_id(1)

        @pl.when(qi == 0)
        def _():
            dk_sum[...] = jnp.zeros_like(dk_sum)
            dv_sum[...] = jnp.zeros_like(dv_sum)

        lane = lax.broadcasted_iota(jnp.int32, (qb, LANES), 1)
        first = lane < head_dim
        q, dov, totv = q_ref[...] * scale, do_ref[...], tot_ref[...]
        second = jnp.logical_not(first)
        q2 = jnp.concatenate([jnp.where(s, q, jnp.zeros_like(q)) for s in (first, second)], axis=0)
        do2 = jnp.concatenate([jnp.where(s, dov, jnp.zeros_like(dov)) for s in (first, second)], axis=0)
        tot2 = jnp.concatenate([totv[:, 0:1], totv[:, head_dim:head_dim + 1]], axis=0)
        rows, cols = _sb_block_iota()
        tri_strict = jnp.where(rows > cols, 1.0, 0.0).astype(BF16)
        tri_before = jnp.where(rows < cols, 1.0, 0.0).astype(BF16)
        q_row = lax.broadcasted_iota(jnp.int32, (qb, kb_rows), 0)
        k_col = lax.broadcasted_iota(jnp.int32, (qb, kb_rows), 1)
        diagonal = [jnp.concatenate([m * kb_rows + k_col < q_row] * 2, axis=0) for m in range(per_q)]

        def step(kb, carry, mask):
            ks = pl.multiple_of(kb * SB_BLOCK, SB_BLOCK)
            k_blk = k_ref[pl.ds(ks, SB_BLOCK), :]
            v_blk = v_ref[pl.ds(ks, SB_BLOCK), :]
            dq, seen, before = carry
            ls, lm, row = _sb_logits(q2, k_blk, mask)
            seen = seen + row
            w = _sb_weights(ls, lm, mask, tri_strict, tot2 - seen)
            da = _dot(do2, v_blk, _NT) * w
            g = _dot_exact_rhs(da, tri_before) + before
            dz = da - jnp.exp(ls) * (da + g)
            if mask is not None:
                dz = jnp.where(mask, dz, 0.0)
            dk_sum[pl.ds(ks, SB_BLOCK), :] += _dot(dz, q2, _TN)
            dv_sum[pl.ds(ks, SB_BLOCK), :] += _dot(w, do2, _TN)
            return dq + _dot(dz, k_blk * scale), seen, before + jnp.sum(da, axis=1, keepdims=True)

        zero = jnp.zeros((2 * qb, LANES), F32)
        zcol = jnp.zeros((2 * qb, 1), F32)
        out = lax.fori_loop(0, per_q * qi, lambda kb, carry: step(kb, carry, None), (zero, zcol, zcol))
        for m in range(per_q):
            out = step(per_q * qi + m, out, diagonal[m])
        dq = out[0]
        dq_ref[...] = jnp.where(first, dq[:qb], dq[qb:]).astype(dq_ref.dtype)

        @pl.when(qi == nq - 1)
        def _():
            dk_ref[...] = dk_sum[...].astype(dk_ref.dtype)
            dv_ref[...] = dv_sum[...].astype(dv_ref.dtype)

    qblk = pl.BlockSpec((qb, LANES), lambda p, i: (i, p))
    col = lambda off: pl.BlockSpec((t, LANES), lambda p, i: (0, off + p))
    shape = jax.ShapeDtypeStruct((t, d), BF16)
    in_specs, operands = [qblk, col(pairs), col(2 * pairs), qblk, qblk], [qkv, qkv, qkv, tot, do]
    out_specs, out_shape = [qblk, col(0), col(0)], [shape, shape, shape]
    sums = [pltpu.VMEM((t, LANES), F32), pltpu.VMEM((t, LANES), F32)]
    if exchange is None:
        return pl.pallas_call(
            body, name=name, grid=(pairs, nq), in_specs=in_specs, out_specs=out_specs, out_shape=out_shape,
            scratch_shapes=sums, compiler_params=_params("parallel", "arbitrary"),
        )(*operands)
    whole = pl.BlockSpec(memory_space=pl.ANY)
    return pl.pallas_call(
        body, name=name, grid=(pairs, nq), in_specs=in_specs + [whole], out_specs=out_specs + [whole],
        out_shape=out_shape + [jax.ShapeDtypeStruct(exchange.shape, exchange.dtype)],
        scratch_shapes=sums + _comm_scratch(), compiler_params=_params("arbitrary", "arbitrary"),
    )(*operands, exchange)


def _shift_down(x, s):
    rows = lax.broadcasted_iota(jnp.int32, x.shape, 0)
    return jnp.where(rows >= s, pltpu.roll(x, s, 0), 0.0)


def _shift_up(x, s):
    t = x.shape[0]
    rows = lax.broadcasted_iota(jnp.int32, x.shape, 0)
    return jnp.where(rows < t - s, pltpu.roll(x, t - s, 0), 0.0)


def conv_fwd(bcu, w, *, name):
    t, d3 = bcu.shape
    d = d3 // 3
    nb = d // LANES
    col = lambda off: pl.BlockSpec((t, LANES), lambda j: (0, off + j))

    def body(b_ref, c_ref, u_ref, w_ref, y_ref):
        hh = c_ref[...] * u_ref[...]
        conv = w_ref[0:1, :] * _shift_down(hh, 2) + w_ref[1:2, :] * _shift_down(hh, 1) + w_ref[2:3, :] * hh
        y_ref[...] = (b_ref[...] * conv).astype(y_ref.dtype)

    return pl.pallas_call(
        body, name=name, grid=(nb,),
        in_specs=[col(0), col(nb), col(2 * nb), pl.BlockSpec((CONV_WIDTH, LANES), lambda j: (0, j))],
        out_specs=col(0), out_shape=jax.ShapeDtypeStruct((t, d), BF16), compiler_params=_params("parallel"),
    )(bcu, bcu, bcu, w)


def conv_bwd(bcu, w, dy, *, name):
    t, d3 = bcu.shape
    d = d3 // 3
    nb = d // LANES
    col = lambda off: pl.BlockSpec((t, LANES), lambda j: (0, off + j))
    wspec = pl.BlockSpec((CONV_WIDTH, LANES), lambda j: (0, j))

    def body(b_ref, c_ref, u_ref, w_ref, dy_ref, db_ref, dc_ref, du_ref, dw_ref):
        c, u, dyv = c_ref[...], u_ref[...], dy_ref[...]
        hh = c * u
        h2, h1 = _shift_down(hh, 2), _shift_down(hh, 1)
        w0, w1, w2 = w_ref[0:1, :], w_ref[1:2, :], w_ref[2:3, :]
        db_ref[...] = (dyv * (w0 * h2 + w1 * h1 + w2 * hh)).astype(db_ref.dtype)
        dconv = dyv * b_ref[...]
        dhh = w2 * dconv + w1 * _shift_up(dconv, 1) + w0 * _shift_up(dconv, 2)
        dc_ref[...] = (dhh * u).astype(dc_ref.dtype)
        du_ref[...] = (dhh * c).astype(du_ref.dtype)
        dw_ref[0:1, :] = jnp.sum(dconv * h2, axis=0, keepdims=True)
        dw_ref[1:2, :] = jnp.sum(dconv * h1, axis=0, keepdims=True)
        dw_ref[2:3, :] = jnp.sum(dconv * hh, axis=0, keepdims=True)

    shape = jax.ShapeDtypeStruct((t, d), BF16)
    return pl.pallas_call(
        body, name=name, grid=(nb,),
        in_specs=[col(0), col(nb), col(2 * nb), wspec, col(0)],
        out_specs=[col(0), col(0), col(0), wspec],
        out_shape=[shape, shape, shape, jax.ShapeDtypeStruct((CONV_WIDTH, d), F32)],
        compiler_params=_params("parallel"),
    )(bcu, bcu, bcu, w, dy)


def _gla_chunk(q_ref, k_ref, lg_ref, scale):
    c = GLA_CHUNK
    rows = lax.broadcasted_iota(jnp.int32, (c, c), 0)
    cols = lax.broadcasted_iota(jnp.int32, (c, c), 1)
    causal = rows >= cols
    tril = jnp.where(causal, 1.0, 0.0).astype(BF16)
    q = q_ref[...] * scale
    k = k_ref[...]
    lg = lg_ref[...]
    cum = _dot_exact_lhs(tril, lg)
    last = cum[c - 1:c, :]
    eq = jnp.exp(cum)
    el = jnp.exp(last - cum)
    return causal, tril, q, k, cum, lg, last, eq, el


def _gla_sub_blocks(q, k, cum, lg):
    key_row = lax.broadcasted_iota(jnp.int32, (GLA_CHUNK, 1), 0)
    out = []
    for lo in range(0, GLA_CHUNK, GLA_SUB):
        hi = lo + GLA_SUB
        ref = cum[lo:lo + 1, :] - lg[lo:lo + 1, :]
        eq = jnp.exp(cum[lo:hi] - ref)
        ek = jnp.where(key_row < hi, jnp.exp(ref - cum), 0.0)
        out.append((slice(lo, hi), eq, ek, q[lo:hi] * eq, k * ek))
    return out


def _gla_scores(subs, causal):
    return jnp.where(causal, jnp.concatenate([_dot(qt, kt, _NT) for _, _, _, qt, kt in subs], axis=0), 0.0)


def gla_fwd(proj, lg, *, name):
    t, d3 = proj.shape
    d = d3 // 3
    dk, dv = d // 2 // GLA_HEADS, d // GLA_HEADS
    assert dk == LANES and dv == 2 * LANES
    c = GLA_CHUNK
    nc = t // c
    scale = dk ** -0.5
    nh = GLA_HEADS

    def body(q_ref, k_ref, v_ref, lg_ref, o_ref, st_out_ref, st_ref):
        @pl.when(pl.program_id(1) == 0)
        def _():
            st_ref[...] = jnp.zeros_like(st_ref)

        causal, _, q, k, cum, lg, last, eq, el = _gla_chunk(q_ref, k_ref, lg_ref, scale)
        v = v_ref[...]
        st = st_ref[...]
        st_out_ref[...] = st
        scores = _gla_scores(_gla_sub_blocks(q, k, cum, lg), causal)
        o_ref[...] = _dot(q * eq, st, _NT) + _dot(scores, v)
        st_ref[...] = st * jnp.exp(last) + _dot(v, k * el, _TN)

    return pl.pallas_call(
        body, name=name, grid=(nh, nc),
        in_specs=[pl.BlockSpec((c, dk), lambda h, i: (i, h)), pl.BlockSpec((c, dk), lambda h, i: (i, nh + h)),
                  pl.BlockSpec((c, dv), lambda h, i: (i, nh + h)), pl.BlockSpec((c, dk), lambda h, i: (i, h))],
        out_specs=[pl.BlockSpec((c, dv), lambda h, i: (i, h)),
                   pl.BlockSpec((None, None, dv, dk), lambda h, i: (h, i, 0, 0))],
        out_shape=[jax.ShapeDtypeStruct((t, d), F32), jax.ShapeDtypeStruct((nh, nc, dv, dk), F32)],
        scratch_shapes=[pltpu.VMEM((dv, dk), F32)], compiler_params=_params("parallel", "arbitrary"),
    )(proj, proj, proj, lg)


def gla_bwd(proj, lg, states, do, *, name):
    t, d3 = proj.shape
    d = d3 // 3
    dk, dv = d // 2 // GLA_HEADS, d // GLA_HEADS
    c = GLA_CHUNK
    nc = t // c
    scale = dk ** -0.5
    nh = GLA_HEADS

    def body(q_ref, k_ref, v_ref, lg_ref, st_ref, do_ref, dq_ref, dk_ref, dv_ref, dzg_ref, dbg_ref, dst_ref):
        @pl.when(pl.program_id(1) == 0)
        def _():
            dst_ref[...] = jnp.zeros_like(dst_ref)
            dbg_ref[...] = jnp.zeros_like(dbg_ref)

        causal, tril, q, k, cum, lg, last, eq, el = _gla_chunk(q_ref, k_ref, lg_ref, scale)
        v, st, dov, dst = v_ref[...], st_ref[...], do_ref[...], dst_ref[...]
        subs = _gla_sub_blocks(q, k, cum, lg)
        qt, kh = q * eq, k * el
        scores = _gla_scores(subs, causal)
        dscores = jnp.where(causal, _dot(dov, v, _NT), 0.0)
        dq_parts = []
        dkh = _dot(v, dst)
        dk = dkh * el
        for rows, eq_sub, ek_sub, qt_sub, kt_sub in subs:
            dq_parts.append(_dot(dscores[rows], kt_sub) * eq_sub)
            dk = dk + _dot(dscores[rows], qt_sub, _TN) * ek_sub
        dq = _dot(dov, st) * eq + jnp.concatenate(dq_parts, axis=0)
        dv_ref[...] = (_dot(scores, dov, _TN) + _dot(kh, dst, _NT)).astype(dv_ref.dtype)
        dq_ref[...] = (dq * scale).astype(dq_ref.dtype)
        dk_ref[...] = dk.astype(dk_ref.dtype)
        e_last = jnp.exp(last)
        dlast = jnp.sum(kh * dkh, axis=0, keepdims=True) + e_last * jnp.sum(dst * st, axis=0, keepdims=True)
        dcum = q * dq - k * dk
        dlg = _dot_exact_lhs(tril, dcum, _TN) + dlast
        dzg = dlg * (1.0 - jnp.exp(lg * GLA_GATE_NORMALIZER)) / GLA_GATE_NORMALIZER
        dzg_ref[...] = dzg.astype(dzg_ref.dtype)
        dbg_ref[...] += jnp.sum(dzg, axis=0, keepdims=True)
        dst_ref[...] = dst * e_last + _dot(dov, qt, _TN)

    rev = lambda i: nc - 1 - i
    half = jax.ShapeDtypeStruct((t, d // 2), BF16)
    return pl.pallas_call(
        body, name=name, grid=(nh, nc),
        in_specs=[pl.BlockSpec((c, dk), lambda h, i: (rev(i), h)), pl.BlockSpec((c, dk), lambda h, i: (rev(i), nh + h)),
                  pl.BlockSpec((c, dv), lambda h, i: (rev(i), nh + h)), pl.BlockSpec((c, dk), lambda h, i: (rev(i), h)),
                  pl.BlockSpec((None, None, dv, dk), lambda h, i: (h, rev(i), 0, 0)),
                  pl.BlockSpec((c, dv), lambda h, i: (rev(i), h))],
        out_specs=[pl.BlockSpec((c, dk), lambda h, i: (rev(i), h)), pl.BlockSpec((c, dk), lambda h, i: (rev(i), h)),
                   pl.BlockSpec((c, dv), lambda h, i: (rev(i), h)), pl.BlockSpec((c, dk), lambda h, i: (rev(i), h)),
                   pl.BlockSpec((1, dk), lambda h, i: (0, h))],
        out_shape=[half, half, jax.ShapeDtypeStruct((t, d), BF16), half, jax.ShapeDtypeStruct((1, d // 2), F32)],
        scratch_shapes=[pltpu.VMEM((dv, dk), F32)], compiler_params=_params("parallel", "arbitrary"),
    )(proj, proj, proj, lg, states, do)


def gla_post_fwd(o, proj, head_norm, *, name, tm=512):
    t, d = o.shape
    dv = d // GLA_HEADS
    tm = min(tm, t)

    def body(o_ref, g_ref, hn_ref, y_ref):
        for h in range(GLA_HEADS):
            sl = slice(h * dv, (h + 1) * dv)
            ov, gv = o_ref[:, sl], g_ref[:, sl]
            r = lax.rsqrt(jnp.mean(ov * ov, axis=-1, keepdims=True) + RMS_EPS)
            y_ref[:, sl] = ((ov * r * hn_ref[:, sl]) * (gv * _sigmoid(gv))).astype(y_ref.dtype)

    row = pl.BlockSpec((tm, d), lambda i: (i, 0))
    return pl.pallas_call(
        body, name=name, grid=(t // tm,),
        in_specs=[row, pl.BlockSpec((tm, d), lambda i: (i, 2)), pl.BlockSpec((1, d), lambda i: (0, 0))],
        out_specs=row, out_shape=jax.ShapeDtypeStruct((t, d), BF16), compiler_params=_params("parallel"),
    )(o, proj, head_norm)


def gla_post_bwd(o, proj, head_norm, dy, *, name, tm=512):
    t, d = o.shape
    dv = d // GLA_HEADS
    tm = min(tm, t)

    def body(o_ref, g_ref, hn_ref, dy_ref, do_ref, dg_ref, dhn_ref):
        @pl.when(pl.program_id(0) == 0)
        def _():
            dhn_ref[...] = jnp.zeros_like(dhn_ref)

        for h in range(GLA_HEADS):
            sl = slice(h * dv, (h + 1) * dv)
            ov, gv, dyv, hn = o_ref[:, sl], g_ref[:, sl], dy_ref[:, sl], hn_ref[:, sl]
            r = lax.rsqrt(jnp.mean(ov * ov, axis=-1, keepdims=True) + RMS_EPS)
            sg = _sigmoid(gv)
            silu = gv * sg
            on = ov * r * hn
            dg_ref[:, sl] = (dyv * on * (sg * (1.0 + gv * (1.0 - sg)))).astype(dg_ref.dtype)
            don = dyv * silu
            u = don * hn
            do_ref[:, sl] = (r * u - ov * (r * r * r * jnp.mean(u * ov, axis=-1, keepdims=True))).astype(do_ref.dtype)
            dhn_ref[:, sl] += jnp.sum(don * ov * r, axis=0, keepdims=True)

    row = pl.BlockSpec((tm, d), lambda i: (i, 0))
    vec = pl.BlockSpec((1, d), lambda i: (0, 0))
    shape = jax.ShapeDtypeStruct((t, d), BF16)
    return pl.pallas_call(
        body, name=name, grid=(t // tm,),
        in_specs=[row, pl.BlockSpec((tm, d), lambda i: (i, 2)), vec, row],
        out_specs=[row, row, vec], out_shape=[shape, shape, jax.ShapeDtypeStruct((1, d), F32)],
        compiler_params=_params("arbitrary"),
    )(o, proj, head_norm, dy)


def _ffn_fwd(h, gains, w_up, w_down, tag):
    xn = rms_fwd(h, gains[2], name=f"{tag}_ffn_norm", out_dtype=BF16)
    act = matmul(xn, w_up, mode='nn', epi='relu2', out_dtype=BF16, name=f"{tag}_ffn_up")
    f = matmul(act, w_down, mode='nn', name=f"{tag}_ffn_down")
    h_out = rms_fwd(f, gains[3], res=h, name=f"{tag}_ffn_out")
    return h_out, (h, xn, act, f)


def _ffn_bwd(dh, saved, gains, w_up, w_down, tag):
    h, xn, act, f = saved
    df, dg3 = rms_bwd(f, gains[3], dh, out_dtype=BF16, name=f"{tag}_ffn_out_bwd")
    du = matmul(df, w_down, mode='nt', epi='mul2sqrt', extra=act, out_dtype=BF16, name=f"{tag}_ffn_da")
    dw_down = matmul(act, df, mode='tn', name=f"{tag}_ffn_dwdown")
    dw_up = matmul(xn, du, mode='tn', name=f"{tag}_ffn_dwup")
    dxn = matmul(du, w_up, mode='nt', name=f"{tag}_ffn_dxn")
    dh_in, dg2 = rms_bwd(h, gains[2], dxn, dres=dh, name=f"{tag}_ffn_norm_bwd")
    return dh_in, dg2, dg3, dw_up, dw_down


def _sb_layer_fwd(xn, w, j, tag, comm=None):
    qkv = matmul(xn, w['sb_w_qkv'][j], mode='nn', out_dtype=BF16, name=f"{tag}_qkv")
    if comm is None:
        o, tot = sb_fwd(qkv, name=f"{tag}_sb")
    else:
        o, tot, gathered = sb_fwd(qkv, name=f"{tag}_sb", gather=comm.rest_payload)
        comm.on_gathered(gathered)
    m = matmul(o, w['sb_w_o'][j], mode='nn', name=f"{tag}_wo")
    return m, (qkv, o, tot)


def _sb_layer_bwd(dm, xn, saved, w, j, tag, comm=None, grads=None):
    qkv, o, tot = saved
    do = matmul(dm, w['sb_w_o'][j], mode='nt', out_dtype=BF16, name=f"{tag}_do")
    dw_o = matmul(o, dm, mode='tn', name=f"{tag}_dwo")
    if comm is None:
        dq, dk, dv = sb_bwd(qkv, tot, do, name=f"{tag}_sb_bwd")
    else:
        parts = comm.rest_parts({**grads, ('sb_w_o', j): dw_o})
        dq, dk, dv, received = sb_bwd(qkv, tot, do, name=f"{tag}_sb_bwd", exchange=parts)
        comm.on_received(received)
    dqkv = jnp.concatenate([dq, dk, dv], axis=1)
    dw_qkv = matmul(xn, dqkv, mode='tn', name=f"{tag}_dwqkv")
    dxn = matmul(dqkv, w['sb_w_qkv'][j], mode='nt', name=f"{tag}_dxn")
    return dxn, {('sb_w_qkv', j): dw_qkv, ('sb_w_o', j): dw_o}


def _conv_layer_fwd(xn, w, j, tag):
    bcu = matmul(xn, w['conv_w_in'][j], mode='nn', name=f"{tag}_in")
    y = conv_fwd(bcu, w['conv_w'][j], name=f"{tag}_conv")
    m = matmul(y, w['conv_w_out'][j], mode='nn', name=f"{tag}_out")
    return m, (bcu, y)


def _conv_layer_bwd(dm, xn, saved, w, j, tag):
    bcu, y = saved
    dy = matmul(dm, w['conv_w_out'][j], mode='nt', name=f"{tag}_dy")
    dw_out = matmul(y, dm, mode='tn', name=f"{tag}_dwout")
    db, dc, du, dw_conv = conv_bwd(bcu, w['conv_w'][j], dy, name=f"{tag}_conv_bwd")
    dbcu = jnp.concatenate([db, dc, du], axis=1)
    dw_in = matmul(xn, dbcu, mode='tn', name=f"{tag}_dwin")
    dxn = matmul(dbcu, w['conv_w_in'][j], mode='nt', name=f"{tag}_dxn")
    return dxn, {('conv_w_in', j): dw_in, ('conv_w', j): dw_conv, ('conv_w_out', j): dw_out}


def _gla_split(w_in, w_gate_up):
    d = w_in.shape[0]
    w_main = w_in[:, :3 * d]
    w_a = jnp.pad(w_in[:, 3 * d:], ((0, 0), (0, LANES - GLA_GATE_RANK)))
    w_gu = jnp.pad(w_gate_up, ((0, LANES - GLA_GATE_RANK), (0, 0)))
    return w_main, w_a, w_gu


def _gla_layer_fwd(xn, w, j, tag):
    w_main, w_a, w_gu = _gla_split(w['gla_w_in'][j], w['gla_w_gate_up'][j])
    proj = matmul(xn, w_main, mode='nn', name=f"{tag}_in")
    a_low = matmul(xn, w_a, mode='nn', out_dtype=BF16, name=f"{tag}_alow")
    lg = matmul(a_low, w_gu, mode='nn', epi='logsig16', extra=w['gla_b_gate'][j][None, :], name=f"{tag}_gate")
    o, states = gla_fwd(proj, lg, name=f"{tag}_gla")
    hn = w['gla_head_norm'][j].reshape(1, -1)
    y = gla_post_fwd(o, proj, hn, name=f"{tag}_post")
    m = matmul(y, w['gla_w_o'][j], mode='nn', name=f"{tag}_wo")
    return m, (proj, a_low, lg, o, states, y)


def _gla_layer_bwd(dm, xn, saved, w, j, tag):
    proj, a_low, lg, o, states, y = saved
    w_main, w_a, w_gu = _gla_split(w['gla_w_in'][j], w['gla_w_gate_up'][j])
    hn = w['gla_head_norm'][j].reshape(1, -1)
    dy = matmul(dm, w['gla_w_o'][j], mode='nt', name=f"{tag}_dy")
    dw_o = matmul(y, dm, mode='tn', name=f"{tag}_dwo")
    do, dg, dhn = gla_post_bwd(o, proj, hn, dy, name=f"{tag}_post_bwd")
    dq, dk, dv, dzg, dbg = gla_bwd(proj, lg, states, do, name=f"{tag}_gla_bwd")
    da_low = matmul(dzg, w_gu, mode='nt', out_dtype=BF16, name=f"{tag}_dalow")
    dw_gu = matmul(a_low, dzg, mode='tn', name=f"{tag}_dwgu")[:GLA_GATE_RANK]
    dproj = jnp.concatenate([dq, dk, dv, dg], axis=1)
    dw_main = matmul(xn, dproj, mode='tn', name=f"{tag}_dwin")
    dw_a = matmul(xn, da_low, mode='tn', name=f"{tag}_dwa")[:, :GLA_GATE_RANK]
    dxn_a = matmul(da_low, w_a, mode='nt', name=f"{tag}_dxn_a")
    dxn = matmul(dproj, w_main, mode='nt', epi='add', extra=dxn_a, name=f"{tag}_dxn")
    grads = {('gla_w_in', j): jnp.concatenate([dw_main, dw_a], axis=1), ('gla_w_gate_up', j): dw_gu,
             ('gla_b_gate', j): dbg[0], ('gla_head_norm', j): dhn.reshape(w['gla_head_norm'][j].shape),
             ('gla_w_o', j): dw_o}
    return dxn, grads


_MIXERS = ((_sb_layer_fwd, _sb_layer_bwd), (_conv_layer_fwd, _conv_layer_bwd), (_gla_layer_fwd, _gla_layer_bwd))


def local_step(x, w, target, comm=None):
    depth = len(w['norm_gains'])
    h = x
    tape = []
    for i in range(depth):
        kind, j = i % 3, i // 3
        tag = f"l{i}"
        extra = {'comm': comm} if (comm is not None and i == 0) else {}
        gains = [w['norm_gains'][i][s][None, :] for s in range(4)]
        xn = rms_fwd(h, gains[0], name=f"{tag}_mix_norm", out_dtype=BF16)
        m, saved = _MIXERS[kind][0](xn, w, j, tag, **extra)
        h_mid = rms_fwd(m, gains[1], res=h, name=f"{tag}_mix_out")
        h_out, ffn_saved = _ffn_fwd(h_mid, gains, w['ffn_w_up'][i], w['ffn_w_down'][i], tag)
        tape.append((h, xn, m, saved, ffn_saved, gains))
        h = h_out
    loss, dh = loss_head(h, target, name="loss_head")

    grads = {}
    for i in reversed(range(depth)):
        kind, j = i % 3, i // 3
        tag = f"l{i}"
        h_in, xn, m, saved, ffn_saved, gains = tape[i]
        dg = [None] * 4
        dh, dg[2], dg[3], grads[('ffn_w_up', i)], grads[('ffn_w_down', i)] = _ffn_bwd(
            dh, ffn_saved, gains, w['ffn_w_up'][i], w['ffn_w_down'][i], tag)
        dm, dg[1] = rms_bwd(m, gains[1], dh, out_dtype=BF16, name=f"{tag}_mix_out_bwd")
        extra = {'comm': comm, 'grads': grads} if (comm is not None and i == 0) else {}
        dxn, g = _MIXERS[kind][1](dm, xn, saved, w, j, tag, **extra)
        grads.update(g)
        dh, dg[0] = rms_bwd(h_in, gains[0], dxn, dres=dh, name=f"{tag}_mix_norm_bwd")
        grads[('norm_gains', i)] = jnp.concatenate(dg, axis=0)
    return loss, dh, grads


def _segments(keys, shard_shapes, f32_as_pairs):
    segs, row = {}, 0
    for name, j in keys:
        n = math.prod(shard_shapes[name][1:]) * (2 if f32_as_pairs and name in F32_PAYLOAD else 1)
        nrows = -(-n // (PACK_COLS * PACK_ROW_ALIGN)) * PACK_ROW_ALIGN
        segs[(name, j)] = (row, nrows, n)
        row += nrows
    return segs, -(-row // PACK_ROW_BLOCK) * PACK_ROW_BLOCK


def _pack(parts, segs, total_rows, dtype):
    pieces, row = [], 0
    for key in segs:
        _, nrows, n = segs[key]
        p = parts[key].astype(dtype)
        lead = p.shape[:-1]
        p = jnp.pad(p, [(0, 0)] * len(lead) + [(0, nrows * PACK_COLS - n)])
        pieces.append(p.reshape(lead + (nrows, PACK_COLS)))
        row += nrows
    if total_rows > row:
        pieces.append(jnp.zeros(lead + (total_rows - row, PACK_COLS), dtype))
    return jnp.concatenate(pieces, axis=-2)


def _unpack(buf, seg):
    first, nrows, n = seg
    piece = buf[..., first:first + nrows, :]
    return piece.reshape(piece.shape[:-2] + (nrows * PACK_COLS,))[..., :n]


def _unshard(gathered, axis):
    moved = jnp.moveaxis(gathered, 0, axis)
    shape = moved.shape
    return moved.reshape(shape[:axis] + (shape[axis] * shape[axis + 1],) + shape[axis + 2:])


def _shard_split(full, axis):
    shape = full.shape
    cut = full.reshape(shape[:axis] + (N_DEV, shape[axis] // N_DEV) + shape[axis + 1:])
    return jnp.moveaxis(cut, axis, 0)


def _mesh_position():
    return lax.axis_index("x"), lax.axis_index("y"), lax.axis_index("c")


def _comm_scratch():
    return [pltpu.SemaphoreType.DMA((N_DEV - 1,)), pltpu.SemaphoreType.DMA((N_DEV - 1,)), pltpu.SemaphoreType.DMA]


def _gather_plan(x_ref, out_ref, send_sems, recv_sems, local_sem):
    rows = x_ref.shape[0]
    x, y, c = _mesh_position()
    me, sibling = (x, y, c), (x, y, 1 - c)
    chips = [(1 - x, y), (x, 1 - y), (1 - x, 1 - y)]

    def block(px, py, pc):
        return out_ref.at[pl.ds((4 * px + 2 * py + pc) * rows, rows), :]

    def copy(k, blk, to, src=None):
        return pltpu.make_async_remote_copy(
            src_ref=block(*blk) if src is None else src, dst_ref=block(*blk),
            send_sem=send_sems.at[k], recv_sem=recv_sems.at[k],
            device_id=to, device_id_type=pl.DeviceIdType.MESH)

    mine = pltpu.make_async_copy(x_ref, block(*me), local_sem)
    first = [copy(0, me, sibling, src=x_ref)]
    first += [copy(1 + j, me, (*chip, c), src=x_ref) for j, chip in enumerate(chips)]
    passed = [copy(4 + j, (*chip, c), sibling) for j, chip in enumerate(chips)]

    def start():
        mine.start()
        for cp in first:
            cp.start()

    def forward():
        for j, chip in enumerate(chips):
            copy(1 + j, (*chip, c), me).wait_recv()
            passed[j].start()

    def finish():
        copy(0, sibling, me).wait_recv()
        for j, chip in enumerate(chips):
            copy(4 + j, (*chip, 1 - c), me).wait_recv()
        for cp in first + passed:
            cp.wait_send()
        mine.wait()

    return start, forward, finish


def _exchange_plan(in_ref, out_ref, send_sems, recv_sems, local_sem):
    x, y, c = _mesh_position()
    my_id = 4 * x + 2 * y + c
    mine = pltpu.make_async_copy(in_ref.at[my_id], out_ref.at[my_id], local_sem)

    def copy(k, receive):
        px = 1 - x if k & 4 else x
        py = 1 - y if k & 2 else y
        pc = 1 - c if k & 1 else c
        peer_id = 4 * px + 2 * py + pc
        return pltpu.make_async_remote_copy(
            src_ref=in_ref.at[peer_id], dst_ref=out_ref.at[peer_id if receive else my_id],
            send_sem=send_sems.at[k - 1], recv_sem=recv_sems.at[k - 1],
            device_id=(px, py, pc), device_id_type=pl.DeviceIdType.MESH)

    def start():
        mine.start()
        for k in range(1, N_DEV):
            copy(k, False).start()

    def finish():
        for k in range(1, N_DEV):
            copy(k, True).wait_recv()
        for k in range(1, N_DEV):
            copy(k, False).wait_send()
        mine.wait()

    return start, finish


def all_gather(shard, *, name):
    rows, cols = shard.shape

    def body(x_ref, out_ref, send_sems, recv_sems, local_sem):
        start, forward, finish = _gather_plan(x_ref, out_ref, send_sems, recv_sems, local_sem)
        start()
        forward()
        finish()

    return pl.pallas_call(
        body, name=name, out_shape=jax.ShapeDtypeStruct((N_DEV * rows, cols), shard.dtype),
        in_specs=[pl.BlockSpec(memory_space=pl.ANY)], out_specs=pl.BlockSpec(memory_space=pl.ANY),
        scratch_shapes=_comm_scratch(),
    )(shard)


def exchange_shards(parts, *, name):
    def body(in_ref, out_ref, send_sems, recv_sems, local_sem):
        start, finish = _exchange_plan(in_ref, out_ref, send_sems, recv_sems, local_sem)
        start()
        finish()

    return pl.pallas_call(
        body, name=name, out_shape=jax.ShapeDtypeStruct(parts.shape, parts.dtype),
        in_specs=[pl.BlockSpec(memory_space=pl.ANY)], out_specs=pl.BlockSpec(memory_space=pl.ANY),
        scratch_shapes=_comm_scratch(),
    )(parts)


def adamw(parts, w, m, v, *, name):
    _, rows, cols = parts.shape
    tr = PACK_ROW_BLOCK
    c1 = 1.0 - ADAM_B1 ** ADAM_STEP
    c2 = 1.0 - ADAM_B2 ** ADAM_STEP

    def body(p_ref, w_ref, m_ref, v_ref, g_ref, d_ref, nm_ref, nv_ref):
        g = p_ref[0].astype(F32)
        for s in range(1, N_DEV):
            g = g + p_ref[s].astype(F32)
        nm = ADAM_B1 * m_ref[...] + (1.0 - ADAM_B1) * g
        nv = ADAM_B2 * v_ref[...] + (1.0 - ADAM_B2) * jnp.square(g)
        m_hat = nm / c1
        v_hat = nv / c2
        g_ref[...] = g
        d_ref[...] = -ADAM_LR * (m_hat / (jnp.sqrt(v_hat) + ADAM_EPS) + ADAM_WD * w_ref[...])
        nm_ref[...] = nm
        nv_ref[...] = nv

    row = pl.BlockSpec((tr, cols), lambda i: (i, 0))
    shape = jax.ShapeDtypeStruct((rows, cols), F32)
    return pl.pallas_call(
        body, name=name, grid=(rows // tr,),
        in_specs=[pl.BlockSpec((N_DEV, tr, cols), lambda i: (0, i, 0)), row, row, row],
        out_specs=[row, row, row, row], out_shape=[shape, shape, shape, shape],
        compiler_params=_params("parallel"),
    )(parts, w, m, v)


def kernel(x, norm_gains, sb_w_qkv, sb_w_o, conv_w_in, conv_w, conv_w_out, gla_w_in, gla_w_gate_up, gla_b_gate, gla_head_norm, gla_w_o, ffn_w_up, ffn_w_down, loss_target, m_norm_gains, m_sb_w_qkv, m_sb_w_o, m_conv_w_in, m_conv_w, m_conv_w_out, m_gla_w_in, m_gla_w_gate_up, m_gla_b_gate, m_gla_head_norm, m_gla_w_o, m_ffn_w_up, m_ffn_w_down, v_norm_gains, v_sb_w_qkv, v_sb_w_o, v_conv_w_in, v_conv_w, v_conv_w_out, v_gla_w_in, v_gla_w_gate_up, v_gla_b_gate, v_gla_head_norm, v_gla_w_o, v_ffn_w_up, v_ffn_w_down):
    shards = dict(zip(WEIGHTS, (norm_gains, sb_w_qkv, sb_w_o, conv_w_in, conv_w, conv_w_out, gla_w_in,
                                gla_w_gate_up, gla_b_gate, gla_head_norm, gla_w_o, ffn_w_up, ffn_w_down)))
    moments_m = dict(zip(WEIGHTS, (m_norm_gains, m_sb_w_qkv, m_sb_w_o, m_conv_w_in, m_conv_w, m_conv_w_out,
                                   m_gla_w_in, m_gla_w_gate_up, m_gla_b_gate, m_gla_head_norm, m_gla_w_o,
                                   m_ffn_w_up, m_ffn_w_down)))
    moments_v = dict(zip(WEIGHTS, (v_norm_gains, v_sb_w_qkv, v_sb_w_o, v_conv_w_in, v_conv_w, v_conv_w_out,
                                   v_gla_w_in, v_gla_w_gate_up, v_gla_b_gate, v_gla_head_norm, v_gla_w_o,
                                   v_ffn_w_up, v_ffn_w_down)))
    shard_shapes = {n: a.shape for n, a in shards.items()}

    keys = [(n, j) for n in WEIGHTS for j in range(shard_shapes[n][0])]
    alone = [('norm_gains', 0), ('sb_w_qkv', 0)]
    groups = [alone, [k for k in keys if k not in alone]]

    def payload(group):
        segs, rows = _segments(group, shard_shapes, True)
        flat = {(n, j): (lax.bitcast_convert_type(shards[n][j], BF16) if n in F32_PAYLOAD
                         else shards[n][j].astype(BF16)).reshape(-1) for n, j in group}
        return segs, _pack(flat, segs, rows, BF16)

    whole = {n: [None] * shard_shapes[n][0] for n in WEIGHTS}

    def take_gathered(segs, gathered):
        gathered = gathered.reshape(N_DEV, -1, PACK_COLS)
        for n, j in segs:
            piece = _unpack(gathered, segs[(n, j)])
            if n in F32_PAYLOAD:
                piece = lax.bitcast_convert_type(piece.reshape(N_DEV, -1, 2), F32)
            whole[n][j] = _unshard(piece.reshape((N_DEV,) + shard_shapes[n][1:]), SHARD_AXIS[n] - 1)

    segs0, payload0 = payload(groups[0])
    segs1, payload1 = payload(groups[1])
    take_gathered(segs0, all_gather(payload0, name="weights_all_gather"))

    gsegs = [_segments(group, shard_shapes, False) for group in groups]
    received = [None, None]

    def parts(g, grads):
        segs, rows = gsegs[g]
        return _pack({(n, j): _shard_split(grads[(n, j)], SHARD_AXIS[n] - 1).reshape(N_DEV, -1) for n, j in segs},
                     segs, rows, BF16)

    def on_received(buf):
        received[1] = buf

    comm = types.SimpleNamespace(rest_payload=payload1, on_gathered=functools.partial(take_gathered, segs1),
                                 rest_parts=functools.partial(parts, 1), on_received=on_received)

    loss, grad_x, grads = local_step(x[0], whole, loss_target[0], comm)
    loss = lax.psum(loss[0, 0], ("x", "y", "c"))
    received[0] = exchange_shards(parts(0, grads), name="grads_exchange")

    results = {}
    for g in range(2):
        segs, rows = gsegs[g]

        def packed(source):
            return _pack({(n, j): source[n][j].reshape(-1) for n, j in segs}, segs, rows, F32)

        outs = adamw(received[g], packed(shards), packed(moments_m), packed(moments_v), name=f"adamw{g}")
        for n, j in segs:
            results[(n, j)] = [_unpack(buf, segs[(n, j)]).reshape(shard_shapes[n][1:]) for buf in outs]
    stacked = [[jnp.stack([results[(n, j)][o] for j in range(shard_shapes[n][0])]) for n in WEIGHTS] for o in range(4)]
    return (loss, grad_x[None], *stacked[0], *stacked[1], *stacked[2], *stacked[3])
```

```python
import functools
import math
import types

import jax
import jax.numpy as jnp
from jax import lax
from jax.experimental import pallas as pl
from jax.experimental.pallas import tpu as pltpu

F32 = jnp.float32
BF16 = jnp.bfloat16

N_DEV = 8
SB_HEADS = 16
GLA_HEADS = 4
GLA_CHUNK = 64
GLA_SUB = 16
GLA_GATE_RANK = 16
GLA_GATE_NORMALIZER = 16.0
CONV_WIDTH = 3
DEPTH = 4
RMS_EPS = 1e-6
ADAM_LR = 0.001
ADAM_B1 = 0.9
ADAM_B2 = 0.999
ADAM_EPS = 1e-08
ADAM_WD = 0.01
ADAM_STEP = 10

LANES = 128
SB_BLOCK = 256
SB_FWD_Q_BLOCK = 512
SB_BWD_Q_BLOCK = 512
VMEM_LIMIT_BYTES = 56 * 1024 * 1024
MM_TM, MM_TN, MM_TK = 1024, 1024, 1024
PACK_COLS = 1024
PACK_ROW_ALIGN = 16
PACK_ROW_BLOCK = 128

WEIGHTS = ['norm_gains', 'sb_w_qkv', 'sb_w_o', 'conv_w_in', 'conv_w', 'conv_w_out', 'gla_w_in',
           'gla_w_gate_up', 'gla_b_gate', 'gla_head_norm', 'gla_w_o', 'ffn_w_up', 'ffn_w_down']
SHARD_AXIS = {'norm_gains': 2, 'sb_w_qkv': 2, 'sb_w_o': 1, 'conv_w_in': 2, 'conv_w': 2, 'conv_w_out': 1,
              'gla_w_in': 2, 'gla_w_gate_up': 2, 'gla_b_gate': 1, 'gla_head_norm': 2, 'gla_w_o': 1,
              'ffn_w_up': 2, 'ffn_w_down': 1}
F32_PAYLOAD = ('norm_gains', 'conv_w', 'gla_b_gate', 'gla_head_norm')

_NN = (((1,), (0,)), ((), ()))
_NT = (((1,), (1,)), ((), ()))
_TN = (((0,), (0,)), ((), ()))
_DIMS = {'nn': _NN, 'nt': _NT, 'tn': _TN}


def _params(*semantics):
    return pltpu.CompilerParams(dimension_semantics=semantics, vmem_limit_bytes=VMEM_LIMIT_BYTES)


def _dot(a, b, dims=_NN):
    return lax.dot_general(a.astype(BF16), b.astype(BF16), dims, preferred_element_type=F32)


def _split_hi_lo(x):
    hi = x.astype(BF16)
    lo = (x - hi.astype(F32)).astype(BF16)
    return hi, lo


def _dot_exact_rhs(x, ones_mat, dims=_NN):
    hi, lo = _split_hi_lo(x)
    return (lax.dot_general(hi, ones_mat, dims, preferred_element_type=F32)
            + lax.dot_general(lo, ones_mat, dims, preferred_element_type=F32))


def _dot_exact_lhs(ones_mat, x, dims=_NN):
    hi, lo = _split_hi_lo(x)
    return (lax.dot_general(ones_mat, hi, dims, preferred_element_type=F32)
            + lax.dot_general(ones_mat, lo, dims, preferred_element_type=F32))


def _log_sigmoid(z):
    return jnp.minimum(z, 0.0) - jnp.log(1.0 + jnp.exp(-jnp.abs(z)))


def _sigmoid(z):
    return 1.0 / (1.0 + jnp.exp(-z))


def matmul(a, b, *, mode, name, out_dtype=F32, epi=None, extra=None, tm=MM_TM, tn=MM_TN, tk=MM_TK):
    if mode == 'nn':
        (m, k), (k2, n) = a.shape, b.shape
    elif mode == 'nt':
        (m, k), (n, k2) = a.shape, b.shape
    else:
        (k, m), (k2, n) = a.shape, b.shape
    assert k == k2, (a.shape, b.shape, mode)
    tm, tn, tk = min(tm, m), min(tn, n), min(tk, k)
    assert m % tm == 0 and n % tn == 0 and k % tk == 0, (a.shape, b.shape, mode)
    nk = k // tk
    if mode == 'tn':
        a_spec = pl.BlockSpec((tk, tm), lambda i, j, kk: (kk, i))
    else:
        a_spec = pl.BlockSpec((tm, tk), lambda i, j, kk: (i, kk))
    if mode == 'nt':
        b_spec = pl.BlockSpec((tn, tk), lambda i, j, kk: (j, kk))
    else:
        b_spec = pl.BlockSpec((tk, tn), lambda i, j, kk: (kk, j))
    in_specs, operands = [a_spec, b_spec], [a, b]
    if epi == 'logsig16':
        in_specs.append(pl.BlockSpec((1, tn), lambda i, j, kk: (0, j)))
        operands.append(extra)
    elif epi in ('mul2sqrt', 'add'):
        in_specs.append(pl.BlockSpec((tm, tn), lambda i, j, kk: (i, j)))
        operands.append(extra)

    n_extra = len(operands) - 2

    def body(a_ref, b_ref, *rest):
        e_ref = rest[0] if n_extra else None
        o_ref = rest[n_extra]

        def finish(r):
            if epi == 'relu2':
                r = jnp.square(jnp.maximum(r, 0.0))
            elif epi == 'mul2sqrt':
                r = r * (2.0 * jnp.sqrt(e_ref[...].astype(F32)))
            elif epi == 'add':
                r = r + e_ref[...]
            elif epi == 'logsig16':
                r = _log_sigmoid(r + e_ref[...]) / GLA_GATE_NORMALIZER
            o_ref[...] = r.astype(o_ref.dtype)

        part = _dot(a_ref[...], b_ref[...], _DIMS[mode])
        if nk == 1:
            finish(part)
        else:
            acc_ref = rest[-1]
            kk = pl.program_id(2)

            @pl.when(kk == 0)
            def _():
                acc_ref[...] = part

            @pl.when(kk > 0)
            def _():
                acc_ref[...] += part

            @pl.when(kk == nk - 1)
            def _():
                finish(acc_ref[...])

    return pl.pallas_call(
        body, name=name, grid=(m // tm, n // tn, nk), in_specs=in_specs,
        out_specs=pl.BlockSpec((tm, tn), lambda i, j, kk: (i, j)),
        out_shape=jax.ShapeDtypeStruct((m, n), out_dtype),
        scratch_shapes=[pltpu.VMEM((tm, tn), F32)] if nk > 1 else [],
        compiler_params=_params("parallel", "parallel", "arbitrary"),
    )(*operands)


def rms_fwd(x, gain, *, name, res=None, out_dtype=F32, tm=512):
    t, d = x.shape
    tm = min(tm, t)
    row = pl.BlockSpec((tm, d), lambda i: (i, 0))
    in_specs, operands = [row, pl.BlockSpec((1, d), lambda i: (0, 0))], [x, gain]
    if res is not None:
        in_specs.append(row)
        operands.append(res)

    def body(x_ref, g_ref, *rest):
        xv = x_ref[...]
        r = lax.rsqrt(jnp.mean(xv * xv, axis=-1, keepdims=True) + RMS_EPS)
        y = xv * r * g_ref[...]
        if res is not None:
            y = rest[0][...] + y
        rest[-1][...] = y.astype(out_dtype)

    return pl.pallas_call(
        body, name=name, grid=(t // tm,), in_specs=in_specs, out_specs=row,
        out_shape=jax.ShapeDtypeStruct((t, d), out_dtype), compiler_params=_params("parallel"),
    )(*operands)


def rms_bwd(x, gain, dy, *, name, dres=None, out_dtype=F32, tm=512):
    t, d = x.shape
    tm = min(tm, t)
    row = pl.BlockSpec((tm, d), lambda i: (i, 0))
    vec = pl.BlockSpec((1, d), lambda i: (0, 0))
    in_specs, operands = [row, vec, row], [x, gain, dy]
    if dres is not None:
        in_specs.append(row)
        operands.append(dres)

    def body(x_ref, g_ref, dy_ref, *rest):
        dx_ref, dg_ref = rest[-2], rest[-1]

        @pl.when(pl.program_id(0) == 0)
        def _():
            dg_ref[...] = jnp.zeros_like(dg_ref)

        xv, dyv = x_ref[...], dy_ref[...]
        r = lax.rsqrt(jnp.mean(xv * xv, axis=-1, keepdims=True) + RMS_EPS)
        u = dyv * g_ref[...]
        dx = r * u - xv * (r * r * r * jnp.mean(u * xv, axis=-1, keepdims=True))
        if dres is not None:
            dx = rest[0][...] + dx
        dx_ref[...] = dx.astype(out_dtype)
        dg_ref[...] += jnp.sum(dyv * xv * r, axis=0, keepdims=True)

    return pl.pallas_call(
        body, name=name, grid=(t // tm,), in_specs=in_specs, out_specs=[row, vec],
        out_shape=[jax.ShapeDtypeStruct((t, d), out_dtype), jax.ShapeDtypeStruct((1, d), F32)],
        compiler_params=_params("arbitrary"),
    )(*operands)


def loss_head(y, target, *, name, tm=512):
    t, d = y.shape
    tm = min(tm, t)
    nt = t // tm
    row = pl.BlockSpec((tm, d), lambda i: (i, 0))

    def body(y_ref, t_ref, loss_ref, dy_ref, acc_ref):
        i = pl.program_id(0)

        @pl.when(i == 0)
        def _():
            acc_ref[...] = jnp.zeros_like(acc_ref)

        err = y_ref[...] - t_ref[...]
        dy_ref[...] = err * (1.0 / d)
        acc_ref[...] += jnp.sum(err * err, axis=0, keepdims=True)

        @pl.when(i == nt - 1)
        def _():
            loss_ref[...] = jnp.sum(acc_ref[...], axis=1, keepdims=True) * (0.5 / d)

    return pl.pallas_call(
        body, name=name, grid=(nt,), in_specs=[row, row],
        out_specs=[pl.BlockSpec((1, 1), lambda i: (0, 0)), row],
        out_shape=[jax.ShapeDtypeStruct((1, 1), F32), jax.ShapeDtypeStruct((t, d), F32)],
        scratch_shapes=[pltpu.VMEM((1, d), F32)], compiler_params=_params("arbitrary"),
    )(y, target)


def _sb_block_iota():
    rows = lax.broadcasted_iota(jnp.int32, (SB_BLOCK, SB_BLOCK), 0)
    cols = lax.broadcasted_iota(jnp.int32, (SB_BLOCK, SB_BLOCK), 1)
    return rows, cols


def _sb_logits(q_h, k_blk, mask):
    z = _dot(q_h, k_blk, _NT)
    ls = _log_sigmoid(z)
    lm = ls - z
    if mask is not None:
        lm = jnp.where(mask, lm, 0.0)
    return ls, lm, jnp.sum(lm, axis=1, keepdims=True)


def _sb_weights(ls, lm, mask, tri_strict, later):
    suffix = _dot_exact_rhs(lm, tri_strict)
    w = jnp.exp(ls + suffix + later)
    return w if mask is None else jnp.where(mask, w, 0.0)


def _on_grid_step(p, i):
    return jnp.logical_and(pl.program_id(0) == p, pl.program_id(1) == i)


def sb_fwd(qkv, *, name, gather=None):
    t, d3 = qkv.shape
    d = d3 // 3
    head_dim = d // SB_HEADS
    qb, kb_rows = SB_FWD_Q_BLOCK, SB_BLOCK
    assert 2 * head_dim == LANES and t % qb == 0
    pairs = d // LANES
    per_q = qb // kb_rows
    nq = t // qb
    scale = head_dim ** -0.5

    def body(q_ref, k_ref, v_ref, *rest):
        if gather is None:
            compute(q_ref, k_ref, v_ref, *rest)
            return
        x_ref, o_ref, tot_ref, out_ref, send_sems, recv_sems, local_sem = rest
        start, forward, finish = _gather_plan(x_ref, out_ref, send_sems, recv_sems, local_sem)
        pl.when(_on_grid_step(0, 0))(start)
        pl.when(_on_grid_step(3 * pairs // 4, 0))(forward)
        compute(q_ref, k_ref, v_ref, o_ref, tot_ref)
        pl.when(_on_grid_step(pairs - 1, nq - 1))(finish)

    def compute(q_ref, k_ref, v_ref, o_ref, tot_ref):
        qi = pl.program_id(1)
        lane = lax.broadcasted_iota(jnp.int32, (qb, LANES), 1)
        first = lane < head_dim
        q = q_ref[...] * scale
        q2 = jnp.concatenate([jnp.where(first, q, jnp.zeros_like(q)), jnp.where(first, jnp.zeros_like(q), q)], axis=0)
        rows, cols = _sb_block_iota()
        tri = jnp.where(rows > cols, 1.0, 0.0).astype(BF16)
        q_row = lax.broadcasted_iota(jnp.int32, (qb, kb_rows), 0)
        k_col = lax.broadcasted_iota(jnp.int32, (qb, kb_rows), 1)
        diagonal = [jnp.concatenate([m * kb_rows + k_col < q_row] * 2, axis=0) for m in range(per_q)]

        def step(kb, carry, mask):
            ks = pl.multiple_of(kb * SB_BLOCK, SB_BLOCK)
            k_blk = k_ref[pl.ds(ks, SB_BLOCK), :]
            v_blk = v_ref[pl.ds(ks, SB_BLOCK), :]
            acc, later = carry
            ls, lm, row = _sb_logits(q2, k_blk, mask)
            w = _sb_weights(ls, lm, mask, tri, later)
            return acc + _dot(w, v_blk), later + row

        out = (jnp.zeros((2 * qb, LANES), F32), jnp.zeros((2 * qb, 1), F32))
        for m in reversed(range(per_q)):
            out = step(per_q * qi + m, out, diagonal[m])
        acc, total = lax.fori_loop(0, per_q * qi, lambda i, carry: step(per_q * qi - 1 - i, carry, None), out)
        o_ref[...] = jnp.where(first, acc[:qb], acc[qb:]).astype(o_ref.dtype)
        tot_ref[...] = jnp.where(first, total[:qb], total[qb:])

    blk = lambda off: pl.BlockSpec((t, LANES), lambda p, i: (0, off + p))
    qblk = pl.BlockSpec((qb, LANES), lambda p, i: (i, p))
    in_specs, operands = [qblk, blk(pairs), blk(2 * pairs)], [qkv, qkv, qkv]
    out_specs = [qblk, qblk]
    out_shape = [jax.ShapeDtypeStruct((t, d), BF16), jax.ShapeDtypeStruct((t, d), F32)]
    if gather is None:
        return pl.pallas_call(
            body, name=name, grid=(pairs, nq), in_specs=in_specs, out_specs=out_specs, out_shape=out_shape,
            compiler_params=_params("parallel", "arbitrary"),
        )(*operands)
    whole = pl.BlockSpec(memory_space=pl.ANY)
    return pl.pallas_call(
        body, name=name, grid=(pairs, nq), in_specs=in_specs + [whole], out_specs=out_specs + [whole],
        out_shape=out_shape + [jax.ShapeDtypeStruct((N_DEV * gather.shape[0], gather.shape[1]), gather.dtype)],
        scratch_shapes=_comm_scratch(), compiler_params=_params("arbitrary", "arbitrary"),
    )(*operands, gather)


def sb_bwd(qkv, tot, do, *, name, exchange=None):
    t, d3 = qkv.shape
    d = d3 // 3
    head_dim = d // SB_HEADS
    pairs = d // LANES
    qb, kb_rows = SB_BWD_Q_BLOCK, SB_BLOCK
    per_q = qb // kb_rows
    nq = t // qb
    scale = head_dim ** -0.5

    def body(*refs):
        if exchange is None:
            compute(*refs)
            return
        q_ref, k_ref, v_ref, tot_ref, do_ref, in_ref, dq_ref, dk_ref, dv_ref, out_ref = refs[:10]
        dk_sum, dv_sum, send_sems, recv_sems, local_sem = refs[10:]
        start, finish = _exchange_plan(in_ref, out_ref, send_sems, recv_sems, local_sem)
        pl.when(_on_grid_step(0, 0))(start)
        compute(q_ref, k_ref, v_ref, tot_ref, do_ref, dq_ref, dk_ref, dv_ref, dk_sum, dv_sum)
        pl.when(_on_grid_step(pairs - 1, nq - 1))(finish)

    def compute(q_ref, k_ref, v_ref, tot_ref, do_ref, dq_ref, dk_ref, dv_ref, dk_sum, dv_sum):
        qi = pl.program_id(1)

        @pl.when(qi == 0)
        def _():
            dk_sum[...] = jnp.zeros_like(dk_sum)
            dv_sum[...] = jnp.zeros_like(dv_sum)

        lane = lax.broadcasted_iota(jnp.int32, (qb, LANES), 1)
        first = lane < head_dim
        q, dov, totv = q_ref[...] * scale, do_ref[...], tot_ref[...]
        second = jnp.logical_not(first)
        q2 = jnp.concatenate([jnp.where(s, q, jnp.zeros_like(q)) for s in (first, second)], axis=0)
        do2 = jnp.concatenate([jnp.where(s, dov, jnp.zeros_like(dov)) for s in (first, second)], axis=0)
        tot2 = jnp.concatenate([totv[:, 0:1], totv[:, head_dim:head_dim + 1]], axis=0)
        rows, cols = _sb_block_iota()
        tri_strict = jnp.where(rows > cols, 1.0, 0.0).astype(BF16)
        tri_before = jnp.where(rows < cols, 1.0, 0.0).astype(BF16)
        q_row = lax.broadcasted_iota(jnp.int32, (qb, kb_rows), 0)
        k_col = lax.broadcasted_iota(jnp.int32, (qb, kb_rows), 1)
        diagonal = [jnp.concatenate([m * kb_rows + k_col < q_row] * 2, axis=0) for m in range(per_q)]

        def step(kb, carry, mask):
            ks = pl.multiple_of(kb * SB_BLOCK, SB_BLOCK)
            k_blk = k_ref[pl.ds(ks, SB_BLOCK), :]
            v_blk = v_ref[pl.ds(ks, SB_BLOCK), :]
            dq, seen, before = carry
            ls, lm, row = _sb_logits(q2, k_blk, mask)
            seen = seen + row
            w = _sb_weights(ls, lm, mask, tri_strict, tot2 - seen)
            da = _dot(do2, v_blk, _NT) * w
            g = _dot_exact_rhs(da, tri_before) + before
            dz = da - jnp.exp(ls) * (da + g)
            if mask is not None:
                dz = jnp.where(mask, dz, 0.0)
            dk_sum[pl.ds(ks, SB_BLOCK), :] += _dot(dz, q2, _TN)
            dv_sum[pl.ds(ks, SB_BLOCK), :] += _dot(w, do2, _TN)
            return dq + _dot(dz, k_blk * scale), seen, before + jnp.sum(da, axis=1, keepdims=True)

        zero = jnp.zeros((2 * qb, LANES), F32)
        zcol = jnp.zeros((2 * qb, 1), F32)
        out = lax.fori_loop(0, per_q * qi, lambda kb, carry: step(kb, carry, None), (zero, zcol, zcol))
        for m in range(per_q):
            out = step(per_q * qi + m, out, diagonal[m])
        dq = out[0]
        dq_ref[...] = jnp.where(first, dq[:qb], dq[qb:]).astype(dq_ref.dtype)

        @pl.when(qi == nq - 1)
        def _():
            dk_ref[...] = dk_sum[...].astype(dk_ref.dtype)
            dv_ref[...] = dv_sum[...].astype(dv_ref.dtype)

    qblk = pl.BlockSpec((qb, LANES), lambda p, i: (i, p))
    col = lambda off: pl.BlockSpec((t, LANES), lambda p, i: (0, off + p))
    shape = jax.ShapeDtypeStruct((t, d), BF16)
    in_specs, operands = [qblk, col(pairs), col(2 * pairs), qblk, qblk], [qkv, qkv, qkv, tot, do]
    out_specs, out_shape = [qblk, col(0), col(0)], [shape, shape, shape]
    sums = [pltpu.VMEM((t, LANES), F32), pltpu.VMEM((t, LANES), F32)]
    if exchange is None:
        return pl.pallas_call(
            body, name=name, grid=(pairs, nq), in_specs=in_specs, out_specs=out_specs, out_shape=out_shape,
            scratch_shapes=sums, compiler_params=_params("parallel", "arbitrary"),
        )(*operands)
    whole = pl.BlockSpec(memory_space=pl.ANY)
    return pl.pallas_call(
        body, name=name, grid=(pairs, nq), in_specs=in_specs + [whole], out_specs=out_specs + [whole],
        out_shape=out_shape + [jax.ShapeDtypeStruct(exchange.shape, exchange.dtype)],
        scratch_shapes=sums + _comm_scratch(), compiler_params=_params("arbitrary", "arbitrary"),
    )(*operands, exchange)


def _shift_down(x, s):
    rows = lax.broadcasted_iota(jnp.int32, x.shape, 0)
    return jnp.where(rows >= s, pltpu.roll(x, s, 0), 0.0)


def _shift_up(x, s):
    t = x.shape[0]
    rows = lax.broadcasted_iota(jnp.int32, x.shape, 0)
    return jnp.where(rows < t - s, pltpu.roll(x, t - s, 0), 0.0)


def conv_fwd(bcu, w, *, name):
    t, d3 = bcu.shape
    d = d3 // 3
    nb = d // LANES
    col = lambda off: pl.BlockSpec((t, LANES), lambda j: (0, off + j))

    def body(b_ref, c_ref, u_ref, w_ref, y_ref):
        hh = c_ref[...] * u_ref[...]
        conv = w_ref[0:1, :] * _shift_down(hh, 2) + w_ref[1:2, :] * _shift_down(hh, 1) + w_ref[2:3, :] * hh
        y_ref[...] = (b_ref[...] * conv).astype(y_ref.dtype)

    return pl.pallas_call(
        body, name=name, grid=(nb,),
        in_specs=[col(0), col(nb), col(2 * nb), pl.BlockSpec((CONV_WIDTH, LANES), lambda j: (0, j))],
        out_specs=col(0), out_shape=jax.ShapeDtypeStruct((t, d), BF16), compiler_params=_params("parallel"),
    )(bcu, bcu, bcu, w)


def conv_bwd(bcu, w, dy, *, name):
    t, d3 = bcu.shape
    d = d3 // 3
    nb = d // LANES
    col = lambda off: pl.BlockSpec((t, LANES), lambda j: (0, off + j))
    wspec = pl.BlockSpec((CONV_WIDTH, LANES), lambda j: (0, j))

    def body(b_ref, c_ref, u_ref, w_ref, dy_ref, db_ref, dc_ref, du_ref, dw_ref):
        c, u, dyv = c_ref[...], u_ref[...], dy_ref[...]
        hh = c * u
        h2, h1 = _shift_down(hh, 2), _shift_down(hh, 1)
        w0, w1, w2 = w_ref[0:1, :], w_ref[1:2, :], w_ref[2:3, :]
        db_ref[...] = (dyv * (w0 * h2 + w1 * h1 + w2 * hh)).astype(db_ref.dtype)
        dconv = dyv * b_ref[...]
        dhh = w2 * dconv + w1 * _shift_up(dconv, 1) + w0 * _shift_up(dconv, 2)
        dc_ref[...] = (dhh * u).astype(dc_ref.dtype)
        du_ref[...] = (dhh * c).astype(du_ref.dtype)
        dw_ref[0:1, :] = jnp.sum(dconv * h2, axis=0, keepdims=True)
        dw_ref[1:2, :] = jnp.sum(dconv * h1, axis=0, keepdims=True)
        dw_ref[2:3, :] = jnp.sum(dconv * hh, axis=0, keepdims=True)

    shape = jax.ShapeDtypeStruct((t, d), BF16)
    return pl.pallas_call(
        body, name=name, grid=(nb,),
        in_specs=[col(0), col(nb), col(2 * nb), wspec, col(0)],
        out_specs=[col(0), col(0), col(0), wspec],
        out_shape=[shape, shape, shape, jax.ShapeDtypeStruct((CONV_WIDTH, d), F32)],
        compiler_params=_params("parallel"),
    )(bcu, bcu, bcu, w, dy)


def _gla_chunk(q_ref, k_ref, lg_ref, scale):
    c = GLA_CHUNK
    rows = lax.broadcasted_iota(jnp.int32, (c, c), 0)
    cols = lax.broadcasted_iota(jnp.int32, (c, c), 1)
    causal = rows >= cols
    tril = jnp.where(causal, 1.0, 0.0).astype(BF16)
    q = q_ref[...] * scale
    k = k_ref[...]
    lg = lg_ref[...]
    cum = _dot_exact_lhs(tril, lg)
    last = cum[c - 1:c, :]
    eq = jnp.exp(cum)
    el = jnp.exp(last - cum)
    return causal, tril, q, k, cum, lg, last, eq, el


def _gla_sub_blocks(q, k, cum, lg):
    key_row = lax.broadcasted_iota(jnp.int32, (GLA_CHUNK, 1), 0)
    out = []
    for lo in range(0, GLA_CHUNK, GLA_SUB):
        hi = lo + GLA_SUB
        ref = cum[lo:lo + 1, :] - lg[lo:lo + 1, :]
        eq = jnp.exp(cum[lo:hi] - ref)
        ek = jnp.where(key_row < hi, jnp.exp(ref - cum), 0.0)
        out.append((slice(lo, hi), eq, ek, q[lo:hi] * eq, k * ek))
    return out


def _gla_scores(subs, causal):
    return jnp.where(causal, jnp.concatenate([_dot(qt, kt, _NT) for _, _, _, qt, kt in subs], axis=0), 0.0)


def gla_fwd(proj, lg, *, name):
    t, d3 = proj.shape
    d = d3 // 3
    dk, dv = d // 2 // GLA_HEADS, d // GLA_HEADS
    assert dk == LANES and dv == 2 * LANES
    c = GLA_CHUNK
    nc = t // c
    scale = dk ** -0.5
    nh = GLA_HEADS

    def body(q_ref, k_ref, v_ref, lg_ref, o_ref, st_out_ref, st_ref):
        @pl.when(pl.program_id(1) == 0)
        def _():
            st_ref[...] = jnp.zeros_like(st_ref)

        causal, _, q, k, cum, lg, last, eq, el = _gla_chunk(q_ref, k_ref, lg_ref, scale)
        v = v_ref[...]
        st = st_ref[...]
        st_out_ref[...] = st
        scores = _gla_scores(_gla_sub_blocks(q, k, cum, lg), causal)
        o_ref[...] = _dot(q * eq, st, _NT) + _dot(scores, v)
        st_ref[...] = st * jnp.exp(last) + _dot(v, k * el, _TN)

    return pl.pallas_call(
        body, name=name, grid=(nh, nc),
        in_specs=[pl.BlockSpec((c, dk), lambda h, i: (i, h)), pl.BlockSpec((c, dk), lambda h, i: (i, nh + h)),
                  pl.BlockSpec((c, dv), lambda h, i: (i, nh + h)), pl.BlockSpec((c, dk), lambda h, i: (i, h))],
        out_specs=[pl.BlockSpec((c, dv), lambda h, i: (i, h)),
                   pl.BlockSpec((None, None, dv, dk), lambda h, i: (h, i, 0, 0))],
        out_shape=[jax.ShapeDtypeStruct((t, d), F32), jax.ShapeDtypeStruct((nh, nc, dv, dk), F32)],
        scratch_shapes=[pltpu.VMEM((dv, dk), F32)], compiler_params=_params("parallel", "arbitrary"),
    )(proj, proj, proj, lg)


def gla_bwd(proj, lg, states, do, *, name):
    t, d3 = proj.shape
    d = d3 // 3
    dk, dv = d // 2 // GLA_HEADS, d // GLA_HEADS
    c = GLA_CHUNK
    nc = t // c
    scale = dk ** -0.5
    nh = GLA_HEADS

    def body(q_ref, k_ref, v_ref, lg_ref, st_ref, do_ref, dq_ref, dk_ref, dv_ref, dzg_ref, dbg_ref, dst_ref):
        @pl.when(pl.program_id(1) == 0)
        def _():
            dst_ref[...] = jnp.zeros_like(dst_ref)
            dbg_ref[...] = jnp.zeros_like(dbg_ref)

        causal, tril, q, k, cum, lg, last, eq, el = _gla_chunk(q_ref, k_ref, lg_ref, scale)
        v, st, dov, dst = v_ref[...], st_ref[...], do_ref[...], dst_ref[...]
        subs = _gla_sub_blocks(q, k, cum, lg)
        qt, kh = q * eq, k * el
        scores = _gla_scores(subs, causal)
        dscores = jnp.where(causal, _dot(dov, v, _NT), 0.0)
        dq_parts = []
        dkh = _dot(v, dst)
        dk = dkh * el
        for rows, eq_sub, ek_sub, qt_sub, kt_sub in subs:
            dq_parts.append(_dot(dscores[rows], kt_sub) * eq_sub)
            dk = dk + _dot(dscores[rows], qt_sub, _TN) * ek_sub
        dq = _dot(dov, st) * eq + jnp.concatenate(dq_parts, axis=0)
        dv_ref[...] = (_dot(scores, dov, _TN) + _dot(kh, dst, _NT)).astype(dv_ref.dtype)
        dq_ref[...] = (dq * scale).astype(dq_ref.dtype)
        dk_ref[...] = dk.astype(dk_ref.dtype)
        e_last = jnp.exp(last)
        dlast = jnp.sum(kh * dkh, axis=0, keepdims=True) + e_last * jnp.sum(dst * st, axis=0, keepdims=True)
        dcum = q * dq - k * dk
        dlg = _dot_exact_lhs(tril, dcum, _TN) + dlast
        dzg = dlg * (1.0 - jnp.exp(lg * GLA_GATE_NORMALIZER)) / GLA_GATE_NORMALIZER
        dzg_ref[...] = dzg.astype(dzg_ref.dtype)
        dbg_ref[...] += jnp.sum(dzg, axis=0, keepdims=True)
        dst_ref[...] = dst * e_last + _dot(dov, qt, _TN)

    rev = lambda i: nc - 1 - i
    half = jax.ShapeDtypeStruct((t, d // 2), BF16)
    return pl.pallas_call(
        body, name=name, grid=(nh, nc),
        in_specs=[pl.BlockSpec((c, dk), lambda h, i: (rev(i), h)), pl.BlockSpec((c, dk), lambda h, i: (rev(i), nh + h)),
                  pl.BlockSpec((c, dv), lambda h, i: (rev(i), nh + h)), pl.BlockSpec((c, dk), lambda h, i: (rev(i), h)),
                  pl.BlockSpec((None, None, dv, dk), lambda h, i: (h, rev(i), 0, 0)),
                  pl.BlockSpec((c, dv), lambda h, i: (rev(i), h))],
        out_specs=[pl.BlockSpec((c, dk), lambda h, i: (rev(i), h)), pl.BlockSpec((c, dk), lambda h, i: (rev(i), h)),
                   pl.BlockSpec((c, dv), lambda h, i: (rev(i), h)), pl.BlockSpec((c, dk), lambda h, i: (rev(i), h)),
                   pl.BlockSpec((1, dk), lambda h, i: (0, h))],
        out_shape=[half, half, jax.ShapeDtypeStruct((t, d), BF16), half, jax.ShapeDtypeStruct((1, d // 2), F32)],
        scratch_shapes=[pltpu.VMEM((dv, dk), F32)], compiler_params=_params("parallel", "arbitrary"),
    )(proj, proj, proj, lg, states, do)


def gla_post_fwd(o, proj, head_norm, *, name, tm=512):
    t, d = o.shape
    dv = d // GLA_HEADS
    tm = min(tm, t)

    def body(o_ref, g_ref, hn_ref, y_ref):
        for h in range(GLA_HEADS):
            sl = slice(h * dv, (h + 1) * dv)
            ov, gv = o_ref[:, sl], g_ref[:, sl]
            r = lax.rsqrt(jnp.mean(ov * ov, axis=-1, keepdims=True) + RMS_EPS)
            y_ref[:, sl] = ((ov * r * hn_ref[:, sl]) * (gv * _sigmoid(gv))).astype(y_ref.dtype)

    row = pl.BlockSpec((tm, d), lambda i: (i, 0))
    return pl.pallas_call(
        body, name=name, grid=(t // tm,),
        in_specs=[row, pl.BlockSpec((tm, d), lambda i: (i, 2)), pl.BlockSpec((1, d), lambda i: (0, 0))],
        out_specs=row, out_shape=jax.ShapeDtypeStruct((t, d), BF16), compiler_params=_params("parallel"),
    )(o, proj, head_norm)


def gla_post_bwd(o, proj, head_norm, dy, *, name, tm=512):
    t, d = o.shape
    dv = d // GLA_HEADS
    tm = min(tm, t)

    def body(o_ref, g_ref, hn_ref, dy_ref, do_ref, dg_ref, dhn_ref):
        @pl.when(pl.program_id(0) == 0)
        def _():
            dhn_ref[...] = jnp.zeros_like(dhn_ref)

        for h in range(GLA_HEADS):
            sl = slice(h * dv, (h + 1) * dv)
            ov, gv, dyv, hn = o_ref[:, sl], g_ref[:, sl], dy_ref[:, sl], hn_ref[:, sl]
            r = lax.rsqrt(jnp.mean(ov * ov, axis=-1, keepdims=True) + RMS_EPS)
            sg = _sigmoid(gv)
            silu = gv * sg
            on = ov * r * hn
            dg_ref[:, sl] = (dyv * on * (sg * (1.0 + gv * (1.0 - sg)))).astype(dg_ref.dtype)
            don = dyv * silu
            u = don * hn
            do_ref[:, sl] = (r * u - ov * (r * r * r * jnp.mean(u * ov, axis=-1, keepdims=True))).astype(do_ref.dtype)
            dhn_ref[:, sl] += jnp.sum(don * ov * r, axis=0, keepdims=True)

    row = pl.BlockSpec((tm, d), lambda i: (i, 0))
    vec = pl.BlockSpec((1, d), lambda i: (0, 0))
    shape = jax.ShapeDtypeStruct((t, d), BF16)
    return pl.pallas_call(
        body, name=name, grid=(t // tm,),
        in_specs=[row, pl.BlockSpec((tm, d), lambda i: (i, 2)), vec, row],
        out_specs=[row, row, vec], out_shape=[shape, shape, jax.ShapeDtypeStruct((1, d), F32)],
        compiler_params=_params("arbitrary"),
    )(o, proj, head_norm, dy)


def _ffn_fwd(h, gains, w_up, w_down, tag):
    xn = rms_fwd(h, gains[2], name=f"{tag}_ffn_norm", out_dtype=BF16)
    act = matmul(xn, w_up, mode='nn', epi='relu2', out_dtype=BF16, name=f"{tag}_ffn_up")
    f = matmul(act, w_down, mode='nn', name=f"{tag}_ffn_down")
    h_out = rms_fwd(f, gains[3], res=h, name=f"{tag}_ffn_out")
    return h_out, (h, xn, act, f)


def _ffn_bwd(dh, saved, gains, w_up, w_down, tag):
    h, xn, act, f = saved
    df, dg3 = rms_bwd(f, gains[3], dh, out_dtype=BF16, name=f"{tag}_ffn_out_bwd")
    du = matmul(df, w_down, mode='nt', epi='mul2sqrt', extra=act, out_dtype=BF16, name=f"{tag}_ffn_da")
    dw_down = matmul(act, df, mode='tn', out_dtype=BF16, name=f"{tag}_ffn_dwdown")
    dw_up = matmul(xn, du, mode='tn', out_dtype=BF16, name=f"{tag}_ffn_dwup")
    dxn = matmul(du, w_up, mode='nt', name=f"{tag}_ffn_dxn")
    dh_in, dg2 = rms_bwd(h, gains[2], dxn, dres=dh, name=f"{tag}_ffn_norm_bwd")
    return dh_in, dg2, dg3, dw_up, dw_down


def _sb_layer_fwd(xn, w, j, tag, comm=None):
    qkv = matmul(xn, w['sb_w_qkv'][j], mode='nn', out_dtype=BF16, name=f"{tag}_qkv")
    if comm is None:
        o, tot = sb_fwd(qkv, name=f"{tag}_sb")
    else:
        o, tot, gathered = sb_fwd(qkv, name=f"{tag}_sb", gather=comm.rest_payload)
        comm.on_gathered(gathered)
    m = matmul(o, w['sb_w_o'][j], mode='nn', name=f"{tag}_wo")
    return m, (qkv, o, tot)


def _sb_layer_bwd(dm, xn, saved, w, j, tag, comm=None, grads=None):
    qkv, o, tot = saved
    do = matmul(dm, w['sb_w_o'][j], mode='nt', out_dtype=BF16, name=f"{tag}_do")
    dw_o = matmul(o, dm, mode='tn', out_dtype=BF16, name=f"{tag}_dwo")
    if comm is None:
        dq, dk, dv = sb_bwd(qkv, tot, do, name=f"{tag}_sb_bwd")
    else:
        parts = comm.rest_parts({**grads, ('sb_w_o', j): dw_o})
        dq, dk, dv, received = sb_bwd(qkv, tot, do, name=f"{tag}_sb_bwd", exchange=parts)
        comm.on_received(received)
    dqkv = jnp.concatenate([dq, dk, dv], axis=1)
    dw_qkv = matmul(xn, dqkv, mode='tn', out_dtype=BF16, name=f"{tag}_dwqkv")
    dxn = matmul(dqkv, w['sb_w_qkv'][j], mode='nt', name=f"{tag}_dxn")
    return dxn, {('sb_w_qkv', j): dw_qkv, ('sb_w_o', j): dw_o}


def _conv_layer_fwd(xn, w, j, tag):
    bcu = matmul(xn, w['conv_w_in'][j], mode='nn', name=f"{tag}_in")
    y = conv_fwd(bcu, w['conv_w'][j], name=f"{tag}_conv")
    m = matmul(y, w['conv_w_out'][j], mode='nn', name=f"{tag}_out")
    return m, (bcu, y)


def _conv_layer_bwd(dm, xn, saved, w, j, tag):
    bcu, y = saved
    dy = matmul(dm, w['conv_w_out'][j], mode='nt', name=f"{tag}_dy")
    dw_out = matmul(y, dm, mode='tn', out_dtype=BF16, name=f"{tag}_dwout")
    db, dc, du, dw_conv = conv_bwd(bcu, w['conv_w'][j], dy, name=f"{tag}_conv_bwd")
    dbcu = jnp.concatenate([db, dc, du], axis=1)
    dw_in = matmul(xn, dbcu, mode='tn', out_dtype=BF16, name=f"{tag}_dwin")
    dxn = matmul(dbcu, w['conv_w_in'][j], mode='nt', name=f"{tag}_dxn")
    return dxn, {('conv_w_in', j): dw_in, ('conv_w', j): dw_conv, ('conv_w_out', j): dw_out}


def _gla_split(w_in, w_gate_up):
    d = w_in.shape[0]
    w_main = w_in[:, :3 * d]
    w_a = jnp.pad(w_in[:, 3 * d:], ((0, 0), (0, LANES - GLA_GATE_RANK)))
    w_gu = jnp.pad(w_gate_up, ((0, LANES - GLA_GATE_RANK), (0, 0)))
    return w_main, w_a, w_gu


def _gla_layer_fwd(xn, w, j, tag):
    w_main, w_a, w_gu = _gla_split(w['gla_w_in'][j], w['gla_w_gate_up'][j])
    proj = matmul(xn, w_main, mode='nn', name=f"{tag}_in")
    a_low = matmul(xn, w_a, mode='nn', out_dtype=BF16, name=f"{tag}_alow")
    lg = matmul(a_low, w_gu, mode='nn', epi='logsig16', extra=w['gla_b_gate'][j][None, :], name=f"{tag}_gate")
    o, states = gla_fwd(proj, lg, name=f"{tag}_gla")
    hn = w['gla_head_norm'][j].reshape(1, -1)
    y = gla_post_fwd(o, proj, hn, name=f"{tag}_post")
    m = matmul(y, w['gla_w_o'][j], mode='nn', name=f"{tag}_wo")
    return m, (proj, a_low, lg, o, states, y)


def _gla_layer_bwd(dm, xn, saved, w, j, tag):
    proj, a_low, lg, o, states, y = saved
    w_main, w_a, w_gu = _gla_split(w['gla_w_in'][j], w['gla_w_gate_up'][j])
    hn = w['gla_head_norm'][j].reshape(1, -1)
    dy = matmul(dm, w['gla_w_o'][j], mode='nt', name=f"{tag}_dy")
    dw_o = matmul(y, dm, mode='tn', out_dtype=BF16, name=f"{tag}_dwo")
    do, dg, dhn = gla_post_bwd(o, proj, hn, dy, name=f"{tag}_post_bwd")
    dq, dk, dv, dzg, dbg = gla_bwd(proj, lg, states, do, name=f"{tag}_gla_bwd")
    da_low = matmul(dzg, w_gu, mode='nt', out_dtype=BF16, name=f"{tag}_dalow")
    dw_gu = matmul(a_low, dzg, mode='tn', out_dtype=BF16, name=f"{tag}_dwgu")[:GLA_GATE_RANK]
    dproj = jnp.concatenate([dq, dk, dv, dg], axis=1)
    dw_main = matmul(xn, dproj, mode='tn', out_dtype=BF16, name=f"{tag}_dwin")
    dw_a = matmul(xn, da_low, mode='tn', out_dtype=BF16, name=f"{tag}_dwa")[:, :GLA_GATE_RANK]
    dxn_a = matmul(da_low, w_a, mode='nt', name=f"{tag}_dxn_a")
    dxn = matmul(dproj, w_main, mode='nt', epi='add', extra=dxn_a, name=f"{tag}_dxn")
    grads = {('gla_w_in', j): jnp.concatenate([dw_main, dw_a], axis=1), ('gla_w_gate_up', j): dw_gu,
             ('gla_b_gate', j): dbg[0], ('gla_head_norm', j): dhn.reshape(w['gla_head_norm'][j].shape),
             ('gla_w_o', j): dw_o}
    return dxn, grads


_MIXERS = ((_sb_layer_fwd, _sb_layer_bwd), (_conv_layer_fwd, _conv_layer_bwd), (_gla_layer_fwd, _gla_layer_bwd))


def local_step(x, w, target, comm=None):
    depth = len(w['norm_gains'])
    h = x
    tape = []
    for i in range(depth):
        kind, j = i % 3, i // 3
        tag = f"l{i}"
        extra = {'comm': comm} if (comm is not None and i == 0) else {}
        gains = [w['norm_gains'][i][s][None, :] for s in range(4)]
        xn = rms_fwd(h, gains[0], name=f"{tag}_mix_norm", out_dtype=BF16)
        m, saved = _MIXERS[kind][0](xn, w, j, tag, **extra)
        h_mid = rms_fwd(m, gains[1], res=h, name=f"{tag}_mix_out")
        h_out, ffn_saved = _ffn_fwd(h_mid, gains, w['ffn_w_up'][i], w['ffn_w_down'][i], tag)
        tape.append((h, xn, m, saved, ffn_saved, gains))
        h = h_out
    loss, dh = loss_head(h, target, name="loss_head")

    grads = {}
    for i in reversed(range(depth)):
        kind, j = i % 3, i // 3
        tag = f"l{i}"
        h_in, xn, m, saved, ffn_saved, gains = tape[i]
        dg = [None] * 4
        dh, dg[2], dg[3], grads[('ffn_w_up', i)], grads[('ffn_w_down', i)] = _ffn_bwd(
            dh, ffn_saved, gains, w['ffn_w_up'][i], w['ffn_w_down'][i], tag)
        dm, dg[1] = rms_bwd(m, gains[1], dh, out_dtype=BF16, name=f"{tag}_mix_out_bwd")
        extra = {'comm': comm, 'grads': grads} if (comm is not None and i == 0) else {}
        dxn, g = _MIXERS[kind][1](dm, xn, saved, w, j, tag, **extra)
        grads.update(g)
        dh, dg[0] = rms_bwd(h_in, gains[0], dxn, dres=dh, name=f"{tag}_mix_norm_bwd")
        grads[('norm_gains', i)] = jnp.concatenate(dg, axis=0)
    return loss, dh, grads


def _segments(keys, shard_shapes, f32_as_pairs):
    segs, row = {}, 0
    for name, lo, hi in keys:
        n = (hi - lo) * math.prod(shard_shapes[name][1:]) * (2 if f32_as_pairs and name in F32_PAYLOAD else 1)
        nrows = -(-n // (PACK_COLS * PACK_ROW_ALIGN)) * PACK_ROW_ALIGN
        segs[(name, lo, hi)] = (row, nrows, n)
        row += nrows
    return segs, -(-row // PACK_ROW_BLOCK) * PACK_ROW_BLOCK


def _pack(parts, segs, total_rows, dtype):
    pieces, row = [], 0
    for key in segs:
        _, nrows, n = segs[key]
        p = parts[key].astype(dtype)
        lead = p.shape[:-1]
        if nrows * PACK_COLS > n:
            p = jnp.pad(p, [(0, 0)] * len(lead) + [(0, nrows * PACK_COLS - n)])
        pieces.append(p.reshape(lead + (nrows, PACK_COLS)))
        row += nrows
    if total_rows > row:
        pieces.append(jnp.zeros(lead + (total_rows - row, PACK_COLS), dtype))
    return jnp.concatenate(pieces, axis=-2)


def _unpack(buf, seg):
    first, nrows, n = seg
    piece = buf[..., first:first + nrows, :]
    return piece.reshape(piece.shape[:-2] + (nrows * PACK_COLS,))[..., :n]


def _unshard(gathered, axis):
    moved = jnp.moveaxis(gathered, 0, axis)
    shape = moved.shape
    return moved.reshape(shape[:axis] + (shape[axis] * shape[axis + 1],) + shape[axis + 2:])


def _shard_split(full, axis):
    shape = full.shape
    cut = full.reshape(shape[:axis] + (N_DEV, shape[axis] // N_DEV) + shape[axis + 1:])
    return jnp.moveaxis(cut, axis, 0)


def _mesh_position():
    return lax.axis_index("x"), lax.axis_index("y"), lax.axis_index("c")


def _comm_scratch():
    return [pltpu.SemaphoreType.DMA((N_DEV - 1,)), pltpu.SemaphoreType.DMA((N_DEV - 1,)), pltpu.SemaphoreType.DMA]


def _gather_plan(x_ref, out_ref, send_sems, recv_sems, local_sem):
    rows = x_ref.shape[0]
    x, y, c = _mesh_position()
    me, sibling = (x, y, c), (x, y, 1 - c)
    chips = [(1 - x, y), (x, 1 - y), (1 - x, 1 - y)]

    def block(px, py, pc):
        return out_ref.at[pl.ds((4 * px + 2 * py + pc) * rows, rows), :]

    def copy(k, blk, to, src=None):
        return pltpu.make_async_remote_copy(
            src_ref=block(*blk) if src is None else src, dst_ref=block(*blk),
            send_sem=send_sems.at[k], recv_sem=recv_sems.at[k],
            device_id=to, device_id_type=pl.DeviceIdType.MESH)

    mine = pltpu.make_async_copy(x_ref, block(*me), local_sem)
    first = [copy(0, me, sibling, src=x_ref)]
    first += [copy(1 + j, me, (*chip, c), src=x_ref) for j, chip in enumerate(chips)]
    passed = [copy(4 + j, (*chip, c), sibling) for j, chip in enumerate(chips)]

    def start():
        mine.start()
        for cp in first:
            cp.start()

    def forward():
        for j, chip in enumerate(chips):
            copy(1 + j, (*chip, c), me).wait_recv()
            passed[j].start()

    def finish():
        copy(0, sibling, me).wait_recv()
        for j, chip in enumerate(chips):
            copy(4 + j, (*chip, 1 - c), me).wait_recv()
        for cp in first + passed:
            cp.wait_send()
        mine.wait()

    return start, forward, finish


def _exchange_plan(in_ref, out_ref, send_sems, recv_sems, local_sem):
    x, y, c = _mesh_position()
    my_id = 4 * x + 2 * y + c
    mine = pltpu.make_async_copy(in_ref.at[my_id], out_ref.at[my_id], local_sem)

    def copy(k, receive):
        px = 1 - x if k & 4 else x
        py = 1 - y if k & 2 else y
        pc = 1 - c if k & 1 else c
        peer_id = 4 * px + 2 * py + pc
        return pltpu.make_async_remote_copy(
            src_ref=in_ref.at[peer_id], dst_ref=out_ref.at[peer_id if receive else my_id],
            send_sem=send_sems.at[k - 1], recv_sem=recv_sems.at[k - 1],
            device_id=(px, py, pc), device_id_type=pl.DeviceIdType.MESH)

    def start():
        mine.start()
        for k in range(1, N_DEV):
            copy(k, False).start()

    def finish():
        for k in range(1, N_DEV):
            copy(k, True).wait_recv()
        for k in range(1, N_DEV):
            copy(k, False).wait_send()
        mine.wait()

    return start, finish


def all_gather(shard, *, name):
    rows, cols = shard.shape

    def body(x_ref, out_ref, send_sems, recv_sems, local_sem):
        start, forward, finish = _gather_plan(x_ref, out_ref, send_sems, recv_sems, local_sem)
        start()
        forward()
        finish()

    return pl.pallas_call(
        body, name=name, out_shape=jax.ShapeDtypeStruct((N_DEV * rows, cols), shard.dtype),
        in_specs=[pl.BlockSpec(memory_space=pl.ANY)], out_specs=pl.BlockSpec(memory_space=pl.ANY),
        scratch_shapes=_comm_scratch(),
    )(shard)


def exchange_shards(parts, *, name):
    def body(in_ref, out_ref, send_sems, recv_sems, local_sem):
        start, finish = _exchange_plan(in_ref, out_ref, send_sems, recv_sems, local_sem)
        start()
        finish()

    return pl.pallas_call(
        body, name=name, out_shape=jax.ShapeDtypeStruct(parts.shape, parts.dtype),
        in_specs=[pl.BlockSpec(memory_space=pl.ANY)], out_specs=pl.BlockSpec(memory_space=pl.ANY),
        scratch_shapes=_comm_scratch(),
    )(parts)


def adamw(parts, w, m, v, *, name):
    _, rows, cols = parts.shape
    tr = PACK_ROW_BLOCK
    c1 = 1.0 - ADAM_B1 ** ADAM_STEP
    c2 = 1.0 - ADAM_B2 ** ADAM_STEP

    def body(p_ref, w_ref, m_ref, v_ref, g_ref, d_ref, nm_ref, nv_ref):
        g = p_ref[0].astype(F32)
        for s in range(1, N_DEV):
            g = g + p_ref[s].astype(F32)
        nm = ADAM_B1 * m_ref[...] + (1.0 - ADAM_B1) * g
        nv = ADAM_B2 * v_ref[...] + (1.0 - ADAM_B2) * jnp.square(g)
        m_hat = nm / c1
        v_hat = nv / c2
        g_ref[...] = g
        d_ref[...] = -ADAM_LR * (m_hat / (jnp.sqrt(v_hat) + ADAM_EPS) + ADAM_WD * w_ref[...])
        nm_ref[...] = nm
        nv_ref[...] = nv

    row = pl.BlockSpec((tr, cols), lambda i: (i, 0))
    shape = jax.ShapeDtypeStruct((rows, cols), F32)
    return pl.pallas_call(
        body, name=name, grid=(rows // tr,),
        in_specs=[pl.BlockSpec((N_DEV, tr, cols), lambda i: (0, i, 0)), row, row, row],
        out_specs=[row, row, row, row], out_shape=[shape, shape, shape, shape],
        compiler_params=_params("parallel"),
    )(parts, w, m, v)


def kernel(x, norm_gains, sb_w_qkv, sb_w_o, conv_w_in, conv_w, conv_w_out, gla_w_in, gla_w_gate_up, gla_b_gate, gla_head_norm, gla_w_o, ffn_w_up, ffn_w_down, loss_target, m_norm_gains, m_sb_w_qkv, m_sb_w_o, m_conv_w_in, m_conv_w, m_conv_w_out, m_gla_w_in, m_gla_w_gate_up, m_gla_b_gate, m_gla_head_norm, m_gla_w_o, m_ffn_w_up, m_ffn_w_down, v_norm_gains, v_sb_w_qkv, v_sb_w_o, v_conv_w_in, v_conv_w, v_conv_w_out, v_gla_w_in, v_gla_w_gate_up, v_gla_b_gate, v_gla_head_norm, v_gla_w_o, v_ffn_w_up, v_ffn_w_down):
    shards = dict(zip(WEIGHTS, (norm_gains, sb_w_qkv, sb_w_o, conv_w_in, conv_w, conv_w_out, gla_w_in,
                                gla_w_gate_up, gla_b_gate, gla_head_norm, gla_w_o, ffn_w_up, ffn_w_down)))
    moments_m = dict(zip(WEIGHTS, (m_norm_gains, m_sb_w_qkv, m_sb_w_o, m_conv_w_in, m_conv_w, m_conv_w_out,
                                   m_gla_w_in, m_gla_w_gate_up, m_gla_b_gate, m_gla_head_norm, m_gla_w_o,
                                   m_ffn_w_up, m_ffn_w_down)))
    moments_v = dict(zip(WEIGHTS, (v_norm_gains, v_sb_w_qkv, v_sb_w_o, v_conv_w_in, v_conv_w, v_conv_w_out,
                                   v_gla_w_in, v_gla_w_gate_up, v_gla_b_gate, v_gla_head_norm, v_gla_w_o,
                                   v_ffn_w_up, v_ffn_w_down)))
    shard_shapes = {n: a.shape for n, a in shards.items()}

    alone = [('norm_gains', 0, 1), ('sb_w_qkv', 0, 1)]
    rest = [(n, 1 if (n, 0, 1) in alone else 0, shard_shapes[n][0]) for n in WEIGHTS]
    groups = [alone, [piece for piece in rest if piece[1] < piece[2]]]

    def payload(group):
        segs, rows = _segments(group, shard_shapes, True)
        flat = {(n, lo, hi): (lax.bitcast_convert_type(shards[n][lo:hi], BF16) if n in F32_PAYLOAD
                              else shards[n][lo:hi].astype(BF16)).reshape(-1) for n, lo, hi in group}
        return segs, _pack(flat, segs, rows, BF16)

    whole = {n: [None] * shard_shapes[n][0] for n in WEIGHTS}

    def take_gathered(segs, gathered):
        gathered = gathered.reshape(N_DEV, -1, PACK_COLS)
        for n, lo, hi in segs:
            piece = _unpack(gathered, segs[(n, lo, hi)])
            if n in F32_PAYLOAD:
                piece = lax.bitcast_convert_type(piece.reshape(N_DEV, -1, 2), F32)
            full = _unshard(piece.reshape((N_DEV, hi - lo) + shard_shapes[n][1:]), SHARD_AXIS[n])
            for j in range(lo, hi):
                whole[n][j] = full[j - lo]

    segs0, payload0 = payload(groups[0])
    segs1, payload1 = payload(groups[1])
    take_gathered(segs0, all_gather(payload0, name="weights_all_gather"))

    gsegs = [_segments(group, shard_shapes, False) for group in groups]
    received = [None, None]

    def parts(g, grads):
        segs, rows = gsegs[g]
        flat = {(n, lo, hi): _shard_split(jnp.stack([grads[(n, j)] for j in range(lo, hi)]), SHARD_AXIS[n])
                .reshape(N_DEV, -1) for n, lo, hi in segs}
        return _pack(flat, segs, rows, BF16)

    def on_received(buf):
        received[1] = buf

    comm = types.SimpleNamespace(rest_payload=payload1, on_gathered=functools.partial(take_gathered, segs1),
                                 rest_parts=functools.partial(parts, 1), on_received=on_received)

    loss, grad_x, grads = local_step(x[0], whole, loss_target[0], comm)
    loss = lax.psum(loss[0, 0], ("x", "y", "c"))
    received[0] = exchange_shards(parts(0, grads), name="grads_exchange")

    results = {n: [[] for _ in range(4)] for n in WEIGHTS}
    for g in range(2):
        segs, rows = gsegs[g]

        def packed(source):
            return _pack({(n, lo, hi): source[n][lo:hi].reshape(-1) for n, lo, hi in segs}, segs, rows, F32)

        outs = adamw(received[g], packed(shards), packed(moments_m), packed(moments_v), name=f"adamw{g}")
        for n, lo, hi in segs:
            for o, buf in enumerate(outs):
                results[n][o].append(_unpack(buf, segs[(n, lo, hi)]).reshape((hi - lo,) + shard_shapes[n][1:]))
    whole_out = [[r[0] if len(r) == 1 else jnp.concatenate(r, axis=0) for r in (results[n][o] for n in WEIGHTS)]
                 for o in range(4)]
    return (loss, grad_x[None], *whole_out[0], *whole_out[1], *whole_out[2], *whole_out[3])
```

```python
import functools
import math
import types

import jax
import jax.numpy as jnp
from jax import lax
from jax.experimental import pallas as pl
from jax.experimental.pallas import tpu as pltpu

F32 = jnp.float32
BF16 = jnp.bfloat16

N_DEV = 8
SB_HEADS = 16
GLA_HEADS = 4
GLA_CHUNK = 64
GLA_SUB = 16
GLA_GATE_RANK = 16
GLA_GATE_NORMALIZER = 16.0
CONV_WIDTH = 3
RMS_EPS = 1e-6
ADAM_LR = 0.001
ADAM_B1 = 0.9
ADAM_B2 = 0.999
ADAM_EPS = 1e-08
ADAM_WD = 0.01
ADAM_STEP = 10

LANES = 128
SB_BLOCK = 256
SB_FWD_Q_BLOCK = 1024
SB_BWD_Q_BLOCK = 1024
VMEM_LIMIT_BYTES = 56 * 1024 * 1024
MM_TM, MM_TN, MM_TK = 1024, 1024, 1024
ROW_BLOCK = 512
PACK_COLS = 1024
PACK_ROW_ALIGN = 16
PACK_ROW_BLOCK = 128

WEIGHTS = ['norm_gains', 'sb_w_qkv', 'sb_w_o', 'conv_w_in', 'conv_w', 'conv_w_out', 'gla_w_in',
           'gla_w_gate_up', 'gla_b_gate', 'gla_head_norm', 'gla_w_o', 'ffn_w_up', 'ffn_w_down']
SHARD_AXIS = {'norm_gains': 2, 'sb_w_qkv': 2, 'sb_w_o': 1, 'conv_w_in': 2, 'conv_w': 2, 'conv_w_out': 1,
              'gla_w_in': 2, 'gla_w_gate_up': 2, 'gla_b_gate': 1, 'gla_head_norm': 2, 'gla_w_o': 1,
              'ffn_w_up': 2, 'ffn_w_down': 1}
F32_PAYLOAD = ('norm_gains', 'conv_w', 'gla_b_gate', 'gla_head_norm')

_NN = (((1,), (0,)), ((), ()))
_NT = (((1,), (1,)), ((), ()))
_TN = (((0,), (0,)), ((), ()))
_DIMS = {'nn': _NN, 'nt': _NT, 'tn': _TN}


def _params(*semantics):
    return pltpu.CompilerParams(dimension_semantics=semantics, vmem_limit_bytes=VMEM_LIMIT_BYTES)


def _dot(a, b, dims=_NN):
    return lax.dot_general(a.astype(BF16), b.astype(BF16), dims, preferred_element_type=F32)


def _split_hi_lo(x):
    hi = x.astype(BF16)
    lo = (x - hi.astype(F32)).astype(BF16)
    return hi, lo


def _dot_exact_rhs(x, ones_mat, dims=_NN):
    hi, lo = _split_hi_lo(x)
    return (lax.dot_general(hi, ones_mat, dims, preferred_element_type=F32)
            + lax.dot_general(lo, ones_mat, dims, preferred_element_type=F32))


def _dot_exact_lhs(ones_mat, x, dims=_NN):
    hi, lo = _split_hi_lo(x)
    return (lax.dot_general(ones_mat, hi, dims, preferred_element_type=F32)
            + lax.dot_general(ones_mat, lo, dims, preferred_element_type=F32))


def _log_sigmoid(z):
    return jnp.minimum(z, 0.0) - jnp.log(1.0 + jnp.exp(-jnp.abs(z)))


def _sigmoid(z):
    return 1.0 / (1.0 + jnp.exp(-z))


def matmul(a, b, *, mode, name, out_dtype=F32, epi=None, extra=None, tm=MM_TM, tn=MM_TN, tk=MM_TK):
    if mode == 'nn':
        (m, k), (k2, n) = a.shape, b.shape
    elif mode == 'nt':
        (m, k), (n, k2) = a.shape, b.shape
    else:
        (k, m), (k2, n) = a.shape, b.shape
    assert k == k2, (a.shape, b.shape, mode)
    tm, tn, tk = min(tm, m), min(tn, n), min(tk, k)
    assert m % tm == 0 and n % tn == 0 and k % tk == 0, (a.shape, b.shape, mode)
    nk = k // tk
    if mode == 'tn':
        a_spec = pl.BlockSpec((tk, tm), lambda i, j, kk: (kk, i))
    else:
        a_spec = pl.BlockSpec((tm, tk), lambda i, j, kk: (i, kk))
    if mode == 'nt':
        b_spec = pl.BlockSpec((tn, tk), lambda i, j, kk: (j, kk))
    else:
        b_spec = pl.BlockSpec((tk, tn), lambda i, j, kk: (kk, j))
    in_specs, operands = [a_spec, b_spec], [a, b]
    if epi == 'logsig16':
        in_specs.append(pl.BlockSpec((1, tn), lambda i, j, kk: (0, j)))
        operands.append(extra)
    elif epi in ('mul2sqrt', 'add'):
        in_specs.append(pl.BlockSpec((tm, tn), lambda i, j, kk: (i, j)))
        operands.append(extra)

    n_extra = len(operands) - 2

    def body(a_ref, b_ref, *rest):
        e_ref = rest[0] if n_extra else None
        o_ref = rest[n_extra]

        def finish(r):
            if epi == 'relu2':
                r = jnp.square(jnp.maximum(r, 0.0))
            elif epi == 'mul2sqrt':
                r = r * (2.0 * jnp.sqrt(e_ref[...].astype(F32)))
            elif epi == 'add':
                r = r + e_ref[...]
            elif epi == 'logsig16':
                r = _log_sigmoid(r + e_ref[...]) / GLA_GATE_NORMALIZER
            o_ref[...] = r.astype(o_ref.dtype)

        part = _dot(a_ref[...], b_ref[...], _DIMS[mode])
        if nk == 1:
            finish(part)
        else:
            acc_ref = rest[-1]
            kk = pl.program_id(2)

            @pl.when(kk == 0)
            def _():
                acc_ref[...] = part

            @pl.when(kk > 0)
            def _():
                acc_ref[...] += part

            @pl.when(kk == nk - 1)
            def _():
                finish(acc_ref[...])

    return pl.pallas_call(
        body, name=name, grid=(m // tm, n // tn, nk), in_specs=in_specs,
        out_specs=pl.BlockSpec((tm, tn), lambda i, j, kk: (i, j)),
        out_shape=jax.ShapeDtypeStruct((m, n), out_dtype),
        scratch_shapes=[pltpu.VMEM((tm, tn), F32)] if nk > 1 else [],
        compiler_params=_params("parallel", "parallel", "arbitrary"),
    )(*operands)


def rms_fwd(x, gain, *, name, res=None, out_dtype=F32, tm=ROW_BLOCK):
    t, d = x.shape
    tm = min(tm, t)
    row = pl.BlockSpec((tm, d), lambda i: (i, 0))
    in_specs, operands = [row, pl.BlockSpec((1, d), lambda i: (0, 0))], [x, gain]
    if res is not None:
        in_specs.append(row)
        operands.append(res)

    def body(x_ref, g_ref, *rest):
        xv = x_ref[...]
        r = lax.rsqrt(jnp.mean(xv * xv, axis=-1, keepdims=True) + RMS_EPS)
        y = xv * r * g_ref[...]
        if res is not None:
            y = rest[0][...] + y
        rest[-1][...] = y.astype(out_dtype)

    return pl.pallas_call(
        body, name=name, grid=(t // tm,), in_specs=in_specs, out_specs=row,
        out_shape=jax.ShapeDtypeStruct((t, d), out_dtype), compiler_params=_params("parallel"),
    )(*operands)


def rms_bwd(x, gain, dy, *, name, dres=None, out_dtype=F32, tm=ROW_BLOCK):
    t, d = x.shape
    tm = min(tm, t)
    row = pl.BlockSpec((tm, d), lambda i: (i, 0))
    vec = pl.BlockSpec((1, d), lambda i: (0, 0))
    in_specs, operands = [row, vec, row], [x, gain, dy]
    if dres is not None:
        in_specs.append(row)
        operands.append(dres)

    def body(x_ref, g_ref, dy_ref, *rest):
        dx_ref, dg_ref = rest[-2], rest[-1]

        @pl.when(pl.program_id(0) == 0)
        def _():
            dg_ref[...] = jnp.zeros_like(dg_ref)

        xv, dyv = x_ref[...], dy_ref[...]
        r = lax.rsqrt(jnp.mean(xv * xv, axis=-1, keepdims=True) + RMS_EPS)
        u = dyv * g_ref[...]
        dx = r * u - xv * (r * r * r * jnp.mean(u * xv, axis=-1, keepdims=True))
        if dres is not None:
            dx = rest[0][...] + dx
        dx_ref[...] = dx.astype(out_dtype)
        dg_ref[...] += jnp.sum(dyv * xv * r, axis=0, keepdims=True)

    return pl.pallas_call(
        body, name=name, grid=(t // tm,), in_specs=in_specs, out_specs=[row, vec],
        out_shape=[jax.ShapeDtypeStruct((t, d), out_dtype), jax.ShapeDtypeStruct((1, d), F32)],
        compiler_params=_params("arbitrary"),
    )(*operands)


def loss_head(y, target, *, name, tm=ROW_BLOCK):
    t, d = y.shape
    tm = min(tm, t)
    nt = t // tm
    row = pl.BlockSpec((tm, d), lambda i: (i, 0))

    def body(y_ref, t_ref, loss_ref, dy_ref, acc_ref):
        i = pl.program_id(0)

        @pl.when(i == 0)
        def _():
            acc_ref[...] = jnp.zeros_like(acc_ref)

        err = y_ref[...] - t_ref[...]
        dy_ref[...] = err * (1.0 / d)
        acc_ref[...] += jnp.sum(err * err, axis=0, keepdims=True)

        @pl.when(i == nt - 1)
        def _():
            loss_ref[...] = jnp.sum(acc_ref[...], axis=1, keepdims=True) * (0.5 / d)

    return pl.pallas_call(
        body, name=name, grid=(nt,), in_specs=[row, row],
        out_specs=[pl.BlockSpec((1, 1), lambda i: (0, 0)), row],
        out_shape=[jax.ShapeDtypeStruct((1, 1), F32), jax.ShapeDtypeStruct((t, d), F32)],
        scratch_shapes=[pltpu.VMEM((1, d), F32)], compiler_params=_params("arbitrary"),
    )(y, target)


def _sb_block_iota():
    rows = lax.broadcasted_iota(jnp.int32, (SB_BLOCK, SB_BLOCK), 0)
    cols = lax.broadcasted_iota(jnp.int32, (SB_BLOCK, SB_BLOCK), 1)
    return rows, cols


def _sb_logits(q_h, k_blk, mask):
    z = _dot(q_h, k_blk, _NT)
    ls = _log_sigmoid(z)
    lm = ls - z
    if mask is not None:
        lm = jnp.where(mask, lm, 0.0)
    return ls, lm, jnp.sum(lm, axis=1, keepdims=True)


def _sb_weights(ls, lm, mask, tri_strict, later):
    suffix = _dot_exact_rhs(lm, tri_strict)
    w = jnp.exp(ls + suffix + later)
    return w if mask is None else jnp.where(mask, w, 0.0)


def _on_grid_step(p, i):
    return jnp.logical_and(pl.program_id(0) == p, pl.program_id(1) == i)


def sb_fwd(qkv, *, name, gather=None):
    t, d3 = qkv.shape
    d = d3 // 3
    head_dim = d // SB_HEADS
    qb, kb_rows = SB_FWD_Q_BLOCK, SB_BLOCK
    assert 2 * head_dim == LANES and t % qb == 0
    pairs = d // LANES
    per_q = qb // kb_rows
    nq = t // qb
    scale = head_dim ** -0.5

    def body(q_ref, k_ref, v_ref, *rest):
        if gather is None:
            compute(q_ref, k_ref, v_ref, *rest)
            return
        x_ref, o_ref, tot_ref, out_ref, send_sems, recv_sems, local_sem = rest
        start, forward, finish = _gather_plan(x_ref, out_ref, send_sems, recv_sems, local_sem)
        pl.when(_on_grid_step(0, 0))(start)
        pl.when(_on_grid_step(3 * pairs // 4, 0))(forward)
        compute(q_ref, k_ref, v_ref, o_ref, tot_ref)
        pl.when(_on_grid_step(pairs - 1, nq - 1))(finish)

    def compute(q_ref, k_ref, v_ref, o_ref, tot_ref):
        qi = pl.program_id(1)
        lane = lax.broadcasted_iota(jnp.int32, (qb, LANES), 1)
        first = lane < head_dim
        q = q_ref[...] * scale
        q2 = jnp.concatenate([jnp.where(first, q, jnp.zeros_like(q)), jnp.where(first, jnp.zeros_like(q), q)], axis=0)
        rows, cols = _sb_block_iota()
        tri = jnp.where(rows > cols, 1.0, 0.0).astype(BF16)
        q_row = lax.broadcasted_iota(jnp.int32, (qb, kb_rows), 0)
        k_col = lax.broadcasted_iota(jnp.int32, (qb, kb_rows), 1)
        diagonal = [jnp.concatenate([m * kb_rows + k_col < q_row] * 2, axis=0) for m in range(per_q)]

        def step(kb, carry, mask):
            ks = pl.multiple_of(kb * SB_BLOCK, SB_BLOCK)
            k_blk = k_ref[pl.ds(ks, SB_BLOCK), :]
            v_blk = v_ref[pl.ds(ks, SB_BLOCK), :]
            acc, later = carry
            ls, lm, row = _sb_logits(q2, k_blk, mask)
            w = _sb_weights(ls, lm, mask, tri, later)
            return acc + _dot(w, v_blk), later + row

        out = (jnp.zeros((2 * qb, LANES), F32), jnp.zeros((2 * qb, 1), F32))
        for m in reversed(range(per_q)):
            out = step(per_q * qi + m, out, diagonal[m])
        acc, total = lax.fori_loop(0, per_q * qi, lambda i, carry: step(per_q * qi - 1 - i, carry, None), out)
        o_ref[...] = jnp.where(first, acc[:qb], acc[qb:]).astype(o_ref.dtype)
        tot_ref[...] = jnp.where(first, total[:qb], total[qb:])

    blk = lambda off: pl.BlockSpec((t, LANES), lambda p, i: (0, off + p))
    qblk = pl.BlockSpec((qb, LANES), lambda p, i: (i, p))
    in_specs, operands = [qblk, blk(pairs), blk(2 * pairs)], [qkv, qkv, qkv]
    out_specs = [qblk, qblk]
    out_shape = [jax.ShapeDtypeStruct((t, d), BF16), jax.ShapeDtypeStruct((t, d), F32)]
    if gather is None:
        return pl.pallas_call(
            body, name=name, grid=(pairs, nq), in_specs=in_specs, out_specs=out_specs, out_shape=out_shape,
            compiler_params=_params("parallel", "arbitrary"),
        )(*operands)
    whole = pl.BlockSpec(memory_space=pl.ANY)
    return pl.pallas_call(
        body, name=name, grid=(pairs, nq), in_specs=in_specs + [whole], out_specs=out_specs + [whole],
        out_shape=out_shape + [jax.ShapeDtypeStruct((N_DEV * gather.shape[0], gather.shape[1]), gather.dtype)],
        scratch_shapes=_comm_scratch(), compiler_params=_params("arbitrary", "arbitrary"),
    )(*operands, gather)


def sb_bwd(qkv, tot, do, *, name, exchange=None):
    t, d3 = qkv.shape
    d = d3 // 3
    head_dim = d // SB_HEADS
    pairs = d // LANES
    qb, kb_rows = SB_BWD_Q_BLOCK, SB_BLOCK
    per_q = qb // kb_rows
    nq = t // qb
    scale = head_dim ** -0.5

    def body(*refs):
        if exchange is None:
            compute(*refs)
            return
        q_ref, k_ref, v_ref, tot_ref, do_ref, in_ref, dq_ref, dk_ref, dv_ref, out_ref = refs[:10]
        dk_sum, dv_sum, send_sems, recv_sems, local_sem = refs[10:]
        start, finish = _exchange_plan(in_ref, out_ref, send_sems, recv_sems, local_sem)
        pl.when(_on_grid_step(0, 0))(start)
        compute(q_ref, k_ref, v_ref, tot_ref, do_ref, dq_ref, dk_ref, dv_ref, dk_sum, dv_sum)
        pl.when(_on_grid_step(pairs - 1, nq - 1))(finish)

    def compute(q_ref, k_ref, v_ref, tot_ref, do_ref, dq_ref, dk_ref, dv_ref, dk_sum, dv_sum):
        qi = pl.program_id(1)

        @pl.when(qi == 0)
        def _():
            dk_sum[...] = jnp.zeros_like(dk_sum)
            dv_sum[...] = jnp.zeros_like(dv_sum)

        lane = lax.broadcasted_iota(jnp.int32, (qb, LANES), 1)
        first = lane < head_dim
        q, dov, totv = q_ref[...] * scale, do_ref[...], tot_ref[...]
        second = jnp.logical_not(first)
        q2 = jnp.concatenate([jnp.where(s, q, jnp.zeros_like(q)) for s in (first, second)], axis=0)
        do2 = jnp.concatenate([jnp.where(s, dov, jnp.zeros_like(dov)) for s in (first, second)], axis=0)
        tot2 = jnp.concatenate([totv[:, 0:1], totv[:, head_dim:head_dim + 1]], axis=0)
        rows, cols = _sb_block_iota()
        tri_strict = jnp.where(rows > cols, 1.0, 0.0).astype(BF16)
        tri_before = jnp.where(rows < cols, 1.0, 0.0).astype(BF16)
        q_row = lax.broadcasted_iota(jnp.int32, (qb, kb_rows), 0)
        k_col = lax.broadcasted_iota(jnp.int32, (qb, kb_rows), 1)
        diagonal = [jnp.concatenate([m * kb_rows + k_col < q_row] * 2, axis=0) for m in range(per_q)]

        def step(kb, carry, mask):
            ks = pl.multiple_of(kb * SB_BLOCK, SB_BLOCK)
            k_blk = k_ref[pl.ds(ks, SB_BLOCK), :]
            v_blk = v_ref[pl.ds(ks, SB_BLOCK), :]
            dq, seen, before = carry
            ls, lm, row = _sb_logits(q2, k_blk, mask)
            seen = seen + row
            w = _sb_weights(ls, lm, mask, tri_strict, tot2 - seen)
            da = _dot(do2, v_blk, _NT) * w
            g = _dot_exact_rhs(da, tri_before) + before
            dz = da - jnp.exp(ls) * (da + g)
            if mask is not None:
                dz = jnp.where(mask, dz, 0.0)
            dk_sum[pl.ds(ks, SB_BLOCK), :] += _dot(dz, q2, _TN)
            dv_sum[pl.ds(ks, SB_BLOCK), :] += _dot(w, do2, _TN)
            return dq + _dot(dz, k_blk * scale), seen, before + jnp.sum(da, axis=1, keepdims=True)

        zero = jnp.zeros((2 * qb, LANES), F32)
        zcol = jnp.zeros((2 * qb, 1), F32)
        out = lax.fori_loop(0, per_q * qi, lambda kb, carry: step(kb, carry, None), (zero, zcol, zcol))
        for m in range(per_q):
            out = step(per_q * qi + m, out, diagonal[m])
        dq = out[0]
        dq_ref[...] = jnp.where(first, dq[:qb], dq[qb:]).astype(dq_ref.dtype)

        @pl.when(qi == nq - 1)
        def _():
            dk_ref[...] = dk_sum[...].astype(dk_ref.dtype)
            dv_ref[...] = dv_sum[...].astype(dv_ref.dtype)

    qblk = pl.BlockSpec((qb, LANES), lambda p, i: (i, p))
    col = lambda off: pl.BlockSpec((t, LANES), lambda p, i: (0, off + p))
    shape = jax.ShapeDtypeStruct((t, d), BF16)
    in_specs, operands = [qblk, col(pairs), col(2 * pairs), qblk, qblk], [qkv, qkv, qkv, tot, do]
    out_specs, out_shape = [qblk, col(0), col(0)], [shape, shape, shape]
    sums = [pltpu.VMEM((t, LANES), F32), pltpu.VMEM((t, LANES), F32)]
    if exchange is None:
        return pl.pallas_call(
            body, name=name, grid=(pairs, nq), in_specs=in_specs, out_specs=out_specs, out_shape=out_shape,
            scratch_shapes=sums, compiler_params=_params("parallel", "arbitrary"),
        )(*operands)
    whole = pl.BlockSpec(memory_space=pl.ANY)
    return pl.pallas_call(
        body, name=name, grid=(pairs, nq), in_specs=in_specs + [whole], out_specs=out_specs + [whole],
        out_shape=out_shape + [jax.ShapeDtypeStruct(exchange.shape, exchange.dtype)],
        scratch_shapes=sums + _comm_scratch(), compiler_params=_params("arbitrary", "arbitrary"),
    )(*operands, exchange)


def _shift_down(x, s):
    rows = lax.broadcasted_iota(jnp.int32, x.shape, 0)
    return jnp.where(rows >= s, pltpu.roll(x, s, 0), 0.0)


def _shift_up(x, s):
    t = x.shape[0]
    rows = lax.broadcasted_iota(jnp.int32, x.shape, 0)
    return jnp.where(rows < t - s, pltpu.roll(x, t - s, 0), 0.0)


def conv_fwd(bcu, w, *, name):
    t, d3 = bcu.shape
    d = d3 // 3
    nb = d // LANES
    col = lambda off: pl.BlockSpec((t, LANES), lambda j: (0, off + j))

    def body(b_ref, c_ref, u_ref, w_ref, y_ref):
        hh = c_ref[...] * u_ref[...]
        conv = w_ref[0:1, :] * _shift_down(hh, 2) + w_ref[1:2, :] * _shift_down(hh, 1) + w_ref[2:3, :] * hh
        y_ref[...] = (b_ref[...] * conv).astype(y_ref.dtype)

    return pl.pallas_call(
        body, name=name, grid=(nb,),
        in_specs=[col(0), col(nb), col(2 * nb), pl.BlockSpec((CONV_WIDTH, LANES), lambda j: (0, j))],
        out_specs=col(0), out_shape=jax.ShapeDtypeStruct((t, d), BF16), compiler_params=_params("parallel"),
    )(bcu, bcu, bcu, w)


def conv_bwd(bcu, w, dy, *, name):
    t, d3 = bcu.shape
    d = d3 // 3
    nb = d // LANES
    col = lambda off: pl.BlockSpec((t, LANES), lambda j: (0, off + j))
    wspec = pl.BlockSpec((CONV_WIDTH, LANES), lambda j: (0, j))

    def body(b_ref, c_ref, u_ref, w_ref, dy_ref, db_ref, dc_ref, du_ref, dw_ref):
        c, u, dyv = c_ref[...], u_ref[...], dy_ref[...]
        hh = c * u
        h2, h1 = _shift_down(hh, 2), _shift_down(hh, 1)
        w0, w1, w2 = w_ref[0:1, :], w_ref[1:2, :], w_ref[2:3, :]
        db_ref[...] = (dyv * (w0 * h2 + w1 * h1 + w2 * hh)).astype(db_ref.dtype)
        dconv = dyv * b_ref[...]
        dhh = w2 * dconv + w1 * _shift_up(dconv, 1) + w0 * _shift_up(dconv, 2)
        dc_ref[...] = (dhh * u).astype(dc_ref.dtype)
        du_ref[...] = (dhh * c).astype(du_ref.dtype)
        dw_ref[0:1, :] = jnp.sum(dconv * h2, axis=0, keepdims=True)
        dw_ref[1:2, :] = jnp.sum(dconv * h1, axis=0, keepdims=True)
        dw_ref[2:3, :] = jnp.sum(dconv * hh, axis=0, keepdims=True)

    shape = jax.ShapeDtypeStruct((t, d), BF16)
    return pl.pallas_call(
        body, name=name, grid=(nb,),
        in_specs=[col(0), col(nb), col(2 * nb), wspec, col(0)],
        out_specs=[col(0), col(0), col(0), wspec],
        out_shape=[shape, shape, shape, jax.ShapeDtypeStruct((CONV_WIDTH, d), F32)],
        compiler_params=_params("parallel"),
    )(bcu, bcu, bcu, w, dy)


def _gla_chunk(q, k, lg, scale):
    c = GLA_CHUNK
    rows = lax.broadcasted_iota(jnp.int32, (c, c), 0)
    cols = lax.broadcasted_iota(jnp.int32, (c, c), 1)
    causal = rows >= cols
    tril = jnp.where(causal, 1.0, 0.0).astype(BF16)
    q = q * scale
    cum = _dot_exact_lhs(tril, lg)
    last = cum[c - 1:c, :]
    eq = jnp.exp(cum)
    el = jnp.exp(last - cum)
    return causal, tril, q, k, cum, lg, last, eq, el


def _gla_sub_blocks(q, k, cum, lg):
    key_row = lax.broadcasted_iota(jnp.int32, (GLA_CHUNK, 1), 0)
    out = []
    for lo in range(0, GLA_CHUNK, GLA_SUB):
        hi = lo + GLA_SUB
        ref = cum[lo:lo + 1, :] - lg[lo:lo + 1, :]
        eq = jnp.exp(cum[lo:hi] - ref)
        ek = jnp.where(key_row < hi, jnp.exp(ref - cum), 0.0)
        out.append((slice(lo, hi), eq, ek, q[lo:hi] * eq, k * ek))
    return out


def _gla_scores(subs, causal):
    return jnp.where(causal, jnp.concatenate([_dot(qt, kt, _NT) for _, _, _, qt, kt in subs], axis=0), 0.0)


def gla_fwd(proj, lg, *, name):
    t, d3 = proj.shape
    d = d3 // 3
    dk, dv = d // 2 // GLA_HEADS, d // GLA_HEADS
    assert dk == LANES and dv == 2 * LANES
    c = GLA_CHUNK
    nc = t // c
    scale = dk ** -0.5
    nh = GLA_HEADS

    def body(q_ref, k_ref, v_ref, lg_ref, o_ref, st_out_ref, st_ref):
        @pl.when(pl.program_id(0) == 0)
        def _():
            st_ref[...] = jnp.zeros_like(st_ref)

        for h in range(nh):
            kcols, vcols = slice(h * dk, (h + 1) * dk), slice(h * dv, (h + 1) * dv)
            causal, _, q, k, cum, lg, last, eq, el = _gla_chunk(q_ref[:, kcols], k_ref[:, kcols], lg_ref[:, kcols], scale)
            v = v_ref[:, vcols]
            st = st_ref[h]
            st_out_ref[h] = st
            scores = _gla_scores(_gla_sub_blocks(q, k, cum, lg), causal)
            o_ref[:, vcols] = _dot(q * eq, st, _NT) + _dot(scores, v)
            st_ref[h] = st * jnp.exp(last) + _dot(v, k * el, _TN)

    half, full = pl.BlockSpec((c, d // 2), lambda i: (i, 0)), pl.BlockSpec((c, d), lambda i: (i, 0))
    return pl.pallas_call(
        body, name=name, grid=(nc,),
        in_specs=[half, pl.BlockSpec((c, d // 2), lambda i: (i, 1)), pl.BlockSpec((c, d), lambda i: (i, 1)), half],
        out_specs=[full, pl.BlockSpec((nh, None, dv, dk), lambda i: (0, i, 0, 0))],
        out_shape=[jax.ShapeDtypeStruct((t, d), F32), jax.ShapeDtypeStruct((nh, nc, dv, dk), F32)],
        scratch_shapes=[pltpu.VMEM((nh, dv, dk), F32)], compiler_params=_params("arbitrary"),
    )(proj, proj, proj, lg)


def gla_bwd(proj, lg, states, do, *, name):
    t, d3 = proj.shape
    d = d3 // 3
    dk, dv = d // 2 // GLA_HEADS, d // GLA_HEADS
    c = GLA_CHUNK
    nc = t // c
    scale = dk ** -0.5
    nh = GLA_HEADS

    def body(q_ref, k_ref, v_ref, lg_ref, st_ref, do_ref, dq_ref, dk_ref, dv_ref, dzg_ref, dbg_ref, dst_ref):
        @pl.when(pl.program_id(0) == 0)
        def _():
            dst_ref[...] = jnp.zeros_like(dst_ref)
            dbg_ref[...] = jnp.zeros_like(dbg_ref)

        for h in range(nh):
            kcols, vcols = slice(h * dk, (h + 1) * dk), slice(h * dv, (h + 1) * dv)
            causal, tril, q, k, cum, lg, last, eq, el = _gla_chunk(q_ref[:, kcols], k_ref[:, kcols], lg_ref[:, kcols], scale)
            v, st, dov, dst = v_ref[:, vcols], st_ref[h], do_ref[:, vcols], dst_ref[h]
            subs = _gla_sub_blocks(q, k, cum, lg)
            qt, kh = q * eq, k * el
            scores = _gla_scores(subs, causal)
            dscores = jnp.where(causal, _dot(dov, v, _NT), 0.0)
            dq_parts = []
            dkh = _dot(v, dst)
            dk_h = dkh * el
            for rows, eq_sub, ek_sub, qt_sub, kt_sub in subs:
                dq_parts.append(_dot(dscores[rows], kt_sub) * eq_sub)
                dk_h = dk_h + _dot(dscores[rows], qt_sub, _TN) * ek_sub
            dq = _dot(dov, st) * eq + jnp.concatenate(dq_parts, axis=0)
            dv_ref[:, vcols] = (_dot(scores, dov, _TN) + _dot(kh, dst, _NT)).astype(dv_ref.dtype)
            dq_ref[:, kcols] = (dq * scale).astype(dq_ref.dtype)
            dk_ref[:, kcols] = dk_h.astype(dk_ref.dtype)
            e_last = jnp.exp(last)
            dlast = jnp.sum(kh * dkh, axis=0, keepdims=True) + e_last * jnp.sum(dst * st, axis=0, keepdims=True)
            dcum = q * dq - k * dk_h
            dlg = _dot_exact_lhs(tril, dcum, _TN) + dlast
            dzg = dlg * (1.0 - jnp.exp(lg * GLA_GATE_NORMALIZER)) / GLA_GATE_NORMALIZER
            dzg_ref[:, kcols] = dzg.astype(dzg_ref.dtype)
            dbg_ref[:, kcols] += jnp.sum(dzg, axis=0, keepdims=True)
            dst_ref[h] = dst * e_last + _dot(dov, qt, _TN)

    rev = lambda i: nc - 1 - i
    half, full = pl.BlockSpec((c, d // 2), lambda i: (rev(i), 0)), pl.BlockSpec((c, d), lambda i: (rev(i), 0))
    half_shape = jax.ShapeDtypeStruct((t, d // 2), BF16)
    return pl.pallas_call(
        body, name=name, grid=(nc,),
        in_specs=[half, pl.BlockSpec((c, d // 2), lambda i: (rev(i), 1)), pl.BlockSpec((c, d), lambda i: (rev(i), 1)), half,
                  pl.BlockSpec((nh, None, dv, dk), lambda i: (0, rev(i), 0, 0)), full],
        out_specs=[half, half, full, half, pl.BlockSpec((1, d // 2), lambda i: (0, 0))],
        out_shape=[half_shape, half_shape, jax.ShapeDtypeStruct((t, d), BF16), half_shape,
                   jax.ShapeDtypeStruct((1, d // 2), F32)],
        scratch_shapes=[pltpu.VMEM((nh, dv, dk), F32)], compiler_params=_params("arbitrary"),
    )(proj, proj, proj, lg, states, do)


def gla_post_fwd(o, proj, head_norm, *, name, tm=ROW_BLOCK):
    t, d = o.shape
    dv = d // GLA_HEADS
    tm = min(tm, t)

    def body(o_ref, g_ref, hn_ref, y_ref):
        for h in range(GLA_HEADS):
            sl = slice(h * dv, (h + 1) * dv)
            ov, gv = o_ref[:, sl], g_ref[:, sl]
            r = lax.rsqrt(jnp.mean(ov * ov, axis=-1, keepdims=True) + RMS_EPS)
            y_ref[:, sl] = ((ov * r * hn_ref[:, sl]) * (gv * _sigmoid(gv))).astype(y_ref.dtype)

    row = pl.BlockSpec((tm, d), lambda i: (i, 0))
    return pl.pallas_call(
        body, name=name, grid=(t // tm,),
        in_specs=[row, pl.BlockSpec((tm, d), lambda i: (i, 2)), pl.BlockSpec((1, d), lambda i: (0, 0))],
        out_specs=row, out_shape=jax.ShapeDtypeStruct((t, d), BF16), compiler_params=_params("parallel"),
    )(o, proj, head_norm)


def gla_post_bwd(o, proj, head_norm, dy, *, name, tm=ROW_BLOCK):
    t, d = o.shape
    dv = d // GLA_HEADS
    tm = min(tm, t)

    def body(o_ref, g_ref, hn_ref, dy_ref, do_ref, dg_ref, dhn_ref):
        @pl.when(pl.program_id(0) == 0)
        def _():
            dhn_ref[...] = jnp.zeros_like(dhn_ref)

        for h in range(GLA_HEADS):
            sl = slice(h * dv, (h + 1) * dv)
            ov, gv, dyv, hn = o_ref[:, sl], g_ref[:, sl], dy_ref[:, sl], hn_ref[:, sl]
            r = lax.rsqrt(jnp.mean(ov * ov, axis=-1, keepdims=True) + RMS_EPS)
            sg = _sigmoid(gv)
            silu = gv * sg
            on = ov * r * hn
            dg_ref[:, sl] = (dyv * on * (sg * (1.0 + gv * (1.0 - sg)))).astype(dg_ref.dtype)
            don = dyv * silu
            u = don * hn
            do_ref[:, sl] = (r * u - ov * (r * r * r * jnp.mean(u * ov, axis=-1, keepdims=True))).astype(do_ref.dtype)
            dhn_ref[:, sl] += jnp.sum(don * ov * r, axis=0, keepdims=True)

    row = pl.BlockSpec((tm, d), lambda i: (i, 0))
    vec = pl.BlockSpec((1, d), lambda i: (0, 0))
    shape = jax.ShapeDtypeStruct((t, d), BF16)
    return pl.pallas_call(
        body, name=name, grid=(t // tm,),
        in_specs=[row, pl.BlockSpec((tm, d), lambda i: (i, 2)), vec, row],
        out_specs=[row, row, vec], out_shape=[shape, shape, jax.ShapeDtypeStruct((1, d), F32)],
        compiler_params=_params("arbitrary"),
    )(o, proj, head_norm, dy)


def _ffn_fwd(h, gains, w_up, w_down, tag):
    xn = rms_fwd(h, gains[2], name=f"{tag}_ffn_norm", out_dtype=BF16)
    act = matmul(xn, w_up, mode='nn', epi='relu2', out_dtype=BF16, name=f"{tag}_ffn_up")
    f = matmul(act, w_down, mode='nn', name=f"{tag}_ffn_down")
    h_out = rms_fwd(f, gains[3], res=h, name=f"{tag}_ffn_out")
    return h_out, (h, xn, act, f)


def _ffn_bwd(dh, saved, gains, w_up, w_down, tag):
    h, xn, act, f = saved
    df, dg3 = rms_bwd(f, gains[3], dh, out_dtype=BF16, name=f"{tag}_ffn_out_bwd")
    du = matmul(df, w_down, mode='nt', epi='mul2sqrt', extra=act, out_dtype=BF16, name=f"{tag}_ffn_da")
    dw_down = matmul(act, df, mode='tn', out_dtype=BF16, name=f"{tag}_ffn_dwdown")
    dw_up = matmul(xn, du, mode='tn', out_dtype=BF16, name=f"{tag}_ffn_dwup")
    dxn = matmul(du, w_up, mode='nt', name=f"{tag}_ffn_dxn")
    dh_in, dg2 = rms_bwd(h, gains[2], dxn, dres=dh, name=f"{tag}_ffn_norm_bwd")
    return dh_in, dg2, dg3, dw_up, dw_down


def _sb_layer_fwd(xn, w, j, tag, comm=None):
    qkv = matmul(xn, w['sb_w_qkv'][j], mode='nn', out_dtype=BF16, name=f"{tag}_qkv")
    if comm is None:
        o, tot = sb_fwd(qkv, name=f"{tag}_sb")
    else:
        o, tot, gathered = sb_fwd(qkv, name=f"{tag}_sb", gather=comm.rest_payload)
        comm.on_gathered(gathered)
    m = matmul(o, w['sb_w_o'][j], mode='nn', name=f"{tag}_wo")
    return m, (qkv, o, tot)


def _sb_layer_bwd(dm, xn, saved, w, j, tag, comm=None, grads=None):
    qkv, o, tot = saved
    do = matmul(dm, w['sb_w_o'][j], mode='nt', out_dtype=BF16, name=f"{tag}_do")
    dw_o = matmul(o, dm, mode='tn', out_dtype=BF16, name=f"{tag}_dwo")
    if comm is None:
        dq, dk, dv = sb_bwd(qkv, tot, do, name=f"{tag}_sb_bwd")
    else:
        parts = comm.rest_parts({**grads, ('sb_w_o', j): dw_o})
        dq, dk, dv, received = sb_bwd(qkv, tot, do, name=f"{tag}_sb_bwd", exchange=parts)
        comm.on_received(received)
    dqkv = jnp.concatenate([dq, dk, dv], axis=1)
    dw_qkv = matmul(xn, dqkv, mode='tn', out_dtype=BF16, name=f"{tag}_dwqkv")
    dxn = matmul(dqkv, w['sb_w_qkv'][j], mode='nt', name=f"{tag}_dxn")
    return dxn, {('sb_w_qkv', j): dw_qkv, ('sb_w_o', j): dw_o}


def _conv_layer_fwd(xn, w, j, tag):
    bcu = matmul(xn, w['conv_w_in'][j], mode='nn', name=f"{tag}_in")
    y = conv_fwd(bcu, w['conv_w'][j], name=f"{tag}_conv")
    m = matmul(y, w['conv_w_out'][j], mode='nn', name=f"{tag}_out")
    return m, (bcu, y)


def _conv_layer_bwd(dm, xn, saved, w, j, tag):
    bcu, y = saved
    dy = matmul(dm, w['conv_w_out'][j], mode='nt', name=f"{tag}_dy")
    dw_out = matmul(y, dm, mode='tn', out_dtype=BF16, name=f"{tag}_dwout")
    db, dc, du, dw_conv = conv_bwd(bcu, w['conv_w'][j], dy, name=f"{tag}_conv_bwd")
    dbcu = jnp.concatenate([db, dc, du], axis=1)
    dw_in = matmul(xn, dbcu, mode='tn', out_dtype=BF16, name=f"{tag}_dwin")
    dxn = matmul(dbcu, w['conv_w_in'][j], mode='nt', name=f"{tag}_dxn")
    return dxn, {('conv_w_in', j): dw_in, ('conv_w', j): dw_conv, ('conv_w_out', j): dw_out}


def _gla_split(w_in, w_gate_up):
    d = w_in.shape[0]
    w_main = w_in[:, :3 * d]
    w_a = jnp.pad(w_in[:, 3 * d:], ((0, 0), (0, LANES - GLA_GATE_RANK)))
    w_gu = jnp.pad(w_gate_up, ((0, LANES - GLA_GATE_RANK), (0, 0)))
    return w_main, w_a, w_gu


def _gla_layer_fwd(xn, w, j, tag):
    w_main, w_a, w_gu = _gla_split(w['gla_w_in'][j], w['gla_w_gate_up'][j])
    proj = matmul(xn, w_main, mode='nn', name=f"{tag}_in")
    a_low = matmul(xn, w_a, mode='nn', out_dtype=BF16, name=f"{tag}_alow")
    lg = matmul(a_low, w_gu, mode='nn', epi='logsig16', extra=w['gla_b_gate'][j][None, :], name=f"{tag}_gate")
    o, states = gla_fwd(proj, lg, name=f"{tag}_gla")
    hn = w['gla_head_norm'][j].reshape(1, -1)
    y = gla_post_fwd(o, proj, hn, name=f"{tag}_post")
    m = matmul(y, w['gla_w_o'][j], mode='nn', name=f"{tag}_wo")
    return m, (proj, a_low, lg, o, states, y)


def _gla_layer_bwd(dm, xn, saved, w, j, tag):
    proj, a_low, lg, o, states, y = saved
    w_main, w_a, w_gu = _gla_split(w['gla_w_in'][j], w['gla_w_gate_up'][j])
    hn = w['gla_head_norm'][j].reshape(1, -1)
    dy = matmul(dm, w['gla_w_o'][j], mode='nt', name=f"{tag}_dy")
    dw_o = matmul(y, dm, mode='tn', out_dtype=BF16, name=f"{tag}_dwo")
    do, dg, dhn = gla_post_bwd(o, proj, hn, dy, name=f"{tag}_post_bwd")
    dq, dk, dv, dzg, dbg = gla_bwd(proj, lg, states, do, name=f"{tag}_gla_bwd")
    da_low = matmul(dzg, w_gu, mode='nt', out_dtype=BF16, name=f"{tag}_dalow")
    dw_gu = matmul(a_low, dzg, mode='tn', out_dtype=BF16, name=f"{tag}_dwgu")[:GLA_GATE_RANK]
    dproj = jnp.concatenate([dq, dk, dv, dg], axis=1)
    dw_main = matmul(xn, dproj, mode='tn', out_dtype=BF16, name=f"{tag}_dwin")
    dw_a = matmul(xn, da_low, mode='tn', out_dtype=BF16, name=f"{tag}_dwa")[:, :GLA_GATE_RANK]
    dxn_a = matmul(da_low, w_a, mode='nt', name=f"{tag}_dxn_a")
    dxn = matmul(dproj, w_main, mode='nt', epi='add', extra=dxn_a, name=f"{tag}_dxn")
    grads = {('gla_w_in', j): jnp.concatenate([dw_main, dw_a], axis=1), ('gla_w_gate_up', j): dw_gu,
             ('gla_b_gate', j): dbg[0], ('gla_head_norm', j): dhn.reshape(w['gla_head_norm'][j].shape),
             ('gla_w_o', j): dw_o}
    return dxn, grads


_MIXERS = ((_sb_layer_fwd, _sb_layer_bwd), (_conv_layer_fwd, _conv_layer_bwd), (_gla_layer_fwd, _gla_layer_bwd))


def local_step(x, w, target, comm=None):
    depth = len(w['norm_gains'])
    h = x
    tape = []
    for i in range(depth):
        kind, j = i % 3, i // 3
        tag = f"l{i}"
        extra = {'comm': comm} if (comm is not None and i == 0) else {}
        gains = [w['norm_gains'][i][s][None, :] for s in range(4)]
        xn = rms_fwd(h, gains[0], name=f"{tag}_mix_norm", out_dtype=BF16)
        m, saved = _MIXERS[kind][0](xn, w, j, tag, **extra)
        h_mid = rms_fwd(m, gains[1], res=h, name=f"{tag}_mix_out")
        h_out, ffn_saved = _ffn_fwd(h_mid, gains, w['ffn_w_up'][i], w['ffn_w_down'][i], tag)
        tape.append((h, xn, m, saved, ffn_saved, gains))
        h = h_out
    loss, dh = loss_head(h, target, name="loss_head")

    grads = {}
    for i in reversed(range(depth)):
        kind, j = i % 3, i // 3
        tag = f"l{i}"
        h_in, xn, m, saved, ffn_saved, gains = tape[i]
        dg = [None] * 4
        dh, dg[2], dg[3], grads[('ffn_w_up', i)], grads[('ffn_w_down', i)] = _ffn_bwd(
            dh, ffn_saved, gains, w['ffn_w_up'][i], w['ffn_w_down'][i], tag)
        dm, dg[1] = rms_bwd(m, gains[1], dh, out_dtype=BF16, name=f"{tag}_mix_out_bwd")
        extra = {'comm': comm, 'grads': grads} if (comm is not None and i == 0) else {}
        dxn, g = _MIXERS[kind][1](dm, xn, saved, w, j, tag, **extra)
        grads.update(g)
        dh, dg[0] = rms_bwd(h_in, gains[0], dxn, dres=dh, name=f"{tag}_mix_norm_bwd")
        grads[('norm_gains', i)] = jnp.concatenate(dg, axis=0)
    return loss, dh, grads


def _segments(keys, shard_shapes, f32_as_pairs):
    segs, row = {}, 0
    for name, lo, hi in keys:
        n = (hi - lo) * math.prod(shard_shapes[name][1:]) * (2 if f32_as_pairs and name in F32_PAYLOAD else 1)
        nrows = -(-n // (PACK_COLS * PACK_ROW_ALIGN)) * PACK_ROW_ALIGN
        segs[(name, lo, hi)] = (row, nrows, n)
        row += nrows
    return segs, -(-row // PACK_ROW_BLOCK) * PACK_ROW_BLOCK


def _pack(parts, segs, total_rows, dtype):
    pieces, row = [], 0
    for key in segs:
        _, nrows, n = segs[key]
        p = parts[key].astype(dtype)
        lead = p.shape[:-1]
        if nrows * PACK_COLS > n:
            p = jnp.pad(p, [(0, 0)] * len(lead) + [(0, nrows * PACK_COLS - n)])
        pieces.append(p.reshape(lead + (nrows, PACK_COLS)))
        row += nrows
    if total_rows > row:
        pieces.append(jnp.zeros(lead + (total_rows - row, PACK_COLS), dtype))
    return jnp.concatenate(pieces, axis=-2)


def _unpack(buf, seg):
    first, nrows, n = seg
    piece = buf[..., first:first + nrows, :]
    return piece.reshape(piece.shape[:-2] + (nrows * PACK_COLS,))[..., :n]


def _unshard(gathered, axis):
    moved = jnp.moveaxis(gathered, 0, axis)
    shape = moved.shape
    return moved.reshape(shape[:axis] + (shape[axis] * shape[axis + 1],) + shape[axis + 2:])


def _shard_split(full, axis):
    shape = full.shape
    cut = full.reshape(shape[:axis] + (N_DEV, shape[axis] // N_DEV) + shape[axis + 1:])
    return jnp.moveaxis(cut, axis, 0)


def _mesh_position():
    return lax.axis_index("x"), lax.axis_index("y"), lax.axis_index("c")


def _comm_scratch():
    return [pltpu.SemaphoreType.DMA((N_DEV - 1,)), pltpu.SemaphoreType.DMA((N_DEV - 1,)), pltpu.SemaphoreType.DMA]


def _gather_plan(x_ref, out_ref, send_sems, recv_sems, local_sem):
    rows = x_ref.shape[0]
    x, y, c = _mesh_position()
    me, sibling = (x, y, c), (x, y, 1 - c)
    chips = [(1 - x, y), (x, 1 - y), (1 - x, 1 - y)]

    def block(px, py, pc):
        return out_ref.at[pl.ds((4 * px + 2 * py + pc) * rows, rows), :]

    def copy(k, blk, to, src=None):
        return pltpu.make_async_remote_copy(
            src_ref=block(*blk) if src is None else src, dst_ref=block(*blk),
            send_sem=send_sems.at[k], recv_sem=recv_sems.at[k],
            device_id=to, device_id_type=pl.DeviceIdType.MESH)

    mine = pltpu.make_async_copy(x_ref, block(*me), local_sem)
    first = [copy(0, me, sibling, src=x_ref)]
    first += [copy(1 + j, me, (*chip, c), src=x_ref) for j, chip in enumerate(chips)]
    passed = [copy(4 + j, (*chip, c), sibling) for j, chip in enumerate(chips)]

    def start():
        mine.start()
        for cp in first:
            cp.start()

    def forward():
        for j, chip in enumerate(chips):
            copy(1 + j, (*chip, c), me).wait_recv()
            passed[j].start()

    def finish():
        copy(0, sibling, me).wait_recv()
        for j, chip in enumerate(chips):
            copy(4 + j, (*chip, 1 - c), me).wait_recv()
        for cp in first + passed:
            cp.wait_send()
        mine.wait()

    return start, forward, finish


def _exchange_plan(in_ref, out_ref, send_sems, recv_sems, local_sem):
    x, y, c = _mesh_position()
    my_id = 4 * x + 2 * y + c
    mine = pltpu.make_async_copy(in_ref.at[my_id], out_ref.at[my_id], local_sem)

    def copy(k, receive):
        px = 1 - x if k & 4 else x
        py = 1 - y if k & 2 else y
        pc = 1 - c if k & 1 else c
        peer_id = 4 * px + 2 * py + pc
        return pltpu.make_async_remote_copy(
            src_ref=in_ref.at[peer_id], dst_ref=out_ref.at[peer_id if receive else my_id],
            send_sem=send_sems.at[k - 1], recv_sem=recv_sems.at[k - 1],
            device_id=(px, py, pc), device_id_type=pl.DeviceIdType.MESH)

    def start():
        mine.start()
        for k in range(1, N_DEV):
            copy(k, False).start()

    def finish():
        for k in range(1, N_DEV):
            copy(k, True).wait_recv()
        for k in range(1, N_DEV):
            copy(k, False).wait_send()
        mine.wait()

    return start, finish


def all_gather(shard, *, name):
    rows, cols = shard.shape

    def body(x_ref, out_ref, send_sems, recv_sems, local_sem):
        start, forward, finish = _gather_plan(x_ref, out_ref, send_sems, recv_sems, local_sem)
        start()
        forward()
        finish()

    return pl.pallas_call(
        body, name=name, out_shape=jax.ShapeDtypeStruct((N_DEV * rows, cols), shard.dtype),
        in_specs=[pl.BlockSpec(memory_space=pl.ANY)], out_specs=pl.BlockSpec(memory_space=pl.ANY),
        scratch_shapes=_comm_scratch(),
    )(shard)


def exchange_shards(parts, *, name):
    def body(in_ref, out_ref, send_sems, recv_sems, local_sem):
        start, finish = _exchange_plan(in_ref, out_ref, send_sems, recv_sems, local_sem)
        start()
        finish()

    return pl.pallas_call(
        body, name=name, out_shape=jax.ShapeDtypeStruct(parts.shape, parts.dtype),
        in_specs=[pl.BlockSpec(memory_space=pl.ANY)], out_specs=pl.BlockSpec(memory_space=pl.ANY),
        scratch_shapes=_comm_scratch(),
    )(parts)


def adamw(parts, w, m, v, *, name):
    _, rows, cols = parts.shape
    tr = PACK_ROW_BLOCK
    c1 = 1.0 - ADAM_B1 ** ADAM_STEP
    c2 = 1.0 - ADAM_B2 ** ADAM_STEP

    def body(p_ref, w_ref, m_ref, v_ref, g_ref, d_ref, nm_ref, nv_ref):
        g = p_ref[0].astype(F32)
        for s in range(1, N_DEV):
            g = g + p_ref[s].astype(F32)
        nm = ADAM_B1 * m_ref[...] + (1.0 - ADAM_B1) * g
        nv = ADAM_B2 * v_ref[...] + (1.0 - ADAM_B2) * jnp.square(g)
        m_hat = nm / c1
        v_hat = nv / c2
        g_ref[...] = g
        d_ref[...] = -ADAM_LR * (m_hat / (jnp.sqrt(v_hat) + ADAM_EPS) + ADAM_WD * w_ref[...])
        nm_ref[...] = nm
        nv_ref[...] = nv

    row = pl.BlockSpec((tr, cols), lambda i: (i, 0))
    shape = jax.ShapeDtypeStruct((rows, cols), F32)
    return pl.pallas_call(
        body, name=name, grid=(rows // tr,),
        in_specs=[pl.BlockSpec((N_DEV, tr, cols), lambda i: (0, i, 0)), row, row, row],
        out_specs=[row, row, row, row], out_shape=[shape, shape, shape, shape],
        compiler_params=_params("parallel"),
    )(parts, w, m, v)


def kernel(x, norm_gains, sb_w_qkv, sb_w_o, conv_w_in, conv_w, conv_w_out, gla_w_in, gla_w_gate_up, gla_b_gate, gla_head_norm, gla_w_o, ffn_w_up, ffn_w_down, loss_target, m_norm_gains, m_sb_w_qkv, m_sb_w_o, m_conv_w_in, m_conv_w, m_conv_w_out, m_gla_w_in, m_gla_w_gate_up, m_gla_b_gate, m_gla_head_norm, m_gla_w_o, m_ffn_w_up, m_ffn_w_down, v_norm_gains, v_sb_w_qkv, v_sb_w_o, v_conv_w_in, v_conv_w, v_conv_w_out, v_gla_w_in, v_gla_w_gate_up, v_gla_b_gate, v_gla_head_norm, v_gla_w_o, v_ffn_w_up, v_ffn_w_down):
    shards = dict(zip(WEIGHTS, (norm_gains, sb_w_qkv, sb_w_o, conv_w_in, conv_w, conv_w_out, gla_w_in,
                                gla_w_gate_up, gla_b_gate, gla_head_norm, gla_w_o, ffn_w_up, ffn_w_down)))
    moments_m = dict(zip(WEIGHTS, (m_norm_gains, m_sb_w_qkv, m_sb_w_o, m_conv_w_in, m_conv_w, m_conv_w_out,
                                   m_gla_w_in, m_gla_w_gate_up, m_gla_b_gate, m_gla_head_norm, m_gla_w_o,
                                   m_ffn_w_up, m_ffn_w_down)))
    moments_v = dict(zip(WEIGHTS, (v_norm_gains, v_sb_w_qkv, v_sb_w_o, v_conv_w_in, v_conv_w, v_conv_w_out,
                                   v_gla_w_in, v_gla_w_gate_up, v_gla_b_gate, v_gla_head_norm, v_gla_w_o,
                                   v_ffn_w_up, v_ffn_w_down)))
    shard_shapes = {n: a.shape for n, a in shards.items()}

    alone = [('norm_gains', 0, 1), ('sb_w_qkv', 0, 1)]
    rest = [(n, 1 if (n, 0, 1) in alone else 0, shard_shapes[n][0]) for n in WEIGHTS]
    groups = [alone, [piece for piece in rest if piece[1] < piece[2]]]

    def payload(group):
        segs, rows = _segments(group, shard_shapes, True)
        flat = {(n, lo, hi): (lax.bitcast_convert_type(shards[n][lo:hi], BF16) if n in F32_PAYLOAD
                              else shards[n][lo:hi].astype(BF16)).reshape(-1) for n, lo, hi in group}
        return segs, _pack(flat, segs, rows, BF16)

    whole = {n: [None] * shard_shapes[n][0] for n in WEIGHTS}

    def take_gathered(segs, gathered):
        gathered = gathered.reshape(N_DEV, -1, PACK_COLS)
        for n, lo, hi in segs:
            piece = _unpack(gathered, segs[(n, lo, hi)])
            if n in F32_PAYLOAD:
                piece = lax.bitcast_convert_type(piece.reshape(N_DEV, -1, 2), F32)
            full = _unshard(piece.reshape((N_DEV, hi - lo) + shard_shapes[n][1:]), SHARD_AXIS[n])
            for j in range(lo, hi):
                whole[n][j] = full[j - lo]

    segs0, payload0 = payload(groups[0])
    segs1, payload1 = payload(groups[1])
    take_gathered(segs0, all_gather(payload0, name="weights_all_gather"))

    gsegs = [_segments(group, shard_shapes, False) for group in groups]
    received = [None, None]

    def parts(g, grads):
        segs, rows = gsegs[g]
        flat = {(n, lo, hi): _shard_split(jnp.stack([grads[(n, j)] for j in range(lo, hi)]), SHARD_AXIS[n])
                .reshape(N_DEV, -1) for n, lo, hi in segs}
        return _pack(flat, segs, rows, BF16)

    def on_received(buf):
        received[1] = buf

    comm = types.SimpleNamespace(rest_payload=payload1, on_gathered=functools.partial(take_gathered, segs1),
                                 rest_parts=functools.partial(parts, 1), on_received=on_received)

    loss, grad_x, grads = local_step(x[0], whole, loss_target[0], comm)
    loss = lax.psum(loss[0, 0], ("x", "y", "c"))
    received[0] = exchange_shards(parts(0, grads), name="grads_exchange")

    results = {n: [[] for _ in range(4)] for n in WEIGHTS}
    for g in range(2):
        segs, rows = gsegs[g]

        def packed(source):
            return _pack({(n, lo, hi): source[n][lo:hi].reshape(-1) for n, lo, hi in segs}, segs, rows, F32)

        outs = adamw(received[g], packed(shards), packed(moments_m), packed(moments_v), name=f"adamw{g}")
        for n, lo, hi in segs:
            for o, buf in enumerate(outs):
                results[n][o].append(_unpack(buf, segs[(n, lo, hi)]).reshape((hi - lo,) + shard_shapes[n][1:]))
    whole_out = [[r[0] if len(r) == 1 else jnp.concatenate(r, axis=0) for r in (results[n][o] for n in WEIGHTS)]
                 for o in range(4)]
    return (loss, grad_x[None], *whole_out[0], *whole_out[1], *whole_out[2], *whole_out[3])
```

```python
import functools
import math
import types

import jax
import jax.numpy as jnp
from jax import lax
from jax.experimental import pallas as pl
from jax.experimental.pallas import tpu as pltpu

F32 = jnp.float32
BF16 = jnp.bfloat16

N_DEV = 8
SB_HEADS = 16
GLA_HEADS = 4
GLA_CHUNK = 64
GLA_SUB = 16
GLA_GATE_RANK = 16
GLA_GATE_NORMALIZER = 16.0
CONV_WIDTH = 3
RMS_EPS = 1e-6
ADAM_LR = 0.001
ADAM_B1 = 0.9
ADAM_B2 = 0.999
ADAM_EPS = 1e-08
ADAM_WD = 0.01
ADAM_STEP = 10

LANES = 128
SB_BLOCK = 256
SB_FWD_Q_BLOCK = 1024
SB_BWD_Q_BLOCK = 1024
VMEM_LIMIT_BYTES = 56 * 1024 * 1024
MM_TM, MM_TN, MM_TK = 1024, 1024, 1024
ROW_BLOCK = 512
PACK_COLS = 1024
PACK_ROW_ALIGN = 16
PACK_ROW_BLOCK = 128

WEIGHTS = ['norm_gains', 'sb_w_qkv', 'sb_w_o', 'conv_w_in', 'conv_w', 'conv_w_out', 'gla_w_in',
           'gla_w_gate_up', 'gla_b_gate', 'gla_head_norm', 'gla_w_o', 'ffn_w_up', 'ffn_w_down']
SHARD_AXIS = {'norm_gains': 2, 'sb_w_qkv': 2, 'sb_w_o': 1, 'conv_w_in': 2, 'conv_w': 2, 'conv_w_out': 1,
              'gla_w_in': 2, 'gla_w_gate_up': 2, 'gla_b_gate': 1, 'gla_head_norm': 2, 'gla_w_o': 1,
              'ffn_w_up': 2, 'ffn_w_down': 1}
F32_PAYLOAD = ('norm_gains', 'conv_w', 'gla_b_gate', 'gla_head_norm')

_NN = (((1,), (0,)), ((), ()))
_NT = (((1,), (1,)), ((), ()))
_TN = (((0,), (0,)), ((), ()))
_DIMS = {'nn': _NN, 'nt': _NT, 'tn': _TN}


def _params(*semantics):
    return pltpu.CompilerParams(dimension_semantics=semantics, vmem_limit_bytes=VMEM_LIMIT_BYTES)


def _dot(a, b, dims=_NN):
    return lax.dot_general(a.astype(BF16), b.astype(BF16), dims, preferred_element_type=F32)


def _split_hi_lo(x):
    hi = x.astype(BF16)
    lo = (x - hi.astype(F32)).astype(BF16)
    return hi, lo


def _dot_exact_rhs(x, ones_mat, dims=_NN):
    hi, lo = _split_hi_lo(x)
    return (lax.dot_general(hi, ones_mat, dims, preferred_element_type=F32)
            + lax.dot_general(lo, ones_mat, dims, preferred_element_type=F32))


def _dot_exact_lhs(ones_mat, x, dims=_NN):
    hi, lo = _split_hi_lo(x)
    return (lax.dot_general(ones_mat, hi, dims, preferred_element_type=F32)
            + lax.dot_general(ones_mat, lo, dims, preferred_element_type=F32))


def _log_sigmoid(z):
    return jnp.minimum(z, 0.0) - jnp.log(1.0 + jnp.exp(-jnp.abs(z)))


def _sigmoid(z):
    return 1.0 / (1.0 + jnp.exp(-z))


def matmul(a, b, *, mode, name, out_dtype=F32, epi=None, extra=None, tm=MM_TM, tn=MM_TN, tk=MM_TK):
    if mode == 'nn':
        (m, k), (k2, n) = a.shape, b.shape
    elif mode == 'nt':
        (m, k), (n, k2) = a.shape, b.shape
    else:
        (k, m), (k2, n) = a.shape, b.shape
    assert k == k2, (a.shape, b.shape, mode)
    tm, tn, tk = min(tm, m), min(tn, n), min(tk, k)
    assert m % tm == 0 and n % tn == 0 and k % tk == 0, (a.shape, b.shape, mode)
    nk = k // tk
    if mode == 'tn':
        a_spec = pl.BlockSpec((tk, tm), lambda i, j, kk: (kk, i))
    else:
        a_spec = pl.BlockSpec((tm, tk), lambda i, j, kk: (i, kk))
    if mode == 'nt':
        b_spec = pl.BlockSpec((tn, tk), lambda i, j, kk: (j, kk))
    else:
        b_spec = pl.BlockSpec((tk, tn), lambda i, j, kk: (kk, j))
    in_specs, operands = [a_spec, b_spec], [a, b]
    if epi == 'logsig16':
        in_specs.append(pl.BlockSpec((1, tn), lambda i, j, kk: (0, j)))
        operands.append(extra)
    elif epi in ('mul2sqrt', 'add'):
        in_specs.append(pl.BlockSpec((tm, tn), lambda i, j, kk: (i, j)))
        operands.append(extra)

    n_extra = len(operands) - 2

    def body(a_ref, b_ref, *rest):
        e_ref = rest[0] if n_extra else None
        o_ref = rest[n_extra]

        def finish(r):
            if epi == 'relu2':
                r = jnp.square(jnp.maximum(r, 0.0))
            elif epi == 'mul2sqrt':
                r = r * (2.0 * jnp.sqrt(e_ref[...].astype(F32)))
            elif epi == 'add':
                r = r + e_ref[...]
            elif epi == 'logsig16':
                r = _log_sigmoid(r + e_ref[...]) / GLA_GATE_NORMALIZER
            o_ref[...] = r.astype(o_ref.dtype)

        part = _dot(a_ref[...], b_ref[...], _DIMS[mode])
        if nk == 1:
            finish(part)
        else:
            acc_ref = rest[-1]
            kk = pl.program_id(2)

            @pl.when(kk == 0)
            def _():
                acc_ref[...] = part

            @pl.when(kk > 0)
            def _():
                acc_ref[...] += part

            @pl.when(kk == nk - 1)
            def _():
                finish(acc_ref[...])

    return pl.pallas_call(
        body, name=name, grid=(m // tm, n // tn, nk), in_specs=in_specs,
        out_specs=pl.BlockSpec((tm, tn), lambda i, j, kk: (i, j)),
        out_shape=jax.ShapeDtypeStruct((m, n), out_dtype),
        scratch_shapes=[pltpu.VMEM((tm, tn), F32)] if nk > 1 else [],
        compiler_params=_params("parallel", "parallel", "arbitrary"),
    )(*operands)


def rms_fwd(x, gain, *, name, res=None, out_dtype=F32, tm=ROW_BLOCK):
    t, d = x.shape
    tm = min(tm, t)
    row = pl.BlockSpec((tm, d), lambda i: (i, 0))
    in_specs, operands = [row, pl.BlockSpec((1, d), lambda i: (0, 0))], [x, gain]
    if res is not None:
        in_specs.append(row)
        operands.append(res)

    def body(x_ref, g_ref, *rest):
        xv = x_ref[...]
        r = lax.rsqrt(jnp.mean(xv * xv, axis=-1, keepdims=True) + RMS_EPS)
        y = xv * r * g_ref[...]
        if res is not None:
            y = rest[0][...] + y
        rest[-1][...] = y.astype(out_dtype)

    return pl.pallas_call(
        body, name=name, grid=(t // tm,), in_specs=in_specs, out_specs=row,
        out_shape=jax.ShapeDtypeStruct((t, d), out_dtype), compiler_params=_params("parallel"),
    )(*operands)


def rms_bwd(x, gain, dy, *, name, dres=None, out_dtype=F32, tm=ROW_BLOCK):
    t, d = x.shape
    tm = min(tm, t)
    row = pl.BlockSpec((tm, d), lambda i: (i, 0))
    vec = pl.BlockSpec((1, d), lambda i: (0, 0))
    in_specs, operands = [row, vec, row], [x, gain, dy]
    if dres is not None:
        in_specs.append(row)
        operands.append(dres)

    def body(x_ref, g_ref, dy_ref, *rest):
        dx_ref, dg_ref = rest[-2], rest[-1]

        @pl.when(pl.program_id(0) == 0)
        def _():
            dg_ref[...] = jnp.zeros_like(dg_ref)

        xv, dyv = x_ref[...], dy_ref[...]
        r = lax.rsqrt(jnp.mean(xv * xv, axis=-1, keepdims=True) + RMS_EPS)
        u = dyv * g_ref[...]
        dx = r * u - xv * (r * r * r * jnp.mean(u * xv, axis=-1, keepdims=True))
        if dres is not None:
            dx = rest[0][...] + dx
        dx_ref[...] = dx.astype(out_dtype)
        dg_ref[...] += jnp.sum(dyv * xv * r, axis=0, keepdims=True)

    return pl.pallas_call(
        body, name=name, grid=(t // tm,), in_specs=in_specs, out_specs=[row, vec],
        out_shape=[jax.ShapeDtypeStruct((t, d), out_dtype), jax.ShapeDtypeStruct((1, d), F32)],
        compiler_params=_params("arbitrary"),
    )(*operands)


def loss_head(y, target, *, name, tm=ROW_BLOCK):
    t, d = y.shape
    tm = min(tm, t)
    nt = t // tm
    row = pl.BlockSpec((tm, d), lambda i: (i, 0))

    def body(y_ref, t_ref, loss_ref, dy_ref, acc_ref):
        i = pl.program_id(0)

        @pl.when(i == 0)
        def _():
            acc_ref[...] = jnp.zeros_like(acc_ref)

        err = y_ref[...] - t_ref[...]
        dy_ref[...] = err * (1.0 / d)
        acc_ref[...] += jnp.sum(err * err, axis=0, keepdims=True)

        @pl.when(i == nt - 1)
        def _():
            loss_ref[...] = jnp.sum(acc_ref[...], axis=1, keepdims=True) * (0.5 / d)

    return pl.pallas_call(
        body, name=name, grid=(nt,), in_specs=[row, row],
        out_specs=[pl.BlockSpec((1, 1), lambda i: (0, 0)), row],
        out_shape=[jax.ShapeDtypeStruct((1, 1), F32), jax.ShapeDtypeStruct((t, d), F32)],
        scratch_shapes=[pltpu.VMEM((1, d), F32)], compiler_params=_params("arbitrary"),
    )(y, target)


def _sb_block_iota():
    rows = lax.broadcasted_iota(jnp.int32, (SB_BLOCK, SB_BLOCK), 0)
    cols = lax.broadcasted_iota(jnp.int32, (SB_BLOCK, SB_BLOCK), 1)
    return rows, cols


def _sb_logits(q_h, k_blk, mask):
    z = _dot(q_h, k_blk, _NT)
    ls = _log_sigmoid(z)
    lm = ls - z
    if mask is not None:
        lm = jnp.where(mask, lm, 0.0)
    return ls, lm, jnp.sum(lm, axis=1, keepdims=True)


def _sb_weights(ls, lm, mask, tri_strict, later):
    suffix = _dot_exact_rhs(lm, tri_strict)
    w = jnp.exp(ls + suffix + later)
    return w if mask is None else jnp.where(mask, w, 0.0)


def _rows_from(x, lo, half):
    return x if not lo else jnp.concatenate([x[lo:half], x[half + lo:]], axis=0)


def _add_from(x, lo, half, upd):
    if not lo:
        return x + upd
    n = half - lo
    return jnp.concatenate([x[:lo], x[lo:half] + upd[:n], x[half:half + lo], x[half + lo:] + upd[n:]], axis=0)


def _on_grid_step(p, i):
    return jnp.logical_and(pl.program_id(0) == p, pl.program_id(1) == i)


def sb_fwd(qkv, *, name, gather=None):
    t, d3 = qkv.shape
    d = d3 // 3
    head_dim = d // SB_HEADS
    qb, kb_rows = SB_FWD_Q_BLOCK, SB_BLOCK
    assert 2 * head_dim == LANES and t % qb == 0
    pairs = d // LANES
    per_q = qb // kb_rows
    nq = t // qb
    scale = head_dim ** -0.5

    def body(q_ref, k_ref, v_ref, *rest):
        if gather is None:
            compute(q_ref, k_ref, v_ref, *rest)
            return
        x_ref, o_ref, tot_ref, out_ref, send_sems, recv_sems, local_sem = rest
        start, forward, finish = _gather_plan(x_ref, out_ref, send_sems, recv_sems, local_sem)
        pl.when(_on_grid_step(0, 0))(start)
        pl.when(_on_grid_step(7 * pairs // 8, 0))(forward)
        compute(q_ref, k_ref, v_ref, o_ref, tot_ref)
        pl.when(_on_grid_step(pairs - 1, nq - 1))(finish)

    def compute(q_ref, k_ref, v_ref, o_ref, tot_ref):
        qi = pl.program_id(1)
        lane = lax.broadcasted_iota(jnp.int32, (qb, LANES), 1)
        first = lane < head_dim
        q = q_ref[...] * scale
        q2 = jnp.concatenate([jnp.where(first, q, jnp.zeros_like(q)), jnp.where(first, jnp.zeros_like(q), q)], axis=0)
        rows, cols = _sb_block_iota()
        tri = jnp.where(rows > cols, 1.0, 0.0).astype(BF16)
        q_row = lax.broadcasted_iota(jnp.int32, (qb, kb_rows), 0)
        k_col = lax.broadcasted_iota(jnp.int32, (qb, kb_rows), 1)
        diagonal = [jnp.concatenate([m * kb_rows + k_col < q_row] * 2, axis=0) for m in range(per_q)]

        def step(kb, carry, mask):
            ks = pl.multiple_of(kb * SB_BLOCK, SB_BLOCK)
            k_blk = k_ref[pl.ds(ks, SB_BLOCK), :]
            v_blk = v_ref[pl.ds(ks, SB_BLOCK), :]
            acc, later = carry
            ls, lm, row = _sb_logits(q2, k_blk, mask)
            w = _sb_weights(ls, lm, mask, tri, later)
            return acc + _dot(w, v_blk), later + row

        out = (jnp.zeros((2 * qb, LANES), F32), jnp.zeros((2 * qb, 1), F32))
        for m in reversed(range(per_q)):
            out = step(per_q * qi + m, out, diagonal[m])
        acc, total = lax.fori_loop(0, per_q * qi, lambda i, carry: step(per_q * qi - 1 - i, carry, None), out)
        o_ref[...] = jnp.where(first, acc[:qb], acc[qb:]).astype(o_ref.dtype)
        tot_ref[...] = jnp.where(first, total[:qb], total[qb:])

    blk = lambda off: pl.BlockSpec((t, LANES), lambda p, i: (0, off + p))
    qblk = pl.BlockSpec((qb, LANES), lambda p, i: (i, p))
    in_specs, operands = [qblk, blk(pairs), blk(2 * pairs)], [qkv, qkv, qkv]
    out_specs = [qblk, qblk]
    out_shape = [jax.ShapeDtypeStruct((t, d), BF16), jax.ShapeDtypeStruct((t, d), F32)]
    if gather is None:
        return pl.pallas_call(
            body, name=name, grid=(pairs, nq), in_specs=in_specs, out_specs=out_specs, out_shape=out_shape,
            compiler_params=_params("parallel", "arbitrary"),
        )(*operands)
    whole = pl.BlockSpec(memory_space=pl.ANY)
    return pl.pallas_call(
        body, name=name, grid=(pairs, nq), in_specs=in_specs + [whole], out_specs=out_specs + [whole],
        out_shape=out_shape + [jax.ShapeDtypeStruct((N_DEV * gather.shape[0], gather.shape[1]), gather.dtype)],
        scratch_shapes=_comm_scratch(), compiler_params=_params("arbitrary", "arbitrary"),
    )(*operands, gather)


def sb_bwd(qkv, tot, do, *, name, exchange=None):
    t, d3 = qkv.shape
    d = d3 // 3
    head_dim = d // SB_HEADS
    pairs = d // LANES
    qb, kb_rows = SB_BWD_Q_BLOCK, SB_BLOCK
    per_q = qb // kb_rows
    nq = t // qb
    scale = head_dim ** -0.5

    def body(*refs):
        if exchange is None:
            compute(*refs)
            return
        q_ref, k_ref, v_ref, tot_ref, do_ref, in_ref, dq_ref, dk_ref, dv_ref, out_ref = refs[:10]
        dk_sum, dv_sum, send_sems, recv_sems, local_sem = refs[10:]
        start, finish = _exchange_plan(in_ref, out_ref, send_sems, recv_sems, local_sem)
        pl.when(_on_grid_step(0, 0))(start)
        compute(q_ref, k_ref, v_ref, tot_ref, do_ref, dq_ref, dk_ref, dv_ref, dk_sum, dv_sum)
        pl.when(_on_grid_step(pairs - 1, nq - 1))(finish)

    def compute(q_ref, k_ref, v_ref, tot_ref, do_ref, dq_ref, dk_ref, dv_ref, dk_sum, dv_sum):
        qi = pl.program_id(1)

        @pl.when(qi == 0)
        def _():
            dk_sum[...] = jnp.zeros_like(dk_sum)
            dv_sum[...] = jnp.zeros_like(dv_sum)

        lane = lax.broadcasted_iota(jnp.int32, (qb, LANES), 1)
        first = lane < head_dim
        q, dov, totv = q_ref[...] * scale, do_ref[...], tot_ref[...]
        second = jnp.logical_not(first)
        q2 = jnp.concatenate([jnp.where(s, q, jnp.zeros_like(q)) for s in (first, second)], axis=0)
        do2 = jnp.concatenate([jnp.where(s, dov, jnp.zeros_like(dov)) for s in (first, second)], axis=0)
        tot2 = jnp.concatenate([totv[:, 0:1], totv[:, head_dim:head_dim + 1]], axis=0)
        rows, cols = _sb_block_iota()
        tri_strict = jnp.where(rows > cols, 1.0, 0.0).astype(BF16)
        tri_before = jnp.where(rows < cols, 1.0, 0.0).astype(BF16)
        q_row = lax.broadcasted_iota(jnp.int32, (qb, kb_rows), 0)
        k_col = lax.broadcasted_iota(jnp.int32, (qb, kb_rows), 1)
        earlier = k_col < q_row

        def step(kb, carry, lo=None):
            ks = pl.multiple_of(kb * SB_BLOCK, SB_BLOCK)
            k_blk = k_ref[pl.ds(ks, SB_BLOCK), :]
            v_blk = v_ref[pl.ds(ks, SB_BLOCK), :]
            dq, seen, before = carry
            mask = None if lo is None else jnp.concatenate([earlier[:qb - lo]] * 2, axis=0)
            q_s, do_s = _rows_from(q2, lo, qb), _rows_from(do2, lo, qb)
            ls, lm, row = _sb_logits(q_s, k_blk, mask)
            seen = _add_from(seen, lo, qb, row)
            w = _sb_weights(ls, lm, mask, tri_strict, _rows_from(tot2 - seen, lo, qb))
            da = _dot(do_s, v_blk, _NT) * w
            g = _dot_exact_rhs(da, tri_before) + _rows_from(before, lo, qb)
            dz = da - jnp.exp(ls) * (da + g)
            if mask is not None:
                dz = jnp.where(mask, dz, 0.0)
            dk_sum[pl.ds(ks, SB_BLOCK), :] += _dot(dz, q_s, _TN)
            dv_sum[pl.ds(ks, SB_BLOCK), :] += _dot(w, do_s, _TN)
            return (_add_from(dq, lo, qb, _dot(dz, k_blk * scale)), seen,
                    _add_from(before, lo, qb, jnp.sum(da, axis=1, keepdims=True)))

        zero = jnp.zeros((2 * qb, LANES), F32)
        zcol = jnp.zeros((2 * qb, 1), F32)
        out = lax.fori_loop(0, per_q * qi, step, (zero, zcol, zcol))
        for m in range(per_q):
            out = step(per_q * qi + m, out, m * kb_rows)
        dq = out[0]
        dq_ref[...] = jnp.where(first, dq[:qb], dq[qb:]).astype(dq_ref.dtype)

        @pl.when(qi == nq - 1)
        def _():
            dk_ref[...] = dk_sum[...].astype(dk_ref.dtype)
            dv_ref[...] = dv_sum[...].astype(dv_ref.dtype)

    qblk = pl.BlockSpec((qb, LANES), lambda p, i: (i, p))
    col = lambda off: pl.BlockSpec((t, LANES), lambda p, i: (0, off + p))
    shape = jax.ShapeDtypeStruct((t, d), BF16)
    in_specs, operands = [qblk, col(pairs), col(2 * pairs), qblk, qblk], [qkv, qkv, qkv, tot, do]
    out_specs, out_shape = [qblk, col(0), col(0)], [shape, shape, shape]
    sums = [pltpu.VMEM((t, LANES), F32), pltpu.VMEM((t, LANES), F32)]
    if exchange is None:
        return pl.pallas_call(
            body, name=name, grid=(pairs, nq), in_specs=in_specs, out_specs=out_specs, out_shape=out_shape,
            scratch_shapes=sums, compiler_params=_params("parallel", "arbitrary"),
        )(*operands)
    whole = pl.BlockSpec(memory_space=pl.ANY)
    return pl.pallas_call(
        body, name=name, grid=(pairs, nq), in_specs=in_specs + [whole], out_specs=out_specs + [whole],
        out_shape=out_shape + [jax.ShapeDtypeStruct(exchange.shape, exchange.dtype)],
        scratch_shapes=sums + _comm_scratch(), compiler_params=_params("arbitrary", "arbitrary"),
    )(*operands, exchange)


def _shift_down(x, s):
    rows = lax.broadcasted_iota(jnp.int32, x.shape, 0)
    return jnp.where(rows >= s, pltpu.roll(x, s, 0), 0.0)


def _shift_up(x, s):
    t = x.shape[0]
    rows = lax.broadcasted_iota(jnp.int32, x.shape, 0)
    return jnp.where(rows < t - s, pltpu.roll(x, t - s, 0), 0.0)


def conv_fwd(bcu, w, *, name):
    t, d3 = bcu.shape
    d = d3 // 3
    nb = d // LANES
    col = lambda off: pl.BlockSpec((t, LANES), lambda j: (0, off + j))

    def body(b_ref, c_ref, u_ref, w_ref, y_ref):
        hh = c_ref[...] * u_ref[...]
        conv = w_ref[0:1, :] * _shift_down(hh, 2) + w_ref[1:2, :] * _shift_down(hh, 1) + w_ref[2:3, :] * hh
        y_ref[...] = (b_ref[...] * conv).astype(y_ref.dtype)

    return pl.pallas_call(
        body, name=name, grid=(nb,),
        in_specs=[col(0), col(nb), col(2 * nb), pl.BlockSpec((CONV_WIDTH, LANES), lambda j: (0, j))],
        out_specs=col(0), out_shape=jax.ShapeDtypeStruct((t, d), BF16), compiler_params=_params("parallel"),
    )(bcu, bcu, bcu, w)


def conv_bwd(bcu, w, dy, *, name):
    t, d3 = bcu.shape
    d = d3 // 3
    nb = d // LANES
    col = lambda off: pl.BlockSpec((t, LANES), lambda j: (0, off + j))
    wspec = pl.BlockSpec((CONV_WIDTH, LANES), lambda j: (0, j))

    def body(b_ref, c_ref, u_ref, w_ref, dy_ref, db_ref, dc_ref, du_ref, dw_ref):
        c, u, dyv = c_ref[...], u_ref[...], dy_ref[...]
        hh = c * u
        h2, h1 = _shift_down(hh, 2), _shift_down(hh, 1)
        w0, w1, w2 = w_ref[0:1, :], w_ref[1:2, :], w_ref[2:3, :]
        db_ref[...] = (dyv * (w0 * h2 + w1 * h1 + w2 * hh)).astype(db_ref.dtype)
        dconv = dyv * b_ref[...]
        dhh = w2 * dconv + w1 * _shift_up(dconv, 1) + w0 * _shift_up(dconv, 2)
        dc_ref[...] = (dhh * u).astype(dc_ref.dtype)
        du_ref[...] = (dhh * c).astype(du_ref.dtype)
        dw_ref[0:1, :] = jnp.sum(dconv * h2, axis=0, keepdims=True)
        dw_ref[1:2, :] = jnp.sum(dconv * h1, axis=0, keepdims=True)
        dw_ref[2:3, :] = jnp.sum(dconv * hh, axis=0, keepdims=True)

    shape = jax.ShapeDtypeStruct((t, d), BF16)
    return pl.pallas_call(
        body, name=name, grid=(nb,),
        in_specs=[col(0), col(nb), col(2 * nb), wspec, col(0)],
        out_specs=[col(0), col(0), col(0), wspec],
        out_shape=[shape, shape, shape, jax.ShapeDtypeStruct((CONV_WIDTH, d), F32)],
        compiler_params=_params("parallel"),
    )(bcu, bcu, bcu, w, dy)


def _gla_chunk(q, k, lg, scale):
    c = GLA_CHUNK
    rows = lax.broadcasted_iota(jnp.int32, (c, c), 0)
    cols = lax.broadcasted_iota(jnp.int32, (c, c), 1)
    causal = rows >= cols
    tril = jnp.where(causal, 1.0, 0.0).astype(BF16)
    q = q * scale
    cum = _dot_exact_lhs(tril, lg)
    last = cum[c - 1:c, :]
    eq = jnp.exp(cum)
    el = jnp.exp(last - cum)
    return causal, tril, q, k, cum, lg, last, eq, el


def _gla_sub_blocks(q, k, cum, lg):
    key_row = lax.broadcasted_iota(jnp.int32, (GLA_CHUNK, 1), 0)
    out = []
    for lo in range(0, GLA_CHUNK, GLA_SUB):
        hi = lo + GLA_SUB
        ref = cum[lo:lo + 1, :] - lg[lo:lo + 1, :]
        eq = jnp.exp(cum[lo:hi] - ref)
        ek = jnp.where(key_row < hi, jnp.exp(ref - cum), 0.0)
        out.append((slice(lo, hi), eq, ek, q[lo:hi] * eq, k * ek))
    return out


def _gla_scores(subs, causal):
    return jnp.where(causal, jnp.concatenate([_dot(qt, kt, _NT) for _, _, _, qt, kt in subs], axis=0), 0.0)


def gla_fwd(proj, lg, *, name):
    t, d3 = proj.shape
    d = d3 // 3
    dk, dv = d // 2 // GLA_HEADS, d // GLA_HEADS
    assert dk == LANES and dv == 2 * LANES
    c = GLA_CHUNK
    nc = t // c
    scale = dk ** -0.5
    nh = GLA_HEADS

    def body(q_ref, k_ref, v_ref, lg_ref, o_ref, st_out_ref, st_ref):
        @pl.when(pl.program_id(0) == 0)
        def _():
            st_ref[...] = jnp.zeros_like(st_ref)

        for h in range(nh):
            kcols, vcols = slice(h * dk, (h + 1) * dk), slice(h * dv, (h + 1) * dv)
            causal, _, q, k, cum, lg, last, eq, el = _gla_chunk(q_ref[:, kcols], k_ref[:, kcols], lg_ref[:, kcols], scale)
            v = v_ref[:, vcols]
            st = st_ref[h]
            st_out_ref[h] = st
            scores = _gla_scores(_gla_sub_blocks(q, k, cum, lg), causal)
            o_ref[:, vcols] = _dot(q * eq, st, _NT) + _dot(scores, v)
            st_ref[h] = st * jnp.exp(last) + _dot(v, k * el, _TN)

    half, full = pl.BlockSpec((c, d // 2), lambda i: (i, 0)), pl.BlockSpec((c, d), lambda i: (i, 0))
    return pl.pallas_call(
        body, name=name, grid=(nc,),
        in_specs=[half, pl.BlockSpec((c, d // 2), lambda i: (i, 1)), pl.BlockSpec((c, d), lambda i: (i, 1)), half],
        out_specs=[full, pl.BlockSpec((nh, None, dv, dk), lambda i: (0, i, 0, 0))],
        out_shape=[jax.ShapeDtypeStruct((t, d), F32), jax.ShapeDtypeStruct((nh, nc, dv, dk), F32)],
        scratch_shapes=[pltpu.VMEM((nh, dv, dk), F32)], compiler_params=_params("arbitrary"),
    )(proj, proj, proj, lg)


def gla_bwd(proj, lg, states, do, *, name):
    t, d3 = proj.shape
    d = d3 // 3
    dk, dv = d // 2 // GLA_HEADS, d // GLA_HEADS
    c = GLA_CHUNK
    nc = t // c
    scale = dk ** -0.5
    nh = GLA_HEADS

    def body(q_ref, k_ref, v_ref, lg_ref, st_ref, do_ref, dq_ref, dk_ref, dv_ref, dzg_ref, dbg_ref, dst_ref):
        @pl.when(pl.program_id(0) == 0)
        def _():
            dst_ref[...] = jnp.zeros_like(dst_ref)
            dbg_ref[...] = jnp.zeros_like(dbg_ref)

        for h in range(nh):
            kcols, vcols = slice(h * dk, (h + 1) * dk), slice(h * dv, (h + 1) * dv)
            causal, tril, q, k, cum, lg, last, eq, el = _gla_chunk(q_ref[:, kcols], k_ref[:, kcols], lg_ref[:, kcols], scale)
            v, st, dov, dst = v_ref[:, vcols], st_ref[h], do_ref[:, vcols], dst_ref[h]
            subs = _gla_sub_blocks(q, k, cum, lg)
            qt, kh = q * eq, k * el
            scores = _gla_scores(subs, causal)
            dscores = jnp.where(causal, _dot(dov, v, _NT), 0.0)
            dq_parts = []
            dkh = _dot(v, dst)
            dk_h = dkh * el
            for rows, eq_sub, ek_sub, qt_sub, kt_sub in subs:
                dq_parts.append(_dot(dscores[rows], kt_sub) * eq_sub)
                dk_h = dk_h + _dot(dscores[rows], qt_sub, _TN) * ek_sub
            dq = _dot(dov, st) * eq + jnp.concatenate(dq_parts, axis=0)
            dv_ref[:, vcols] = (_dot(scores, dov, _TN) + _dot(kh, dst, _NT)).astype(dv_ref.dtype)
            dq_ref[:, kcols] = (dq * scale).astype(dq_ref.dtype)
            dk_ref[:, kcols] = dk_h.astype(dk_ref.dtype)
            e_last = jnp.exp(last)
            dlast = jnp.sum(kh * dkh, axis=0, keepdims=True) + e_last * jnp.sum(dst * st, axis=0, keepdims=True)
            dcum = q * dq - k * dk_h
            dlg = _dot_exact_lhs(tril, dcum, _TN) + dlast
            dzg = dlg * (1.0 - jnp.exp(lg * GLA_GATE_NORMALIZER)) / GLA_GATE_NORMALIZER
            dzg_ref[:, kcols] = dzg.astype(dzg_ref.dtype)
            dbg_ref[:, kcols] += jnp.sum(dzg, axis=0, keepdims=True)
            dst_ref[h] = dst * e_last + _dot(dov, qt, _TN)

    rev = lambda i: nc - 1 - i
    half, full = pl.BlockSpec((c, d // 2), lambda i: (rev(i), 0)), pl.BlockSpec((c, d), lambda i: (rev(i), 0))
    half_shape = jax.ShapeDtypeStruct((t, d // 2), BF16)
    return pl.pallas_call(
        body, name=name, grid=(nc,),
        in_specs=[half, pl.BlockSpec((c, d // 2), lambda i: (rev(i), 1)), pl.BlockSpec((c, d), lambda i: (rev(i), 1)), half,
                  pl.BlockSpec((nh, None, dv, dk), lambda i: (0, rev(i), 0, 0)), full],
        out_specs=[half, half, full, half, pl.BlockSpec((1, d // 2), lambda i: (0, 0))],
        out_shape=[half_shape, half_shape, jax.ShapeDtypeStruct((t, d), BF16), half_shape,
                   jax.ShapeDtypeStruct((1, d // 2), F32)],
        scratch_shapes=[pltpu.VMEM((nh, dv, dk), F32)], compiler_params=_params("arbitrary"),
    )(proj, proj, proj, lg, states, do)


def gla_post_fwd(o, proj, head_norm, *, name, tm=ROW_BLOCK):
    t, d = o.shape
    dv = d // GLA_HEADS
    tm = min(tm, t)

    def body(o_ref, g_ref, hn_ref, y_ref):
        for h in range(GLA_HEADS):
            sl = slice(h * dv, (h + 1) * dv)
            ov, gv = o_ref[:, sl], g_ref[:, sl]
            r = lax.rsqrt(jnp.mean(ov * ov, axis=-1, keepdims=True) + RMS_EPS)
            y_ref[:, sl] = ((ov * r * hn_ref[:, sl]) * (gv * _sigmoid(gv))).astype(y_ref.dtype)

    row = pl.BlockSpec((tm, d), lambda i: (i, 0))
    return pl.pallas_call(
        body, name=name, grid=(t // tm,),
        in_specs=[row, pl.BlockSpec((tm, d), lambda i: (i, 2)), pl.BlockSpec((1, d), lambda i: (0, 0))],
        out_specs=row, out_shape=jax.ShapeDtypeStruct((t, d), BF16), compiler_params=_params("parallel"),
    )(o, proj, head_norm)


def gla_post_bwd(o, proj, head_norm, dy, *, name, tm=ROW_BLOCK):
    t, d = o.shape
    dv = d // GLA_HEADS
    tm = min(tm, t)

    def body(o_ref, g_ref, hn_ref, dy_ref, do_ref, dg_ref, dhn_ref):
        @pl.when(pl.program_id(0) == 0)
        def _():
            dhn_ref[...] = jnp.zeros_like(dhn_ref)

        for h in range(GLA_HEADS):
            sl = slice(h * dv, (h + 1) * dv)
            ov, gv, dyv, hn = o_ref[:, sl], g_ref[:, sl], dy_ref[:, sl], hn_ref[:, sl]
            r = lax.rsqrt(jnp.mean(ov * ov, axis=-1, keepdims=True) + RMS_EPS)
            sg = _sigmoid(gv)
            silu = gv * sg
            on = ov * r * hn
            dg_ref[:, sl] = (dyv * on * (sg * (1.0 + gv * (1.0 - sg)))).astype(dg_ref.dtype)
            don = dyv * silu
            u = don * hn
            do_ref[:, sl] = (r * u - ov * (r * r * r * jnp.mean(u * ov, axis=-1, keepdims=True))).astype(do_ref.dtype)
            dhn_ref[:, sl] += jnp.sum(don * ov * r, axis=0, keepdims=True)

    row = pl.BlockSpec((tm, d), lambda i: (i, 0))
    vec = pl.BlockSpec((1, d), lambda i: (0, 0))
    shape = jax.ShapeDtypeStruct((t, d), BF16)
    return pl.pallas_call(
        body, name=name, grid=(t // tm,),
        in_specs=[row, pl.BlockSpec((tm, d), lambda i: (i, 2)), vec, row],
        out_specs=[row, row, vec], out_shape=[shape, shape, jax.ShapeDtypeStruct((1, d), F32)],
        compiler_params=_params("arbitrary"),
    )(o, proj, head_norm, dy)


def _ffn_fwd(h, gains, w_up, w_down, tag):
    xn = rms_fwd(h, gains[2], name=f"{tag}_ffn_norm", out_dtype=BF16)
    act = matmul(xn, w_up, mode='nn', epi='relu2', out_dtype=BF16, name=f"{tag}_ffn_up")
    f = matmul(act, w_down, mode='nn', name=f"{tag}_ffn_down")
    h_out = rms_fwd(f, gains[3], res=h, name=f"{tag}_ffn_out")
    return h_out, (h, xn, act, f)


def _ffn_bwd(dh, saved, gains, w_up, w_down, tag):
    h, xn, act, f = saved
    df, dg3 = rms_bwd(f, gains[3], dh, out_dtype=BF16, name=f"{tag}_ffn_out_bwd")
    du = matmul(df, w_down, mode='nt', epi='mul2sqrt', extra=act, out_dtype=BF16, name=f"{tag}_ffn_da")
    dw_down = matmul(act, df, mode='tn', out_dtype=BF16, name=f"{tag}_ffn_dwdown")
    dw_up = matmul(xn, du, mode='tn', out_dtype=BF16, name=f"{tag}_ffn_dwup")
    dxn = matmul(du, w_up, mode='nt', name=f"{tag}_ffn_dxn")
    dh_in, dg2 = rms_bwd(h, gains[2], dxn, dres=dh, name=f"{tag}_ffn_norm_bwd")
    return dh_in, dg2, dg3, dw_up, dw_down


def _sb_layer_fwd(xn, w, j, tag, comm=None):
    qkv = matmul(xn, w['sb_w_qkv'][j], mode='nn', out_dtype=BF16, name=f"{tag}_qkv")
    if comm is None:
        o, tot = sb_fwd(qkv, name=f"{tag}_sb")
    else:
        o, tot, gathered = sb_fwd(qkv, name=f"{tag}_sb", gather=comm.rest_payload)
        comm.on_gathered(gathered)
    m = matmul(o, w['sb_w_o'][j], mode='nn', name=f"{tag}_wo")
    return m, (qkv, o, tot)


def _sb_layer_bwd(dm, xn, saved, w, j, tag, comm=None, grads=None):
    qkv, o, tot = saved
    do = matmul(dm, w['sb_w_o'][j], mode='nt', out_dtype=BF16, name=f"{tag}_do")
    dw_o = matmul(o, dm, mode='tn', out_dtype=BF16, name=f"{tag}_dwo")
    if comm is None:
        dq, dk, dv = sb_bwd(qkv, tot, do, name=f"{tag}_sb_bwd")
    else:
        parts = comm.rest_parts({**grads, ('sb_w_o', j): dw_o})
        dq, dk, dv, received = sb_bwd(qkv, tot, do, name=f"{tag}_sb_bwd", exchange=parts)
        comm.on_received(received)
    dqkv = jnp.concatenate([dq, dk, dv], axis=1)
    dw_qkv = matmul(xn, dqkv, mode='tn', out_dtype=BF16, name=f"{tag}_dwqkv")
    dxn = matmul(dqkv, w['sb_w_qkv'][j], mode='nt', name=f"{tag}_dxn")
    return dxn, {('sb_w_qkv', j): dw_qkv, ('sb_w_o', j): dw_o}


def _conv_layer_fwd(xn, w, j, tag):
    bcu = matmul(xn, w['conv_w_in'][j], mode='nn', name=f"{tag}_in")
    y = conv_fwd(bcu, w['conv_w'][j], name=f"{tag}_conv")
    m = matmul(y, w['conv_w_out'][j], mode='nn', name=f"{tag}_out")
    return m, (bcu, y)


def _conv_layer_bwd(dm, xn, saved, w, j, tag):
    bcu, y = saved
    dy = matmul(dm, w['conv_w_out'][j], mode='nt', name=f"{tag}_dy")
    dw_out = matmul(y, dm, mode='tn', out_dtype=BF16, name=f"{tag}_dwout")
    db, dc, du, dw_conv = conv_bwd(bcu, w['conv_w'][j], dy, name=f"{tag}_conv_bwd")
    dbcu = jnp.concatenate([db, dc, du], axis=1)
    dw_in = matmul(xn, dbcu, mode='tn', out_dtype=BF16, name=f"{tag}_dwin")
    dxn = matmul(dbcu, w['conv_w_in'][j], mode='nt', name=f"{tag}_dxn")
    return dxn, {('conv_w_in', j): dw_in, ('conv_w', j): dw_conv, ('conv_w_out', j): dw_out}


def _gla_split(w_in, w_gate_up):
    d = w_in.shape[0]
    w_main = w_in[:, :3 * d]
    w_a = jnp.pad(w_in[:, 3 * d:], ((0, 0), (0, LANES - GLA_GATE_RANK)))
    w_gu = jnp.pad(w_gate_up, ((0, LANES - GLA_GATE_RANK), (0, 0)))
    return w_main, w_a, w_gu


def _gla_layer_fwd(xn, w, j, tag):
    w_main, w_a, w_gu = _gla_split(w['gla_w_in'][j], w['gla_w_gate_up'][j])
    proj = matmul(xn, w_main, mode='nn', name=f"{tag}_in")
    a_low = matmul(xn, w_a, mode='nn', out_dtype=BF16, name=f"{tag}_alow")
    lg = matmul(a_low, w_gu, mode='nn', epi='logsig16', extra=w['gla_b_gate'][j][None, :], name=f"{tag}_gate")
    o, states = gla_fwd(proj, lg, name=f"{tag}_gla")
    hn = w['gla_head_norm'][j].reshape(1, -1)
    y = gla_post_fwd(o, proj, hn, name=f"{tag}_post")
    m = matmul(y, w['gla_w_o'][j], mode='nn', name=f"{tag}_wo")
    return m, (proj, a_low, lg, o, states, y)


def _gla_layer_bwd(dm, xn, saved, w, j, tag):
    proj, a_low, lg, o, states, y = saved
    w_main, w_a, w_gu = _gla_split(w['gla_w_in'][j], w['gla_w_gate_up'][j])
    hn = w['gla_head_norm'][j].reshape(1, -1)
    dy = matmul(dm, w['gla_w_o'][j], mode='nt', name=f"{tag}_dy")
    dw_o = matmul(y, dm, mode='tn', out_dtype=BF16, name=f"{tag}_dwo")
    do, dg, dhn = gla_post_bwd(o, proj, hn, dy, name=f"{tag}_post_bwd")
    dq, dk, dv, dzg, dbg = gla_bwd(proj, lg, states, do, name=f"{tag}_gla_bwd")
    da_low = matmul(dzg, w_gu, mode='nt', out_dtype=BF16, name=f"{tag}_dalow")
    dw_gu = matmul(a_low, dzg, mode='tn', out_dtype=BF16, name=f"{tag}_dwgu")[:GLA_GATE_RANK]
    dproj = jnp.concatenate([dq, dk, dv, dg], axis=1)
    dw_main = matmul(xn, dproj, mode='tn', out_dtype=BF16, name=f"{tag}_dwin")
    dw_a = matmul(xn, da_low, mode='tn', out_dtype=BF16, name=f"{tag}_dwa")[:, :GLA_GATE_RANK]
    dxn_a = matmul(da_low, w_a, mode='nt', name=f"{tag}_dxn_a")
    dxn = matmul(dproj, w_main, mode='nt', epi='add', extra=dxn_a, name=f"{tag}_dxn")
    grads = {('gla_w_in', j): jnp.concatenate([dw_main, dw_a], axis=1), ('gla_w_gate_up', j): dw_gu,
             ('gla_b_gate', j): dbg[0], ('gla_head_norm', j): dhn.reshape(w['gla_head_norm'][j].shape),
             ('gla_w_o', j): dw_o}
    return dxn, grads


_MIXERS = ((_sb_layer_fwd, _sb_layer_bwd), (_conv_layer_fwd, _conv_layer_bwd), (_gla_layer_fwd, _gla_layer_bwd))


def local_step(x, w, target, comm=None):
    depth = len(w['norm_gains'])
    h = x
    tape = []
    for i in range(depth):
        kind, j = i % 3, i // 3
        tag = f"l{i}"
        extra = {'comm': comm} if (comm is not None and i == 0) else {}
        gains = [w['norm_gains'][i][s][None, :] for s in range(4)]
        xn = rms_fwd(h, gains[0], name=f"{tag}_mix_norm", out_dtype=BF16)
        m, saved = _MIXERS[kind][0](xn, w, j, tag, **extra)
        h_mid = rms_fwd(m, gains[1], res=h, name=f"{tag}_mix_out")
        h_out, ffn_saved = _ffn_fwd(h_mid, gains, w['ffn_w_up'][i], w['ffn_w_down'][i], tag)
        tape.append((h, xn, m, saved, ffn_saved, gains))
        h = h_out
    loss, dh = loss_head(h, target, name="loss_head")

    grads = {}
    for i in reversed(range(depth)):
        kind, j = i % 3, i // 3
        tag = f"l{i}"
        h_in, xn, m, saved, ffn_saved, gains = tape[i]
        dg = [None] * 4
        dh, dg[2], dg[3], grads[('ffn_w_up', i)], grads[('ffn_w_down', i)] = _ffn_bwd(
            dh, ffn_saved, gains, w['ffn_w_up'][i], w['ffn_w_down'][i], tag)
        dm, dg[1] = rms_bwd(m, gains[1], dh, out_dtype=BF16, name=f"{tag}_mix_out_bwd")
        extra = {'comm': comm, 'grads': grads} if (comm is not None and i == 0) else {}
        dxn, g = _MIXERS[kind][1](dm, xn, saved, w, j, tag, **extra)
        grads.update(g)
        dh, dg[0] = rms_bwd(h_in, gains[0], dxn, dres=dh, name=f"{tag}_mix_norm_bwd")
        grads[('norm_gains', i)] = jnp.concatenate(dg, axis=0)
    return loss, dh, grads


def _segments(keys, shard_shapes, f32_as_pairs):
    segs, row = {}, 0
    for name, lo, hi in keys:
        n = (hi - lo) * math.prod(shard_shapes[name][1:]) * (2 if f32_as_pairs and name in F32_PAYLOAD else 1)
        nrows = -(-n // (PACK_COLS * PACK_ROW_ALIGN)) * PACK_ROW_ALIGN
        segs[(name, lo, hi)] = (row, nrows, n)
        row += nrows
    return segs, -(-row // PACK_ROW_BLOCK) * PACK_ROW_BLOCK


def _pack(parts, segs, total_rows, dtype):
    pieces, row = [], 0
    for key in segs:
        _, nrows, n = segs[key]
        p = parts[key].astype(dtype)
        lead = p.shape[:-1]
        if nrows * PACK_COLS > n:
            p = jnp.pad(p, [(0, 0)] * len(lead) + [(0, nrows * PACK_COLS - n)])
        pieces.append(p.reshape(lead + (nrows, PACK_COLS)))
        row += nrows
    if total_rows > row:
        pieces.append(jnp.zeros(lead + (total_rows - row, PACK_COLS), dtype))
    return jnp.concatenate(pieces, axis=-2)


def _unpack(buf, seg):
    first, nrows, n = seg
    piece = buf[..., first:first + nrows, :]
    return piece.reshape(piece.shape[:-2] + (nrows * PACK_COLS,))[..., :n]


def _unshard(gathered, axis):
    moved = jnp.moveaxis(gathered, 0, axis)
    shape = moved.shape
    return moved.reshape(shape[:axis] + (shape[axis] * shape[axis + 1],) + shape[axis + 2:])


def _shard_split(full, axis):
    shape = full.shape
    cut = full.reshape(shape[:axis] + (N_DEV, shape[axis] // N_DEV) + shape[axis + 1:])
    return jnp.moveaxis(cut, axis, 0)


def _mesh_position():
    return lax.axis_index("x"), lax.axis_index("y"), lax.axis_index("c")


def _comm_scratch():
    return [pltpu.SemaphoreType.DMA((N_DEV - 1,)), pltpu.SemaphoreType.DMA((N_DEV - 1,)), pltpu.SemaphoreType.DMA]


def _gather_plan(x_ref, out_ref, send_sems, recv_sems, local_sem):
    rows = x_ref.shape[0]
    x, y, c = _mesh_position()
    me, sibling = (x, y, c), (x, y, 1 - c)
    chips = [(1 - x, y), (x, 1 - y), (1 - x, 1 - y)]

    def block(px, py, pc):
        return out_ref.at[pl.ds((4 * px + 2 * py + pc) * rows, rows), :]

    def copy(k, blk, to, src=None):
        return pltpu.make_async_remote_copy(
            src_ref=block(*blk) if src is None else src, dst_ref=block(*blk),
            send_sem=send_sems.at[k], recv_sem=recv_sems.at[k],
            device_id=to, device_id_type=pl.DeviceIdType.MESH)

    mine = pltpu.make_async_copy(x_ref, block(*me), local_sem)
    first = [copy(0, me, sibling, src=x_ref)]
    first += [copy(1 + j, me, (*chip, c), src=x_ref) for j, chip in enumerate(chips)]
    passed = [copy(4 + j, (*chip, c), sibling) for j, chip in enumerate(chips)]

    def start():
        mine.start()
        for cp in first:
            cp.start()

    def forward():
        for j, chip in enumerate(chips):
            copy(1 + j, (*chip, c), me).wait_recv()
            passed[j].start()

    def finish():
        copy(0, sibling, me).wait_recv()
        for j, chip in enumerate(chips):
            copy(4 + j, (*chip, 1 - c), me).wait_recv()
        for cp in first + passed:
            cp.wait_send()
        mine.wait()

    return start, forward, finish


def _exchange_plan(in_ref, out_ref, send_sems, recv_sems, local_sem):
    x, y, c = _mesh_position()
    my_id = 4 * x + 2 * y + c
    mine = pltpu.make_async_copy(in_ref.at[my_id], out_ref.at[my_id], local_sem)

    def copy(k, receive):
        px = 1 - x if k & 4 else x
        py = 1 - y if k & 2 else y
        pc = 1 - c if k & 1 else c
        peer_id = 4 * px + 2 * py + pc
        return pltpu.make_async_remote_copy(
            src_ref=in_ref.at[peer_id], dst_ref=out_ref.at[peer_id if receive else my_id],
            send_sem=send_sems.at[k - 1], recv_sem=recv_sems.at[k - 1],
            device_id=(px, py, pc), device_id_type=pl.DeviceIdType.MESH)

    def start():
        mine.start()
        for k in range(1, N_DEV):
            copy(k, False).start()

    def finish():
        for k in range(1, N_DEV):
            copy(k, True).wait_recv()
        for k in range(1, N_DEV):
            copy(k, False).wait_send()
        mine.wait()

    return start, finish


def all_gather(shard, *, name):
    rows, cols = shard.shape

    def body(x_ref, out_ref, send_sems, recv_sems, local_sem):
        start, forward, finish = _gather_plan(x_ref, out_ref, send_sems, recv_sems, local_sem)
        start()
        forward()
        finish()

    return pl.pallas_call(
        body, name=name, out_shape=jax.ShapeDtypeStruct((N_DEV * rows, cols), shard.dtype),
        in_specs=[pl.BlockSpec(memory_space=pl.ANY)], out_specs=pl.BlockSpec(memory_space=pl.ANY),
        scratch_shapes=_comm_scratch(),
    )(shard)


def exchange_shards(parts, *, name):
    def body(in_ref, out_ref, send_sems, recv_sems, local_sem):
        start, finish = _exchange_plan(in_ref, out_ref, send_sems, recv_sems, local_sem)
        start()
        finish()

    return pl.pallas_call(
        body, name=name, out_shape=jax.ShapeDtypeStruct(parts.shape, parts.dtype),
        in_specs=[pl.BlockSpec(memory_space=pl.ANY)], out_specs=pl.BlockSpec(memory_space=pl.ANY),
        scratch_shapes=_comm_scratch(),
    )(parts)


def adamw(parts, w, m, v, *, name):
    _, rows, cols = parts.shape
    tr = PACK_ROW_BLOCK
    c1 = 1.0 - ADAM_B1 ** ADAM_STEP
    c2 = 1.0 - ADAM_B2 ** ADAM_STEP

    def body(p_ref, w_ref, m_ref, v_ref, g_ref, d_ref, nm_ref, nv_ref):
        g = p_ref[0].astype(F32)
        for s in range(1, N_DEV):
            g = g + p_ref[s].astype(F32)
        nm = ADAM_B1 * m_ref[...] + (1.0 - ADAM_B1) * g
        nv = ADAM_B2 * v_ref[...] + (1.0 - ADAM_B2) * jnp.square(g)
        m_hat = nm / c1
        v_hat = nv / c2
        g_ref[...] = g
        d_ref[...] = -ADAM_LR * (m_hat / (jnp.sqrt(v_hat) + ADAM_EPS) + ADAM_WD * w_ref[...])
        nm_ref[...] = nm
        nv_ref[...] = nv

    row = pl.BlockSpec((tr, cols), lambda i: (i, 0))
    shape = jax.ShapeDtypeStruct((rows, cols), F32)
    return pl.pallas_call(
        body, name=name, grid=(rows // tr,),
        in_specs=[pl.BlockSpec((N_DEV, tr, cols), lambda i: (0, i, 0)), row, row, row],
        out_specs=[row, row, row, row], out_shape=[shape, shape, shape, shape],
        compiler_params=_params("parallel"),
    )(parts, w, m, v)


def kernel(x, norm_gains, sb_w_qkv, sb_w_o, conv_w_in, conv_w, conv_w_out, gla_w_in, gla_w_gate_up, gla_b_gate, gla_head_norm, gla_w_o, ffn_w_up, ffn_w_down, loss_target, m_norm_gains, m_sb_w_qkv, m_sb_w_o, m_conv_w_in, m_conv_w, m_conv_w_out, m_gla_w_in, m_gla_w_gate_up, m_gla_b_gate, m_gla_head_norm, m_gla_w_o, m_ffn_w_up, m_ffn_w_down, v_norm_gains, v_sb_w_qkv, v_sb_w_o, v_conv_w_in, v_conv_w, v_conv_w_out, v_gla_w_in, v_gla_w_gate_up, v_gla_b_gate, v_gla_head_norm, v_gla_w_o, v_ffn_w_up, v_ffn_w_down):
    shards = dict(zip(WEIGHTS, (norm_gains, sb_w_qkv, sb_w_o, conv_w_in, conv_w, conv_w_out, gla_w_in,
                                gla_w_gate_up, gla_b_gate, gla_head_norm, gla_w_o, ffn_w_up, ffn_w_down)))
    moments_m = dict(zip(WEIGHTS, (m_norm_gains, m_sb_w_qkv, m_sb_w_o, m_conv_w_in, m_conv_w, m_conv_w_out,
                                   m_gla_w_in, m_gla_w_gate_up, m_gla_b_gate, m_gla_head_norm, m_gla_w_o,
                                   m_ffn_w_up, m_ffn_w_down)))
    moments_v = dict(zip(WEIGHTS, (v_norm_gains, v_sb_w_qkv, v_sb_w_o, v_conv_w_in, v_conv_w, v_conv_w_out,
                                   v_gla_w_in, v_gla_w_gate_up, v_gla_b_gate, v_gla_head_norm, v_gla_w_o,
                                   v_ffn_w_up, v_ffn_w_down)))
    shard_shapes = {n: a.shape for n, a in shards.items()}

    alone = [('norm_gains', 0, 1), ('sb_w_qkv', 0, 1)]
    rest = [(n, 1 if (n, 0, 1) in alone else 0, shard_shapes[n][0]) for n in WEIGHTS]
    groups = [alone, [piece for piece in rest if piece[1] < piece[2]]]

    def payload(group):
        segs, rows = _segments(group, shard_shapes, True)
        flat = {(n, lo, hi): (lax.bitcast_convert_type(shards[n][lo:hi], BF16) if n in F32_PAYLOAD
                              else shards[n][lo:hi].astype(BF16)).reshape(-1) for n, lo, hi in group}
        return segs, _pack(flat, segs, rows, BF16)

    whole = {n: [None] * shard_shapes[n][0] for n in WEIGHTS}

    def take_gathered(segs, gathered):
        gathered = gathered.reshape(N_DEV, -1, PACK_COLS)
        for n, lo, hi in segs:
            piece = _unpack(gathered, segs[(n, lo, hi)])
            if n in F32_PAYLOAD:
                piece = lax.bitcast_convert_type(piece.reshape(N_DEV, -1, 2), F32)
            full = _unshard(piece.reshape((N_DEV, hi - lo) + shard_shapes[n][1:]), SHARD_AXIS[n])
            for j in range(lo, hi):
                whole[n][j] = full[j - lo]

    segs0, payload0 = payload(groups[0])
    segs1, payload1 = payload(groups[1])
    take_gathered(segs0, all_gather(payload0, name="weights_all_gather"))

    gsegs = [_segments(group, shard_shapes, False) for group in groups]
    received = [None, None]

    def parts(g, grads):
        segs, rows = gsegs[g]
        flat = {(n, lo, hi): _shard_split(jnp.stack([grads[(n, j)] for j in range(lo, hi)]), SHARD_AXIS[n])
                .reshape(N_DEV, -1) for n, lo, hi in segs}
        return _pack(flat, segs, rows, BF16)

    def on_received(buf):
        received[1] = buf

    comm = types.SimpleNamespace(rest_payload=payload1, on_gathered=functools.partial(take_gathered, segs1),
                                 rest_parts=functools.partial(parts, 1), on_received=on_received)

    loss, grad_x, grads = local_step(x[0], whole, loss_target[0], comm)
    loss = lax.psum(loss[0, 0], ("x", "y", "c"))
    received[0] = exchange_shards(parts(0, grads), name="grads_exchange")

    results = {n: [[] for _ in range(4)] for n in WEIGHTS}
    for g in range(2):
        segs, rows = gsegs[g]

        def packed(source):
            return _pack({(n, lo, hi): source[n][lo:hi].reshape(-1) for n, lo, hi in segs}, segs, rows, F32)

        outs = adamw(received[g], packed(shards), packed(moments_m), packed(moments_v), name=f"adamw{g}")
        for n, lo, hi in segs:
            for o, buf in enumerate(outs):
                results[n][o].append(_unpack(buf, segs[(n, lo, hi)]).reshape((hi - lo,) + shard_shapes[n][1:]))
    whole_out = [[r[0] if len(r) == 1 else jnp.concatenate(r, axis=0) for r in (results[n][o] for n in WEIGHTS)]
                 for o in range(4)]
    return (loss, grad_x[None], *whole_out[0], *whole_out[1], *whole_out[2], *whole_out[3])
```

```python
import functools
import math
import types

import jax
import jax.numpy as jnp
from jax import lax
from jax.experimental import pallas as pl
from jax.experimental.pallas import tpu as pltpu

F32 = jnp.float32
BF16 = jnp.bfloat16

N_DEV = 8
SB_HEADS = 16
GLA_HEADS = 4
GLA_CHUNK = 64
GLA_SUB = 16
GLA_GATE_RANK = 16
GLA_GATE_NORMALIZER = 16.0
CONV_WIDTH = 3
RMS_EPS = 1e-6
ADAM_LR = 0.001
ADAM_B1 = 0.9
ADAM_B2 = 0.999
ADAM_EPS = 1e-08
ADAM_WD = 0.01
ADAM_STEP = 10

LANES = 128
SB_BLOCK = 256
SB_FWD_Q_BLOCK = 1024
SB_BWD_Q_BLOCK = 1024
VMEM_LIMIT_BYTES = 56 * 1024 * 1024
MM_TM, MM_TN, MM_TK = 1024, 1024, 1024
ROW_BLOCK = 512
PACK_COLS = 1024
PACK_ROW_ALIGN = 16
PACK_ROW_BLOCK = 128

WEIGHTS = ['norm_gains', 'sb_w_qkv', 'sb_w_o', 'conv_w_in', 'conv_w', 'conv_w_out', 'gla_w_in',
           'gla_w_gate_up', 'gla_b_gate', 'gla_head_norm', 'gla_w_o', 'ffn_w_up', 'ffn_w_down']
SHARD_AXIS = {'norm_gains': 2, 'sb_w_qkv': 2, 'sb_w_o': 1, 'conv_w_in': 2, 'conv_w': 2, 'conv_w_out': 1,
              'gla_w_in': 2, 'gla_w_gate_up': 2, 'gla_b_gate': 1, 'gla_head_norm': 2, 'gla_w_o': 1,
              'ffn_w_up': 2, 'ffn_w_down': 1}
F32_PAYLOAD = ('norm_gains', 'conv_w', 'gla_b_gate', 'gla_head_norm')

_NN = (((1,), (0,)), ((), ()))
_NT = (((1,), (1,)), ((), ()))
_TN = (((0,), (0,)), ((), ()))
_DIMS = {'nn': _NN, 'nt': _NT, 'tn': _TN}


def _params(*semantics):
    return pltpu.CompilerParams(dimension_semantics=semantics, vmem_limit_bytes=VMEM_LIMIT_BYTES)


def _dot(a, b, dims=_NN):
    return lax.dot_general(a.astype(BF16), b.astype(BF16), dims, preferred_element_type=F32)


def _split_hi_lo(x):
    hi = x.astype(BF16)
    lo = (x - hi.astype(F32)).astype(BF16)
    return hi, lo


def _dot_exact_rhs(x, ones_mat, dims=_NN):
    hi, lo = _split_hi_lo(x)
    return (lax.dot_general(hi, ones_mat, dims, preferred_element_type=F32)
            + lax.dot_general(lo, ones_mat, dims, preferred_element_type=F32))


def _dot_exact_lhs(ones_mat, x, dims=_NN):
    hi, lo = _split_hi_lo(x)
    return (lax.dot_general(ones_mat, hi, dims, preferred_element_type=F32)
            + lax.dot_general(ones_mat, lo, dims, preferred_element_type=F32))


def _log_sigmoid(z):
    return jnp.minimum(z, 0.0) - jnp.log(1.0 + jnp.exp(-jnp.abs(z)))


def _sigmoid(z):
    return 1.0 / (1.0 + jnp.exp(-z))


def matmul(a, b, *, mode, name, out_dtype=F32, epi=None, extra=None, tm=MM_TM, tn=MM_TN, tk=MM_TK):
    if mode == 'nn':
        (m, k), (k2, n) = a.shape, b.shape
    elif mode == 'nt':
        (m, k), (n, k2) = a.shape, b.shape
    else:
        (k, m), (k2, n) = a.shape, b.shape
    assert k == k2, (a.shape, b.shape, mode)
    tm, tn, tk = min(tm, m), min(tn, n), min(tk, k)
    assert m % tm == 0 and n % tn == 0 and k % tk == 0, (a.shape, b.shape, mode)
    nk = k // tk
    if mode == 'tn':
        a_spec = pl.BlockSpec((tk, tm), lambda i, j, kk: (kk, i))
    else:
        a_spec = pl.BlockSpec((tm, tk), lambda i, j, kk: (i, kk))
    if mode == 'nt':
        b_spec = pl.BlockSpec((tn, tk), lambda i, j, kk: (j, kk))
    else:
        b_spec = pl.BlockSpec((tk, tn), lambda i, j, kk: (kk, j))
    in_specs, operands = [a_spec, b_spec], [a, b]
    if epi == 'logsig16':
        in_specs.append(pl.BlockSpec((1, tn), lambda i, j, kk: (0, j)))
        operands.append(extra)
    elif epi in ('mul2sqrt', 'add'):
        in_specs.append(pl.BlockSpec((tm, tn), lambda i, j, kk: (i, j)))
        operands.append(extra)

    n_extra = len(operands) - 2

    def body(a_ref, b_ref, *rest):
        e_ref = rest[0] if n_extra else None
        o_ref = rest[n_extra]

        def finish(r):
            if epi == 'relu2':
                r = jnp.square(jnp.maximum(r, 0.0))
            elif epi == 'mul2sqrt':
                r = r * (2.0 * jnp.sqrt(e_ref[...].astype(F32)))
            elif epi == 'add':
                r = r + e_ref[...]
            elif epi == 'logsig16':
                r = _log_sigmoid(r + e_ref[...]) / GLA_GATE_NORMALIZER
            o_ref[...] = r.astype(o_ref.dtype)

        part = _dot(a_ref[...], b_ref[...], _DIMS[mode])
        if nk == 1:
            finish(part)
        else:
            acc_ref = rest[-1]
            kk = pl.program_id(2)

            @pl.when(kk == 0)
            def _():
                acc_ref[...] = part

            @pl.when(kk > 0)
            def _():
                acc_ref[...] += part

            @pl.when(kk == nk - 1)
            def _():
                finish(acc_ref[...])

    return pl.pallas_call(
        body, name=name, grid=(m // tm, n // tn, nk), in_specs=in_specs,
        out_specs=pl.BlockSpec((tm, tn), lambda i, j, kk: (i, j)),
        out_shape=jax.ShapeDtypeStruct((m, n), out_dtype),
        scratch_shapes=[pltpu.VMEM((tm, tn), F32)] if nk > 1 else [],
        compiler_params=_params("parallel", "parallel", "arbitrary"),
    )(*operands)


def rms_fwd(x, gain, *, name, res=None, out_dtype=F32, tm=ROW_BLOCK):
    t, d = x.shape
    tm = min(tm, t)
    row = pl.BlockSpec((tm, d), lambda i: (i, 0))
    in_specs, operands = [row, pl.BlockSpec((1, d), lambda i: (0, 0))], [x, gain]
    if res is not None:
        in_specs.append(row)
        operands.append(res)

    def body(x_ref, g_ref, *rest):
        xv = x_ref[...]
        r = lax.rsqrt(jnp.mean(xv * xv, axis=-1, keepdims=True) + RMS_EPS)
        y = xv * r * g_ref[...]
        if res is not None:
            y = rest[0][...] + y
        rest[-1][...] = y.astype(out_dtype)

    return pl.pallas_call(
        body, name=name, grid=(t // tm,), in_specs=in_specs, out_specs=row,
        out_shape=jax.ShapeDtypeStruct((t, d), out_dtype), compiler_params=_params("parallel"),
    )(*operands)


def rms_bwd(x, gain, dy, *, name, dres=None, out_dtype=F32, tm=ROW_BLOCK):
    t, d = x.shape
    tm = min(tm, t)
    row = pl.BlockSpec((tm, d), lambda i: (i, 0))
    vec = pl.BlockSpec((1, d), lambda i: (0, 0))
    in_specs, operands = [row, vec, row], [x, gain, dy]
    if dres is not None:
        in_specs.append(row)
        operands.append(dres)

    def body(x_ref, g_ref, dy_ref, *rest):
        dx_ref, dg_ref = rest[-2], rest[-1]

        @pl.when(pl.program_id(0) == 0)
        def _():
            dg_ref[...] = jnp.zeros_like(dg_ref)

        xv, dyv = x_ref[...], dy_ref[...]
        r = lax.rsqrt(jnp.mean(xv * xv, axis=-1, keepdims=True) + RMS_EPS)
        u = dyv * g_ref[...]
        dx = r * u - xv * (r * r * r * jnp.mean(u * xv, axis=-1, keepdims=True))
        if dres is not None:
            dx = rest[0][...] + dx
        dx_ref[...] = dx.astype(out_dtype)
        dg_ref[...] += jnp.sum(dyv * xv * r, axis=0, keepdims=True)

    return pl.pallas_call(
        body, name=name, grid=(t // tm,), in_specs=in_specs, out_specs=[row, vec],
        out_shape=[jax.ShapeDtypeStruct((t, d), out_dtype), jax.ShapeDtypeStruct((1, d), F32)],
        compiler_params=_params("arbitrary"),
    )(*operands)


def loss_head(y, target, *, name, tm=ROW_BLOCK):
    t, d = y.shape
    tm = min(tm, t)
    nt = t // tm
    row = pl.BlockSpec((tm, d), lambda i: (i, 0))

    def body(y_ref, t_ref, loss_ref, dy_ref, acc_ref):
        i = pl.program_id(0)

        @pl.when(i == 0)
        def _():
            acc_ref[...] = jnp.zeros_like(acc_ref)

        err = y_ref[...] - t_ref[...]
        dy_ref[...] = err * (1.0 / d)
        acc_ref[...] += jnp.sum(err * err, axis=0, keepdims=True)

        @pl.when(i == nt - 1)
        def _():
            loss_ref[...] = jnp.sum(acc_ref[...], axis=1, keepdims=True) * (0.5 / d)

    return pl.pallas_call(
        body, name=name, grid=(nt,), in_specs=[row, row],
        out_specs=[pl.BlockSpec((1, 1), lambda i: (0, 0)), row],
        out_shape=[jax.ShapeDtypeStruct((1, 1), F32), jax.ShapeDtypeStruct((t, d), F32)],
        scratch_shapes=[pltpu.VMEM((1, d), F32)], compiler_params=_params("arbitrary"),
    )(y, target)


def _sb_block_iota():
    rows = lax.broadcasted_iota(jnp.int32, (SB_BLOCK, SB_BLOCK), 0)
    cols = lax.broadcasted_iota(jnp.int32, (SB_BLOCK, SB_BLOCK), 1)
    return rows, cols


def _sb_logits(q_h, k_blk, mask):
    z = _dot(q_h, k_blk, _NT)
    ls = _log_sigmoid(z)
    lm = ls - z
    if mask is not None:
        lm = jnp.where(mask, lm, 0.0)
    return ls, lm, jnp.sum(lm, axis=1, keepdims=True)


def _sb_weights(ls, lm, mask, tri_strict, later):
    suffix = _dot_exact_rhs(lm, tri_strict)
    w = jnp.exp(ls + suffix + later)
    return w if mask is None else jnp.where(mask, w, 0.0)


def _rows_from(x, lo, half):
    return x if not lo else jnp.concatenate([x[lo:half], x[half + lo:]], axis=0)


def _add_from(x, lo, half, upd):
    if not lo:
        return x + upd
    n = half - lo
    return jnp.concatenate([x[:lo], x[lo:half] + upd[:n], x[half:half + lo], x[half + lo:] + upd[n:]], axis=0)


def _on_grid_step(p, i):
    return jnp.logical_and(pl.program_id(0) == p, pl.program_id(1) == i)


def sb_fwd(qkv, *, name, gather=None):
    t, d3 = qkv.shape
    d = d3 // 3
    head_dim = d // SB_HEADS
    qb, kb_rows = SB_FWD_Q_BLOCK, SB_BLOCK
    assert 2 * head_dim == LANES and t % qb == 0
    pairs = d // LANES
    per_q = qb // kb_rows
    nq = t // qb
    scale = head_dim ** -0.5

    def body(q_ref, k_ref, v_ref, *rest):
        if gather is None:
            compute(q_ref, k_ref, v_ref, *rest)
            return
        x_ref, o_ref, tot_ref, out_ref, send_sems, recv_sems, local_sem = rest
        start, forward, finish = _gather_plan(x_ref, out_ref, send_sems, recv_sems, local_sem)
        pl.when(_on_grid_step(0, 0))(start)
        pl.when(_on_grid_step(7 * pairs // 8, 0))(forward)
        compute(q_ref, k_ref, v_ref, o_ref, tot_ref)
        pl.when(_on_grid_step(pairs - 1, nq - 1))(finish)

    def compute(q_ref, k_ref, v_ref, o_ref, tot_ref):
        qi = pl.program_id(1)
        lane = lax.broadcasted_iota(jnp.int32, (qb, LANES), 1)
        first = lane < head_dim
        q = q_ref[...] * scale
        q2 = jnp.concatenate([jnp.where(first, q, jnp.zeros_like(q)), jnp.where(first, jnp.zeros_like(q), q)], axis=0)
        rows, cols = _sb_block_iota()
        tri = jnp.where(rows > cols, 1.0, 0.0).astype(BF16)
        q_row = lax.broadcasted_iota(jnp.int32, (qb, kb_rows), 0)
        k_col = lax.broadcasted_iota(jnp.int32, (qb, kb_rows), 1)
        diagonal = [jnp.concatenate([m * kb_rows + k_col < q_row] * 2, axis=0) for m in range(per_q)]

        def step(kb, carry, mask):
            ks = pl.multiple_of(kb * SB_BLOCK, SB_BLOCK)
            k_blk = k_ref[pl.ds(ks, SB_BLOCK), :]
            v_blk = v_ref[pl.ds(ks, SB_BLOCK), :]
            acc, later = carry
            ls, lm, row = _sb_logits(q2, k_blk, mask)
            w = _sb_weights(ls, lm, mask, tri, later)
            return acc + _dot(w, v_blk), later + row

        out = (jnp.zeros((2 * qb, LANES), F32), jnp.zeros((2 * qb, 1), F32))
        for m in reversed(range(per_q)):
            out = step(per_q * qi + m, out, diagonal[m])
        acc, total = lax.fori_loop(0, per_q * qi, lambda i, carry: step(per_q * qi - 1 - i, carry, None), out)
        o_ref[...] = jnp.where(first, acc[:qb], acc[qb:]).astype(o_ref.dtype)
        tot_ref[...] = jnp.where(first, total[:qb], total[qb:])

    blk = lambda off: pl.BlockSpec((t, LANES), lambda p, i: (0, off + p))
    qblk = pl.BlockSpec((qb, LANES), lambda p, i: (i, p))
    in_specs, operands = [qblk, blk(pairs), blk(2 * pairs)], [qkv, qkv, qkv]
    out_specs = [qblk, qblk]
    out_shape = [jax.ShapeDtypeStruct((t, d), BF16), jax.ShapeDtypeStruct((t, d), F32)]
    if gather is None:
        return pl.pallas_call(
            body, name=name, grid=(pairs, nq), in_specs=in_specs, out_specs=out_specs, out_shape=out_shape,
            compiler_params=_params("parallel", "arbitrary"),
        )(*operands)
    whole = pl.BlockSpec(memory_space=pl.ANY)
    return pl.pallas_call(
        body, name=name, grid=(pairs, nq), in_specs=in_specs + [whole], out_specs=out_specs + [whole],
        out_shape=out_shape + [jax.ShapeDtypeStruct((N_DEV * gather.shape[0], gather.shape[1]), gather.dtype)],
        scratch_shapes=_comm_scratch(), compiler_params=_params("arbitrary", "arbitrary"),
    )(*operands, gather)


def sb_bwd(qkv, tot, do, *, name, exchange=None):
    t, d3 = qkv.shape
    d = d3 // 3
    head_dim = d // SB_HEADS
    pairs = d // LANES
    qb, kb_rows = SB_BWD_Q_BLOCK, SB_BLOCK
    per_q = qb // kb_rows
    nq = t // qb
    scale = head_dim ** -0.5

    def body(*refs):
        if exchange is None:
            compute(*refs)
            return
        q_ref, k_ref, v_ref, tot_ref, do_ref, in_ref, dq_ref, dk_ref, dv_ref, out_ref = refs[:10]
        dk_sum, dv_sum, send_sems, recv_sems, local_sem = refs[10:]
        start, finish = _exchange_plan(in_ref, out_ref, send_sems, recv_sems, local_sem)
        pl.when(_on_grid_step(0, 0))(start)
        compute(q_ref, k_ref, v_ref, tot_ref, do_ref, dq_ref, dk_ref, dv_ref, dk_sum, dv_sum)
        pl.when(_on_grid_step(pairs - 1, nq - 1))(finish)

    def compute(q_ref, k_ref, v_ref, tot_ref, do_ref, dq_ref, dk_ref, dv_ref, dk_sum, dv_sum):
        qi = pl.program_id(1)

        @pl.when(qi == 0)
        def _():
            dk_sum[...] = jnp.zeros_like(dk_sum)
            dv_sum[...] = jnp.zeros_like(dv_sum)

        lane = lax.broadcasted_iota(jnp.int32, (qb, LANES), 1)
        first = lane < head_dim
        q, dov, totv = q_ref[...] * scale, do_ref[...], tot_ref[...]
        second = jnp.logical_not(first)
        q2 = jnp.concatenate([jnp.where(s, q, jnp.zeros_like(q)) for s in (first, second)], axis=0)
        do2 = jnp.concatenate([jnp.where(s, dov, jnp.zeros_like(dov)) for s in (first, second)], axis=0)
        tot2 = jnp.concatenate([totv[:, 0:1], totv[:, head_dim:head_dim + 1]], axis=0)
        rows, cols = _sb_block_iota()
        tri_strict = jnp.where(rows > cols, 1.0, 0.0).astype(BF16)
        tri_before = jnp.where(rows < cols, 1.0, 0.0).astype(BF16)
        q_row = lax.broadcasted_iota(jnp.int32, (qb, kb_rows), 0)
        k_col = lax.broadcasted_iota(jnp.int32, (qb, kb_rows), 1)
        earlier = k_col < q_row

        def step(kb, carry, lo=None):
            ks = pl.multiple_of(kb * SB_BLOCK, SB_BLOCK)
            k_blk = k_ref[pl.ds(ks, SB_BLOCK), :]
            v_blk = v_ref[pl.ds(ks, SB_BLOCK), :]
            dq, seen, before = carry
            mask = None if lo is None else jnp.concatenate([earlier[:qb - lo]] * 2, axis=0)
            q_s, do_s = _rows_from(q2, lo, qb), _rows_from(do2, lo, qb)
            ls, lm, row = _sb_logits(q_s, k_blk, mask)
            seen = _add_from(seen, lo, qb, row)
            w = _sb_weights(ls, lm, mask, tri_strict, _rows_from(tot2 - seen, lo, qb))
            da = _dot(do_s, v_blk, _NT) * w
            g = _dot_exact_rhs(da, tri_before) + _rows_from(before, lo, qb)
            dz = da - jnp.exp(ls) * (da + g)
            if mask is not None:
                dz = jnp.where(mask, dz, 0.0)
            dk_sum[pl.ds(ks, SB_BLOCK), :] += _dot(dz, q_s, _TN)
            dv_sum[pl.ds(ks, SB_BLOCK), :] += _dot(w, do_s, _TN)
            return (_add_from(dq, lo, qb, _dot(dz, k_blk * scale)), seen,
                    _add_from(before, lo, qb, jnp.sum(da, axis=1, keepdims=True)))

        zero = jnp.zeros((2 * qb, LANES), F32)
        zcol = jnp.zeros((2 * qb, 1), F32)
        out = lax.fori_loop(0, per_q * qi, step, (zero, zcol, zcol))
        for m in range(per_q):
            out = step(per_q * qi + m, out, m * kb_rows)
        dq = out[0]
        dq_ref[...] = jnp.where(first, dq[:qb], dq[qb:]).astype(dq_ref.dtype)

        @pl.when(qi == nq - 1)
        def _():
            dk_ref[...] = dk_sum[...].astype(dk_ref.dtype)
            dv_ref[...] = dv_sum[...].astype(dv_ref.dtype)

    qblk = pl.BlockSpec((qb, LANES), lambda p, i: (i, p))
    col = lambda off: pl.BlockSpec((t, LANES), lambda p, i: (0, off + p))
    shape = jax.ShapeDtypeStruct((t, d), BF16)
    in_specs, operands = [qblk, col(pairs), col(2 * pairs), qblk, qblk], [qkv, qkv, qkv, tot, do]
    out_specs, out_shape = [qblk, col(0), col(0)], [shape, shape, shape]
    sums = [pltpu.VMEM((t, LANES), F32), pltpu.VMEM((t, LANES), F32)]
    if exchange is None:
        return pl.pallas_call(
            body, name=name, grid=(pairs, nq), in_specs=in_specs, out_specs=out_specs, out_shape=out_shape,
            scratch_shapes=sums, compiler_params=_params("parallel", "arbitrary"),
        )(*operands)
    whole = pl.BlockSpec(memory_space=pl.ANY)
    return pl.pallas_call(
        body, name=name, grid=(pairs, nq), in_specs=in_specs + [whole], out_specs=out_specs + [whole],
        out_shape=out_shape + [jax.ShapeDtypeStruct(exchange.shape, exchange.dtype)],
        scratch_shapes=sums + _comm_scratch(), compiler_params=_params("arbitrary", "arbitrary"),
    )(*operands, exchange)


def _shift_down(x, s):
    rows = lax.broadcasted_iota(jnp.int32, x.shape, 0)
    return jnp.where(rows >= s, pltpu.roll(x, s, 0), 0.0)


def _shift_up(x, s):
    t = x.shape[0]
    rows = lax.broadcasted_iota(jnp.int32, x.shape, 0)
    return jnp.where(rows < t - s, pltpu.roll(x, t - s, 0), 0.0)


def conv_fwd(bcu, w, *, name):
    t, d3 = bcu.shape
    d = d3 // 3
    nb = d // LANES
    col = lambda off: pl.BlockSpec((t, LANES), lambda j: (0, off + j))

    def body(b_ref, c_ref, u_ref, w_ref, y_ref):
        hh = c_ref[...] * u_ref[...]
        conv = w_ref[0:1, :] * _shift_down(hh, 2) + w_ref[1:2, :] * _shift_down(hh, 1) + w_ref[2:3, :] * hh
        y_ref[...] = (b_ref[...] * conv).astype(y_ref.dtype)

    return pl.pallas_call(
        body, name=name, grid=(nb,),
        in_specs=[col(0), col(nb), col(2 * nb), pl.BlockSpec((CONV_WIDTH, LANES), lambda j: (0, j))],
        out_specs=col(0), out_shape=jax.ShapeDtypeStruct((t, d), BF16), compiler_params=_params("parallel"),
    )(bcu, bcu, bcu, w)


def conv_bwd(bcu, w, dy, *, name):
    t, d3 = bcu.shape
    d = d3 // 3
    nb = d // LANES
    col = lambda off: pl.BlockSpec((t, LANES), lambda j: (0, off + j))
    wspec = pl.BlockSpec((CONV_WIDTH, LANES), lambda j: (0, j))

    def body(b_ref, c_ref, u_ref, w_ref, dy_ref, db_ref, dc_ref, du_ref, dw_ref):
        c, u, dyv = c_ref[...], u_ref[...], dy_ref[...]
        hh = c * u
        h2, h1 = _shift_down(hh, 2), _shift_down(hh, 1)
        w0, w1, w2 = w_ref[0:1, :], w_ref[1:2, :], w_ref[2:3, :]
        db_ref[...] = (dyv * (w0 * h2 + w1 * h1 + w2 * hh)).astype(db_ref.dtype)
        dconv = dyv * b_ref[...]
        dhh = w2 * dconv + w1 * _shift_up(dconv, 1) + w0 * _shift_up(dconv, 2)
        dc_ref[...] = (dhh * u).astype(dc_ref.dtype)
        du_ref[...] = (dhh * c).astype(du_ref.dtype)
        dw_ref[0:1, :] = jnp.sum(dconv * h2, axis=0, keepdims=True)
        dw_ref[1:2, :] = jnp.sum(dconv * h1, axis=0, keepdims=True)
        dw_ref[2:3, :] = jnp.sum(dconv * hh, axis=0, keepdims=True)

    shape = jax.ShapeDtypeStruct((t, d), BF16)
    return pl.pallas_call(
        body, name=name, grid=(nb,),
        in_specs=[col(0), col(nb), col(2 * nb), wspec, col(0)],
        out_specs=[col(0), col(0), col(0), wspec],
        out_shape=[shape, shape, shape, jax.ShapeDtypeStruct((CONV_WIDTH, d), F32)],
        compiler_params=_params("parallel"),
    )(bcu, bcu, bcu, w, dy)


def _gla_chunk(q, k, lg, scale):
    c = GLA_CHUNK
    rows = lax.broadcasted_iota(jnp.int32, (c, c), 0)
    cols = lax.broadcasted_iota(jnp.int32, (c, c), 1)
    causal = rows >= cols
    tril = jnp.where(causal, 1.0, 0.0).astype(BF16)
    q = q * scale
    cum = _dot_exact_lhs(tril, lg)
    last = cum[c - 1:c, :]
    eq = jnp.exp(cum)
    el = jnp.exp(last - cum)
    return causal, tril, q, k, cum, lg, last, eq, el


def _gla_sub_blocks(q, k, cum, lg):
    key_row = lax.broadcasted_iota(jnp.int32, (GLA_CHUNK, 1), 0)
    out = []
    for lo in range(0, GLA_CHUNK, GLA_SUB):
        hi = lo + GLA_SUB
        ref = cum[lo:lo + 1, :] - lg[lo:lo + 1, :]
        eq = jnp.exp(cum[lo:hi] - ref)
        ek = jnp.where(key_row < hi, jnp.exp(ref - cum), 0.0)
        out.append((slice(lo, hi), eq, ek, q[lo:hi] * eq, k * ek))
    return out


def _gla_scores(subs, causal):
    return jnp.where(causal, jnp.concatenate([_dot(qt, kt, _NT) for _, _, _, qt, kt in subs], axis=0), 0.0)


def gla_fwd(proj, lg, *, name):
    t, d3 = proj.shape
    d = d3 // 3
    dk, dv = d // 2 // GLA_HEADS, d // GLA_HEADS
    assert dk == LANES and dv == 2 * LANES
    c = GLA_CHUNK
    nc = t // c
    scale = dk ** -0.5
    nh = GLA_HEADS

    def body(q_ref, k_ref, v_ref, lg_ref, o_ref, st_out_ref, st_ref):
        @pl.when(pl.program_id(0) == 0)
        def _():
            st_ref[...] = jnp.zeros_like(st_ref)

        for h in range(nh):
            kcols, vcols = slice(h * dk, (h + 1) * dk), slice(h * dv, (h + 1) * dv)
            causal, _, q, k, cum, lg, last, eq, el = _gla_chunk(q_ref[:, kcols], k_ref[:, kcols], lg_ref[:, kcols], scale)
            v = v_ref[:, vcols]
            st = st_ref[h]
            st_out_ref[h] = st
            scores = _gla_scores(_gla_sub_blocks(q, k, cum, lg), causal)
            o_ref[:, vcols] = _dot(q * eq, st, _NT) + _dot(scores, v)
            st_ref[h] = st * jnp.exp(last) + _dot(v, k * el, _TN)

    half, full = pl.BlockSpec((c, d // 2), lambda i: (i, 0)), pl.BlockSpec((c, d), lambda i: (i, 0))
    return pl.pallas_call(
        body, name=name, grid=(nc,),
        in_specs=[half, pl.BlockSpec((c, d // 2), lambda i: (i, 1)), pl.BlockSpec((c, d), lambda i: (i, 1)), half],
        out_specs=[full, pl.BlockSpec((nh, None, dv, dk), lambda i: (0, i, 0, 0))],
        out_shape=[jax.ShapeDtypeStruct((t, d), F32), jax.ShapeDtypeStruct((nh, nc, dv, dk), F32)],
        scratch_shapes=[pltpu.VMEM((nh, dv, dk), F32)], compiler_params=_params("arbitrary"),
    )(proj, proj, proj, lg)


def gla_bwd(proj, lg, states, do, *, name):
    t, d3 = proj.shape
    d = d3 // 3
    dk, dv = d // 2 // GLA_HEADS, d // GLA_HEADS
    c = GLA_CHUNK
    nc = t // c
    scale = dk ** -0.5
    nh = GLA_HEADS

    def body(q_ref, k_ref, v_ref, lg_ref, st_ref, do_ref, dq_ref, dk_ref, dv_ref, dzg_ref, dbg_ref, dst_ref):
        @pl.when(pl.program_id(0) == 0)
        def _():
            dst_ref[...] = jnp.zeros_like(dst_ref)
            dbg_ref[...] = jnp.zeros_like(dbg_ref)

        for h in range(nh):
            kcols, vcols = slice(h * dk, (h + 1) * dk), slice(h * dv, (h + 1) * dv)
            causal, tril, q, k, cum, lg, last, eq, el = _gla_chunk(q_ref[:, kcols], k_ref[:, kcols], lg_ref[:, kcols], scale)
            v, st, dov, dst = v_ref[:, vcols], st_ref[h], do_ref[:, vcols], dst_ref[h]
            subs = _gla_sub_blocks(q, k, cum, lg)
            qt, kh = q * eq, k * el
            scores = _gla_scores(subs, causal)
            dscores = jnp.where(causal, _dot(dov, v, _NT), 0.0)
            dq_parts = []
            dkh = _dot(v, dst)
            dk_h = dkh * el
            for rows, eq_sub, ek_sub, qt_sub, kt_sub in subs:
                dq_parts.append(_dot(dscores[rows], kt_sub) * eq_sub)
                dk_h = dk_h + _dot(dscores[rows], qt_sub, _TN) * ek_sub
            dq = _dot(dov, st) * eq + jnp.concatenate(dq_parts, axis=0)
            dv_ref[:, vcols] = (_dot(scores, dov, _TN) + _dot(kh, dst, _NT)).astype(dv_ref.dtype)
            dq_ref[:, kcols] = (dq * scale).astype(dq_ref.dtype)
            dk_ref[:, kcols] = dk_h.astype(dk_ref.dtype)
            e_last = jnp.exp(last)
            dlast = jnp.sum(kh * dkh, axis=0, keepdims=True) + e_last * jnp.sum(dst * st, axis=0, keepdims=True)
            dcum = q * dq - k * dk_h
            dlg = _dot_exact_lhs(tril, dcum, _TN) + dlast
            dzg = dlg * (1.0 - jnp.exp(lg * GLA_GATE_NORMALIZER)) / GLA_GATE_NORMALIZER
            dzg_ref[:, kcols] = dzg.astype(dzg_ref.dtype)
            dbg_ref[:, kcols] += jnp.sum(dzg, axis=0, keepdims=True)
            dst_ref[h] = dst * e_last + _dot(dov, qt, _TN)

    rev = lambda i: nc - 1 - i
    half, full = pl.BlockSpec((c, d // 2), lambda i: (rev(i), 0)), pl.BlockSpec((c, d), lambda i: (rev(i), 0))
    half_shape = jax.ShapeDtypeStruct((t, d // 2), BF16)
    return pl.pallas_call(
        body, name=name, grid=(nc,),
        in_specs=[half, pl.BlockSpec((c, d // 2), lambda i: (rev(i), 1)), pl.BlockSpec((c, d), lambda i: (rev(i), 1)), half,
                  pl.BlockSpec((nh, None, dv, dk), lambda i: (0, rev(i), 0, 0)), full],
        out_specs=[half, half, full, half, pl.BlockSpec((1, d // 2), lambda i: (0, 0))],
        out_shape=[half_shape, half_shape, jax.ShapeDtypeStruct((t, d), BF16), half_shape,
                   jax.ShapeDtypeStruct((1, d // 2), F32)],
        scratch_shapes=[pltpu.VMEM((nh, dv, dk), F32)], compiler_params=_params("arbitrary"),
    )(proj, proj, proj, lg, states, do)


def gla_post_fwd(o, proj, head_norm, *, name, tm=ROW_BLOCK):
    t, d = o.shape
    dv = d // GLA_HEADS
    tm = min(tm, t)

    def body(o_ref, g_ref, hn_ref, y_ref):
        for h in range(GLA_HEADS):
            sl = slice(h * dv, (h + 1) * dv)
            ov, gv = o_ref[:, sl], g_ref[:, sl]
            r = lax.rsqrt(jnp.mean(ov * ov, axis=-1, keepdims=True) + RMS_EPS)
            y_ref[:, sl] = ((ov * r * hn_ref[:, sl]) * (gv * _sigmoid(gv))).astype(y_ref.dtype)

    row = pl.BlockSpec((tm, d), lambda i: (i, 0))
    return pl.pallas_call(
        body, name=name, grid=(t // tm,),
        in_specs=[row, pl.BlockSpec((tm, d), lambda i: (i, 2)), pl.BlockSpec((1, d), lambda i: (0, 0))],
        out_specs=row, out_shape=jax.ShapeDtypeStruct((t, d), BF16), compiler_params=_params("parallel"),
    )(o, proj, head_norm)


def gla_post_bwd(o, proj, head_norm, dy, *, name, tm=ROW_BLOCK):
    t, d = o.shape
    dv = d // GLA_HEADS
    tm = min(tm, t)

    def body(o_ref, g_ref, hn_ref, dy_ref, do_ref, dg_ref, dhn_ref):
        @pl.when(pl.program_id(0) == 0)
        def _():
            dhn_ref[...] = jnp.zeros_like(dhn_ref)

        for h in range(GLA_HEADS):
            sl = slice(h * dv, (h + 1) * dv)
            ov, gv, dyv, hn = o_ref[:, sl], g_ref[:, sl], dy_ref[:, sl], hn_ref[:, sl]
            r = lax.rsqrt(jnp.mean(ov * ov, axis=-1, keepdims=True) + RMS_EPS)
            sg = _sigmoid(gv)
            silu = gv * sg
            on = ov * r * hn
            dg_ref[:, sl] = (dyv * on * (sg * (1.0 + gv * (1.0 - sg)))).astype(dg_ref.dtype)
            don = dyv * silu
            u = don * hn
            do_ref[:, sl] = (r * u - ov * (r * r * r * jnp.mean(u * ov, axis=-1, keepdims=True))).astype(do_ref.dtype)
            dhn_ref[:, sl] += jnp.sum(don * ov * r, axis=0, keepdims=True)

    row = pl.BlockSpec((tm, d), lambda i: (i, 0))
    vec = pl.BlockSpec((1, d), lambda i: (0, 0))
    shape = jax.ShapeDtypeStruct((t, d), BF16)
    return pl.pallas_call(
        body, name=name, grid=(t // tm,),
        in_specs=[row, pl.BlockSpec((tm, d), lambda i: (i, 2)), vec, row],
        out_specs=[row, row, vec], out_shape=[shape, shape, jax.ShapeDtypeStruct((1, d), F32)],
        compiler_params=_params("arbitrary"),
    )(o, proj, head_norm, dy)


def _ffn_fwd(h, gains, w_up, w_down, tag):
    xn = rms_fwd(h, gains[2], name=f"{tag}_ffn_norm", out_dtype=BF16)
    act = matmul(xn, w_up, mode='nn', epi='relu2', out_dtype=BF16, name=f"{tag}_ffn_up")
    f = matmul(act, w_down, mode='nn', name=f"{tag}_ffn_down")
    h_out = rms_fwd(f, gains[3], res=h, name=f"{tag}_ffn_out")
    return h_out, (h, xn, act, f)


def _ffn_bwd(dh, saved, gains, w_up, w_down, tag):
    h, xn, act, f = saved
    df, dg3 = rms_bwd(f, gains[3], dh, out_dtype=BF16, name=f"{tag}_ffn_out_bwd")
    du = matmul(df, w_down, mode='nt', epi='mul2sqrt', extra=act, out_dtype=BF16, name=f"{tag}_ffn_da")
    dw_down = matmul(act, df, mode='tn', out_dtype=BF16, name=f"{tag}_ffn_dwdown")
    dw_up = matmul(xn, du, mode='tn', out_dtype=BF16, name=f"{tag}_ffn_dwup")
    dxn = matmul(du, w_up, mode='nt', name=f"{tag}_ffn_dxn")
    dh_in, dg2 = rms_bwd(h, gains[2], dxn, dres=dh, name=f"{tag}_ffn_norm_bwd")
    return dh_in, dg2, dg3, dw_up, dw_down


def _sb_layer_fwd(xn, w, j, tag, comm=None):
    qkv = matmul(xn, w['sb_w_qkv'][j], mode='nn', out_dtype=BF16, name=f"{tag}_qkv")
    if comm is None:
        o, tot = sb_fwd(qkv, name=f"{tag}_sb")
    else:
        o, tot, gathered = sb_fwd(qkv, name=f"{tag}_sb", gather=comm.rest_payload)
        comm.on_gathered(gathered)
    m = matmul(o, w['sb_w_o'][j], mode='nn', name=f"{tag}_wo")
    return m, (qkv, o, tot)


def _sb_layer_bwd(dm, xn, saved, w, j, tag, comm=None, grads=None):
    qkv, o, tot = saved
    do = matmul(dm, w['sb_w_o'][j], mode='nt', out_dtype=BF16, name=f"{tag}_do")
    dw_o = matmul(o, dm, mode='tn', out_dtype=BF16, name=f"{tag}_dwo")
    if comm is None:
        dq, dk, dv = sb_bwd(qkv, tot, do, name=f"{tag}_sb_bwd")
    else:
        parts = comm.rest_parts({**grads, ('sb_w_o', j): dw_o})
        dq, dk, dv, received = sb_bwd(qkv, tot, do, name=f"{tag}_sb_bwd", exchange=parts)
        comm.on_received(received)
    dqkv = jnp.concatenate([dq, dk, dv], axis=1)
    dw_qkv = matmul(xn, dqkv, mode='tn', out_dtype=BF16, name=f"{tag}_dwqkv")
    dxn = matmul(dqkv, w['sb_w_qkv'][j], mode='nt', name=f"{tag}_dxn")
    return dxn, {('sb_w_qkv', j): dw_qkv, ('sb_w_o', j): dw_o}


def _conv_layer_fwd(xn, w, j, tag):
    bcu = matmul(xn, w['conv_w_in'][j], mode='nn', name=f"{tag}_in")
    y = conv_fwd(bcu, w['conv_w'][j], name=f"{tag}_conv")
    m = matmul(y, w['conv_w_out'][j], mode='nn', name=f"{tag}_out")
    return m, (bcu, y)


def _conv_layer_bwd(dm, xn, saved, w, j, tag):
    bcu, y = saved
    dy = matmul(dm, w['conv_w_out'][j], mode='nt', name=f"{tag}_dy")
    dw_out = matmul(y, dm, mode='tn', out_dtype=BF16, name=f"{tag}_dwout")
    db, dc, du, dw_conv = conv_bwd(bcu, w['conv_w'][j], dy, name=f"{tag}_conv_bwd")
    dbcu = jnp.concatenate([db, dc, du], axis=1)
    dw_in = matmul(xn, dbcu, mode='tn', out_dtype=BF16, name=f"{tag}_dwin")
    dxn = matmul(dbcu, w['conv_w_in'][j], mode='nt', name=f"{tag}_dxn")
    return dxn, {('conv_w_in', j): dw_in, ('conv_w', j): dw_conv, ('conv_w_out', j): dw_out}


def _gla_split(w_in, w_gate_up):
    d = w_in.shape[0]
    w_main = w_in[:, :3 * d]
    w_a = jnp.pad(w_in[:, 3 * d:], ((0, 0), (0, LANES - GLA_GATE_RANK)))
    w_gu = jnp.pad(w_gate_up, ((0, LANES - GLA_GATE_RANK), (0, 0)))
    return w_main, w_a, w_gu


def _gla_layer_fwd(xn, w, j, tag):
    w_main, w_a, w_gu = _gla_split(w['gla_w_in'][j], w['gla_w_gate_up'][j])
    proj = matmul(xn, w_main, mode='nn', name=f"{tag}_in")
    a_low = matmul(xn, w_a, mode='nn', out_dtype=BF16, name=f"{tag}_alow")
    lg = matmul(a_low, w_gu, mode='nn', epi='logsig16', extra=w['gla_b_gate'][j][None, :], name=f"{tag}_gate")
    o, states = gla_fwd(proj, lg, name=f"{tag}_gla")
    hn = w['gla_head_norm'][j].reshape(1, -1)
    y = gla_post_fwd(o, proj, hn, name=f"{tag}_post")
    m = matmul(y, w['gla_w_o'][j], mode='nn', name=f"{tag}_wo")
    return m, (proj, a_low, lg, o, states, y)


def _gla_layer_bwd(dm, xn, saved, w, j, tag):
    proj, a_low, lg, o, states, y = saved
    w_main, w_a, w_gu = _gla_split(w['gla_w_in'][j], w['gla_w_gate_up'][j])
    hn = w['gla_head_norm'][j].reshape(1, -1)
    dy = matmul(dm, w['gla_w_o'][j], mode='nt', name=f"{tag}_dy")
    dw_o = matmul(y, dm, mode='tn', out_dtype=BF16, name=f"{tag}_dwo")
    do, dg, dhn = gla_post_bwd(o, proj, hn, dy, name=f"{tag}_post_bwd")
    dq, dk, dv, dzg, dbg = gla_bwd(proj, lg, states, do, name=f"{tag}_gla_bwd")
    da_low = matmul(dzg, w_gu, mode='nt', out_dtype=BF16, name=f"{tag}_dalow")
    dw_gu = matmul(a_low, dzg, mode='tn', out_dtype=BF16, name=f"{tag}_dwgu")[:GLA_GATE_RANK]
    dproj = jnp.concatenate([dq, dk, dv, dg], axis=1)
    dw_main = matmul(xn, dproj, mode='tn', out_dtype=BF16, name=f"{tag}_dwin")
    dw_a = matmul(xn, da_low, mode='tn', out_dtype=BF16, name=f"{tag}_dwa")[:, :GLA_GATE_RANK]
    dxn_a = matmul(da_low, w_a, mode='nt', name=f"{tag}_dxn_a")
    dxn = matmul(dproj, w_main, mode='nt', epi='add', extra=dxn_a, name=f"{tag}_dxn")
    grads = {('gla_w_in', j): jnp.concatenate([dw_main, dw_a], axis=1), ('gla_w_gate_up', j): dw_gu,
             ('gla_b_gate', j): dbg[0], ('gla_head_norm', j): dhn.reshape(w['gla_head_norm'][j].shape),
             ('gla_w_o', j): dw_o}
    return dxn, grads


_MIXERS = ((_sb_layer_fwd, _sb_layer_bwd), (_conv_layer_fwd, _conv_layer_bwd), (_gla_layer_fwd, _gla_layer_bwd))


def local_step(x, w, target, comm=None):
    depth = len(w['norm_gains'])
    h = x
    tape = []
    for i in range(depth):
        kind, j = i % 3, i // 3
        tag = f"l{i}"
        extra = {'comm': comm} if (comm is not None and i == 0) else {}
        gains = [w['norm_gains'][i][s][None, :] for s in range(4)]
        xn = rms_fwd(h, gains[0], name=f"{tag}_mix_norm", out_dtype=BF16)
        m, saved = _MIXERS[kind][0](xn, w, j, tag, **extra)
        h_mid = rms_fwd(m, gains[1], res=h, name=f"{tag}_mix_out")
        h_out, ffn_saved = _ffn_fwd(h_mid, gains, w['ffn_w_up'][i], w['ffn_w_down'][i], tag)
        tape.append((h, xn, m, saved, ffn_saved, gains))
        h = h_out
    loss, dh = loss_head(h, target, name="loss_head")

    grads = {}
    for i in reversed(range(depth)):
        kind, j = i % 3, i // 3
        tag = f"l{i}"
        h_in, xn, m, saved, ffn_saved, gains = tape[i]
        dg = [None] * 4
        dh, dg[2], dg[3], grads[('ffn_w_up', i)], grads[('ffn_w_down', i)] = _ffn_bwd(
            dh, ffn_saved, gains, w['ffn_w_up'][i], w['ffn_w_down'][i], tag)
        dm, dg[1] = rms_bwd(m, gains[1], dh, out_dtype=BF16, name=f"{tag}_mix_out_bwd")
        extra = {'comm': comm, 'grads': grads} if (comm is not None and i == 0) else {}
        dxn, g = _MIXERS[kind][1](dm, xn, saved, w, j, tag, **extra)
        grads.update(g)
        dh, dg[0] = rms_bwd(h_in, gains[0], dxn, dres=dh, name=f"{tag}_mix_norm_bwd")
        grads[('norm_gains', i)] = jnp.concatenate(dg, axis=0)
    return loss, dh, grads


def _segments(keys, shard_shapes, f32_as_pairs):
    segs, row = {}, 0
    for name, lo, hi in keys:
        n = (hi - lo) * math.prod(shard_shapes[name][1:]) * (2 if f32_as_pairs and name in F32_PAYLOAD else 1)
        nrows = -(-n // (PACK_COLS * PACK_ROW_ALIGN)) * PACK_ROW_ALIGN
        segs[(name, lo, hi)] = (row, nrows, n)
        row += nrows
    return segs, -(-row // PACK_ROW_BLOCK) * PACK_ROW_BLOCK


def _pack(parts, segs, total_rows, dtype):
    pieces, row = [], 0
    for key in segs:
        _, nrows, n = segs[key]
        p = parts[key].astype(dtype)
        lead = p.shape[:-1]
        if nrows * PACK_COLS > n:
            p = jnp.pad(p, [(0, 0)] * len(lead) + [(0, nrows * PACK_COLS - n)])
        pieces.append(p.reshape(lead + (nrows, PACK_COLS)))
        row += nrows
    if total_rows > row:
        pieces.append(jnp.zeros(lead + (total_rows - row, PACK_COLS), dtype))
    return jnp.concatenate(pieces, axis=-2)


def _unpack(buf, seg):
    first, nrows, n = seg
    piece = buf[..., first:first + nrows, :]
    return piece.reshape(piece.shape[:-2] + (nrows * PACK_COLS,))[..., :n]


def _unshard(gathered, axis):
    moved = jnp.moveaxis(gathered, 0, axis)
    shape = moved.shape
    return moved.reshape(shape[:axis] + (shape[axis] * shape[axis + 1],) + shape[axis + 2:])


def _shard_split(full, axis):
    shape = full.shape
    cut = full.reshape(shape[:axis] + (N_DEV, shape[axis] // N_DEV) + shape[axis + 1:])
    return jnp.moveaxis(cut, axis, 0)


def _mesh_position():
    return lax.axis_index("x"), lax.axis_index("y"), lax.axis_index("c")


def _comm_scratch():
    return [pltpu.SemaphoreType.DMA((N_DEV - 1,)), pltpu.SemaphoreType.DMA((N_DEV - 1,)), pltpu.SemaphoreType.DMA]


def _gather_plan(x_ref, out_ref, send_sems, recv_sems, local_sem):
    rows = x_ref.shape[0]
    x, y, c = _mesh_position()
    me, sibling = (x, y, c), (x, y, 1 - c)
    chips = [(1 - x, y), (x, 1 - y), (1 - x, 1 - y)]

    def block(px, py, pc):
        return out_ref.at[pl.ds((4 * px + 2 * py + pc) * rows, rows), :]

    def copy(k, blk, to, src=None):
        return pltpu.make_async_remote_copy(
            src_ref=block(*blk) if src is None else src, dst_ref=block(*blk),
            send_sem=send_sems.at[k], recv_sem=recv_sems.at[k],
            device_id=to, device_id_type=pl.DeviceIdType.MESH)

    mine = pltpu.make_async_copy(x_ref, block(*me), local_sem)
    first = [copy(0, me, sibling, src=x_ref)]
    first += [copy(1 + j, me, (*chip, c), src=x_ref) for j, chip in enumerate(chips)]
    passed = [copy(4 + j, (*chip, c), sibling) for j, chip in enumerate(chips)]

    def start():
        mine.start()
        for cp in first:
            cp.start()

    def forward():
        for j, chip in enumerate(chips):
            copy(1 + j, (*chip, c), me).wait_recv()
            passed[j].start()

    def finish():
        copy(0, sibling, me).wait_recv()
        for j, chip in enumerate(chips):
            copy(4 + j, (*chip, 1 - c), me).wait_recv()
        for cp in first + passed:
            cp.wait_send()
        mine.wait()

    return start, forward, finish


def _exchange_plan(in_ref, out_ref, send_sems, recv_sems, local_sem):
    x, y, c = _mesh_position()
    my_id = 4 * x + 2 * y + c
    mine = pltpu.make_async_copy(in_ref.at[my_id], out_ref.at[my_id], local_sem)

    def copy(k, receive):
        px = 1 - x if k & 4 else x
        py = 1 - y if k & 2 else y
        pc = 1 - c if k & 1 else c
        peer_id = 4 * px + 2 * py + pc
        return pltpu.make_async_remote_copy(
            src_ref=in_ref.at[peer_id], dst_ref=out_ref.at[peer_id if receive else my_id],
            send_sem=send_sems.at[k - 1], recv_sem=recv_sems.at[k - 1],
            device_id=(px, py, pc), device_id_type=pl.DeviceIdType.MESH)

    def start():
        mine.start()
        for k in range(1, N_DEV):
            copy(k, False).start()

    def finish():
        for k in range(1, N_DEV):
            copy(k, True).wait_recv()
        for k in range(1, N_DEV):
            copy(k, False).wait_send()
        mine.wait()

    return start, finish


def all_gather(shard, *, name):
    rows, cols = shard.shape

    def body(x_ref, out_ref, send_sems, recv_sems, local_sem):
        start, forward, finish = _gather_plan(x_ref, out_ref, send_sems, recv_sems, local_sem)
        start()
        forward()
        finish()

    return pl.pallas_call(
        body, name=name, out_shape=jax.ShapeDtypeStruct((N_DEV * rows, cols), shard.dtype),
        in_specs=[pl.BlockSpec(memory_space=pl.ANY)], out_specs=pl.BlockSpec(memory_space=pl.ANY),
        scratch_shapes=_comm_scratch(),
    )(shard)


def adamw(parts, w, m, v, *, name, exchange=None):
    _, rows, cols = parts.shape
    tr = PACK_ROW_BLOCK
    steps = rows // tr
    c1 = 1.0 - ADAM_B1 ** ADAM_STEP
    c2 = 1.0 - ADAM_B2 ** ADAM_STEP

    def body(p_ref, w_ref, m_ref, v_ref, *rest):
        if exchange is None:
            update(p_ref, w_ref, m_ref, v_ref, *rest)
            return
        in_ref, g_ref, d_ref, nm_ref, nv_ref, out_ref, send_sems, recv_sems, local_sem = rest
        start, finish = _exchange_plan(in_ref, out_ref, send_sems, recv_sems, local_sem)
        pl.when(pl.program_id(0) == 0)(start)
        update(p_ref, w_ref, m_ref, v_ref, g_ref, d_ref, nm_ref, nv_ref)
        pl.when(pl.program_id(0) == steps - 1)(finish)

    def update(p_ref, w_ref, m_ref, v_ref, g_ref, d_ref, nm_ref, nv_ref):
        g = p_ref[0].astype(F32)
        for s in range(1, N_DEV):
            g = g + p_ref[s].astype(F32)
        nm = ADAM_B1 * m_ref[...] + (1.0 - ADAM_B1) * g
        nv = ADAM_B2 * v_ref[...] + (1.0 - ADAM_B2) * jnp.square(g)
        m_hat = nm / c1
        v_hat = nv / c2
        g_ref[...] = g
        d_ref[...] = -ADAM_LR * (m_hat / (jnp.sqrt(v_hat) + ADAM_EPS) + ADAM_WD * w_ref[...])
        nm_ref[...] = nm
        nv_ref[...] = nv

    row = pl.BlockSpec((tr, cols), lambda i: (i, 0))
    shape = jax.ShapeDtypeStruct((rows, cols), F32)
    in_specs = [pl.BlockSpec((N_DEV, tr, cols), lambda i: (0, i, 0)), row, row, row]
    if exchange is None:
        return pl.pallas_call(
            body, name=name, grid=(steps,), in_specs=in_specs,
            out_specs=[row, row, row, row], out_shape=[shape, shape, shape, shape],
            compiler_params=_params("parallel"),
        )(parts, w, m, v)
    whole = pl.BlockSpec(memory_space=pl.ANY)
    return pl.pallas_call(
        body, name=name, grid=(steps,), in_specs=in_specs + [whole],
        out_specs=[row, row, row, row, whole],
        out_shape=[shape, shape, shape, shape, jax.ShapeDtypeStruct(exchange.shape, exchange.dtype)],
        scratch_shapes=_comm_scratch(), compiler_params=_params("arbitrary"),
    )(parts, w, m, v, exchange)


def kernel(x, norm_gains, sb_w_qkv, sb_w_o, conv_w_in, conv_w, conv_w_out, gla_w_in, gla_w_gate_up, gla_b_gate, gla_head_norm, gla_w_o, ffn_w_up, ffn_w_down, loss_target, m_norm_gains, m_sb_w_qkv, m_sb_w_o, m_conv_w_in, m_conv_w, m_conv_w_out, m_gla_w_in, m_gla_w_gate_up, m_gla_b_gate, m_gla_head_norm, m_gla_w_o, m_ffn_w_up, m_ffn_w_down, v_norm_gains, v_sb_w_qkv, v_sb_w_o, v_conv_w_in, v_conv_w, v_conv_w_out, v_gla_w_in, v_gla_w_gate_up, v_gla_b_gate, v_gla_head_norm, v_gla_w_o, v_ffn_w_up, v_ffn_w_down):
    shards = dict(zip(WEIGHTS, (norm_gains, sb_w_qkv, sb_w_o, conv_w_in, conv_w, conv_w_out, gla_w_in,
                                gla_w_gate_up, gla_b_gate, gla_head_norm, gla_w_o, ffn_w_up, ffn_w_down)))
    moments_m = dict(zip(WEIGHTS, (m_norm_gains, m_sb_w_qkv, m_sb_w_o, m_conv_w_in, m_conv_w, m_conv_w_out,
                                   m_gla_w_in, m_gla_w_gate_up, m_gla_b_gate, m_gla_head_norm, m_gla_w_o,
                                   m_ffn_w_up, m_ffn_w_down)))
    moments_v = dict(zip(WEIGHTS, (v_norm_gains, v_sb_w_qkv, v_sb_w_o, v_conv_w_in, v_conv_w, v_conv_w_out,
                                   v_gla_w_in, v_gla_w_gate_up, v_gla_b_gate, v_gla_head_norm, v_gla_w_o,
                                   v_ffn_w_up, v_ffn_w_down)))
    shard_shapes = {n: a.shape for n, a in shards.items()}

    alone = [('norm_gains', 0, 1), ('sb_w_qkv', 0, 1)]
    rest = [(n, 1 if (n, 0, 1) in alone else 0, shard_shapes[n][0]) for n in WEIGHTS]
    groups = [alone, [piece for piece in rest if piece[1] < piece[2]]]

    def payload(group):
        segs, rows = _segments(group, shard_shapes, True)
        flat = {(n, lo, hi): (lax.bitcast_convert_type(shards[n][lo:hi], BF16) if n in F32_PAYLOAD
                              else shards[n][lo:hi].astype(BF16)).reshape(-1) for n, lo, hi in group}
        return segs, _pack(flat, segs, rows, BF16)

    whole = {n: [None] * shard_shapes[n][0] for n in WEIGHTS}

    def take_gathered(segs, gathered):
        gathered = gathered.reshape(N_DEV, -1, PACK_COLS)
        for n, lo, hi in segs:
            piece = _unpack(gathered, segs[(n, lo, hi)])
            if n in F32_PAYLOAD:
                piece = lax.bitcast_convert_type(piece.reshape(N_DEV, -1, 2), F32)
            full = _unshard(piece.reshape((N_DEV, hi - lo) + shard_shapes[n][1:]), SHARD_AXIS[n])
            for j in range(lo, hi):
                whole[n][j] = full[j - lo]

    segs0, payload0 = payload(groups[0])
    segs1, payload1 = payload(groups[1])
    take_gathered(segs0, all_gather(payload0, name="weights_all_gather"))

    gsegs = [_segments(group, shard_shapes, False) for group in groups]
    received = [None, None]

    def parts(g, grads):
        segs, rows = gsegs[g]
        flat = {(n, lo, hi): _shard_split(jnp.stack([grads[(n, j)] for j in range(lo, hi)]), SHARD_AXIS[n])
                .reshape(N_DEV, -1) for n, lo, hi in segs}
        return _pack(flat, segs, rows, BF16)

    def on_received(buf):
        received[1] = buf

    comm = types.SimpleNamespace(rest_payload=payload1, on_gathered=functools.partial(take_gathered, segs1),
                                 rest_parts=functools.partial(parts, 1), on_received=on_received)

    loss, grad_x, grads = local_step(x[0], whole, loss_target[0], comm)
    loss = lax.psum(loss[0, 0], ("x", "y", "c"))

    results = {n: [[] for _ in range(4)] for n in WEIGHTS}
    outs_of = {}
    for g in (1, 0):
        segs, rows = gsegs[g]

        def packed(source):
            return _pack({(n, lo, hi): source[n][lo:hi].reshape(-1) for n, lo, hi in segs}, segs, rows, F32)

        states = (packed(shards), packed(moments_m), packed(moments_v))
        if g == 1:
            *outs_of[g], received[0] = adamw(received[1], *states, name="adamw1", exchange=parts(0, grads))
        else:
            outs_of[g] = adamw(received[0], *states, name="adamw0")
    for g in range(2):
        segs, _ = gsegs[g]
        for n, lo, hi in segs:
            for o, buf in enumerate(outs_of[g]):
                results[n][o].append(_unpack(buf, segs[(n, lo, hi)]).reshape((hi - lo,) + shard_shapes[n][1:]))
    whole_out = [[r[0] if len(r) == 1 else jnp.concatenate(r, axis=0) for r in (results[n][o] for n in WEIGHTS)]
                 for o in range(4)]
    return (loss, grad_x[None], *whole_out[0], *whole_out[1], *whole_out[2], *whole_out[3])
```

```python
import functools
import math
import types

import jax
import jax.numpy as jnp
from jax import lax
from jax.experimental import pallas as pl
from jax.experimental.pallas import tpu as pltpu

F32 = jnp.float32
BF16 = jnp.bfloat16

N_DEV = 8
SB_HEADS = 16
GLA_HEADS = 4
GLA_CHUNK = 64
GLA_SUB = 16
GLA_GATE_RANK = 16
GLA_GATE_NORMALIZER = 16.0
CONV_WIDTH = 3
RMS_EPS = 1e-6
ADAM_LR = 0.001
ADAM_B1 = 0.9
ADAM_B2 = 0.999
ADAM_EPS = 1e-08
ADAM_WD = 0.01
ADAM_STEP = 10

LANES = 128
SB_BLOCK = 256
SB_FWD_Q_BLOCK = 1024
SB_BWD_Q_BLOCK = 1024
VMEM_LIMIT_BYTES = 56 * 1024 * 1024
MM_TM, MM_TN, MM_TK = 1024, 1024, 4096
ROW_BLOCK = 512
PACK_COLS = 1024
PACK_ROW_ALIGN = 16
PACK_ROW_BLOCK = 128

WEIGHTS = ['norm_gains', 'sb_w_qkv', 'sb_w_o', 'conv_w_in', 'conv_w', 'conv_w_out', 'gla_w_in',
           'gla_w_gate_up', 'gla_b_gate', 'gla_head_norm', 'gla_w_o', 'ffn_w_up', 'ffn_w_down']
SHARD_AXIS = {'norm_gains': 2, 'sb_w_qkv': 2, 'sb_w_o': 1, 'conv_w_in': 2, 'conv_w': 2, 'conv_w_out': 1,
              'gla_w_in': 2, 'gla_w_gate_up': 2, 'gla_b_gate': 1, 'gla_head_norm': 2, 'gla_w_o': 1,
              'ffn_w_up': 2, 'ffn_w_down': 1}
F32_PAYLOAD = ('norm_gains', 'conv_w', 'gla_b_gate', 'gla_head_norm')

_NN = (((1,), (0,)), ((), ()))
_NT = (((1,), (1,)), ((), ()))
_TN = (((0,), (0,)), ((), ()))
_DIMS = {'nn': _NN, 'nt': _NT, 'tn': _TN}


def _params(*semantics):
    return pltpu.CompilerParams(dimension_semantics=semantics, vmem_limit_bytes=VMEM_LIMIT_BYTES)


def _dot(a, b, dims=_NN):
    return lax.dot_general(a.astype(BF16), b.astype(BF16), dims, preferred_element_type=F32)


def _split_hi_lo(x):
    hi = x.astype(BF16)
    lo = (x - hi.astype(F32)).astype(BF16)
    return hi, lo


def _dot_exact_rhs(x, ones_mat, dims=_NN):
    hi, lo = _split_hi_lo(x)
    return (lax.dot_general(hi, ones_mat, dims, preferred_element_type=F32)
            + lax.dot_general(lo, ones_mat, dims, preferred_element_type=F32))


def _dot_exact_lhs(ones_mat, x, dims=_NN):
    hi, lo = _split_hi_lo(x)
    return (lax.dot_general(ones_mat, hi, dims, preferred_element_type=F32)
            + lax.dot_general(ones_mat, lo, dims, preferred_element_type=F32))


def _log_sigmoid(z):
    return jnp.minimum(z, 0.0) - jnp.log(1.0 + jnp.exp(-jnp.abs(z)))


def _sigmoid(z):
    return 1.0 / (1.0 + jnp.exp(-z))


def matmul(a, b, *, mode, name, out_dtype=F32, epi=None, extra=None, tm=MM_TM, tn=MM_TN, tk=MM_TK):
    if mode == 'nn':
        (m, k), (k2, n) = a.shape, b.shape
    elif mode == 'nt':
        (m, k), (n, k2) = a.shape, b.shape
    else:
        (k, m), (k2, n) = a.shape, b.shape
    assert k == k2, (a.shape, b.shape, mode)
    tm, tn, tk = min(tm, m), min(tn, n), min(tk, k)
    assert m % tm == 0 and n % tn == 0 and k % tk == 0, (a.shape, b.shape, mode)
    nk = k // tk
    if mode == 'tn':
        a_spec = pl.BlockSpec((tk, tm), lambda i, j, kk: (kk, i))
    else:
        a_spec = pl.BlockSpec((tm, tk), lambda i, j, kk: (i, kk))
    if mode == 'nt':
        b_spec = pl.BlockSpec((tn, tk), lambda i, j, kk: (j, kk))
    else:
        b_spec = pl.BlockSpec((tk, tn), lambda i, j, kk: (kk, j))
    in_specs, operands = [a_spec, b_spec], [a, b]
    if epi == 'logsig16':
        in_specs.append(pl.BlockSpec((1, tn), lambda i, j, kk: (0, j)))
        operands.append(extra)
    elif epi in ('mul2sqrt', 'add'):
        in_specs.append(pl.BlockSpec((tm, tn), lambda i, j, kk: (i, j)))
        operands.append(extra)

    n_extra = len(operands) - 2

    def body(a_ref, b_ref, *rest):
        e_ref = rest[0] if n_extra else None
        o_ref = rest[n_extra]

        def finish(r):
            if epi == 'relu2':
                r = jnp.square(jnp.maximum(r, 0.0))
            elif epi == 'mul2sqrt':
                r = r * (2.0 * jnp.sqrt(e_ref[...].astype(F32)))
            elif epi == 'add':
                r = r + e_ref[...]
            elif epi == 'logsig16':
                r = _log_sigmoid(r + e_ref[...]) / GLA_GATE_NORMALIZER
            o_ref[...] = r.astype(o_ref.dtype)

        part = _dot(a_ref[...], b_ref[...], _DIMS[mode])
        if nk == 1:
            finish(part)
        else:
            acc_ref = rest[-1]
            kk = pl.program_id(2)

            @pl.when(kk == 0)
            def _():
                acc_ref[...] = part

            @pl.when(kk > 0)
            def _():
                acc_ref[...] += part

            @pl.when(kk == nk - 1)
            def _():
                finish(acc_ref[...])

    return pl.pallas_call(
        body, name=name, grid=(m // tm, n // tn, nk), in_specs=in_specs,
        out_specs=pl.BlockSpec((tm, tn), lambda i, j, kk: (i, j)),
        out_shape=jax.ShapeDtypeStruct((m, n), out_dtype),
        scratch_shapes=[pltpu.VMEM((tm, tn), F32)] if nk > 1 else [],
        compiler_params=_params("parallel", "parallel", "arbitrary"),
    )(*operands)


def rms_fwd(x, gain, *, name, res=None, out_dtype=F32, tm=ROW_BLOCK):
    t, d = x.shape
    tm = min(tm, t)
    row = pl.BlockSpec((tm, d), lambda i: (i, 0))
    in_specs, operands = [row, pl.BlockSpec((1, d), lambda i: (0, 0))], [x, gain]
    if res is not None:
        in_specs.append(row)
        operands.append(res)

    def body(x_ref, g_ref, *rest):
        xv = x_ref[...]
        r = lax.rsqrt(jnp.mean(xv * xv, axis=-1, keepdims=True) + RMS_EPS)
        y = xv * r * g_ref[...]
        if res is not None:
            y = rest[0][...] + y
        rest[-1][...] = y.astype(out_dtype)

    return pl.pallas_call(
        body, name=name, grid=(t // tm,), in_specs=in_specs, out_specs=row,
        out_shape=jax.ShapeDtypeStruct((t, d), out_dtype), compiler_params=_params("parallel"),
    )(*operands)


def rms_bwd(x, gain, dy, *, name, dres=None, out_dtype=F32, tm=ROW_BLOCK):
    t, d = x.shape
    tm = min(tm, t)
    row = pl.BlockSpec((tm, d), lambda i: (i, 0))
    vec = pl.BlockSpec((1, d), lambda i: (0, 0))
    in_specs, operands = [row, vec, row], [x, gain, dy]
    if dres is not None:
        in_specs.append(row)
        operands.append(dres)

    def body(x_ref, g_ref, dy_ref, *rest):
        dx_ref, dg_ref = rest[-2], rest[-1]

        @pl.when(pl.program_id(0) == 0)
        def _():
            dg_ref[...] = jnp.zeros_like(dg_ref)

        xv, dyv = x_ref[...], dy_ref[...]
        r = lax.rsqrt(jnp.mean(xv * xv, axis=-1, keepdims=True) + RMS_EPS)
        u = dyv * g_ref[...]
        dx = r * u - xv * (r * r * r * jnp.mean(u * xv, axis=-1, keepdims=True))
        if dres is not None:
            dx = rest[0][...] + dx
        dx_ref[...] = dx.astype(out_dtype)
        dg_ref[...] += jnp.sum(dyv * xv * r, axis=0, keepdims=True)

    return pl.pallas_call(
        body, name=name, grid=(t // tm,), in_specs=in_specs, out_specs=[row, vec],
        out_shape=[jax.ShapeDtypeStruct((t, d), out_dtype), jax.ShapeDtypeStruct((1, d), F32)],
        compiler_params=_params("arbitrary"),
    )(*operands)


def loss_head(y, target, *, name, tm=ROW_BLOCK):
    t, d = y.shape
    tm = min(tm, t)
    nt = t // tm
    row = pl.BlockSpec((tm, d), lambda i: (i, 0))

    def body(y_ref, t_ref, loss_ref, dy_ref, acc_ref):
        i = pl.program_id(0)

        @pl.when(i == 0)
        def _():
            acc_ref[...] = jnp.zeros_like(acc_ref)

        err = y_ref[...] - t_ref[...]
        dy_ref[...] = err * (1.0 / d)
        acc_ref[...] += jnp.sum(err * err, axis=0, keepdims=True)

        @pl.when(i == nt - 1)
        def _():
            loss_ref[...] = jnp.sum(acc_ref[...], axis=1, keepdims=True) * (0.5 / d)

    return pl.pallas_call(
        body, name=name, grid=(nt,), in_specs=[row, row],
        out_specs=[pl.BlockSpec((1, 1), lambda i: (0, 0)), row],
        out_shape=[jax.ShapeDtypeStruct((1, 1), F32), jax.ShapeDtypeStruct((t, d), F32)],
        scratch_shapes=[pltpu.VMEM((1, d), F32)], compiler_params=_params("arbitrary"),
    )(y, target)


def _sb_block_iota():
    rows = lax.broadcasted_iota(jnp.int32, (SB_BLOCK, SB_BLOCK), 0)
    cols = lax.broadcasted_iota(jnp.int32, (SB_BLOCK, SB_BLOCK), 1)
    return rows, cols


def _sb_logits(q_h, k_blk, mask):
    z = _dot(q_h, k_blk, _NT)
    ls = _log_sigmoid(z)
    lm = ls - z
    if mask is not None:
        lm = jnp.where(mask, lm, 0.0)
    return ls, lm, jnp.sum(lm, axis=1, keepdims=True)


def _sb_weights(ls, lm, mask, tri_strict, later):
    suffix = _dot_exact_rhs(lm, tri_strict)
    w = jnp.exp(ls + suffix + later)
    return w if mask is None else jnp.where(mask, w, 0.0)


def _rows_from(x, lo, half):
    return x if not lo else jnp.concatenate([x[lo:half], x[half + lo:]], axis=0)


def _add_from(x, lo, half, upd):
    if not lo:
        return x + upd
    n = half - lo
    return jnp.concatenate([x[:lo], x[lo:half] + upd[:n], x[half:half + lo], x[half + lo:] + upd[n:]], axis=0)


def _on_grid_step(p, i):
    return jnp.logical_and(pl.program_id(0) == p, pl.program_id(1) == i)


def sb_fwd(qkv, *, name, gather=None):
    t, d3 = qkv.shape
    d = d3 // 3
    head_dim = d // SB_HEADS
    qb, kb_rows = SB_FWD_Q_BLOCK, SB_BLOCK
    assert 2 * head_dim == LANES and t % qb == 0
    pairs = d // LANES
    per_q = qb // kb_rows
    nq = t // qb
    scale = head_dim ** -0.5

    def body(q_ref, k_ref, v_ref, *rest):
        if gather is None:
            compute(q_ref, k_ref, v_ref, *rest)
            return
        x_ref, o_ref, tot_ref, out_ref, acc_ref, later_ref, send_sems, recv_sems, local_sem = rest
        start, forward, finish = _gather_plan(x_ref, out_ref, send_sems, recv_sems, local_sem)
        pl.when(_on_grid_step(0, 0))(start)
        pl.when(_on_grid_step(7 * pairs // 8, 0))(forward)
        compute(q_ref, k_ref, v_ref, o_ref, tot_ref, acc_ref, later_ref)
        pl.when(_on_grid_step(pairs - 1, nq - 1))(finish)

    def compute(q_ref, k_ref, v_ref, o_ref, tot_ref, acc_ref, later_ref):
        qi = pl.program_id(1)
        lane = lax.broadcasted_iota(jnp.int32, (qb, LANES), 1)
        first = lane < head_dim
        q = q_ref[...] * scale
        q2 = jnp.concatenate([jnp.where(first, q, jnp.zeros_like(q)), jnp.where(first, jnp.zeros_like(q), q)], axis=0)
        rows, cols = _sb_block_iota()
        tri = jnp.where(rows > cols, 1.0, 0.0).astype(BF16)
        q_row = lax.broadcasted_iota(jnp.int32, (qb, kb_rows), 0)
        k_col = lax.broadcasted_iota(jnp.int32, (qb, kb_rows), 1)
        earlier = k_col < q_row
        acc_ref[...] = jnp.zeros_like(acc_ref)
        later_ref[...] = jnp.zeros_like(later_ref)

        def step(kb, lo=None):
            ks = pl.multiple_of(kb * SB_BLOCK, SB_BLOCK)
            k_blk = k_ref[pl.ds(ks, SB_BLOCK), :]
            v_blk = v_ref[pl.ds(ks, SB_BLOCK), :]
            if not lo:
                mask = None if lo is None else jnp.concatenate([earlier] * 2, axis=0)
                ls, lm, row = _sb_logits(q2, k_blk, mask)
                w = _sb_weights(ls, lm, mask, tri, later_ref[...])
                acc_ref[...] += _dot(w, v_blk)
                later_ref[...] += row
                return
            n = qb - lo
            mask = jnp.concatenate([earlier[:n]] * 2, axis=0)
            later = jnp.concatenate([later_ref[lo:qb], later_ref[qb + lo:]], axis=0)
            ls, lm, row = _sb_logits(_rows_from(q2, lo, qb), k_blk, mask)
            upd = _dot(_sb_weights(ls, lm, mask, tri, later), v_blk)
            for base in (0, qb):
                half = slice(0, n) if base == 0 else slice(n, 2 * n)
                acc_ref[base + lo:base + qb] += upd[half]
                later_ref[base + lo:base + qb] += row[half]

        for m in reversed(range(per_q)):
            step(per_q * qi + m, m * kb_rows)

        @pl.loop(0, per_q * qi)
        def _(i):
            step(per_q * qi - 1 - i)

        acc, total = acc_ref[...], later_ref[...]
        o_ref[...] = jnp.where(first, acc[:qb], acc[qb:]).astype(o_ref.dtype)
        tot_ref[...] = jnp.where(first, total[:qb], total[qb:])

    blk = lambda off: pl.BlockSpec((t, LANES), lambda p, i: (0, off + p))
    qblk = pl.BlockSpec((qb, LANES), lambda p, i: (i, p))
    in_specs, operands = [qblk, blk(pairs), blk(2 * pairs)], [qkv, qkv, qkv]
    out_specs = [qblk, qblk]
    out_shape = [jax.ShapeDtypeStruct((t, d), BF16), jax.ShapeDtypeStruct((t, d), F32)]
    sums = [pltpu.VMEM((2 * qb, LANES), F32), pltpu.VMEM((2 * qb, 1), F32)]
    if gather is None:
        return pl.pallas_call(
            body, name=name, grid=(pairs, nq), in_specs=in_specs, out_specs=out_specs, out_shape=out_shape,
            scratch_shapes=sums, compiler_params=_params("parallel", "arbitrary"),
        )(*operands)
    whole = pl.BlockSpec(memory_space=pl.ANY)
    return pl.pallas_call(
        body, name=name, grid=(pairs, nq), in_specs=in_specs + [whole], out_specs=out_specs + [whole],
        out_shape=out_shape + [jax.ShapeDtypeStruct((N_DEV * gather.shape[0], gather.shape[1]), gather.dtype)],
        scratch_shapes=sums + _comm_scratch(), compiler_params=_params("arbitrary", "arbitrary"),
    )(*operands, gather)


def sb_bwd(qkv, tot, do, *, name, exchange=None):
    t, d3 = qkv.shape
    d = d3 // 3
    head_dim = d // SB_HEADS
    pairs = d // LANES
    qb, kb_rows = SB_BWD_Q_BLOCK, SB_BLOCK
    per_q = qb // kb_rows
    nq = t // qb
    scale = head_dim ** -0.5

    def body(*refs):
        if exchange is None:
            compute(*refs)
            return
        q_ref, k_ref, v_ref, tot_ref, do_ref, in_ref, dq_ref, dk_ref, dv_ref, out_ref = refs[:10]
        dk_sum, dv_sum, send_sems, recv_sems, local_sem = refs[10:]
        start, finish = _exchange_plan(in_ref, out_ref, send_sems, recv_sems, local_sem)
        pl.when(_on_grid_step(0, 0))(start)
        compute(q_ref, k_ref, v_ref, tot_ref, do_ref, dq_ref, dk_ref, dv_ref, dk_sum, dv_sum)
        pl.when(_on_grid_step(pairs - 1, nq - 1))(finish)

    def compute(q_ref, k_ref, v_ref, tot_ref, do_ref, dq_ref, dk_ref, dv_ref, dk_sum, dv_sum):
        qi = pl.program_id(1)

        @pl.when(qi == 0)
        def _():
            dk_sum[...] = jnp.zeros_like(dk_sum)
            dv_sum[...] = jnp.zeros_like(dv_sum)

        lane = lax.broadcasted_iota(jnp.int32, (qb, LANES), 1)
        first = lane < head_dim
        q, dov, totv = q_ref[...] * scale, do_ref[...], tot_ref[...]
        second = jnp.logical_not(first)
        q2 = jnp.concatenate([jnp.where(s, q, jnp.zeros_like(q)) for s in (first, second)], axis=0)
        do2 = jnp.concatenate([jnp.where(s, dov, jnp.zeros_like(dov)) for s in (first, second)], axis=0)
        tot2 = jnp.concatenate([totv[:, 0:1], totv[:, head_dim:head_dim + 1]], axis=0)
        rows, cols = _sb_block_iota()
        tri_strict = jnp.where(rows > cols, 1.0, 0.0).astype(BF16)
        tri_before = jnp.where(rows < cols, 1.0, 0.0).astype(BF16)
        q_row = lax.broadcasted_iota(jnp.int32, (qb, kb_rows), 0)
        k_col = lax.broadcasted_iota(jnp.int32, (qb, kb_rows), 1)
        earlier = k_col < q_row

        def step(kb, carry, lo=None):
            ks = pl.multiple_of(kb * SB_BLOCK, SB_BLOCK)
            k_blk = k_ref[pl.ds(ks, SB_BLOCK), :]
            v_blk = v_ref[pl.ds(ks, SB_BLOCK), :]
            dq, seen, before = carry
            mask = None if lo is None else jnp.concatenate([earlier[:qb - lo]] * 2, axis=0)
            q_s, do_s = _rows_from(q2, lo, qb), _rows_from(do2, lo, qb)
            ls, lm, row = _sb_logits(q_s, k_blk, mask)
            seen = _add_from(seen, lo, qb, row)
            w = _sb_weights(ls, lm, mask, tri_strict, _rows_from(tot2 - seen, lo, qb))
            da = _dot(do_s, v_blk, _NT) * w
            g = _dot_exact_rhs(da, tri_before) + _rows_from(before, lo, qb)
            dz = da - jnp.exp(ls) * (da + g)
            if mask is not None:
                dz = jnp.where(mask, dz, 0.0)
            dk_sum[pl.ds(ks, SB_BLOCK), :] += _dot(dz, q_s, _TN)
            dv_sum[pl.ds(ks, SB_BLOCK), :] += _dot(w, do_s, _TN)
            return (_add_from(dq, lo, qb, _dot(dz, k_blk * scale)), seen,
                    _add_from(before, lo, qb, jnp.sum(da, axis=1, keepdims=True)))

        zero = jnp.zeros((2 * qb, LANES), F32)
        zcol = jnp.zeros((2 * qb, 1), F32)
        out = lax.fori_loop(0, per_q * qi, step, (zero, zcol, zcol))
        for m in range(per_q):
            out = step(per_q * qi + m, out, m * kb_rows)
        dq = out[0]
        dq_ref[...] = jnp.where(first, dq[:qb], dq[qb:]).astype(dq_ref.dtype)

        @pl.when(qi == nq - 1)
        def _():
            dk_ref[...] = dk_sum[...].astype(dk_ref.dtype)
            dv_ref[...] = dv_sum[...].astype(dv_ref.dtype)

    qblk = pl.BlockSpec((qb, LANES), lambda p, i: (i, p))
    col = lambda off: pl.BlockSpec((t, LANES), lambda p, i: (0, off + p))
    shape = jax.ShapeDtypeStruct((t, d), BF16)
    in_specs, operands = [qblk, col(pairs), col(2 * pairs), qblk, qblk], [qkv, qkv, qkv, tot, do]
    out_specs, out_shape = [qblk, col(0), col(0)], [shape, shape, shape]
    sums = [pltpu.VMEM((t, LANES), F32), pltpu.VMEM((t, LANES), F32)]
    if exchange is None:
        return pl.pallas_call(
            body, name=name, grid=(pairs, nq), in_specs=in_specs, out_specs=out_specs, out_shape=out_shape,
            scratch_shapes=sums, compiler_params=_params("parallel", "arbitrary"),
        )(*operands)
    whole = pl.BlockSpec(memory_space=pl.ANY)
    return pl.pallas_call(
        body, name=name, grid=(pairs, nq), in_specs=in_specs + [whole], out_specs=out_specs + [whole],
        out_shape=out_shape + [jax.ShapeDtypeStruct(exchange.shape, exchange.dtype)],
        scratch_shapes=sums + _comm_scratch(), compiler_params=_params("arbitrary", "arbitrary"),
    )(*operands, exchange)


def _shift_down(x, s):
    rows = lax.broadcasted_iota(jnp.int32, x.shape, 0)
    return jnp.where(rows >= s, pltpu.roll(x, s, 0), 0.0)


def _shift_up(x, s):
    t = x.shape[0]
    rows = lax.broadcasted_iota(jnp.int32, x.shape, 0)
    return jnp.where(rows < t - s, pltpu.roll(x, t - s, 0), 0.0)


def conv_fwd(bcu, w, *, name):
    t, d3 = bcu.shape
    d = d3 // 3
    nb = d // LANES
    col = lambda off: pl.BlockSpec((t, LANES), lambda j: (0, off + j))

    def body(b_ref, c_ref, u_ref, w_ref, y_ref):
        hh = c_ref[...] * u_ref[...]
        conv = w_ref[0:1, :] * _shift_down(hh, 2) + w_ref[1:2, :] * _shift_down(hh, 1) + w_ref[2:3, :] * hh
        y_ref[...] = (b_ref[...] * conv).astype(y_ref.dtype)

    return pl.pallas_call(
        body, name=name, grid=(nb,),
        in_specs=[col(0), col(nb), col(2 * nb), pl.BlockSpec((CONV_WIDTH, LANES), lambda j: (0, j))],
        out_specs=col(0), out_shape=jax.ShapeDtypeStruct((t, d), BF16), compiler_params=_params("parallel"),
    )(bcu, bcu, bcu, w)


def conv_bwd(bcu, w, dy, *, name):
    t, d3 = bcu.shape
    d = d3 // 3
    nb = d // LANES
    col = lambda off: pl.BlockSpec((t, LANES), lambda j: (0, off + j))
    wspec = pl.BlockSpec((CONV_WIDTH, LANES), lambda j: (0, j))

    def body(b_ref, c_ref, u_ref, w_ref, dy_ref, db_ref, dc_ref, du_ref, dw_ref):
        c, u, dyv = c_ref[...], u_ref[...], dy_ref[...]
        hh = c * u
        h2, h1 = _shift_down(hh, 2), _shift_down(hh, 1)
        w0, w1, w2 = w_ref[0:1, :], w_ref[1:2, :], w_ref[2:3, :]
        db_ref[...] = (dyv * (w0 * h2 + w1 * h1 + w2 * hh)).astype(db_ref.dtype)
        dconv = dyv * b_ref[...]
        dhh = w2 * dconv + w1 * _shift_up(dconv, 1) + w0 * _shift_up(dconv, 2)
        dc_ref[...] = (dhh * u).astype(dc_ref.dtype)
        du_ref[...] = (dhh * c).astype(du_ref.dtype)
        dw_ref[0:1, :] = jnp.sum(dconv * h2, axis=0, keepdims=True)
        dw_ref[1:2, :] = jnp.sum(dconv * h1, axis=0, keepdims=True)
        dw_ref[2:3, :] = jnp.sum(dconv * hh, axis=0, keepdims=True)

    shape = jax.ShapeDtypeStruct((t, d), BF16)
    return pl.pallas_call(
        body, name=name, grid=(nb,),
        in_specs=[col(0), col(nb), col(2 * nb), wspec, col(0)],
        out_specs=[col(0), col(0), col(0), wspec],
        out_shape=[shape, shape, shape, jax.ShapeDtypeStruct((CONV_WIDTH, d), F32)],
        compiler_params=_params("parallel"),
    )(bcu, bcu, bcu, w, dy)


def _gla_chunk(q, k, lg, scale):
    c = GLA_CHUNK
    rows = lax.broadcasted_iota(jnp.int32, (c, c), 0)
    cols = lax.broadcasted_iota(jnp.int32, (c, c), 1)
    causal = rows >= cols
    tril = jnp.where(causal, 1.0, 0.0).astype(BF16)
    q = q * scale
    cum = _dot_exact_lhs(tril, lg)
    last = cum[c - 1:c, :]
    eq = jnp.exp(cum)
    el = jnp.exp(last - cum)
    return causal, tril, q, k, cum, lg, last, eq, el


def _gla_sub_blocks(q, k, cum, lg):
    key_row = lax.broadcasted_iota(jnp.int32, (GLA_CHUNK, 1), 0)
    out = []
    for lo in range(0, GLA_CHUNK, GLA_SUB):
        hi = lo + GLA_SUB
        ref = cum[lo:lo + 1, :] - lg[lo:lo + 1, :]
        eq = jnp.exp(cum[lo:hi] - ref)
        ek = jnp.where(key_row < hi, jnp.exp(ref - cum), 0.0)
        out.append((slice(lo, hi), eq, ek, q[lo:hi] * eq, k * ek))
    return out


def _gla_scores(subs, causal):
    return jnp.where(causal, jnp.concatenate([_dot(qt, kt, _NT) for _, _, _, qt, kt in subs], axis=0), 0.0)


def gla_fwd(proj, lg, *, name):
    t, d3 = proj.shape
    d = d3 // 3
    dk, dv = d // 2 // GLA_HEADS, d // GLA_HEADS
    assert dk == LANES and dv == 2 * LANES
    c = GLA_CHUNK
    nc = t // c
    scale = dk ** -0.5
    nh = GLA_HEADS

    def body(q_ref, k_ref, v_ref, lg_ref, o_ref, st_out_ref, st_ref):
        @pl.when(pl.program_id(0) == 0)
        def _():
            st_ref[...] = jnp.zeros_like(st_ref)

        for h in range(nh):
            kcols, vcols = slice(h * dk, (h + 1) * dk), slice(h * dv, (h + 1) * dv)
            causal, _, q, k, cum, lg, last, eq, el = _gla_chunk(q_ref[:, kcols], k_ref[:, kcols], lg_ref[:, kcols], scale)
            v = v_ref[:, vcols]
            st = st_ref[h]
            st_out_ref[h] = st
            scores = _gla_scores(_gla_sub_blocks(q, k, cum, lg), causal)
            o_ref[:, vcols] = _dot(q * eq, st, _NT) + _dot(scores, v)
            st_ref[h] = st * jnp.exp(last) + _dot(v, k * el, _TN)

    half, full = pl.BlockSpec((c, d // 2), lambda i: (i, 0)), pl.BlockSpec((c, d), lambda i: (i, 0))
    return pl.pallas_call(
        body, name=name, grid=(nc,),
        in_specs=[half, pl.BlockSpec((c, d // 2), lambda i: (i, 1)), pl.BlockSpec((c, d), lambda i: (i, 1)), half],
        out_specs=[full, pl.BlockSpec((nh, None, dv, dk), lambda i: (0, i, 0, 0))],
        out_shape=[jax.ShapeDtypeStruct((t, d), F32), jax.ShapeDtypeStruct((nh, nc, dv, dk), F32)],
        scratch_shapes=[pltpu.VMEM((nh, dv, dk), F32)], compiler_params=_params("arbitrary"),
    )(proj, proj, proj, lg)


def gla_bwd(proj, lg, states, do, *, name):
    t, d3 = proj.shape
    d = d3 // 3
    dk, dv = d // 2 // GLA_HEADS, d // GLA_HEADS
    c = GLA_CHUNK
    nc = t // c
    scale = dk ** -0.5
    nh = GLA_HEADS

    def body(q_ref, k_ref, v_ref, lg_ref, st_ref, do_ref, dq_ref, dk_ref, dv_ref, dzg_ref, dbg_ref, dst_ref):
        @pl.when(pl.program_id(0) == 0)
        def _():
            dst_ref[...] = jnp.zeros_like(dst_ref)
            dbg_ref[...] = jnp.zeros_like(dbg_ref)

        for h in range(nh):
            kcols, vcols = slice(h * dk, (h + 1) * dk), slice(h * dv, (h + 1) * dv)
            causal, tril, q, k, cum, lg, last, eq, el = _gla_chunk(q_ref[:, kcols], k_ref[:, kcols], lg_ref[:, kcols], scale)
            v, st, dov, dst = v_ref[:, vcols], st_ref[h], do_ref[:, vcols], dst_ref[h]
            subs = _gla_sub_blocks(q, k, cum, lg)
            qt, kh = q * eq, k * el
            scores = _gla_scores(subs, causal)
            dscores = jnp.where(causal, _dot(dov, v, _NT), 0.0)
            dq_parts = []
            dkh = _dot(v, dst)
            dk_h = dkh * el
            for rows, eq_sub, ek_sub, qt_sub, kt_sub in subs:
                dq_parts.append(_dot(dscores[rows], kt_sub) * eq_sub)
                dk_h = dk_h + _dot(dscores[rows], qt_sub, _TN) * ek_sub
            dq = _dot(dov, st) * eq + jnp.concatenate(dq_parts, axis=0)
            dv_ref[:, vcols] = (_dot(scores, dov, _TN) + _dot(kh, dst, _NT)).astype(dv_ref.dtype)
            dq_ref[:, kcols] = (dq * scale).astype(dq_ref.dtype)
            dk_ref[:, kcols] = dk_h.astype(dk_ref.dtype)
            e_last = jnp.exp(last)
            dlast = jnp.sum(kh * dkh, axis=0, keepdims=True) + e_last * jnp.sum(dst * st, axis=0, keepdims=True)
            dcum = q * dq - k * dk_h
            dlg = _dot_exact_lhs(tril, dcum, _TN) + dlast
            dzg = dlg * (1.0 - jnp.exp(lg * GLA_GATE_NORMALIZER)) / GLA_GATE_NORMALIZER
            dzg_ref[:, kcols] = dzg.astype(dzg_ref.dtype)
            dbg_ref[:, kcols] += jnp.sum(dzg, axis=0, keepdims=True)
            dst_ref[h] = dst * e_last + _dot(dov, qt, _TN)

    rev = lambda i: nc - 1 - i
    half, full = pl.BlockSpec((c, d // 2), lambda i: (rev(i), 0)), pl.BlockSpec((c, d), lambda i: (rev(i), 0))
    half_shape = jax.ShapeDtypeStruct((t, d // 2), BF16)
    return pl.pallas_call(
        body, name=name, grid=(nc,),
        in_specs=[half, pl.BlockSpec((c, d // 2), lambda i: (rev(i), 1)), pl.BlockSpec((c, d), lambda i: (rev(i), 1)), half,
                  pl.BlockSpec((nh, None, dv, dk), lambda i: (0, rev(i), 0, 0)), full],
        out_specs=[half, half, full, half, pl.BlockSpec((1, d // 2), lambda i: (0, 0))],
        out_shape=[half_shape, half_shape, jax.ShapeDtypeStruct((t, d), BF16), half_shape,
                   jax.ShapeDtypeStruct((1, d // 2), F32)],
        scratch_shapes=[pltpu.VMEM((nh, dv, dk), F32)], compiler_params=_params("arbitrary"),
    )(proj, proj, proj, lg, states, do)


def gla_post_fwd(o, proj, head_norm, *, name, tm=ROW_BLOCK):
    t, d = o.shape
    dv = d // GLA_HEADS
    tm = min(tm, t)

    def body(o_ref, g_ref, hn_ref, y_ref):
        for h in range(GLA_HEADS):
            sl = slice(h * dv, (h + 1) * dv)
            ov, gv = o_ref[:, sl], g_ref[:, sl]
            r = lax.rsqrt(jnp.mean(ov * ov, axis=-1, keepdims=True) + RMS_EPS)
            y_ref[:, sl] = ((ov * r * hn_ref[:, sl]) * (gv * _sigmoid(gv))).astype(y_ref.dtype)

    row = pl.BlockSpec((tm, d), lambda i: (i, 0))
    return pl.pallas_call(
        body, name=name, grid=(t // tm,),
        in_specs=[row, pl.BlockSpec((tm, d), lambda i: (i, 2)), pl.BlockSpec((1, d), lambda i: (0, 0))],
        out_specs=row, out_shape=jax.ShapeDtypeStruct((t, d), BF16), compiler_params=_params("parallel"),
    )(o, proj, head_norm)


def gla_post_bwd(o, proj, head_norm, dy, *, name, tm=ROW_BLOCK):
    t, d = o.shape
    dv = d // GLA_HEADS
    tm = min(tm, t)

    def body(o_ref, g_ref, hn_ref, dy_ref, do_ref, dg_ref, dhn_ref):
        @pl.when(pl.program_id(0) == 0)
        def _():
            dhn_ref[...] = jnp.zeros_like(dhn_ref)

        for h in range(GLA_HEADS):
            sl = slice(h * dv, (h + 1) * dv)
            ov, gv, dyv, hn = o_ref[:, sl], g_ref[:, sl], dy_ref[:, sl], hn_ref[:, sl]
            r = lax.rsqrt(jnp.mean(ov * ov, axis=-1, keepdims=True) + RMS_EPS)
            sg = _sigmoid(gv)
            silu = gv * sg
            on = ov * r * hn
            dg_ref[:, sl] = (dyv * on * (sg * (1.0 + gv * (1.0 - sg)))).astype(dg_ref.dtype)
            don = dyv * silu
            u = don * hn
            do_ref[:, sl] = (r * u - ov * (r * r * r * jnp.mean(u * ov, axis=-1, keepdims=True))).astype(do_ref.dtype)
            dhn_ref[:, sl] += jnp.sum(don * ov * r, axis=0, keepdims=True)

    row = pl.BlockSpec((tm, d), lambda i: (i, 0))
    vec = pl.BlockSpec((1, d), lambda i: (0, 0))
    shape = jax.ShapeDtypeStruct((t, d), BF16)
    return pl.pallas_call(
        body, name=name, grid=(t // tm,),
        in_specs=[row, pl.BlockSpec((tm, d), lambda i: (i, 2)), vec, row],
        out_specs=[row, row, vec], out_shape=[shape, shape, jax.ShapeDtypeStruct((1, d), F32)],
        compiler_params=_params("arbitrary"),
    )(o, proj, head_norm, dy)


def _ffn_fwd(h, gains, w_up, w_down, tag):
    xn = rms_fwd(h, gains[2], name=f"{tag}_ffn_norm", out_dtype=BF16)
    act = matmul(xn, w_up, mode='nn', epi='relu2', out_dtype=BF16, name=f"{tag}_ffn_up")
    f = matmul(act, w_down, mode='nn', name=f"{tag}_ffn_down")
    h_out = rms_fwd(f, gains[3], res=h, name=f"{tag}_ffn_out")
    return h_out, (h, xn, act, f)


def _ffn_bwd(dh, saved, gains, w_up, w_down, tag):
    h, xn, act, f = saved
    df, dg3 = rms_bwd(f, gains[3], dh, out_dtype=BF16, name=f"{tag}_ffn_out_bwd")
    du = matmul(df, w_down, mode='nt', epi='mul2sqrt', extra=act, out_dtype=BF16, name=f"{tag}_ffn_da")
    dw_down = matmul(act, df, mode='tn', out_dtype=BF16, name=f"{tag}_ffn_dwdown")
    dw_up = matmul(xn, du, mode='tn', out_dtype=BF16, name=f"{tag}_ffn_dwup")
    dxn = matmul(du, w_up, mode='nt', name=f"{tag}_ffn_dxn")
    dh_in, dg2 = rms_bwd(h, gains[2], dxn, dres=dh, name=f"{tag}_ffn_norm_bwd")
    return dh_in, dg2, dg3, dw_up, dw_down


def _sb_layer_fwd(xn, w, j, tag, comm=None):
    qkv = matmul(xn, w['sb_w_qkv'][j], mode='nn', out_dtype=BF16, name=f"{tag}_qkv")
    if comm is None:
        o, tot = sb_fwd(qkv, name=f"{tag}_sb")
    else:
        o, tot, gathered = sb_fwd(qkv, name=f"{tag}_sb", gather=comm.rest_payload)
        comm.on_gathered(gathered)
    m = matmul(o, w['sb_w_o'][j], mode='nn', name=f"{tag}_wo")
    return m, (qkv, o, tot)


def _sb_layer_bwd(dm, xn, saved, w, j, tag, comm=None, grads=None):
    qkv, o, tot = saved
    do = matmul(dm, w['sb_w_o'][j], mode='nt', out_dtype=BF16, name=f"{tag}_do")
    dw_o = matmul(o, dm, mode='tn', out_dtype=BF16, name=f"{tag}_dwo")
    if comm is None:
        dq, dk, dv = sb_bwd(qkv, tot, do, name=f"{tag}_sb_bwd")
    else:
        parts = comm.rest_parts({**grads, ('sb_w_o', j): dw_o})
        dq, dk, dv, received = sb_bwd(qkv, tot, do, name=f"{tag}_sb_bwd", exchange=parts)
        comm.on_received(received)
    dqkv = jnp.concatenate([dq, dk, dv], axis=1)
    dw_qkv = matmul(xn, dqkv, mode='tn', out_dtype=BF16, name=f"{tag}_dwqkv")
    dxn = matmul(dqkv, w['sb_w_qkv'][j], mode='nt', name=f"{tag}_dxn")
    return dxn, {('sb_w_qkv', j): dw_qkv, ('sb_w_o', j): dw_o}


def _conv_layer_fwd(xn, w, j, tag):
    bcu = matmul(xn, w['conv_w_in'][j], mode='nn', name=f"{tag}_in")
    y = conv_fwd(bcu, w['conv_w'][j], name=f"{tag}_conv")
    m = matmul(y, w['conv_w_out'][j], mode='nn', name=f"{tag}_out")
    return m, (bcu, y)


def _conv_layer_bwd(dm, xn, saved, w, j, tag):
    bcu, y = saved
    dy = matmul(dm, w['conv_w_out'][j], mode='nt', name=f"{tag}_dy")
    dw_out = matmul(y, dm, mode='tn', out_dtype=BF16, name=f"{tag}_dwout")
    db, dc, du, dw_conv = conv_bwd(bcu, w['conv_w'][j], dy, name=f"{tag}_conv_bwd")
    dbcu = jnp.concatenate([db, dc, du], axis=1)
    dw_in = matmul(xn, dbcu, mode='tn', out_dtype=BF16, name=f"{tag}_dwin")
    dxn = matmul(dbcu, w['conv_w_in'][j], mode='nt', name=f"{tag}_dxn")
    return dxn, {('conv_w_in', j): dw_in, ('conv_w', j): dw_conv, ('conv_w_out', j): dw_out}


def _gla_split(w_in, w_gate_up):
    d = w_in.shape[0]
    w_main = w_in[:, :3 * d]
    w_a = jnp.pad(w_in[:, 3 * d:], ((0, 0), (0, LANES - GLA_GATE_RANK)))
    w_gu = jnp.pad(w_gate_up, ((0, LANES - GLA_GATE_RANK), (0, 0)))
    return w_main, w_a, w_gu


def _gla_layer_fwd(xn, w, j, tag):
    w_main, w_a, w_gu = _gla_split(w['gla_w_in'][j], w['gla_w_gate_up'][j])
    proj = matmul(xn, w_main, mode='nn', name=f"{tag}_in")
    a_low = matmul(xn, w_a, mode='nn', out_dtype=BF16, name=f"{tag}_alow")
    lg = matmul(a_low, w_gu, mode='nn', epi='logsig16', extra=w['gla_b_gate'][j][None, :], name=f"{tag}_gate")
    o, states = gla_fwd(proj, lg, name=f"{tag}_gla")
    hn = w['gla_head_norm'][j].reshape(1, -1)
    y = gla_post_fwd(o, proj, hn, name=f"{tag}_post")
    m = matmul(y, w['gla_w_o'][j], mode='nn', name=f"{tag}_wo")
    return m, (proj, a_low, lg, o, states, y)


def _gla_layer_bwd(dm, xn, saved, w, j, tag):
    proj, a_low, lg, o, states, y = saved
    w_main, w_a, w_gu = _gla_split(w['gla_w_in'][j], w['gla_w_gate_up'][j])
    hn = w['gla_head_norm'][j].reshape(1, -1)
    dy = matmul(dm, w['gla_w_o'][j], mode='nt', name=f"{tag}_dy")
    dw_o = matmul(y, dm, mode='tn', out_dtype=BF16, name=f"{tag}_dwo")
    do, dg, dhn = gla_post_bwd(o, proj, hn, dy, name=f"{tag}_post_bwd")
    dq, dk, dv, dzg, dbg = gla_bwd(proj, lg, states, do, name=f"{tag}_gla_bwd")
    da_low = matmul(dzg, w_gu, mode='nt', out_dtype=BF16, name=f"{tag}_dalow")
    dw_gu = matmul(a_low, dzg, mode='tn', out_dtype=BF16, name=f"{tag}_dwgu")[:GLA_GATE_RANK]
    dproj = jnp.concatenate([dq, dk, dv, dg], axis=1)
    dw_main = matmul(xn, dproj, mode='tn', out_dtype=BF16, name=f"{tag}_dwin")
    dw_a = matmul(xn, da_low, mode='tn', out_dtype=BF16, name=f"{tag}_dwa")[:, :GLA_GATE_RANK]
    dxn_a = matmul(da_low, w_a, mode='nt', name=f"{tag}_dxn_a")
    dxn = matmul(dproj, w_main, mode='nt', epi='add', extra=dxn_a, name=f"{tag}_dxn")
    grads = {('gla_w_in', j): jnp.concatenate([dw_main, dw_a], axis=1), ('gla_w_gate_up', j): dw_gu,
             ('gla_b_gate', j): dbg[0], ('gla_head_norm', j): dhn.reshape(w['gla_head_norm'][j].shape),
             ('gla_w_o', j): dw_o}
    return dxn, grads


_MIXERS = ((_sb_layer_fwd, _sb_layer_bwd), (_conv_layer_fwd, _conv_layer_bwd), (_gla_layer_fwd, _gla_layer_bwd))


def local_step(x, w, target, comm=None):
    depth = len(w['norm_gains'])
    h = x
    tape = []
    for i in range(depth):
        kind, j = i % 3, i // 3
        tag = f"l{i}"
        extra = {'comm': comm} if (comm is not None and i == 0) else {}
        gains = [w['norm_gains'][i][s][None, :] for s in range(4)]
        xn = rms_fwd(h, gains[0], name=f"{tag}_mix_norm", out_dtype=BF16)
        m, saved = _MIXERS[kind][0](xn, w, j, tag, **extra)
        h_mid = rms_fwd(m, gains[1], res=h, name=f"{tag}_mix_out")
        h_out, ffn_saved = _ffn_fwd(h_mid, gains, w['ffn_w_up'][i], w['ffn_w_down'][i], tag)
        tape.append((h, xn, m, saved, ffn_saved, gains))
        h = h_out
    loss, dh = loss_head(h, target, name="loss_head")

    grads = {}
    for i in reversed(range(depth)):
        kind, j = i % 3, i // 3
        tag = f"l{i}"
        h_in, xn, m, saved, ffn_saved, gains = tape[i]
        dg = [None] * 4
        dh, dg[2], dg[3], grads[('ffn_w_up', i)], grads[('ffn_w_down', i)] = _ffn_bwd(
            dh, ffn_saved, gains, w['ffn_w_up'][i], w['ffn_w_down'][i], tag)
        dm, dg[1] = rms_bwd(m, gains[1], dh, out_dtype=BF16, name=f"{tag}_mix_out_bwd")
        extra = {'comm': comm, 'grads': grads} if (comm is not None and i == 0) else {}
        dxn, g = _MIXERS[kind][1](dm, xn, saved, w, j, tag, **extra)
        grads.update(g)
        dh, dg[0] = rms_bwd(h_in, gains[0], dxn, dres=dh, name=f"{tag}_mix_norm_bwd")
        grads[('norm_gains', i)] = jnp.concatenate(dg, axis=0)
    return loss, dh, grads


def _segments(keys, shard_shapes, f32_as_pairs):
    segs, row = {}, 0
    for name, lo, hi in keys:
        n = (hi - lo) * math.prod(shard_shapes[name][1:]) * (2 if f32_as_pairs and name in F32_PAYLOAD else 1)
        nrows = -(-n // (PACK_COLS * PACK_ROW_ALIGN)) * PACK_ROW_ALIGN
        segs[(name, lo, hi)] = (row, nrows, n)
        row += nrows
    return segs, -(-row // PACK_ROW_BLOCK) * PACK_ROW_BLOCK


def _pack(parts, segs, total_rows, dtype):
    pieces, row = [], 0
    for key in segs:
        _, nrows, n = segs[key]
        p = parts[key].astype(dtype)
        lead = p.shape[:-1]
        if nrows * PACK_COLS > n:
            p = jnp.pad(p, [(0, 0)] * len(lead) + [(0, nrows * PACK_COLS - n)])
        pieces.append(p.reshape(lead + (nrows, PACK_COLS)))
        row += nrows
    if total_rows > row:
        pieces.append(jnp.zeros(lead + (total_rows - row, PACK_COLS), dtype))
    return jnp.concatenate(pieces, axis=-2)


def _unpack(buf, seg):
    first, nrows, n = seg
    piece = buf[..., first:first + nrows, :]
    return piece.reshape(piece.shape[:-2] + (nrows * PACK_COLS,))[..., :n]


def _unshard(gathered, axis):
    moved = jnp.moveaxis(gathered, 0, axis)
    shape = moved.shape
    return moved.reshape(shape[:axis] + (shape[axis] * shape[axis + 1],) + shape[axis + 2:])


def _shard_split(full, axis):
    shape = full.shape
    cut = full.reshape(shape[:axis] + (N_DEV, shape[axis] // N_DEV) + shape[axis + 1:])
    return jnp.moveaxis(cut, axis, 0)


def _mesh_position():
    return lax.axis_index("x"), lax.axis_index("y"), lax.axis_index("c")


def _comm_scratch():
    return [pltpu.SemaphoreType.DMA((N_DEV - 1,)), pltpu.SemaphoreType.DMA((N_DEV - 1,)), pltpu.SemaphoreType.DMA]


def _gather_plan(x_ref, out_ref, send_sems, recv_sems, local_sem):
    rows = x_ref.shape[0]
    x, y, c = _mesh_position()
    me, sibling = (x, y, c), (x, y, 1 - c)
    chips = [(1 - x, y), (x, 1 - y), (1 - x, 1 - y)]

    def block(px, py, pc):
        return out_ref.at[pl.ds((4 * px + 2 * py + pc) * rows, rows), :]

    def copy(k, blk, to, src=None):
        return pltpu.make_async_remote_copy(
            src_ref=block(*blk) if src is None else src, dst_ref=block(*blk),
            send_sem=send_sems.at[k], recv_sem=recv_sems.at[k],
            device_id=to, device_id_type=pl.DeviceIdType.MESH)

    mine = pltpu.make_async_copy(x_ref, block(*me), local_sem)
    first = [copy(0, me, sibling, src=x_ref)]
    first += [copy(1 + j, me, (*chip, c), src=x_ref) for j, chip in enumerate(chips)]
    passed = [copy(4 + j, (*chip, c), sibling) for j, chip in enumerate(chips)]

    def start():
        mine.start()
        for cp in first:
            cp.start()

    def forward():
        for j, chip in enumerate(chips):
            copy(1 + j, (*chip, c), me).wait_recv()
            passed[j].start()

    def finish():
        copy(0, sibling, me).wait_recv()
        for j, chip in enumerate(chips):
            copy(4 + j, (*chip, 1 - c), me).wait_recv()
        for cp in first + passed:
            cp.wait_send()
        mine.wait()

    return start, forward, finish


def _exchange_plan(in_ref, out_ref, send_sems, recv_sems, local_sem):
    x, y, c = _mesh_position()
    my_id = 4 * x + 2 * y + c
    mine = pltpu.make_async_copy(in_ref.at[my_id], out_ref.at[my_id], local_sem)

    def copy(k, receive):
        px = 1 - x if k & 4 else x
        py = 1 - y if k & 2 else y
        pc = 1 - c if k & 1 else c
        peer_id = 4 * px + 2 * py + pc
        return pltpu.make_async_remote_copy(
            src_ref=in_ref.at[peer_id], dst_ref=out_ref.at[peer_id if receive else my_id],
            send_sem=send_sems.at[k - 1], recv_sem=recv_sems.at[k - 1],
            device_id=(px, py, pc), device_id_type=pl.DeviceIdType.MESH)

    def start():
        mine.start()
        for k in range(1, N_DEV):
            copy(k, False).start()

    def finish():
        for k in range(1, N_DEV):
            copy(k, True).wait_recv()
        for k in range(1, N_DEV):
            copy(k, False).wait_send()
        mine.wait()

    return start, finish


def all_gather(shard, *, name):
    rows, cols = shard.shape

    def body(x_ref, out_ref, send_sems, recv_sems, local_sem):
        start, forward, finish = _gather_plan(x_ref, out_ref, send_sems, recv_sems, local_sem)
        start()
        forward()
        finish()

    return pl.pallas_call(
        body, name=name, out_shape=jax.ShapeDtypeStruct((N_DEV * rows, cols), shard.dtype),
        in_specs=[pl.BlockSpec(memory_space=pl.ANY)], out_specs=pl.BlockSpec(memory_space=pl.ANY),
        scratch_shapes=_comm_scratch(),
    )(shard)


def adamw(parts, w, m, v, *, name, exchange=None):
    _, rows, cols = parts.shape
    tr = PACK_ROW_BLOCK
    steps = rows // tr
    c1 = 1.0 - ADAM_B1 ** ADAM_STEP
    c2 = 1.0 - ADAM_B2 ** ADAM_STEP

    def body(p_ref, w_ref, m_ref, v_ref, *rest):
        if exchange is None:
            update(p_ref, w_ref, m_ref, v_ref, *rest)
            return
        in_ref, g_ref, d_ref, nm_ref, nv_ref, out_ref, send_sems, recv_sems, local_sem = rest
        start, finish = _exchange_plan(in_ref, out_ref, send_sems, recv_sems, local_sem)
        pl.when(pl.program_id(0) == 0)(start)
        update(p_ref, w_ref, m_ref, v_ref, g_ref, d_ref, nm_ref, nv_ref)
        pl.when(pl.program_id(0) == steps - 1)(finish)

    def update(p_ref, w_ref, m_ref, v_ref, g_ref, d_ref, nm_ref, nv_ref):
        g = p_ref[0].astype(F32)
        for s in range(1, N_DEV):
            g = g + p_ref[s].astype(F32)
        nm = ADAM_B1 * m_ref[...] + (1.0 - ADAM_B1) * g
        nv = ADAM_B2 * v_ref[...] + (1.0 - ADAM_B2) * jnp.square(g)
        m_hat = nm / c1
        v_hat = nv / c2
        g_ref[...] = g
        d_ref[...] = -ADAM_LR * (m_hat / (jnp.sqrt(v_hat) + ADAM_EPS) + ADAM_WD * w_ref[...])
        nm_ref[...] = nm
        nv_ref[...] = nv

    row = pl.BlockSpec((tr, cols), lambda i: (i, 0))
    shape = jax.ShapeDtypeStruct((rows, cols), F32)
    in_specs = [pl.BlockSpec((N_DEV, tr, cols), lambda i: (0, i, 0)), row, row, row]
    if exchange is None:
        return pl.pallas_call(
            body, name=name, grid=(steps,), in_specs=in_specs,
            out_specs=[row, row, row, row], out_shape=[shape, shape, shape, shape],
            compiler_params=_params("parallel"),
        )(parts, w, m, v)
    whole = pl.BlockSpec(memory_space=pl.ANY)
    return pl.pallas_call(
        body, name=name, grid=(steps,), in_specs=in_specs + [whole],
        out_specs=[row, row, row, row, whole],
        out_shape=[shape, shape, shape, shape, jax.ShapeDtypeStruct(exchange.shape, exchange.dtype)],
        scratch_shapes=_comm_scratch(), compiler_params=_params("arbitrary"),
    )(parts, w, m, v, exchange)


def kernel(x, norm_gains, sb_w_qkv, sb_w_o, conv_w_in, conv_w, conv_w_out, gla_w_in, gla_w_gate_up, gla_b_gate, gla_head_norm, gla_w_o, ffn_w_up, ffn_w_down, loss_target, m_norm_gains, m_sb_w_qkv, m_sb_w_o, m_conv_w_in, m_conv_w, m_conv_w_out, m_gla_w_in, m_gla_w_gate_up, m_gla_b_gate, m_gla_head_norm, m_gla_w_o, m_ffn_w_up, m_ffn_w_down, v_norm_gains, v_sb_w_qkv, v_sb_w_o, v_conv_w_in, v_conv_w, v_conv_w_out, v_gla_w_in, v_gla_w_gate_up, v_gla_b_gate, v_gla_head_norm, v_gla_w_o, v_ffn_w_up, v_ffn_w_down):
    shards = dict(zip(WEIGHTS, (norm_gains, sb_w_qkv, sb_w_o, conv_w_in, conv_w, conv_w_out, gla_w_in,
                                gla_w_gate_up, gla_b_gate, gla_head_norm, gla_w_o, ffn_w_up, ffn_w_down)))
    moments_m = dict(zip(WEIGHTS, (m_norm_gains, m_sb_w_qkv, m_sb_w_o, m_conv_w_in, m_conv_w, m_conv_w_out,
                                   m_gla_w_in, m_gla_w_gate_up, m_gla_b_gate, m_gla_head_norm, m_gla_w_o,
                                   m_ffn_w_up, m_ffn_w_down)))
    moments_v = dict(zip(WEIGHTS, (v_norm_gains, v_sb_w_qkv, v_sb_w_o, v_conv_w_in, v_conv_w, v_conv_w_out,
                                   v_gla_w_in, v_gla_w_gate_up, v_gla_b_gate, v_gla_head_norm, v_gla_w_o,
                                   v_ffn_w_up, v_ffn_w_down)))
    shard_shapes = {n: a.shape for n, a in shards.items()}

    alone = [('norm_gains', 0, 1), ('sb_w_qkv', 0, 1)]
    rest = [(n, 1 if (n, 0, 1) in alone else 0, shard_shapes[n][0]) for n in WEIGHTS]
    groups = [alone, [piece for piece in rest if piece[1] < piece[2]]]

    def payload(group):
        segs, rows = _segments(group, shard_shapes, True)
        flat = {(n, lo, hi): (lax.bitcast_convert_type(shards[n][lo:hi], BF16) if n in F32_PAYLOAD
                              else shards[n][lo:hi].astype(BF16)).reshape(-1) for n, lo, hi in group}
        return segs, _pack(flat, segs, rows, BF16)

    whole = {n: [None] * shard_shapes[n][0] for n in WEIGHTS}

    def take_gathered(segs, gathered):
        gathered = gathered.reshape(N_DEV, -1, PACK_COLS)
        for n, lo, hi in segs:
            piece = _unpack(gathered, segs[(n, lo, hi)])
            if n in F32_PAYLOAD:
                piece = lax.bitcast_convert_type(piece.reshape(N_DEV, -1, 2), F32)
            full = _unshard(piece.reshape((N_DEV, hi - lo) + shard_shapes[n][1:]), SHARD_AXIS[n])
            for j in range(lo, hi):
                whole[n][j] = full[j - lo]

    segs0, payload0 = payload(groups[0])
    segs1, payload1 = payload(groups[1])
    take_gathered(segs0, all_gather(payload0, name="weights_all_gather"))

    gsegs = [_segments(group, shard_shapes, False) for group in groups]
    received = [None, None]

    def parts(g, grads):
        segs, rows = gsegs[g]
        flat = {(n, lo, hi): _shard_split(jnp.stack([grads[(n, j)] for j in range(lo, hi)]), SHARD_AXIS[n])
                .reshape(N_DEV, -1) for n, lo, hi in segs}
        return _pack(flat, segs, rows, BF16)

    def on_received(buf):
        received[1] = buf

    comm = types.SimpleNamespace(rest_payload=payload1, on_gathered=functools.partial(take_gathered, segs1),
                                 rest_parts=functools.partial(parts, 1), on_received=on_received)

    loss, grad_x, grads = local_step(x[0], whole, loss_target[0], comm)
    loss = lax.psum(loss[0, 0], ("x", "y", "c"))

    results = {n: [[] for _ in range(4)] for n in WEIGHTS}
    outs_of = {}
    for g in (1, 0):
        segs, rows = gsegs[g]

        def packed(source):
            return _pack({(n, lo, hi): source[n][lo:hi].reshape(-1) for n, lo, hi in segs}, segs, rows, F32)

        states = (packed(shards), packed(moments_m), packed(moments_v))
        if g == 1:
            *outs_of[g], received[0] = adamw(received[1], *states, name="adamw1", exchange=parts(0, grads))
        else:
            outs_of[g] = adamw(received[0], *states, name="adamw0")
    for g in range(2):
        segs, _ = gsegs[g]
        for n, lo, hi in segs:
            for o, buf in enumerate(outs_of[g]):
                results[n][o].append(_unpack(buf, segs[(n, lo, hi)]).reshape((hi - lo,) + shard_shapes[n][1:]))
    whole_out = [[r[0] if len(r) == 1 else jnp.concatenate(r, axis=0) for r in (results[n][o] for n in WEIGHTS)]
                 for o in range(4)]
    return (loss, grad_x[None], *whole_out[0], *whole_out[1], *whole_out[2], *whole_out[3])
```

```python
import functools
import math
import types
import typing

import jax
import jax.numpy as jnp
from jax import lax
from jax.experimental import pallas as pl
from jax.experimental.pallas import tpu as pltpu

F32 = jnp.float32
BF16 = jnp.bfloat16

N_DEV = 8
SB_HEADS = 16
GLA_HEADS = 4
GLA_CHUNK = 64
GLA_SUB = 16
GLA_GATE_RANK = 16
GLA_GATE_NORMALIZER = 16.0
CONV_WIDTH = 3
RMS_EPS = 1e-6
ADAM_LR = 0.001
ADAM_B1 = 0.9
ADAM_B2 = 0.999
ADAM_EPS = 1e-08
ADAM_WD = 0.01
ADAM_STEP = 10

LANES = 128
SB_BLOCK = 256
SB_FWD_Q_BLOCK = 1024
SB_BWD_Q_BLOCK = 1024
VMEM_LIMIT_BYTES = 56 * 1024 * 1024
MM_TM, MM_TN, MM_TK = 1024, 1024, 4096
ROW_BLOCK = 512
PACK_COLS = 1024
PACK_ROW_ALIGN = 16
PACK_ROW_BLOCK = 128

WEIGHTS = ['norm_gains', 'sb_w_qkv', 'sb_w_o', 'conv_w_in', 'conv_w', 'conv_w_out', 'gla_w_in',
           'gla_w_gate_up', 'gla_b_gate', 'gla_head_norm', 'gla_w_o', 'ffn_w_up', 'ffn_w_down']
SHARD_AXIS = {'norm_gains': 2, 'sb_w_qkv': 2, 'sb_w_o': 1, 'conv_w_in': 2, 'conv_w': 2, 'conv_w_out': 1,
              'gla_w_in': 2, 'gla_w_gate_up': 2, 'gla_b_gate': 1, 'gla_head_norm': 2, 'gla_w_o': 1,
              'ffn_w_up': 2, 'ffn_w_down': 1}
F32_PAYLOAD = ('norm_gains', 'conv_w', 'gla_b_gate', 'gla_head_norm')

_NN = (((1,), (0,)), ((), ()))
_NT = (((1,), (1,)), ((), ()))
_TN = (((0,), (0,)), ((), ()))
_DIMS = {'nn': _NN, 'nt': _NT, 'tn': _TN}


def _params(*semantics):
    return pltpu.CompilerParams(dimension_semantics=semantics, vmem_limit_bytes=VMEM_LIMIT_BYTES)


def _dot(a, b, dims=_NN):
    return lax.dot_general(a.astype(BF16), b.astype(BF16), dims, preferred_element_type=F32)


def _split_hi_lo(x):
    hi = x.astype(BF16)
    lo = (x - hi.astype(F32)).astype(BF16)
    return hi, lo


def _dot_exact_rhs(x, ones_mat, dims=_NN):
    hi, lo = _split_hi_lo(x)
    return (lax.dot_general(hi, ones_mat, dims, preferred_element_type=F32)
            + lax.dot_general(lo, ones_mat, dims, preferred_element_type=F32))


def _dot_exact_lhs(ones_mat, x, dims=_NN):
    hi, lo = _split_hi_lo(x)
    return (lax.dot_general(ones_mat, hi, dims, preferred_element_type=F32)
            + lax.dot_general(ones_mat, lo, dims, preferred_element_type=F32))


def _log_sigmoid(z):
    return jnp.minimum(z, 0.0) - jnp.log(1.0 + jnp.exp(-jnp.abs(z)))


def _sigmoid(z):
    return 1.0 / (1.0 + jnp.exp(-z))


class GatheredWeight(typing.NamedTuple):
    blocks: jax.Array
    index: int
    by_rows: bool

    @property
    def whole_shape(self):
        _, _, r, c = self.blocks.shape
        return (N_DEV * r, c) if self.by_rows else (r, N_DEV * c)


def matmul(a, b, *, mode, name, out_dtype=F32, epi=None, extra=None, tm=MM_TM, tn=MM_TN, tk=MM_TK):
    gathered = isinstance(b, GatheredWeight)
    b_shape = b.shape if not gathered else b.whole_shape
    if mode == 'nn':
        (m, k), (k2, n) = a.shape, b_shape
    elif mode == 'nt':
        (m, k), (n, k2) = a.shape, b_shape
    else:
        (k, m), (k2, n) = a.shape, b_shape
    assert k == k2, (a.shape, b_shape, mode)
    tm, tn, tk = min(tm, m), min(tn, n), min(tk, k)
    contract = lambda a_ref, b_ref: _dot(a_ref[...], b_ref[...], _DIMS[mode])
    if gathered:
        assert mode in ('nn', 'nt') and tk == k
        _, _, r, c = b.blocks.shape
        l = b.index
        along_k = (mode == 'nn') == b.by_rows
        if along_k and b.by_rows:
            b_spec = pl.BlockSpec((N_DEV, None, r, tn), lambda i, j, kk: (0, l, 0, j))
            contract = lambda a_ref, b_ref: _dot(a_ref[...], b_ref[...].reshape(N_DEV * r, tn), _NN)
        elif along_k:
            b_spec = pl.BlockSpec((N_DEV, None, tn, c), lambda i, j, kk: (0, l, j, 0))
            contract = lambda a_ref, b_ref: sum(_dot(a_ref[:, s * c:(s + 1) * c], b_ref[s], _NT) for s in range(N_DEV))
        elif b.by_rows:
            assert tn % r == 0
            b_spec = pl.BlockSpec((tn // r, None, r, tk), lambda i, j, kk: (j, l, 0, kk))
            contract = lambda a_ref, b_ref: _dot(a_ref[...], b_ref[...].reshape(tn, tk), _NT)
        else:
            tn = c
            b_spec = pl.BlockSpec((None, None, tk, c), lambda i, j, kk: (j, l, kk, 0))
    elif mode == 'nt':
        b_spec = pl.BlockSpec((tn, tk), lambda i, j, kk: (j, kk))
    else:
        b_spec = pl.BlockSpec((tk, tn), lambda i, j, kk: (kk, j))
    assert m % tm == 0 and n % tn == 0 and k % tk == 0, (a.shape, b_shape, mode)
    nk = k // tk
    if mode == 'tn':
        a_spec = pl.BlockSpec((tk, tm), lambda i, j, kk: (kk, i))
    else:
        a_spec = pl.BlockSpec((tm, tk), lambda i, j, kk: (i, kk))
    in_specs, operands = [a_spec, b_spec], [a, b.blocks if gathered else b]
    if epi == 'logsig16':
        in_specs.append(pl.BlockSpec((1, tn), lambda i, j, kk: (0, j)))
        operands.append(extra)
    elif epi in ('mul2sqrt', 'add'):
        in_specs.append(pl.BlockSpec((tm, tn), lambda i, j, kk: (i, j)))
        operands.append(extra)

    n_extra = len(operands) - 2

    def body(a_ref, b_ref, *rest):
        e_ref = rest[0] if n_extra else None
        o_ref = rest[n_extra]

        def finish(r):
            if epi == 'relu2':
                r = jnp.square(jnp.maximum(r, 0.0))
            elif epi == 'mul2sqrt':
                r = r * (2.0 * jnp.sqrt(e_ref[...].astype(F32)))
            elif epi == 'add':
                r = r + e_ref[...]
            elif epi == 'logsig16':
                r = _log_sigmoid(r + e_ref[...]) / GLA_GATE_NORMALIZER
            o_ref[...] = r.astype(o_ref.dtype)

        part = contract(a_ref, b_ref)
        if nk == 1:
            finish(part)
        else:
            acc_ref = rest[-1]
            kk = pl.program_id(2)

            @pl.when(kk == 0)
            def _():
                acc_ref[...] = part

            @pl.when(kk > 0)
            def _():
                acc_ref[...] += part

            @pl.when(kk == nk - 1)
            def _():
                finish(acc_ref[...])

    return pl.pallas_call(
        body, name=name, grid=(m // tm, n // tn, nk), in_specs=in_specs,
        out_specs=pl.BlockSpec((tm, tn), lambda i, j, kk: (i, j)),
        out_shape=jax.ShapeDtypeStruct((m, n), out_dtype),
        scratch_shapes=[pltpu.VMEM((tm, tn), F32)] if nk > 1 else [],
        compiler_params=_params("parallel", "parallel", "arbitrary"),
    )(*operands)


def rms_fwd(x, gain, *, name, res=None, out_dtype=F32, tm=ROW_BLOCK):
    t, d = x.shape
    tm = min(tm, t)
    row = pl.BlockSpec((tm, d), lambda i: (i, 0))
    in_specs, operands = [row, pl.BlockSpec((1, d), lambda i: (0, 0))], [x, gain]
    if res is not None:
        in_specs.append(row)
        operands.append(res)

    def body(x_ref, g_ref, *rest):
        xv = x_ref[...]
        r = lax.rsqrt(jnp.mean(xv * xv, axis=-1, keepdims=True) + RMS_EPS)
        y = xv * r * g_ref[...]
        if res is not None:
            y = rest[0][...] + y
        rest[-1][...] = y.astype(out_dtype)

    return pl.pallas_call(
        body, name=name, grid=(t // tm,), in_specs=in_specs, out_specs=row,
        out_shape=jax.ShapeDtypeStruct((t, d), out_dtype), compiler_params=_params("parallel"),
    )(*operands)


def rms_bwd(x, gain, dy, *, name, dres=None, out_dtype=F32, tm=ROW_BLOCK):
    t, d = x.shape
    tm = min(tm, t)
    row = pl.BlockSpec((tm, d), lambda i: (i, 0))
    vec = pl.BlockSpec((1, d), lambda i: (0, 0))
    in_specs, operands = [row, vec, row], [x, gain, dy]
    if dres is not None:
        in_specs.append(row)
        operands.append(dres)

    def body(x_ref, g_ref, dy_ref, *rest):
        dx_ref, dg_ref = rest[-2], rest[-1]

        @pl.when(pl.program_id(0) == 0)
        def _():
            dg_ref[...] = jnp.zeros_like(dg_ref)

        xv, dyv = x_ref[...], dy_ref[...]
        r = lax.rsqrt(jnp.mean(xv * xv, axis=-1, keepdims=True) + RMS_EPS)
        u = dyv * g_ref[...]
        dx = r * u - xv * (r * r * r * jnp.mean(u * xv, axis=-1, keepdims=True))
        if dres is not None:
            dx = rest[0][...] + dx
        dx_ref[...] = dx.astype(out_dtype)
        dg_ref[...] += jnp.sum(dyv * xv * r, axis=0, keepdims=True)

    return pl.pallas_call(
        body, name=name, grid=(t // tm,), in_specs=in_specs, out_specs=[row, vec],
        out_shape=[jax.ShapeDtypeStruct((t, d), out_dtype), jax.ShapeDtypeStruct((1, d), F32)],
        compiler_params=_params("arbitrary"),
    )(*operands)


def loss_head(y, target, *, name, tm=ROW_BLOCK):
    t, d = y.shape
    tm = min(tm, t)
    nt = t // tm
    row = pl.BlockSpec((tm, d), lambda i: (i, 0))

    def body(y_ref, t_ref, loss_ref, dy_ref, acc_ref):
        i = pl.program_id(0)

        @pl.when(i == 0)
        def _():
            acc_ref[...] = jnp.zeros_like(acc_ref)

        err = y_ref[...] - t_ref[...]
        dy_ref[...] = err * (1.0 / d)
        acc_ref[...] += jnp.sum(err * err, axis=0, keepdims=True)

        @pl.when(i == nt - 1)
        def _():
            loss_ref[...] = jnp.sum(acc_ref[...], axis=1, keepdims=True) * (0.5 / d)

    return pl.pallas_call(
        body, name=name, grid=(nt,), in_specs=[row, row],
        out_specs=[pl.BlockSpec((1, 1), lambda i: (0, 0)), row],
        out_shape=[jax.ShapeDtypeStruct((1, 1), F32), jax.ShapeDtypeStruct((t, d), F32)],
        scratch_shapes=[pltpu.VMEM((1, d), F32)], compiler_params=_params("arbitrary"),
    )(y, target)


def _sb_block_iota():
    rows = lax.broadcasted_iota(jnp.int32, (SB_BLOCK, SB_BLOCK), 0)
    cols = lax.broadcasted_iota(jnp.int32, (SB_BLOCK, SB_BLOCK), 1)
    return rows, cols


def _sb_logits(q_h, k_blk, mask):
    z = _dot(q_h, k_blk, _NT)
    ls = _log_sigmoid(z)
    lm = ls - z
    if mask is not None:
        lm = jnp.where(mask, lm, 0.0)
    return ls, lm, jnp.sum(lm, axis=1, keepdims=True)


def _sb_weights(ls, lm, mask, tri_strict, later):
    suffix = _dot_exact_rhs(lm, tri_strict)
    w = jnp.exp(ls + suffix + later)
    return w if mask is None else jnp.where(mask, w, 0.0)


def _rows_from(x, lo, half):
    return x if not lo else jnp.concatenate([x[lo:half], x[half + lo:]], axis=0)


def _add_from(x, lo, half, upd):
    if not lo:
        return x + upd
    n = half - lo
    return jnp.concatenate([x[:lo], x[lo:half] + upd[:n], x[half:half + lo], x[half + lo:] + upd[n:]], axis=0)


def _on_grid_step(p, i):
    return jnp.logical_and(pl.program_id(0) == p, pl.program_id(1) == i)


def sb_fwd(qkv, *, name, gather=None):
    t, d3 = qkv.shape
    d = d3 // 3
    head_dim = d // SB_HEADS
    qb, kb_rows = SB_FWD_Q_BLOCK, SB_BLOCK
    assert 2 * head_dim == LANES and t % qb == 0
    pairs = d // LANES
    per_q = qb // kb_rows
    nq = t // qb
    scale = head_dim ** -0.5

    def body(q_ref, k_ref, v_ref, *rest):
        if gather is None:
            compute(q_ref, k_ref, v_ref, *rest)
            return
        x_ref, o_ref, tot_ref, out_ref, acc_ref, later_ref, send_sems, recv_sems, local_sem = rest
        start, forward, finish = _gather_plan(x_ref, out_ref, send_sems, recv_sems, local_sem)
        pl.when(_on_grid_step(0, 0))(start)
        pl.when(_on_grid_step(7 * pairs // 8, 0))(forward)
        compute(q_ref, k_ref, v_ref, o_ref, tot_ref, acc_ref, later_ref)
        pl.when(_on_grid_step(pairs - 1, nq - 1))(finish)

    def compute(q_ref, k_ref, v_ref, o_ref, tot_ref, acc_ref, later_ref):
        qi = pl.program_id(1)
        lane = lax.broadcasted_iota(jnp.int32, (qb, LANES), 1)
        first = lane < head_dim
        q = q_ref[...] * scale
        q2 = jnp.concatenate([jnp.where(first, q, jnp.zeros_like(q)), jnp.where(first, jnp.zeros_like(q), q)], axis=0)
        rows, cols = _sb_block_iota()
        tri = jnp.where(rows > cols, 1.0, 0.0).astype(BF16)
        q_row = lax.broadcasted_iota(jnp.int32, (qb, kb_rows), 0)
        k_col = lax.broadcasted_iota(jnp.int32, (qb, kb_rows), 1)
        earlier = k_col < q_row
        acc_ref[...] = jnp.zeros_like(acc_ref)
        later_ref[...] = jnp.zeros_like(later_ref)

        def step(kb, lo=None):
            ks = pl.multiple_of(kb * SB_BLOCK, SB_BLOCK)
            k_blk = k_ref[pl.ds(ks, SB_BLOCK), :]
            v_blk = v_ref[pl.ds(ks, SB_BLOCK), :]
            if not lo:
                mask = None if lo is None else jnp.concatenate([earlier] * 2, axis=0)
                ls, lm, row = _sb_logits(q2, k_blk, mask)
                w = _sb_weights(ls, lm, mask, tri, later_ref[...])
                acc_ref[...] += _dot(w, v_blk)
                later_ref[...] += row
                return
            n = qb - lo
            mask = jnp.concatenate([earlier[:n]] * 2, axis=0)
            later = jnp.concatenate([later_ref[lo:qb], later_ref[qb + lo:]], axis=0)
            ls, lm, row = _sb_logits(_rows_from(q2, lo, qb), k_blk, mask)
            upd = _dot(_sb_weights(ls, lm, mask, tri, later), v_blk)
            for base in (0, qb):
                half = slice(0, n) if base == 0 else slice(n, 2 * n)
                acc_ref[base + lo:base + qb] += upd[half]
                later_ref[base + lo:base + qb] += row[half]

        for m in reversed(range(per_q)):
            step(per_q * qi + m, m * kb_rows)

        @pl.loop(0, per_q * qi)
        def _(i):
            step(per_q * qi - 1 - i)

        acc, total = acc_ref[...], later_ref[...]
        o_ref[...] = jnp.where(first, acc[:qb], acc[qb:]).astype(o_ref.dtype)
        tot_ref[...] = jnp.where(first, total[:qb], total[qb:])

    blk = lambda off: pl.BlockSpec((t, LANES), lambda p, i: (0, off + p))
    qblk = pl.BlockSpec((qb, LANES), lambda p, i: (i, p))
    in_specs, operands = [qblk, blk(pairs), blk(2 * pairs)], [qkv, qkv, qkv]
    out_specs = [qblk, qblk]
    out_shape = [jax.ShapeDtypeStruct((t, d), BF16), jax.ShapeDtypeStruct((t, d), F32)]
    sums = [pltpu.VMEM((2 * qb, LANES), F32), pltpu.VMEM((2 * qb, 1), F32)]
    if gather is None:
        return pl.pallas_call(
            body, name=name, grid=(pairs, nq), in_specs=in_specs, out_specs=out_specs, out_shape=out_shape,
            scratch_shapes=sums, compiler_params=_params("parallel", "arbitrary"),
        )(*operands)
    whole = pl.BlockSpec(memory_space=pl.ANY)
    return pl.pallas_call(
        body, name=name, grid=(pairs, nq), in_specs=in_specs + [whole], out_specs=out_specs + [whole],
        out_shape=out_shape + [jax.ShapeDtypeStruct((N_DEV * gather.shape[0], gather.shape[1]), gather.dtype)],
        scratch_shapes=sums + _comm_scratch(), compiler_params=_params("arbitrary", "arbitrary"),
    )(*operands, gather)


def sb_bwd(qkv, tot, do, *, name, exchange=None):
    t, d3 = qkv.shape
    d = d3 // 3
    head_dim = d // SB_HEADS
    pairs = d // LANES
    qb, kb_rows = SB_BWD_Q_BLOCK, SB_BLOCK
    per_q = qb // kb_rows
    nq = t // qb
    scale = head_dim ** -0.5

    def body(*refs):
        if exchange is None:
            compute(*refs)
            return
        q_ref, k_ref, v_ref, tot_ref, do_ref, in_ref, dq_ref, dk_ref, dv_ref, out_ref = refs[:10]
        dk_sum, dv_sum, send_sems, recv_sems, local_sem = refs[10:]
        start, finish = _exchange_plan(in_ref, out_ref, send_sems, recv_sems, local_sem)
        pl.when(_on_grid_step(0, 0))(start)
        compute(q_ref, k_ref, v_ref, tot_ref, do_ref, dq_ref, dk_ref, dv_ref, dk_sum, dv_sum)
        pl.when(_on_grid_step(pairs - 1, nq - 1))(finish)

    def compute(q_ref, k_ref, v_ref, tot_ref, do_ref, dq_ref, dk_ref, dv_ref, dk_sum, dv_sum):
        qi = pl.program_id(1)

        @pl.when(qi == 0)
        def _():
            dk_sum[...] = jnp.zeros_like(dk_sum)
            dv_sum[...] = jnp.zeros_like(dv_sum)

        lane = lax.broadcasted_iota(jnp.int32, (qb, LANES), 1)
        first = lane < head_dim
        q, dov, totv = q_ref[...] * scale, do_ref[...], tot_ref[...]
        second = jnp.logical_not(first)
        q2 = jnp.concatenate([jnp.where(s, q, jnp.zeros_like(q)) for s in (first, second)], axis=0)
        do2 = jnp.concatenate([jnp.where(s, dov, jnp.zeros_like(dov)) for s in (first, second)], axis=0)
        tot2 = jnp.concatenate([totv[:, 0:1], totv[:, head_dim:head_dim + 1]], axis=0)
        rows, cols = _sb_block_iota()
        tri_strict = jnp.where(rows > cols, 1.0, 0.0).astype(BF16)
        tri_before = jnp.where(rows < cols, 1.0, 0.0).astype(BF16)
        q_row = lax.broadcasted_iota(jnp.int32, (qb, kb_rows), 0)
        k_col = lax.broadcasted_iota(jnp.int32, (qb, kb_rows), 1)
        earlier = k_col < q_row

        def step(kb, carry, lo=None):
            ks = pl.multiple_of(kb * SB_BLOCK, SB_BLOCK)
            k_blk = k_ref[pl.ds(ks, SB_BLOCK), :]
            v_blk = v_ref[pl.ds(ks, SB_BLOCK), :]
            dq, seen, before = carry
            mask = None if lo is None else jnp.concatenate([earlier[:qb - lo]] * 2, axis=0)
            q_s, do_s = _rows_from(q2, lo, qb), _rows_from(do2, lo, qb)
            ls, lm, row = _sb_logits(q_s, k_blk, mask)
            seen = _add_from(seen, lo, qb, row)
            w = _sb_weights(ls, lm, mask, tri_strict, _rows_from(tot2 - seen, lo, qb))
            da = _dot(do_s, v_blk, _NT) * w
            g = _dot_exact_rhs(da, tri_before) + _rows_from(before, lo, qb)
            dz = da - jnp.exp(ls) * (da + g)
            if mask is not None:
                dz = jnp.where(mask, dz, 0.0)
            dk_sum[pl.ds(ks, SB_BLOCK), :] += _dot(dz, q_s, _TN)
            dv_sum[pl.ds(ks, SB_BLOCK), :] += _dot(w, do_s, _TN)
            return (_add_from(dq, lo, qb, _dot(dz, k_blk * scale)), seen,
                    _add_from(before, lo, qb, jnp.sum(da, axis=1, keepdims=True)))

        zero = jnp.zeros((2 * qb, LANES), F32)
        zcol = jnp.zeros((2 * qb, 1), F32)
        out = lax.fori_loop(0, per_q * qi, step, (zero, zcol, zcol))
        for m in range(per_q):
            out = step(per_q * qi + m, out, m * kb_rows)
        dq = out[0]
        dq_ref[...] = jnp.where(first, dq[:qb], dq[qb:]).astype(dq_ref.dtype)

        @pl.when(qi == nq - 1)
        def _():
            dk_ref[...] = dk_sum[...].astype(dk_ref.dtype)
            dv_ref[...] = dv_sum[...].astype(dv_ref.dtype)

    qblk = pl.BlockSpec((qb, LANES), lambda p, i: (i, p))
    col = lambda off: pl.BlockSpec((t, LANES), lambda p, i: (0, off + p))
    shape = jax.ShapeDtypeStruct((t, d), BF16)
    in_specs, operands = [qblk, col(pairs), col(2 * pairs), qblk, qblk], [qkv, qkv, qkv, tot, do]
    out_specs, out_shape = [qblk, col(0), col(0)], [shape, shape, shape]
    sums = [pltpu.VMEM((t, LANES), F32), pltpu.VMEM((t, LANES), F32)]
    if exchange is None:
        return pl.pallas_call(
            body, name=name, grid=(pairs, nq), in_specs=in_specs, out_specs=out_specs, out_shape=out_shape,
            scratch_shapes=sums, compiler_params=_params("parallel", "arbitrary"),
        )(*operands)
    whole = pl.BlockSpec(memory_space=pl.ANY)
    return pl.pallas_call(
        body, name=name, grid=(pairs, nq), in_specs=in_specs + [whole], out_specs=out_specs + [whole],
        out_shape=out_shape + [jax.ShapeDtypeStruct(exchange.shape, exchange.dtype)],
        scratch_shapes=sums + _comm_scratch(), compiler_params=_params("arbitrary", "arbitrary"),
    )(*operands, exchange)


def _shift_down(x, s):
    rows = lax.broadcasted_iota(jnp.int32, x.shape, 0)
    return jnp.where(rows >= s, pltpu.roll(x, s, 0), 0.0)


def _shift_up(x, s):
    t = x.shape[0]
    rows = lax.broadcasted_iota(jnp.int32, x.shape, 0)
    return jnp.where(rows < t - s, pltpu.roll(x, t - s, 0), 0.0)


def conv_fwd(bcu, w, *, name):
    t, d3 = bcu.shape
    d = d3 // 3
    nb = d // LANES
    col = lambda off: pl.BlockSpec((t, LANES), lambda j: (0, off + j))

    def body(b_ref, c_ref, u_ref, w_ref, y_ref):
        hh = c_ref[...] * u_ref[...]
        conv = w_ref[0:1, :] * _shift_down(hh, 2) + w_ref[1:2, :] * _shift_down(hh, 1) + w_ref[2:3, :] * hh
        y_ref[...] = (b_ref[...] * conv).astype(y_ref.dtype)

    return pl.pallas_call(
        body, name=name, grid=(nb,),
        in_specs=[col(0), col(nb), col(2 * nb), pl.BlockSpec((CONV_WIDTH, LANES), lambda j: (0, j))],
        out_specs=col(0), out_shape=jax.ShapeDtypeStruct((t, d), BF16), compiler_params=_params("parallel"),
    )(bcu, bcu, bcu, w)


def conv_bwd(bcu, w, dy, *, name):
    t, d3 = bcu.shape
    d = d3 // 3
    nb = d // LANES
    col = lambda off: pl.BlockSpec((t, LANES), lambda j: (0, off + j))
    wspec = pl.BlockSpec((CONV_WIDTH, LANES), lambda j: (0, j))

    def body(b_ref, c_ref, u_ref, w_ref, dy_ref, db_ref, dc_ref, du_ref, dw_ref):
        c, u, dyv = c_ref[...], u_ref[...], dy_ref[...]
        hh = c * u
        h2, h1 = _shift_down(hh, 2), _shift_down(hh, 1)
        w0, w1, w2 = w_ref[0:1, :], w_ref[1:2, :], w_ref[2:3, :]
        db_ref[...] = (dyv * (w0 * h2 + w1 * h1 + w2 * hh)).astype(db_ref.dtype)
        dconv = dyv * b_ref[...]
        dhh = w2 * dconv + w1 * _shift_up(dconv, 1) + w0 * _shift_up(dconv, 2)
        dc_ref[...] = (dhh * u).astype(dc_ref.dtype)
        du_ref[...] = (dhh * c).astype(du_ref.dtype)
        dw_ref[0:1, :] = jnp.sum(dconv * h2, axis=0, keepdims=True)
        dw_ref[1:2, :] = jnp.sum(dconv * h1, axis=0, keepdims=True)
        dw_ref[2:3, :] = jnp.sum(dconv * hh, axis=0, keepdims=True)

    shape = jax.ShapeDtypeStruct((t, d), BF16)
    return pl.pallas_call(
        body, name=name, grid=(nb,),
        in_specs=[col(0), col(nb), col(2 * nb), wspec, col(0)],
        out_specs=[col(0), col(0), col(0), wspec],
        out_shape=[shape, shape, shape, jax.ShapeDtypeStruct((CONV_WIDTH, d), F32)],
        compiler_params=_params("parallel"),
    )(bcu, bcu, bcu, w, dy)


def _gla_chunk(q, k, lg, scale):
    c = GLA_CHUNK
    rows = lax.broadcasted_iota(jnp.int32, (c, c), 0)
    cols = lax.broadcasted_iota(jnp.int32, (c, c), 1)
    causal = rows >= cols
    tril = jnp.where(causal, 1.0, 0.0).astype(BF16)
    q = q * scale
    cum = _dot_exact_lhs(tril, lg)
    last = cum[c - 1:c, :]
    eq = jnp.exp(cum)
    el = jnp.exp(last - cum)
    return causal, tril, q, k, cum, lg, last, eq, el


def _gla_sub_blocks(q, k, cum, lg):
    key_row = lax.broadcasted_iota(jnp.int32, (GLA_CHUNK, 1), 0)
    out = []
    for lo in range(0, GLA_CHUNK, GLA_SUB):
        hi = lo + GLA_SUB
        ref = cum[lo:lo + 1, :] - lg[lo:lo + 1, :]
        eq = jnp.exp(cum[lo:hi] - ref)
        ek = jnp.where(key_row < hi, jnp.exp(ref - cum), 0.0)
        out.append((slice(lo, hi), eq, ek, q[lo:hi] * eq, k * ek))
    return out


def _gla_scores(subs, causal):
    return jnp.where(causal, jnp.concatenate([_dot(qt, kt, _NT) for _, _, _, qt, kt in subs], axis=0), 0.0)


def gla_fwd(proj, lg, *, name):
    t, d3 = proj.shape
    d = d3 // 3
    dk, dv = d // 2 // GLA_HEADS, d // GLA_HEADS
    assert dk == LANES and dv == 2 * LANES
    c = GLA_CHUNK
    nc = t // c
    scale = dk ** -0.5
    nh = GLA_HEADS

    def body(q_ref, k_ref, v_ref, lg_ref, o_ref, st_out_ref, st_ref):
        @pl.when(pl.program_id(0) == 0)
        def _():
            st_ref[...] = jnp.zeros_like(st_ref)

        for h in range(nh):
            kcols, vcols = slice(h * dk, (h + 1) * dk), slice(h * dv, (h + 1) * dv)
            causal, _, q, k, cum, lg, last, eq, el = _gla_chunk(q_ref[:, kcols], k_ref[:, kcols], lg_ref[:, kcols], scale)
            v = v_ref[:, vcols]
            st = st_ref[h]
            st_out_ref[h] = st
            scores = _gla_scores(_gla_sub_blocks(q, k, cum, lg), causal)
            o_ref[:, vcols] = _dot(q * eq, st, _NT) + _dot(scores, v)
            st_ref[h] = st * jnp.exp(last) + _dot(v, k * el, _TN)

    half, full = pl.BlockSpec((c, d // 2), lambda i: (i, 0)), pl.BlockSpec((c, d), lambda i: (i, 0))
    return pl.pallas_call(
        body, name=name, grid=(nc,),
        in_specs=[half, pl.BlockSpec((c, d // 2), lambda i: (i, 1)), pl.BlockSpec((c, d), lambda i: (i, 1)), half],
        out_specs=[full, pl.BlockSpec((nh, None, dv, dk), lambda i: (0, i, 0, 0))],
        out_shape=[jax.ShapeDtypeStruct((t, d), F32), jax.ShapeDtypeStruct((nh, nc, dv, dk), F32)],
        scratch_shapes=[pltpu.VMEM((nh, dv, dk), F32)], compiler_params=_params("arbitrary"),
    )(proj, proj, proj, lg)


def gla_bwd(proj, lg, states, do, *, name):
    t, d3 = proj.shape
    d = d3 // 3
    dk, dv = d // 2 // GLA_HEADS, d // GLA_HEADS
    c = GLA_CHUNK
    nc = t // c
    scale = dk ** -0.5
    nh = GLA_HEADS

    def body(q_ref, k_ref, v_ref, lg_ref, st_ref, do_ref, dq_ref, dk_ref, dv_ref, dzg_ref, dbg_ref, dst_ref):
        @pl.when(pl.program_id(0) == 0)
        def _():
            dst_ref[...] = jnp.zeros_like(dst_ref)
            dbg_ref[...] = jnp.zeros_like(dbg_ref)

        for h in range(nh):
            kcols, vcols = slice(h * dk, (h + 1) * dk), slice(h * dv, (h + 1) * dv)
            causal, tril, q, k, cum, lg, last, eq, el = _gla_chunk(q_ref[:, kcols], k_ref[:, kcols], lg_ref[:, kcols], scale)
            v, st, dov, dst = v_ref[:, vcols], st_ref[h], do_ref[:, vcols], dst_ref[h]
            subs = _gla_sub_blocks(q, k, cum, lg)
            qt, kh = q * eq, k * el
            scores = _gla_scores(subs, causal)
            dscores = jnp.where(causal, _dot(dov, v, _NT), 0.0)
            dq_parts = []
            dkh = _dot(v, dst)
            dk_h = dkh * el
            for rows, eq_sub, ek_sub, qt_sub, kt_sub in subs:
                dq_parts.append(_dot(dscores[rows], kt_sub) * eq_sub)
                dk_h = dk_h + _dot(dscores[rows], qt_sub, _TN) * ek_sub
            dq = _dot(dov, st) * eq + jnp.concatenate(dq_parts, axis=0)
            dv_ref[:, vcols] = (_dot(scores, dov, _TN) + _dot(kh, dst, _NT)).astype(dv_ref.dtype)
            dq_ref[:, kcols] = (dq * scale).astype(dq_ref.dtype)
            dk_ref[:, kcols] = dk_h.astype(dk_ref.dtype)
            e_last = jnp.exp(last)
            dlast = jnp.sum(kh * dkh, axis=0, keepdims=True) + e_last * jnp.sum(dst * st, axis=0, keepdims=True)
            dcum = q * dq - k * dk_h
            dlg = _dot_exact_lhs(tril, dcum, _TN) + dlast
            dzg = dlg * (1.0 - jnp.exp(lg * GLA_GATE_NORMALIZER)) / GLA_GATE_NORMALIZER
            dzg_ref[:, kcols] = dzg.astype(dzg_ref.dtype)
            dbg_ref[:, kcols] += jnp.sum(dzg, axis=0, keepdims=True)
            dst_ref[h] = dst * e_last + _dot(dov, qt, _TN)

    rev = lambda i: nc - 1 - i
    half, full = pl.BlockSpec((c, d // 2), lambda i: (rev(i), 0)), pl.BlockSpec((c, d), lambda i: (rev(i), 0))
    half_shape = jax.ShapeDtypeStruct((t, d // 2), BF16)
    return pl.pallas_call(
        body, name=name, grid=(nc,),
        in_specs=[half, pl.BlockSpec((c, d // 2), lambda i: (rev(i), 1)), pl.BlockSpec((c, d), lambda i: (rev(i), 1)), half,
                  pl.BlockSpec((nh, None, dv, dk), lambda i: (0, rev(i), 0, 0)), full],
        out_specs=[half, half, full, half, pl.BlockSpec((1, d // 2), lambda i: (0, 0))],
        out_shape=[half_shape, half_shape, jax.ShapeDtypeStruct((t, d), BF16), half_shape,
                   jax.ShapeDtypeStruct((1, d // 2), F32)],
        scratch_shapes=[pltpu.VMEM((nh, dv, dk), F32)], compiler_params=_params("arbitrary"),
    )(proj, proj, proj, lg, states, do)


def gla_post_fwd(o, proj, head_norm, *, name, tm=ROW_BLOCK):
    t, d = o.shape
    dv = d // GLA_HEADS
    tm = min(tm, t)

    def body(o_ref, g_ref, hn_ref, y_ref):
        for h in range(GLA_HEADS):
            sl = slice(h * dv, (h + 1) * dv)
            ov, gv = o_ref[:, sl], g_ref[:, sl]
            r = lax.rsqrt(jnp.mean(ov * ov, axis=-1, keepdims=True) + RMS_EPS)
            y_ref[:, sl] = ((ov * r * hn_ref[:, sl]) * (gv * _sigmoid(gv))).astype(y_ref.dtype)

    row = pl.BlockSpec((tm, d), lambda i: (i, 0))
    return pl.pallas_call(
        body, name=name, grid=(t // tm,),
        in_specs=[row, pl.BlockSpec((tm, d), lambda i: (i, 2)), pl.BlockSpec((1, d), lambda i: (0, 0))],
        out_specs=row, out_shape=jax.ShapeDtypeStruct((t, d), BF16), compiler_params=_params("parallel"),
    )(o, proj, head_norm)


def gla_post_bwd(o, proj, head_norm, dy, *, name, tm=ROW_BLOCK):
    t, d = o.shape
    dv = d // GLA_HEADS
    tm = min(tm, t)

    def body(o_ref, g_ref, hn_ref, dy_ref, do_ref, dg_ref, dhn_ref):
        @pl.when(pl.program_id(0) == 0)
        def _():
            dhn_ref[...] = jnp.zeros_like(dhn_ref)

        for h in range(GLA_HEADS):
            sl = slice(h * dv, (h + 1) * dv)
            ov, gv, dyv, hn = o_ref[:, sl], g_ref[:, sl], dy_ref[:, sl], hn_ref[:, sl]
            r = lax.rsqrt(jnp.mean(ov * ov, axis=-1, keepdims=True) + RMS_EPS)
            sg = _sigmoid(gv)
            silu = gv * sg
            on = ov * r * hn
            dg_ref[:, sl] = (dyv * on * (sg * (1.0 + gv * (1.0 - sg)))).astype(dg_ref.dtype)
            don = dyv * silu
            u = don * hn
            do_ref[:, sl] = (r * u - ov * (r * r * r * jnp.mean(u * ov, axis=-1, keepdims=True))).astype(do_ref.dtype)
            dhn_ref[:, sl] += jnp.sum(don * ov * r, axis=0, keepdims=True)

    row = pl.BlockSpec((tm, d), lambda i: (i, 0))
    vec = pl.BlockSpec((1, d), lambda i: (0, 0))
    shape = jax.ShapeDtypeStruct((t, d), BF16)
    return pl.pallas_call(
        body, name=name, grid=(t // tm,),
        in_specs=[row, pl.BlockSpec((tm, d), lambda i: (i, 2)), vec, row],
        out_specs=[row, row, vec], out_shape=[shape, shape, jax.ShapeDtypeStruct((1, d), F32)],
        compiler_params=_params("arbitrary"),
    )(o, proj, head_norm, dy)


def _ffn_fwd(h, gains, w_up, w_down, tag):
    xn = rms_fwd(h, gains[2], name=f"{tag}_ffn_norm", out_dtype=BF16)
    act = matmul(xn, w_up, mode='nn', epi='relu2', out_dtype=BF16, name=f"{tag}_ffn_up")
    f = matmul(act, w_down, mode='nn', name=f"{tag}_ffn_down")
    h_out = rms_fwd(f, gains[3], res=h, name=f"{tag}_ffn_out")
    return h_out, (h, xn, act, f)


def _ffn_bwd(dh, saved, gains, w_up, w_down, tag):
    h, xn, act, f = saved
    df, dg3 = rms_bwd(f, gains[3], dh, out_dtype=BF16, name=f"{tag}_ffn_out_bwd")
    du = matmul(df, w_down, mode='nt', epi='mul2sqrt', extra=act, out_dtype=BF16, name=f"{tag}_ffn_da")
    dw_down = matmul(act, df, mode='tn', out_dtype=BF16, name=f"{tag}_ffn_dwdown")
    dw_up = matmul(xn, du, mode='tn', out_dtype=BF16, name=f"{tag}_ffn_dwup")
    dxn = matmul(du, w_up, mode='nt', name=f"{tag}_ffn_dxn")
    dh_in, dg2 = rms_bwd(h, gains[2], dxn, dres=dh, name=f"{tag}_ffn_norm_bwd")
    return dh_in, dg2, dg3, dw_up, dw_down


def _sb_layer_fwd(xn, w, j, tag, comm=None):
    qkv = matmul(xn, w['sb_w_qkv'][j], mode='nn', out_dtype=BF16, name=f"{tag}_qkv")
    if comm is None:
        o, tot = sb_fwd(qkv, name=f"{tag}_sb")
    else:
        o, tot, gathered = sb_fwd(qkv, name=f"{tag}_sb", gather=comm.rest_payload)
        comm.on_gathered(gathered)
    m = matmul(o, w['sb_w_o'][j], mode='nn', name=f"{tag}_wo")
    return m, (qkv, o, tot)


def _sb_layer_bwd(dm, xn, saved, w, j, tag, comm=None, grads=None):
    qkv, o, tot = saved
    do = matmul(dm, w['sb_w_o'][j], mode='nt', out_dtype=BF16, name=f"{tag}_do")
    dw_o = matmul(o, dm, mode='tn', out_dtype=BF16, name=f"{tag}_dwo")
    if comm is None:
        dq, dk, dv = sb_bwd(qkv, tot, do, name=f"{tag}_sb_bwd")
    else:
        parts = comm.rest_parts({**grads, ('sb_w_o', j): dw_o})
        dq, dk, dv, received = sb_bwd(qkv, tot, do, name=f"{tag}_sb_bwd", exchange=parts)
        comm.on_received(received)
    dqkv = jnp.concatenate([dq, dk, dv], axis=1)
    dw_qkv = matmul(xn, dqkv, mode='tn', out_dtype=BF16, name=f"{tag}_dwqkv")
    dxn = matmul(dqkv, w['sb_w_qkv'][j], mode='nt', name=f"{tag}_dxn")
    return dxn, {('sb_w_qkv', j): dw_qkv, ('sb_w_o', j): dw_o}


def _conv_layer_fwd(xn, w, j, tag):
    bcu = matmul(xn, w['conv_w_in'][j], mode='nn', name=f"{tag}_in")
    y = conv_fwd(bcu, w['conv_w'][j], name=f"{tag}_conv")
    m = matmul(y, w['conv_w_out'][j], mode='nn', name=f"{tag}_out")
    return m, (bcu, y)


def _conv_layer_bwd(dm, xn, saved, w, j, tag):
    bcu, y = saved
    dy = matmul(dm, w['conv_w_out'][j], mode='nt', name=f"{tag}_dy")
    dw_out = matmul(y, dm, mode='tn', out_dtype=BF16, name=f"{tag}_dwout")
    db, dc, du, dw_conv = conv_bwd(bcu, w['conv_w'][j], dy, name=f"{tag}_conv_bwd")
    dbcu = jnp.concatenate([db, dc, du], axis=1)
    dw_in = matmul(xn, dbcu, mode='tn', out_dtype=BF16, name=f"{tag}_dwin")
    dxn = matmul(dbcu, w['conv_w_in'][j], mode='nt', name=f"{tag}_dxn")
    return dxn, {('conv_w_in', j): dw_in, ('conv_w', j): dw_conv, ('conv_w_out', j): dw_out}


def _gla_split(w_in, w_gate_up):
    d = w_in.shape[0]
    w_main = w_in[:, :3 * d]
    w_a = jnp.pad(w_in[:, 3 * d:], ((0, 0), (0, LANES - GLA_GATE_RANK)))
    w_gu = jnp.pad(w_gate_up, ((0, LANES - GLA_GATE_RANK), (0, 0)))
    return w_main, w_a, w_gu


def _gla_layer_fwd(xn, w, j, tag):
    w_main, w_a, w_gu = _gla_split(w['gla_w_in'][j], w['gla_w_gate_up'][j])
    proj = matmul(xn, w_main, mode='nn', name=f"{tag}_in")
    a_low = matmul(xn, w_a, mode='nn', out_dtype=BF16, name=f"{tag}_alow")
    lg = matmul(a_low, w_gu, mode='nn', epi='logsig16', extra=w['gla_b_gate'][j][None, :], name=f"{tag}_gate")
    o, states = gla_fwd(proj, lg, name=f"{tag}_gla")
    hn = w['gla_head_norm'][j].reshape(1, -1)
    y = gla_post_fwd(o, proj, hn, name=f"{tag}_post")
    m = matmul(y, w['gla_w_o'][j], mode='nn', name=f"{tag}_wo")
    return m, (proj, a_low, lg, o, states, y)


def _gla_layer_bwd(dm, xn, saved, w, j, tag):
    proj, a_low, lg, o, states, y = saved
    w_main, w_a, w_gu = _gla_split(w['gla_w_in'][j], w['gla_w_gate_up'][j])
    hn = w['gla_head_norm'][j].reshape(1, -1)
    dy = matmul(dm, w['gla_w_o'][j], mode='nt', name=f"{tag}_dy")
    dw_o = matmul(y, dm, mode='tn', out_dtype=BF16, name=f"{tag}_dwo")
    do, dg, dhn = gla_post_bwd(o, proj, hn, dy, name=f"{tag}_post_bwd")
    dq, dk, dv, dzg, dbg = gla_bwd(proj, lg, states, do, name=f"{tag}_gla_bwd")
    da_low = matmul(dzg, w_gu, mode='nt', out_dtype=BF16, name=f"{tag}_dalow")
    dw_gu = matmul(a_low, dzg, mode='tn', out_dtype=BF16, name=f"{tag}_dwgu")[:GLA_GATE_RANK]
    dproj = jnp.concatenate([dq, dk, dv, dg], axis=1)
    dw_main = matmul(xn, dproj, mode='tn', out_dtype=BF16, name=f"{tag}_dwin")
    dw_a = matmul(xn, da_low, mode='tn', out_dtype=BF16, name=f"{tag}_dwa")[:, :GLA_GATE_RANK]
    dxn_a = matmul(da_low, w_a, mode='nt', name=f"{tag}_dxn_a")
    dxn = matmul(dproj, w_main, mode='nt', epi='add', extra=dxn_a, name=f"{tag}_dxn")
    grads = {('gla_w_in', j): jnp.concatenate([dw_main, dw_a], axis=1), ('gla_w_gate_up', j): dw_gu,
             ('gla_b_gate', j): dbg[0], ('gla_head_norm', j): dhn.reshape(w['gla_head_norm'][j].shape),
             ('gla_w_o', j): dw_o}
    return dxn, grads


_MIXERS = ((_sb_layer_fwd, _sb_layer_bwd), (_conv_layer_fwd, _conv_layer_bwd), (_gla_layer_fwd, _gla_layer_bwd))


def local_step(x, w, target, comm=None):
    depth = len(w['norm_gains'])
    h = x
    tape = []
    for i in range(depth):
        kind, j = i % 3, i // 3
        tag = f"l{i}"
        extra = {'comm': comm} if (comm is not None and i == 0) else {}
        gains = [w['norm_gains'][i][s][None, :] for s in range(4)]
        xn = rms_fwd(h, gains[0], name=f"{tag}_mix_norm", out_dtype=BF16)
        m, saved = _MIXERS[kind][0](xn, w, j, tag, **extra)
        h_mid = rms_fwd(m, gains[1], res=h, name=f"{tag}_mix_out")
        h_out, ffn_saved = _ffn_fwd(h_mid, gains, w['ffn_w_up'][i], w['ffn_w_down'][i], tag)
        tape.append((h, xn, m, saved, ffn_saved, gains))
        h = h_out
    loss, dh = loss_head(h, target, name="loss_head")

    grads = {}
    for i in reversed(range(depth)):
        kind, j = i % 3, i // 3
        tag = f"l{i}"
        h_in, xn, m, saved, ffn_saved, gains = tape[i]
        dg = [None] * 4
        dh, dg[2], dg[3], grads[('ffn_w_up', i)], grads[('ffn_w_down', i)] = _ffn_bwd(
            dh, ffn_saved, gains, w['ffn_w_up'][i], w['ffn_w_down'][i], tag)
        dm, dg[1] = rms_bwd(m, gains[1], dh, out_dtype=BF16, name=f"{tag}_mix_out_bwd")
        extra = {'comm': comm, 'grads': grads} if (comm is not None and i == 0) else {}
        dxn, g = _MIXERS[kind][1](dm, xn, saved, w, j, tag, **extra)
        grads.update(g)
        dh, dg[0] = rms_bwd(h_in, gains[0], dxn, dres=dh, name=f"{tag}_mix_norm_bwd")
        grads[('norm_gains', i)] = jnp.concatenate(dg, axis=0)
    return loss, dh, grads


def _segments(keys, shard_shapes, f32_as_pairs):
    segs, row = {}, 0
    for name, lo, hi in keys:
        n = (hi - lo) * math.prod(shard_shapes[name][1:]) * (2 if f32_as_pairs and name in F32_PAYLOAD else 1)
        nrows = -(-n // (PACK_COLS * PACK_ROW_ALIGN)) * PACK_ROW_ALIGN
        segs[(name, lo, hi)] = (row, nrows, n)
        row += nrows
    return segs, -(-row // PACK_ROW_BLOCK) * PACK_ROW_BLOCK


def _pack(parts, segs, total_rows, dtype):
    pieces, row = [], 0
    for key in segs:
        _, nrows, n = segs[key]
        p = parts[key].astype(dtype)
        lead = p.shape[:-1]
        if nrows * PACK_COLS > n:
            p = jnp.pad(p, [(0, 0)] * len(lead) + [(0, nrows * PACK_COLS - n)])
        pieces.append(p.reshape(lead + (nrows, PACK_COLS)))
        row += nrows
    if total_rows > row:
        pieces.append(jnp.zeros(lead + (total_rows - row, PACK_COLS), dtype))
    return jnp.concatenate(pieces, axis=-2)


def _unpack(buf, seg):
    first, nrows, n = seg
    piece = buf[..., first:first + nrows, :]
    return piece.reshape(piece.shape[:-2] + (nrows * PACK_COLS,))[..., :n]


def _unshard(gathered, axis):
    moved = jnp.moveaxis(gathered, 0, axis)
    shape = moved.shape
    return moved.reshape(shape[:axis] + (shape[axis] * shape[axis + 1],) + shape[axis + 2:])


def _shard_split(full, axis):
    shape = full.shape
    cut = full.reshape(shape[:axis] + (N_DEV, shape[axis] // N_DEV) + shape[axis + 1:])
    return jnp.moveaxis(cut, axis, 0)


def _mesh_position():
    return lax.axis_index("x"), lax.axis_index("y"), lax.axis_index("c")


def _comm_scratch():
    return [pltpu.SemaphoreType.DMA((N_DEV - 1,)), pltpu.SemaphoreType.DMA((N_DEV - 1,)), pltpu.SemaphoreType.DMA]


def _gather_plan(x_ref, out_ref, send_sems, recv_sems, local_sem):
    rows = x_ref.shape[0]
    x, y, c = _mesh_position()
    me, sibling = (x, y, c), (x, y, 1 - c)
    chips = [(1 - x, y), (x, 1 - y), (1 - x, 1 - y)]

    def block(px, py, pc):
        return out_ref.at[pl.ds((4 * px + 2 * py + pc) * rows, rows), :]

    def copy(k, blk, to, src=None):
        return pltpu.make_async_remote_copy(
            src_ref=block(*blk) if src is None else src, dst_ref=block(*blk),
            send_sem=send_sems.at[k], recv_sem=recv_sems.at[k],
            device_id=to, device_id_type=pl.DeviceIdType.MESH)

    mine = pltpu.make_async_copy(x_ref, block(*me), local_sem)
    first = [copy(0, me, sibling, src=x_ref)]
    first += [copy(1 + j, me, (*chip, c), src=x_ref) for j, chip in enumerate(chips)]
    passed = [copy(4 + j, (*chip, c), sibling) for j, chip in enumerate(chips)]

    def start():
        mine.start()
        for cp in first:
            cp.start()

    def forward():
        for j, chip in enumerate(chips):
            copy(1 + j, (*chip, c), me).wait_recv()
            passed[j].start()

    def finish():
        copy(0, sibling, me).wait_recv()
        for j, chip in enumerate(chips):
            copy(4 + j, (*chip, 1 - c), me).wait_recv()
        for cp in first + passed:
            cp.wait_send()
        mine.wait()

    return start, forward, finish


def _exchange_plan(in_ref, out_ref, send_sems, recv_sems, local_sem):
    x, y, c = _mesh_position()
    my_id = 4 * x + 2 * y + c
    mine = pltpu.make_async_copy(in_ref.at[my_id], out_ref.at[my_id], local_sem)

    def copy(k, receive):
        px = 1 - x if k & 4 else x
        py = 1 - y if k & 2 else y
        pc = 1 - c if k & 1 else c
        peer_id = 4 * px + 2 * py + pc
        return pltpu.make_async_remote_copy(
            src_ref=in_ref.at[peer_id], dst_ref=out_ref.at[peer_id if receive else my_id],
            send_sem=send_sems.at[k - 1], recv_sem=recv_sems.at[k - 1],
            device_id=(px, py, pc), device_id_type=pl.DeviceIdType.MESH)

    def start():
        mine.start()
        for k in range(1, N_DEV):
            copy(k, False).start()

    def finish():
        for k in range(1, N_DEV):
            copy(k, True).wait_recv()
        for k in range(1, N_DEV):
            copy(k, False).wait_send()
        mine.wait()

    return start, finish


def all_gather(shard, *, name):
    rows, cols = shard.shape

    def body(x_ref, out_ref, send_sems, recv_sems, local_sem):
        start, forward, finish = _gather_plan(x_ref, out_ref, send_sems, recv_sems, local_sem)
        start()
        forward()
        finish()

    return pl.pallas_call(
        body, name=name, out_shape=jax.ShapeDtypeStruct((N_DEV * rows, cols), shard.dtype),
        in_specs=[pl.BlockSpec(memory_space=pl.ANY)], out_specs=pl.BlockSpec(memory_space=pl.ANY),
        scratch_shapes=_comm_scratch(),
    )(shard)


def adamw(parts, w, m, v, *, name, exchange=None):
    _, rows, cols = parts.shape
    tr = PACK_ROW_BLOCK
    steps = rows // tr
    c1 = 1.0 - ADAM_B1 ** ADAM_STEP
    c2 = 1.0 - ADAM_B2 ** ADAM_STEP

    def body(p_ref, w_ref, m_ref, v_ref, *rest):
        if exchange is None:
            update(p_ref, w_ref, m_ref, v_ref, *rest)
            return
        in_ref, g_ref, d_ref, nm_ref, nv_ref, out_ref, send_sems, recv_sems, local_sem = rest
        start, finish = _exchange_plan(in_ref, out_ref, send_sems, recv_sems, local_sem)
        pl.when(pl.program_id(0) == 0)(start)
        update(p_ref, w_ref, m_ref, v_ref, g_ref, d_ref, nm_ref, nv_ref)
        pl.when(pl.program_id(0) == steps - 1)(finish)

    def update(p_ref, w_ref, m_ref, v_ref, g_ref, d_ref, nm_ref, nv_ref):
        g = p_ref[0].astype(F32)
        for s in range(1, N_DEV):
            g = g + p_ref[s].astype(F32)
        nm = ADAM_B1 * m_ref[...] + (1.0 - ADAM_B1) * g
        nv = ADAM_B2 * v_ref[...] + (1.0 - ADAM_B2) * jnp.square(g)
        m_hat = nm / c1
        v_hat = nv / c2
        g_ref[...] = g
        d_ref[...] = -ADAM_LR * (m_hat / (jnp.sqrt(v_hat) + ADAM_EPS) + ADAM_WD * w_ref[...])
        nm_ref[...] = nm
        nv_ref[...] = nv

    row = pl.BlockSpec((tr, cols), lambda i: (i, 0))
    shape = jax.ShapeDtypeStruct((rows, cols), F32)
    in_specs = [pl.BlockSpec((N_DEV, tr, cols), lambda i: (0, i, 0)), row, row, row]
    if exchange is None:
        return pl.pallas_call(
            body, name=name, grid=(steps,), in_specs=in_specs,
            out_specs=[row, row, row, row], out_shape=[shape, shape, shape, shape],
            compiler_params=_params("parallel"),
        )(parts, w, m, v)
    whole = pl.BlockSpec(memory_space=pl.ANY)
    return pl.pallas_call(
        body, name=name, grid=(steps,), in_specs=in_specs + [whole],
        out_specs=[row, row, row, row, whole],
        out_shape=[shape, shape, shape, shape, jax.ShapeDtypeStruct(exchange.shape, exchange.dtype)],
        scratch_shapes=_comm_scratch(), compiler_params=_params("arbitrary"),
    )(parts, w, m, v, exchange)


def kernel(x, norm_gains, sb_w_qkv, sb_w_o, conv_w_in, conv_w, conv_w_out, gla_w_in, gla_w_gate_up, gla_b_gate, gla_head_norm, gla_w_o, ffn_w_up, ffn_w_down, loss_target, m_norm_gains, m_sb_w_qkv, m_sb_w_o, m_conv_w_in, m_conv_w, m_conv_w_out, m_gla_w_in, m_gla_w_gate_up, m_gla_b_gate, m_gla_head_norm, m_gla_w_o, m_ffn_w_up, m_ffn_w_down, v_norm_gains, v_sb_w_qkv, v_sb_w_o, v_conv_w_in, v_conv_w, v_conv_w_out, v_gla_w_in, v_gla_w_gate_up, v_gla_b_gate, v_gla_head_norm, v_gla_w_o, v_ffn_w_up, v_ffn_w_down):
    shards = dict(zip(WEIGHTS, (norm_gains, sb_w_qkv, sb_w_o, conv_w_in, conv_w, conv_w_out, gla_w_in,
                                gla_w_gate_up, gla_b_gate, gla_head_norm, gla_w_o, ffn_w_up, ffn_w_down)))
    moments_m = dict(zip(WEIGHTS, (m_norm_gains, m_sb_w_qkv, m_sb_w_o, m_conv_w_in, m_conv_w, m_conv_w_out,
                                   m_gla_w_in, m_gla_w_gate_up, m_gla_b_gate, m_gla_head_norm, m_gla_w_o,
                                   m_ffn_w_up, m_ffn_w_down)))
    moments_v = dict(zip(WEIGHTS, (v_norm_gains, v_sb_w_qkv, v_sb_w_o, v_conv_w_in, v_conv_w, v_conv_w_out,
                                   v_gla_w_in, v_gla_w_gate_up, v_gla_b_gate, v_gla_head_norm, v_gla_w_o,
                                   v_ffn_w_up, v_ffn_w_down)))
    shard_shapes = {n: a.shape for n, a in shards.items()}

    alone = [('norm_gains', 0, 1), ('sb_w_qkv', 0, 1)]
    rest = [(n, 1 if (n, 0, 1) in alone else 0, shard_shapes[n][0]) for n in WEIGHTS]
    groups = [alone, [piece for piece in rest if piece[1] < piece[2]]]

    def payload(group):
        segs, rows = _segments(group, shard_shapes, True)
        flat = {(n, lo, hi): (lax.bitcast_convert_type(shards[n][lo:hi], BF16) if n in F32_PAYLOAD
                              else shards[n][lo:hi].astype(BF16)).reshape(-1) for n, lo, hi in group}
        return segs, _pack(flat, segs, rows, BF16)

    whole = {n: [None] * shard_shapes[n][0] for n in WEIGHTS}

    def take_gathered(segs, gathered):
        gathered = gathered.reshape(N_DEV, -1, PACK_COLS)
        for n, lo, hi in segs:
            piece = _unpack(gathered, segs[(n, lo, hi)])
            if n in F32_PAYLOAD:
                piece = lax.bitcast_convert_type(piece.reshape(N_DEV, -1, 2), F32)
            blocks = piece.reshape((N_DEV, hi - lo) + shard_shapes[n][1:])
            if n in ('ffn_w_up', 'ffn_w_down'):
                for j in range(lo, hi):
                    whole[n][j] = GatheredWeight(blocks, j - lo, SHARD_AXIS[n] == 1)
                continue
            full = _unshard(blocks, SHARD_AXIS[n])
            for j in range(lo, hi):
                whole[n][j] = full[j - lo]

    segs0, payload0 = payload(groups[0])
    segs1, payload1 = payload(groups[1])
    take_gathered(segs0, all_gather(payload0, name="weights_all_gather"))

    gsegs = [_segments(group, shard_shapes, False) for group in groups]
    received = [None, None]

    def parts(g, grads):
        segs, rows = gsegs[g]
        flat = {(n, lo, hi): _shard_split(jnp.stack([grads[(n, j)] for j in range(lo, hi)]), SHARD_AXIS[n])
                .reshape(N_DEV, -1) for n, lo, hi in segs}
        return _pack(flat, segs, rows, BF16)

    def on_received(buf):
        received[1] = buf

    comm = types.SimpleNamespace(rest_payload=payload1, on_gathered=functools.partial(take_gathered, segs1),
                                 rest_parts=functools.partial(parts, 1), on_received=on_received)

    loss, grad_x, grads = local_step(x[0], whole, loss_target[0], comm)
    loss = lax.psum(loss[0, 0], ("x", "y", "c"))

    results = {n: [[] for _ in range(4)] for n in WEIGHTS}
    outs_of = {}
    for g in (1, 0):
        segs, rows = gsegs[g]

        def packed(source):
            return _pack({(n, lo, hi): source[n][lo:hi].reshape(-1) for n, lo, hi in segs}, segs, rows, F32)

        states = (packed(shards), packed(moments_m), packed(moments_v))
        if g == 1:
            *outs_of[g], received[0] = adamw(received[1], *states, name="adamw1", exchange=parts(0, grads))
        else:
            outs_of[g] = adamw(received[0], *states, name="adamw0")
    for g in range(2):
        segs, _ = gsegs[g]
        for n, lo, hi in segs:
            for o, buf in enumerate(outs_of[g]):
                results[n][o].append(_unpack(buf, segs[(n, lo, hi)]).reshape((hi - lo,) + shard_shapes[n][1:]))
    whole_out = [[r[0] if len(r) == 1 else jnp.concatenate(r, axis=0) for r in (results[n][o] for n in WEIGHTS)]
                 for o in range(4)]
    return (loss, grad_x[None], *whole_out[0], *whole_out[1], *whole_out[2], *whole_out[3])
```

```python
import functools
import math
import types

import jax
import jax.numpy as jnp
from jax import lax
from jax.experimental import pallas as pl
from jax.experimental.pallas import tpu as pltpu

F32 = jnp.float32
BF16 = jnp.bfloat16

N_DEV = 8
SB_HEADS = 16
GLA_HEADS = 4
GLA_CHUNK = 64
GLA_SUB = 16
GLA_GATE_RANK = 16
GLA_GATE_NORMALIZER = 16.0
CONV_WIDTH = 3
RMS_EPS = 1e-6
ADAM_LR = 0.001
ADAM_B1 = 0.9
ADAM_B2 = 0.999
ADAM_EPS = 1e-08
ADAM_WD = 0.01
ADAM_STEP = 10

LANES = 128
SB_BLOCK = 256
SB_FWD_Q_BLOCK = 1024
SB_BWD_Q_BLOCK = 1024
VMEM_LIMIT_BYTES = 56 * 1024 * 1024
MM_TM, MM_TN, MM_TK = 1024, 1024, 4096
ROW_BLOCK = 1024
PACK_COLS = 1024
PACK_ROW_ALIGN = 16
PACK_ROW_BLOCK = 256

WEIGHTS = ['norm_gains', 'sb_w_qkv', 'sb_w_o', 'conv_w_in', 'conv_w', 'conv_w_out', 'gla_w_in',
           'gla_w_gate_up', 'gla_b_gate', 'gla_head_norm', 'gla_w_o', 'ffn_w_up', 'ffn_w_down']
SHARD_AXIS = {'norm_gains': 2, 'sb_w_qkv': 2, 'sb_w_o': 1, 'conv_w_in': 2, 'conv_w': 2, 'conv_w_out': 1,
              'gla_w_in': 2, 'gla_w_gate_up': 2, 'gla_b_gate': 1, 'gla_head_norm': 2, 'gla_w_o': 1,
              'ffn_w_up': 2, 'ffn_w_down': 1}
F32_PAYLOAD = ('norm_gains', 'conv_w', 'gla_b_gate', 'gla_head_norm')

_NN = (((1,), (0,)), ((), ()))
_NT = (((1,), (1,)), ((), ()))
_TN = (((0,), (0,)), ((), ()))
_DIMS = {'nn': _NN, 'nt': _NT, 'tn': _TN}


def _params(*semantics):
    return pltpu.CompilerParams(dimension_semantics=semantics, vmem_limit_bytes=VMEM_LIMIT_BYTES)


def _dot(a, b, dims=_NN):
    return lax.dot_general(a.astype(BF16), b.astype(BF16), dims, preferred_element_type=F32)


def _split_hi_lo(x):
    hi = x.astype(BF16)
    lo = (x - hi.astype(F32)).astype(BF16)
    return hi, lo


def _dot_exact_rhs(x, ones_mat, dims=_NN):
    hi, lo = _split_hi_lo(x)
    return (lax.dot_general(hi, ones_mat, dims, preferred_element_type=F32)
            + lax.dot_general(lo, ones_mat, dims, preferred_element_type=F32))


def _dot_exact_lhs(ones_mat, x, dims=_NN):
    hi, lo = _split_hi_lo(x)
    return (lax.dot_general(ones_mat, hi, dims, preferred_element_type=F32)
            + lax.dot_general(ones_mat, lo, dims, preferred_element_type=F32))


def _log_sigmoid(z):
    return jnp.minimum(z, 0.0) - jnp.log(1.0 + jnp.exp(-jnp.abs(z)))


def _sigmoid(z):
    return 1.0 / (1.0 + jnp.exp(-z))


def matmul(a, b, *, mode, name, out_dtype=F32, epi=None, extra=None, tm=MM_TM, tn=MM_TN, tk=MM_TK):
    if mode == 'nn':
        (m, k), (k2, n) = a.shape, b.shape
    elif mode == 'nt':
        (m, k), (n, k2) = a.shape, b.shape
    else:
        (k, m), (k2, n) = a.shape, b.shape
    assert k == k2, (a.shape, b.shape, mode)
    tm, tn, tk = min(tm, m), min(tn, n), min(tk, k)
    assert m % tm == 0 and n % tn == 0 and k % tk == 0, (a.shape, b.shape, mode)
    nk = k // tk
    if mode == 'tn':
        a_spec = pl.BlockSpec((tk, tm), lambda i, j, kk: (kk, i))
    else:
        a_spec = pl.BlockSpec((tm, tk), lambda i, j, kk: (i, kk))
    if mode == 'nt':
        b_spec = pl.BlockSpec((tn, tk), lambda i, j, kk: (j, kk))
    else:
        b_spec = pl.BlockSpec((tk, tn), lambda i, j, kk: (kk, j))
    in_specs, operands = [a_spec, b_spec], [a, b]
    if epi == 'logsig16':
        in_specs.append(pl.BlockSpec((1, tn), lambda i, j, kk: (0, j)))
        operands.append(extra)
    elif epi in ('mul2sqrt', 'add'):
        in_specs.append(pl.BlockSpec((tm, tn), lambda i, j, kk: (i, j)))
        operands.append(extra)

    n_extra = len(operands) - 2

    def body(a_ref, b_ref, *rest):
        e_ref = rest[0] if n_extra else None
        o_ref = rest[n_extra]

        def finish(r):
            if epi == 'relu2':
                r = jnp.square(jnp.maximum(r, 0.0))
            elif epi == 'mul2sqrt':
                r = r * (2.0 * jnp.sqrt(e_ref[...].astype(F32)))
            elif epi == 'add':
                r = r + e_ref[...]
            elif epi == 'logsig16':
                r = _log_sigmoid(r + e_ref[...]) / GLA_GATE_NORMALIZER
            o_ref[...] = r.astype(o_ref.dtype)

        part = _dot(a_ref[...], b_ref[...], _DIMS[mode])
        if nk == 1:
            finish(part)
        else:
            acc_ref = rest[-1]
            kk = pl.program_id(2)

            @pl.when(kk == 0)
            def _():
                acc_ref[...] = part

            @pl.when(kk > 0)
            def _():
                acc_ref[...] += part

            @pl.when(kk == nk - 1)
            def _():
                finish(acc_ref[...])

    return pl.pallas_call(
        body, name=name, grid=(m // tm, n // tn, nk), in_specs=in_specs,
        out_specs=pl.BlockSpec((tm, tn), lambda i, j, kk: (i, j)),
        out_shape=jax.ShapeDtypeStruct((m, n), out_dtype),
        scratch_shapes=[pltpu.VMEM((tm, tn), F32)] if nk > 1 else [],
        compiler_params=_params("parallel", "parallel", "arbitrary"),
    )(*operands)


def rms_fwd(x, gain, *, name, res=None, out_dtype=F32, tm=ROW_BLOCK):
    t, d = x.shape
    tm = min(tm, t)
    row = pl.BlockSpec((tm, d), lambda i: (i, 0))
    in_specs, operands = [row, pl.BlockSpec((1, d), lambda i: (0, 0))], [x, gain]
    if res is not None:
        in_specs.append(row)
        operands.append(res)

    def body(x_ref, g_ref, *rest):
        xv = x_ref[...]
        r = lax.rsqrt(jnp.mean(xv * xv, axis=-1, keepdims=True) + RMS_EPS)
        y = xv * r * g_ref[...]
        if res is not None:
            y = rest[0][...] + y
        rest[-1][...] = y.astype(out_dtype)

    return pl.pallas_call(
        body, name=name, grid=(t // tm,), in_specs=in_specs, out_specs=row,
        out_shape=jax.ShapeDtypeStruct((t, d), out_dtype), compiler_params=_params("parallel"),
    )(*operands)


def rms_bwd(x, gain, dy, *, name, dres=None, out_dtype=F32, tm=ROW_BLOCK):
    t, d = x.shape
    tm = min(tm, t)
    row = pl.BlockSpec((tm, d), lambda i: (i, 0))
    vec = pl.BlockSpec((1, d), lambda i: (0, 0))
    in_specs, operands = [row, vec, row], [x, gain, dy]
    if dres is not None:
        in_specs.append(row)
        operands.append(dres)

    def body(x_ref, g_ref, dy_ref, *rest):
        dx_ref, dg_ref = rest[-2], rest[-1]

        @pl.when(pl.program_id(0) == 0)
        def _():
            dg_ref[...] = jnp.zeros_like(dg_ref)

        xv, dyv = x_ref[...], dy_ref[...]
        r = lax.rsqrt(jnp.mean(xv * xv, axis=-1, keepdims=True) + RMS_EPS)
        u = dyv * g_ref[...]
        dx = r * u - xv * (r * r * r * jnp.mean(u * xv, axis=-1, keepdims=True))
        if dres is not None:
            dx = rest[0][...] + dx
        dx_ref[...] = dx.astype(out_dtype)
        dg_ref[...] += jnp.sum(dyv * xv * r, axis=0, keepdims=True)

    return pl.pallas_call(
        body, name=name, grid=(t // tm,), in_specs=in_specs, out_specs=[row, vec],
        out_shape=[jax.ShapeDtypeStruct((t, d), out_dtype), jax.ShapeDtypeStruct((1, d), F32)],
        compiler_params=_params("arbitrary"),
    )(*operands)


def loss_head(y, target, *, name, tm=ROW_BLOCK):
    t, d = y.shape
    tm = min(tm, t)
    nt = t // tm
    row = pl.BlockSpec((tm, d), lambda i: (i, 0))

    def body(y_ref, t_ref, loss_ref, dy_ref, acc_ref):
        i = pl.program_id(0)

        @pl.when(i == 0)
        def _():
            acc_ref[...] = jnp.zeros_like(acc_ref)

        err = y_ref[...] - t_ref[...]
        dy_ref[...] = err * (1.0 / d)
        acc_ref[...] += jnp.sum(err * err, axis=0, keepdims=True)

        @pl.when(i == nt - 1)
        def _():
            loss_ref[...] = jnp.sum(acc_ref[...], axis=1, keepdims=True) * (0.5 / d)

    return pl.pallas_call(
        body, name=name, grid=(nt,), in_specs=[row, row],
        out_specs=[pl.BlockSpec((1, 1), lambda i: (0, 0)), row],
        out_shape=[jax.ShapeDtypeStruct((1, 1), F32), jax.ShapeDtypeStruct((t, d), F32)],
        scratch_shapes=[pltpu.VMEM((1, d), F32)], compiler_params=_params("arbitrary"),
    )(y, target)


def _sb_block_iota():
    rows = lax.broadcasted_iota(jnp.int32, (SB_BLOCK, SB_BLOCK), 0)
    cols = lax.broadcasted_iota(jnp.int32, (SB_BLOCK, SB_BLOCK), 1)
    return rows, cols


def _sb_logits(q_h, k_blk, mask):
    z = _dot(q_h, k_blk, _NT)
    ls = _log_sigmoid(z)
    lm = ls - z
    if mask is not None:
        lm = jnp.where(mask, lm, 0.0)
    return ls, lm, jnp.sum(lm, axis=1, keepdims=True)


def _sb_weights(ls, lm, mask, tri_strict, later):
    suffix = _dot_exact_rhs(lm, tri_strict)
    w = jnp.exp(ls + suffix + later)
    return w if mask is None else jnp.where(mask, w, 0.0)


def _rows_from(x, lo, half):
    return x if not lo else jnp.concatenate([x[lo:half], x[half + lo:]], axis=0)


def _add_from(x, lo, half, upd):
    if not lo:
        return x + upd
    n = half - lo
    return jnp.concatenate([x[:lo], x[lo:half] + upd[:n], x[half:half + lo], x[half + lo:] + upd[n:]], axis=0)


def _on_grid_step(p, i):
    return jnp.logical_and(pl.program_id(0) == p, pl.program_id(1) == i)


def sb_fwd(qkv, *, name, gather=None):
    t, d3 = qkv.shape
    d = d3 // 3
    head_dim = d // SB_HEADS
    qb, kb_rows = SB_FWD_Q_BLOCK, SB_BLOCK
    assert 2 * head_dim == LANES and t % qb == 0
    pairs = d // LANES
    per_q = qb // kb_rows
    nq = t // qb
    scale = head_dim ** -0.5

    def body(q_ref, k_ref, v_ref, *rest):
        if gather is None:
            compute(q_ref, k_ref, v_ref, *rest)
            return
        x_ref, o_ref, tot_ref, out_ref, acc_ref, later_ref, send_sems, recv_sems, local_sem = rest
        start, forward, finish = _gather_plan(x_ref, out_ref, send_sems, recv_sems, local_sem)
        pl.when(_on_grid_step(0, 0))(start)
        pl.when(_on_grid_step(7 * pairs // 8, 0))(forward)
        compute(q_ref, k_ref, v_ref, o_ref, tot_ref, acc_ref, later_ref)
        pl.when(_on_grid_step(pairs - 1, nq - 1))(finish)

    def compute(q_ref, k_ref, v_ref, o_ref, tot_ref, acc_ref, later_ref):
        qi = pl.program_id(1)
        lane = lax.broadcasted_iota(jnp.int32, (qb, LANES), 1)
        first = lane < head_dim
        q = q_ref[...] * scale
        q2 = jnp.concatenate([jnp.where(first, q, jnp.zeros_like(q)), jnp.where(first, jnp.zeros_like(q), q)], axis=0)
        rows, cols = _sb_block_iota()
        tri = jnp.where(rows > cols, 1.0, 0.0).astype(BF16)
        q_row = lax.broadcasted_iota(jnp.int32, (qb, kb_rows), 0)
        k_col = lax.broadcasted_iota(jnp.int32, (qb, kb_rows), 1)
        earlier = k_col < q_row
        acc_ref[...] = jnp.zeros_like(acc_ref)
        later_ref[...] = jnp.zeros_like(later_ref)

        def step(kb, lo=None):
            ks = pl.multiple_of(kb * SB_BLOCK, SB_BLOCK)
            k_blk = k_ref[pl.ds(ks, SB_BLOCK), :]
            v_blk = v_ref[pl.ds(ks, SB_BLOCK), :]
            if not lo:
                mask = None if lo is None else jnp.concatenate([earlier] * 2, axis=0)
                ls, lm, row = _sb_logits(q2, k_blk, mask)
                w = _sb_weights(ls, lm, mask, tri, later_ref[...])
                acc_ref[...] += _dot(w, v_blk)
                later_ref[...] += row
                return
            n = qb - lo
            mask = jnp.concatenate([earlier[:n]] * 2, axis=0)
            later = jnp.concatenate([later_ref[lo:qb], later_ref[qb + lo:]], axis=0)
            ls, lm, row = _sb_logits(_rows_from(q2, lo, qb), k_blk, mask)
            upd = _dot(_sb_weights(ls, lm, mask, tri, later), v_blk)
            for base in (0, qb):
                half = slice(0, n) if base == 0 else slice(n, 2 * n)
                acc_ref[base + lo:base + qb] += upd[half]
                later_ref[base + lo:base + qb] += row[half]

        for m in reversed(range(per_q)):
            step(per_q * qi + m, m * kb_rows)

        @pl.loop(0, per_q * qi)
        def _(i):
            step(per_q * qi - 1 - i)

        acc, total = acc_ref[...], later_ref[...]
        o_ref[...] = jnp.where(first, acc[:qb], acc[qb:]).astype(o_ref.dtype)
        tot_ref[...] = jnp.where(first, total[:qb], total[qb:])

    blk = lambda off: pl.BlockSpec((t, LANES), lambda p, i: (0, off + p))
    qblk = pl.BlockSpec((qb, LANES), lambda p, i: (i, p))
    in_specs, operands = [qblk, blk(pairs), blk(2 * pairs)], [qkv, qkv, qkv]
    out_specs = [qblk, qblk]
    out_shape = [jax.ShapeDtypeStruct((t, d), BF16), jax.ShapeDtypeStruct((t, d), F32)]
    sums = [pltpu.VMEM((2 * qb, LANES), F32), pltpu.VMEM((2 * qb, 1), F32)]
    if gather is None:
        return pl.pallas_call(
            body, name=name, grid=(pairs, nq), in_specs=in_specs, out_specs=out_specs, out_shape=out_shape,
            scratch_shapes=sums, compiler_params=_params("parallel", "arbitrary"),
        )(*operands)
    whole = pl.BlockSpec(memory_space=pl.ANY)
    return pl.pallas_call(
        body, name=name, grid=(pairs, nq), in_specs=in_specs + [whole], out_specs=out_specs + [whole],
        out_shape=out_shape + [jax.ShapeDtypeStruct((N_DEV * gather.shape[0], gather.shape[1]), gather.dtype)],
        scratch_shapes=sums + _comm_scratch(), compiler_params=_params("arbitrary", "arbitrary"),
    )(*operands, gather)


def sb_bwd(qkv, tot, do, *, name, exchange=None):
    t, d3 = qkv.shape
    d = d3 // 3
    head_dim = d // SB_HEADS
    pairs = d // LANES
    qb, kb_rows = SB_BWD_Q_BLOCK, SB_BLOCK
    per_q = qb // kb_rows
    nq = t // qb
    scale = head_dim ** -0.5

    def body(*refs):
        if exchange is None:
            compute(*refs)
            return
        q_ref, k_ref, v_ref, tot_ref, do_ref, in_ref, dq_ref, dk_ref, dv_ref, out_ref = refs[:10]
        dk_sum, dv_sum, send_sems, recv_sems, local_sem = refs[10:]
        start, finish = _exchange_plan(in_ref, out_ref, send_sems, recv_sems, local_sem)
        pl.when(_on_grid_step(0, 0))(start)
        compute(q_ref, k_ref, v_ref, tot_ref, do_ref, dq_ref, dk_ref, dv_ref, dk_sum, dv_sum)
        pl.when(_on_grid_step(pairs - 1, nq - 1))(finish)

    def compute(q_ref, k_ref, v_ref, tot_ref, do_ref, dq_ref, dk_ref, dv_ref, dk_sum, dv_sum):
        qi = pl.program_id(1)

        @pl.when(qi == 0)
        def _():
            dk_sum[...] = jnp.zeros_like(dk_sum)
            dv_sum[...] = jnp.zeros_like(dv_sum)

        lane = lax.broadcasted_iota(jnp.int32, (qb, LANES), 1)
        first = lane < head_dim
        q, dov, totv = q_ref[...] * scale, do_ref[...], tot_ref[...]
        second = jnp.logical_not(first)
        q2 = jnp.concatenate([jnp.where(s, q, jnp.zeros_like(q)) for s in (first, second)], axis=0)
        do2 = jnp.concatenate([jnp.where(s, dov, jnp.zeros_like(dov)) for s in (first, second)], axis=0)
        tot2 = jnp.concatenate([totv[:, 0:1], totv[:, head_dim:head_dim + 1]], axis=0)
        rows, cols = _sb_block_iota()
        tri_strict = jnp.where(rows > cols, 1.0, 0.0).astype(BF16)
        tri_before = jnp.where(rows < cols, 1.0, 0.0).astype(BF16)
        q_row = lax.broadcasted_iota(jnp.int32, (qb, kb_rows), 0)
        k_col = lax.broadcasted_iota(jnp.int32, (qb, kb_rows), 1)
        earlier = k_col < q_row

        def step(kb, carry, lo=None):
            ks = pl.multiple_of(kb * SB_BLOCK, SB_BLOCK)
            k_blk = k_ref[pl.ds(ks, SB_BLOCK), :]
            v_blk = v_ref[pl.ds(ks, SB_BLOCK), :]
            dq, seen, before = carry
            mask = None if lo is None else jnp.concatenate([earlier[:qb - lo]] * 2, axis=0)
            q_s, do_s = _rows_from(q2, lo, qb), _rows_from(do2, lo, qb)
            ls, lm, row = _sb_logits(q_s, k_blk, mask)
            seen = _add_from(seen, lo, qb, row)
            w = _sb_weights(ls, lm, mask, tri_strict, _rows_from(tot2 - seen, lo, qb))
            da = _dot(do_s, v_blk, _NT) * w
            g = _dot_exact_rhs(da, tri_before) + _rows_from(before, lo, qb)
            dz = da - jnp.exp(ls) * (da + g)
            if mask is not None:
                dz = jnp.where(mask, dz, 0.0)
            dk_sum[pl.ds(ks, SB_BLOCK), :] += _dot(dz, q_s, _TN)
            dv_sum[pl.ds(ks, SB_BLOCK), :] += _dot(w, do_s, _TN)
            return (_add_from(dq, lo, qb, _dot(dz, k_blk * scale)), seen,
                    _add_from(before, lo, qb, jnp.sum(da, axis=1, keepdims=True)))

        zero = jnp.zeros((2 * qb, LANES), F32)
        zcol = jnp.zeros((2 * qb, 1), F32)
        out = lax.fori_loop(0, per_q * qi, step, (zero, zcol, zcol))
        for m in range(per_q):
            out = step(per_q * qi + m, out, m * kb_rows)
        dq = out[0]
        dq_ref[...] = jnp.where(first, dq[:qb], dq[qb:]).astype(dq_ref.dtype)

        @pl.when(qi == nq - 1)
        def _():
            dk_ref[...] = dk_sum[...].astype(dk_ref.dtype)
            dv_ref[...] = dv_sum[...].astype(dv_ref.dtype)

    qblk = pl.BlockSpec((qb, LANES), lambda p, i: (i, p))
    col = lambda off: pl.BlockSpec((t, LANES), lambda p, i: (0, off + p))
    shape = jax.ShapeDtypeStruct((t, d), BF16)
    in_specs, operands = [qblk, col(pairs), col(2 * pairs), qblk, qblk], [qkv, qkv, qkv, tot, do]
    out_specs, out_shape = [qblk, col(0), col(0)], [shape, shape, shape]
    sums = [pltpu.VMEM((t, LANES), F32), pltpu.VMEM((t, LANES), F32)]
    if exchange is None:
        return pl.pallas_call(
            body, name=name, grid=(pairs, nq), in_specs=in_specs, out_specs=out_specs, out_shape=out_shape,
            scratch_shapes=sums, compiler_params=_params("parallel", "arbitrary"),
        )(*operands)
    whole = pl.BlockSpec(memory_space=pl.ANY)
    return pl.pallas_call(
        body, name=name, grid=(pairs, nq), in_specs=in_specs + [whole], out_specs=out_specs + [whole],
        out_shape=out_shape + [jax.ShapeDtypeStruct(exchange.shape, exchange.dtype)],
        scratch_shapes=sums + _comm_scratch(), compiler_params=_params("arbitrary", "arbitrary"),
    )(*operands, exchange)


def _shift_down(x, s):
    rows = lax.broadcasted_iota(jnp.int32, x.shape, 0)
    return jnp.where(rows >= s, pltpu.roll(x, s, 0), 0.0)


def _shift_up(x, s):
    t = x.shape[0]
    rows = lax.broadcasted_iota(jnp.int32, x.shape, 0)
    return jnp.where(rows < t - s, pltpu.roll(x, t - s, 0), 0.0)


def conv_fwd(bcu, w, *, name):
    t, d3 = bcu.shape
    d = d3 // 3
    nb = d // LANES
    col = lambda off: pl.BlockSpec((t, LANES), lambda j: (0, off + j))

    def body(b_ref, c_ref, u_ref, w_ref, y_ref):
        hh = c_ref[...] * u_ref[...]
        conv = w_ref[0:1, :] * _shift_down(hh, 2) + w_ref[1:2, :] * _shift_down(hh, 1) + w_ref[2:3, :] * hh
        y_ref[...] = (b_ref[...] * conv).astype(y_ref.dtype)

    return pl.pallas_call(
        body, name=name, grid=(nb,),
        in_specs=[col(0), col(nb), col(2 * nb), pl.BlockSpec((CONV_WIDTH, LANES), lambda j: (0, j))],
        out_specs=col(0), out_shape=jax.ShapeDtypeStruct((t, d), BF16), compiler_params=_params("parallel"),
    )(bcu, bcu, bcu, w)


def conv_bwd(bcu, w, dy, *, name):
    t, d3 = bcu.shape
    d = d3 // 3
    nb = d // LANES
    col = lambda off: pl.BlockSpec((t, LANES), lambda j: (0, off + j))
    wspec = pl.BlockSpec((CONV_WIDTH, LANES), lambda j: (0, j))

    def body(b_ref, c_ref, u_ref, w_ref, dy_ref, db_ref, dc_ref, du_ref, dw_ref):
        c, u, dyv = c_ref[...], u_ref[...], dy_ref[...]
        hh = c * u
        h2, h1 = _shift_down(hh, 2), _shift_down(hh, 1)
        w0, w1, w2 = w_ref[0:1, :], w_ref[1:2, :], w_ref[2:3, :]
        db_ref[...] = (dyv * (w0 * h2 + w1 * h1 + w2 * hh)).astype(db_ref.dtype)
        dconv = dyv * b_ref[...]
        dhh = w2 * dconv + w1 * _shift_up(dconv, 1) + w0 * _shift_up(dconv, 2)
        dc_ref[...] = (dhh * u).astype(dc_ref.dtype)
        du_ref[...] = (dhh * c).astype(du_ref.dtype)
        dw_ref[0:1, :] = jnp.sum(dconv * h2, axis=0, keepdims=True)
        dw_ref[1:2, :] = jnp.sum(dconv * h1, axis=0, keepdims=True)
        dw_ref[2:3, :] = jnp.sum(dconv * hh, axis=0, keepdims=True)

    shape = jax.ShapeDtypeStruct((t, d), BF16)
    return pl.pallas_call(
        body, name=name, grid=(nb,),
        in_specs=[col(0), col(nb), col(2 * nb), wspec, col(0)],
        out_specs=[col(0), col(0), col(0), wspec],
        out_shape=[shape, shape, shape, jax.ShapeDtypeStruct((CONV_WIDTH, d), F32)],
        compiler_params=_params("parallel"),
    )(bcu, bcu, bcu, w, dy)


def _gla_chunk(q, k, lg, scale):
    c = GLA_CHUNK
    rows = lax.broadcasted_iota(jnp.int32, (c, c), 0)
    cols = lax.broadcasted_iota(jnp.int32, (c, c), 1)
    causal = rows >= cols
    tril = jnp.where(causal, 1.0, 0.0).astype(BF16)
    q = q * scale
    cum = _dot_exact_lhs(tril, lg)
    last = cum[c - 1:c, :]
    eq = jnp.exp(cum)
    el = jnp.exp(last - cum)
    return causal, tril, q, k, cum, lg, last, eq, el


def _gla_sub_blocks(q, k, cum, lg):
    key_row = lax.broadcasted_iota(jnp.int32, (GLA_CHUNK, 1), 0)
    out = []
    for lo in range(0, GLA_CHUNK, GLA_SUB):
        hi = lo + GLA_SUB
        ref = cum[lo:lo + 1, :] - lg[lo:lo + 1, :]
        eq = jnp.exp(cum[lo:hi] - ref)
        ek = jnp.where(key_row < hi, jnp.exp(ref - cum), 0.0)
        out.append((slice(lo, hi), eq, ek, q[lo:hi] * eq, k * ek))
    return out


def _gla_scores(subs, causal):
    return jnp.where(causal, jnp.concatenate([_dot(qt, kt, _NT) for _, _, _, qt, kt in subs], axis=0), 0.0)


def gla_fwd(proj, lg, *, name):
    t, d3 = proj.shape
    d = d3 // 3
    dk, dv = d // 2 // GLA_HEADS, d // GLA_HEADS
    assert dk == LANES and dv == 2 * LANES
    c = GLA_CHUNK
    nc = t // c
    scale = dk ** -0.5
    nh = GLA_HEADS

    def body(q_ref, k_ref, v_ref, lg_ref, o_ref, st_out_ref, st_ref):
        @pl.when(pl.program_id(0) == 0)
        def _():
            st_ref[...] = jnp.zeros_like(st_ref)

        for h in range(nh):
            kcols, vcols = slice(h * dk, (h + 1) * dk), slice(h * dv, (h + 1) * dv)
            causal, _, q, k, cum, lg, last, eq, el = _gla_chunk(q_ref[:, kcols], k_ref[:, kcols], lg_ref[:, kcols], scale)
            v = v_ref[:, vcols]
            st = st_ref[h]
            st_out_ref[h] = st
            scores = _gla_scores(_gla_sub_blocks(q, k, cum, lg), causal)
            o_ref[:, vcols] = _dot(q * eq, st, _NT) + _dot(scores, v)
            st_ref[h] = st * jnp.exp(last) + _dot(v, k * el, _TN)

    half, full = pl.BlockSpec((c, d // 2), lambda i: (i, 0)), pl.BlockSpec((c, d), lambda i: (i, 0))
    return pl.pallas_call(
        body, name=name, grid=(nc,),
        in_specs=[half, pl.BlockSpec((c, d // 2), lambda i: (i, 1)), pl.BlockSpec((c, d), lambda i: (i, 1)), half],
        out_specs=[full, pl.BlockSpec((nh, None, dv, dk), lambda i: (0, i, 0, 0))],
        out_shape=[jax.ShapeDtypeStruct((t, d), F32), jax.ShapeDtypeStruct((nh, nc, dv, dk), F32)],
        scratch_shapes=[pltpu.VMEM((nh, dv, dk), F32)], compiler_params=_params("arbitrary"),
    )(proj, proj, proj, lg)


def gla_bwd(proj, lg, states, do, *, name):
    t, d3 = proj.shape
    d = d3 // 3
    dk, dv = d // 2 // GLA_HEADS, d // GLA_HEADS
    c = GLA_CHUNK
    nc = t // c
    scale = dk ** -0.5
    nh = GLA_HEADS

    def body(q_ref, k_ref, v_ref, lg_ref, st_ref, do_ref, dq_ref, dk_ref, dv_ref, dzg_ref, dbg_ref, dst_ref):
        @pl.when(pl.program_id(0) == 0)
        def _():
            dst_ref[...] = jnp.zeros_like(dst_ref)
            dbg_ref[...] = jnp.zeros_like(dbg_ref)

        for h in range(nh):
            kcols, vcols = slice(h * dk, (h + 1) * dk), slice(h * dv, (h + 1) * dv)
            causal, tril, q, k, cum, lg, last, eq, el = _gla_chunk(q_ref[:, kcols], k_ref[:, kcols], lg_ref[:, kcols], scale)
            v, st, dov, dst = v_ref[:, vcols], st_ref[h], do_ref[:, vcols], dst_ref[h]
            subs = _gla_sub_blocks(q, k, cum, lg)
            qt, kh = q * eq, k * el
            scores = _gla_scores(subs, causal)
            dscores = jnp.where(causal, _dot(dov, v, _NT), 0.0)
            dq_parts = []
            dkh = _dot(v, dst)
            dk_h = dkh * el
            for rows, eq_sub, ek_sub, qt_sub, kt_sub in subs:
                dq_parts.append(_dot(dscores[rows], kt_sub) * eq_sub)
                dk_h = dk_h + _dot(dscores[rows], qt_sub, _TN) * ek_sub
            dq = _dot(dov, st) * eq + jnp.concatenate(dq_parts, axis=0)
            dv_ref[:, vcols] = (_dot(scores, dov, _TN) + _dot(kh, dst, _NT)).astype(dv_ref.dtype)
            dq_ref[:, kcols] = (dq * scale).astype(dq_ref.dtype)
            dk_ref[:, kcols] = dk_h.astype(dk_ref.dtype)
            e_last = jnp.exp(last)
            dlast = jnp.sum(kh * dkh, axis=0, keepdims=True) + e_last * jnp.sum(dst * st, axis=0, keepdims=True)
            dcum = q * dq - k * dk_h
            dlg = _dot_exact_lhs(tril, dcum, _TN) + dlast
            dzg = dlg * (1.0 - jnp.exp(lg * GLA_GATE_NORMALIZER)) / GLA_GATE_NORMALIZER
            dzg_ref[:, kcols] = dzg.astype(dzg_ref.dtype)
            dbg_ref[:, kcols] += jnp.sum(dzg, axis=0, keepdims=True)
            dst_ref[h] = dst * e_last + _dot(dov, qt, _TN)

    rev = lambda i: nc - 1 - i
    half, full = pl.BlockSpec((c, d // 2), lambda i: (rev(i), 0)), pl.BlockSpec((c, d), lambda i: (rev(i), 0))
    half_shape = jax.ShapeDtypeStruct((t, d // 2), BF16)
    return pl.pallas_call(
        body, name=name, grid=(nc,),
        in_specs=[half, pl.BlockSpec((c, d // 2), lambda i: (rev(i), 1)), pl.BlockSpec((c, d), lambda i: (rev(i), 1)), half,
                  pl.BlockSpec((nh, None, dv, dk), lambda i: (0, rev(i), 0, 0)), full],
        out_specs=[half, half, full, half, pl.BlockSpec((1, d // 2), lambda i: (0, 0))],
        out_shape=[half_shape, half_shape, jax.ShapeDtypeStruct((t, d), BF16), half_shape,
                   jax.ShapeDtypeStruct((1, d // 2), F32)],
        scratch_shapes=[pltpu.VMEM((nh, dv, dk), F32)], compiler_params=_params("arbitrary"),
    )(proj, proj, proj, lg, states, do)


def gla_post_fwd(o, proj, head_norm, *, name, tm=ROW_BLOCK):
    t, d = o.shape
    dv = d // GLA_HEADS
    tm = min(tm, t)

    def body(o_ref, g_ref, hn_ref, y_ref):
        for h in range(GLA_HEADS):
            sl = slice(h * dv, (h + 1) * dv)
            ov, gv = o_ref[:, sl], g_ref[:, sl]
            r = lax.rsqrt(jnp.mean(ov * ov, axis=-1, keepdims=True) + RMS_EPS)
            y_ref[:, sl] = ((ov * r * hn_ref[:, sl]) * (gv * _sigmoid(gv))).astype(y_ref.dtype)

    row = pl.BlockSpec((tm, d), lambda i: (i, 0))
    return pl.pallas_call(
        body, name=name, grid=(t // tm,),
        in_specs=[row, pl.BlockSpec((tm, d), lambda i: (i, 2)), pl.BlockSpec((1, d), lambda i: (0, 0))],
        out_specs=row, out_shape=jax.ShapeDtypeStruct((t, d), BF16), compiler_params=_params("parallel"),
    )(o, proj, head_norm)


def gla_post_bwd(o, proj, head_norm, dy, *, name, tm=ROW_BLOCK):
    t, d = o.shape
    dv = d // GLA_HEADS
    tm = min(tm, t)

    def body(o_ref, g_ref, hn_ref, dy_ref, do_ref, dg_ref, dhn_ref):
        @pl.when(pl.program_id(0) == 0)
        def _():
            dhn_ref[...] = jnp.zeros_like(dhn_ref)

        for h in range(GLA_HEADS):
            sl = slice(h * dv, (h + 1) * dv)
            ov, gv, dyv, hn = o_ref[:, sl], g_ref[:, sl], dy_ref[:, sl], hn_ref[:, sl]
            r = lax.rsqrt(jnp.mean(ov * ov, axis=-1, keepdims=True) + RMS_EPS)
            sg = _sigmoid(gv)
            silu = gv * sg
            on = ov * r * hn
            dg_ref[:, sl] = (dyv * on * (sg * (1.0 + gv * (1.0 - sg)))).astype(dg_ref.dtype)
            don = dyv * silu
            u = don * hn
            do_ref[:, sl] = (r * u - ov * (r * r * r * jnp.mean(u * ov, axis=-1, keepdims=True))).astype(do_ref.dtype)
            dhn_ref[:, sl] += jnp.sum(don * ov * r, axis=0, keepdims=True)

    row = pl.BlockSpec((tm, d), lambda i: (i, 0))
    vec = pl.BlockSpec((1, d), lambda i: (0, 0))
    shape = jax.ShapeDtypeStruct((t, d), BF16)
    return pl.pallas_call(
        body, name=name, grid=(t // tm,),
        in_specs=[row, pl.BlockSpec((tm, d), lambda i: (i, 2)), vec, row],
        out_specs=[row, row, vec], out_shape=[shape, shape, jax.ShapeDtypeStruct((1, d), F32)],
        compiler_params=_params("arbitrary"),
    )(o, proj, head_norm, dy)


def _ffn_fwd(h, gains, w_up, w_down, tag):
    xn = rms_fwd(h, gains[2], name=f"{tag}_ffn_norm", out_dtype=BF16)
    act = matmul(xn, w_up, mode='nn', epi='relu2', out_dtype=BF16, tn=2 * MM_TN, name=f"{tag}_ffn_up")
    f = matmul(act, w_down, mode='nn', name=f"{tag}_ffn_down")
    h_out = rms_fwd(f, gains[3], res=h, name=f"{tag}_ffn_out")
    return h_out, (h, xn, act, f)


def _ffn_bwd(dh, saved, gains, w_up, w_down, tag):
    h, xn, act, f = saved
    df, dg3 = rms_bwd(f, gains[3], dh, out_dtype=BF16, name=f"{tag}_ffn_out_bwd")
    du = matmul(df, w_down, mode='nt', epi='mul2sqrt', extra=act, out_dtype=BF16, tn=2 * MM_TN, name=f"{tag}_ffn_da")
    dw_down = matmul(act, df, mode='tn', out_dtype=BF16, name=f"{tag}_ffn_dwdown")
    dw_up = matmul(xn, du, mode='tn', out_dtype=BF16, name=f"{tag}_ffn_dwup")
    dxn = matmul(du, w_up, mode='nt', name=f"{tag}_ffn_dxn")
    dh_in, dg2 = rms_bwd(h, gains[2], dxn, dres=dh, name=f"{tag}_ffn_norm_bwd")
    return dh_in, dg2, dg3, dw_up, dw_down


def _sb_layer_fwd(xn, w, j, tag, comm=None):
    qkv = matmul(xn, w['sb_w_qkv'][j], mode='nn', out_dtype=BF16, name=f"{tag}_qkv")
    if comm is None:
        o, tot = sb_fwd(qkv, name=f"{tag}_sb")
    else:
        o, tot, gathered = sb_fwd(qkv, name=f"{tag}_sb", gather=comm.rest_payload)
        comm.on_gathered(gathered)
    m = matmul(o, w['sb_w_o'][j], mode='nn', name=f"{tag}_wo")
    return m, (qkv, o, tot)


def _sb_layer_bwd(dm, xn, saved, w, j, tag, comm=None, grads=None):
    qkv, o, tot = saved
    do = matmul(dm, w['sb_w_o'][j], mode='nt', out_dtype=BF16, name=f"{tag}_do")
    dw_o = matmul(o, dm, mode='tn', out_dtype=BF16, name=f"{tag}_dwo")
    if comm is None:
        dq, dk, dv = sb_bwd(qkv, tot, do, name=f"{tag}_sb_bwd")
    else:
        parts = comm.rest_parts({**grads, ('sb_w_o', j): dw_o})
        dq, dk, dv, received = sb_bwd(qkv, tot, do, name=f"{tag}_sb_bwd", exchange=parts)
        comm.on_received(received)
    dqkv = jnp.concatenate([dq, dk, dv], axis=1)
    dw_qkv = matmul(xn, dqkv, mode='tn', out_dtype=BF16, name=f"{tag}_dwqkv")
    dxn = matmul(dqkv, w['sb_w_qkv'][j], mode='nt', name=f"{tag}_dxn")
    return dxn, {('sb_w_qkv', j): dw_qkv, ('sb_w_o', j): dw_o}


def _conv_layer_fwd(xn, w, j, tag):
    bcu = matmul(xn, w['conv_w_in'][j], mode='nn', name=f"{tag}_in")
    y = conv_fwd(bcu, w['conv_w'][j], name=f"{tag}_conv")
    m = matmul(y, w['conv_w_out'][j], mode='nn', name=f"{tag}_out")
    return m, (bcu, y)


def _conv_layer_bwd(dm, xn, saved, w, j, tag):
    bcu, y = saved
    dy = matmul(dm, w['conv_w_out'][j], mode='nt', name=f"{tag}_dy")
    dw_out = matmul(y, dm, mode='tn', out_dtype=BF16, name=f"{tag}_dwout")
    db, dc, du, dw_conv = conv_bwd(bcu, w['conv_w'][j], dy, name=f"{tag}_conv_bwd")
    dbcu = jnp.concatenate([db, dc, du], axis=1)
    dw_in = matmul(xn, dbcu, mode='tn', out_dtype=BF16, name=f"{tag}_dwin")
    dxn = matmul(dbcu, w['conv_w_in'][j], mode='nt', name=f"{tag}_dxn")
    return dxn, {('conv_w_in', j): dw_in, ('conv_w', j): dw_conv, ('conv_w_out', j): dw_out}


def _gla_split(w_in, w_gate_up):
    d = w_in.shape[0]
    w_main = w_in[:, :3 * d]
    w_a = jnp.pad(w_in[:, 3 * d:], ((0, 0), (0, LANES - GLA_GATE_RANK)))
    w_gu = jnp.pad(w_gate_up, ((0, LANES - GLA_GATE_RANK), (0, 0)))
    return w_main, w_a, w_gu


def _gla_layer_fwd(xn, w, j, tag):
    w_main, w_a, w_gu = _gla_split(w['gla_w_in'][j], w['gla_w_gate_up'][j])
    proj = matmul(xn, w_main, mode='nn', name=f"{tag}_in")
    a_low = matmul(xn, w_a, mode='nn', out_dtype=BF16, name=f"{tag}_alow")
    lg = matmul(a_low, w_gu, mode='nn', epi='logsig16', extra=w['gla_b_gate'][j][None, :], name=f"{tag}_gate")
    o, states = gla_fwd(proj, lg, name=f"{tag}_gla")
    hn = w['gla_head_norm'][j].reshape(1, -1)
    y = gla_post_fwd(o, proj, hn, name=f"{tag}_post")
    m = matmul(y, w['gla_w_o'][j], mode='nn', name=f"{tag}_wo")
    return m, (proj, a_low, lg, o, states, y)


def _gla_layer_bwd(dm, xn, saved, w, j, tag):
    proj, a_low, lg, o, states, y = saved
    w_main, w_a, w_gu = _gla_split(w['gla_w_in'][j], w['gla_w_gate_up'][j])
    hn = w['gla_head_norm'][j].reshape(1, -1)
    dy = matmul(dm, w['gla_w_o'][j], mode='nt', name=f"{tag}_dy")
    dw_o = matmul(y, dm, mode='tn', out_dtype=BF16, name=f"{tag}_dwo")
    do, dg, dhn = gla_post_bwd(o, proj, hn, dy, name=f"{tag}_post_bwd")
    dq, dk, dv, dzg, dbg = gla_bwd(proj, lg, states, do, name=f"{tag}_gla_bwd")
    da_low = matmul(dzg, w_gu, mode='nt', out_dtype=BF16, name=f"{tag}_dalow")
    dw_gu = matmul(a_low, dzg, mode='tn', out_dtype=BF16, name=f"{tag}_dwgu")[:GLA_GATE_RANK]
    dproj = jnp.concatenate([dq, dk, dv, dg], axis=1)
    dw_main = matmul(xn, dproj, mode='tn', out_dtype=BF16, name=f"{tag}_dwin")
    dw_a = matmul(xn, da_low, mode='tn', out_dtype=BF16, name=f"{tag}_dwa")[:, :GLA_GATE_RANK]
    dxn_a = matmul(da_low, w_a, mode='nt', name=f"{tag}_dxn_a")
    dxn = matmul(dproj, w_main, mode='nt', epi='add', extra=dxn_a, name=f"{tag}_dxn")
    grads = {('gla_w_in', j): jnp.concatenate([dw_main, dw_a], axis=1), ('gla_w_gate_up', j): dw_gu,
             ('gla_b_gate', j): dbg[0], ('gla_head_norm', j): dhn.reshape(w['gla_head_norm'][j].shape),
             ('gla_w_o', j): dw_o}
    return dxn, grads


_MIXERS = ((_sb_layer_fwd, _sb_layer_bwd), (_conv_layer_fwd, _conv_layer_bwd), (_gla_layer_fwd, _gla_layer_bwd))


def local_step(x, w, target, comm=None):
    depth = len(w['norm_gains'])
    h = x
    tape = []
    for i in range(depth):
        kind, j = i % 3, i // 3
        tag = f"l{i}"
        extra = {'comm': comm} if (comm is not None and i == 0) else {}
        gains = [w['norm_gains'][i][s][None, :] for s in range(4)]
        xn = rms_fwd(h, gains[0], name=f"{tag}_mix_norm", out_dtype=BF16)
        m, saved = _MIXERS[kind][0](xn, w, j, tag, **extra)
        h_mid = rms_fwd(m, gains[1], res=h, name=f"{tag}_mix_out")
        h_out, ffn_saved = _ffn_fwd(h_mid, gains, w['ffn_w_up'][i], w['ffn_w_down'][i], tag)
        tape.append((h, xn, m, saved, ffn_saved, gains))
        h = h_out
    loss, dh = loss_head(h, target, name="loss_head")

    grads = {}
    for i in reversed(range(depth)):
        kind, j = i % 3, i // 3
        tag = f"l{i}"
        h_in, xn, m, saved, ffn_saved, gains = tape[i]
        dg = [None] * 4
        dh, dg[2], dg[3], grads[('ffn_w_up', i)], grads[('ffn_w_down', i)] = _ffn_bwd(
            dh, ffn_saved, gains, w['ffn_w_up'][i], w['ffn_w_down'][i], tag)
        dm, dg[1] = rms_bwd(m, gains[1], dh, out_dtype=BF16, name=f"{tag}_mix_out_bwd")
        extra = {'comm': comm, 'grads': grads} if (comm is not None and i == 0) else {}
        dxn, g = _MIXERS[kind][1](dm, xn, saved, w, j, tag, **extra)
        grads.update(g)
        dh, dg[0] = rms_bwd(h_in, gains[0], dxn, dres=dh, name=f"{tag}_mix_norm_bwd")
        grads[('norm_gains', i)] = jnp.concatenate(dg, axis=0)
    return loss, dh, grads


def _segments(keys, shard_shapes, f32_as_pairs):
    segs, row = {}, 0
    for name, lo, hi in keys:
        n = (hi - lo) * math.prod(shard_shapes[name][1:]) * (2 if f32_as_pairs and name in F32_PAYLOAD else 1)
        nrows = -(-n // (PACK_COLS * PACK_ROW_ALIGN)) * PACK_ROW_ALIGN
        segs[(name, lo, hi)] = (row, nrows, n)
        row += nrows
    return segs, -(-row // PACK_ROW_BLOCK) * PACK_ROW_BLOCK


def _pack(parts, segs, total_rows, dtype):
    pieces, row = [], 0
    for key in segs:
        _, nrows, n = segs[key]
        p = parts[key].astype(dtype)
        lead = p.shape[:-1]
        if nrows * PACK_COLS > n:
            p = jnp.pad(p, [(0, 0)] * len(lead) + [(0, nrows * PACK_COLS - n)])
        pieces.append(p.reshape(lead + (nrows, PACK_COLS)))
        row += nrows
    if total_rows > row:
        pieces.append(jnp.zeros(lead + (total_rows - row, PACK_COLS), dtype))
    return jnp.concatenate(pieces, axis=-2)


def _unpack(buf, seg):
    first, nrows, n = seg
    piece = buf[..., first:first + nrows, :]
    return piece.reshape(piece.shape[:-2] + (nrows * PACK_COLS,))[..., :n]


def _unshard(gathered, axis):
    moved = jnp.moveaxis(gathered, 0, axis)
    shape = moved.shape
    return moved.reshape(shape[:axis] + (shape[axis] * shape[axis + 1],) + shape[axis + 2:])


def _shard_split(full, axis):
    shape = full.shape
    cut = full.reshape(shape[:axis] + (N_DEV, shape[axis] // N_DEV) + shape[axis + 1:])
    return jnp.moveaxis(cut, axis, 0)


def _mesh_position():
    return lax.axis_index("x"), lax.axis_index("y"), lax.axis_index("c")


def _comm_scratch():
    return [pltpu.SemaphoreType.DMA((N_DEV - 1,)), pltpu.SemaphoreType.DMA((N_DEV - 1,)), pltpu.SemaphoreType.DMA]


def _gather_plan(x_ref, out_ref, send_sems, recv_sems, local_sem):
    rows = x_ref.shape[0]
    x, y, c = _mesh_position()
    me, sibling = (x, y, c), (x, y, 1 - c)
    chips = [(1 - x, y), (x, 1 - y), (1 - x, 1 - y)]

    def block(px, py, pc):
        return out_ref.at[pl.ds((4 * px + 2 * py + pc) * rows, rows), :]

    def copy(k, blk, to, src=None):
        return pltpu.make_async_remote_copy(
            src_ref=block(*blk) if src is None else src, dst_ref=block(*blk),
            send_sem=send_sems.at[k], recv_sem=recv_sems.at[k],
            device_id=to, device_id_type=pl.DeviceIdType.MESH)

    mine = pltpu.make_async_copy(x_ref, block(*me), local_sem)
    first = [copy(0, me, sibling, src=x_ref)]
    first += [copy(1 + j, me, (*chip, c), src=x_ref) for j, chip in enumerate(chips)]
    passed = [copy(4 + j, (*chip, c), sibling) for j, chip in enumerate(chips)]

    def start():
        mine.start()
        for cp in first:
            cp.start()

    def forward():
        for j, chip in enumerate(chips):
            copy(1 + j, (*chip, c), me).wait_recv()
            passed[j].start()

    def finish():
        copy(0, sibling, me).wait_recv()
        for j, chip in enumerate(chips):
            copy(4 + j, (*chip, 1 - c), me).wait_recv()
        for cp in first + passed:
            cp.wait_send()
        mine.wait()

    return start, forward, finish


def _exchange_plan(in_ref, out_ref, send_sems, recv_sems, local_sem):
    x, y, c = _mesh_position()
    my_id = 4 * x + 2 * y + c
    mine = pltpu.make_async_copy(in_ref.at[my_id], out_ref.at[my_id], local_sem)

    def copy(k, receive):
        px = 1 - x if k & 4 else x
        py = 1 - y if k & 2 else y
        pc = 1 - c if k & 1 else c
        peer_id = 4 * px + 2 * py + pc
        return pltpu.make_async_remote_copy(
            src_ref=in_ref.at[peer_id], dst_ref=out_ref.at[peer_id if receive else my_id],
            send_sem=send_sems.at[k - 1], recv_sem=recv_sems.at[k - 1],
            device_id=(px, py, pc), device_id_type=pl.DeviceIdType.MESH)

    def start():
        mine.start()
        for k in range(1, N_DEV):
            copy(k, False).start()

    def finish():
        for k in range(1, N_DEV):
            copy(k, True).wait_recv()
        for k in range(1, N_DEV):
            copy(k, False).wait_send()
        mine.wait()

    return start, finish


def all_gather(shard, *, name):
    rows, cols = shard.shape

    def body(x_ref, out_ref, send_sems, recv_sems, local_sem):
        start, forward, finish = _gather_plan(x_ref, out_ref, send_sems, recv_sems, local_sem)
        start()
        forward()
        finish()

    return pl.pallas_call(
        body, name=name, out_shape=jax.ShapeDtypeStruct((N_DEV * rows, cols), shard.dtype),
        in_specs=[pl.BlockSpec(memory_space=pl.ANY)], out_specs=pl.BlockSpec(memory_space=pl.ANY),
        scratch_shapes=_comm_scratch(),
    )(shard)


def adamw(parts, w, m, v, *, name, exchange=None):
    _, rows, cols = parts.shape
    tr = PACK_ROW_BLOCK
    steps = rows // tr
    c1 = 1.0 - ADAM_B1 ** ADAM_STEP
    c2 = 1.0 - ADAM_B2 ** ADAM_STEP

    def body(p_ref, w_ref, m_ref, v_ref, *rest):
        if exchange is None:
            update(p_ref, w_ref, m_ref, v_ref, *rest)
            return
        in_ref, g_ref, d_ref, nm_ref, nv_ref, out_ref, send_sems, recv_sems, local_sem = rest
        start, finish = _exchange_plan(in_ref, out_ref, send_sems, recv_sems, local_sem)
        pl.when(pl.program_id(0) == 0)(start)
        update(p_ref, w_ref, m_ref, v_ref, g_ref, d_ref, nm_ref, nv_ref)
        pl.when(pl.program_id(0) == steps - 1)(finish)

    def update(p_ref, w_ref, m_ref, v_ref, g_ref, d_ref, nm_ref, nv_ref):
        g = p_ref[0].astype(F32)
        for s in range(1, N_DEV):
            g = g + p_ref[s].astype(F32)
        nm = ADAM_B1 * m_ref[...] + (1.0 - ADAM_B1) * g
        nv = ADAM_B2 * v_ref[...] + (1.0 - ADAM_B2) * jnp.square(g)
        m_hat = nm / c1
        v_hat = nv / c2
        g_ref[...] = g
        d_ref[...] = -ADAM_LR * (m_hat / (jnp.sqrt(v_hat) + ADAM_EPS) + ADAM_WD * w_ref[...])
        nm_ref[...] = nm
        nv_ref[...] = nv

    row = pl.BlockSpec((tr, cols), lambda i: (i, 0))
    shape = jax.ShapeDtypeStruct((rows, cols), F32)
    in_specs = [pl.BlockSpec((N_DEV, tr, cols), lambda i: (0, i, 0)), row, row, row]
    if exchange is None:
        return pl.pallas_call(
            body, name=name, grid=(steps,), in_specs=in_specs,
            out_specs=[row, row, row, row], out_shape=[shape, shape, shape, shape],
            compiler_params=_params("parallel"),
        )(parts, w, m, v)
    whole = pl.BlockSpec(memory_space=pl.ANY)
    return pl.pallas_call(
        body, name=name, grid=(steps,), in_specs=in_specs + [whole],
        out_specs=[row, row, row, row, whole],
        out_shape=[shape, shape, shape, shape, jax.ShapeDtypeStruct(exchange.shape, exchange.dtype)],
        scratch_shapes=_comm_scratch(), compiler_params=_params("arbitrary"),
    )(parts, w, m, v, exchange)


def kernel(x, norm_gains, sb_w_qkv, sb_w_o, conv_w_in, conv_w, conv_w_out, gla_w_in, gla_w_gate_up, gla_b_gate, gla_head_norm, gla_w_o, ffn_w_up, ffn_w_down, loss_target, m_norm_gains, m_sb_w_qkv, m_sb_w_o, m_conv_w_in, m_conv_w, m_conv_w_out, m_gla_w_in, m_gla_w_gate_up, m_gla_b_gate, m_gla_head_norm, m_gla_w_o, m_ffn_w_up, m_ffn_w_down, v_norm_gains, v_sb_w_qkv, v_sb_w_o, v_conv_w_in, v_conv_w, v_conv_w_out, v_gla_w_in, v_gla_w_gate_up, v_gla_b_gate, v_gla_head_norm, v_gla_w_o, v_ffn_w_up, v_ffn_w_down):
    shards = dict(zip(WEIGHTS, (norm_gains, sb_w_qkv, sb_w_o, conv_w_in, conv_w, conv_w_out, gla_w_in,
                                gla_w_gate_up, gla_b_gate, gla_head_norm, gla_w_o, ffn_w_up, ffn_w_down)))
    moments_m = dict(zip(WEIGHTS, (m_norm_gains, m_sb_w_qkv, m_sb_w_o, m_conv_w_in, m_conv_w, m_conv_w_out,
                                   m_gla_w_in, m_gla_w_gate_up, m_gla_b_gate, m_gla_head_norm, m_gla_w_o,
                                   m_ffn_w_up, m_ffn_w_down)))
    moments_v = dict(zip(WEIGHTS, (v_norm_gains, v_sb_w_qkv, v_sb_w_o, v_conv_w_in, v_conv_w, v_conv_w_out,
                                   v_gla_w_in, v_gla_w_gate_up, v_gla_b_gate, v_gla_head_norm, v_gla_w_o,
                                   v_ffn_w_up, v_ffn_w_down)))
    shard_shapes = {n: a.shape for n, a in shards.items()}

    alone = [('norm_gains', 0, 1), ('sb_w_qkv', 0, 1)]
    rest = [(n, 1 if (n, 0, 1) in alone else 0, shard_shapes[n][0]) for n in WEIGHTS]
    groups = [alone, [piece for piece in rest if piece[1] < piece[2]]]

    def payload(group):
        segs, rows = _segments(group, shard_shapes, True)
        flat = {(n, lo, hi): (lax.bitcast_convert_type(shards[n][lo:hi], BF16) if n in F32_PAYLOAD
                              else shards[n][lo:hi].astype(BF16)).reshape(-1) for n, lo, hi in group}
        return segs, _pack(flat, segs, rows, BF16)

    whole = {n: [None] * shard_shapes[n][0] for n in WEIGHTS}

    def take_gathered(segs, gathered):
        gathered = gathered.reshape(N_DEV, -1, PACK_COLS)
        for n, lo, hi in segs:
            piece = _unpack(gathered, segs[(n, lo, hi)])
            if n in F32_PAYLOAD:
                piece = lax.bitcast_convert_type(piece.reshape(N_DEV, -1, 2), F32)
            full = _unshard(piece.reshape((N_DEV, hi - lo) + shard_shapes[n][1:]), SHARD_AXIS[n])
            for j in range(lo, hi):
                whole[n][j] = full[j - lo]

    segs0, payload0 = payload(groups[0])
    segs1, payload1 = payload(groups[1])
    take_gathered(segs0, all_gather(payload0, name="weights_all_gather"))

    gsegs = [_segments(group, shard_shapes, False) for group in groups]
    received = [None, None]

    def parts(g, grads):
        segs, rows = gsegs[g]
        flat = {(n, lo, hi): _shard_split(jnp.stack([grads[(n, j)] for j in range(lo, hi)]), SHARD_AXIS[n])
                .reshape(N_DEV, -1) for n, lo, hi in segs}
        return _pack(flat, segs, rows, BF16)

    def on_received(buf):
        received[1] = buf

    comm = types.SimpleNamespace(rest_payload=payload1, on_gathered=functools.partial(take_gathered, segs1),
                                 rest_parts=functools.partial(parts, 1), on_received=on_received)

    loss, grad_x, grads = local_step(x[0], whole, loss_target[0], comm)
    loss = lax.psum(loss[0, 0], ("x", "y", "c"))

    results = {n: [[] for _ in range(4)] for n in WEIGHTS}
    outs_of = {}
    for g in (1, 0):
        segs, rows = gsegs[g]

        def packed(source):
            return _pack({(n, lo, hi): source[n][lo:hi].reshape(-1) for n, lo, hi in segs}, segs, rows, F32)

        states = (packed(shards), packed(moments_m), packed(moments_v))
        if g == 1:
            *outs_of[g], received[0] = adamw(received[1], *states, name="adamw1", exchange=parts(0, grads))
        else:
            outs_of[g] = adamw(received[0], *states, name="adamw0")
    for g in range(2):
        segs, _ = gsegs[g]
        for n, lo, hi in segs:
            for o, buf in enumerate(outs_of[g]):
                results[n][o].append(_unpack(buf, segs[(n, lo, hi)]).reshape((hi - lo,) + shard_shapes[n][1:]))
    whole_out = [[r[0] if len(r) == 1 else jnp.concatenate(r, axis=0) for r in (results[n][o] for n in WEIGHTS)]
                 for o in range(4)]
    return (loss, grad_x[None], *whole_out[0], *whole_out[1], *whole_out[2], *whole_out[3])
```

```python
import functools
import math
import types

import jax
import jax.numpy as jnp
from jax import lax
from jax.experimental import pallas as pl
from jax.experimental.pallas import tpu as pltpu

F32 = jnp.float32
BF16 = jnp.bfloat16

N_DEV = 8
SB_HEADS = 16
GLA_HEADS = 4
GLA_CHUNK = 64
GLA_SUB = 16
GLA_GATE_RANK = 16
GLA_GATE_NORMALIZER = 16.0
CONV_WIDTH = 3
RMS_EPS = 1e-6
ADAM_LR = 0.001
ADAM_B1 = 0.9
ADAM_B2 = 0.999
ADAM_EPS = 1e-08
ADAM_WD = 0.01
ADAM_STEP = 10

LANES = 128
SB_BLOCK = 256
SB_FWD_Q_BLOCK = 1024
SB_BWD_Q_BLOCK = 1024
VMEM_LIMIT_BYTES = 56 * 1024 * 1024
MM_TM, MM_TN, MM_TK = 1024, 1024, 4096
ROW_BLOCK = 512
PACK_COLS = 1024
PACK_ROW_ALIGN = 16
PACK_ROW_BLOCK = 128

WEIGHTS = ['norm_gains', 'sb_w_qkv', 'sb_w_o', 'conv_w_in', 'conv_w', 'conv_w_out', 'gla_w_in',
           'gla_w_gate_up', 'gla_b_gate', 'gla_head_norm', 'gla_w_o', 'ffn_w_up', 'ffn_w_down']
SHARD_AXIS = {'norm_gains': 2, 'sb_w_qkv': 2, 'sb_w_o': 1, 'conv_w_in': 2, 'conv_w': 2, 'conv_w_out': 1,
              'gla_w_in': 2, 'gla_w_gate_up': 2, 'gla_b_gate': 1, 'gla_head_norm': 2, 'gla_w_o': 1,
              'ffn_w_up': 2, 'ffn_w_down': 1}
F32_PAYLOAD = ('norm_gains', 'conv_w', 'gla_b_gate', 'gla_head_norm')

_NN = (((1,), (0,)), ((), ()))
_NT = (((1,), (1,)), ((), ()))
_TN = (((0,), (0,)), ((), ()))
_DIMS = {'nn': _NN, 'nt': _NT, 'tn': _TN}


def _params(*semantics):
    return pltpu.CompilerParams(dimension_semantics=semantics, vmem_limit_bytes=VMEM_LIMIT_BYTES)


def _dot(a, b, dims=_NN):
    return lax.dot_general(a.astype(BF16), b.astype(BF16), dims, preferred_element_type=F32)


def _split_hi_lo(x):
    hi = x.astype(BF16)
    lo = (x - hi.astype(F32)).astype(BF16)
    return hi, lo


def _dot_exact_rhs(x, ones_mat, dims=_NN):
    hi, lo = _split_hi_lo(x)
    return (lax.dot_general(hi, ones_mat, dims, preferred_element_type=F32)
            + lax.dot_general(lo, ones_mat, dims, preferred_element_type=F32))


def _dot_exact_lhs(ones_mat, x, dims=_NN):
    hi, lo = _split_hi_lo(x)
    return (lax.dot_general(ones_mat, hi, dims, preferred_element_type=F32)
            + lax.dot_general(ones_mat, lo, dims, preferred_element_type=F32))


def _log_sigmoid(z):
    return jnp.minimum(z, 0.0) - jnp.log(1.0 + jnp.exp(-jnp.abs(z)))


def _sigmoid(z):
    return 1.0 / (1.0 + jnp.exp(-z))


def matmul(a, b, *, mode, name, out_dtype=F32, epi=None, extra=None, tm=MM_TM, tn=MM_TN, tk=MM_TK):
    if mode == 'nn':
        (m, k), (k2, n) = a.shape, b.shape
    elif mode == 'nt':
        (m, k), (n, k2) = a.shape, b.shape
    else:
        (k, m), (k2, n) = a.shape, b.shape
    assert k == k2, (a.shape, b.shape, mode)
    tm, tn, tk = min(tm, m), min(tn, n), min(tk, k)
    assert m % tm == 0 and n % tn == 0 and k % tk == 0, (a.shape, b.shape, mode)
    nk = k // tk
    if mode == 'tn':
        a_spec = pl.BlockSpec((tk, tm), lambda i, j, kk: (kk, i))
    else:
        a_spec = pl.BlockSpec((tm, tk), lambda i, j, kk: (i, kk))
    if mode == 'nt':
        b_spec = pl.BlockSpec((tn, tk), lambda i, j, kk: (j, kk))
    else:
        b_spec = pl.BlockSpec((tk, tn), lambda i, j, kk: (kk, j))
    in_specs, operands = [a_spec, b_spec], [a, b]
    if epi == 'logsig16':
        in_specs.append(pl.BlockSpec((1, tn), lambda i, j, kk: (0, j)))
        operands.append(extra)
    elif epi in ('mul2sqrt', 'add'):
        in_specs.append(pl.BlockSpec((tm, tn), lambda i, j, kk: (i, j)))
        operands.append(extra)

    n_extra = len(operands) - 2

    def body(a_ref, b_ref, *rest):
        e_ref = rest[0] if n_extra else None
        o_ref = rest[n_extra]

        def finish(r):
            if epi == 'relu2':
                r = jnp.square(jnp.maximum(r, 0.0))
            elif epi == 'mul2sqrt':
                r = r * (2.0 * jnp.sqrt(e_ref[...].astype(F32)))
            elif epi == 'add':
                r = r + e_ref[...]
            elif epi == 'logsig16':
                r = _log_sigmoid(r + e_ref[...]) / GLA_GATE_NORMALIZER
            o_ref[...] = r.astype(o_ref.dtype)

        part = _dot(a_ref[...], b_ref[...], _DIMS[mode])
        if nk == 1:
            finish(part)
        else:
            acc_ref = rest[-1]
            kk = pl.program_id(2)

            @pl.when(kk == 0)
            def _():
                acc_ref[...] = part

            @pl.when(kk > 0)
            def _():
                acc_ref[...] += part

            @pl.when(kk == nk - 1)
            def _():
                finish(acc_ref[...])

    return pl.pallas_call(
        body, name=name, grid=(m // tm, n // tn, nk), in_specs=in_specs,
        out_specs=pl.BlockSpec((tm, tn), lambda i, j, kk: (i, j)),
        out_shape=jax.ShapeDtypeStruct((m, n), out_dtype),
        scratch_shapes=[pltpu.VMEM((tm, tn), F32)] if nk > 1 else [],
        compiler_params=_params("parallel", "parallel", "arbitrary"),
    )(*operands)


def rms_fwd(x, gain, *, name, res=None, next_gain=None, out_dtype=F32, tm=ROW_BLOCK):
    t, d = x.shape
    tm = min(tm, t)
    row = pl.BlockSpec((tm, d), lambda i: (i, 0))
    vec = pl.BlockSpec((1, d), lambda i: (0, 0))
    in_specs, operands = [row, vec], [x, gain]
    if res is not None:
        in_specs.append(row)
        operands.append(res)
    if next_gain is not None:
        in_specs.append(vec)
        operands.append(next_gain)
    n_in = len(operands)

    def body(x_ref, g_ref, *rest):
        xv = x_ref[...]
        r = lax.rsqrt(jnp.mean(xv * xv, axis=-1, keepdims=True) + RMS_EPS)
        y = xv * r * g_ref[...]
        if res is not None:
            y = rest[0][...] + y
        rest[n_in - 2][...] = y.astype(out_dtype)
        if next_gain is not None:
            r2 = lax.rsqrt(jnp.mean(y * y, axis=-1, keepdims=True) + RMS_EPS)
            rest[n_in - 1][...] = (y * r2 * rest[n_in - 3][...]).astype(BF16)

    shape = jax.ShapeDtypeStruct((t, d), out_dtype)
    return pl.pallas_call(
        body, name=name, grid=(t // tm,), in_specs=in_specs,
        out_specs=row if next_gain is None else [row, row],
        out_shape=shape if next_gain is None else [shape, jax.ShapeDtypeStruct((t, d), BF16)],
        compiler_params=_params("parallel"),
    )(*operands)


def rms_bwd(x, gain, dy, *, name, dres=None, out_dtype=F32, tm=ROW_BLOCK):
    t, d = x.shape
    tm = min(tm, t)
    row = pl.BlockSpec((tm, d), lambda i: (i, 0))
    vec = pl.BlockSpec((1, d), lambda i: (0, 0))
    in_specs, operands = [row, vec, row], [x, gain, dy]
    if dres is not None:
        in_specs.append(row)
        operands.append(dres)

    def body(x_ref, g_ref, dy_ref, *rest):
        dx_ref, dg_ref = rest[-2], rest[-1]

        @pl.when(pl.program_id(0) == 0)
        def _():
            dg_ref[...] = jnp.zeros_like(dg_ref)

        xv, dyv = x_ref[...], dy_ref[...]
        r = lax.rsqrt(jnp.mean(xv * xv, axis=-1, keepdims=True) + RMS_EPS)
        u = dyv * g_ref[...]
        dx = r * u - xv * (r * r * r * jnp.mean(u * xv, axis=-1, keepdims=True))
        if dres is not None:
            dx = rest[0][...] + dx
        dx_ref[...] = dx.astype(out_dtype)
        dg_ref[...] += jnp.sum(dyv * xv * r, axis=0, keepdims=True)

    return pl.pallas_call(
        body, name=name, grid=(t // tm,), in_specs=in_specs, out_specs=[row, vec],
        out_shape=[jax.ShapeDtypeStruct((t, d), out_dtype), jax.ShapeDtypeStruct((1, d), F32)],
        compiler_params=_params("arbitrary"),
    )(*operands)


def loss_head(y, target, *, name, tm=ROW_BLOCK):
    t, d = y.shape
    tm = min(tm, t)
    nt = t // tm
    row = pl.BlockSpec((tm, d), lambda i: (i, 0))

    def body(y_ref, t_ref, loss_ref, dy_ref, acc_ref):
        i = pl.program_id(0)

        @pl.when(i == 0)
        def _():
            acc_ref[...] = jnp.zeros_like(acc_ref)

        err = y_ref[...] - t_ref[...]
        dy_ref[...] = err * (1.0 / d)
        acc_ref[...] += jnp.sum(err * err, axis=0, keepdims=True)

        @pl.when(i == nt - 1)
        def _():
            loss_ref[...] = jnp.sum(acc_ref[...], axis=1, keepdims=True) * (0.5 / d)

    return pl.pallas_call(
        body, name=name, grid=(nt,), in_specs=[row, row],
        out_specs=[pl.BlockSpec((1, 1), lambda i: (0, 0)), row],
        out_shape=[jax.ShapeDtypeStruct((1, 1), F32), jax.ShapeDtypeStruct((t, d), F32)],
        scratch_shapes=[pltpu.VMEM((1, d), F32)], compiler_params=_params("arbitrary"),
    )(y, target)


def _sb_block_iota():
    rows = lax.broadcasted_iota(jnp.int32, (SB_BLOCK, SB_BLOCK), 0)
    cols = lax.broadcasted_iota(jnp.int32, (SB_BLOCK, SB_BLOCK), 1)
    return rows, cols


def _sb_logits(q_h, k_blk, mask):
    z = _dot(q_h, k_blk, _NT)
    ls = _log_sigmoid(z)
    lm = ls - z
    if mask is not None:
        lm = jnp.where(mask, lm, 0.0)
    return ls, lm, jnp.sum(lm, axis=1, keepdims=True)


def _sb_weights(ls, lm, mask, tri_strict, later):
    suffix = _dot_exact_rhs(lm, tri_strict)
    w = jnp.exp(ls + suffix + later)
    return w if mask is None else jnp.where(mask, w, 0.0)


def _rows_from(x, lo, half):
    return x if not lo else jnp.concatenate([x[lo:half], x[half + lo:]], axis=0)


def _add_from(x, lo, half, upd):
    if not lo:
        return x + upd
    n = half - lo
    return jnp.concatenate([x[:lo], x[lo:half] + upd[:n], x[half:half + lo], x[half + lo:] + upd[n:]], axis=0)


def _on_grid_step(p, i):
    return jnp.logical_and(pl.program_id(0) == p, pl.program_id(1) == i)


def sb_fwd(qkv, *, name, gather=None):
    t, d3 = qkv.shape
    d = d3 // 3
    head_dim = d // SB_HEADS
    qb, kb_rows = SB_FWD_Q_BLOCK, SB_BLOCK
    assert 2 * head_dim == LANES and t % qb == 0
    pairs = d // LANES
    per_q = qb // kb_rows
    nq = t // qb
    scale = head_dim ** -0.5

    def body(q_ref, k_ref, v_ref, *rest):
        if gather is None:
            compute(q_ref, k_ref, v_ref, *rest)
            return
        x_ref, o_ref, tot_ref, out_ref, acc_ref, later_ref, send_sems, recv_sems, local_sem = rest
        start, forward, finish = _gather_plan(x_ref, out_ref, send_sems, recv_sems, local_sem)
        pl.when(_on_grid_step(0, 0))(start)
        pl.when(_on_grid_step(7 * pairs // 8, 0))(forward)
        compute(q_ref, k_ref, v_ref, o_ref, tot_ref, acc_ref, later_ref)
        pl.when(_on_grid_step(pairs - 1, nq - 1))(finish)

    def compute(q_ref, k_ref, v_ref, o_ref, tot_ref, acc_ref, later_ref):
        qi = pl.program_id(1)
        lane = lax.broadcasted_iota(jnp.int32, (qb, LANES), 1)
        first = lane < head_dim
        q = q_ref[...] * scale
        q2 = jnp.concatenate([jnp.where(first, q, jnp.zeros_like(q)), jnp.where(first, jnp.zeros_like(q), q)], axis=0)
        rows, cols = _sb_block_iota()
        tri = jnp.where(rows > cols, 1.0, 0.0).astype(BF16)
        q_row = lax.broadcasted_iota(jnp.int32, (qb, kb_rows), 0)
        k_col = lax.broadcasted_iota(jnp.int32, (qb, kb_rows), 1)
        earlier = k_col < q_row
        acc_ref[...] = jnp.zeros_like(acc_ref)
        later_ref[...] = jnp.zeros_like(later_ref)

        def step(kb, lo=None):
            ks = pl.multiple_of(kb * SB_BLOCK, SB_BLOCK)
            k_blk = k_ref[pl.ds(ks, SB_BLOCK), :]
            v_blk = v_ref[pl.ds(ks, SB_BLOCK), :]
            if not lo:
                mask = None if lo is None else jnp.concatenate([earlier] * 2, axis=0)
                ls, lm, row = _sb_logits(q2, k_blk, mask)
                w = _sb_weights(ls, lm, mask, tri, later_ref[...])
                acc_ref[...] += _dot(w, v_blk)
                later_ref[...] += row
                return
            n = qb - lo
            mask = jnp.concatenate([earlier[:n]] * 2, axis=0)
            later = jnp.concatenate([later_ref[lo:qb], later_ref[qb + lo:]], axis=0)
            ls, lm, row = _sb_logits(_rows_from(q2, lo, qb), k_blk, mask)
            upd = _dot(_sb_weights(ls, lm, mask, tri, later), v_blk)
            for base in (0, qb):
                half = slice(0, n) if base == 0 else slice(n, 2 * n)
                acc_ref[base + lo:base + qb] += upd[half]
                later_ref[base + lo:base + qb] += row[half]

        for m in reversed(range(per_q)):
            step(per_q * qi + m, m * kb_rows)

        @pl.loop(0, per_q * qi)
        def _(i):
            step(per_q * qi - 1 - i)

        acc, total = acc_ref[...], later_ref[...]
        o_ref[...] = jnp.where(first, acc[:qb], acc[qb:]).astype(o_ref.dtype)
        tot_ref[...] = jnp.where(first, total[:qb], total[qb:])

    blk = lambda off: pl.BlockSpec((t, LANES), lambda p, i: (0, off + p))
    qblk = pl.BlockSpec((qb, LANES), lambda p, i: (i, p))
    in_specs, operands = [qblk, blk(pairs), blk(2 * pairs)], [qkv, qkv, qkv]
    out_specs = [qblk, qblk]
    out_shape = [jax.ShapeDtypeStruct((t, d), BF16), jax.ShapeDtypeStruct((t, d), F32)]
    sums = [pltpu.VMEM((2 * qb, LANES), F32), pltpu.VMEM((2 * qb, 1), F32)]
    if gather is None:
        return pl.pallas_call(
            body, name=name, grid=(pairs, nq), in_specs=in_specs, out_specs=out_specs, out_shape=out_shape,
            scratch_shapes=sums, compiler_params=_params("parallel", "arbitrary"),
        )(*operands)
    whole = pl.BlockSpec(memory_space=pl.ANY)
    return pl.pallas_call(
        body, name=name, grid=(pairs, nq), in_specs=in_specs + [whole], out_specs=out_specs + [whole],
        out_shape=out_shape + [jax.ShapeDtypeStruct((N_DEV * gather.shape[0], gather.shape[1]), gather.dtype)],
        scratch_shapes=sums + _comm_scratch(), compiler_params=_params("arbitrary", "arbitrary"),
    )(*operands, gather)


def sb_bwd(qkv, tot, do, *, name, exchange=None):
    t, d3 = qkv.shape
    d = d3 // 3
    head_dim = d // SB_HEADS
    pairs = d // LANES
    qb, kb_rows = SB_BWD_Q_BLOCK, SB_BLOCK
    per_q = qb // kb_rows
    nq = t // qb
    scale = head_dim ** -0.5

    def body(*refs):
        if exchange is None:
            compute(*refs)
            return
        q_ref, k_ref, v_ref, tot_ref, do_ref, in_ref, dq_ref, dk_ref, dv_ref, out_ref = refs[:10]
        dk_sum, dv_sum, send_sems, recv_sems, local_sem = refs[10:]
        start, finish = _exchange_plan(in_ref, out_ref, send_sems, recv_sems, local_sem)
        pl.when(_on_grid_step(0, 0))(start)
        compute(q_ref, k_ref, v_ref, tot_ref, do_ref, dq_ref, dk_ref, dv_ref, dk_sum, dv_sum)
        pl.when(_on_grid_step(pairs - 1, nq - 1))(finish)

    def compute(q_ref, k_ref, v_ref, tot_ref, do_ref, dq_ref, dk_ref, dv_ref, dk_sum, dv_sum):
        qi = pl.program_id(1)

        @pl.when(qi == 0)
        def _():
            dk_sum[...] = jnp.zeros_like(dk_sum)
            dv_sum[...] = jnp.zeros_like(dv_sum)

        lane = lax.broadcasted_iota(jnp.int32, (qb, LANES), 1)
        first = lane < head_dim
        q, dov, totv = q_ref[...] * scale, do_ref[...], tot_ref[...]
        second = jnp.logical_not(first)
        q2 = jnp.concatenate([jnp.where(s, q, jnp.zeros_like(q)) for s in (first, second)], axis=0)
        do2 = jnp.concatenate([jnp.where(s, dov, jnp.zeros_like(dov)) for s in (first, second)], axis=0)
        tot2 = jnp.concatenate([totv[:, 0:1], totv[:, head_dim:head_dim + 1]], axis=0)
        rows, cols = _sb_block_iota()
        tri_strict = jnp.where(rows > cols, 1.0, 0.0).astype(BF16)
        tri_before = jnp.where(rows < cols, 1.0, 0.0).astype(BF16)
        q_row = lax.broadcasted_iota(jnp.int32, (qb, kb_rows), 0)
        k_col = lax.broadcasted_iota(jnp.int32, (qb, kb_rows), 1)
        earlier = k_col < q_row

        def step(kb, carry, lo=None):
            ks = pl.multiple_of(kb * SB_BLOCK, SB_BLOCK)
            k_blk = k_ref[pl.ds(ks, SB_BLOCK), :]
            v_blk = v_ref[pl.ds(ks, SB_BLOCK), :]
            dq, seen, before = carry
            mask = None if lo is None else jnp.concatenate([earlier[:qb - lo]] * 2, axis=0)
            q_s, do_s = _rows_from(q2, lo, qb), _rows_from(do2, lo, qb)
            ls, lm, row = _sb_logits(q_s, k_blk, mask)
            seen = _add_from(seen, lo, qb, row)
            w = _sb_weights(ls, lm, mask, tri_strict, _rows_from(tot2 - seen, lo, qb))
            da = _dot(do_s, v_blk, _NT) * w
            g = _dot_exact_rhs(da, tri_before) + _rows_from(before, lo, qb)
            dz = da - jnp.exp(ls) * (da + g)
            if mask is not None:
                dz = jnp.where(mask, dz, 0.0)
            dk_sum[pl.ds(ks, SB_BLOCK), :] += _dot(dz, q_s, _TN)
            dv_sum[pl.ds(ks, SB_BLOCK), :] += _dot(w, do_s, _TN)
            return (_add_from(dq, lo, qb, _dot(dz, k_blk * scale)), seen,
                    _add_from(before, lo, qb, jnp.sum(da, axis=1, keepdims=True)))

        zero = jnp.zeros((2 * qb, LANES), F32)
        zcol = jnp.zeros((2 * qb, 1), F32)
        out = lax.fori_loop(0, per_q * qi, step, (zero, zcol, zcol))
        for m in range(per_q):
            out = step(per_q * qi + m, out, m * kb_rows)
        dq = out[0]
        dq_ref[...] = jnp.where(first, dq[:qb], dq[qb:]).astype(dq_ref.dtype)

        @pl.when(qi == nq - 1)
        def _():
            dk_ref[...] = dk_sum[...].astype(dk_ref.dtype)
            dv_ref[...] = dv_sum[...].astype(dv_ref.dtype)

    qblk = pl.BlockSpec((qb, LANES), lambda p, i: (i, p))
    col = lambda off: pl.BlockSpec((t, LANES), lambda p, i: (0, off + p))
    shape = jax.ShapeDtypeStruct((t, d), BF16)
    in_specs, operands = [qblk, col(pairs), col(2 * pairs), qblk, qblk], [qkv, qkv, qkv, tot, do]
    out_specs, out_shape = [qblk, col(0), col(0)], [shape, shape, shape]
    sums = [pltpu.VMEM((t, LANES), F32), pltpu.VMEM((t, LANES), F32)]
    if exchange is None:
        return pl.pallas_call(
            body, name=name, grid=(pairs, nq), in_specs=in_specs, out_specs=out_specs, out_shape=out_shape,
            scratch_shapes=sums, compiler_params=_params("parallel", "arbitrary"),
        )(*operands)
    whole = pl.BlockSpec(memory_space=pl.ANY)
    return pl.pallas_call(
        body, name=name, grid=(pairs, nq), in_specs=in_specs + [whole], out_specs=out_specs + [whole],
        out_shape=out_shape + [jax.ShapeDtypeStruct(exchange.shape, exchange.dtype)],
        scratch_shapes=sums + _comm_scratch(), compiler_params=_params("arbitrary", "arbitrary"),
    )(*operands, exchange)


def _shift_down(x, s):
    rows = lax.broadcasted_iota(jnp.int32, x.shape, 0)
    return jnp.where(rows >= s, pltpu.roll(x, s, 0), 0.0)


def _shift_up(x, s):
    t = x.shape[0]
    rows = lax.broadcasted_iota(jnp.int32, x.shape, 0)
    return jnp.where(rows < t - s, pltpu.roll(x, t - s, 0), 0.0)


def conv_fwd(bcu, w, *, name):
    t, d3 = bcu.shape
    d = d3 // 3
    nb = d // LANES
    col = lambda off: pl.BlockSpec((t, LANES), lambda j: (0, off + j))

    def body(b_ref, c_ref, u_ref, w_ref, y_ref):
        hh = c_ref[...] * u_ref[...]
        conv = w_ref[0:1, :] * _shift_down(hh, 2) + w_ref[1:2, :] * _shift_down(hh, 1) + w_ref[2:3, :] * hh
        y_ref[...] = (b_ref[...] * conv).astype(y_ref.dtype)

    return pl.pallas_call(
        body, name=name, grid=(nb,),
        in_specs=[col(0), col(nb), col(2 * nb), pl.BlockSpec((CONV_WIDTH, LANES), lambda j: (0, j))],
        out_specs=col(0), out_shape=jax.ShapeDtypeStruct((t, d), BF16), compiler_params=_params("parallel"),
    )(bcu, bcu, bcu, w)


def conv_bwd(bcu, w, dy, *, name):
    t, d3 = bcu.shape
    d = d3 // 3
    nb = d // LANES
    col = lambda off: pl.BlockSpec((t, LANES), lambda j: (0, off + j))
    wspec = pl.BlockSpec((CONV_WIDTH, LANES), lambda j: (0, j))

    def body(b_ref, c_ref, u_ref, w_ref, dy_ref, db_ref, dc_ref, du_ref, dw_ref):
        c, u, dyv = c_ref[...], u_ref[...], dy_ref[...]
        hh = c * u
        h2, h1 = _shift_down(hh, 2), _shift_down(hh, 1)
        w0, w1, w2 = w_ref[0:1, :], w_ref[1:2, :], w_ref[2:3, :]
        db_ref[...] = (dyv * (w0 * h2 + w1 * h1 + w2 * hh)).astype(db_ref.dtype)
        dconv = dyv * b_ref[...]
        dhh = w2 * dconv + w1 * _shift_up(dconv, 1) + w0 * _shift_up(dconv, 2)
        dc_ref[...] = (dhh * u).astype(dc_ref.dtype)
        du_ref[...] = (dhh * c).astype(du_ref.dtype)
        dw_ref[0:1, :] = jnp.sum(dconv * h2, axis=0, keepdims=True)
        dw_ref[1:2, :] = jnp.sum(dconv * h1, axis=0, keepdims=True)
        dw_ref[2:3, :] = jnp.sum(dconv * hh, axis=0, keepdims=True)

    shape = jax.ShapeDtypeStruct((t, d), BF16)
    return pl.pallas_call(
        body, name=name, grid=(nb,),
        in_specs=[col(0), col(nb), col(2 * nb), wspec, col(0)],
        out_specs=[col(0), col(0), col(0), wspec],
        out_shape=[shape, shape, shape, jax.ShapeDtypeStruct((CONV_WIDTH, d), F32)],
        compiler_params=_params("parallel"),
    )(bcu, bcu, bcu, w, dy)


def _gla_chunk(q, k, lg, scale):
    c = GLA_CHUNK
    rows = lax.broadcasted_iota(jnp.int32, (c, c), 0)
    cols = lax.broadcasted_iota(jnp.int32, (c, c), 1)
    causal = rows >= cols
    tril = jnp.where(causal, 1.0, 0.0).astype(BF16)
    q = q * scale
    cum = _dot_exact_lhs(tril, lg)
    last = cum[c - 1:c, :]
    eq = jnp.exp(cum)
    el = jnp.exp(last - cum)
    return causal, tril, q, k, cum, lg, last, eq, el


def _gla_sub_blocks(q, k, cum, lg):
    key_row = lax.broadcasted_iota(jnp.int32, (GLA_CHUNK, 1), 0)
    out = []
    for lo in range(0, GLA_CHUNK, GLA_SUB):
        hi = lo + GLA_SUB
        ref = cum[lo:lo + 1, :] - lg[lo:lo + 1, :]
        eq = jnp.exp(cum[lo:hi] - ref)
        ek = jnp.where(key_row < hi, jnp.exp(ref - cum), 0.0)
        out.append((slice(lo, hi), eq, ek, q[lo:hi] * eq, k * ek))
    return out


def _gla_scores(subs, causal):
    return jnp.where(causal, jnp.concatenate([_dot(qt, kt, _NT) for _, _, _, qt, kt in subs], axis=0), 0.0)


def gla_fwd(proj, lg, *, name):
    t, d3 = proj.shape
    d = d3 // 3
    dk, dv = d // 2 // GLA_HEADS, d // GLA_HEADS
    assert dk == LANES and dv == 2 * LANES
    c = GLA_CHUNK
    nc = t // c
    scale = dk ** -0.5
    nh = GLA_HEADS

    def body(q_ref, k_ref, v_ref, lg_ref, o_ref, st_out_ref, st_ref):
        @pl.when(pl.program_id(0) == 0)
        def _():
            st_ref[...] = jnp.zeros_like(st_ref)

        for h in range(nh):
            kcols, vcols = slice(h * dk, (h + 1) * dk), slice(h * dv, (h + 1) * dv)
            causal, _, q, k, cum, lg, last, eq, el = _gla_chunk(q_ref[:, kcols], k_ref[:, kcols], lg_ref[:, kcols], scale)
            v = v_ref[:, vcols]
            st = st_ref[h]
            st_out_ref[h] = st
            scores = _gla_scores(_gla_sub_blocks(q, k, cum, lg), causal)
            o_ref[:, vcols] = _dot(q * eq, st, _NT) + _dot(scores, v)
            st_ref[h] = st * jnp.exp(last) + _dot(v, k * el, _TN)

    half, full = pl.BlockSpec((c, d // 2), lambda i: (i, 0)), pl.BlockSpec((c, d), lambda i: (i, 0))
    return pl.pallas_call(
        body, name=name, grid=(nc,),
        in_specs=[half, pl.BlockSpec((c, d // 2), lambda i: (i, 1)), pl.BlockSpec((c, d), lambda i: (i, 1)), half],
        out_specs=[full, pl.BlockSpec((nh, None, dv, dk), lambda i: (0, i, 0, 0))],
        out_shape=[jax.ShapeDtypeStruct((t, d), F32), jax.ShapeDtypeStruct((nh, nc, dv, dk), F32)],
        scratch_shapes=[pltpu.VMEM((nh, dv, dk), F32)], compiler_params=_params("arbitrary"),
    )(proj, proj, proj, lg)


def gla_bwd(proj, lg, states, do, *, name):
    t, d3 = proj.shape
    d = d3 // 3
    dk, dv = d // 2 // GLA_HEADS, d // GLA_HEADS
    c = GLA_CHUNK
    nc = t // c
    scale = dk ** -0.5
    nh = GLA_HEADS

    def body(q_ref, k_ref, v_ref, lg_ref, st_ref, do_ref, dq_ref, dk_ref, dv_ref, dzg_ref, dbg_ref, dst_ref):
        @pl.when(pl.program_id(0) == 0)
        def _():
            dst_ref[...] = jnp.zeros_like(dst_ref)
            dbg_ref[...] = jnp.zeros_like(dbg_ref)

        for h in range(nh):
            kcols, vcols = slice(h * dk, (h + 1) * dk), slice(h * dv, (h + 1) * dv)
            causal, tril, q, k, cum, lg, last, eq, el = _gla_chunk(q_ref[:, kcols], k_ref[:, kcols], lg_ref[:, kcols], scale)
            v, st, dov, dst = v_ref[:, vcols], st_ref[h], do_ref[:, vcols], dst_ref[h]
            subs = _gla_sub_blocks(q, k, cum, lg)
            qt, kh = q * eq, k * el
            scores = _gla_scores(subs, causal)
            dscores = jnp.where(causal, _dot(dov, v, _NT), 0.0)
            dq_parts = []
            dkh = _dot(v, dst)
            dk_h = dkh * el
            for rows, eq_sub, ek_sub, qt_sub, kt_sub in subs:
                dq_parts.append(_dot(dscores[rows], kt_sub) * eq_sub)
                dk_h = dk_h + _dot(dscores[rows], qt_sub, _TN) * ek_sub
            dq = _dot(dov, st) * eq + jnp.concatenate(dq_parts, axis=0)
            dv_ref[:, vcols] = (_dot(scores, dov, _TN) + _dot(kh, dst, _NT)).astype(dv_ref.dtype)
            dq_ref[:, kcols] = (dq * scale).astype(dq_ref.dtype)
            dk_ref[:, kcols] = dk_h.astype(dk_ref.dtype)
            e_last = jnp.exp(last)
            dlast = jnp.sum(kh * dkh, axis=0, keepdims=True) + e_last * jnp.sum(dst * st, axis=0, keepdims=True)
            dcum = q * dq - k * dk_h
            dlg = _dot_exact_lhs(tril, dcum, _TN) + dlast
            dzg = dlg * (1.0 - jnp.exp(lg * GLA_GATE_NORMALIZER)) / GLA_GATE_NORMALIZER
            dzg_ref[:, kcols] = dzg.astype(dzg_ref.dtype)
            dbg_ref[:, kcols] += jnp.sum(dzg, axis=0, keepdims=True)
            dst_ref[h] = dst * e_last + _dot(dov, qt, _TN)

    rev = lambda i: nc - 1 - i
    half, full = pl.BlockSpec((c, d // 2), lambda i: (rev(i), 0)), pl.BlockSpec((c, d), lambda i: (rev(i), 0))
    half_shape = jax.ShapeDtypeStruct((t, d // 2), BF16)
    return pl.pallas_call(
        body, name=name, grid=(nc,),
        in_specs=[half, pl.BlockSpec((c, d // 2), lambda i: (rev(i), 1)), pl.BlockSpec((c, d), lambda i: (rev(i), 1)), half,
                  pl.BlockSpec((nh, None, dv, dk), lambda i: (0, rev(i), 0, 0)), full],
        out_specs=[half, half, full, half, pl.BlockSpec((1, d // 2), lambda i: (0, 0))],
        out_shape=[half_shape, half_shape, jax.ShapeDtypeStruct((t, d), BF16), half_shape,
                   jax.ShapeDtypeStruct((1, d // 2), F32)],
        scratch_shapes=[pltpu.VMEM((nh, dv, dk), F32)], compiler_params=_params("arbitrary"),
    )(proj, proj, proj, lg, states, do)


def gla_post_fwd(o, proj, head_norm, *, name, tm=ROW_BLOCK):
    t, d = o.shape
    dv = d // GLA_HEADS
    tm = min(tm, t)

    def body(o_ref, g_ref, hn_ref, y_ref):
        for h in range(GLA_HEADS):
            sl = slice(h * dv, (h + 1) * dv)
            ov, gv = o_ref[:, sl], g_ref[:, sl]
            r = lax.rsqrt(jnp.mean(ov * ov, axis=-1, keepdims=True) + RMS_EPS)
            y_ref[:, sl] = ((ov * r * hn_ref[:, sl]) * (gv * _sigmoid(gv))).astype(y_ref.dtype)

    row = pl.BlockSpec((tm, d), lambda i: (i, 0))
    return pl.pallas_call(
        body, name=name, grid=(t // tm,),
        in_specs=[row, pl.BlockSpec((tm, d), lambda i: (i, 2)), pl.BlockSpec((1, d), lambda i: (0, 0))],
        out_specs=row, out_shape=jax.ShapeDtypeStruct((t, d), BF16), compiler_params=_params("parallel"),
    )(o, proj, head_norm)


def gla_post_bwd(o, proj, head_norm, dy, *, name, tm=ROW_BLOCK):
    t, d = o.shape
    dv = d // GLA_HEADS
    tm = min(tm, t)

    def body(o_ref, g_ref, hn_ref, dy_ref, do_ref, dg_ref, dhn_ref):
        @pl.when(pl.program_id(0) == 0)
        def _():
            dhn_ref[...] = jnp.zeros_like(dhn_ref)

        for h in range(GLA_HEADS):
            sl = slice(h * dv, (h + 1) * dv)
            ov, gv, dyv, hn = o_ref[:, sl], g_ref[:, sl], dy_ref[:, sl], hn_ref[:, sl]
            r = lax.rsqrt(jnp.mean(ov * ov, axis=-1, keepdims=True) + RMS_EPS)
            sg = _sigmoid(gv)
            silu = gv * sg
            on = ov * r * hn
            dg_ref[:, sl] = (dyv * on * (sg * (1.0 + gv * (1.0 - sg)))).astype(dg_ref.dtype)
            don = dyv * silu
            u = don * hn
            do_ref[:, sl] = (r * u - ov * (r * r * r * jnp.mean(u * ov, axis=-1, keepdims=True))).astype(do_ref.dtype)
            dhn_ref[:, sl] += jnp.sum(don * ov * r, axis=0, keepdims=True)

    row = pl.BlockSpec((tm, d), lambda i: (i, 0))
    vec = pl.BlockSpec((1, d), lambda i: (0, 0))
    shape = jax.ShapeDtypeStruct((t, d), BF16)
    return pl.pallas_call(
        body, name=name, grid=(t // tm,),
        in_specs=[row, pl.BlockSpec((tm, d), lambda i: (i, 2)), vec, row],
        out_specs=[row, row, vec], out_shape=[shape, shape, jax.ShapeDtypeStruct((1, d), F32)],
        compiler_params=_params("arbitrary"),
    )(o, proj, head_norm, dy)


def _ffn_fwd(h, xn, gains, w_up, w_down, tag, next_gain):
    act = matmul(xn, w_up, mode='nn', epi='relu2', out_dtype=BF16, name=f"{tag}_ffn_up")
    f = matmul(act, w_down, mode='nn', name=f"{tag}_ffn_down")
    out = rms_fwd(f, gains[3], res=h, next_gain=next_gain, name=f"{tag}_ffn_out")
    h_out, xn_next = (out, None) if next_gain is None else out
    return h_out, xn_next, (h, xn, act, f)


def _ffn_bwd(dh, saved, gains, w_up, w_down, tag):
    h, xn, act, f = saved
    df, dg3 = rms_bwd(f, gains[3], dh, out_dtype=BF16, name=f"{tag}_ffn_out_bwd")
    du = matmul(df, w_down, mode='nt', epi='mul2sqrt', extra=act, out_dtype=BF16, name=f"{tag}_ffn_da")
    dw_down = matmul(act, df, mode='tn', out_dtype=BF16, name=f"{tag}_ffn_dwdown")
    dw_up = matmul(xn, du, mode='tn', out_dtype=BF16, name=f"{tag}_ffn_dwup")
    dxn = matmul(du, w_up, mode='nt', name=f"{tag}_ffn_dxn")
    dh_in, dg2 = rms_bwd(h, gains[2], dxn, dres=dh, name=f"{tag}_ffn_norm_bwd")
    return dh_in, dg2, dg3, dw_up, dw_down


def _sb_layer_fwd(xn, w, j, tag, comm=None):
    qkv = matmul(xn, w['sb_w_qkv'][j], mode='nn', out_dtype=BF16, name=f"{tag}_qkv")
    if comm is None:
        o, tot = sb_fwd(qkv, name=f"{tag}_sb")
    else:
        o, tot, gathered = sb_fwd(qkv, name=f"{tag}_sb", gather=comm.rest_payload)
        comm.on_gathered(gathered)
    m = matmul(o, w['sb_w_o'][j], mode='nn', name=f"{tag}_wo")
    return m, (qkv, o, tot)


def _sb_layer_bwd(dm, xn, saved, w, j, tag, comm=None, grads=None):
    qkv, o, tot = saved
    do = matmul(dm, w['sb_w_o'][j], mode='nt', out_dtype=BF16, name=f"{tag}_do")
    dw_o = matmul(o, dm, mode='tn', out_dtype=BF16, name=f"{tag}_dwo")
    if comm is None:
        dq, dk, dv = sb_bwd(qkv, tot, do, name=f"{tag}_sb_bwd")
    else:
        parts = comm.rest_parts({**grads, ('sb_w_o', j): dw_o})
        dq, dk, dv, received = sb_bwd(qkv, tot, do, name=f"{tag}_sb_bwd", exchange=parts)
        comm.on_received(received)
    dqkv = jnp.concatenate([dq, dk, dv], axis=1)
    dw_qkv = matmul(xn, dqkv, mode='tn', out_dtype=BF16, name=f"{tag}_dwqkv")
    dxn = matmul(dqkv, w['sb_w_qkv'][j], mode='nt', name=f"{tag}_dxn")
    return dxn, {('sb_w_qkv', j): dw_qkv, ('sb_w_o', j): dw_o}


def _conv_layer_fwd(xn, w, j, tag):
    bcu = matmul(xn, w['conv_w_in'][j], mode='nn', name=f"{tag}_in")
    y = conv_fwd(bcu, w['conv_w'][j], name=f"{tag}_conv")
    m = matmul(y, w['conv_w_out'][j], mode='nn', name=f"{tag}_out")
    return m, (bcu, y)


def _conv_layer_bwd(dm, xn, saved, w, j, tag):
    bcu, y = saved
    dy = matmul(dm, w['conv_w_out'][j], mode='nt', name=f"{tag}_dy")
    dw_out = matmul(y, dm, mode='tn', out_dtype=BF16, name=f"{tag}_dwout")
    db, dc, du, dw_conv = conv_bwd(bcu, w['conv_w'][j], dy, name=f"{tag}_conv_bwd")
    dbcu = jnp.concatenate([db, dc, du], axis=1)
    dw_in = matmul(xn, dbcu, mode='tn', out_dtype=BF16, name=f"{tag}_dwin")
    dxn = matmul(dbcu, w['conv_w_in'][j], mode='nt', name=f"{tag}_dxn")
    return dxn, {('conv_w_in', j): dw_in, ('conv_w', j): dw_conv, ('conv_w_out', j): dw_out}


def _gla_split(w_in, w_gate_up):
    d = w_in.shape[0]
    w_main = w_in[:, :3 * d]
    w_a = jnp.pad(w_in[:, 3 * d:], ((0, 0), (0, LANES - GLA_GATE_RANK)))
    w_gu = jnp.pad(w_gate_up, ((0, LANES - GLA_GATE_RANK), (0, 0)))
    return w_main, w_a, w_gu


def _gla_layer_fwd(xn, w, j, tag):
    w_main, w_a, w_gu = _gla_split(w['gla_w_in'][j], w['gla_w_gate_up'][j])
    proj = matmul(xn, w_main, mode='nn', name=f"{tag}_in")
    a_low = matmul(xn, w_a, mode='nn', out_dtype=BF16, name=f"{tag}_alow")
    lg = matmul(a_low, w_gu, mode='nn', epi='logsig16', extra=w['gla_b_gate'][j][None, :], name=f"{tag}_gate")
    o, states = gla_fwd(proj, lg, name=f"{tag}_gla")
    hn = w['gla_head_norm'][j].reshape(1, -1)
    y = gla_post_fwd(o, proj, hn, name=f"{tag}_post")
    m = matmul(y, w['gla_w_o'][j], mode='nn', name=f"{tag}_wo")
    return m, (proj, a_low, lg, o, states, y)


def _gla_layer_bwd(dm, xn, saved, w, j, tag):
    proj, a_low, lg, o, states, y = saved
    w_main, w_a, w_gu = _gla_split(w['gla_w_in'][j], w['gla_w_gate_up'][j])
    hn = w['gla_head_norm'][j].reshape(1, -1)
    dy = matmul(dm, w['gla_w_o'][j], mode='nt', name=f"{tag}_dy")
    dw_o = matmul(y, dm, mode='tn', out_dtype=BF16, name=f"{tag}_dwo")
    do, dg, dhn = gla_post_bwd(o, proj, hn, dy, name=f"{tag}_post_bwd")
    dq, dk, dv, dzg, dbg = gla_bwd(proj, lg, states, do, name=f"{tag}_gla_bwd")
    da_low = matmul(dzg, w_gu, mode='nt', out_dtype=BF16, name=f"{tag}_dalow")
    dw_gu = matmul(a_low, dzg, mode='tn', out_dtype=BF16, name=f"{tag}_dwgu")[:GLA_GATE_RANK]
    dproj = jnp.concatenate([dq, dk, dv, dg], axis=1)
    dw_main = matmul(xn, dproj, mode='tn', out_dtype=BF16, name=f"{tag}_dwin")
    dw_a = matmul(xn, da_low, mode='tn', out_dtype=BF16, name=f"{tag}_dwa")[:, :GLA_GATE_RANK]
    dxn_a = matmul(da_low, w_a, mode='nt', name=f"{tag}_dxn_a")
    dxn = matmul(dproj, w_main, mode='nt', epi='add', extra=dxn_a, name=f"{tag}_dxn")
    grads = {('gla_w_in', j): jnp.concatenate([dw_main, dw_a], axis=1), ('gla_w_gate_up', j): dw_gu,
             ('gla_b_gate', j): dbg[0], ('gla_head_norm', j): dhn.reshape(w['gla_head_norm'][j].shape),
             ('gla_w_o', j): dw_o}
    return dxn, grads


_MIXERS = ((_sb_layer_fwd, _sb_layer_bwd), (_conv_layer_fwd, _conv_layer_bwd), (_gla_layer_fwd, _gla_layer_bwd))


def local_step(x, w, target, comm=None):
    depth = len(w['norm_gains'])
    h = x
    tape = []
    xn = None
    for i in range(depth):
        kind, j = i % 3, i // 3
        tag = f"l{i}"
        extra = {'comm': comm} if (comm is not None and i == 0) else {}
        gains = [w['norm_gains'][i][s][None, :] for s in range(4)]
        if xn is None:
            xn = rms_fwd(h, gains[0], name=f"{tag}_mix_norm", out_dtype=BF16)
        m, saved = _MIXERS[kind][0](xn, w, j, tag, **extra)
        h_mid, xn_ffn = rms_fwd(m, gains[1], res=h, next_gain=gains[2], name=f"{tag}_mix_out")
        next_gain = w['norm_gains'][i + 1][0][None, :] if i + 1 < depth else None
        h_out, xn_next, ffn_saved = _ffn_fwd(h_mid, xn_ffn, gains, w['ffn_w_up'][i], w['ffn_w_down'][i], tag, next_gain)
        tape.append((h, xn, m, saved, ffn_saved, gains))
        h, xn = h_out, xn_next
    loss, dh = loss_head(h, target, name="loss_head")

    grads = {}
    for i in reversed(range(depth)):
        kind, j = i % 3, i // 3
        tag = f"l{i}"
        h_in, xn, m, saved, ffn_saved, gains = tape[i]
        dg = [None] * 4
        dh, dg[2], dg[3], grads[('ffn_w_up', i)], grads[('ffn_w_down', i)] = _ffn_bwd(
            dh, ffn_saved, gains, w['ffn_w_up'][i], w['ffn_w_down'][i], tag)
        dm, dg[1] = rms_bwd(m, gains[1], dh, out_dtype=BF16, name=f"{tag}_mix_out_bwd")
        extra = {'comm': comm, 'grads': grads} if (comm is not None and i == 0) else {}
        dxn, g = _MIXERS[kind][1](dm, xn, saved, w, j, tag, **extra)
        grads.update(g)
        dh, dg[0] = rms_bwd(h_in, gains[0], dxn, dres=dh, name=f"{tag}_mix_norm_bwd")
        grads[('norm_gains', i)] = jnp.concatenate(dg, axis=0)
    return loss, dh, grads


def _segments(keys, shard_shapes, f32_as_pairs):
    segs, row = {}, 0
    for name, lo, hi in keys:
        n = (hi - lo) * math.prod(shard_shapes[name][1:]) * (2 if f32_as_pairs and name in F32_PAYLOAD else 1)
        nrows = -(-n // (PACK_COLS * PACK_ROW_ALIGN)) * PACK_ROW_ALIGN
        segs[(name, lo, hi)] = (row, nrows, n)
        row += nrows
    return segs, -(-row // PACK_ROW_BLOCK) * PACK_ROW_BLOCK


def _pack(parts, segs, total_rows, dtype):
    pieces, row = [], 0
    for key in segs:
        _, nrows, n = segs[key]
        p = parts[key].astype(dtype)
        lead = p.shape[:-1]
        if nrows * PACK_COLS > n:
            p = jnp.pad(p, [(0, 0)] * len(lead) + [(0, nrows * PACK_COLS - n)])
        pieces.append(p.reshape(lead + (nrows, PACK_COLS)))
        row += nrows
    if total_rows > row:
        pieces.append(jnp.zeros(lead + (total_rows - row, PACK_COLS), dtype))
    return jnp.concatenate(pieces, axis=-2)


def _unpack(buf, seg):
    first, nrows, n = seg
    piece = buf[..., first:first + nrows, :]
    return piece.reshape(piece.shape[:-2] + (nrows * PACK_COLS,))[..., :n]


def _unshard(gathered, axis):
    moved = jnp.moveaxis(gathered, 0, axis)
    shape = moved.shape
    return moved.reshape(shape[:axis] + (shape[axis] * shape[axis + 1],) + shape[axis + 2:])


def _shard_split(full, axis):
    shape = full.shape
    cut = full.reshape(shape[:axis] + (N_DEV, shape[axis] // N_DEV) + shape[axis + 1:])
    return jnp.moveaxis(cut, axis, 0)


def _mesh_position():
    return lax.axis_index("x"), lax.axis_index("y"), lax.axis_index("c")


def _comm_scratch():
    return [pltpu.SemaphoreType.DMA((N_DEV - 1,)), pltpu.SemaphoreType.DMA((N_DEV - 1,)), pltpu.SemaphoreType.DMA]


def _gather_plan(x_ref, out_ref, send_sems, recv_sems, local_sem):
    rows = x_ref.shape[0]
    x, y, c = _mesh_position()
    me, sibling = (x, y, c), (x, y, 1 - c)
    chips = [(1 - x, y), (x, 1 - y), (1 - x, 1 - y)]

    def block(px, py, pc):
        return out_ref.at[pl.ds((4 * px + 2 * py + pc) * rows, rows), :]

    def copy(k, blk, to, src=None):
        return pltpu.make_async_remote_copy(
            src_ref=block(*blk) if src is None else src, dst_ref=block(*blk),
            send_sem=send_sems.at[k], recv_sem=recv_sems.at[k],
            device_id=to, device_id_type=pl.DeviceIdType.MESH)

    mine = pltpu.make_async_copy(x_ref, block(*me), local_sem)
    first = [copy(0, me, sibling, src=x_ref)]
    first += [copy(1 + j, me, (*chip, c), src=x_ref) for j, chip in enumerate(chips)]
    passed = [copy(4 + j, (*chip, c), sibling) for j, chip in enumerate(chips)]

    def start():
        mine.start()
        for cp in first:
            cp.start()

    def forward():
        for j, chip in enumerate(chips):
            copy(1 + j, (*chip, c), me).wait_recv()
            passed[j].start()

    def finish():
        copy(0, sibling, me).wait_recv()
        for j, chip in enumerate(chips):
            copy(4 + j, (*chip, 1 - c), me).wait_recv()
        for cp in first + passed:
            cp.wait_send()
        mine.wait()

    return start, forward, finish


def _exchange_plan(in_ref, out_ref, send_sems, recv_sems, local_sem):
    x, y, c = _mesh_position()
    my_id = 4 * x + 2 * y + c
    mine = pltpu.make_async_copy(in_ref.at[my_id], out_ref.at[my_id], local_sem)

    def copy(k, receive):
        px = 1 - x if k & 4 else x
        py = 1 - y if k & 2 else y
        pc = 1 - c if k & 1 else c
        peer_id = 4 * px + 2 * py + pc
        return pltpu.make_async_remote_copy(
            src_ref=in_ref.at[peer_id], dst_ref=out_ref.at[peer_id if receive else my_id],
            send_sem=send_sems.at[k - 1], recv_sem=recv_sems.at[k - 1],
            device_id=(px, py, pc), device_id_type=pl.DeviceIdType.MESH)

    def start():
        mine.start()
        for k in range(1, N_DEV):
            copy(k, False).start()

    def finish():
        for k in range(1, N_DEV):
            copy(k, True).wait_recv()
        for k in range(1, N_DEV):
            copy(k, False).wait_send()
        mine.wait()

    return start, finish


def all_gather(shard, *, name):
    rows, cols = shard.shape

    def body(x_ref, out_ref, send_sems, recv_sems, local_sem):
        start, forward, finish = _gather_plan(x_ref, out_ref, send_sems, recv_sems, local_sem)
        start()
        forward()
        finish()

    return pl.pallas_call(
        body, name=name, out_shape=jax.ShapeDtypeStruct((N_DEV * rows, cols), shard.dtype),
        in_specs=[pl.BlockSpec(memory_space=pl.ANY)], out_specs=pl.BlockSpec(memory_space=pl.ANY),
        scratch_shapes=_comm_scratch(),
    )(shard)


def adamw(parts, w, m, v, *, name, exchange=None):
    _, rows, cols = parts.shape
    tr = PACK_ROW_BLOCK
    steps = rows // tr
    c1 = 1.0 - ADAM_B1 ** ADAM_STEP
    c2 = 1.0 - ADAM_B2 ** ADAM_STEP

    def body(p_ref, w_ref, m_ref, v_ref, *rest):
        if exchange is None:
            update(p_ref, w_ref, m_ref, v_ref, *rest)
            return
        in_ref, g_ref, d_ref, nm_ref, nv_ref, out_ref, send_sems, recv_sems, local_sem = rest
        start, finish = _exchange_plan(in_ref, out_ref, send_sems, recv_sems, local_sem)
        pl.when(pl.program_id(0) == 0)(start)
        update(p_ref, w_ref, m_ref, v_ref, g_ref, d_ref, nm_ref, nv_ref)
        pl.when(pl.program_id(0) == steps - 1)(finish)

    def update(p_ref, w_ref, m_ref, v_ref, g_ref, d_ref, nm_ref, nv_ref):
        g = p_ref[0].astype(F32)
        for s in range(1, N_DEV):
            g = g + p_ref[s].astype(F32)
        nm = ADAM_B1 * m_ref[...] + (1.0 - ADAM_B1) * g
        nv = ADAM_B2 * v_ref[...] + (1.0 - ADAM_B2) * jnp.square(g)
        m_hat = nm / c1
        v_hat = nv / c2
        g_ref[...] = g
        d_ref[...] = -ADAM_LR * (m_hat / (jnp.sqrt(v_hat) + ADAM_EPS) + ADAM_WD * w_ref[...])
        nm_ref[...] = nm
        nv_ref[...] = nv

    row = pl.BlockSpec((tr, cols), lambda i: (i, 0))
    shape = jax.ShapeDtypeStruct((rows, cols), F32)
    in_specs = [pl.BlockSpec((N_DEV, tr, cols), lambda i: (0, i, 0)), row, row, row]
    if exchange is None:
        return pl.pallas_call(
            body, name=name, grid=(steps,), in_specs=in_specs,
            out_specs=[row, row, row, row], out_shape=[shape, shape, shape, shape],
            compiler_params=_params("parallel"),
        )(parts, w, m, v)
    whole = pl.BlockSpec(memory_space=pl.ANY)
    return pl.pallas_call(
        body, name=name, grid=(steps,), in_specs=in_specs + [whole],
        out_specs=[row, row, row, row, whole],
        out_shape=[shape, shape, shape, shape, jax.ShapeDtypeStruct(exchange.shape, exchange.dtype)],
        scratch_shapes=_comm_scratch(), compiler_params=_params("arbitrary"),
    )(parts, w, m, v, exchange)


def kernel(x, norm_gains, sb_w_qkv, sb_w_o, conv_w_in, conv_w, conv_w_out, gla_w_in, gla_w_gate_up, gla_b_gate, gla_head_norm, gla_w_o, ffn_w_up, ffn_w_down, loss_target, m_norm_gains, m_sb_w_qkv, m_sb_w_o, m_conv_w_in, m_conv_w, m_conv_w_out, m_gla_w_in, m_gla_w_gate_up, m_gla_b_gate, m_gla_head_norm, m_gla_w_o, m_ffn_w_up, m_ffn_w_down, v_norm_gains, v_sb_w_qkv, v_sb_w_o, v_conv_w_in, v_conv_w, v_conv_w_out, v_gla_w_in, v_gla_w_gate_up, v_gla_b_gate, v_gla_head_norm, v_gla_w_o, v_ffn_w_up, v_ffn_w_down):
    shards = dict(zip(WEIGHTS, (norm_gains, sb_w_qkv, sb_w_o, conv_w_in, conv_w, conv_w_out, gla_w_in,
                                gla_w_gate_up, gla_b_gate, gla_head_norm, gla_w_o, ffn_w_up, ffn_w_down)))
    moments_m = dict(zip(WEIGHTS, (m_norm_gains, m_sb_w_qkv, m_sb_w_o, m_conv_w_in, m_conv_w, m_conv_w_out,
                                   m_gla_w_in, m_gla_w_gate_up, m_gla_b_gate, m_gla_head_norm, m_gla_w_o,
                                   m_ffn_w_up, m_ffn_w_down)))
    moments_v = dict(zip(WEIGHTS, (v_norm_gains, v_sb_w_qkv, v_sb_w_o, v_conv_w_in, v_conv_w, v_conv_w_out,
                                   v_gla_w_in, v_gla_w_gate_up, v_gla_b_gate, v_gla_head_norm, v_gla_w_o,
                                   v_ffn_w_up, v_ffn_w_down)))
    shard_shapes = {n: a.shape for n, a in shards.items()}

    alone = [('norm_gains', 0, 1), ('sb_w_qkv', 0, 1)]
    rest = [(n, 1 if (n, 0, 1) in alone else 0, shard_shapes[n][0]) for n in WEIGHTS]
    groups = [alone, [piece for piece in rest if piece[1] < piece[2]]]

    def payload(group):
        segs, rows = _segments(group, shard_shapes, True)
        flat = {(n, lo, hi): (lax.bitcast_convert_type(shards[n][lo:hi], BF16) if n in F32_PAYLOAD
                              else shards[n][lo:hi].astype(BF16)).reshape(-1) for n, lo, hi in group}
        return segs, _pack(flat, segs, rows, BF16)

    whole = {n: [None] * shard_shapes[n][0] for n in WEIGHTS}

    def take_gathered(segs, gathered):
        gathered = gathered.reshape(N_DEV, -1, PACK_COLS)
        for n, lo, hi in segs:
            piece = _unpack(gathered, segs[(n, lo, hi)])
            if n in F32_PAYLOAD:
                piece = lax.bitcast_convert_type(piece.reshape(N_DEV, -1, 2), F32)
            full = _unshard(piece.reshape((N_DEV, hi - lo) + shard_shapes[n][1:]), SHARD_AXIS[n])
            for j in range(lo, hi):
                whole[n][j] = full[j - lo]

    segs0, payload0 = payload(groups[0])
    segs1, payload1 = payload(groups[1])
    take_gathered(segs0, all_gather(payload0, name="weights_all_gather"))

    gsegs = [_segments(group, shard_shapes, False) for group in groups]
    received = [None, None]

    def parts(g, grads):
        segs, rows = gsegs[g]
        flat = {(n, lo, hi): _shard_split(jnp.stack([grads[(n, j)] for j in range(lo, hi)]), SHARD_AXIS[n])
                .reshape(N_DEV, -1) for n, lo, hi in segs}
        return _pack(flat, segs, rows, BF16)

    def on_received(buf):
        received[1] = buf

    comm = types.SimpleNamespace(rest_payload=payload1, on_gathered=functools.partial(take_gathered, segs1),
                                 rest_parts=functools.partial(parts, 1), on_received=on_received)

    loss, grad_x, grads = local_step(x[0], whole, loss_target[0], comm)
    loss = lax.psum(loss[0, 0], ("x", "y", "c"))

    results = {n: [[] for _ in range(4)] for n in WEIGHTS}
    outs_of = {}
    for g in (1, 0):
        segs, rows = gsegs[g]

        def packed(source):
            return _pack({(n, lo, hi): source[n][lo:hi].reshape(-1) for n, lo, hi in segs}, segs, rows, F32)

        states = (packed(shards), packed(moments_m), packed(moments_v))
        if g == 1:
            *outs_of[g], received[0] = adamw(received[1], *states, name="adamw1", exchange=parts(0, grads))
        else:
            outs_of[g] = adamw(received[0], *states, name="adamw0")
    for g in range(2):
        segs, _ = gsegs[g]
        for n, lo, hi in segs:
            for o, buf in enumerate(outs_of[g]):
                results[n][o].append(_unpack(buf, segs[(n, lo, hi)]).reshape((hi - lo,) + shard_shapes[n][1:]))
    whole_out = [[r[0] if len(r) == 1 else jnp.concatenate(r, axis=0) for r in (results[n][o] for n in WEIGHTS)]
                 for o in range(4)]
    return (loss, grad_x[None], *whole_out[0], *whole_out[1], *whole_out[2], *whole_out[3])
```
